```python
import math
import jax, jax.numpy as jnp
from jax import lax
import numpy as np

D_MODEL = 1024
BATCH = 16
SEQ = 2048
DEPTH = 1

CHUNK = 64
Q_BLOCK = 128

MLA_HEADS = 8
QK_NOPE = 64
QK_ROPE = 32
V_DIM = 64
Q_RANK = 256
KV_RANK = 128
ROPE_THETA = 10000.0
MLA_WIDTH = MLA_HEADS * V_DIM

SSM_GROUP = 16
SSM_STATE = 64
SSM_WIDTH = D_MODEL // 2
SSM_GROUPS = SSM_WIDTH // SSM_GROUP
DT_MIN = 1e-3
DT_MAX = 1e-1

MIX_WIDTH = MLA_WIDTH + SSM_WIDTH
IN_WIDTH = Q_RANK + KV_RANK + QK_ROPE + SSM_WIDTH

N_EXPERTS = 64
TOP_K = 8
N_GROUPS = 8
TOP_GROUPS = 4
EXPERT_FF = 256
ROUTED_SCALE = 2.5
MOE_BLOCK = 128

ALPHA = (2.0 * DEPTH) ** 0.25
BETA = (8.0 * DEPTH) ** -0.25
EPS = 1e-5

kernel_name = "hybrid_mla_s5_moe_deepnorm_block"


def layer_norm(x, g, b):
    xf = x.astype(jnp.float32)
    mu = jnp.mean(xf, axis=-1, keepdims=True)
    var = jnp.mean(jnp.square(xf - mu), axis=-1, keepdims=True)
    return ((xf - mu) * lax.rsqrt(var + EPS) * g.astype(jnp.float32) + b.astype(jnp.float32)).astype(x.dtype)


def rms_norm(x, g):
    xf = x.astype(jnp.float32)
    return (xf * lax.rsqrt(jnp.mean(xf * xf, axis=-1, keepdims=True) + EPS) * g.astype(jnp.float32)).astype(x.dtype)


def rope(x, positions):
    half = QK_ROPE // 2
    inv_freq = ROPE_THETA ** (-jnp.arange(half, dtype=jnp.float32) / half)
    ang = positions.astype(jnp.float32)[..., None] * inv_freq
    ang = ang.reshape(ang.shape[:2] + (1,) * (x.ndim - 3) + (half,))
    cos, sin = jnp.cos(ang), jnp.sin(ang)
    xf = x.astype(jnp.float32)
    x1, x2 = xf[..., :half], xf[..., half:]
    return jnp.concatenate([x1 * cos - x2 * sin, x2 * cos + x1 * sin], axis=-1).astype(x.dtype)


def mla(c_q, c_kv, k_rope_raw, positions, q_norm_g, kv_norm_g, w_uq, w_ukv):
    B, S, _ = c_q.shape
    q = jnp.einsum('bsr,rhd->bshd', rms_norm(c_q, q_norm_g), w_uq)
    q_nope, q_rope = q[..., :QK_NOPE], rope(q[..., QK_NOPE:], positions)
    kv = jnp.einsum('bsr,rhd->bshd', rms_norm(c_kv, kv_norm_g), w_ukv)
    k_nope, v = kv[..., :QK_NOPE], kv[..., QK_NOPE:]
    k_rope = rope(k_rope_raw, positions)
    scale = (QK_NOPE + QK_ROPE) ** -0.5
    chunk_id = jnp.arange(S) // CHUNK
    outs = []
    for i in range(S // Q_BLOCK):
        q0, q1 = i * Q_BLOCK, (i + 1) * Q_BLOCK
        s = (jnp.einsum('bqhn,bkhn->bhqk', q_nope[:, q0:q1], k_nope[:, :q1]).astype(jnp.float32)
             + jnp.einsum('bqhr,bkr->bhqk', q_rope[:, q0:q1], k_rope[:, :q1]).astype(jnp.float32)) * scale
        mask = chunk_id[q0:q1, None] >= chunk_id[None, :q1]
        s = jnp.where(mask, s, -jnp.inf)
        p = jax.nn.softmax(s, axis=-1).astype(v.dtype)
        outs.append(jnp.einsum('bhqk,bkhd->bqhd', p, v[:, :q1]))
    return jnp.concatenate(outs, axis=1).reshape(B, S, MLA_WIDTH)


def s5(u, lambda_re, lambda_im, log_step, b_re, b_im, c_re, c_im, d_skip, w_glu, b_glu):
    B, S, _ = u.shape
    uf = u.astype(jnp.float32).reshape(B, S, SSM_GROUPS, SSM_GROUP)
    lam = lax.complex(jnp.minimum(lambda_re.astype(jnp.float32), -1e-4),
                      lambda_im.astype(jnp.float32))
    step = jnp.exp(log_step.astype(jnp.float32))[:, None]
    lam_bar = jnp.exp(lam * step)
    b_mat = lax.complex(b_re.astype(jnp.float32), b_im.astype(jnp.float32))
    b_bar = ((lam_bar - 1.0) / lam)[..., None] * b_mat
    bu = jnp.einsum('gpc,bsgc->sbgp', b_bar, uf.astype(jnp.complex64))
    a = jnp.broadcast_to(lam_bar, (S, 1) + lam_bar.shape)

    def combine(left, right):
        a_l, b_l = left
        a_r, b_r = right
        return a_r * a_l, a_r * b_l + b_r

    _, states = lax.associative_scan(combine, (a, bu), axis=0)
    c_mat = lax.complex(c_re.astype(jnp.float32), c_im.astype(jnp.float32))
    y = jnp.einsum('gcp,sbgp->bsgc', c_mat, states).real + d_skip.astype(jnp.float32) * uf
    y = jax.nn.gelu(y.reshape(B, S, SSM_WIDTH))
    return y * jax.nn.sigmoid(y @ w_glu.astype(jnp.float32) + b_glu.astype(jnp.float32))


def moe(x, w_router, router_bias, w_gate, w_up, w_down, ws_gate, ws_up, ws_down):
    B, S, D = x.shape
    xt = x.reshape(-1, D)
    T = xt.shape[0]
    scores = jax.nn.sigmoid((xt @ w_router).astype(jnp.float32))
    sel = scores + router_bias.astype(jnp.float32)
    grp_score = lax.top_k(sel.reshape(T, N_GROUPS, N_EXPERTS // N_GROUPS), 2)[0].sum(-1)
    _, top_g = lax.top_k(grp_score, TOP_GROUPS)
    gmask = jax.nn.one_hot(top_g, N_GROUPS, dtype=jnp.float32).sum(1)
    emask = jnp.repeat(gmask, N_EXPERTS // N_GROUPS, axis=1) > 0
    _, idx = lax.top_k(jnp.where(emask, sel, -jnp.inf), TOP_K)
    w = jnp.take_along_axis(scores, idx, axis=1)
    w = w / jnp.sum(w, axis=-1, keepdims=True) * ROUTED_SCALE
    gates = jnp.einsum('tk,tke->te', w, jax.nn.one_hot(idx, N_EXPERTS, dtype=jnp.float32))

    def expert_block(args):
        xb, gb = args
        h = jax.nn.silu(jnp.einsum('td,edf->tef', xb, w_gate)) * jnp.einsum('td,edf->tef', xb, w_up)
        return jnp.einsum('tef,efd->td', h * gb[..., None].astype(h.dtype), w_down)

    routed = lax.map(expert_block, (xt.reshape(-1, MOE_BLOCK, D),
                                    gates.reshape(-1, MOE_BLOCK, N_EXPERTS)))
    shared = (jax.nn.silu(xt @ ws_gate) * (xt @ ws_up)) @ ws_down
    return (routed.reshape(T, D) + shared).reshape(B, S, D)


def setup_inputs(seed: int = 0) -> dict:
    key = jax.random.key(seed)
    ks = iter(jax.random.split(key, 40))
    nrm = lambda shape, s: jax.random.normal(next(ks), shape, jnp.float32) * s
    L = DEPTH
    x = nrm((BATCH, SEQ, D_MODEL), 1.0)
    offset = jax.random.randint(next(ks), (BATCH, 1), 0, 4096, dtype=jnp.int32)
    positions = (offset + jnp.arange(SEQ, dtype=jnp.int32)[None, :]).astype(jnp.int32)
    ln_in_g = 1.0 + nrm((D_MODEL,), 0.01)
    ln_in_b = nrm((D_MODEL,), 0.01)
    w_in = nrm((L, D_MODEL, IN_WIDTH), D_MODEL ** -0.5)
    q_norm_g = 1.0 + nrm((L, Q_RANK), 0.01)
    kv_norm_g = 1.0 + nrm((L, KV_RANK), 0.01)
    w_uq = nrm((L, Q_RANK, MLA_HEADS, QK_NOPE + QK_ROPE), Q_RANK ** -0.5)
    w_ukv = nrm((L, KV_RANK, MLA_HEADS, QK_NOPE + V_DIM), KV_RANK ** -0.5)
    w_ukv = w_ukv.at[..., QK_NOPE:].multiply(BETA)
    lambda_re = -0.5 + nrm((L, SSM_GROUPS, SSM_STATE), 0.01)
    lambda_im = math.pi * jnp.arange(SSM_STATE, dtype=jnp.float32) + nrm((L, SSM_GROUPS, SSM_STATE), 0.01)
    log_step = jax.random.uniform(next(ks), (L, SSM_GROUPS), jnp.float32,
                                  math.log(DT_MIN), math.log(DT_MAX))
    b_scale = (2.0 * SSM_GROUP) ** -0.5
    b_re = nrm((L, SSM_GROUPS, SSM_STATE, SSM_GROUP), b_scale)
    b_im = nrm((L, SSM_GROUPS, SSM_STATE, SSM_GROUP), b_scale)
    c_scale = (2.0 * SSM_STATE) ** -0.5
    c_re = nrm((L, SSM_GROUPS, SSM_GROUP, SSM_STATE), c_scale)
    c_im = nrm((L, SSM_GROUPS, SSM_GROUP, SSM_STATE), c_scale)
    d_skip = nrm((L, SSM_GROUPS, SSM_GROUP), 1.0)
    w_glu = nrm((L, SSM_WIDTH, SSM_WIDTH), SSM_WIDTH ** -0.5)
    b_glu = nrm((L, SSM_WIDTH), 0.01)
    attn_out_g = 1.0 + nrm((L, MLA_WIDTH), 0.01)
    ssm_out_g = 1.0 + nrm((L, SSM_WIDTH), 0.01)
    w_o = nrm((L, MIX_WIDTH, D_MODEL), MIX_WIDTH ** -0.5 * BETA)
    ln1_g = 1.0 + nrm((L, D_MODEL), 0.01)
    ln1_b = nrm((L, D_MODEL), 0.01)
    w_router = nrm((L, D_MODEL, N_EXPERTS), D_MODEL ** -0.5)
    router_bias = nrm((L, N_EXPERTS), 0.01)
    w_gate = nrm((L, N_EXPERTS, D_MODEL, EXPERT_FF), D_MODEL ** -0.5)
    w_up = nrm((L, N_EXPERTS, D_MODEL, EXPERT_FF), D_MODEL ** -0.5)
    w_down = nrm((L, N_EXPERTS, EXPERT_FF, D_MODEL), EXPERT_FF ** -0.5 * BETA)
    ws_gate = nrm((L, D_MODEL, EXPERT_FF), D_MODEL ** -0.5)
    ws_up = nrm((L, D_MODEL, EXPERT_FF), D_MODEL ** -0.5)
    ws_down = nrm((L, EXPERT_FF, D_MODEL), EXPERT_FF ** -0.5 * BETA)
    ln2_g = 1.0 + nrm((L, D_MODEL), 0.01)
    ln2_b = nrm((L, D_MODEL), 0.01)
    return {"x": x, "positions": positions, "ln_in_g": ln_in_g, "ln_in_b": ln_in_b,
            "w_in": w_in, "q_norm_g": q_norm_g, "kv_norm_g": kv_norm_g, "w_uq": w_uq, "w_ukv": w_ukv,
            "lambda_re": lambda_re, "lambda_im": lambda_im, "log_step": log_step,
            "b_re": b_re, "b_im": b_im, "c_re": c_re, "c_im": c_im, "d_skip": d_skip,
            "w_glu": w_glu, "b_glu": b_glu, "attn_out_g": attn_out_g, "ssm_out_g": ssm_out_g,
            "w_o": w_o, "ln1_g": ln1_g, "ln1_b": ln1_b, "w_router": w_router, "router_bias": router_bias,
            "w_gate": w_gate, "w_up": w_up, "w_down": w_down,
            "ws_gate": ws_gate, "ws_up": ws_up, "ws_down": ws_down, "ln2_g": ln2_g, "ln2_b": ln2_b}


def reference(x, positions, ln_in_g, ln_in_b, w_in, q_norm_g, kv_norm_g, w_uq, w_ukv,
              lambda_re, lambda_im, log_step, b_re, b_im, c_re, c_im, d_skip, w_glu, b_glu,
              attn_out_g, ssm_out_g, w_o, ln1_g, ln1_b, w_router, router_bias,
              w_gate, w_up, w_down, ws_gate, ws_up, ws_down, ln2_g, ln2_b):
    x = layer_norm(x, ln_in_g, ln_in_b)
    s1 = Q_RANK
    s2 = s1 + KV_RANK
    s3 = s2 + QK_ROPE
    for l in range(DEPTH):
        h = x @ w_in[l]
        c_q, c_kv, k_rope_raw, u = h[..., :s1], h[..., s1:s2], h[..., s2:s3], h[..., s3:]
        att = mla(c_q, c_kv, k_rope_raw, positions, q_norm_g[l], kv_norm_g[l], w_uq[l], w_ukv[l])
        ssm = s5(u, lambda_re[l], lambda_im[l], log_step[l], b_re[l], b_im[l], c_re[l], c_im[l],
                 d_skip[l], w_glu[l], b_glu[l]).astype(att.dtype)
        mix = jnp.concatenate([rms_norm(att, attn_out_g[l]), rms_norm(ssm, ssm_out_g[l])], axis=-1) @ w_o[l]
        x = layer_norm(ALPHA * x + mix, ln1_g[l], ln1_b[l])
        ffn = moe(x, w_router[l], router_bias[l], w_gate[l], w_up[l], w_down[l],
                  ws_gate[l], ws_up[l], ws_down[l])
        x = layer_norm(ALPHA * x + ffn, ln2_g[l], ln2_b[l])
    return x
```

```python
import functools

import jax
import jax.numpy as jnp
from jax import lax
from jax.experimental import pallas as pl
from jax.experimental.pallas import tpu as pltpu

CHUNK = 64
MLA_HEADS = 8
QK_NOPE = 64
QK_ROPE = 32
V_DIM = 64
Q_RANK = 256
KV_RANK = 128
ROPE_THETA = 10000.0
SSM_GROUP = 16
SSM_STATE = 64
N_EXPERTS = 64
TOP_K = 8
N_GROUPS = 8
TOP_GROUPS = 4
ROUTED_SCALE = 2.5
DEPTH = 1
ALPHA = (2.0 * DEPTH) ** 0.25
EPS = 1e-5

LANES = 128
HEAD_PAD = LANES
VMEM_LIMIT = 56 * 1024 * 1024

BF16 = jnp.bfloat16
F32 = jnp.float32
NT_DIMS = (((1,), (1,)), ((), ()))


def _dot(a, b):
    return jnp.dot(a, b, preferred_element_type=F32)


def _layer_norm(x, g, b):
    mu = jnp.mean(x, axis=-1, keepdims=True)
    xc = x - mu
    var = jnp.mean(xc * xc, axis=-1, keepdims=True)
    return xc * lax.rsqrt(var + EPS) * g + b


def _rms_norm(x, g):
    return x * lax.rsqrt(jnp.mean(x * x, axis=-1, keepdims=True) + EPS) * g


def _inproj_kernel(x_ref, pos_ref, lng_ref, lnb_ref, w1_ref, qg_ref, kvg_ref,
                   wq_ref, wqr_ref, wk_ref, wv_ref, freq_ref,
                   xn_ref, q_ref, k_ref, v_ref, u_ref, *, ssm_width):
    xn = _layer_norm(x_ref[...], lng_ref[...], lnb_ref[...])
    xn_ref[...] = xn
    h = _dot(xn.astype(BF16), w1_ref[...])
    o1 = Q_RANK
    o2 = o1 + KV_RANK
    o3 = o2 + ssm_width
    o4 = o3 + HEAD_PAD
    cq = h[:, :o1]
    ckv = h[:, o1:o2]
    u_ref[...] = h[:, o2:o3]
    kr_raw = h[:, o3:o4]
    kr_rot = h[:, o4:o4 + HEAD_PAD]
    cqn = _rms_norm(cq, qg_ref[...]).astype(BF16)
    ckvn = _rms_norm(ckv, kvg_ref[...]).astype(BF16)

    ang = pos_ref[...] * freq_ref[...]
    c = jnp.cos(ang)
    s = jnp.sin(ang)
    lane = lax.broadcasted_iota(jnp.int32, (1, HEAD_PAD), 1)
    scale = (QK_NOPE + QK_ROPE) ** -0.5
    is_rope = (lane >= QK_NOPE) & (lane < QK_NOPE + QK_ROPE)
    cos1 = jnp.where(lane < QK_NOPE, 1.0, jnp.where(is_rope, c, 0.0)) * scale
    sin1 = jnp.where(is_rope, s, 0.0) * scale
    cos_t = jnp.concatenate([cos1] * MLA_HEADS, axis=1)
    sin_t = jnp.concatenate([sin1] * MLA_HEADS, axis=1)
    q = _dot(cqn, wq_ref[...]) * cos_t + _dot(cqn, wqr_ref[...]) * sin_t
    q_ref[...] = q.astype(BF16)

    kr = kr_raw * c + kr_rot * s
    k = _dot(ckvn, wk_ref[...]) + jnp.concatenate([kr] * MLA_HEADS, axis=1)
    k_ref[...] = k.astype(BF16)
    v_ref[...] = _dot(ckvn, wv_ref[...]).astype(BF16)


def _attn_kernel(q_ref, k_ref, v_ref, o_ref, *, tq):
    qi = pl.program_id(2)
    row_chunk = lax.broadcasted_iota(jnp.int32, (tq, tq), 0) // CHUNK
    col_chunk = lax.broadcasted_iota(jnp.int32, (tq, tq), 1) // CHUNK
    diag_mask = row_chunk >= col_chunk
    outs = []
    for hh in range(2):
        qh = q_ref[:, hh * HEAD_PAD:(hh + 1) * HEAD_PAD]

        def block(j, carry, masked, hh=hh, qh=qh):
            m, l, acc = carry
            start = pl.multiple_of(j * tq, tq)
            kb = k_ref[pl.ds(start, tq), hh * HEAD_PAD:(hh + 1) * HEAD_PAD]
            vb = v_ref[pl.ds(start, tq), hh * V_DIM:(hh + 1) * V_DIM]
            s = lax.dot_general(qh, kb, NT_DIMS, preferred_element_type=F32)
            if masked:
                s = jnp.where(diag_mask, s, -jnp.inf)
            m_new = jnp.maximum(m, jnp.max(s, axis=-1, keepdims=True))
            a = jnp.exp(m - m_new)
            p = jnp.exp(s - m_new)
            l = a * l + jnp.sum(p, axis=-1, keepdims=True)
            acc = a * acc + _dot(p.astype(BF16), vb)
            return m_new, l, acc

        init = (jnp.full((tq, 1), -jnp.inf, F32), jnp.zeros((tq, 1), F32),
                jnp.zeros((tq, V_DIM), F32))
        carry = lax.fori_loop(0, qi, functools.partial(block, masked=False), init)
        m, l, acc = block(qi, carry, True)
        outs.append(acc / l)
    o_ref[...] = jnp.concatenate(outs, axis=1).astype(o_ref.dtype)


def _s5_kernel(u_ref, win_ref, are_ref, aim_ref, cre_ref, cim_ref, dskip_ref,
               wglu_ref, bglu_ref, o_ref, vx_ref, hre_ref, him_ref, *,
               batch, lt, n_state, scan_tiles):
    ti = pl.program_id(0)

    @pl.when(ti == 0)
    def _():
        hre_ref[...] = jnp.zeros_like(hre_ref)
        him_ref[...] = jnp.zeros_like(him_ref)

    width = u_ref.shape[-1]
    n_slab = width // LANES
    n_tiles = n_state // LANES
    slab_tiles = n_tiles // n_slab
    u2 = u_ref[...].reshape(batch * lt, width)
    ub = u2.astype(BF16)
    for j in range(n_slab):
        vj = _dot(ub[:, j * LANES:(j + 1) * LANES], win_ref[j])
        for i in range(slab_tiles):
            vx_ref[j * slab_tiles + i] = vj[:, i * LANES:(i + 1) * LANES]
            vx_ref[n_tiles + j * slab_tiles + i] = vj[:, (slab_tiles + i) * LANES:(slab_tiles + i + 1) * LANES]

    for c0 in range(0, n_tiles, scan_tiles):
        tiles = range(c0, c0 + scan_tiles)
        ar = [jnp.broadcast_to(are_ref[:, c * LANES:(c + 1) * LANES], (batch, LANES)) for c in tiles]
        ai = [jnp.broadcast_to(aim_ref[:, c * LANES:(c + 1) * LANES], (batch, LANES)) for c in tiles]

        def step(t, carry, tiles=tiles, ar=ar, ai=ai):
            rows = pl.ds(t, batch, stride=lt)
            out = []
            for n, c in enumerate(tiles):
                hr, hi = carry[2 * n], carry[2 * n + 1]
                nr = ar[n] * hr - ai[n] * hi + vx_ref[c, rows, :]
                ni = ar[n] * hi + ai[n] * hr + vx_ref[n_tiles + c, rows, :]
                vx_ref[c, rows, :] = nr
                vx_ref[n_tiles + c, rows, :] = ni
                out += [nr, ni]
            return tuple(out)

        init = []
        for c in tiles:
            init += [hre_ref[c], him_ref[c]]
        fin = lax.fori_loop(0, lt, step, tuple(init), unroll=4)
        for n, c in enumerate(tiles):
            hre_ref[c] = fin[2 * n]
            him_ref[c] = fin[2 * n + 1]

    ys = []
    for j in range(n_slab):
        xr = jnp.concatenate([vx_ref[j * slab_tiles + i].astype(BF16) for i in range(slab_tiles)], axis=1)
        xi = jnp.concatenate([vx_ref[n_tiles + j * slab_tiles + i].astype(BF16) for i in range(slab_tiles)], axis=1)
        ys.append(_dot(xr, cre_ref[j]) + _dot(xi, cim_ref[j]))
    y = jnp.concatenate(ys, axis=1) + dskip_ref[...] * u2
    y = jax.nn.gelu(y)
    z = _dot(y.astype(BF16), wglu_ref[...]) + bglu_ref[...]
    out = y * jax.nn.sigmoid(z)
    o_ref[...] = out.reshape(batch, lt, width).astype(o_ref.dtype)


def _router_gates(logits_t, rbias):
    n_exp, tm = logits_t.shape
    per_group = n_exp // N_GROUPS
    scores = jax.nn.sigmoid(logits_t)
    sel = scores + rbias
    neg_inf = -jnp.inf
    sub_iota = lax.broadcasted_iota(jnp.int32, (per_group, tm), 0).astype(F32)
    group_score = []
    for g in range(N_GROUPS):
        sg = sel[g * per_group:(g + 1) * per_group, :]
        m1 = jnp.max(sg, axis=0, keepdims=True)
        first = jnp.min(jnp.where(sg == m1, sub_iota, float(per_group)), axis=0, keepdims=True)
        m2 = jnp.max(jnp.where(sub_iota == first, neg_inf, sg), axis=0, keepdims=True)
        group_score.append(m1 + m2)
    masked = []
    for g in range(N_GROUPS):
        rank = jnp.zeros((1, tm), F32)
        for g2 in range(N_GROUPS):
            if g2 == g:
                continue
            ahead = (group_score[g2] >= group_score[g]) if g2 < g else (group_score[g2] > group_score[g])
            rank = rank + jnp.where(ahead, 1.0, 0.0)
        keep = rank < float(TOP_GROUPS)
        masked.append(jnp.where(keep, sel[g * per_group:(g + 1) * per_group, :], neg_inf))
    cur = jnp.concatenate(masked, axis=0)
    iota = lax.broadcasted_iota(jnp.int32, (n_exp, tm), 0).astype(F32)
    chosen = jnp.zeros((n_exp, tm), F32)
    for _ in range(TOP_K):
        m = jnp.max(cur, axis=0, keepdims=True)
        idx = jnp.min(jnp.where(cur == m, iota, float(n_exp)), axis=0, keepdims=True)
        pick = iota == idx
        chosen = jnp.where(pick, 1.0, chosen)
        cur = jnp.where(pick, neg_inf, cur)
    w = scores * chosen
    return w / jnp.sum(w, axis=0, keepdims=True) * ROUTED_SCALE


def _mix_kernel(att_ref, ssm_ref, xn_ref, ag_ref, sg_ref, woa_ref, wos_ref,
                g_ref, b_ref, wrh_ref, wrl_ref, rb_ref,
                x1_ref, x1b_ref, gates_ref):
    an = _rms_norm(att_ref[...].astype(F32), ag_ref[...]).astype(BF16)
    sn = _rms_norm(ssm_ref[...].astype(F32), sg_ref[...]).astype(BF16)
    mix = _dot(an, woa_ref[...]) + _dot(sn, wos_ref[...])
    x1 = _layer_norm(ALPHA * xn_ref[...] + mix, g_ref[...], b_ref[...])
    x1_ref[...] = x1
    x_hi = x1.astype(BF16)
    x1b_ref[...] = x_hi
    x_lo = (x1 - x_hi.astype(F32)).astype(BF16)
    dg = functools.partial(lax.dot_general, dimension_numbers=NT_DIMS, preferred_element_type=F32)
    logits_t = dg(wrh_ref[...], x_hi) + dg(wrl_ref[...], x_hi) + dg(wrh_ref[...], x_lo)
    gates_ref[...] = _router_gates(logits_t, rb_ref[...])


def _moe_kernel(xb_ref, x1_ref, gates_ref, wg_ref, wu_ref, wd_ref, g_ref, b_ref,
                o_ref, acc_ref):
    e = pl.program_id(1)

    @pl.when(e == 0)
    def _():
        acc_ref[...] = jnp.zeros_like(acc_ref)

    xb = xb_ref[...]
    lane = lax.broadcasted_iota(jnp.int32, gates_ref.shape, 1)
    gcol = jnp.sum(jnp.where(lane == e, gates_ref[...], 0.0), axis=1, keepdims=True)
    hg = _dot(xb, wg_ref[0])
    hu = _dot(xb, wu_ref[0])
    h = jax.nn.silu(hg) * hu * gcol
    acc_ref[...] += _dot(h.astype(BF16), wd_ref[0])

    @pl.when(e == pl.num_programs(1) - 1)
    def _():
        o_ref[...] = _layer_norm(ALPHA * x1_ref[...] + acc_ref[...], g_ref[...], b_ref[...])


def _row(v):
    return v.reshape(1, -1).astype(F32)


def _const_spec(shape):
    nd = len(shape)
    return pl.BlockSpec(shape, lambda *_: (0,) * nd)


def _pad_heads(w, width):
    r, h, _ = w.shape
    return jnp.pad(w, ((0, 0), (0, 0), (0, HEAD_PAD - width))).reshape(r, h * HEAD_PAD)


def _half_rotate(w):
    half = QK_ROPE // 2
    return jnp.concatenate([-w[..., half:], w[..., :half]], axis=-1)


def kernel(x, positions, ln_in_g, ln_in_b, w_in, q_norm_g, kv_norm_g, w_uq, w_ukv, lambda_re, lambda_im, log_step, b_re, b_im, c_re, c_im, d_skip, w_glu, b_glu, attn_out_g, ssm_out_g, w_o, ln1_g, ln1_b, w_router, router_bias, w_gate, w_up, w_down, ws_gate, ws_up, ws_down, ln2_g, ln2_b):
    B, S, D = x.shape
    T = B * S
    l = 0
    ssm_width = w_glu.shape[-1]
    n_groups = ssm_width // SSM_GROUP
    n_state = n_groups * SSM_STATE
    mla_width = MLA_HEADS * V_DIM
    qk_pad = MLA_HEADS * HEAD_PAD
    cparams = functools.partial(pltpu.CompilerParams, vmem_limit_bytes=VMEM_LIMIT)

    s1, s2, s3 = Q_RANK, Q_RANK + KV_RANK, Q_RANK + KV_RANK + QK_ROPE
    wi = w_in[l]
    w_kr = wi[:, s2:s3]
    pad_rope = lambda w: jnp.pad(w, ((0, 0), (QK_NOPE, HEAD_PAD - QK_NOPE - QK_ROPE)))
    w1 = jnp.concatenate([wi[:, :s2], wi[:, s3:], pad_rope(w_kr), pad_rope(_half_rotate(w_kr))],
                         axis=1).astype(BF16)
    wq = w_uq[l]
    zeros_nope = jnp.zeros(wq.shape[:2] + (QK_NOPE,), wq.dtype)
    wq_main = _pad_heads(wq, QK_NOPE + QK_ROPE).astype(BF16)
    wq_rot = _pad_heads(jnp.concatenate([zeros_nope, _half_rotate(wq[..., QK_NOPE:])], axis=-1),
                        QK_NOPE + QK_ROPE).astype(BF16)
    wkv = w_ukv[l]
    wk = _pad_heads(wkv[..., :QK_NOPE], QK_NOPE).astype(BF16)
    wv = wkv[..., QK_NOPE:].reshape(KV_RANK, mla_width).astype(BF16)
    half = QK_ROPE // 2
    inv_freq = ROPE_THETA ** (-jnp.arange(half, dtype=F32) / half)
    freq = jnp.pad(jnp.concatenate([inv_freq, inv_freq]),
                   (QK_NOPE, HEAD_PAD - QK_NOPE - QK_ROPE)).reshape(1, HEAD_PAD)
    pos_f = positions.astype(F32).reshape(T, 1)

    tm = min(512, T)
    w1_cols = w1.shape[1]
    tok = lambda width: pl.BlockSpec((tm, width), lambda i: (i, 0))
    xn, q, k, v, u = pl.pallas_call(
        functools.partial(_inproj_kernel, ssm_width=ssm_width),
        grid=(T // tm,),
        in_specs=[tok(D), tok(1), _const_spec((1, D)), _const_spec((1, D)),
                  _const_spec((D, w1_cols)), _const_spec((1, Q_RANK)), _const_spec((1, KV_RANK)),
                  _const_spec((Q_RANK, qk_pad)), _const_spec((Q_RANK, qk_pad)),
                  _const_spec((KV_RANK, qk_pad)), _const_spec((KV_RANK, mla_width)),
                  _const_spec((1, HEAD_PAD))],
        out_specs=[tok(D), tok(qk_pad), tok(qk_pad), tok(mla_width), tok(ssm_width)],
        out_shape=[jax.ShapeDtypeStruct((T, D), F32), jax.ShapeDtypeStruct((T, qk_pad), BF16),
                   jax.ShapeDtypeStruct((T, qk_pad), BF16), jax.ShapeDtypeStruct((T, mla_width), BF16),
                   jax.ShapeDtypeStruct((T, ssm_width), F32)],
        compiler_params=cparams(dimension_semantics=("parallel",)),
        name="inproj",
    )(x.reshape(T, D), pos_f, _row(ln_in_g), _row(ln_in_b), w1, _row(q_norm_g[l]), _row(kv_norm_g[l]),
      wq_main, wq_rot, wk, wv, freq)

    tq = min(256, S)
    nq = S // tq
    att = pl.pallas_call(
        functools.partial(_attn_kernel, tq=tq),
        grid=(B, MLA_HEADS // 2, nq),
        in_specs=[pl.BlockSpec((tq, 2 * HEAD_PAD), lambda b, h, i: (b * nq + i, h)),
                  pl.BlockSpec((S, 2 * HEAD_PAD), lambda b, h, i: (b, h)),
                  pl.BlockSpec((S, 2 * V_DIM), lambda b, h, i: (b, h))],
        out_specs=pl.BlockSpec((tq, 2 * V_DIM), lambda b, h, i: (b * nq + i, h)),
        out_shape=jax.ShapeDtypeStruct((T, mla_width), BF16),
        compiler_params=cparams(dimension_semantics=("parallel", "parallel", "arbitrary")),
        name="attention",
    )(q, k, v)

    lam = lax.complex(jnp.minimum(lambda_re[l].astype(F32), -1e-4), lambda_im[l].astype(F32))
    step = jnp.exp(log_step[l].astype(F32))[:, None]
    lam_bar = jnp.exp(lam * step)
    b_bar = ((lam_bar - 1.0) / lam)[..., None] * lax.complex(b_re[l].astype(F32), b_im[l].astype(F32))
    n_slab = ssm_width // LANES
    g_per_slab = n_groups // n_slab
    eye = jnp.eye(n_groups, dtype=F32)

    def expand_in(bpart):
        return jnp.einsum('gpc,gh->gchp', bpart, eye).reshape(ssm_width, n_state)

    def expand_out(cpart):
        return jnp.einsum('gcp,gh->gphc', cpart, eye).reshape(n_state, ssm_width)

    slab = n_state // n_slab
    win_re, win_im = expand_in(jnp.real(b_bar)), expand_in(jnp.imag(b_bar))
    win = jnp.stack([jnp.concatenate([win_re[j * LANES:(j + 1) * LANES, j * slab:(j + 1) * slab],
                                      win_im[j * LANES:(j + 1) * LANES, j * slab:(j + 1) * slab]], axis=1)
                     for j in range(n_slab)]).astype(BF16)
    wc_re, wc_im = expand_out(c_re[l].astype(F32)), expand_out(-c_im[l].astype(F32))
    cre = jnp.stack([wc_re[j * slab:(j + 1) * slab, j * LANES:(j + 1) * LANES] for j in range(n_slab)]).astype(BF16)
    cim = jnp.stack([wc_im[j * slab:(j + 1) * slab, j * LANES:(j + 1) * LANES] for j in range(n_slab)]).astype(BF16)
    a_re = jnp.real(lam_bar).reshape(1, n_state)
    a_im = jnp.imag(lam_bar).reshape(1, n_state)

    lt = min(64, S)
    ssm = pl.pallas_call(
        functools.partial(_s5_kernel, batch=B, lt=lt, n_state=n_state, scan_tiles=4),
        grid=(S // lt,),
        in_specs=[pl.BlockSpec((B, lt, ssm_width), lambda t: (0, t, 0)),
                  _const_spec(win.shape), _const_spec((1, n_state)), _const_spec((1, n_state)),
                  _const_spec(cre.shape), _const_spec(cim.shape), _const_spec((1, ssm_width)),
                  _const_spec((ssm_width, ssm_width)), _const_spec((1, ssm_width))],
        out_specs=pl.BlockSpec((B, lt, ssm_width), lambda t: (0, t, 0)),
        out_shape=jax.ShapeDtypeStruct((B, S, ssm_width), BF16),
        scratch_shapes=[pltpu.VMEM((2 * n_state // LANES, B * lt, LANES), F32),
                        pltpu.VMEM((n_state // LANES, B, LANES), F32),
                        pltpu.VMEM((n_state // LANES, B, LANES), F32)],
        compiler_params=cparams(dimension_semantics=("arbitrary",)),
        name="s5",
    )(u.reshape(B, S, ssm_width), win, a_re, a_im, cre, cim, _row(d_skip[l]),
      w_glu[l].astype(BF16), _row(b_glu[l]))

    wo = w_o[l].astype(BF16)
    wr_t = w_router[l].T.astype(F32)
    wr_hi = wr_t.astype(BF16)
    wr_lo = (wr_t - wr_hi.astype(F32)).astype(BF16)
    x1, x1b, gates_t = pl.pallas_call(
        _mix_kernel,
        grid=(T // tm,),
        in_specs=[tok(mla_width), tok(ssm_width), tok(D), _const_spec((1, mla_width)),
                  _const_spec((1, ssm_width)), _const_spec((mla_width, D)), _const_spec((ssm_width, D)),
                  _const_spec((1, D)), _const_spec((1, D)), _const_spec((N_EXPERTS, D)),
                  _const_spec((N_EXPERTS, D)), _const_spec((N_EXPERTS, 1))],
        out_specs=[tok(D), tok(D), pl.BlockSpec((N_EXPERTS, tm), lambda i: (0, i))],
        out_shape=[jax.ShapeDtypeStruct((T, D), F32), jax.ShapeDtypeStruct((T, D), BF16),
                   jax.ShapeDtypeStruct((N_EXPERTS, T), F32)],
        compiler_params=cparams(dimension_semantics=("parallel",)),
        name="mix_router",
    )(att, ssm.reshape(T, ssm_width), xn, _row(attn_out_g[l]), _row(ssm_out_g[l]),
      wo[:mla_width], wo[mla_width:], _row(ln1_g[l]), _row(ln1_b[l]), wr_hi, wr_lo,
      router_bias[l].astype(F32).reshape(N_EXPERTS, 1))

    n_all = N_EXPERTS + 1
    gates = jnp.concatenate([gates_t.T, jnp.ones((T, 1), F32),
                             jnp.zeros((T, LANES - n_all), F32)], axis=1)
    wg_all = jnp.concatenate([w_gate[l], ws_gate[l][None]], axis=0).astype(BF16)
    wu_all = jnp.concatenate([w_up[l], ws_up[l][None]], axis=0).astype(BF16)
    wd_all = jnp.concatenate([w_down[l], ws_down[l][None]], axis=0).astype(BF16)
    ff = wg_all.shape[-1]
    tme = min(1024, T)
    out = pl.pallas_call(
        _moe_kernel,
        grid=(T // tme, n_all),
        in_specs=[pl.BlockSpec((tme, D), lambda i, e: (i, 0)),
                  pl.BlockSpec((tme, D), lambda i, e: (i, 0)),
                  pl.BlockSpec((tme, LANES), lambda i, e: (i, 0)),
                  pl.BlockSpec((1, D, ff), lambda i, e: (e, 0, 0)),
                  pl.BlockSpec((1, D, ff), lambda i, e: (e, 0, 0)),
                  pl.BlockSpec((1, ff, D), lambda i, e: (e, 0, 0)),
                  pl.BlockSpec((1, D), lambda i, e: (0, 0)),
                  pl.BlockSpec((1, D), lambda i, e: (0, 0))],
        out_specs=pl.BlockSpec((tme, D), lambda i, e: (i, 0)),
        out_shape=jax.ShapeDtypeStruct((T, D), F32),
        scratch_shapes=[pltpu.VMEM((tme, D), F32)],
        compiler_params=cparams(dimension_semantics=("parallel", "arbitrary")),
        name="experts",
    )(x1b, x1, gates, wg_all, wu_all, wd_all, _row(ln2_g[l]), _row(ln2_b[l]))
    return out.reshape(B, S, D)
```

```python
import functools

import jax
import jax.numpy as jnp
from jax import lax
from jax.experimental import pallas as pl
from jax.experimental.pallas import tpu as pltpu

CHUNK = 64
MLA_HEADS = 8
QK_NOPE = 64
QK_ROPE = 32
V_DIM = 64
Q_RANK = 256
KV_RANK = 128
ROPE_THETA = 10000.0
SSM_GROUP = 16
SSM_STATE = 64
N_EXPERTS = 64
TOP_K = 8
N_GROUPS = 8
TOP_GROUPS = 4
ROUTED_SCALE = 2.5
DEPTH = 1
ALPHA = (2.0 * DEPTH) ** 0.25
EPS = 1e-5
LOG2_E = 1.4426950408889634

LANES = 128
HEAD_PAD = LANES
VMEM_LIMIT = 56 * 1024 * 1024

BF16 = jnp.bfloat16
F32 = jnp.float32
NT_DIMS = (((1,), (1,)), ((), ()))


def _dot(a, b):
    return jnp.dot(a, b, preferred_element_type=F32)


def _layer_norm(x, g, b):
    mu = jnp.mean(x, axis=-1, keepdims=True)
    xc = x - mu
    var = jnp.mean(xc * xc, axis=-1, keepdims=True)
    return xc * lax.rsqrt(var + EPS) * g + b


def _rms_norm(x, g):
    return x * lax.rsqrt(jnp.mean(x * x, axis=-1, keepdims=True) + EPS) * g


def _inproj_kernel(x_ref, pos_ref, lng_ref, lnb_ref, w1_ref, qg_ref, kvg_ref,
                   wq_ref, wqr_ref, wk_ref, wv_ref, freq_ref,
                   xn_ref, q_ref, k_ref, v_ref, u_ref, *, ssm_width):
    xn = _layer_norm(x_ref[...], lng_ref[...], lnb_ref[...])
    xn_ref[...] = xn
    h = _dot(xn.astype(BF16), w1_ref[...])
    o1 = Q_RANK
    o2 = o1 + KV_RANK
    o3 = o2 + ssm_width
    o4 = o3 + HEAD_PAD
    cq = h[:, :o1]
    ckv = h[:, o1:o2]
    u_ref[...] = h[:, o2:o3]
    kr_raw = h[:, o3:o4]
    kr_rot = h[:, o4:o4 + HEAD_PAD]
    cqn = _rms_norm(cq, qg_ref[...]).astype(BF16)
    ckvn = _rms_norm(ckv, kvg_ref[...]).astype(BF16)

    ang = pos_ref[...] * freq_ref[...]
    c = jnp.cos(ang)
    s = jnp.sin(ang)
    lane = lax.broadcasted_iota(jnp.int32, (1, HEAD_PAD), 1)
    scale = (QK_NOPE + QK_ROPE) ** -0.5 * LOG2_E
    is_rope = (lane >= QK_NOPE) & (lane < QK_NOPE + QK_ROPE)
    cos1 = jnp.where(lane < QK_NOPE, 1.0, jnp.where(is_rope, c, 0.0)) * scale
    sin1 = jnp.where(is_rope, s, 0.0) * scale
    cos_t = jnp.concatenate([cos1] * MLA_HEADS, axis=1)
    sin_t = jnp.concatenate([sin1] * MLA_HEADS, axis=1)
    q = _dot(cqn, wq_ref[...]) * cos_t + _dot(cqn, wqr_ref[...]) * sin_t
    q_ref[...] = q.astype(BF16)

    kr = kr_raw * c + kr_rot * s
    k = _dot(ckvn, wk_ref[...]) + jnp.concatenate([kr] * MLA_HEADS, axis=1)
    k_ref[...] = k.astype(BF16)
    ones_col = jnp.concatenate([jnp.where(lane == V_DIM, 1.0, 0.0)] * MLA_HEADS, axis=1)
    v_ref[...] = (_dot(ckvn, wv_ref[...]) + ones_col).astype(BF16)


def _attn_kernel(q_ref, k_ref, v_ref, o_ref, *, tq, heads):
    qi = pl.program_id(1)
    row_chunk = lax.broadcasted_iota(jnp.int32, (tq, tq), 0) // CHUNK
    col_chunk = lax.broadcasted_iota(jnp.int32, (tq, tq), 1) // CHUNK
    diag_mask = row_chunk >= col_chunk

    def block(j, carry, masked):
        start = pl.multiple_of(j * tq, tq)
        new = []
        for h in range(heads):
            m, acc = carry[2 * h], carry[2 * h + 1]
            cols = slice(h * HEAD_PAD, (h + 1) * HEAD_PAD)
            s = lax.dot_general(q_ref[:, cols], k_ref[pl.ds(start, tq), cols], NT_DIMS,
                                preferred_element_type=F32)
            if masked:
                s = jnp.where(diag_mask, s, -jnp.inf)
            m_new = jnp.maximum(m, jnp.max(s, axis=-1, keepdims=True))
            p = jnp.exp2(s - m_new).astype(BF16)
            acc = jnp.exp2(m - m_new) * acc + _dot(p, v_ref[pl.ds(start, tq), cols])
            new += [m_new, acc]
        return tuple(new)

    init = (jnp.full((tq, 1), -jnp.inf, F32), jnp.zeros((tq, HEAD_PAD), F32)) * heads
    carry = lax.fori_loop(0, qi, functools.partial(block, masked=False), init)
    fin = block(qi, carry, True)
    outs = [fin[2 * h + 1][:, :V_DIM] / fin[2 * h + 1][:, V_DIM:V_DIM + 1] for h in range(heads)]
    o_ref[...] = jnp.concatenate(outs, axis=1).astype(o_ref.dtype)


def _s5_kernel(u_ref, win_ref, are_ref, aim_ref, cre_ref, cim_ref, dskip_ref,
               wglu_ref, bglu_ref, o_ref, vx_ref, hre_ref, him_ref, io_ref, utm_ref, *,
               batch, lt, n_state, scan_tiles):
    ti = pl.program_id(0)

    @pl.when(ti == 0)
    def _():
        hre_ref[...] = jnp.zeros_like(hre_ref)
        him_ref[...] = jnp.zeros_like(him_ref)

    width = u_ref.shape[-1]
    n_slab = width // LANES
    n_tiles = n_state // LANES
    slab_tiles = n_tiles // n_slab
    for b in range(batch):
        for c in range(n_slab):
            io_ref[c, b * lt:(b + 1) * lt, :] = u_ref[b, :, c * LANES:(c + 1) * LANES]

    def to_time_major(t, _):
        dst = pl.ds(pl.multiple_of(t * batch, batch), batch)
        for c in range(n_slab):
            utm_ref[dst, c * LANES:(c + 1) * LANES] = io_ref[c, pl.ds(t, batch, stride=lt), :]
        return 0

    lax.fori_loop(0, lt, to_time_major, 0, unroll=4)
    u2 = utm_ref[...]
    ub = u2.astype(BF16)
    for j in range(n_slab):
        vj = _dot(ub[:, j * LANES:(j + 1) * LANES], win_ref[j])
        for i in range(slab_tiles):
            vx_ref[j * slab_tiles + i] = vj[:, i * LANES:(i + 1) * LANES]
            vx_ref[n_tiles + j * slab_tiles + i] = vj[:, (slab_tiles + i) * LANES:(slab_tiles + i + 1) * LANES]

    for c0 in range(0, n_tiles, scan_tiles):
        tiles = range(c0, c0 + scan_tiles)
        ar = [jnp.broadcast_to(are_ref[:, c * LANES:(c + 1) * LANES], (batch, LANES)) for c in tiles]
        ai = [jnp.broadcast_to(aim_ref[:, c * LANES:(c + 1) * LANES], (batch, LANES)) for c in tiles]

        def step(t, carry, tiles=tiles, ar=ar, ai=ai):
            rows = pl.ds(pl.multiple_of(t * batch, batch), batch)
            out = []
            for n, c in enumerate(tiles):
                hr, hi = carry[2 * n], carry[2 * n + 1]
                nr = ar[n] * hr - ai[n] * hi + vx_ref[c, rows, :]
                ni = ar[n] * hi + ai[n] * hr + vx_ref[n_tiles + c, rows, :]
                vx_ref[c, rows, :] = nr
                vx_ref[n_tiles + c, rows, :] = ni
                out += [nr, ni]
            return tuple(out)

        init = []
        for c in tiles:
            init += [hre_ref[c], him_ref[c]]
        fin = lax.fori_loop(0, lt, step, tuple(init), unroll=4)
        for n, c in enumerate(tiles):
            hre_ref[c] = fin[2 * n]
            him_ref[c] = fin[2 * n + 1]

    ys = []
    for j in range(n_slab):
        xr = jnp.concatenate([vx_ref[j * slab_tiles + i].astype(BF16) for i in range(slab_tiles)], axis=1)
        xi = jnp.concatenate([vx_ref[n_tiles + j * slab_tiles + i].astype(BF16) for i in range(slab_tiles)], axis=1)
        ys.append(_dot(xr, cre_ref[j]) + _dot(xi, cim_ref[j]))
    y = jnp.concatenate(ys, axis=1) + dskip_ref[...] * u2
    y = jax.nn.gelu(y)
    z = _dot(y.astype(BF16), wglu_ref[...]) + bglu_ref[...]
    out = y * jax.nn.sigmoid(z)
    for c in range(n_slab):
        io_ref[c] = out[:, c * LANES:(c + 1) * LANES]
    for b in range(batch):
        for c in range(n_slab):
            o_ref[b, :, c * LANES:(c + 1) * LANES] = io_ref[c, pl.ds(b, lt, stride=batch), :].astype(o_ref.dtype)


def _router_gates(logits_t, rbias):
    n_exp, tm = logits_t.shape
    per_group = n_exp // N_GROUPS
    scores = jax.nn.sigmoid(logits_t)
    sel = scores + rbias
    neg_inf = -jnp.inf
    sub_iota = lax.broadcasted_iota(jnp.int32, (per_group, tm), 0).astype(F32)
    group_score = []
    for g in range(N_GROUPS):
        sg = sel[g * per_group:(g + 1) * per_group, :]
        m1 = jnp.max(sg, axis=0, keepdims=True)
        first = jnp.min(jnp.where(sg == m1, sub_iota, float(per_group)), axis=0, keepdims=True)
        m2 = jnp.max(jnp.where(sub_iota == first, neg_inf, sg), axis=0, keepdims=True)
        group_score.append(m1 + m2)
    masked = []
    for g in range(N_GROUPS):
        rank = jnp.zeros((1, tm), F32)
        for g2 in range(N_GROUPS):
            if g2 == g:
                continue
            ahead = (group_score[g2] >= group_score[g]) if g2 < g else (group_score[g2] > group_score[g])
            rank = rank + jnp.where(ahead, 1.0, 0.0)
        keep = rank < float(TOP_GROUPS)
        masked.append(jnp.where(keep, sel[g * per_group:(g + 1) * per_group, :], neg_inf))
    cur = jnp.concatenate(masked, axis=0)
    iota = lax.broadcasted_iota(jnp.int32, (n_exp, tm), 0).astype(F32)
    chosen = jnp.zeros((n_exp, tm), F32)
    for _ in range(TOP_K):
        m = jnp.max(cur, axis=0, keepdims=True)
        idx = jnp.min(jnp.where(cur == m, iota, float(n_exp)), axis=0, keepdims=True)
        pick = iota == idx
        chosen = jnp.where(pick, 1.0, chosen)
        cur = jnp.where(pick, neg_inf, cur)
    w = scores * chosen
    return w / jnp.sum(w, axis=0, keepdims=True) * ROUTED_SCALE


def _mix_kernel(att_ref, ssm_ref, xn_ref, ag_ref, sg_ref, woa_ref, wos_ref,
                g_ref, b_ref, wrh_ref, wrl_ref, rb_ref,
                x1_ref, x1b_ref, gates_ref):
    an = _rms_norm(att_ref[...].astype(F32), ag_ref[...]).astype(BF16)
    sn = _rms_norm(ssm_ref[...].astype(F32), sg_ref[...]).astype(BF16)
    mix = _dot(an, woa_ref[...]) + _dot(sn, wos_ref[...])
    x1 = _layer_norm(ALPHA * xn_ref[...] + mix, g_ref[...], b_ref[...])
    x1_ref[...] = x1
    x_hi = x1.astype(BF16)
    x1b_ref[...] = x_hi
    x_lo = (x1 - x_hi.astype(F32)).astype(BF16)
    dg = functools.partial(lax.dot_general, dimension_numbers=NT_DIMS, preferred_element_type=F32)
    logits_t = dg(wrh_ref[...], x_hi) + dg(wrl_ref[...], x_hi) + dg(wrh_ref[...], x_lo)
    gates_ref[...] = _router_gates(logits_t, rb_ref[...])


def _moe_kernel(xb_ref, x1_ref, gates_ref, wg_ref, wu_ref, wd_ref, g_ref, b_ref,
                o_ref, acc_ref):
    e = pl.program_id(1)

    @pl.when(e == 0)
    def _():
        acc_ref[...] = jnp.zeros_like(acc_ref)

    xb = xb_ref[...]
    lane = lax.broadcasted_iota(jnp.int32, gates_ref.shape, 1)
    gcol = jnp.sum(jnp.where(lane == e, gates_ref[...], 0.0), axis=1, keepdims=True)
    hg = _dot(xb, wg_ref[0])
    hu = _dot(xb, wu_ref[0])
    h = jax.nn.silu(hg) * hu * gcol
    acc_ref[...] += _dot(h.astype(BF16), wd_ref[0])

    @pl.when(e == pl.num_programs(1) - 1)
    def _():
        o_ref[...] = _layer_norm(ALPHA * x1_ref[...] + acc_ref[...], g_ref[...], b_ref[...])


def _row(v):
    return v.reshape(1, -1).astype(F32)


def _const_spec(shape):
    nd = len(shape)
    return pl.BlockSpec(shape, lambda *_: (0,) * nd)


def _pad_heads(w, width):
    r, h, _ = w.shape
    return jnp.pad(w, ((0, 0), (0, 0), (0, HEAD_PAD - width))).reshape(r, h * HEAD_PAD)


def _half_rotate(w):
    half = QK_ROPE // 2
    return jnp.concatenate([-w[..., half:], w[..., :half]], axis=-1)


def kernel(x, positions, ln_in_g, ln_in_b, w_in, q_norm_g, kv_norm_g, w_uq, w_ukv, lambda_re, lambda_im, log_step, b_re, b_im, c_re, c_im, d_skip, w_glu, b_glu, attn_out_g, ssm_out_g, w_o, ln1_g, ln1_b, w_router, router_bias, w_gate, w_up, w_down, ws_gate, ws_up, ws_down, ln2_g, ln2_b):
    B, S, D = x.shape
    T = B * S
    l = 0
    ssm_width = w_glu.shape[-1]
    n_groups = ssm_width // SSM_GROUP
    n_state = n_groups * SSM_STATE
    mla_width = MLA_HEADS * V_DIM
    qk_pad = MLA_HEADS * HEAD_PAD
    cparams = functools.partial(pltpu.CompilerParams, vmem_limit_bytes=VMEM_LIMIT)

    s1, s2, s3 = Q_RANK, Q_RANK + KV_RANK, Q_RANK + KV_RANK + QK_ROPE
    wi = w_in[l]
    w_kr = wi[:, s2:s3]
    pad_rope = lambda w: jnp.pad(w, ((0, 0), (QK_NOPE, HEAD_PAD - QK_NOPE - QK_ROPE)))
    w1 = jnp.concatenate([wi[:, :s2], wi[:, s3:], pad_rope(w_kr), pad_rope(_half_rotate(w_kr))],
                         axis=1).astype(BF16)
    wq = w_uq[l]
    zeros_nope = jnp.zeros(wq.shape[:2] + (QK_NOPE,), wq.dtype)
    wq_main = _pad_heads(wq, QK_NOPE + QK_ROPE).astype(BF16)
    wq_rot = _pad_heads(jnp.concatenate([zeros_nope, _half_rotate(wq[..., QK_NOPE:])], axis=-1),
                        QK_NOPE + QK_ROPE).astype(BF16)
    wkv = w_ukv[l]
    wk = _pad_heads(wkv[..., :QK_NOPE], QK_NOPE).astype(BF16)
    wv = _pad_heads(wkv[..., QK_NOPE:], V_DIM).astype(BF16)
    half = QK_ROPE // 2
    inv_freq = ROPE_THETA ** (-jnp.arange(half, dtype=F32) / half)
    freq = jnp.pad(jnp.concatenate([inv_freq, inv_freq]),
                   (QK_NOPE, HEAD_PAD - QK_NOPE - QK_ROPE)).reshape(1, HEAD_PAD)
    pos_f = positions.astype(F32).reshape(T, 1)

    tm = min(512, T)
    w1_cols = w1.shape[1]
    tok = lambda width: pl.BlockSpec((tm, width), lambda i: (i, 0))
    xn, q, k, v, u = pl.pallas_call(
        functools.partial(_inproj_kernel, ssm_width=ssm_width),
        grid=(T // tm,),
        in_specs=[tok(D), tok(1), _const_spec((1, D)), _const_spec((1, D)),
                  _const_spec((D, w1_cols)), _const_spec((1, Q_RANK)), _const_spec((1, KV_RANK)),
                  _const_spec((Q_RANK, qk_pad)), _const_spec((Q_RANK, qk_pad)),
                  _const_spec((KV_RANK, qk_pad)), _const_spec((KV_RANK, qk_pad)),
                  _const_spec((1, HEAD_PAD))],
        out_specs=[tok(D), tok(qk_pad), tok(qk_pad), tok(qk_pad), tok(ssm_width)],
        out_shape=[jax.ShapeDtypeStruct((T, D), F32), jax.ShapeDtypeStruct((T, qk_pad), BF16),
                   jax.ShapeDtypeStruct((T, qk_pad), BF16), jax.ShapeDtypeStruct((T, qk_pad), BF16),
                   jax.ShapeDtypeStruct((T, ssm_width), F32)],
        compiler_params=cparams(dimension_semantics=("parallel",)),
        name="inproj",
    )(x.reshape(T, D), pos_f, _row(ln_in_g), _row(ln_in_b), w1, _row(q_norm_g[l]), _row(kv_norm_g[l]),
      wq_main, wq_rot, wk, wv, freq)

    tq = min(256, S)
    nq = S // tq
    att = pl.pallas_call(
        functools.partial(_attn_kernel, tq=tq, heads=MLA_HEADS),
        grid=(B, nq),
        in_specs=[pl.BlockSpec((tq, qk_pad), lambda b, i: (b * nq + i, 0)),
                  pl.BlockSpec((S, qk_pad), lambda b, i: (b, 0)),
                  pl.BlockSpec((S, qk_pad), lambda b, i: (b, 0))],
        out_specs=pl.BlockSpec((tq, mla_width), lambda b, i: (b * nq + i, 0)),
        out_shape=jax.ShapeDtypeStruct((T, mla_width), BF16),
        compiler_params=cparams(dimension_semantics=("parallel", "arbitrary")),
        name="attention",
    )(q, k, v)

    lam = lax.complex(jnp.minimum(lambda_re[l].astype(F32), -1e-4), lambda_im[l].astype(F32))
    step = jnp.exp(log_step[l].astype(F32))[:, None]
    lam_bar = jnp.exp(lam * step)
    b_bar = ((lam_bar - 1.0) / lam)[..., None] * lax.complex(b_re[l].astype(F32), b_im[l].astype(F32))
    n_slab = ssm_width // LANES
    g_per_slab = n_groups // n_slab
    eye = jnp.eye(n_groups, dtype=F32)

    def expand_in(bpart):
        return jnp.einsum('gpc,gh->gchp', bpart, eye).reshape(ssm_width, n_state)

    def expand_out(cpart):
        return jnp.einsum('gcp,gh->gphc', cpart, eye).reshape(n_state, ssm_width)

    slab = n_state // n_slab
    win_re, win_im = expand_in(jnp.real(b_bar)), expand_in(jnp.imag(b_bar))
    win = jnp.stack([jnp.concatenate([win_re[j * LANES:(j + 1) * LANES, j * slab:(j + 1) * slab],
                                      win_im[j * LANES:(j + 1) * LANES, j * slab:(j + 1) * slab]], axis=1)
                     for j in range(n_slab)]).astype(BF16)
    wc_re, wc_im = expand_out(c_re[l].astype(F32)), expand_out(-c_im[l].astype(F32))
    cre = jnp.stack([wc_re[j * slab:(j + 1) * slab, j * LANES:(j + 1) * LANES] for j in range(n_slab)]).astype(BF16)
    cim = jnp.stack([wc_im[j * slab:(j + 1) * slab, j * LANES:(j + 1) * LANES] for j in range(n_slab)]).astype(BF16)
    a_re = jnp.real(lam_bar).reshape(1, n_state)
    a_im = jnp.imag(lam_bar).reshape(1, n_state)

    lt = min(64, S)
    ssm = pl.pallas_call(
        functools.partial(_s5_kernel, batch=B, lt=lt, n_state=n_state, scan_tiles=4),
        grid=(S // lt,),
        in_specs=[pl.BlockSpec((B, lt, ssm_width), lambda t: (0, t, 0)),
                  _const_spec(win.shape), _const_spec((1, n_state)), _const_spec((1, n_state)),
                  _const_spec(cre.shape), _const_spec(cim.shape), _const_spec((1, ssm_width)),
                  _const_spec((ssm_width, ssm_width)), _const_spec((1, ssm_width))],
        out_specs=pl.BlockSpec((B, lt, ssm_width), lambda t: (0, t, 0)),
        out_shape=jax.ShapeDtypeStruct((B, S, ssm_width), BF16),
        scratch_shapes=[pltpu.VMEM((2 * n_state // LANES, B * lt, LANES), F32),
                        pltpu.VMEM((n_state // LANES, B, LANES), F32),
                        pltpu.VMEM((n_state // LANES, B, LANES), F32),
                        pltpu.VMEM((ssm_width // LANES, B * lt, LANES), F32),
                        pltpu.VMEM((B * lt, ssm_width), F32)],
        compiler_params=cparams(dimension_semantics=("arbitrary",)),
        name="s5",
    )(u.reshape(B, S, ssm_width), win, a_re, a_im, cre, cim, _row(d_skip[l]),
      w_glu[l].astype(BF16), _row(b_glu[l]))

    wo = w_o[l].astype(BF16)
    wr_t = w_router[l].T.astype(F32)
    wr_hi = wr_t.astype(BF16)
    wr_lo = (wr_t - wr_hi.astype(F32)).astype(BF16)
    x1, x1b, gates_t = pl.pallas_call(
        _mix_kernel,
        grid=(T // tm,),
        in_specs=[tok(mla_width), tok(ssm_width), tok(D), _const_spec((1, mla_width)),
                  _const_spec((1, ssm_width)), _const_spec((mla_width, D)), _const_spec((ssm_width, D)),
                  _const_spec((1, D)), _const_spec((1, D)), _const_spec((N_EXPERTS, D)),
                  _const_spec((N_EXPERTS, D)), _const_spec((N_EXPERTS, 1))],
        out_specs=[tok(D), tok(D), pl.BlockSpec((N_EXPERTS, tm), lambda i: (0, i))],
        out_shape=[jax.ShapeDtypeStruct((T, D), F32), jax.ShapeDtypeStruct((T, D), BF16),
                   jax.ShapeDtypeStruct((N_EXPERTS, T), F32)],
        compiler_params=cparams(dimension_semantics=("parallel",)),
        name="mix_router",
    )(att, ssm.reshape(T, ssm_width), xn, _row(attn_out_g[l]), _row(ssm_out_g[l]),
      wo[:mla_width], wo[mla_width:], _row(ln1_g[l]), _row(ln1_b[l]), wr_hi, wr_lo,
      router_bias[l].astype(F32).reshape(N_EXPERTS, 1))

    n_all = N_EXPERTS + 1
    gates = jnp.concatenate([gates_t.T, jnp.ones((T, 1), F32),
                             jnp.zeros((T, LANES - n_all), F32)], axis=1)
    wg_all = jnp.concatenate([w_gate[l], ws_gate[l][None]], axis=0).astype(BF16)
    wu_all = jnp.concatenate([w_up[l], ws_up[l][None]], axis=0).astype(BF16)
    wd_all = jnp.concatenate([w_down[l], ws_down[l][None]], axis=0).astype(BF16)
    ff = wg_all.shape[-1]
    tme = min(1024, T)
    out = pl.pallas_call(
        _moe_kernel,
        grid=(T // tme, n_all),
        in_specs=[pl.BlockSpec((tme, D), lambda i, e: (i, 0)),
                  pl.BlockSpec((tme, D), lambda i, e: (i, 0)),
                  pl.BlockSpec((tme, LANES), lambda i, e: (i, 0)),
                  pl.BlockSpec((1, D, ff), lambda i, e: (e, 0, 0)),
                  pl.BlockSpec((1, D, ff), lambda i, e: (e, 0, 0)),
                  pl.BlockSpec((1, ff, D), lambda i, e: (e, 0, 0)),
                  pl.BlockSpec((1, D), lambda i, e: (0, 0)),
                  pl.BlockSpec((1, D), lambda i, e: (0, 0))],
        out_specs=pl.BlockSpec((tme, D), lambda i, e: (i, 0)),
        out_shape=jax.ShapeDtypeStruct((T, D), F32),
        scratch_shapes=[pltpu.VMEM((tme, D), F32)],
        compiler_params=cparams(dimension_semantics=("parallel", "arbitrary")),
        name="experts",
    )(x1b, x1, gates, wg_all, wu_all, wd_all, _row(ln2_g[l]), _row(ln2_b[l]))
    return out.reshape(B, S, D)
```

```python
import functools

import jax
import jax.numpy as jnp
from jax import lax
from jax.experimental import pallas as pl
from jax.experimental.pallas import tpu as pltpu
from jax.experimental.pallas import tpu_sc as plsc

CHUNK = 64
MLA_HEADS = 8
QK_NOPE = 64
QK_ROPE = 32
V_DIM = 64
Q_RANK = 256
KV_RANK = 128
ROPE_THETA = 10000.0
SSM_GROUP = 16
SSM_STATE = 64
N_EXPERTS = 64
TOP_K = 8
N_GROUPS = 8
TOP_GROUPS = 4
ROUTED_SCALE = 2.5
DEPTH = 1
ALPHA = (2.0 * DEPTH) ** 0.25
EPS = 1e-5
LOG2_E = 1.4426950408889634

LANES = 128
HEAD_PAD = LANES
VMEM_LIMIT = 56 * 1024 * 1024
EXPERT_ROW_TILE = 512

BF16 = jnp.bfloat16
F32 = jnp.float32
NT_DIMS = (((1,), (1,)), ((), ()))


def _dot(a, b):
    return jnp.dot(a, b, preferred_element_type=F32)


def _layer_norm(x, g, b):
    mu = jnp.mean(x, axis=-1, keepdims=True)
    xc = x - mu
    var = jnp.mean(xc * xc, axis=-1, keepdims=True)
    return xc * lax.rsqrt(var + EPS) * g + b


def _rms_norm(x, g):
    return x * lax.rsqrt(jnp.mean(x * x, axis=-1, keepdims=True) + EPS) * g


def _inproj_kernel(x_ref, pos_ref, lng_ref, lnb_ref, w1_ref, qg_ref, kvg_ref,
                   wq_ref, wqr_ref, wk_ref, wv_ref, freq_ref,
                   xn_ref, q_ref, k_ref, v_ref, u_ref, *, ssm_width):
    xn = _layer_norm(x_ref[...], lng_ref[...], lnb_ref[...])
    xn_ref[...] = xn
    h = _dot(xn.astype(BF16), w1_ref[...])
    o1 = Q_RANK
    o2 = o1 + KV_RANK
    o3 = o2 + ssm_width
    o4 = o3 + HEAD_PAD
    cq = h[:, :o1]
    ckv = h[:, o1:o2]
    u_ref[...] = h[:, o2:o3]
    kr_raw = h[:, o3:o4]
    kr_rot = h[:, o4:o4 + HEAD_PAD]
    cqn = _rms_norm(cq, qg_ref[...]).astype(BF16)
    ckvn = _rms_norm(ckv, kvg_ref[...]).astype(BF16)

    ang = pos_ref[...] * freq_ref[...]
    c = jnp.cos(ang)
    s = jnp.sin(ang)
    lane = lax.broadcasted_iota(jnp.int32, (1, HEAD_PAD), 1)
    scale = (QK_NOPE + QK_ROPE) ** -0.5 * LOG2_E
    is_rope = (lane >= QK_NOPE) & (lane < QK_NOPE + QK_ROPE)
    cos1 = jnp.where(lane < QK_NOPE, 1.0, jnp.where(is_rope, c, 0.0)) * scale
    sin1 = jnp.where(is_rope, s, 0.0) * scale
    cos_t = jnp.concatenate([cos1] * MLA_HEADS, axis=1)
    sin_t = jnp.concatenate([sin1] * MLA_HEADS, axis=1)
    q = _dot(cqn, wq_ref[...]) * cos_t + _dot(cqn, wqr_ref[...]) * sin_t
    q_ref[...] = q.astype(BF16)

    kr = kr_raw * c + kr_rot * s
    k = _dot(ckvn, wk_ref[...]) + jnp.concatenate([kr] * MLA_HEADS, axis=1)
    k_ref[...] = k.astype(BF16)
    ones_col = jnp.concatenate([jnp.where(lane == V_DIM, 1.0, 0.0)] * MLA_HEADS, axis=1)
    v_ref[...] = (_dot(ckvn, wv_ref[...]) + ones_col).astype(BF16)


def _attn_kernel(q_ref, k_ref, v_ref, o_ref, *, tq, heads):
    qi = pl.program_id(1)
    row_chunk = lax.broadcasted_iota(jnp.int32, (tq, tq), 0) // CHUNK
    col_chunk = lax.broadcasted_iota(jnp.int32, (tq, tq), 1) // CHUNK
    diag_mask = row_chunk >= col_chunk

    def block(j, carry, masked):
        start = pl.multiple_of(j * tq, tq)
        new = []
        for h in range(heads):
            m, acc = carry[2 * h], carry[2 * h + 1]
            cols = slice(h * HEAD_PAD, (h + 1) * HEAD_PAD)
            s = lax.dot_general(q_ref[:, cols], k_ref[pl.ds(start, tq), cols], NT_DIMS,
                                preferred_element_type=F32)
            if masked:
                s = jnp.where(diag_mask, s, -jnp.inf)
            m_new = jnp.maximum(m, jnp.max(s, axis=-1, keepdims=True))
            p = jnp.exp2(s - m_new).astype(BF16)
            acc = jnp.exp2(m - m_new) * acc + _dot(p, v_ref[pl.ds(start, tq), cols])
            new += [m_new, acc]
        return tuple(new)

    init = (jnp.full((tq, 1), -jnp.inf, F32), jnp.zeros((tq, HEAD_PAD), F32)) * heads
    carry = lax.fori_loop(0, qi, functools.partial(block, masked=False), init)
    fin = block(qi, carry, True)
    outs = [fin[2 * h + 1][:, :V_DIM] / fin[2 * h + 1][:, V_DIM:V_DIM + 1] for h in range(heads)]
    o_ref[...] = jnp.concatenate(outs, axis=1).astype(o_ref.dtype)


def _s5_kernel(u_ref, win_ref, are_ref, aim_ref, cre_ref, cim_ref, dskip_ref,
               wglu_ref, bglu_ref, o_ref, vx_ref, hre_ref, him_ref, io_ref, utm_ref, *,
               batch, lt, n_state, scan_tiles):
    ti = pl.program_id(0)

    @pl.when(ti == 0)
    def _():
        hre_ref[...] = jnp.zeros_like(hre_ref)
        him_ref[...] = jnp.zeros_like(him_ref)

    width = u_ref.shape[-1]
    n_slab = width // LANES
    n_tiles = n_state // LANES
    slab_tiles = n_tiles // n_slab
    for b in range(batch):
        for c in range(n_slab):
            io_ref[c, b * lt:(b + 1) * lt, :] = u_ref[b, :, c * LANES:(c + 1) * LANES]

    def to_time_major(t, _):
        dst = pl.ds(pl.multiple_of(t * batch, batch), batch)
        for c in range(n_slab):
            utm_ref[dst, c * LANES:(c + 1) * LANES] = io_ref[c, pl.ds(t, batch, stride=lt), :]
        return 0

    lax.fori_loop(0, lt, to_time_major, 0, unroll=4)
    u2 = utm_ref[...]
    ub = u2.astype(BF16)
    for j in range(n_slab):
        vj = _dot(ub[:, j * LANES:(j + 1) * LANES], win_ref[j])
        for i in range(slab_tiles):
            vx_ref[j * slab_tiles + i] = vj[:, i * LANES:(i + 1) * LANES]
            vx_ref[n_tiles + j * slab_tiles + i] = vj[:, (slab_tiles + i) * LANES:(slab_tiles + i + 1) * LANES]

    for c0 in range(0, n_tiles, scan_tiles):
        tiles = range(c0, c0 + scan_tiles)
        ar = [jnp.broadcast_to(are_ref[:, c * LANES:(c + 1) * LANES], (batch, LANES)) for c in tiles]
        ai = [jnp.broadcast_to(aim_ref[:, c * LANES:(c + 1) * LANES], (batch, LANES)) for c in tiles]

        def step(t, carry, tiles=tiles, ar=ar, ai=ai):
            rows = pl.ds(pl.multiple_of(t * batch, batch), batch)
            out = []
            for n, c in enumerate(tiles):
                hr, hi = carry[2 * n], carry[2 * n + 1]
                nr = ar[n] * hr - ai[n] * hi + vx_ref[c, rows, :]
                ni = ar[n] * hi + ai[n] * hr + vx_ref[n_tiles + c, rows, :]
                vx_ref[c, rows, :] = nr
                vx_ref[n_tiles + c, rows, :] = ni
                out += [nr, ni]
            return tuple(out)

        init = []
        for c in tiles:
            init += [hre_ref[c], him_ref[c]]
        fin = lax.fori_loop(0, lt, step, tuple(init), unroll=4)
        for n, c in enumerate(tiles):
            hre_ref[c] = fin[2 * n]
            him_ref[c] = fin[2 * n + 1]

    ys = []
    for j in range(n_slab):
        xr = jnp.concatenate([vx_ref[j * slab_tiles + i].astype(BF16) for i in range(slab_tiles)], axis=1)
        xi = jnp.concatenate([vx_ref[n_tiles + j * slab_tiles + i].astype(BF16) for i in range(slab_tiles)], axis=1)
        ys.append(_dot(xr, cre_ref[j]) + _dot(xi, cim_ref[j]))
    y = jnp.concatenate(ys, axis=1) + dskip_ref[...] * u2
    y = jax.nn.gelu(y)
    z = _dot(y.astype(BF16), wglu_ref[...]) + bglu_ref[...]
    out = y * jax.nn.sigmoid(z)
    for c in range(n_slab):
        io_ref[c] = out[:, c * LANES:(c + 1) * LANES]
    for b in range(batch):
        for c in range(n_slab):
            o_ref[b, :, c * LANES:(c + 1) * LANES] = io_ref[c, pl.ds(b, lt, stride=batch), :].astype(o_ref.dtype)


def _router_gates(logits_t, rbias):
    n_exp, tm = logits_t.shape
    per_group = n_exp // N_GROUPS
    scores = jax.nn.sigmoid(logits_t)
    sel = scores + rbias
    neg_inf = -jnp.inf
    sub_iota = lax.broadcasted_iota(jnp.int32, (per_group, tm), 0).astype(F32)
    group_score = []
    for g in range(N_GROUPS):
        sg = sel[g * per_group:(g + 1) * per_group, :]
        m1 = jnp.max(sg, axis=0, keepdims=True)
        first = jnp.min(jnp.where(sg == m1, sub_iota, float(per_group)), axis=0, keepdims=True)
        m2 = jnp.max(jnp.where(sub_iota == first, neg_inf, sg), axis=0, keepdims=True)
        group_score.append(m1 + m2)
    masked = []
    for g in range(N_GROUPS):
        rank = jnp.zeros((1, tm), F32)
        for g2 in range(N_GROUPS):
            if g2 == g:
                continue
            ahead = (group_score[g2] >= group_score[g]) if g2 < g else (group_score[g2] > group_score[g])
            rank = rank + jnp.where(ahead, 1.0, 0.0)
        keep = rank < float(TOP_GROUPS)
        masked.append(jnp.where(keep, sel[g * per_group:(g + 1) * per_group, :], neg_inf))
    cur = jnp.concatenate(masked, axis=0)
    iota = lax.broadcasted_iota(jnp.int32, (n_exp, tm), 0).astype(F32)
    chosen = jnp.zeros((n_exp, tm), F32)
    picks, weights = [], []
    for _ in range(TOP_K):
        m = jnp.max(cur, axis=0, keepdims=True)
        idx = jnp.min(jnp.where(cur == m, iota, float(n_exp)), axis=0, keepdims=True)
        pick = iota == idx
        chosen = jnp.where(pick, 1.0, chosen)
        cur = jnp.where(pick, neg_inf, cur)
        picks.append(idx)
        weights.append(jnp.sum(jnp.where(pick, scores, 0.0), axis=0, keepdims=True))
    idx_k = jnp.concatenate(picks, axis=0)
    w_k = jnp.concatenate(weights, axis=0)
    gate_k = w_k / jnp.sum(w_k, axis=0, keepdims=True) * ROUTED_SCALE
    return idx_k, gate_k, chosen


def _pack_bf16_pairs(x):
    n = x.shape[1] // 2
    hi = lax.bitcast_convert_type(x[:, :n].astype(BF16).astype(F32), jnp.int32)
    lo = lax.bitcast_convert_type(x[:, n:].astype(BF16).astype(F32), jnp.int32)
    return hi | lax.shift_right_logical(lo, 16)


def _unpack_bf16_pairs(p):
    hi = lax.bitcast_convert_type(p & jnp.int32(-65536), F32).astype(BF16)
    lo = lax.bitcast_convert_type(lax.shift_left(p, 16), F32).astype(BF16)
    return hi, lo


def _mix_kernel(att_ref, ssm_ref, xn_ref, ag_ref, sg_ref, woa_ref, wos_ref,
                g_ref, b_ref, wrh_ref, wrl_ref, rb_ref,
                x1_ref, x1p_ref, idx_ref, gate_ref, rank_ref, cnt_ref, carry_ref):
    @pl.when(pl.program_id(0) == 0)
    def _():
        carry_ref[...] = jnp.zeros_like(carry_ref)

    an = _rms_norm(att_ref[...].astype(F32), ag_ref[...]).astype(BF16)
    sn = _rms_norm(ssm_ref[...].astype(F32), sg_ref[...]).astype(BF16)
    mix = _dot(an, woa_ref[...]) + _dot(sn, wos_ref[...])
    x1 = _layer_norm(ALPHA * xn_ref[...] + mix, g_ref[...], b_ref[...])
    x1_ref[...] = x1
    x1p_ref[...] = _pack_bf16_pairs(x1)
    x_hi = x1.astype(BF16)
    x_lo = (x1 - x_hi.astype(F32)).astype(BF16)
    dg = functools.partial(lax.dot_general, dimension_numbers=NT_DIMS, preferred_element_type=F32)
    logits_t = dg(wrh_ref[...], x_hi) + dg(wrl_ref[...], x_hi) + dg(wrh_ref[...], x_lo)
    idx_k, gate_k, chosen = _router_gates(logits_t, rb_ref[...])
    idx_ref[...] = idx_k.astype(jnp.int32)
    gate_ref[...] = gate_k

    n_exp, tm = chosen.shape
    before = (lax.broadcasted_iota(jnp.int32, (tm, tm), 0)
              < lax.broadcasted_iota(jnp.int32, (tm, tm), 1))
    excl = _dot(chosen.astype(BF16), jnp.where(before, 1.0, 0.0).astype(BF16))
    rank_full = carry_ref[...] + excl
    iota = lax.broadcasted_iota(jnp.int32, (n_exp, tm), 0).astype(F32)
    ranks = [jnp.sum(jnp.where(iota == idx_k[k:k + 1, :], rank_full, 0.0), axis=0, keepdims=True)
             for k in range(TOP_K)]
    rank_ref[...] = jnp.concatenate(ranks, axis=0).astype(jnp.int32)
    total = carry_ref[...] + jnp.sum(chosen, axis=1, keepdims=True)
    carry_ref[...] = total
    cnt_ref[...] = jnp.broadcast_to(total, cnt_ref.shape)


def _positions_kernel(off_ref, idx_ref, rank_ref, pos_ref):
    idx = idx_ref[...]
    base = jnp.zeros(idx.shape, jnp.int32)
    for e in range(N_EXPERTS):
        base = jnp.where(idx == e, off_ref[e], base)
    pos_ref[...] = rank_ref[...] + base


def _swiglu(x_hi, x_lo, wg_ref, wu_ref, wd_ref):
    half = x_hi.shape[1]
    hg = _dot(x_hi, wg_ref[:half, :]) + _dot(x_lo, wg_ref[half:, :])
    hu = _dot(x_hi, wu_ref[:half, :]) + _dot(x_lo, wu_ref[half:, :])
    h = jax.nn.silu(hg) * hu
    return _dot(h.astype(BF16), wd_ref[...])


def _experts_kernel(te_ref, valid_ref, xs_ref, wg_ref, wu_ref, wd_ref, ys_ref):
    i = pl.program_id(0)
    valid = valid_ref[i]

    @pl.when(valid > 0)
    def _():
        rows = lax.broadcasted_iota(jnp.int32, xs_ref.shape, 0)
        x_hi, x_lo = _unpack_bf16_pairs(jnp.where(rows < valid, xs_ref[...], 0))
        ys_ref[...] = _pack_bf16_pairs(_swiglu(x_hi, x_lo, wg_ref.at[0], wu_ref.at[0], wd_ref.at[0]))

    @pl.when(valid == 0)
    def _():
        ys_ref[...] = jnp.zeros_like(ys_ref)


def _combine_kernel(yg_ref, gate_ref, x1_ref, wsg_ref, wsu_ref, wsd_ref, g_ref, b_ref, o_ref):
    x1 = x1_ref[...]
    half = x1.shape[1] // 2
    acc = _swiglu(x1[:, :half].astype(BF16), x1[:, half:].astype(BF16), wsg_ref, wsu_ref, wsd_ref)
    gates = gate_ref[...]
    acc_hi, acc_lo = acc[:, :half], acc[:, half:]
    for k in range(TOP_K):
        y_hi, y_lo = _unpack_bf16_pairs(yg_ref[k])
        gk = gates[:, k:k + 1]
        acc_hi = acc_hi + gk * y_hi.astype(F32)
        acc_lo = acc_lo + gk * y_lo.astype(F32)
    ffn = jnp.concatenate([acc_hi, acc_lo], axis=1)
    o_ref[...] = _layer_norm(ALPHA * x1 + ffn, g_ref[...], b_ref[...])


SC_CORES = 2
SC_SUBCORES = 16
SC_WORKERS = SC_CORES * SC_SUBCORES
SC_CHUNK = 64


def _sc_mesh():
    return plsc.VectorSubcoreMesh(core_axis_name="c", subcore_axis_name="s")


def _sc_gather_rows(table, idx):
    n = idx.shape[0]
    d = table.shape[1]
    per_w = n // SC_WORKERS
    n_ch = per_w // SC_CHUNK

    @functools.partial(
        pl.kernel, mesh=_sc_mesh(),
        out_type=jax.ShapeDtypeStruct((n, d), table.dtype),
        scratch_types=[pltpu.VMEM((n_ch, SC_CHUNK), jnp.int32),
                       pltpu.VMEM((2, SC_CHUNK, d), table.dtype),
                       pltpu.SemaphoreType.DMA((2,)),
                       pltpu.SemaphoreType.DMA((2,))],
    )
    def k(table_hbm, idx_hbm, out_hbm, idx_v, buf, gsem, osem):
        wid = lax.axis_index("s") * SC_CORES + lax.axis_index("c")
        base = wid * per_w
        pltpu.sync_copy(idx_hbm.at[wid], idx_v)

        def gather(c, b):
            return pltpu.make_async_copy(table_hbm.at[idx_v.at[c]], buf.at[b], gsem.at[b])

        def put(c, b):
            return pltpu.make_async_copy(buf.at[b], out_hbm.at[pl.ds(base + c * SC_CHUNK, SC_CHUNK)],
                                         osem.at[b])

        gather(0, 0).start()

        @pl.loop(0, n_ch, step=2)
        def _(c):
            for b in range(2):
                cc = c + b
                gather(cc, b).wait()

                @pl.when(cc + 1 < n_ch)
                def _():
                    @pl.when(cc >= 1)
                    def _():
                        put(cc - 1, 1 - b).wait()
                    gather(cc + 1, 1 - b).start()

                put(cc, b).start()

        put(n_ch - 2, 0).wait()
        put(n_ch - 1, 1).wait()

    return k(table, idx.reshape(SC_WORKERS, n_ch, SC_CHUNK))


def _sc_scatter_rows(x, pos, n_out):
    t, d = x.shape
    kk = pos.shape[0]
    per_w = t // SC_WORKERS
    n_ch = per_w // SC_CHUNK
    pos_w = pos.reshape(kk, SC_WORKERS, n_ch, SC_CHUNK).transpose(1, 2, 0, 3)
    pos_w = pos_w.reshape(SC_WORKERS, n_ch * kk, SC_CHUNK)

    @functools.partial(
        pl.kernel, mesh=_sc_mesh(),
        out_type=jax.ShapeDtypeStruct((n_out, d), x.dtype),
        scratch_types=[pltpu.VMEM((n_ch * kk, SC_CHUNK), jnp.int32),
                       pltpu.VMEM((2, SC_CHUNK, d), x.dtype),
                       pltpu.SemaphoreType.DMA((2,)),
                       pltpu.SemaphoreType.DMA((2,))],
    )
    def k(x_hbm, pos_hbm, out_hbm, idx_v, buf, isem, osem):
        wid = lax.axis_index("s") * SC_CORES + lax.axis_index("c")
        base = wid * per_w
        pltpu.sync_copy(pos_hbm.at[wid], idx_v)

        def get(c, b):
            return pltpu.make_async_copy(x_hbm.at[pl.ds(base + c * SC_CHUNK, SC_CHUNK)], buf.at[b],
                                         isem.at[b])

        def put(c, j, b):
            return pltpu.make_async_copy(buf.at[b], out_hbm.at[idx_v.at[c * kk + j]], osem.at[b])

        get(0, 0).start()

        @pl.loop(0, n_ch, step=2)
        def _(c):
            for b in range(2):
                cc = c + b
                get(cc, b).wait()

                @pl.when(cc + 1 < n_ch)
                def _():
                    @pl.when(cc >= 1)
                    def _():
                        for j in range(kk):
                            put(cc - 1, j, 1 - b).wait()
                    get(cc + 1, 1 - b).start()

                for j in range(kk):
                    put(cc, j, b).start()

        for j in range(kk):
            put(n_ch - 2, j, 0).wait()
        for j in range(kk):
            put(n_ch - 1, j, 1).wait()

    return k(x, pos_w)


def _row(v):
    return v.reshape(1, -1).astype(F32)


def _const_spec(shape):
    nd = len(shape)
    return pl.BlockSpec(shape, lambda *_: (0,) * nd)


def _pad_heads(w, width):
    r, h, _ = w.shape
    return jnp.pad(w, ((0, 0), (0, 0), (0, HEAD_PAD - width))).reshape(r, h * HEAD_PAD)


def _half_rotate(w):
    half = QK_ROPE // 2
    return jnp.concatenate([-w[..., half:], w[..., :half]], axis=-1)


def kernel(x, positions, ln_in_g, ln_in_b, w_in, q_norm_g, kv_norm_g, w_uq, w_ukv, lambda_re, lambda_im, log_step, b_re, b_im, c_re, c_im, d_skip, w_glu, b_glu, attn_out_g, ssm_out_g, w_o, ln1_g, ln1_b, w_router, router_bias, w_gate, w_up, w_down, ws_gate, ws_up, ws_down, ln2_g, ln2_b):
    B, S, D = x.shape
    T = B * S
    l = 0
    ssm_width = w_glu.shape[-1]
    n_groups = ssm_width // SSM_GROUP
    n_state = n_groups * SSM_STATE
    mla_width = MLA_HEADS * V_DIM
    qk_pad = MLA_HEADS * HEAD_PAD
    cparams = functools.partial(pltpu.CompilerParams, vmem_limit_bytes=VMEM_LIMIT)

    s1, s2, s3 = Q_RANK, Q_RANK + KV_RANK, Q_RANK + KV_RANK + QK_ROPE
    wi = w_in[l]
    w_kr = wi[:, s2:s3]
    pad_rope = lambda w: jnp.pad(w, ((0, 0), (QK_NOPE, HEAD_PAD - QK_NOPE - QK_ROPE)))
    w1 = jnp.concatenate([wi[:, :s2], wi[:, s3:], pad_rope(w_kr), pad_rope(_half_rotate(w_kr))],
                         axis=1).astype(BF16)
    wq = w_uq[l]
    zeros_nope = jnp.zeros(wq.shape[:2] + (QK_NOPE,), wq.dtype)
    wq_main = _pad_heads(wq, QK_NOPE + QK_ROPE).astype(BF16)
    wq_rot = _pad_heads(jnp.concatenate([zeros_nope, _half_rotate(wq[..., QK_NOPE:])], axis=-1),
                        QK_NOPE + QK_ROPE).astype(BF16)
    wkv = w_ukv[l]
    wk = _pad_heads(wkv[..., :QK_NOPE], QK_NOPE).astype(BF16)
    wv = _pad_heads(wkv[..., QK_NOPE:], V_DIM).astype(BF16)
    half = QK_ROPE // 2
    inv_freq = ROPE_THETA ** (-jnp.arange(half, dtype=F32) / half)
    freq = jnp.pad(jnp.concatenate([inv_freq, inv_freq]),
                   (QK_NOPE, HEAD_PAD - QK_NOPE - QK_ROPE)).reshape(1, HEAD_PAD)
    pos_f = positions.astype(F32).reshape(T, 1)

    tm = min(512, T)
    w1_cols = w1.shape[1]
    tok = lambda width: pl.BlockSpec((tm, width), lambda i: (i, 0))
    xn, q, k, v, u = pl.pallas_call(
        functools.partial(_inproj_kernel, ssm_width=ssm_width),
        grid=(T // tm,),
        in_specs=[tok(D), tok(1), _const_spec((1, D)), _const_spec((1, D)),
                  _const_spec((D, w1_cols)), _const_spec((1, Q_RANK)), _const_spec((1, KV_RANK)),
                  _const_spec((Q_RANK, qk_pad)), _const_spec((Q_RANK, qk_pad)),
                  _const_spec((KV_RANK, qk_pad)), _const_spec((KV_RANK, qk_pad)),
                  _const_spec((1, HEAD_PAD))],
        out_specs=[tok(D), tok(qk_pad), tok(qk_pad), tok(qk_pad), tok(ssm_width)],
        out_shape=[jax.ShapeDtypeStruct((T, D), F32), jax.ShapeDtypeStruct((T, qk_pad), BF16),
                   jax.ShapeDtypeStruct((T, qk_pad), BF16), jax.ShapeDtypeStruct((T, qk_pad), BF16),
                   jax.ShapeDtypeStruct((T, ssm_width), F32)],
        compiler_params=cparams(dimension_semantics=("parallel",)),
        name="inproj",
    )(x.reshape(T, D), pos_f, _row(ln_in_g), _row(ln_in_b), w1, _row(q_norm_g[l]), _row(kv_norm_g[l]),
      wq_main, wq_rot, wk, wv, freq)

    tq = min(256, S)
    nq = S // tq
    att = pl.pallas_call(
        functools.partial(_attn_kernel, tq=tq, heads=MLA_HEADS),
        grid=(B, nq),
        in_specs=[pl.BlockSpec((tq, qk_pad), lambda b, i: (b * nq + i, 0)),
                  pl.BlockSpec((S, qk_pad), lambda b, i: (b, 0)),
                  pl.BlockSpec((S, qk_pad), lambda b, i: (b, 0))],
        out_specs=pl.BlockSpec((tq, mla_width), lambda b, i: (b * nq + i, 0)),
        out_shape=jax.ShapeDtypeStruct((T, mla_width), BF16),
        compiler_params=cparams(dimension_semantics=("parallel", "arbitrary")),
        name="attention",
    )(q, k, v)

    lam = lax.complex(jnp.minimum(lambda_re[l].astype(F32), -1e-4), lambda_im[l].astype(F32))
    step = jnp.exp(log_step[l].astype(F32))[:, None]
    lam_bar = jnp.exp(lam * step)
    b_bar = ((lam_bar - 1.0) / lam)[..., None] * lax.complex(b_re[l].astype(F32), b_im[l].astype(F32))
    n_slab = ssm_width // LANES
    g_per_slab = n_groups // n_slab
    eye = jnp.eye(n_groups, dtype=F32)

    def expand_in(bpart):
        return jnp.einsum('gpc,gh->gchp', bpart, eye).reshape(ssm_width, n_state)

    def expand_out(cpart):
        return jnp.einsum('gcp,gh->gphc', cpart, eye).reshape(n_state, ssm_width)

    slab = n_state // n_slab
    win_re, win_im = expand_in(jnp.real(b_bar)), expand_in(jnp.imag(b_bar))
    win = jnp.stack([jnp.concatenate([win_re[j * LANES:(j + 1) * LANES, j * slab:(j + 1) * slab],
                                      win_im[j * LANES:(j + 1) * LANES, j * slab:(j + 1) * slab]], axis=1)
                     for j in range(n_slab)]).astype(BF16)
    wc_re, wc_im = expand_out(c_re[l].astype(F32)), expand_out(-c_im[l].astype(F32))
    cre = jnp.stack([wc_re[j * slab:(j + 1) * slab, j * LANES:(j + 1) * LANES] for j in range(n_slab)]).astype(BF16)
    cim = jnp.stack([wc_im[j * slab:(j + 1) * slab, j * LANES:(j + 1) * LANES] for j in range(n_slab)]).astype(BF16)
    a_re = jnp.real(lam_bar).reshape(1, n_state)
    a_im = jnp.imag(lam_bar).reshape(1, n_state)

    lt = min(64, S)
    ssm = pl.pallas_call(
        functools.partial(_s5_kernel, batch=B, lt=lt, n_state=n_state, scan_tiles=4),
        grid=(S // lt,),
        in_specs=[pl.BlockSpec((B, lt, ssm_width), lambda t: (0, t, 0)),
                  _const_spec(win.shape), _const_spec((1, n_state)), _const_spec((1, n_state)),
                  _const_spec(cre.shape), _const_spec(cim.shape), _const_spec((1, ssm_width)),
                  _const_spec((ssm_width, ssm_width)), _const_spec((1, ssm_width))],
        out_specs=pl.BlockSpec((B, lt, ssm_width), lambda t: (0, t, 0)),
        out_shape=jax.ShapeDtypeStruct((B, S, ssm_width), BF16),
        scratch_shapes=[pltpu.VMEM((2 * n_state // LANES, B * lt, LANES), F32),
                        pltpu.VMEM((n_state // LANES, B, LANES), F32),
                        pltpu.VMEM((n_state // LANES, B, LANES), F32),
                        pltpu.VMEM((ssm_width // LANES, B * lt, LANES), F32),
                        pltpu.VMEM((B * lt, ssm_width), F32)],
        compiler_params=cparams(dimension_semantics=("arbitrary",)),
        name="s5",
    )(u.reshape(B, S, ssm_width), win, a_re, a_im, cre, cim, _row(d_skip[l]),
      w_glu[l].astype(BF16), _row(b_glu[l]))

    wo = w_o[l].astype(BF16)
    wr_t = w_router[l].T.astype(F32)
    wr_hi = wr_t.astype(BF16)
    wr_lo = (wr_t - wr_hi.astype(F32)).astype(BF16)
    half = D // 2
    kt = lambda dt: jax.ShapeDtypeStruct((TOP_K, T), dt)
    k_spec = pl.BlockSpec((TOP_K, tm), lambda i: (0, i))
    x1, x1p, idx_k, gate_k, rank_k, counts = pl.pallas_call(
        _mix_kernel,
        grid=(T // tm,),
        in_specs=[tok(mla_width), tok(ssm_width), tok(D), _const_spec((1, mla_width)),
                  _const_spec((1, ssm_width)), _const_spec((mla_width, D)), _const_spec((ssm_width, D)),
                  _const_spec((1, D)), _const_spec((1, D)), _const_spec((N_EXPERTS, D)),
                  _const_spec((N_EXPERTS, D)), _const_spec((N_EXPERTS, 1))],
        out_specs=[tok(D), tok(half), k_spec, k_spec, k_spec, _const_spec((N_EXPERTS, LANES))],
        out_shape=[jax.ShapeDtypeStruct((T, D), F32), jax.ShapeDtypeStruct((T, half), jnp.int32),
                   kt(jnp.int32), kt(F32), kt(jnp.int32),
                   jax.ShapeDtypeStruct((N_EXPERTS, LANES), F32)],
        scratch_shapes=[pltpu.VMEM((N_EXPERTS, 1), F32)],
        compiler_params=cparams(dimension_semantics=("arbitrary",)),
        name="mix_router",
    )(att, ssm.reshape(T, ssm_width), xn, _row(attn_out_g[l]), _row(ssm_out_g[l]),
      wo[:mla_width], wo[mla_width:], _row(ln1_g[l]), _row(ln1_b[l]), wr_hi, wr_lo,
      router_bias[l].astype(F32).reshape(N_EXPERTS, 1))

    tr = EXPERT_ROW_TILE
    n_tiles = (T * TOP_K) // tr + N_EXPERTS
    n_rows = n_tiles * tr
    cnt = counts[:, 0].astype(jnp.int32)
    tiles_e = (cnt + tr - 1) // tr
    tile_end = jnp.cumsum(tiles_e)
    tile_start = tile_end - tiles_e
    row_off = tile_start * tr
    tile_ids = jnp.arange(n_tiles, dtype=jnp.int32)
    tile_expert = jnp.minimum(jnp.searchsorted(tile_end, tile_ids, side='right'),
                              N_EXPERTS - 1).astype(jnp.int32)
    tile_valid = jnp.clip(cnt[tile_expert] - (tile_ids - tile_start[tile_expert]) * tr, 0, tr)
    tile_valid = tile_valid.astype(jnp.int32)

    tp = min(2048, T)
    pos = pl.pallas_call(
        _positions_kernel,
        grid_spec=pltpu.PrefetchScalarGridSpec(
            num_scalar_prefetch=1, grid=(T // tp,),
            in_specs=[pl.BlockSpec((TOP_K, tp), lambda i, off: (0, i)),
                      pl.BlockSpec((TOP_K, tp), lambda i, off: (0, i))],
            out_specs=pl.BlockSpec((TOP_K, tp), lambda i, off: (0, i))),
        out_shape=kt(jnp.int32),
        name="positions",
    )(row_off.astype(jnp.int32), idx_k, rank_k)

    xs = _sc_scatter_rows(x1p, pos, n_rows)

    wg = w_gate[l].astype(BF16)
    wu = w_up[l].astype(BF16)
    wd = w_down[l].astype(BF16)
    ff = wg.shape[-1]
    ys = pl.pallas_call(
        _experts_kernel,
        grid_spec=pltpu.PrefetchScalarGridSpec(
            num_scalar_prefetch=2, grid=(n_tiles,),
            in_specs=[pl.BlockSpec((tr, half), lambda i, te, tv: (i, 0)),
                      pl.BlockSpec((1, D, ff), lambda i, te, tv: (te[i], 0, 0)),
                      pl.BlockSpec((1, D, ff), lambda i, te, tv: (te[i], 0, 0)),
                      pl.BlockSpec((1, ff, D), lambda i, te, tv: (te[i], 0, 0))],
            out_specs=pl.BlockSpec((tr, half), lambda i, te, tv: (i, 0))),
        out_shape=jax.ShapeDtypeStruct((n_rows, half), jnp.int32),
        compiler_params=cparams(dimension_semantics=("arbitrary",)),
        name="experts",
    )(tile_expert, tile_valid, xs, wg, wu, wd)

    yg = _sc_gather_rows(ys, pos.reshape(TOP_K * T))

    out = pl.pallas_call(
        _combine_kernel,
        grid=(T // tm,),
        in_specs=[pl.BlockSpec((TOP_K, tm, half), lambda i: (0, i, 0)),
                  pl.BlockSpec((tm, TOP_K), lambda i: (i, 0)), tok(D),
                  _const_spec((D, ff)), _const_spec((D, ff)), _const_spec((ff, D)),
                  _const_spec((1, D)), _const_spec((1, D))],
        out_specs=tok(D),
        out_shape=jax.ShapeDtypeStruct((T, D), F32),
        compiler_params=cparams(dimension_semantics=("parallel",)),
        name="combine",
    )(yg.reshape(TOP_K, T, half), gate_k.T, x1, ws_gate[l].astype(BF16), ws_up[l].astype(BF16),
      ws_down[l].astype(BF16), _row(ln2_g[l]), _row(ln2_b[l]))
    return out.reshape(B, S, D)
```

```python
import functools

import jax
import jax.numpy as jnp
from jax import lax
from jax.experimental import pallas as pl
from jax.experimental.pallas import tpu as pltpu
from jax.experimental.pallas import tpu_sc as plsc

CHUNK = 64
MLA_HEADS = 8
QK_NOPE = 64
QK_ROPE = 32
V_DIM = 64
Q_RANK = 256
KV_RANK = 128
ROPE_THETA = 10000.0
SSM_GROUP = 16
SSM_STATE = 64
N_EXPERTS = 64
TOP_K = 8
N_GROUPS = 8
TOP_GROUPS = 4
ROUTED_SCALE = 2.5
DEPTH = 1
ALPHA = (2.0 * DEPTH) ** 0.25
EPS = 1e-5
LOG2_E = 1.4426950408889634

LANES = 128
HEAD_PAD = LANES
VMEM_LIMIT = 56 * 1024 * 1024
EXPERT_ROW_TILE = 512

BF16 = jnp.bfloat16
F32 = jnp.float32
NT_DIMS = (((1,), (1,)), ((), ()))


def _dot(a, b):
    return jnp.dot(a, b, preferred_element_type=F32)


def _layer_norm(x, g, b):
    mu = jnp.mean(x, axis=-1, keepdims=True)
    xc = x - mu
    var = jnp.mean(xc * xc, axis=-1, keepdims=True)
    return xc * lax.rsqrt(var + EPS) * g + b


def _rms_norm(x, g):
    return x * lax.rsqrt(jnp.mean(x * x, axis=-1, keepdims=True) + EPS) * g


def _inproj_kernel(x_ref, pos_ref, lng_ref, lnb_ref, w1_ref, qg_ref, kvg_ref,
                   wq_ref, wqr_ref, wk_ref, wv_ref, freq_ref,
                   xn_ref, q_ref, k_ref, v_ref, u_ref, *, ssm_width):
    xn = _layer_norm(x_ref[...], lng_ref[...], lnb_ref[...])
    xn_ref[...] = xn
    h = _dot(xn.astype(BF16), w1_ref[...])
    o1 = Q_RANK
    o2 = o1 + KV_RANK
    o3 = o2 + ssm_width
    o4 = o3 + HEAD_PAD
    cq = h[:, :o1]
    ckv = h[:, o1:o2]
    u_ref[...] = h[:, o2:o3]
    kr_raw = h[:, o3:o4]
    kr_rot = h[:, o4:o4 + HEAD_PAD]
    cqn = _rms_norm(cq, qg_ref[...]).astype(BF16)
    ckvn = _rms_norm(ckv, kvg_ref[...]).astype(BF16)

    ang = pos_ref[...] * freq_ref[...]
    c = jnp.cos(ang)
    s = jnp.sin(ang)
    lane = lax.broadcasted_iota(jnp.int32, (1, HEAD_PAD), 1)
    scale = (QK_NOPE + QK_ROPE) ** -0.5 * LOG2_E
    is_rope = (lane >= QK_NOPE) & (lane < QK_NOPE + QK_ROPE)
    cos1 = jnp.where(lane < QK_NOPE, 1.0, jnp.where(is_rope, c, 0.0)) * scale
    sin1 = jnp.where(is_rope, s, 0.0) * scale
    cos_t = jnp.concatenate([cos1] * MLA_HEADS, axis=1)
    sin_t = jnp.concatenate([sin1] * MLA_HEADS, axis=1)
    q = _dot(cqn, wq_ref[...]) * cos_t + _dot(cqn, wqr_ref[...]) * sin_t
    q_ref[...] = q.astype(BF16)

    kr = kr_raw * c + kr_rot * s
    k = _dot(ckvn, wk_ref[...]) + jnp.concatenate([kr] * MLA_HEADS, axis=1)
    k_ref[...] = k.astype(BF16)
    ones_col = jnp.concatenate([jnp.where(lane == V_DIM, 1.0, 0.0)] * MLA_HEADS, axis=1)
    v_ref[...] = (_dot(ckvn, wv_ref[...]) + ones_col).astype(BF16)


def _attn_kernel(q_ref, k_ref, v_ref, o_ref, *, tq, heads):
    qi = pl.program_id(1)
    row_chunk = lax.broadcasted_iota(jnp.int32, (tq, tq), 0) // CHUNK
    col_chunk = lax.broadcasted_iota(jnp.int32, (tq, tq), 1) // CHUNK
    diag_mask = row_chunk >= col_chunk

    def block(j, carry, masked):
        start = pl.multiple_of(j * tq, tq)
        new = []
        for h in range(heads):
            m, acc = carry[2 * h], carry[2 * h + 1]
            cols = slice(h * HEAD_PAD, (h + 1) * HEAD_PAD)
            s = lax.dot_general(q_ref[:, cols], k_ref[pl.ds(start, tq), cols], NT_DIMS,
                                preferred_element_type=F32)
            if masked:
                s = jnp.where(diag_mask, s, -jnp.inf)
            m_new = jnp.maximum(m, jnp.max(s, axis=-1, keepdims=True))
            p = jnp.exp2(s - m_new).astype(BF16)
            acc = jnp.exp2(m - m_new) * acc + _dot(p, v_ref[pl.ds(start, tq), cols])
            new += [m_new, acc]
        return tuple(new)

    init = (jnp.full((tq, 1), -jnp.inf, F32), jnp.zeros((tq, HEAD_PAD), F32)) * heads
    carry = lax.fori_loop(0, qi, functools.partial(block, masked=False), init)
    fin = block(qi, carry, True)
    outs = [fin[2 * h + 1][:, :V_DIM] / fin[2 * h + 1][:, V_DIM:V_DIM + 1] for h in range(heads)]
    o_ref[...] = jnp.concatenate(outs, axis=1).astype(o_ref.dtype)


def _s5_kernel(u_ref, win_ref, are_ref, aim_ref, cre_ref, cim_ref, dskip_ref,
               wglu_ref, bglu_ref, o_ref, vx_ref, hre_ref, him_ref, io_ref, utm_ref, *,
               batch, lt, n_state, scan_tiles):
    ti = pl.program_id(0)

    @pl.when(ti == 0)
    def _():
        hre_ref[...] = jnp.zeros_like(hre_ref)
        him_ref[...] = jnp.zeros_like(him_ref)

    width = u_ref.shape[-1]
    n_slab = width // LANES
    n_tiles = n_state // LANES
    slab_tiles = n_tiles // n_slab
    for b in range(batch):
        for c in range(n_slab):
            io_ref[c, b * lt:(b + 1) * lt, :] = u_ref[b, :, c * LANES:(c + 1) * LANES]

    def to_time_major(t, _):
        dst = pl.ds(pl.multiple_of(t * batch, batch), batch)
        for c in range(n_slab):
            utm_ref[dst, c * LANES:(c + 1) * LANES] = io_ref[c, pl.ds(t, batch, stride=lt), :]
        return 0

    lax.fori_loop(0, lt, to_time_major, 0, unroll=4)
    u2 = utm_ref[...]
    ub = u2.astype(BF16)
    for j in range(n_slab):
        vj = _dot(ub[:, j * LANES:(j + 1) * LANES], win_ref[j])
        for i in range(slab_tiles):
            vx_ref[j * slab_tiles + i] = vj[:, i * LANES:(i + 1) * LANES]
            vx_ref[n_tiles + j * slab_tiles + i] = vj[:, (slab_tiles + i) * LANES:(slab_tiles + i + 1) * LANES]

    for c0 in range(0, n_tiles, scan_tiles):
        tiles = range(c0, c0 + scan_tiles)
        ar = [jnp.broadcast_to(are_ref[:, c * LANES:(c + 1) * LANES], (batch, LANES)) for c in tiles]
        ai = [jnp.broadcast_to(aim_ref[:, c * LANES:(c + 1) * LANES], (batch, LANES)) for c in tiles]

        def step(t, carry, tiles=tiles, ar=ar, ai=ai):
            rows = pl.ds(pl.multiple_of(t * batch, batch), batch)
            out = []
            for n, c in enumerate(tiles):
                hr, hi = carry[2 * n], carry[2 * n + 1]
                nr = ar[n] * hr - ai[n] * hi + vx_ref[c, rows, :]
                ni = ar[n] * hi + ai[n] * hr + vx_ref[n_tiles + c, rows, :]
                vx_ref[c, rows, :] = nr
                vx_ref[n_tiles + c, rows, :] = ni
                out += [nr, ni]
            return tuple(out)

        init = []
        for c in tiles:
            init += [hre_ref[c], him_ref[c]]
        fin = lax.fori_loop(0, lt, step, tuple(init), unroll=4)
        for n, c in enumerate(tiles):
            hre_ref[c] = fin[2 * n]
            him_ref[c] = fin[2 * n + 1]

    ys = []
    for j in range(n_slab):
        xr = jnp.concatenate([vx_ref[j * slab_tiles + i].astype(BF16) for i in range(slab_tiles)], axis=1)
        xi = jnp.concatenate([vx_ref[n_tiles + j * slab_tiles + i].astype(BF16) for i in range(slab_tiles)], axis=1)
        ys.append(_dot(xr, cre_ref[j]) + _dot(xi, cim_ref[j]))
    y = jnp.concatenate(ys, axis=1) + dskip_ref[...] * u2
    y = jax.nn.gelu(y)
    z = _dot(y.astype(BF16), wglu_ref[...]) + bglu_ref[...]
    out = y * jax.nn.sigmoid(z)
    for c in range(n_slab):
        io_ref[c] = out[:, c * LANES:(c + 1) * LANES]
    for b in range(batch):
        for c in range(n_slab):
            o_ref[b, :, c * LANES:(c + 1) * LANES] = io_ref[c, pl.ds(b, lt, stride=batch), :].astype(o_ref.dtype)


def _router_gates(logits_t, rbias):
    n_exp, tm = logits_t.shape
    per_group = n_exp // N_GROUPS
    scores = jax.nn.sigmoid(logits_t)
    sel = scores + rbias
    neg_inf = -jnp.inf
    sub_iota = lax.broadcasted_iota(jnp.int32, (per_group, tm), 0).astype(F32)
    group_score = []
    for g in range(N_GROUPS):
        sg = sel[g * per_group:(g + 1) * per_group, :]
        m1 = jnp.max(sg, axis=0, keepdims=True)
        first = jnp.min(jnp.where(sg == m1, sub_iota, float(per_group)), axis=0, keepdims=True)
        m2 = jnp.max(jnp.where(sub_iota == first, neg_inf, sg), axis=0, keepdims=True)
        group_score.append(m1 + m2)
    masked = []
    for g in range(N_GROUPS):
        rank = jnp.zeros((1, tm), F32)
        for g2 in range(N_GROUPS):
            if g2 == g:
                continue
            ahead = (group_score[g2] >= group_score[g]) if g2 < g else (group_score[g2] > group_score[g])
            rank = rank + jnp.where(ahead, 1.0, 0.0)
        keep = rank < float(TOP_GROUPS)
        masked.append(jnp.where(keep, sel[g * per_group:(g + 1) * per_group, :], neg_inf))
    cur = jnp.concatenate(masked, axis=0)
    iota = lax.broadcasted_iota(jnp.int32, (n_exp, tm), 0).astype(F32)
    chosen = jnp.zeros((n_exp, tm), F32)
    picks, weights = [], []
    for _ in range(TOP_K):
        m = jnp.max(cur, axis=0, keepdims=True)
        idx = jnp.min(jnp.where(cur == m, iota, float(n_exp)), axis=0, keepdims=True)
        pick = iota == idx
        chosen = jnp.where(pick, 1.0, chosen)
        cur = jnp.where(pick, neg_inf, cur)
        picks.append(idx)
        weights.append(jnp.sum(jnp.where(pick, scores, 0.0), axis=0, keepdims=True))
    idx_k = jnp.concatenate(picks, axis=0)
    w_k = jnp.concatenate(weights, axis=0)
    gate_k = w_k / jnp.sum(w_k, axis=0, keepdims=True) * ROUTED_SCALE
    return idx_k, gate_k, chosen


def _pack_bf16_pairs(x):
    n = x.shape[1] // 2
    hi = lax.bitcast_convert_type(x[:, :n].astype(BF16).astype(F32), jnp.int32)
    lo = lax.bitcast_convert_type(x[:, n:].astype(BF16).astype(F32), jnp.int32)
    return hi | lax.shift_right_logical(lo, 16)


def _unpack_bf16_pairs(p):
    hi = lax.bitcast_convert_type(p & jnp.int32(-65536), F32).astype(BF16)
    lo = lax.bitcast_convert_type(lax.shift_left(p, 16), F32).astype(BF16)
    return hi, lo


def _mix_kernel(att_ref, ssm_ref, xn_ref, ag_ref, sg_ref, woa_ref, wos_ref,
                g_ref, b_ref, wrh_ref, wrl_ref, rb_ref,
                x1_ref, x1p_ref, idx_ref, gate_ref, rank_ref, cnt_ref, carry_ref):
    @pl.when(pl.program_id(0) == 0)
    def _():
        carry_ref[...] = jnp.zeros_like(carry_ref)

    an = _rms_norm(att_ref[...].astype(F32), ag_ref[...]).astype(BF16)
    sn = _rms_norm(ssm_ref[...].astype(F32), sg_ref[...]).astype(BF16)
    mix = _dot(an, woa_ref[...]) + _dot(sn, wos_ref[...])
    x1 = _layer_norm(ALPHA * xn_ref[...] + mix, g_ref[...], b_ref[...])
    x1_ref[...] = x1
    x1p_ref[...] = _pack_bf16_pairs(x1)
    x_hi = x1.astype(BF16)
    x_lo = (x1 - x_hi.astype(F32)).astype(BF16)
    dg = functools.partial(lax.dot_general, dimension_numbers=NT_DIMS, preferred_element_type=F32)
    logits_t = dg(wrh_ref[...], x_hi) + dg(wrl_ref[...], x_hi) + dg(wrh_ref[...], x_lo)
    idx_k, gate_k, chosen = _router_gates(logits_t, rb_ref[...])
    idx_ref[...] = idx_k.astype(jnp.int32)
    gate_ref[...] = gate_k

    n_exp, tm = chosen.shape
    before = (lax.broadcasted_iota(jnp.int32, (tm, tm), 0)
              < lax.broadcasted_iota(jnp.int32, (tm, tm), 1))
    excl = _dot(chosen.astype(BF16), jnp.where(before, 1.0, 0.0).astype(BF16))
    rank_full = carry_ref[...] + excl
    iota = lax.broadcasted_iota(jnp.int32, (n_exp, tm), 0).astype(F32)
    ranks = [jnp.sum(jnp.where(iota == idx_k[k:k + 1, :], rank_full, 0.0), axis=0, keepdims=True)
             for k in range(TOP_K)]
    rank_ref[...] = jnp.concatenate(ranks, axis=0).astype(jnp.int32)
    total = carry_ref[...] + jnp.sum(chosen, axis=1, keepdims=True)
    carry_ref[...] = total
    cnt_ref[...] = jnp.broadcast_to(total, cnt_ref.shape)


def _positions_kernel(off_ref, idx_ref, rank_ref, pos_ref):
    idx = idx_ref[...]
    base = jnp.zeros(idx.shape, jnp.int32)
    for e in range(N_EXPERTS):
        base = jnp.where(idx == e, off_ref[e], base)
    pos_ref[...] = rank_ref[...] + base


def _swiglu(x_hi, x_lo, wg_ref, wu_ref, wd_ref):
    half = x_hi.shape[1]
    hg = _dot(x_hi, wg_ref[:half, :]) + _dot(x_lo, wg_ref[half:, :])
    hu = _dot(x_hi, wu_ref[:half, :]) + _dot(x_lo, wu_ref[half:, :])
    h = jax.nn.silu(hg) * hu
    return _dot(h.astype(BF16), wd_ref[...])


def _experts_kernel(te_ref, valid_ref, xs_ref, wg_ref, wu_ref, wd_ref, ys_ref,
                    wgb_ref, wub_ref, wdb_ref):
    i = pl.program_id(0)
    valid = valid_ref[i]

    @pl.when((i == 0) | (te_ref[i] != te_ref[jnp.maximum(i - 1, 0)]))
    def _():
        wgb_ref[...] = wg_ref[0].astype(BF16)
        wub_ref[...] = wu_ref[0].astype(BF16)
        wdb_ref[...] = wd_ref[0].astype(BF16)

    @pl.when(valid > 0)
    def _():
        rows = lax.broadcasted_iota(jnp.int32, xs_ref.shape, 0)
        x_hi, x_lo = _unpack_bf16_pairs(jnp.where(rows < valid, xs_ref[...], 0))
        ys_ref[...] = _pack_bf16_pairs(_swiglu(x_hi, x_lo, wgb_ref, wub_ref, wdb_ref))

    @pl.when(valid == 0)
    def _():
        ys_ref[...] = jnp.zeros_like(ys_ref)


def _combine_kernel(yg_ref, gate_ref, x1_ref, wsg_ref, wsu_ref, wsd_ref, g_ref, b_ref, o_ref):
    x1 = x1_ref[...]
    half = x1.shape[1] // 2
    acc = _swiglu(x1[:, :half].astype(BF16), x1[:, half:].astype(BF16), wsg_ref, wsu_ref, wsd_ref)
    gates = gate_ref[...]
    acc_hi, acc_lo = acc[:, :half], acc[:, half:]
    for k in range(TOP_K):
        y_hi, y_lo = _unpack_bf16_pairs(yg_ref[k])
        gk = gates[:, k:k + 1]
        acc_hi = acc_hi + gk * y_hi.astype(F32)
        acc_lo = acc_lo + gk * y_lo.astype(F32)
    ffn = jnp.concatenate([acc_hi, acc_lo], axis=1)
    o_ref[...] = _layer_norm(ALPHA * x1 + ffn, g_ref[...], b_ref[...])


SC_CORES = 2
SC_SUBCORES = 16
SC_WORKERS = SC_CORES * SC_SUBCORES
SC_CHUNK = 64


def _sc_mesh():
    return plsc.VectorSubcoreMesh(core_axis_name="c", subcore_axis_name="s")


def _sc_gather_rows(table, idx):
    n = idx.shape[0]
    d = table.shape[1]
    per_w = n // SC_WORKERS
    n_ch = per_w // SC_CHUNK

    @functools.partial(
        pl.kernel, mesh=_sc_mesh(),
        out_type=jax.ShapeDtypeStruct((n, d), table.dtype),
        scratch_types=[pltpu.VMEM((n_ch, SC_CHUNK), jnp.int32),
                       pltpu.VMEM((2, SC_CHUNK, d), table.dtype),
                       pltpu.SemaphoreType.DMA((2,)),
                       pltpu.SemaphoreType.DMA((2,))],
    )
    def k(table_hbm, idx_hbm, out_hbm, idx_v, buf, gsem, osem):
        wid = lax.axis_index("s") * SC_CORES + lax.axis_index("c")
        base = wid * per_w
        pltpu.sync_copy(idx_hbm.at[wid], idx_v)

        def gather(c, b):
            return pltpu.make_async_copy(table_hbm.at[idx_v.at[c]], buf.at[b], gsem.at[b])

        def put(c, b):
            return pltpu.make_async_copy(buf.at[b], out_hbm.at[pl.ds(base + c * SC_CHUNK, SC_CHUNK)],
                                         osem.at[b])

        gather(0, 0).start()

        @pl.loop(0, n_ch, step=2)
        def _(c):
            for b in range(2):
                cc = c + b
                gather(cc, b).wait()

                @pl.when(cc + 1 < n_ch)
                def _():
                    @pl.when(cc >= 1)
                    def _():
                        put(cc - 1, 1 - b).wait()
                    gather(cc + 1, 1 - b).start()

                put(cc, b).start()

        put(n_ch - 2, 0).wait()
        put(n_ch - 1, 1).wait()

    return k(table, idx.reshape(SC_WORKERS, n_ch, SC_CHUNK))


def _sc_scatter_rows(x, pos, n_out):
    t, d = x.shape
    kk = pos.shape[0]
    per_w = t // SC_WORKERS
    n_ch = per_w // SC_CHUNK
    pos_w = pos.reshape(kk, SC_WORKERS, n_ch, SC_CHUNK).transpose(1, 2, 0, 3)
    pos_w = pos_w.reshape(SC_WORKERS, n_ch * kk, SC_CHUNK)

    @functools.partial(
        pl.kernel, mesh=_sc_mesh(),
        out_type=jax.ShapeDtypeStruct((n_out, d), x.dtype),
        scratch_types=[pltpu.VMEM((n_ch * kk, SC_CHUNK), jnp.int32),
                       pltpu.VMEM((2, SC_CHUNK, d), x.dtype),
                       pltpu.SemaphoreType.DMA((2,)),
                       pltpu.SemaphoreType.DMA((2,))],
    )
    def k(x_hbm, pos_hbm, out_hbm, idx_v, buf, isem, osem):
        wid = lax.axis_index("s") * SC_CORES + lax.axis_index("c")
        base = wid * per_w
        pltpu.sync_copy(pos_hbm.at[wid], idx_v)

        def get(c, b):
            return pltpu.make_async_copy(x_hbm.at[pl.ds(base + c * SC_CHUNK, SC_CHUNK)], buf.at[b],
                                         isem.at[b])

        def put(c, j, b):
            return pltpu.make_async_copy(buf.at[b], out_hbm.at[idx_v.at[c * kk + j]], osem.at[b])

        get(0, 0).start()

        @pl.loop(0, n_ch, step=2)
        def _(c):
            for b in range(2):
                cc = c + b
                get(cc, b).wait()

                @pl.when(cc + 1 < n_ch)
                def _():
                    @pl.when(cc >= 1)
                    def _():
                        for j in range(kk):
                            put(cc - 1, j, 1 - b).wait()
                    get(cc + 1, 1 - b).start()

                for j in range(kk):
                    put(cc, j, b).start()

        for j in range(kk):
            put(n_ch - 2, j, 0).wait()
        for j in range(kk):
            put(n_ch - 1, j, 1).wait()

    return k(x, pos_w)


def _row(v):
    return v.reshape(1, -1).astype(F32)


def _const_spec(shape):
    nd = len(shape)
    return pl.BlockSpec(shape, lambda *_: (0,) * nd)


def _pad_heads(w, width):
    r, h, _ = w.shape
    return jnp.pad(w, ((0, 0), (0, 0), (0, HEAD_PAD - width))).reshape(r, h * HEAD_PAD)


def _half_rotate(w):
    half = QK_ROPE // 2
    return jnp.concatenate([-w[..., half:], w[..., :half]], axis=-1)


def kernel(x, positions, ln_in_g, ln_in_b, w_in, q_norm_g, kv_norm_g, w_uq, w_ukv, lambda_re, lambda_im, log_step, b_re, b_im, c_re, c_im, d_skip, w_glu, b_glu, attn_out_g, ssm_out_g, w_o, ln1_g, ln1_b, w_router, router_bias, w_gate, w_up, w_down, ws_gate, ws_up, ws_down, ln2_g, ln2_b):
    B, S, D = x.shape
    T = B * S
    l = 0
    ssm_width = w_glu.shape[-1]
    n_groups = ssm_width // SSM_GROUP
    n_state = n_groups * SSM_STATE
    mla_width = MLA_HEADS * V_DIM
    qk_pad = MLA_HEADS * HEAD_PAD
    cparams = functools.partial(pltpu.CompilerParams, vmem_limit_bytes=VMEM_LIMIT)

    s1, s2, s3 = Q_RANK, Q_RANK + KV_RANK, Q_RANK + KV_RANK + QK_ROPE
    wi = w_in[l]
    w_kr = wi[:, s2:s3]
    pad_rope = lambda w: jnp.pad(w, ((0, 0), (QK_NOPE, HEAD_PAD - QK_NOPE - QK_ROPE)))
    w1 = jnp.concatenate([wi[:, :s2], wi[:, s3:], pad_rope(w_kr), pad_rope(_half_rotate(w_kr))],
                         axis=1).astype(BF16)
    wq = w_uq[l]
    zeros_nope = jnp.zeros(wq.shape[:2] + (QK_NOPE,), wq.dtype)
    wq_main = _pad_heads(wq, QK_NOPE + QK_ROPE).astype(BF16)
    wq_rot = _pad_heads(jnp.concatenate([zeros_nope, _half_rotate(wq[..., QK_NOPE:])], axis=-1),
                        QK_NOPE + QK_ROPE).astype(BF16)
    wkv = w_ukv[l]
    wk = _pad_heads(wkv[..., :QK_NOPE], QK_NOPE).astype(BF16)
    wv = _pad_heads(wkv[..., QK_NOPE:], V_DIM).astype(BF16)
    half = QK_ROPE // 2
    inv_freq = ROPE_THETA ** (-jnp.arange(half, dtype=F32) / half)
    freq = jnp.pad(jnp.concatenate([inv_freq, inv_freq]),
                   (QK_NOPE, HEAD_PAD - QK_NOPE - QK_ROPE)).reshape(1, HEAD_PAD)
    pos_f = positions.astype(F32).reshape(T, 1)

    tm = min(512, T)
    w1_cols = w1.shape[1]
    tok = lambda width: pl.BlockSpec((tm, width), lambda i: (i, 0))
    xn, q, k, v, u = pl.pallas_call(
        functools.partial(_inproj_kernel, ssm_width=ssm_width),
        grid=(T // tm,),
        in_specs=[tok(D), tok(1), _const_spec((1, D)), _const_spec((1, D)),
                  _const_spec((D, w1_cols)), _const_spec((1, Q_RANK)), _const_spec((1, KV_RANK)),
                  _const_spec((Q_RANK, qk_pad)), _const_spec((Q_RANK, qk_pad)),
                  _const_spec((KV_RANK, qk_pad)), _const_spec((KV_RANK, qk_pad)),
                  _const_spec((1, HEAD_PAD))],
        out_specs=[tok(D), tok(qk_pad), tok(qk_pad), tok(qk_pad), tok(ssm_width)],
        out_shape=[jax.ShapeDtypeStruct((T, D), F32), jax.ShapeDtypeStruct((T, qk_pad), BF16),
                   jax.ShapeDtypeStruct((T, qk_pad), BF16), jax.ShapeDtypeStruct((T, qk_pad), BF16),
                   jax.ShapeDtypeStruct((T, ssm_width), F32)],
        compiler_params=cparams(dimension_semantics=("parallel",)),
        name="inproj",
    )(x.reshape(T, D), pos_f, _row(ln_in_g), _row(ln_in_b), w1, _row(q_norm_g[l]), _row(kv_norm_g[l]),
      wq_main, wq_rot, wk, wv, freq)

    tq = min(256, S)
    nq = S // tq
    att = pl.pallas_call(
        functools.partial(_attn_kernel, tq=tq, heads=MLA_HEADS),
        grid=(B, nq),
        in_specs=[pl.BlockSpec((tq, qk_pad), lambda b, i: (b * nq + i, 0)),
                  pl.BlockSpec((S, qk_pad), lambda b, i: (b, 0)),
                  pl.BlockSpec((S, qk_pad), lambda b, i: (b, 0))],
        out_specs=pl.BlockSpec((tq, mla_width), lambda b, i: (b * nq + i, 0)),
        out_shape=jax.ShapeDtypeStruct((T, mla_width), BF16),
        compiler_params=cparams(dimension_semantics=("parallel", "arbitrary")),
        name="attention",
    )(q, k, v)

    lam = lax.complex(jnp.minimum(lambda_re[l].astype(F32), -1e-4), lambda_im[l].astype(F32))
    step = jnp.exp(log_step[l].astype(F32))[:, None]
    lam_bar = jnp.exp(lam * step)
    b_bar = ((lam_bar - 1.0) / lam)[..., None] * lax.complex(b_re[l].astype(F32), b_im[l].astype(F32))
    n_slab = ssm_width // LANES
    g_per_slab = n_groups // n_slab
    eye = jnp.eye(n_groups, dtype=F32)

    def expand_in(bpart):
        return jnp.einsum('gpc,gh->gchp', bpart, eye).reshape(ssm_width, n_state)

    def expand_out(cpart):
        return jnp.einsum('gcp,gh->gphc', cpart, eye).reshape(n_state, ssm_width)

    slab = n_state // n_slab
    win_re, win_im = expand_in(jnp.real(b_bar)), expand_in(jnp.imag(b_bar))
    win = jnp.stack([jnp.concatenate([win_re[j * LANES:(j + 1) * LANES, j * slab:(j + 1) * slab],
                                      win_im[j * LANES:(j + 1) * LANES, j * slab:(j + 1) * slab]], axis=1)
                     for j in range(n_slab)]).astype(BF16)
    wc_re, wc_im = expand_out(c_re[l].astype(F32)), expand_out(-c_im[l].astype(F32))
    cre = jnp.stack([wc_re[j * slab:(j + 1) * slab, j * LANES:(j + 1) * LANES] for j in range(n_slab)]).astype(BF16)
    cim = jnp.stack([wc_im[j * slab:(j + 1) * slab, j * LANES:(j + 1) * LANES] for j in range(n_slab)]).astype(BF16)
    a_re = jnp.real(lam_bar).reshape(1, n_state)
    a_im = jnp.imag(lam_bar).reshape(1, n_state)

    lt = min(64, S)
    ssm = pl.pallas_call(
        functools.partial(_s5_kernel, batch=B, lt=lt, n_state=n_state, scan_tiles=4),
        grid=(S // lt,),
        in_specs=[pl.BlockSpec((B, lt, ssm_width), lambda t: (0, t, 0)),
                  _const_spec(win.shape), _const_spec((1, n_state)), _const_spec((1, n_state)),
                  _const_spec(cre.shape), _const_spec(cim.shape), _const_spec((1, ssm_width)),
                  _const_spec((ssm_width, ssm_width)), _const_spec((1, ssm_width))],
        out_specs=pl.BlockSpec((B, lt, ssm_width), lambda t: (0, t, 0)),
        out_shape=jax.ShapeDtypeStruct((B, S, ssm_width), BF16),
        scratch_shapes=[pltpu.VMEM((2 * n_state // LANES, B * lt, LANES), F32),
                        pltpu.VMEM((n_state // LANES, B, LANES), F32),
                        pltpu.VMEM((n_state // LANES, B, LANES), F32),
                        pltpu.VMEM((ssm_width // LANES, B * lt, LANES), F32),
                        pltpu.VMEM((B * lt, ssm_width), F32)],
        compiler_params=cparams(dimension_semantics=("arbitrary",)),
        name="s5",
    )(u.reshape(B, S, ssm_width), win, a_re, a_im, cre, cim, _row(d_skip[l]),
      w_glu[l].astype(BF16), _row(b_glu[l]))

    wo = w_o[l].astype(BF16)
    wr_t = w_router[l].T.astype(F32)
    wr_hi = wr_t.astype(BF16)
    wr_lo = (wr_t - wr_hi.astype(F32)).astype(BF16)
    half = D // 2
    kt = lambda dt: jax.ShapeDtypeStruct((TOP_K, T), dt)
    k_spec = pl.BlockSpec((TOP_K, tm), lambda i: (0, i))
    x1, x1p, idx_k, gate_k, rank_k, counts = pl.pallas_call(
        _mix_kernel,
        grid=(T // tm,),
        in_specs=[tok(mla_width), tok(ssm_width), tok(D), _const_spec((1, mla_width)),
                  _const_spec((1, ssm_width)), _const_spec((mla_width, D)), _const_spec((ssm_width, D)),
                  _const_spec((1, D)), _const_spec((1, D)), _const_spec((N_EXPERTS, D)),
                  _const_spec((N_EXPERTS, D)), _const_spec((N_EXPERTS, 1))],
        out_specs=[tok(D), tok(half), k_spec, k_spec, k_spec, _const_spec((N_EXPERTS, LANES))],
        out_shape=[jax.ShapeDtypeStruct((T, D), F32), jax.ShapeDtypeStruct((T, half), jnp.int32),
                   kt(jnp.int32), kt(F32), kt(jnp.int32),
                   jax.ShapeDtypeStruct((N_EXPERTS, LANES), F32)],
        scratch_shapes=[pltpu.VMEM((N_EXPERTS, 1), F32)],
        compiler_params=cparams(dimension_semantics=("arbitrary",)),
        name="mix_router",
    )(att, ssm.reshape(T, ssm_width), xn, _row(attn_out_g[l]), _row(ssm_out_g[l]),
      wo[:mla_width], wo[mla_width:], _row(ln1_g[l]), _row(ln1_b[l]), wr_hi, wr_lo,
      router_bias[l].astype(F32).reshape(N_EXPERTS, 1))

    tr = EXPERT_ROW_TILE
    n_tiles = (T * TOP_K) // tr + N_EXPERTS
    n_rows = n_tiles * tr
    cnt = counts[:, 0].astype(jnp.int32)
    tiles_e = (cnt + tr - 1) // tr
    tile_end = jnp.cumsum(tiles_e)
    tile_start = tile_end - tiles_e
    row_off = tile_start * tr
    tile_ids = jnp.arange(n_tiles, dtype=jnp.int32)
    tile_expert = jnp.sum((tile_end[None, :] <= tile_ids[:, None]).astype(jnp.int32), axis=1)
    tile_expert = jnp.minimum(tile_expert, N_EXPERTS - 1)
    tile_valid = jnp.clip(cnt[tile_expert] - (tile_ids - tile_start[tile_expert]) * tr, 0, tr)
    tile_valid = tile_valid.astype(jnp.int32)

    tp = min(2048, T)
    pos = pl.pallas_call(
        _positions_kernel,
        grid_spec=pltpu.PrefetchScalarGridSpec(
            num_scalar_prefetch=1, grid=(T // tp,),
            in_specs=[pl.BlockSpec((TOP_K, tp), lambda i, off: (0, i)),
                      pl.BlockSpec((TOP_K, tp), lambda i, off: (0, i))],
            out_specs=pl.BlockSpec((TOP_K, tp), lambda i, off: (0, i))),
        out_shape=kt(jnp.int32),
        name="positions",
    )(row_off.astype(jnp.int32), idx_k, rank_k)

    xs = _sc_scatter_rows(x1p, pos, n_rows)

    wg, wu, wd = w_gate[l], w_up[l], w_down[l]
    ff = wg.shape[-1]
    ys = pl.pallas_call(
        _experts_kernel,
        grid_spec=pltpu.PrefetchScalarGridSpec(
            num_scalar_prefetch=2, grid=(n_tiles,),
            in_specs=[pl.BlockSpec((tr, half), lambda i, te, tv: (i, 0)),
                      pl.BlockSpec((1, D, ff), lambda i, te, tv: (te[i], 0, 0)),
                      pl.BlockSpec((1, D, ff), lambda i, te, tv: (te[i], 0, 0)),
                      pl.BlockSpec((1, ff, D), lambda i, te, tv: (te[i], 0, 0))],
            out_specs=pl.BlockSpec((tr, half), lambda i, te, tv: (i, 0)),
            scratch_shapes=[pltpu.VMEM((D, ff), BF16), pltpu.VMEM((D, ff), BF16),
                            pltpu.VMEM((ff, D), BF16)]),
        out_shape=jax.ShapeDtypeStruct((n_rows, half), jnp.int32),
        compiler_params=cparams(dimension_semantics=("arbitrary",)),
        name="experts",
    )(tile_expert, tile_valid, xs, wg, wu, wd)

    yg = _sc_gather_rows(ys, pos.reshape(TOP_K * T))

    out = pl.pallas_call(
        _combine_kernel,
        grid=(T // tm,),
        in_specs=[pl.BlockSpec((TOP_K, tm, half), lambda i: (0, i, 0)),
                  pl.BlockSpec((tm, TOP_K), lambda i: (i, 0)), tok(D),
                  _const_spec((D, ff)), _const_spec((D, ff)), _const_spec((ff, D)),
                  _const_spec((1, D)), _const_spec((1, D))],
        out_specs=tok(D),
        out_shape=jax.ShapeDtypeStruct((T, D), F32),
        compiler_params=cparams(dimension_semantics=("parallel",)),
        name="combine",
    )(yg.reshape(TOP_K, T, half), gate_k.T, x1, ws_gate[l].astype(BF16), ws_up[l].astype(BF16),
      ws_down[l].astype(BF16), _row(ln2_g[l]), _row(ln2_b[l]))
    return out.reshape(B, S, D)
```

```python
import functools

import jax
import jax.numpy as jnp
from jax import lax
from jax.experimental import pallas as pl
from jax.experimental.pallas import tpu as pltpu
from jax.experimental.pallas import tpu_sc as plsc

CHUNK = 64
MLA_HEADS = 8
QK_NOPE = 64
QK_ROPE = 32
V_DIM = 64
Q_RANK = 256
KV_RANK = 128
ROPE_THETA = 10000.0
SSM_GROUP = 16
SSM_STATE = 64
N_EXPERTS = 64
TOP_K = 8
N_GROUPS = 8
TOP_GROUPS = 4
ROUTED_SCALE = 2.5
DEPTH = 1
ALPHA = (2.0 * DEPTH) ** 0.25
EPS = 1e-5
LOG2_E = 1.4426950408889634

LANES = 128
HEAD_PAD = LANES
VMEM_LIMIT = 56 * 1024 * 1024
EXPERT_ROW_TILE = 512
MOE_SLABS = 2

BF16 = jnp.bfloat16
F32 = jnp.float32
NT_DIMS = (((1,), (1,)), ((), ()))


def _dot(a, b):
    return jnp.dot(a, b, preferred_element_type=F32)


def _layer_norm(x, g, b):
    mu = jnp.mean(x, axis=-1, keepdims=True)
    xc = x - mu
    var = jnp.mean(xc * xc, axis=-1, keepdims=True)
    return xc * lax.rsqrt(var + EPS) * g + b


def _rms_norm(x, g):
    return x * lax.rsqrt(jnp.mean(x * x, axis=-1, keepdims=True) + EPS) * g


def _inproj_kernel(x_ref, pos_ref, lng_ref, lnb_ref, w1_ref, qg_ref, kvg_ref,
                   wq_ref, wqr_ref, wk_ref, wv_ref, freq_ref,
                   xn_ref, q_ref, k_ref, v_ref, u_ref, *, ssm_width):
    xn = _layer_norm(x_ref[...], lng_ref[...], lnb_ref[...])
    xn_ref[...] = xn
    h = _dot(xn.astype(BF16), w1_ref[...])
    o1 = Q_RANK
    o2 = o1 + KV_RANK
    o3 = o2 + ssm_width
    o4 = o3 + HEAD_PAD
    cq = h[:, :o1]
    ckv = h[:, o1:o2]
    u_ref[...] = h[:, o2:o3]
    kr_raw = h[:, o3:o4]
    kr_rot = h[:, o4:o4 + HEAD_PAD]
    cqn = _rms_norm(cq, qg_ref[...]).astype(BF16)
    ckvn = _rms_norm(ckv, kvg_ref[...]).astype(BF16)

    ang = pos_ref[...] * freq_ref[...]
    c = jnp.cos(ang)
    s = jnp.sin(ang)
    lane = lax.broadcasted_iota(jnp.int32, (1, HEAD_PAD), 1)
    scale = (QK_NOPE + QK_ROPE) ** -0.5 * LOG2_E
    is_rope = (lane >= QK_NOPE) & (lane < QK_NOPE + QK_ROPE)
    cos1 = jnp.where(lane < QK_NOPE, 1.0, jnp.where(is_rope, c, 0.0)) * scale
    sin1 = jnp.where(is_rope, s, 0.0) * scale
    cos_t = jnp.concatenate([cos1] * MLA_HEADS, axis=1)
    sin_t = jnp.concatenate([sin1] * MLA_HEADS, axis=1)
    q = _dot(cqn, wq_ref[...]) * cos_t + _dot(cqn, wqr_ref[...]) * sin_t
    q_ref[...] = q.astype(BF16)

    kr = kr_raw * c + kr_rot * s
    k = _dot(ckvn, wk_ref[...]) + jnp.concatenate([kr] * MLA_HEADS, axis=1)
    k_ref[...] = k.astype(BF16)
    ones_col = jnp.concatenate([jnp.where(lane == V_DIM, 1.0, 0.0)] * MLA_HEADS, axis=1)
    v_ref[...] = (_dot(ckvn, wv_ref[...]) + ones_col).astype(BF16)


def _attn_kernel(q_ref, k_ref, v_ref, o_ref, *, tq, heads):
    qi = pl.program_id(1)
    row_chunk = lax.broadcasted_iota(jnp.int32, (tq, tq), 0) // CHUNK
    col_chunk = lax.broadcasted_iota(jnp.int32, (tq, tq), 1) // CHUNK
    diag_mask = row_chunk >= col_chunk

    def block(j, carry, masked):
        start = pl.multiple_of(j * tq, tq)
        new = []
        for h in range(heads):
            m, acc = carry[2 * h], carry[2 * h + 1]
            cols = slice(h * HEAD_PAD, (h + 1) * HEAD_PAD)
            s = lax.dot_general(q_ref[:, cols], k_ref[pl.ds(start, tq), cols], NT_DIMS,
                                preferred_element_type=F32)
            if masked:
                s = jnp.where(diag_mask, s, -jnp.inf)
            m_new = jnp.maximum(m, jnp.max(s, axis=-1, keepdims=True))
            p = jnp.exp2(s - m_new).astype(BF16)
            acc = jnp.exp2(m - m_new) * acc + _dot(p, v_ref[pl.ds(start, tq), cols])
            new += [m_new, acc]
        return tuple(new)

    init = (jnp.full((tq, 1), -jnp.inf, F32), jnp.zeros((tq, HEAD_PAD), F32)) * heads
    carry = lax.fori_loop(0, qi, functools.partial(block, masked=False), init)
    fin = block(qi, carry, True)
    outs = [fin[2 * h + 1][:, :V_DIM] / fin[2 * h + 1][:, V_DIM:V_DIM + 1] for h in range(heads)]
    o_ref[...] = jnp.concatenate(outs, axis=1).astype(o_ref.dtype)


def _s5_kernel(u_ref, win_ref, are_ref, aim_ref, cre_ref, cim_ref, dskip_ref,
               wglu_ref, bglu_ref, o_ref, vx_ref, hre_ref, him_ref, io_ref, utm_ref, *,
               batch, lt, n_state, scan_tiles):
    ti = pl.program_id(0)

    @pl.when(ti == 0)
    def _():
        hre_ref[...] = jnp.zeros_like(hre_ref)
        him_ref[...] = jnp.zeros_like(him_ref)

    width = u_ref.shape[-1]
    n_slab = width // LANES
    n_tiles = n_state // LANES
    slab_tiles = n_tiles // n_slab
    for b in range(batch):
        for c in range(n_slab):
            io_ref[c, b * lt:(b + 1) * lt, :] = u_ref[b, :, c * LANES:(c + 1) * LANES]

    def to_time_major(t, _):
        dst = pl.ds(pl.multiple_of(t * batch, batch), batch)
        for c in range(n_slab):
            utm_ref[dst, c * LANES:(c + 1) * LANES] = io_ref[c, pl.ds(t, batch, stride=lt), :]
        return 0

    lax.fori_loop(0, lt, to_time_major, 0, unroll=4)
    u2 = utm_ref[...]
    ub = u2.astype(BF16)
    for j in range(n_slab):
        vj = _dot(ub[:, j * LANES:(j + 1) * LANES], win_ref[j])
        for i in range(slab_tiles):
            vx_ref[j * slab_tiles + i] = vj[:, i * LANES:(i + 1) * LANES]
            vx_ref[n_tiles + j * slab_tiles + i] = vj[:, (slab_tiles + i) * LANES:(slab_tiles + i + 1) * LANES]

    for c0 in range(0, n_tiles, scan_tiles):
        tiles = range(c0, c0 + scan_tiles)
        ar = [jnp.broadcast_to(are_ref[:, c * LANES:(c + 1) * LANES], (batch, LANES)) for c in tiles]
        ai = [jnp.broadcast_to(aim_ref[:, c * LANES:(c + 1) * LANES], (batch, LANES)) for c in tiles]

        def step(t, carry, tiles=tiles, ar=ar, ai=ai):
            rows = pl.ds(pl.multiple_of(t * batch, batch), batch)
            out = []
            for n, c in enumerate(tiles):
                hr, hi = carry[2 * n], carry[2 * n + 1]
                nr = ar[n] * hr - ai[n] * hi + vx_ref[c, rows, :]
                ni = ar[n] * hi + ai[n] * hr + vx_ref[n_tiles + c, rows, :]
                vx_ref[c, rows, :] = nr
                vx_ref[n_tiles + c, rows, :] = ni
                out += [nr, ni]
            return tuple(out)

        init = []
        for c in tiles:
            init += [hre_ref[c], him_ref[c]]
        fin = lax.fori_loop(0, lt, step, tuple(init), unroll=4)
        for n, c in enumerate(tiles):
            hre_ref[c] = fin[2 * n]
            him_ref[c] = fin[2 * n + 1]

    ys = []
    for j in range(n_slab):
        xr = jnp.concatenate([vx_ref[j * slab_tiles + i].astype(BF16) for i in range(slab_tiles)], axis=1)
        xi = jnp.concatenate([vx_ref[n_tiles + j * slab_tiles + i].astype(BF16) for i in range(slab_tiles)], axis=1)
        ys.append(_dot(xr, cre_ref[j]) + _dot(xi, cim_ref[j]))
    y = jnp.concatenate(ys, axis=1) + dskip_ref[...] * u2
    y = jax.nn.gelu(y)
    z = _dot(y.astype(BF16), wglu_ref[...]) + bglu_ref[...]
    out = y * jax.nn.sigmoid(z)
    for c in range(n_slab):
        io_ref[c] = out[:, c * LANES:(c + 1) * LANES]
    for b in range(batch):
        for c in range(n_slab):
            o_ref[b, :, c * LANES:(c + 1) * LANES] = io_ref[c, pl.ds(b, lt, stride=batch), :].astype(o_ref.dtype)


def _router_gates(logits_t, rbias):
    n_exp, tm = logits_t.shape
    per_group = n_exp // N_GROUPS
    scores = jax.nn.sigmoid(logits_t)
    sel = scores + rbias
    neg_inf = -jnp.inf
    sub_iota = lax.broadcasted_iota(jnp.int32, (per_group, tm), 0).astype(F32)
    group_score = []
    for g in range(N_GROUPS):
        sg = sel[g * per_group:(g + 1) * per_group, :]
        m1 = jnp.max(sg, axis=0, keepdims=True)
        first = jnp.min(jnp.where(sg == m1, sub_iota, float(per_group)), axis=0, keepdims=True)
        m2 = jnp.max(jnp.where(sub_iota == first, neg_inf, sg), axis=0, keepdims=True)
        group_score.append(m1 + m2)
    masked = []
    for g in range(N_GROUPS):
        rank = jnp.zeros((1, tm), F32)
        for g2 in range(N_GROUPS):
            if g2 == g:
                continue
            ahead = (group_score[g2] >= group_score[g]) if g2 < g else (group_score[g2] > group_score[g])
            rank = rank + jnp.where(ahead, 1.0, 0.0)
        keep = rank < float(TOP_GROUPS)
        masked.append(jnp.where(keep, sel[g * per_group:(g + 1) * per_group, :], neg_inf))
    cur = jnp.concatenate(masked, axis=0)
    iota = lax.broadcasted_iota(jnp.int32, (n_exp, tm), 0).astype(F32)
    chosen = jnp.zeros((n_exp, tm), F32)
    picks, weights = [], []
    for _ in range(TOP_K):
        m = jnp.max(cur, axis=0, keepdims=True)
        idx = jnp.min(jnp.where(cur == m, iota, float(n_exp)), axis=0, keepdims=True)
        pick = iota == idx
        chosen = jnp.where(pick, 1.0, chosen)
        cur = jnp.where(pick, neg_inf, cur)
        picks.append(idx)
        weights.append(jnp.sum(jnp.where(pick, scores, 0.0), axis=0, keepdims=True))
    idx_k = jnp.concatenate(picks, axis=0)
    w_k = jnp.concatenate(weights, axis=0)
    gate_k = w_k / jnp.sum(w_k, axis=0, keepdims=True) * ROUTED_SCALE
    return idx_k, gate_k, chosen


def _pack_bf16_pairs(x):
    n = x.shape[1] // 2
    hi = lax.bitcast_convert_type(x[:, :n].astype(BF16).astype(F32), jnp.int32)
    lo = lax.bitcast_convert_type(x[:, n:].astype(BF16).astype(F32), jnp.int32)
    return hi | lax.shift_right_logical(lo, 16)


def _unpack_bf16_pairs(p):
    hi = lax.bitcast_convert_type(p & jnp.int32(-65536), F32).astype(BF16)
    lo = lax.bitcast_convert_type(lax.shift_left(p, 16), F32).astype(BF16)
    return hi, lo


def _mix_kernel(att_ref, ssm_ref, xn_ref, ag_ref, sg_ref, woa_ref, wos_ref,
                g_ref, b_ref, wrh_ref, wrl_ref, rb_ref,
                x1_ref, x1p_ref, idx_ref, gate_ref, rank_ref, cnt_ref, carry_ref):
    @pl.when(pl.program_id(0) == 0)
    def _():
        carry_ref[...] = jnp.zeros_like(carry_ref)

    an = _rms_norm(att_ref[...].astype(F32), ag_ref[...]).astype(BF16)
    sn = _rms_norm(ssm_ref[...].astype(F32), sg_ref[...]).astype(BF16)
    mix = _dot(an, woa_ref[...]) + _dot(sn, wos_ref[...])
    x1 = _layer_norm(ALPHA * xn_ref[...] + mix, g_ref[...], b_ref[...])
    x1_ref[...] = x1
    x1p_ref[...] = _pack_bf16_pairs(x1)
    x_hi = x1.astype(BF16)
    x_lo = (x1 - x_hi.astype(F32)).astype(BF16)
    dg = functools.partial(lax.dot_general, dimension_numbers=NT_DIMS, preferred_element_type=F32)
    logits_t = dg(wrh_ref[...], x_hi) + dg(wrl_ref[...], x_hi) + dg(wrh_ref[...], x_lo)
    idx_k, gate_k, chosen = _router_gates(logits_t, rb_ref[...])
    idx_ref[...] = idx_k.astype(jnp.int32)
    gate_ref[...] = gate_k

    n_exp, tm = chosen.shape
    before = (lax.broadcasted_iota(jnp.int32, (tm, tm), 0)
              < lax.broadcasted_iota(jnp.int32, (tm, tm), 1))
    excl = _dot(chosen.astype(BF16), jnp.where(before, 1.0, 0.0).astype(BF16))
    rank_full = carry_ref[...] + excl
    iota = lax.broadcasted_iota(jnp.int32, (n_exp, tm), 0).astype(F32)
    ranks = [jnp.sum(jnp.where(iota == idx_k[k:k + 1, :], rank_full, 0.0), axis=0, keepdims=True)
             for k in range(TOP_K)]
    rank_ref[...] = jnp.concatenate(ranks, axis=0).astype(jnp.int32)
    total = carry_ref[...] + jnp.sum(chosen, axis=1, keepdims=True)
    carry_ref[...] = total
    cnt_ref[...] = jnp.broadcast_to(total, cnt_ref.shape)


def _positions_kernel(off_ref, idx_ref, rank_ref, pos_ref):
    idx = idx_ref[...]
    base = jnp.zeros(idx.shape, jnp.int32)
    for e in range(N_EXPERTS):
        base = jnp.where(idx == e, off_ref[e], base)
    pos_ref[...] = rank_ref[...] + base


def _swiglu(x_hi, x_lo, wg_ref, wu_ref, wd_ref):
    half = x_hi.shape[1]
    hg = _dot(x_hi, wg_ref[:half, :]) + _dot(x_lo, wg_ref[half:, :])
    hu = _dot(x_hi, wu_ref[:half, :]) + _dot(x_lo, wu_ref[half:, :])
    h = jax.nn.silu(hg) * hu
    return _dot(h.astype(BF16), wd_ref[...])


def _experts_kernel(te_ref, valid_ref, xs_ref, wg_ref, wu_ref, wd_ref, ys_ref,
                    wgb_ref, wub_ref, wdb_ref):
    i = pl.program_id(0)
    valid = valid_ref[i]

    @pl.when((i == 0) | (te_ref[i] != te_ref[jnp.maximum(i - 1, 0)]))
    def _():
        wgb_ref[...] = wg_ref[0].astype(BF16)
        wub_ref[...] = wu_ref[0].astype(BF16)
        wdb_ref[...] = wd_ref[0].astype(BF16)

    @pl.when(valid > 0)
    def _():
        rows = lax.broadcasted_iota(jnp.int32, xs_ref.shape, 0)
        x_hi, x_lo = _unpack_bf16_pairs(jnp.where(rows < valid, xs_ref[...], 0))
        ys_ref[...] = _pack_bf16_pairs(_swiglu(x_hi, x_lo, wgb_ref, wub_ref, wdb_ref))

    @pl.when(valid == 0)
    def _():
        ys_ref[...] = jnp.zeros_like(ys_ref)


def _combine_kernel(yg_ref, gate_ref, x1_ref, wsg_ref, wsu_ref, wsd_ref, g_ref, b_ref, o_ref):
    x1 = x1_ref[...]
    half = x1.shape[1] // 2
    acc = _swiglu(x1[:, :half].astype(BF16), x1[:, half:].astype(BF16), wsg_ref, wsu_ref, wsd_ref)
    gates = gate_ref[...]
    acc_hi, acc_lo = acc[:, :half], acc[:, half:]
    for k in range(TOP_K):
        y_hi, y_lo = _unpack_bf16_pairs(yg_ref[k])
        gk = gates[:, k:k + 1]
        acc_hi = acc_hi + gk * y_hi.astype(F32)
        acc_lo = acc_lo + gk * y_lo.astype(F32)
    ffn = jnp.concatenate([acc_hi, acc_lo], axis=1)
    o_ref[...] = _layer_norm(ALPHA * x1 + ffn, g_ref[...], b_ref[...])


def _combine_into_kernel(prev_ref, *refs):
    del prev_ref
    _combine_kernel(*refs)


SC_CORES = 2
SC_SUBCORES = 16
SC_WORKERS = SC_CORES * SC_SUBCORES
SC_CHUNK = 64


def _sc_mesh():
    return plsc.VectorSubcoreMesh(core_axis_name="c", subcore_axis_name="s")


def _sc_gather_rows(table, idx):
    n = idx.shape[0]
    d = table.shape[1]
    per_w = n // SC_WORKERS
    n_ch = per_w // SC_CHUNK

    @functools.partial(
        pl.kernel, mesh=_sc_mesh(),
        out_type=jax.ShapeDtypeStruct((n, d), table.dtype),
        scratch_types=[pltpu.VMEM((n_ch, SC_CHUNK), jnp.int32),
                       pltpu.VMEM((2, SC_CHUNK, d), table.dtype),
                       pltpu.SemaphoreType.DMA((2,)),
                       pltpu.SemaphoreType.DMA((2,))],
    )
    def k(table_hbm, idx_hbm, out_hbm, idx_v, buf, gsem, osem):
        wid = lax.axis_index("s") * SC_CORES + lax.axis_index("c")
        base = wid * per_w
        pltpu.sync_copy(idx_hbm.at[wid], idx_v)

        def gather(c, b):
            return pltpu.make_async_copy(table_hbm.at[idx_v.at[c]], buf.at[b], gsem.at[b])

        def put(c, b):
            return pltpu.make_async_copy(buf.at[b], out_hbm.at[pl.ds(base + c * SC_CHUNK, SC_CHUNK)],
                                         osem.at[b])

        gather(0, 0).start()

        @pl.loop(0, n_ch, step=2)
        def _(c):
            for b in range(2):
                cc = c + b
                gather(cc, b).wait()

                @pl.when(cc + 1 < n_ch)
                def _():
                    @pl.when(cc >= 1)
                    def _():
                        put(cc - 1, 1 - b).wait()
                    gather(cc + 1, 1 - b).start()

                put(cc, b).start()

        put(n_ch - 2, 0).wait()
        put(n_ch - 1, 1).wait()

    return k(table, idx.reshape(SC_WORKERS, n_ch, SC_CHUNK))


def _sc_scatter_rows(x, pos, n_out):
    t, d = x.shape
    kk = pos.shape[0]
    per_w = t // SC_WORKERS
    n_ch = per_w // SC_CHUNK
    pos_w = pos.reshape(kk, SC_WORKERS, n_ch, SC_CHUNK).transpose(1, 2, 0, 3)
    pos_w = pos_w.reshape(SC_WORKERS, n_ch * kk, SC_CHUNK)

    @functools.partial(
        pl.kernel, mesh=_sc_mesh(),
        out_type=jax.ShapeDtypeStruct((n_out, d), x.dtype),
        scratch_types=[pltpu.VMEM((n_ch * kk, SC_CHUNK), jnp.int32),
                       pltpu.VMEM((2, SC_CHUNK, d), x.dtype),
                       pltpu.SemaphoreType.DMA((2,)),
                       pltpu.SemaphoreType.DMA((2,))],
    )
    def k(x_hbm, pos_hbm, out_hbm, idx_v, buf, isem, osem):
        wid = lax.axis_index("s") * SC_CORES + lax.axis_index("c")
        base = wid * per_w
        pltpu.sync_copy(pos_hbm.at[wid], idx_v)

        def get(c, b):
            return pltpu.make_async_copy(x_hbm.at[pl.ds(base + c * SC_CHUNK, SC_CHUNK)], buf.at[b],
                                         isem.at[b])

        def put(c, j, b):
            return pltpu.make_async_copy(buf.at[b], out_hbm.at[idx_v.at[c * kk + j]], osem.at[b])

        get(0, 0).start()

        @pl.loop(0, n_ch, step=2)
        def _(c):
            for b in range(2):
                cc = c + b
                get(cc, b).wait()

                @pl.when(cc + 1 < n_ch)
                def _():
                    @pl.when(cc >= 1)
                    def _():
                        for j in range(kk):
                            put(cc - 1, j, 1 - b).wait()
                    get(cc + 1, 1 - b).start()

                for j in range(kk):
                    put(cc, j, b).start()

        for j in range(kk):
            put(n_ch - 2, j, 0).wait()
        for j in range(kk):
            put(n_ch - 1, j, 1).wait()

    return k(x, pos_w)


def _row(v):
    return v.reshape(1, -1).astype(F32)


def _const_spec(shape):
    nd = len(shape)
    return pl.BlockSpec(shape, lambda *_: (0,) * nd)


def _pad_heads(w, width):
    r, h, _ = w.shape
    return jnp.pad(w, ((0, 0), (0, 0), (0, HEAD_PAD - width))).reshape(r, h * HEAD_PAD)


def _half_rotate(w):
    half = QK_ROPE // 2
    return jnp.concatenate([-w[..., half:], w[..., :half]], axis=-1)


def kernel(x, positions, ln_in_g, ln_in_b, w_in, q_norm_g, kv_norm_g, w_uq, w_ukv, lambda_re, lambda_im, log_step, b_re, b_im, c_re, c_im, d_skip, w_glu, b_glu, attn_out_g, ssm_out_g, w_o, ln1_g, ln1_b, w_router, router_bias, w_gate, w_up, w_down, ws_gate, ws_up, ws_down, ln2_g, ln2_b):
    B, S, D = x.shape
    T = B * S
    l = 0
    ssm_width = w_glu.shape[-1]
    n_groups = ssm_width // SSM_GROUP
    n_state = n_groups * SSM_STATE
    mla_width = MLA_HEADS * V_DIM
    qk_pad = MLA_HEADS * HEAD_PAD
    cparams = functools.partial(pltpu.CompilerParams, vmem_limit_bytes=VMEM_LIMIT)

    s1, s2, s3 = Q_RANK, Q_RANK + KV_RANK, Q_RANK + KV_RANK + QK_ROPE
    wi = w_in[l]
    w_kr = wi[:, s2:s3]
    pad_rope = lambda w: jnp.pad(w, ((0, 0), (QK_NOPE, HEAD_PAD - QK_NOPE - QK_ROPE)))
    w1 = jnp.concatenate([wi[:, :s2], wi[:, s3:], pad_rope(w_kr), pad_rope(_half_rotate(w_kr))],
                         axis=1).astype(BF16)
    wq = w_uq[l]
    zeros_nope = jnp.zeros(wq.shape[:2] + (QK_NOPE,), wq.dtype)
    wq_main = _pad_heads(wq, QK_NOPE + QK_ROPE).astype(BF16)
    wq_rot = _pad_heads(jnp.concatenate([zeros_nope, _half_rotate(wq[..., QK_NOPE:])], axis=-1),
                        QK_NOPE + QK_ROPE).astype(BF16)
    wkv = w_ukv[l]
    wk = _pad_heads(wkv[..., :QK_NOPE], QK_NOPE).astype(BF16)
    wv = _pad_heads(wkv[..., QK_NOPE:], V_DIM).astype(BF16)
    half = QK_ROPE // 2
    inv_freq = ROPE_THETA ** (-jnp.arange(half, dtype=F32) / half)
    freq = jnp.pad(jnp.concatenate([inv_freq, inv_freq]),
                   (QK_NOPE, HEAD_PAD - QK_NOPE - QK_ROPE)).reshape(1, HEAD_PAD)
    pos_f = positions.astype(F32).reshape(T, 1)

    tm = min(512, T)
    w1_cols = w1.shape[1]
    tok = lambda width: pl.BlockSpec((tm, width), lambda i: (i, 0))
    xn, q, k, v, u = pl.pallas_call(
        functools.partial(_inproj_kernel, ssm_width=ssm_width),
        grid=(T // tm,),
        in_specs=[tok(D), tok(1), _const_spec((1, D)), _const_spec((1, D)),
                  _const_spec((D, w1_cols)), _const_spec((1, Q_RANK)), _const_spec((1, KV_RANK)),
                  _const_spec((Q_RANK, qk_pad)), _const_spec((Q_RANK, qk_pad)),
                  _const_spec((KV_RANK, qk_pad)), _const_spec((KV_RANK, qk_pad)),
                  _const_spec((1, HEAD_PAD))],
        out_specs=[tok(D), tok(qk_pad), tok(qk_pad), tok(qk_pad), tok(ssm_width)],
        out_shape=[jax.ShapeDtypeStruct((T, D), F32), jax.ShapeDtypeStruct((T, qk_pad), BF16),
                   jax.ShapeDtypeStruct((T, qk_pad), BF16), jax.ShapeDtypeStruct((T, qk_pad), BF16),
                   jax.ShapeDtypeStruct((T, ssm_width), F32)],
        compiler_params=cparams(dimension_semantics=("parallel",)),
        name="inproj",
    )(x.reshape(T, D), pos_f, _row(ln_in_g), _row(ln_in_b), w1, _row(q_norm_g[l]), _row(kv_norm_g[l]),
      wq_main, wq_rot, wk, wv, freq)

    tq = min(256, S)
    nq = S // tq
    att = pl.pallas_call(
        functools.partial(_attn_kernel, tq=tq, heads=MLA_HEADS),
        grid=(B, nq),
        in_specs=[pl.BlockSpec((tq, qk_pad), lambda b, i: (b * nq + i, 0)),
                  pl.BlockSpec((S, qk_pad), lambda b, i: (b, 0)),
                  pl.BlockSpec((S, qk_pad), lambda b, i: (b, 0))],
        out_specs=pl.BlockSpec((tq, mla_width), lambda b, i: (b * nq + i, 0)),
        out_shape=jax.ShapeDtypeStruct((T, mla_width), BF16),
        compiler_params=cparams(dimension_semantics=("parallel", "arbitrary")),
        name="attention",
    )(q, k, v)

    lam = lax.complex(jnp.minimum(lambda_re[l].astype(F32), -1e-4), lambda_im[l].astype(F32))
    step = jnp.exp(log_step[l].astype(F32))[:, None]
    lam_bar = jnp.exp(lam * step)
    b_bar = ((lam_bar - 1.0) / lam)[..., None] * lax.complex(b_re[l].astype(F32), b_im[l].astype(F32))
    n_slab = ssm_width // LANES
    g_per_slab = n_groups // n_slab
    eye = jnp.eye(n_groups, dtype=F32)

    def expand_in(bpart):
        return jnp.einsum('gpc,gh->gchp', bpart, eye).reshape(ssm_width, n_state)

    def expand_out(cpart):
        return jnp.einsum('gcp,gh->gphc', cpart, eye).reshape(n_state, ssm_width)

    slab = n_state // n_slab
    win_re, win_im = expand_in(jnp.real(b_bar)), expand_in(jnp.imag(b_bar))
    win = jnp.stack([jnp.concatenate([win_re[j * LANES:(j + 1) * LANES, j * slab:(j + 1) * slab],
                                      win_im[j * LANES:(j + 1) * LANES, j * slab:(j + 1) * slab]], axis=1)
                     for j in range(n_slab)]).astype(BF16)
    wc_re, wc_im = expand_out(c_re[l].astype(F32)), expand_out(-c_im[l].astype(F32))
    cre = jnp.stack([wc_re[j * slab:(j + 1) * slab, j * LANES:(j + 1) * LANES] for j in range(n_slab)]).astype(BF16)
    cim = jnp.stack([wc_im[j * slab:(j + 1) * slab, j * LANES:(j + 1) * LANES] for j in range(n_slab)]).astype(BF16)
    a_re = jnp.real(lam_bar).reshape(1, n_state)
    a_im = jnp.imag(lam_bar).reshape(1, n_state)

    lt = min(64, S)
    ssm = pl.pallas_call(
        functools.partial(_s5_kernel, batch=B, lt=lt, n_state=n_state, scan_tiles=4),
        grid=(S // lt,),
        in_specs=[pl.BlockSpec((B, lt, ssm_width), lambda t: (0, t, 0)),
                  _const_spec(win.shape), _const_spec((1, n_state)), _const_spec((1, n_state)),
                  _const_spec(cre.shape), _const_spec(cim.shape), _const_spec((1, ssm_width)),
                  _const_spec((ssm_width, ssm_width)), _const_spec((1, ssm_width))],
        out_specs=pl.BlockSpec((B, lt, ssm_width), lambda t: (0, t, 0)),
        out_shape=jax.ShapeDtypeStruct((B, S, ssm_width), BF16),
        scratch_shapes=[pltpu.VMEM((2 * n_state // LANES, B * lt, LANES), F32),
                        pltpu.VMEM((n_state // LANES, B, LANES), F32),
                        pltpu.VMEM((n_state // LANES, B, LANES), F32),
                        pltpu.VMEM((ssm_width // LANES, B * lt, LANES), F32),
                        pltpu.VMEM((B * lt, ssm_width), F32)],
        compiler_params=cparams(dimension_semantics=("arbitrary",)),
        name="s5",
    )(u.reshape(B, S, ssm_width), win, a_re, a_im, cre, cim, _row(d_skip[l]),
      w_glu[l].astype(BF16), _row(b_glu[l]))

    wo = w_o[l].astype(BF16)
    wr_t = w_router[l].T.astype(F32)
    wr_hi = wr_t.astype(BF16)
    wr_lo = (wr_t - wr_hi.astype(F32)).astype(BF16)
    half = D // 2
    n_slabs = MOE_SLABS
    ts = T // n_slabs
    nt = ts // tm
    ssm2 = ssm.reshape(T, ssm_width)
    rbias = router_bias[l].astype(F32).reshape(N_EXPERTS, 1)
    kt = lambda dt: jax.ShapeDtypeStruct((TOP_K, ts), dt)
    k_spec = pl.BlockSpec((TOP_K, tm), lambda i: (0, i))

    def route(s):
        tok_s = lambda width: pl.BlockSpec((tm, width), lambda i: (i + s * nt, 0))
        return pl.pallas_call(
            _mix_kernel,
            grid=(nt,),
            in_specs=[tok_s(mla_width), tok_s(ssm_width), tok_s(D), _const_spec((1, mla_width)),
                      _const_spec((1, ssm_width)), _const_spec((mla_width, D)),
                      _const_spec((ssm_width, D)), _const_spec((1, D)), _const_spec((1, D)),
                      _const_spec((N_EXPERTS, D)), _const_spec((N_EXPERTS, D)),
                      _const_spec((N_EXPERTS, 1))],
            out_specs=[tok(D), tok(half), k_spec, k_spec, k_spec, _const_spec((N_EXPERTS, LANES))],
            out_shape=[jax.ShapeDtypeStruct((ts, D), F32), jax.ShapeDtypeStruct((ts, half), jnp.int32),
                       kt(jnp.int32), kt(F32), kt(jnp.int32),
                       jax.ShapeDtypeStruct((N_EXPERTS, LANES), F32)],
            scratch_shapes=[pltpu.VMEM((N_EXPERTS, 1), F32)],
            compiler_params=cparams(dimension_semantics=("arbitrary",)),
            name="mix_router",
        )(att, ssm2, xn, _row(attn_out_g[l]), _row(ssm_out_g[l]), wo[:mla_width], wo[mla_width:],
          _row(ln1_g[l]), _row(ln1_b[l]), wr_hi, wr_lo, rbias)

    tr = EXPERT_ROW_TILE
    n_tiles = (ts * TOP_K) // tr + N_EXPERTS
    n_rows = n_tiles * tr
    tp = min(2048, ts)

    def dispatch(x1p, idx_k, rank_k, counts):
        cnt = counts[:, 0].astype(jnp.int32)
        tiles_e = (cnt + tr - 1) // tr
        tile_end = jnp.cumsum(tiles_e)
        tile_start = tile_end - tiles_e
        tile_ids = jnp.arange(n_tiles, dtype=jnp.int32)
        tile_expert = jnp.sum((tile_end[None, :] <= tile_ids[:, None]).astype(jnp.int32), axis=1)
        tile_expert = jnp.minimum(tile_expert, N_EXPERTS - 1)
        owner = (tile_start[None, :] <= tile_ids[:, None]) & (tile_ids[:, None] < tile_end[None, :])
        left = jnp.sum(jnp.where(owner, cnt[None, :] - (tile_ids[:, None] - tile_start[None, :]) * tr, 0),
                       axis=1)
        tile_valid = jnp.clip(left, 0, tr).astype(jnp.int32)
        pos = pl.pallas_call(
            _positions_kernel,
            grid_spec=pltpu.PrefetchScalarGridSpec(
                num_scalar_prefetch=1, grid=(ts // tp,),
                in_specs=[pl.BlockSpec((TOP_K, tp), lambda i, off: (0, i)),
                          pl.BlockSpec((TOP_K, tp), lambda i, off: (0, i))],
                out_specs=pl.BlockSpec((TOP_K, tp), lambda i, off: (0, i))),
            out_shape=kt(jnp.int32),
            name="positions",
        )((tile_start * tr).astype(jnp.int32), idx_k, rank_k)
        return _sc_scatter_rows(x1p, pos, n_rows), pos, tile_expert, tile_valid

    wg, wu, wd = w_gate[l], w_up[l], w_down[l]
    ff = wg.shape[-1]

    def experts(xs, tile_expert, tile_valid):
        return pl.pallas_call(
            _experts_kernel,
            grid_spec=pltpu.PrefetchScalarGridSpec(
                num_scalar_prefetch=2, grid=(n_tiles,),
                in_specs=[pl.BlockSpec((tr, half), lambda i, te, tv: (i, 0)),
                          pl.BlockSpec((1, D, ff), lambda i, te, tv: (te[i], 0, 0)),
                          pl.BlockSpec((1, D, ff), lambda i, te, tv: (te[i], 0, 0)),
                          pl.BlockSpec((1, ff, D), lambda i, te, tv: (te[i], 0, 0))],
                out_specs=pl.BlockSpec((tr, half), lambda i, te, tv: (i, 0)),
                scratch_shapes=[pltpu.VMEM((D, ff), BF16), pltpu.VMEM((D, ff), BF16),
                                pltpu.VMEM((ff, D), BF16)]),
            out_shape=jax.ShapeDtypeStruct((n_rows, half), jnp.int32),
            compiler_params=cparams(dimension_semantics=("arbitrary",)),
            name="experts",
        )(tile_expert, tile_valid, xs, wg, wu, wd)

    shared = (ws_gate[l].astype(BF16), ws_up[l].astype(BF16), ws_down[l].astype(BF16))

    def combine(s, out_so_far, yg, gate_k, x1):
        specs = [pl.BlockSpec((TOP_K, tm, half), lambda i: (0, i, 0)),
                 pl.BlockSpec((tm, TOP_K), lambda i: (i, 0)), tok(D),
                 _const_spec((D, ff)), _const_spec((D, ff)), _const_spec((ff, D)),
                 _const_spec((1, D)), _const_spec((1, D))]
        args = (yg.reshape(TOP_K, ts, half), gate_k.T, x1, *shared, _row(ln2_g[l]), _row(ln2_b[l]))
        body, aliases = _combine_kernel, {}
        if out_so_far is not None:
            specs = [pl.BlockSpec(memory_space=pl.ANY)] + specs
            args = (out_so_far,) + args
            body, aliases = _combine_into_kernel, {0: 0}
        return pl.pallas_call(
            body,
            grid=(nt,),
            in_specs=specs,
            out_specs=pl.BlockSpec((tm, D), lambda i: (i + s * nt, 0)),
            out_shape=jax.ShapeDtypeStruct((T, D), F32),
            input_output_aliases=aliases,
            compiler_params=cparams(dimension_semantics=("parallel",)),
            name="combine",
        )(*args)

    routed, moved = [], []
    for s in range(n_slabs):
        x1, x1p, idx_k, gate_k, rank_k, counts = route(s)
        routed.append((x1, gate_k))
        moved.append(dispatch(x1p, idx_k, rank_k, counts))
    gathered = []
    for xs, pos, tile_expert, tile_valid in moved:
        gathered.append(_sc_gather_rows(experts(xs, tile_expert, tile_valid), pos.reshape(TOP_K * ts)))
    out = None
    for s in range(n_slabs):
        out = combine(s, out, gathered[s], routed[s][1], routed[s][0])
    return out.reshape(B, S, D)
```

```python
import functools

import jax
import jax.numpy as jnp
from jax import lax
from jax.experimental import pallas as pl
from jax.experimental.pallas import tpu as pltpu
from jax.experimental.pallas import tpu_sc as plsc

CHUNK = 64
MLA_HEADS = 8
QK_NOPE = 64
QK_ROPE = 32
V_DIM = 64
Q_RANK = 256
KV_RANK = 128
ROPE_THETA = 10000.0
SSM_GROUP = 16
SSM_STATE = 64
N_EXPERTS = 64
TOP_K = 8
N_GROUPS = 8
TOP_GROUPS = 4
ROUTED_SCALE = 2.5
DEPTH = 1
ALPHA = (2.0 * DEPTH) ** 0.25
EPS = 1e-5
LOG2_E = 1.4426950408889634

LANES = 128
HEAD_PAD = LANES
VMEM_LIMIT = 56 * 1024 * 1024
EXPERT_ROW_TILE = 512
MOE_SLABS = 2

BF16 = jnp.bfloat16
F32 = jnp.float32
NT_DIMS = (((1,), (1,)), ((), ()))


def _dot(a, b):
    return jnp.dot(a, b, preferred_element_type=F32)


def _layer_norm(x, g, b):
    mu = jnp.mean(x, axis=-1, keepdims=True)
    xc = x - mu
    var = jnp.mean(xc * xc, axis=-1, keepdims=True)
    return xc * lax.rsqrt(var + EPS) * g + b


def _rms_norm(x, g):
    return x * lax.rsqrt(jnp.mean(x * x, axis=-1, keepdims=True) + EPS) * g


def _inproj_kernel(x_ref, pos_ref, lng_ref, lnb_ref, w1_ref, qg_ref, kvg_ref,
                   wq_ref, wqr_ref, wk_ref, wv_ref, freq_ref,
                   xn_ref, q_ref, k_ref, v_ref, u_ref, *, ssm_width):
    xn = _layer_norm(x_ref[...], lng_ref[...], lnb_ref[...])
    xn_ref[...] = xn
    h = _dot(xn.astype(BF16), w1_ref[...])
    o1 = Q_RANK
    o2 = o1 + KV_RANK
    o3 = o2 + ssm_width
    o4 = o3 + HEAD_PAD
    cq = h[:, :o1]
    ckv = h[:, o1:o2]
    u_ref[...] = h[:, o2:o3]
    kr_raw = h[:, o3:o4]
    kr_rot = h[:, o4:o4 + HEAD_PAD]
    cqn = _rms_norm(cq, qg_ref[...]).astype(BF16)
    ckvn = _rms_norm(ckv, kvg_ref[...]).astype(BF16)

    ang = pos_ref[...] * freq_ref[...]
    c = jnp.cos(ang)
    s = jnp.sin(ang)
    lane = lax.broadcasted_iota(jnp.int32, (1, HEAD_PAD), 1)
    scale = (QK_NOPE + QK_ROPE) ** -0.5 * LOG2_E
    is_rope = (lane >= QK_NOPE) & (lane < QK_NOPE + QK_ROPE)
    cos1 = jnp.where(lane < QK_NOPE, 1.0, jnp.where(is_rope, c, 0.0)) * scale
    sin1 = jnp.where(is_rope, s, 0.0) * scale
    cos_t = jnp.concatenate([cos1] * MLA_HEADS, axis=1)
    sin_t = jnp.concatenate([sin1] * MLA_HEADS, axis=1)
    q = _dot(cqn, wq_ref[...]) * cos_t + _dot(cqn, wqr_ref[...]) * sin_t
    q_ref[...] = q.astype(BF16)

    kr = kr_raw * c + kr_rot * s
    k = _dot(ckvn, wk_ref[...]) + jnp.concatenate([kr] * MLA_HEADS, axis=1)
    k_ref[...] = k.astype(BF16)
    ones_col = jnp.concatenate([jnp.where(lane == V_DIM, 1.0, 0.0)] * MLA_HEADS, axis=1)
    v_ref[...] = (_dot(ckvn, wv_ref[...]) + ones_col).astype(BF16)


def _attn_kernel(q_ref, k_ref, v_ref, o_ref, s_ref, mx_ref, acc_ref, *, tq, heads):
    qi = pl.program_id(1)
    row_chunk = lax.broadcasted_iota(jnp.int32, (tq, tq), 0) // CHUNK
    col_chunk = lax.broadcasted_iota(jnp.int32, (tq, tq), 1) // CHUNK
    diag_mask = row_chunk >= col_chunk
    head_cols = [slice(h * HEAD_PAD, (h + 1) * HEAD_PAD) for h in range(heads)]
    mx_ref[...] = jnp.full(mx_ref.shape, -jnp.inf, F32)
    acc_ref[...] = jnp.zeros(acc_ref.shape, F32)

    def scores(j, masked):
        start = pl.multiple_of(j * tq, tq)
        for h in range(heads):
            s = lax.dot_general(q_ref[:, head_cols[h]], k_ref[pl.ds(start, tq), head_cols[h]], NT_DIMS,
                                preferred_element_type=F32)
            if masked:
                s = jnp.where(diag_mask, s, -jnp.inf)
            s_ref[h, j] = s
            lane_max = s[:, :LANES]
            for c in range(1, tq // LANES):
                lane_max = jnp.maximum(lane_max, s[:, c * LANES:(c + 1) * LANES])
            mx_ref[h] = jnp.maximum(mx_ref[h], lane_max)

    def scores_step(j, carry):
        scores(j, False)
        return carry

    lax.fori_loop(0, qi, scores_step, 0)
    scores(qi, True)
    row_max = [jnp.max(mx_ref[h], axis=-1, keepdims=True) for h in range(heads)]

    def values_step(j, carry):
        start = pl.multiple_of(j * tq, tq)
        for h in range(heads):
            p = jnp.exp2(s_ref[h, j] - row_max[h]).astype(BF16)
            acc_ref[h] += _dot(p, v_ref[pl.ds(start, tq), head_cols[h]])
        return carry

    lax.fori_loop(0, qi + 1, values_step, 0)
    outs = [acc_ref[h][:, :V_DIM] / acc_ref[h][:, V_DIM:V_DIM + 1] for h in range(heads)]
    o_ref[...] = jnp.concatenate(outs, axis=1).astype(o_ref.dtype)


def _s5_kernel(u_ref, win_ref, are_ref, aim_ref, cre_ref, cim_ref, dskip_ref,
               wglu_ref, bglu_ref, o_ref, vx_ref, hre_ref, him_ref, io_ref, utm_ref, *,
               batch, lt, n_state, scan_tiles):
    ti = pl.program_id(0)

    @pl.when(ti == 0)
    def _():
        hre_ref[...] = jnp.zeros_like(hre_ref)
        him_ref[...] = jnp.zeros_like(him_ref)

    width = u_ref.shape[-1]
    n_slab = width // LANES
    n_tiles = n_state // LANES
    slab_tiles = n_tiles // n_slab
    for b in range(batch):
        for c in range(n_slab):
            io_ref[c, b * lt:(b + 1) * lt, :] = u_ref[b, :, c * LANES:(c + 1) * LANES]

    def to_time_major(t, _):
        dst = pl.ds(pl.multiple_of(t * batch, batch), batch)
        for c in range(n_slab):
            utm_ref[dst, c * LANES:(c + 1) * LANES] = io_ref[c, pl.ds(t, batch, stride=lt), :]
        return 0

    lax.fori_loop(0, lt, to_time_major, 0, unroll=4)
    u2 = utm_ref[...]
    ub = u2.astype(BF16)
    for j in range(n_slab):
        vj = _dot(ub[:, j * LANES:(j + 1) * LANES], win_ref[j])
        for i in range(slab_tiles):
            vx_ref[j * slab_tiles + i] = vj[:, i * LANES:(i + 1) * LANES]
            vx_ref[n_tiles + j * slab_tiles + i] = vj[:, (slab_tiles + i) * LANES:(slab_tiles + i + 1) * LANES]

    for c0 in range(0, n_tiles, scan_tiles):
        tiles = range(c0, c0 + scan_tiles)
        ar = [jnp.broadcast_to(are_ref[:, c * LANES:(c + 1) * LANES], (batch, LANES)) for c in tiles]
        ai = [jnp.broadcast_to(aim_ref[:, c * LANES:(c + 1) * LANES], (batch, LANES)) for c in tiles]

        def step(t, carry, tiles=tiles, ar=ar, ai=ai):
            rows = pl.ds(pl.multiple_of(t * batch, batch), batch)
            out = []
            for n, c in enumerate(tiles):
                hr, hi = carry[2 * n], carry[2 * n + 1]
                nr = ar[n] * hr - ai[n] * hi + vx_ref[c, rows, :]
                ni = ar[n] * hi + ai[n] * hr + vx_ref[n_tiles + c, rows, :]
                vx_ref[c, rows, :] = nr
                vx_ref[n_tiles + c, rows, :] = ni
                out += [nr, ni]
            return tuple(out)

        init = []
        for c in tiles:
            init += [hre_ref[c], him_ref[c]]
        fin = lax.fori_loop(0, lt, step, tuple(init), unroll=4)
        for n, c in enumerate(tiles):
            hre_ref[c] = fin[2 * n]
            him_ref[c] = fin[2 * n + 1]

    ys = []
    for j in range(n_slab):
        xr = jnp.concatenate([vx_ref[j * slab_tiles + i].astype(BF16) for i in range(slab_tiles)], axis=1)
        xi = jnp.concatenate([vx_ref[n_tiles + j * slab_tiles + i].astype(BF16) for i in range(slab_tiles)], axis=1)
        ys.append(_dot(xr, cre_ref[j]) + _dot(xi, cim_ref[j]))
    y = jnp.concatenate(ys, axis=1) + dskip_ref[...] * u2
    y = jax.nn.gelu(y)
    z = _dot(y.astype(BF16), wglu_ref[...]) + bglu_ref[...]
    out = y * jax.nn.sigmoid(z)
    for c in range(n_slab):
        io_ref[c] = out[:, c * LANES:(c + 1) * LANES]
    for b in range(batch):
        for c in range(n_slab):
            o_ref[b, :, c * LANES:(c + 1) * LANES] = io_ref[c, pl.ds(b, lt, stride=batch), :].astype(o_ref.dtype)


def _router_gates(logits_t, rbias):
    n_exp, tm = logits_t.shape
    per_group = n_exp // N_GROUPS
    scores = jax.nn.sigmoid(logits_t)
    sel = scores + rbias
    neg_inf = -jnp.inf
    sub_iota = lax.broadcasted_iota(jnp.int32, (per_group, tm), 0).astype(F32)
    group_score = []
    for g in range(N_GROUPS):
        sg = sel[g * per_group:(g + 1) * per_group, :]
        m1 = jnp.max(sg, axis=0, keepdims=True)
        first = jnp.min(jnp.where(sg == m1, sub_iota, float(per_group)), axis=0, keepdims=True)
        m2 = jnp.max(jnp.where(sub_iota == first, neg_inf, sg), axis=0, keepdims=True)
        group_score.append(m1 + m2)
    masked = []
    for g in range(N_GROUPS):
        rank = jnp.zeros((1, tm), F32)
        for g2 in range(N_GROUPS):
            if g2 == g:
                continue
            ahead = (group_score[g2] >= group_score[g]) if g2 < g else (group_score[g2] > group_score[g])
            rank = rank + jnp.where(ahead, 1.0, 0.0)
        keep = rank < float(TOP_GROUPS)
        masked.append(jnp.where(keep, sel[g * per_group:(g + 1) * per_group, :], neg_inf))
    cur = jnp.concatenate(masked, axis=0)
    iota = lax.broadcasted_iota(jnp.int32, (n_exp, tm), 0).astype(F32)
    chosen = jnp.zeros((n_exp, tm), F32)
    picks, weights = [], []
    for _ in range(TOP_K):
        m = jnp.max(cur, axis=0, keepdims=True)
        idx = jnp.min(jnp.where(cur == m, iota, float(n_exp)), axis=0, keepdims=True)
        pick = iota == idx
        chosen = jnp.where(pick, 1.0, chosen)
        cur = jnp.where(pick, neg_inf, cur)
        picks.append(idx)
        weights.append(jnp.sum(jnp.where(pick, scores, 0.0), axis=0, keepdims=True))
    idx_k = jnp.concatenate(picks, axis=0)
    w_k = jnp.concatenate(weights, axis=0)
    gate_k = w_k / jnp.sum(w_k, axis=0, keepdims=True) * ROUTED_SCALE
    return idx_k, gate_k, chosen


def _pack_bf16_pairs(x):
    n = x.shape[1] // 2
    hi = lax.bitcast_convert_type(x[:, :n].astype(BF16).astype(F32), jnp.int32)
    lo = lax.bitcast_convert_type(x[:, n:].astype(BF16).astype(F32), jnp.int32)
    return hi | lax.shift_right_logical(lo, 16)


def _unpack_bf16_pairs(p):
    hi = lax.bitcast_convert_type(p & jnp.int32(-65536), F32).astype(BF16)
    lo = lax.bitcast_convert_type(lax.shift_left(p, 16), F32).astype(BF16)
    return hi, lo


def _mix_kernel(att_ref, ssm_ref, xn_ref, ag_ref, sg_ref, woa_ref, wos_ref,
                g_ref, b_ref, wrh_ref, wrl_ref, rb_ref,
                x1_ref, x1p_ref, idx_ref, gate_ref, rank_ref, cnt_ref, carry_ref):
    @pl.when(pl.program_id(0) == 0)
    def _():
        carry_ref[...] = jnp.zeros_like(carry_ref)

    an = _rms_norm(att_ref[...].astype(F32), ag_ref[...]).astype(BF16)
    sn = _rms_norm(ssm_ref[...].astype(F32), sg_ref[...]).astype(BF16)
    mix = _dot(an, woa_ref[...]) + _dot(sn, wos_ref[...])
    x1 = _layer_norm(ALPHA * xn_ref[...] + mix, g_ref[...], b_ref[...])
    x1_ref[...] = x1
    x1p_ref[...] = _pack_bf16_pairs(x1)
    x_hi = x1.astype(BF16)
    x_lo = (x1 - x_hi.astype(F32)).astype(BF16)
    dg = functools.partial(lax.dot_general, dimension_numbers=NT_DIMS, preferred_element_type=F32)
    logits_t = dg(wrh_ref[...], x_hi) + dg(wrl_ref[...], x_hi) + dg(wrh_ref[...], x_lo)
    idx_k, gate_k, chosen = _router_gates(logits_t, rb_ref[...])
    idx_ref[...] = idx_k.astype(jnp.int32)
    gate_ref[...] = gate_k

    n_exp, tm = chosen.shape
    before = (lax.broadcasted_iota(jnp.int32, (tm, tm), 0)
              < lax.broadcasted_iota(jnp.int32, (tm, tm), 1))
    excl = _dot(chosen.astype(BF16), jnp.where(before, 1.0, 0.0).astype(BF16))
    rank_full = carry_ref[...] + excl
    iota = lax.broadcasted_iota(jnp.int32, (n_exp, tm), 0).astype(F32)
    ranks = [jnp.sum(jnp.where(iota == idx_k[k:k + 1, :], rank_full, 0.0), axis=0, keepdims=True)
             for k in range(TOP_K)]
    rank_ref[...] = jnp.concatenate(ranks, axis=0).astype(jnp.int32)
    total = carry_ref[...] + jnp.sum(chosen, axis=1, keepdims=True)
    carry_ref[...] = total
    cnt_ref[...] = jnp.broadcast_to(total, cnt_ref.shape)


def _positions_kernel(off_ref, idx_ref, rank_ref, pos_ref):
    idx = idx_ref[...]
    base = jnp.zeros(idx.shape, jnp.int32)
    for e in range(N_EXPERTS):
        base = jnp.where(idx == e, off_ref[e], base)
    pos_ref[...] = rank_ref[...] + base


def _swiglu(x_hi, x_lo, wg_ref, wu_ref, wd_ref):
    half = x_hi.shape[1]
    hg = _dot(x_hi, wg_ref[:half, :]) + _dot(x_lo, wg_ref[half:, :])
    hu = _dot(x_hi, wu_ref[:half, :]) + _dot(x_lo, wu_ref[half:, :])
    h = jax.nn.silu(hg) * hu
    return _dot(h.astype(BF16), wd_ref[...])


def _experts_kernel(te_ref, valid_ref, xs_ref, wg_ref, wu_ref, wd_ref, ys_ref,
                    wgb_ref, wub_ref, wdb_ref):
    i = pl.program_id(0)
    valid = valid_ref[i]

    @pl.when((i == 0) | (te_ref[i] != te_ref[jnp.maximum(i - 1, 0)]))
    def _():
        wgb_ref[...] = wg_ref[0].astype(BF16)
        wub_ref[...] = wu_ref[0].astype(BF16)
        wdb_ref[...] = wd_ref[0].astype(BF16)

    @pl.when(valid > 0)
    def _():
        rows = lax.broadcasted_iota(jnp.int32, xs_ref.shape, 0)
        x_hi, x_lo = _unpack_bf16_pairs(jnp.where(rows < valid, xs_ref[...], 0))
        ys_ref[...] = _pack_bf16_pairs(_swiglu(x_hi, x_lo, wgb_ref, wub_ref, wdb_ref))

    @pl.when(valid == 0)
    def _():
        ys_ref[...] = jnp.zeros_like(ys_ref)


def _combine_kernel(yg_ref, gate_ref, x1_ref, wsg_ref, wsu_ref, wsd_ref, g_ref, b_ref, o_ref):
    x1 = x1_ref[...]
    half = x1.shape[1] // 2
    acc = _swiglu(x1[:, :half].astype(BF16), x1[:, half:].astype(BF16), wsg_ref, wsu_ref, wsd_ref)
    gates = gate_ref[...]
    acc_hi, acc_lo = acc[:, :half], acc[:, half:]
    for k in range(TOP_K):
        y_hi, y_lo = _unpack_bf16_pairs(yg_ref[k])
        gk = gates[:, k:k + 1]
        acc_hi = acc_hi + gk * y_hi.astype(F32)
        acc_lo = acc_lo + gk * y_lo.astype(F32)
    ffn = jnp.concatenate([acc_hi, acc_lo], axis=1)
    o_ref[...] = _layer_norm(ALPHA * x1 + ffn, g_ref[...], b_ref[...])


def _combine_into_kernel(prev_ref, *refs):
    del prev_ref
    _combine_kernel(*refs)


SC_CORES = 2
SC_SUBCORES = 16
SC_WORKERS = SC_CORES * SC_SUBCORES
SC_CHUNK = 64


def _sc_mesh():
    return plsc.VectorSubcoreMesh(core_axis_name="c", subcore_axis_name="s")


def _sc_gather_rows(table, idx):
    n = idx.shape[0]
    d = table.shape[1]
    per_w = n // SC_WORKERS
    n_ch = per_w // SC_CHUNK

    @functools.partial(
        pl.kernel, mesh=_sc_mesh(),
        out_type=jax.ShapeDtypeStruct((n, d), table.dtype),
        scratch_types=[pltpu.VMEM((n_ch, SC_CHUNK), jnp.int32),
                       pltpu.VMEM((2, SC_CHUNK, d), table.dtype),
                       pltpu.SemaphoreType.DMA((2,)),
                       pltpu.SemaphoreType.DMA((2,))],
    )
    def k(table_hbm, idx_hbm, out_hbm, idx_v, buf, gsem, osem):
        wid = lax.axis_index("s") * SC_CORES + lax.axis_index("c")
        base = wid * per_w
        pltpu.sync_copy(idx_hbm.at[wid], idx_v)

        def gather(c, b):
            return pltpu.make_async_copy(table_hbm.at[idx_v.at[c]], buf.at[b], gsem.at[b])

        def put(c, b):
            return pltpu.make_async_copy(buf.at[b], out_hbm.at[pl.ds(base + c * SC_CHUNK, SC_CHUNK)],
                                         osem.at[b])

        gather(0, 0).start()

        @pl.loop(0, n_ch, step=2)
        def _(c):
            for b in range(2):
                cc = c + b
                gather(cc, b).wait()

                @pl.when(cc + 1 < n_ch)
                def _():
                    @pl.when(cc >= 1)
                    def _():
                        put(cc - 1, 1 - b).wait()
                    gather(cc + 1, 1 - b).start()

                put(cc, b).start()

        put(n_ch - 2, 0).wait()
        put(n_ch - 1, 1).wait()

    return k(table, idx.reshape(SC_WORKERS, n_ch, SC_CHUNK))


def _sc_scatter_rows(x, pos, n_out):
    t, d = x.shape
    kk = pos.shape[0]
    per_w = t // SC_WORKERS
    n_ch = per_w // SC_CHUNK
    pos_w = pos.reshape(kk, SC_WORKERS, n_ch, SC_CHUNK).transpose(1, 2, 0, 3)
    pos_w = pos_w.reshape(SC_WORKERS, n_ch * kk, SC_CHUNK)

    @functools.partial(
        pl.kernel, mesh=_sc_mesh(),
        out_type=jax.ShapeDtypeStruct((n_out, d), x.dtype),
        scratch_types=[pltpu.VMEM((n_ch * kk, SC_CHUNK), jnp.int32),
                       pltpu.VMEM((2, SC_CHUNK, d), x.dtype),
                       pltpu.SemaphoreType.DMA((2,)),
                       pltpu.SemaphoreType.DMA((2,))],
    )
    def k(x_hbm, pos_hbm, out_hbm, idx_v, buf, isem, osem):
        wid = lax.axis_index("s") * SC_CORES + lax.axis_index("c")
        base = wid * per_w
        pltpu.sync_copy(pos_hbm.at[wid], idx_v)

        def get(c, b):
            return pltpu.make_async_copy(x_hbm.at[pl.ds(base + c * SC_CHUNK, SC_CHUNK)], buf.at[b],
                                         isem.at[b])

        def put(c, j, b):
            return pltpu.make_async_copy(buf.at[b], out_hbm.at[idx_v.at[c * kk + j]], osem.at[b])

        get(0, 0).start()

        @pl.loop(0, n_ch, step=2)
        def _(c):
            for b in range(2):
                cc = c + b
                get(cc, b).wait()

                @pl.when(cc + 1 < n_ch)
                def _():
                    @pl.when(cc >= 1)
                    def _():
                        for j in range(kk):
                            put(cc - 1, j, 1 - b).wait()
                    get(cc + 1, 1 - b).start()

                for j in range(kk):
                    put(cc, j, b).start()

        for j in range(kk):
            put(n_ch - 2, j, 0).wait()
        for j in range(kk):
            put(n_ch - 1, j, 1).wait()

    return k(x, pos_w)


def _row(v):
    return v.reshape(1, -1).astype(F32)


def _const_spec(shape):
    nd = len(shape)
    return pl.BlockSpec(shape, lambda *_: (0,) * nd)


def _pad_heads(w, width):
    r, h, _ = w.shape
    return jnp.pad(w, ((0, 0), (0, 0), (0, HEAD_PAD - width))).reshape(r, h * HEAD_PAD)


def _half_rotate(w):
    half = QK_ROPE // 2
    return jnp.concatenate([-w[..., half:], w[..., :half]], axis=-1)


def kernel(x, positions, ln_in_g, ln_in_b, w_in, q_norm_g, kv_norm_g, w_uq, w_ukv, lambda_re, lambda_im, log_step, b_re, b_im, c_re, c_im, d_skip, w_glu, b_glu, attn_out_g, ssm_out_g, w_o, ln1_g, ln1_b, w_router, router_bias, w_gate, w_up, w_down, ws_gate, ws_up, ws_down, ln2_g, ln2_b):
    B, S, D = x.shape
    T = B * S
    l = 0
    ssm_width = w_glu.shape[-1]
    n_groups = ssm_width // SSM_GROUP
    n_state = n_groups * SSM_STATE
    mla_width = MLA_HEADS * V_DIM
    qk_pad = MLA_HEADS * HEAD_PAD
    cparams = functools.partial(pltpu.CompilerParams, vmem_limit_bytes=VMEM_LIMIT)

    s1, s2, s3 = Q_RANK, Q_RANK + KV_RANK, Q_RANK + KV_RANK + QK_ROPE
    wi = w_in[l]
    w_kr = wi[:, s2:s3]
    pad_rope = lambda w: jnp.pad(w, ((0, 0), (QK_NOPE, HEAD_PAD - QK_NOPE - QK_ROPE)))
    w1 = jnp.concatenate([wi[:, :s2], wi[:, s3:], pad_rope(w_kr), pad_rope(_half_rotate(w_kr))],
                         axis=1).astype(BF16)
    wq = w_uq[l]
    zeros_nope = jnp.zeros(wq.shape[:2] + (QK_NOPE,), wq.dtype)
    wq_main = _pad_heads(wq, QK_NOPE + QK_ROPE).astype(BF16)
    wq_rot = _pad_heads(jnp.concatenate([zeros_nope, _half_rotate(wq[..., QK_NOPE:])], axis=-1),
                        QK_NOPE + QK_ROPE).astype(BF16)
    wkv = w_ukv[l]
    wk = _pad_heads(wkv[..., :QK_NOPE], QK_NOPE).astype(BF16)
    wv = _pad_heads(wkv[..., QK_NOPE:], V_DIM).astype(BF16)
    half = QK_ROPE // 2
    inv_freq = ROPE_THETA ** (-jnp.arange(half, dtype=F32) / half)
    freq = jnp.pad(jnp.concatenate([inv_freq, inv_freq]),
                   (QK_NOPE, HEAD_PAD - QK_NOPE - QK_ROPE)).reshape(1, HEAD_PAD)
    pos_f = positions.astype(F32).reshape(T, 1)

    tm = min(512, T)
    w1_cols = w1.shape[1]
    tok = lambda width: pl.BlockSpec((tm, width), lambda i: (i, 0))
    xn, q, k, v, u = pl.pallas_call(
        functools.partial(_inproj_kernel, ssm_width=ssm_width),
        grid=(T // tm,),
        in_specs=[tok(D), tok(1), _const_spec((1, D)), _const_spec((1, D)),
                  _const_spec((D, w1_cols)), _const_spec((1, Q_RANK)), _const_spec((1, KV_RANK)),
                  _const_spec((Q_RANK, qk_pad)), _const_spec((Q_RANK, qk_pad)),
                  _const_spec((KV_RANK, qk_pad)), _const_spec((KV_RANK, qk_pad)),
                  _const_spec((1, HEAD_PAD))],
        out_specs=[tok(D), tok(qk_pad), tok(qk_pad), tok(qk_pad), tok(ssm_width)],
        out_shape=[jax.ShapeDtypeStruct((T, D), F32), jax.ShapeDtypeStruct((T, qk_pad), BF16),
                   jax.ShapeDtypeStruct((T, qk_pad), BF16), jax.ShapeDtypeStruct((T, qk_pad), BF16),
                   jax.ShapeDtypeStruct((T, ssm_width), F32)],
        compiler_params=cparams(dimension_semantics=("parallel",)),
        name="inproj",
    )(x.reshape(T, D), pos_f, _row(ln_in_g), _row(ln_in_b), w1, _row(q_norm_g[l]), _row(kv_norm_g[l]),
      wq_main, wq_rot, wk, wv, freq)

    tq = min(256, S)
    nq = S // tq
    att = pl.pallas_call(
        functools.partial(_attn_kernel, tq=tq, heads=MLA_HEADS),
        grid=(B, nq),
        in_specs=[pl.BlockSpec((tq, qk_pad), lambda b, i: (b * nq + i, 0)),
                  pl.BlockSpec((S, qk_pad), lambda b, i: (b, 0)),
                  pl.BlockSpec((S, qk_pad), lambda b, i: (b, 0))],
        out_specs=pl.BlockSpec((tq, mla_width), lambda b, i: (b * nq + i, 0)),
        out_shape=jax.ShapeDtypeStruct((T, mla_width), BF16),
        scratch_shapes=[pltpu.VMEM((MLA_HEADS, nq, tq, tq), F32),
                        pltpu.VMEM((MLA_HEADS, tq, LANES), F32),
                        pltpu.VMEM((MLA_HEADS, tq, HEAD_PAD), F32)],
        compiler_params=cparams(dimension_semantics=("parallel", "arbitrary")),
        name="attention",
    )(q, k, v)

    lam = lax.complex(jnp.minimum(lambda_re[l].astype(F32), -1e-4), lambda_im[l].astype(F32))
    step = jnp.exp(log_step[l].astype(F32))[:, None]
    lam_bar = jnp.exp(lam * step)
    b_bar = ((lam_bar - 1.0) / lam)[..., None] * lax.complex(b_re[l].astype(F32), b_im[l].astype(F32))
    n_slab = ssm_width // LANES
    g_per_slab = n_groups // n_slab
    eye = jnp.eye(n_groups, dtype=F32)

    def expand_in(bpart):
        return jnp.einsum('gpc,gh->gchp', bpart, eye).reshape(ssm_width, n_state)

    def expand_out(cpart):
        return jnp.einsum('gcp,gh->gphc', cpart, eye).reshape(n_state, ssm_width)

    slab = n_state // n_slab
    win_re, win_im = expand_in(jnp.real(b_bar)), expand_in(jnp.imag(b_bar))
    win = jnp.stack([jnp.concatenate([win_re[j * LANES:(j + 1) * LANES, j * slab:(j + 1) * slab],
                                      win_im[j * LANES:(j + 1) * LANES, j * slab:(j + 1) * slab]], axis=1)
                     for j in range(n_slab)]).astype(BF16)
    wc_re, wc_im = expand_out(c_re[l].astype(F32)), expand_out(-c_im[l].astype(F32))
    cre = jnp.stack([wc_re[j * slab:(j + 1) * slab, j * LANES:(j + 1) * LANES] for j in range(n_slab)]).astype(BF16)
    cim = jnp.stack([wc_im[j * slab:(j + 1) * slab, j * LANES:(j + 1) * LANES] for j in range(n_slab)]).astype(BF16)
    a_re = jnp.real(lam_bar).reshape(1, n_state)
    a_im = jnp.imag(lam_bar).reshape(1, n_state)

    lt = min(64, S)
    ssm = pl.pallas_call(
        functools.partial(_s5_kernel, batch=B, lt=lt, n_state=n_state, scan_tiles=4),
        grid=(S // lt,),
        in_specs=[pl.BlockSpec((B, lt, ssm_width), lambda t: (0, t, 0)),
                  _const_spec(win.shape), _const_spec((1, n_state)), _const_spec((1, n_state)),
                  _const_spec(cre.shape), _const_spec(cim.shape), _const_spec((1, ssm_width)),
                  _const_spec((ssm_width, ssm_width)), _const_spec((1, ssm_width))],
        out_specs=pl.BlockSpec((B, lt, ssm_width), lambda t: (0, t, 0)),
        out_shape=jax.ShapeDtypeStruct((B, S, ssm_width), BF16),
        scratch_shapes=[pltpu.VMEM((2 * n_state // LANES, B * lt, LANES), F32),
                        pltpu.VMEM((n_state // LANES, B, LANES), F32),
                        pltpu.VMEM((n_state // LANES, B, LANES), F32),
                        pltpu.VMEM((ssm_width // LANES, B * lt, LANES), F32),
                        pltpu.VMEM((B * lt, ssm_width), F32)],
        compiler_params=cparams(dimension_semantics=("arbitrary",)),
        name="s5",
    )(u.reshape(B, S, ssm_width), win, a_re, a_im, cre, cim, _row(d_skip[l]),
      w_glu[l].astype(BF16), _row(b_glu[l]))

    wo = w_o[l].astype(BF16)
    wr_t = w_router[l].T.astype(F32)
    wr_hi = wr_t.astype(BF16)
    wr_lo = (wr_t - wr_hi.astype(F32)).astype(BF16)
    half = D // 2
    n_slabs = MOE_SLABS
    ts = T // n_slabs
    nt = ts // tm
    ssm2 = ssm.reshape(T, ssm_width)
    rbias = router_bias[l].astype(F32).reshape(N_EXPERTS, 1)
    kt = lambda dt: jax.ShapeDtypeStruct((TOP_K, ts), dt)
    k_spec = pl.BlockSpec((TOP_K, tm), lambda i: (0, i))

    def route(s):
        tok_s = lambda width: pl.BlockSpec((tm, width), lambda i: (i + s * nt, 0))
        return pl.pallas_call(
            _mix_kernel,
            grid=(nt,),
            in_specs=[tok_s(mla_width), tok_s(ssm_width), tok_s(D), _const_spec((1, mla_width)),
                      _const_spec((1, ssm_width)), _const_spec((mla_width, D)),
                      _const_spec((ssm_width, D)), _const_spec((1, D)), _const_spec((1, D)),
                      _const_spec((N_EXPERTS, D)), _const_spec((N_EXPERTS, D)),
                      _const_spec((N_EXPERTS, 1))],
            out_specs=[tok(D), tok(half), k_spec, k_spec, k_spec, _const_spec((N_EXPERTS, LANES))],
            out_shape=[jax.ShapeDtypeStruct((ts, D), F32), jax.ShapeDtypeStruct((ts, half), jnp.int32),
                       kt(jnp.int32), kt(F32), kt(jnp.int32),
                       jax.ShapeDtypeStruct((N_EXPERTS, LANES), F32)],
            scratch_shapes=[pltpu.VMEM((N_EXPERTS, 1), F32)],
            compiler_params=cparams(dimension_semantics=("arbitrary",)),
            name="mix_router",
        )(att, ssm2, xn, _row(attn_out_g[l]), _row(ssm_out_g[l]), wo[:mla_width], wo[mla_width:],
          _row(ln1_g[l]), _row(ln1_b[l]), wr_hi, wr_lo, rbias)

    tr = EXPERT_ROW_TILE
    n_tiles = (ts * TOP_K) // tr + N_EXPERTS
    n_rows = n_tiles * tr
    tp = min(2048, ts)

    def dispatch(x1p, idx_k, rank_k, counts):
        cnt = counts[:, 0].astype(jnp.int32)
        tiles_e = (cnt + tr - 1) // tr
        tile_end = jnp.cumsum(tiles_e)
        tile_start = tile_end - tiles_e
        tile_ids = jnp.arange(n_tiles, dtype=jnp.int32)
        tile_expert = jnp.sum((tile_end[None, :] <= tile_ids[:, None]).astype(jnp.int32), axis=1)
        tile_expert = jnp.minimum(tile_expert, N_EXPERTS - 1)
        owner = (tile_start[None, :] <= tile_ids[:, None]) & (tile_ids[:, None] < tile_end[None, :])
        left = jnp.sum(jnp.where(owner, cnt[None, :] - (tile_ids[:, None] - tile_start[None, :]) * tr, 0),
                       axis=1)
        tile_valid = jnp.clip(left, 0, tr).astype(jnp.int32)
        pos = pl.pallas_call(
            _positions_kernel,
            grid_spec=pltpu.PrefetchScalarGridSpec(
                num_scalar_prefetch=1, grid=(ts // tp,),
                in_specs=[pl.BlockSpec((TOP_K, tp), lambda i, off: (0, i)),
                          pl.BlockSpec((TOP_K, tp), lambda i, off: (0, i))],
                out_specs=pl.BlockSpec((TOP_K, tp), lambda i, off: (0, i))),
            out_shape=kt(jnp.int32),
            name="positions",
        )((tile_start * tr).astype(jnp.int32), idx_k, rank_k)
        return _sc_scatter_rows(x1p, pos, n_rows), pos, tile_expert, tile_valid

    wg, wu, wd = w_gate[l], w_up[l], w_down[l]
    ff = wg.shape[-1]

    def experts(xs, tile_expert, tile_valid):
        return pl.pallas_call(
            _experts_kernel,
            grid_spec=pltpu.PrefetchScalarGridSpec(
                num_scalar_prefetch=2, grid=(n_tiles,),
                in_specs=[pl.BlockSpec((tr, half), lambda i, te, tv: (i, 0)),
                          pl.BlockSpec((1, D, ff), lambda i, te, tv: (te[i], 0, 0)),
                          pl.BlockSpec((1, D, ff), lambda i, te, tv: (te[i], 0, 0)),
                          pl.BlockSpec((1, ff, D), lambda i, te, tv: (te[i], 0, 0))],
                out_specs=pl.BlockSpec((tr, half), lambda i, te, tv: (i, 0)),
                scratch_shapes=[pltpu.VMEM((D, ff), BF16), pltpu.VMEM((D, ff), BF16),
                                pltpu.VMEM((ff, D), BF16)]),
            out_shape=jax.ShapeDtypeStruct((n_rows, half), jnp.int32),
            compiler_params=cparams(dimension_semantics=("arbitrary",)),
            name="experts",
        )(tile_expert, tile_valid, xs, wg, wu, wd)

    shared = (ws_gate[l].astype(BF16), ws_up[l].astype(BF16), ws_down[l].astype(BF16))

    def combine(s, out_so_far, yg, gate_k, x1):
        specs = [pl.BlockSpec((TOP_K, tm, half), lambda i: (0, i, 0)),
                 pl.BlockSpec((tm, TOP_K), lambda i: (i, 0)), tok(D),
                 _const_spec((D, ff)), _const_spec((D, ff)), _const_spec((ff, D)),
                 _const_spec((1, D)), _const_spec((1, D))]
        args = (yg.reshape(TOP_K, ts, half), gate_k.T, x1, *shared, _row(ln2_g[l]), _row(ln2_b[l]))
        body, aliases = _combine_kernel, {}
        if out_so_far is not None:
            specs = [pl.BlockSpec(memory_space=pl.ANY)] + specs
            args = (out_so_far,) + args
            body, aliases = _combine_into_kernel, {0: 0}
        return pl.pallas_call(
            body,
            grid=(nt,),
            in_specs=specs,
            out_specs=pl.BlockSpec((tm, D), lambda i: (i + s * nt, 0)),
            out_shape=jax.ShapeDtypeStruct((T, D), F32),
            input_output_aliases=aliases,
            compiler_params=cparams(dimension_semantics=("parallel",)),
            name="combine",
        )(*args)

    routed, moved = [], []
    for s in range(n_slabs):
        x1, x1p, idx_k, gate_k, rank_k, counts = route(s)
        routed.append((x1, gate_k))
        moved.append(dispatch(x1p, idx_k, rank_k, counts))
    gathered = []
    for xs, pos, tile_expert, tile_valid in moved:
        gathered.append(_sc_gather_rows(experts(xs, tile_expert, tile_valid), pos.reshape(TOP_K * ts)))
    out = None
    for s in range(n_slabs):
        out = combine(s, out, gathered[s], routed[s][1], routed[s][0])
    return out.reshape(B, S, D)
```

```python
import functools

import jax
import jax.numpy as jnp
from jax import lax
from jax.experimental import pallas as pl
from jax.experimental.pallas import tpu as pltpu
from jax.experimental.pallas import tpu_sc as plsc

CHUNK = 64
MLA_HEADS = 8
QK_NOPE = 64
QK_ROPE = 32
V_DIM = 64
Q_RANK = 256
KV_RANK = 128
ROPE_THETA = 10000.0
SSM_GROUP = 16
SSM_STATE = 64
N_EXPERTS = 64
TOP_K = 8
N_GROUPS = 8
TOP_GROUPS = 4
ROUTED_SCALE = 2.5
DEPTH = 1
ALPHA = (2.0 * DEPTH) ** 0.25
EPS = 1e-5
LOG2_E = 1.4426950408889634

LANES = 128
HEAD_PAD = LANES
VMEM_LIMIT = 56 * 1024 * 1024
EXPERT_ROW_TILE = 1024
MOE_SLABS = 2

BF16 = jnp.bfloat16
F32 = jnp.float32
NT_DIMS = (((1,), (1,)), ((), ()))


def _dot(a, b):
    return jnp.dot(a, b, preferred_element_type=F32)


def _layer_norm(x, g, b):
    mu = jnp.mean(x, axis=-1, keepdims=True)
    xc = x - mu
    var = jnp.mean(xc * xc, axis=-1, keepdims=True)
    return xc * lax.rsqrt(var + EPS) * g + b


def _rms_norm(x, g):
    return x * lax.rsqrt(jnp.mean(x * x, axis=-1, keepdims=True) + EPS) * g


def _inproj_kernel(x_ref, pos_ref, lng_ref, lnb_ref, w1_ref, qg_ref, kvg_ref,
                   wq_ref, wqr_ref, wk_ref, wv_ref, freq_ref,
                   xn_ref, q_ref, k_ref, v_ref, u_ref, *, ssm_width):
    xn = _layer_norm(x_ref[...], lng_ref[...], lnb_ref[...])
    xn_ref[...] = xn
    h = _dot(xn.astype(BF16), w1_ref[...])
    o1 = Q_RANK
    o2 = o1 + KV_RANK
    o3 = o2 + ssm_width
    o4 = o3 + HEAD_PAD
    cq = h[:, :o1]
    ckv = h[:, o1:o2]
    u_ref[...] = h[:, o2:o3]
    kr_raw = h[:, o3:o4]
    kr_rot = h[:, o4:o4 + HEAD_PAD]
    cqn = _rms_norm(cq, qg_ref[...]).astype(BF16)
    ckvn = _rms_norm(ckv, kvg_ref[...]).astype(BF16)

    ang = pos_ref[...] * freq_ref[...]
    c = jnp.cos(ang)
    s = jnp.sin(ang)
    lane = lax.broadcasted_iota(jnp.int32, (1, HEAD_PAD), 1)
    scale = (QK_NOPE + QK_ROPE) ** -0.5 * LOG2_E
    is_rope = (lane >= QK_NOPE) & (lane < QK_NOPE + QK_ROPE)
    cos1 = jnp.where(lane < QK_NOPE, 1.0, jnp.where(is_rope, c, 0.0)) * scale
    sin1 = jnp.where(is_rope, s, 0.0) * scale
    cos_t = jnp.concatenate([cos1] * MLA_HEADS, axis=1)
    sin_t = jnp.concatenate([sin1] * MLA_HEADS, axis=1)
    q = _dot(cqn, wq_ref[...]) * cos_t + _dot(cqn, wqr_ref[...]) * sin_t
    q_ref[...] = q.astype(BF16)

    kr = kr_raw * c + kr_rot * s
    k = _dot(ckvn, wk_ref[...]) + jnp.concatenate([kr] * MLA_HEADS, axis=1)
    k_ref[...] = k.astype(BF16)
    ones_col = jnp.concatenate([jnp.where(lane == V_DIM, 1.0, 0.0)] * MLA_HEADS, axis=1)
    v_ref[...] = (_dot(ckvn, wv_ref[...]) + ones_col).astype(BF16)


def _attn_kernel(q_ref, k_ref, v_ref, o_ref, s_ref, mx_ref, acc_ref, *, tq, heads):
    qi = pl.program_id(1)
    row_chunk = lax.broadcasted_iota(jnp.int32, (tq, tq), 0) // CHUNK
    col_chunk = lax.broadcasted_iota(jnp.int32, (tq, tq), 1) // CHUNK
    diag_mask = row_chunk >= col_chunk
    head_cols = [slice(h * HEAD_PAD, (h + 1) * HEAD_PAD) for h in range(heads)]
    mx_ref[...] = jnp.full(mx_ref.shape, -jnp.inf, F32)
    acc_ref[...] = jnp.zeros(acc_ref.shape, F32)

    def scores(j, masked):
        start = pl.multiple_of(j * tq, tq)
        for h in range(heads):
            s = lax.dot_general(q_ref[:, head_cols[h]], k_ref[pl.ds(start, tq), head_cols[h]], NT_DIMS,
                                preferred_element_type=F32)
            if masked:
                s = jnp.where(diag_mask, s, -jnp.inf)
            s_ref[h, j] = s
            lane_max = s[:, :LANES]
            for c in range(1, tq // LANES):
                lane_max = jnp.maximum(lane_max, s[:, c * LANES:(c + 1) * LANES])
            mx_ref[h] = jnp.maximum(mx_ref[h], lane_max)

    def scores_step(j, carry):
        scores(j, False)
        return carry

    lax.fori_loop(0, qi, scores_step, 0)
    scores(qi, True)
    row_max = [jnp.max(mx_ref[h], axis=-1, keepdims=True) for h in range(heads)]

    def values_step(j, carry):
        start = pl.multiple_of(j * tq, tq)
        for h in range(heads):
            p = jnp.exp2(s_ref[h, j] - row_max[h]).astype(BF16)
            acc_ref[h] += _dot(p, v_ref[pl.ds(start, tq), head_cols[h]])
        return carry

    lax.fori_loop(0, qi + 1, values_step, 0)
    outs = [acc_ref[h][:, :V_DIM] / acc_ref[h][:, V_DIM:V_DIM + 1] for h in range(heads)]
    o_ref[...] = jnp.concatenate(outs, axis=1).astype(o_ref.dtype)


def _s5_kernel(u_ref, win_ref, are_ref, aim_ref, cre_ref, cim_ref, dskip_ref,
               wglu_ref, bglu_ref, o_ref, vx_ref, hre_ref, him_ref, io_ref, utm_ref, *,
               batch, lt, n_state, scan_tiles):
    ti = pl.program_id(0)

    @pl.when(ti == 0)
    def _():
        hre_ref[...] = jnp.zeros_like(hre_ref)
        him_ref[...] = jnp.zeros_like(him_ref)

    width = u_ref.shape[-1]
    n_slab = width // LANES
    n_tiles = n_state // LANES
    slab_tiles = n_tiles // n_slab
    for b in range(batch):
        for c in range(n_slab):
            io_ref[c, b * lt:(b + 1) * lt, :] = u_ref[b, :, c * LANES:(c + 1) * LANES]

    def to_time_major(t, _):
        dst = pl.ds(pl.multiple_of(t * batch, batch), batch)
        for c in range(n_slab):
            utm_ref[dst, c * LANES:(c + 1) * LANES] = io_ref[c, pl.ds(t, batch, stride=lt), :]
        return 0

    lax.fori_loop(0, lt, to_time_major, 0, unroll=4)
    u2 = utm_ref[...]
    ub = u2.astype(BF16)
    for j in range(n_slab):
        vj = _dot(ub[:, j * LANES:(j + 1) * LANES], win_ref[j])
        for i in range(slab_tiles):
            vx_ref[j * slab_tiles + i] = vj[:, i * LANES:(i + 1) * LANES]
            vx_ref[n_tiles + j * slab_tiles + i] = vj[:, (slab_tiles + i) * LANES:(slab_tiles + i + 1) * LANES]

    for c0 in range(0, n_tiles, scan_tiles):
        tiles = range(c0, c0 + scan_tiles)
        ar = [jnp.broadcast_to(are_ref[:, c * LANES:(c + 1) * LANES], (batch, LANES)) for c in tiles]
        ai = [jnp.broadcast_to(aim_ref[:, c * LANES:(c + 1) * LANES], (batch, LANES)) for c in tiles]

        def step(t, carry, tiles=tiles, ar=ar, ai=ai):
            rows = pl.ds(pl.multiple_of(t * batch, batch), batch)
            out = []
            for n, c in enumerate(tiles):
                hr, hi = carry[2 * n], carry[2 * n + 1]
                nr = ar[n] * hr - ai[n] * hi + vx_ref[c, rows, :]
                ni = ar[n] * hi + ai[n] * hr + vx_ref[n_tiles + c, rows, :]
                vx_ref[c, rows, :] = nr
                vx_ref[n_tiles + c, rows, :] = ni
                out += [nr, ni]
            return tuple(out)

        init = []
        for c in tiles:
            init += [hre_ref[c], him_ref[c]]
        fin = lax.fori_loop(0, lt, step, tuple(init), unroll=4)
        for n, c in enumerate(tiles):
            hre_ref[c] = fin[2 * n]
            him_ref[c] = fin[2 * n + 1]

    ys = []
    for j in range(n_slab):
        xr = jnp.concatenate([vx_ref[j * slab_tiles + i].astype(BF16) for i in range(slab_tiles)], axis=1)
        xi = jnp.concatenate([vx_ref[n_tiles + j * slab_tiles + i].astype(BF16) for i in range(slab_tiles)], axis=1)
        ys.append(_dot(xr, cre_ref[j]) + _dot(xi, cim_ref[j]))
    y = jnp.concatenate(ys, axis=1) + dskip_ref[...] * u2
    y = jax.nn.gelu(y)
    z = _dot(y.astype(BF16), wglu_ref[...]) + bglu_ref[...]
    out = y * jax.nn.sigmoid(z)
    for c in range(n_slab):
        io_ref[c] = out[:, c * LANES:(c + 1) * LANES]
    for b in range(batch):
        for c in range(n_slab):
            o_ref[b, :, c * LANES:(c + 1) * LANES] = io_ref[c, pl.ds(b, lt, stride=batch), :].astype(o_ref.dtype)


def _router_gates(logits_t, rbias):
    n_exp, tm = logits_t.shape
    per_group = n_exp // N_GROUPS
    scores = jax.nn.sigmoid(logits_t)
    sel = scores + rbias
    neg_inf = -jnp.inf
    sub_iota = lax.broadcasted_iota(jnp.int32, (per_group, tm), 0).astype(F32)
    group_score = []
    for g in range(N_GROUPS):
        sg = sel[g * per_group:(g + 1) * per_group, :]
        m1 = jnp.max(sg, axis=0, keepdims=True)
        first = jnp.min(jnp.where(sg == m1, sub_iota, float(per_group)), axis=0, keepdims=True)
        m2 = jnp.max(jnp.where(sub_iota == first, neg_inf, sg), axis=0, keepdims=True)
        group_score.append(m1 + m2)
    masked = []
    for g in range(N_GROUPS):
        rank = jnp.zeros((1, tm), F32)
        for g2 in range(N_GROUPS):
            if g2 == g:
                continue
            ahead = (group_score[g2] >= group_score[g]) if g2 < g else (group_score[g2] > group_score[g])
            rank = rank + jnp.where(ahead, 1.0, 0.0)
        keep = rank < float(TOP_GROUPS)
        masked.append(jnp.where(keep, sel[g * per_group:(g + 1) * per_group, :], neg_inf))
    cur = jnp.concatenate(masked, axis=0)
    iota = lax.broadcasted_iota(jnp.int32, (n_exp, tm), 0).astype(F32)
    chosen = jnp.zeros((n_exp, tm), F32)
    picks, weights = [], []
    for _ in range(TOP_K):
        m = jnp.max(cur, axis=0, keepdims=True)
        idx = jnp.min(jnp.where(cur == m, iota, float(n_exp)), axis=0, keepdims=True)
        pick = iota == idx
        chosen = jnp.where(pick, 1.0, chosen)
        cur = jnp.where(pick, neg_inf, cur)
        picks.append(idx)
        weights.append(jnp.sum(jnp.where(pick, scores, 0.0), axis=0, keepdims=True))
    idx_k = jnp.concatenate(picks, axis=0)
    w_k = jnp.concatenate(weights, axis=0)
    gate_k = w_k / jnp.sum(w_k, axis=0, keepdims=True) * ROUTED_SCALE
    return idx_k, gate_k, chosen


def _pack_bf16_pairs(x):
    n = x.shape[1] // 2
    hi = lax.bitcast_convert_type(x[:, :n].astype(BF16).astype(F32), jnp.int32)
    lo = lax.bitcast_convert_type(x[:, n:].astype(BF16).astype(F32), jnp.int32)
    return hi | lax.shift_right_logical(lo, 16)


def _unpack_bf16_pairs(p):
    hi = lax.bitcast_convert_type(p & jnp.int32(-65536), F32).astype(BF16)
    lo = lax.bitcast_convert_type(lax.shift_left(p, 16), F32).astype(BF16)
    return hi, lo


def _mix_kernel(att_ref, ssm_ref, xn_ref, ag_ref, sg_ref, woa_ref, wos_ref,
                g_ref, b_ref, wrh_ref, wrl_ref, rb_ref,
                x1_ref, x1p_ref, idx_ref, gate_ref, rank_ref, cnt_ref, carry_ref):
    @pl.when(pl.program_id(0) == 0)
    def _():
        carry_ref[...] = jnp.zeros_like(carry_ref)

    an = _rms_norm(att_ref[...].astype(F32), ag_ref[...]).astype(BF16)
    sn = _rms_norm(ssm_ref[...].astype(F32), sg_ref[...]).astype(BF16)
    mix = _dot(an, woa_ref[...]) + _dot(sn, wos_ref[...])
    x1 = _layer_norm(ALPHA * xn_ref[...] + mix, g_ref[...], b_ref[...])
    x1_ref[...] = x1
    x1p_ref[...] = _pack_bf16_pairs(x1)
    x_hi = x1.astype(BF16)
    x_lo = (x1 - x_hi.astype(F32)).astype(BF16)
    dg = functools.partial(lax.dot_general, dimension_numbers=NT_DIMS, preferred_element_type=F32)
    logits_t = dg(wrh_ref[...], x_hi) + dg(wrl_ref[...], x_hi) + dg(wrh_ref[...], x_lo)
    idx_k, gate_k, chosen = _router_gates(logits_t, rb_ref[...])
    idx_ref[...] = idx_k.astype(jnp.int32)
    gate_ref[...] = gate_k

    n_exp, tm = chosen.shape
    before = (lax.broadcasted_iota(jnp.int32, (tm, tm), 0)
              < lax.broadcasted_iota(jnp.int32, (tm, tm), 1))
    excl = _dot(chosen.astype(BF16), jnp.where(before, 1.0, 0.0).astype(BF16))
    rank_full = carry_ref[...] + excl
    iota = lax.broadcasted_iota(jnp.int32, (n_exp, tm), 0).astype(F32)
    ranks = [jnp.sum(jnp.where(iota == idx_k[k:k + 1, :], rank_full, 0.0), axis=0, keepdims=True)
             for k in range(TOP_K)]
    rank_ref[...] = jnp.concatenate(ranks, axis=0).astype(jnp.int32)
    total = carry_ref[...] + jnp.sum(chosen, axis=1, keepdims=True)
    carry_ref[...] = total
    cnt_ref[...] = jnp.broadcast_to(total, cnt_ref.shape)


def _positions_kernel(off_ref, idx_ref, rank_ref, pos_ref):
    idx = idx_ref[...]
    base = jnp.zeros(idx.shape, jnp.int32)
    for e in range(N_EXPERTS):
        base = jnp.where(idx == e, off_ref[e], base)
    pos_ref[...] = rank_ref[...] + base


def _swiglu(x_hi, x_lo, wg_ref, wu_ref, wd_ref):
    half = x_hi.shape[1]
    hg = _dot(x_hi, wg_ref[:half, :]) + _dot(x_lo, wg_ref[half:, :])
    hu = _dot(x_hi, wu_ref[:half, :]) + _dot(x_lo, wu_ref[half:, :])
    h = jax.nn.silu(hg) * hu
    return _dot(h.astype(BF16), wd_ref[...])


def _experts_kernel(te_ref, valid_ref, xs_ref, wg_ref, wu_ref, wd_ref, ys_ref,
                    wgb_ref, wub_ref, wdb_ref):
    i = pl.program_id(0)
    valid = valid_ref[i]

    @pl.when((i == 0) | (te_ref[i] != te_ref[jnp.maximum(i - 1, 0)]))
    def _():
        wgb_ref[...] = wg_ref[0].astype(BF16)
        wub_ref[...] = wu_ref[0].astype(BF16)
        wdb_ref[...] = wd_ref[0].astype(BF16)

    @pl.when(valid > 0)
    def _():
        rows = lax.broadcasted_iota(jnp.int32, xs_ref.shape, 0)
        x_hi, x_lo = _unpack_bf16_pairs(jnp.where(rows < valid, xs_ref[...], 0))
        ys_ref[...] = _pack_bf16_pairs(_swiglu(x_hi, x_lo, wgb_ref, wub_ref, wdb_ref))

    @pl.when(valid == 0)
    def _():
        ys_ref[...] = jnp.zeros_like(ys_ref)


def _combine_kernel(yg_ref, gate_ref, x1_ref, wsg_ref, wsu_ref, wsd_ref, g_ref, b_ref, o_ref):
    x1 = x1_ref[...]
    half = x1.shape[1] // 2
    acc = _swiglu(x1[:, :half].astype(BF16), x1[:, half:].astype(BF16), wsg_ref, wsu_ref, wsd_ref)
    gates = gate_ref[...]
    acc_hi, acc_lo = acc[:, :half], acc[:, half:]
    for k in range(TOP_K):
        y_hi, y_lo = _unpack_bf16_pairs(yg_ref[k])
        gk = gates[:, k:k + 1]
        acc_hi = acc_hi + gk * y_hi.astype(F32)
        acc_lo = acc_lo + gk * y_lo.astype(F32)
    ffn = jnp.concatenate([acc_hi, acc_lo], axis=1)
    o_ref[...] = _layer_norm(ALPHA * x1 + ffn, g_ref[...], b_ref[...])


def _combine_into_kernel(prev_ref, *refs):
    del prev_ref
    _combine_kernel(*refs)


SC_CORES = 2
SC_SUBCORES = 16
SC_WORKERS = SC_CORES * SC_SUBCORES
SC_CHUNK = 64


def _sc_mesh():
    return plsc.VectorSubcoreMesh(core_axis_name="c", subcore_axis_name="s")


def _sc_gather_rows(table, idx):
    n = idx.shape[0]
    d = table.shape[1]
    per_w = n // SC_WORKERS
    n_ch = per_w // SC_CHUNK

    @functools.partial(
        pl.kernel, mesh=_sc_mesh(),
        out_type=jax.ShapeDtypeStruct((n, d), table.dtype),
        scratch_types=[pltpu.VMEM((n_ch, SC_CHUNK), jnp.int32),
                       pltpu.VMEM((2, SC_CHUNK, d), table.dtype),
                       pltpu.SemaphoreType.DMA((2,)),
                       pltpu.SemaphoreType.DMA((2,))],
    )
    def k(table_hbm, idx_hbm, out_hbm, idx_v, buf, gsem, osem):
        wid = lax.axis_index("s") * SC_CORES + lax.axis_index("c")
        base = wid * per_w
        pltpu.sync_copy(idx_hbm.at[wid], idx_v)

        def gather(c, b):
            return pltpu.make_async_copy(table_hbm.at[idx_v.at[c]], buf.at[b], gsem.at[b])

        def put(c, b):
            return pltpu.make_async_copy(buf.at[b], out_hbm.at[pl.ds(base + c * SC_CHUNK, SC_CHUNK)],
                                         osem.at[b])

        gather(0, 0).start()

        @pl.loop(0, n_ch, step=2)
        def _(c):
            for b in range(2):
                cc = c + b
                gather(cc, b).wait()

                @pl.when(cc + 1 < n_ch)
                def _():
                    @pl.when(cc >= 1)
                    def _():
                        put(cc - 1, 1 - b).wait()
                    gather(cc + 1, 1 - b).start()

                put(cc, b).start()

        put(n_ch - 2, 0).wait()
        put(n_ch - 1, 1).wait()

    return k(table, idx.reshape(SC_WORKERS, n_ch, SC_CHUNK))


def _sc_scatter_rows(x, pos, n_out):
    t, d = x.shape
    kk = pos.shape[0]
    per_w = t // SC_WORKERS
    n_ch = per_w // SC_CHUNK
    pos_w = pos.reshape(kk, SC_WORKERS, n_ch, SC_CHUNK).transpose(1, 2, 0, 3)
    pos_w = pos_w.reshape(SC_WORKERS, n_ch * kk, SC_CHUNK)

    @functools.partial(
        pl.kernel, mesh=_sc_mesh(),
        out_type=jax.ShapeDtypeStruct((n_out, d), x.dtype),
        scratch_types=[pltpu.VMEM((n_ch * kk, SC_CHUNK), jnp.int32),
                       pltpu.VMEM((2, SC_CHUNK, d), x.dtype),
                       pltpu.SemaphoreType.DMA((2,)),
                       pltpu.SemaphoreType.DMA((2,))],
    )
    def k(x_hbm, pos_hbm, out_hbm, idx_v, buf, isem, osem):
        wid = lax.axis_index("s") * SC_CORES + lax.axis_index("c")
        base = wid * per_w
        pltpu.sync_copy(pos_hbm.at[wid], idx_v)

        def get(c, b):
            return pltpu.make_async_copy(x_hbm.at[pl.ds(base + c * SC_CHUNK, SC_CHUNK)], buf.at[b],
                                         isem.at[b])

        def put(c, j, b):
            return pltpu.make_async_copy(buf.at[b], out_hbm.at[idx_v.at[c * kk + j]], osem.at[b])

        get(0, 0).start()

        @pl.loop(0, n_ch, step=2)
        def _(c):
            for b in range(2):
                cc = c + b
                get(cc, b).wait()

                @pl.when(cc + 1 < n_ch)
                def _():
                    @pl.when(cc >= 1)
                    def _():
                        for j in range(kk):
                            put(cc - 1, j, 1 - b).wait()
                    get(cc + 1, 1 - b).start()

                for j in range(kk):
                    put(cc, j, b).start()

        for j in range(kk):
            put(n_ch - 2, j, 0).wait()
        for j in range(kk):
            put(n_ch - 1, j, 1).wait()

    return k(x, pos_w)


def _row(v):
    return v.reshape(1, -1).astype(F32)


def _const_spec(shape):
    nd = len(shape)
    return pl.BlockSpec(shape, lambda *_: (0,) * nd)


def _pad_heads(w, width):
    r, h, _ = w.shape
    return jnp.pad(w, ((0, 0), (0, 0), (0, HEAD_PAD - width))).reshape(r, h * HEAD_PAD)


def _half_rotate(w):
    half = QK_ROPE // 2
    return jnp.concatenate([-w[..., half:], w[..., :half]], axis=-1)


def kernel(x, positions, ln_in_g, ln_in_b, w_in, q_norm_g, kv_norm_g, w_uq, w_ukv, lambda_re, lambda_im, log_step, b_re, b_im, c_re, c_im, d_skip, w_glu, b_glu, attn_out_g, ssm_out_g, w_o, ln1_g, ln1_b, w_router, router_bias, w_gate, w_up, w_down, ws_gate, ws_up, ws_down, ln2_g, ln2_b):
    B, S, D = x.shape
    T = B * S
    l = 0
    ssm_width = w_glu.shape[-1]
    n_groups = ssm_width // SSM_GROUP
    n_state = n_groups * SSM_STATE
    mla_width = MLA_HEADS * V_DIM
    qk_pad = MLA_HEADS * HEAD_PAD
    cparams = functools.partial(pltpu.CompilerParams, vmem_limit_bytes=VMEM_LIMIT)

    s1, s2, s3 = Q_RANK, Q_RANK + KV_RANK, Q_RANK + KV_RANK + QK_ROPE
    wi = w_in[l]
    w_kr = wi[:, s2:s3]
    pad_rope = lambda w: jnp.pad(w, ((0, 0), (QK_NOPE, HEAD_PAD - QK_NOPE - QK_ROPE)))
    w1 = jnp.concatenate([wi[:, :s2], wi[:, s3:], pad_rope(w_kr), pad_rope(_half_rotate(w_kr))],
                         axis=1).astype(BF16)
    wq = w_uq[l]
    zeros_nope = jnp.zeros(wq.shape[:2] + (QK_NOPE,), wq.dtype)
    wq_main = _pad_heads(wq, QK_NOPE + QK_ROPE).astype(BF16)
    wq_rot = _pad_heads(jnp.concatenate([zeros_nope, _half_rotate(wq[..., QK_NOPE:])], axis=-1),
                        QK_NOPE + QK_ROPE).astype(BF16)
    wkv = w_ukv[l]
    wk = _pad_heads(wkv[..., :QK_NOPE], QK_NOPE).astype(BF16)
    wv = _pad_heads(wkv[..., QK_NOPE:], V_DIM).astype(BF16)
    half = QK_ROPE // 2
    inv_freq = ROPE_THETA ** (-jnp.arange(half, dtype=F32) / half)
    freq = jnp.pad(jnp.concatenate([inv_freq, inv_freq]),
                   (QK_NOPE, HEAD_PAD - QK_NOPE - QK_ROPE)).reshape(1, HEAD_PAD)
    pos_f = positions.astype(F32).reshape(T, 1)

    tm = min(512, T)
    w1_cols = w1.shape[1]
    tok = lambda width: pl.BlockSpec((tm, width), lambda i: (i, 0))
    xn, q, k, v, u = pl.pallas_call(
        functools.partial(_inproj_kernel, ssm_width=ssm_width),
        grid=(T // tm,),
        in_specs=[tok(D), tok(1), _const_spec((1, D)), _const_spec((1, D)),
                  _const_spec((D, w1_cols)), _const_spec((1, Q_RANK)), _const_spec((1, KV_RANK)),
                  _const_spec((Q_RANK, qk_pad)), _const_spec((Q_RANK, qk_pad)),
                  _const_spec((KV_RANK, qk_pad)), _const_spec((KV_RANK, qk_pad)),
                  _const_spec((1, HEAD_PAD))],
        out_specs=[tok(D), tok(qk_pad), tok(qk_pad), tok(qk_pad), tok(ssm_width)],
        out_shape=[jax.ShapeDtypeStruct((T, D), F32), jax.ShapeDtypeStruct((T, qk_pad), BF16),
                   jax.ShapeDtypeStruct((T, qk_pad), BF16), jax.ShapeDtypeStruct((T, qk_pad), BF16),
                   jax.ShapeDtypeStruct((T, ssm_width), F32)],
        compiler_params=cparams(dimension_semantics=("parallel",)),
        name="inproj",
    )(x.reshape(T, D), pos_f, _row(ln_in_g), _row(ln_in_b), w1, _row(q_norm_g[l]), _row(kv_norm_g[l]),
      wq_main, wq_rot, wk, wv, freq)

    tq = min(256, S)
    nq = S // tq
    att = pl.pallas_call(
        functools.partial(_attn_kernel, tq=tq, heads=MLA_HEADS),
        grid=(B, nq),
        in_specs=[pl.BlockSpec((tq, qk_pad), lambda b, i: (b * nq + i, 0)),
                  pl.BlockSpec((S, qk_pad), lambda b, i: (b, 0)),
                  pl.BlockSpec((S, qk_pad), lambda b, i: (b, 0))],
        out_specs=pl.BlockSpec((tq, mla_width), lambda b, i: (b * nq + i, 0)),
        out_shape=jax.ShapeDtypeStruct((T, mla_width), BF16),
        scratch_shapes=[pltpu.VMEM((MLA_HEADS, nq, tq, tq), F32),
                        pltpu.VMEM((MLA_HEADS, tq, LANES), F32),
                        pltpu.VMEM((MLA_HEADS, tq, HEAD_PAD), F32)],
        compiler_params=cparams(dimension_semantics=("parallel", "arbitrary")),
        name="attention",
    )(q, k, v)

    lam = lax.complex(jnp.minimum(lambda_re[l].astype(F32), -1e-4), lambda_im[l].astype(F32))
    step = jnp.exp(log_step[l].astype(F32))[:, None]
    lam_bar = jnp.exp(lam * step)
    b_bar = ((lam_bar - 1.0) / lam)[..., None] * lax.complex(b_re[l].astype(F32), b_im[l].astype(F32))
    n_slab = ssm_width // LANES
    g_per_slab = n_groups // n_slab
    eye = jnp.eye(n_groups, dtype=F32)

    def expand_in(bpart):
        return jnp.einsum('gpc,gh->gchp', bpart, eye).reshape(ssm_width, n_state)

    def expand_out(cpart):
        return jnp.einsum('gcp,gh->gphc', cpart, eye).reshape(n_state, ssm_width)

    slab = n_state // n_slab
    win_re, win_im = expand_in(jnp.real(b_bar)), expand_in(jnp.imag(b_bar))
    win = jnp.stack([jnp.concatenate([win_re[j * LANES:(j + 1) * LANES, j * slab:(j + 1) * slab],
                                      win_im[j * LANES:(j + 1) * LANES, j * slab:(j + 1) * slab]], axis=1)
                     for j in range(n_slab)]).astype(BF16)
    wc_re, wc_im = expand_out(c_re[l].astype(F32)), expand_out(-c_im[l].astype(F32))
    cre = jnp.stack([wc_re[j * slab:(j + 1) * slab, j * LANES:(j + 1) * LANES] for j in range(n_slab)]).astype(BF16)
    cim = jnp.stack([wc_im[j * slab:(j + 1) * slab, j * LANES:(j + 1) * LANES] for j in range(n_slab)]).astype(BF16)
    a_re = jnp.real(lam_bar).reshape(1, n_state)
    a_im = jnp.imag(lam_bar).reshape(1, n_state)

    lt = min(64, S)
    ssm = pl.pallas_call(
        functools.partial(_s5_kernel, batch=B, lt=lt, n_state=n_state, scan_tiles=4),
        grid=(S // lt,),
        in_specs=[pl.BlockSpec((B, lt, ssm_width), lambda t: (0, t, 0)),
                  _const_spec(win.shape), _const_spec((1, n_state)), _const_spec((1, n_state)),
                  _const_spec(cre.shape), _const_spec(cim.shape), _const_spec((1, ssm_width)),
                  _const_spec((ssm_width, ssm_width)), _const_spec((1, ssm_width))],
        out_specs=pl.BlockSpec((B, lt, ssm_width), lambda t: (0, t, 0)),
        out_shape=jax.ShapeDtypeStruct((B, S, ssm_width), BF16),
        scratch_shapes=[pltpu.VMEM((2 * n_state // LANES, B * lt, LANES), F32),
                        pltpu.VMEM((n_state // LANES, B, LANES), F32),
                        pltpu.VMEM((n_state // LANES, B, LANES), F32),
                        pltpu.VMEM((ssm_width // LANES, B * lt, LANES), F32),
                        pltpu.VMEM((B * lt, ssm_width), F32)],
        compiler_params=cparams(dimension_semantics=("arbitrary",)),
        name="s5",
    )(u.reshape(B, S, ssm_width), win, a_re, a_im, cre, cim, _row(d_skip[l]),
      w_glu[l].astype(BF16), _row(b_glu[l]))

    wo = w_o[l].astype(BF16)
    wr_t = w_router[l].T.astype(F32)
    wr_hi = wr_t.astype(BF16)
    wr_lo = (wr_t - wr_hi.astype(F32)).astype(BF16)
    half = D // 2
    n_slabs = MOE_SLABS
    ts = T // n_slabs
    nt = ts // tm
    ssm2 = ssm.reshape(T, ssm_width)
    rbias = router_bias[l].astype(F32).reshape(N_EXPERTS, 1)
    kt = lambda dt: jax.ShapeDtypeStruct((TOP_K, ts), dt)
    k_spec = pl.BlockSpec((TOP_K, tm), lambda i: (0, i))

    def route(s):
        tok_s = lambda width: pl.BlockSpec((tm, width), lambda i: (i + s * nt, 0))
        return pl.pallas_call(
            _mix_kernel,
            grid=(nt,),
            in_specs=[tok_s(mla_width), tok_s(ssm_width), tok_s(D), _const_spec((1, mla_width)),
                      _const_spec((1, ssm_width)), _const_spec((mla_width, D)),
                      _const_spec((ssm_width, D)), _const_spec((1, D)), _const_spec((1, D)),
                      _const_spec((N_EXPERTS, D)), _const_spec((N_EXPERTS, D)),
                      _const_spec((N_EXPERTS, 1))],
            out_specs=[tok(D), tok(half), k_spec, k_spec, k_spec, _const_spec((N_EXPERTS, LANES))],
            out_shape=[jax.ShapeDtypeStruct((ts, D), F32), jax.ShapeDtypeStruct((ts, half), jnp.int32),
                       kt(jnp.int32), kt(F32), kt(jnp.int32),
                       jax.ShapeDtypeStruct((N_EXPERTS, LANES), F32)],
            scratch_shapes=[pltpu.VMEM((N_EXPERTS, 1), F32)],
            compiler_params=cparams(dimension_semantics=("arbitrary",)),
            name="mix_router",
        )(att, ssm2, xn, _row(attn_out_g[l]), _row(ssm_out_g[l]), wo[:mla_width], wo[mla_width:],
          _row(ln1_g[l]), _row(ln1_b[l]), wr_hi, wr_lo, rbias)

    tr = EXPERT_ROW_TILE
    n_tiles = (ts * TOP_K) // tr + N_EXPERTS
    n_rows = n_tiles * tr
    tp = min(2048, ts)

    def dispatch(x1p, idx_k, rank_k, counts):
        cnt = counts[:, 0].astype(jnp.int32)
        tiles_e = (cnt + tr - 1) // tr
        tile_end = jnp.cumsum(tiles_e)
        tile_start = tile_end - tiles_e
        tile_ids = jnp.arange(n_tiles, dtype=jnp.int32)
        tile_expert = jnp.sum((tile_end[None, :] <= tile_ids[:, None]).astype(jnp.int32), axis=1)
        tile_expert = jnp.minimum(tile_expert, N_EXPERTS - 1)
        owner = (tile_start[None, :] <= tile_ids[:, None]) & (tile_ids[:, None] < tile_end[None, :])
        left = jnp.sum(jnp.where(owner, cnt[None, :] - (tile_ids[:, None] - tile_start[None, :]) * tr, 0),
                       axis=1)
        tile_valid = jnp.clip(left, 0, tr).astype(jnp.int32)
        pos = pl.pallas_call(
            _positions_kernel,
            grid_spec=pltpu.PrefetchScalarGridSpec(
                num_scalar_prefetch=1, grid=(ts // tp,),
                in_specs=[pl.BlockSpec((TOP_K, tp), lambda i, off: (0, i)),
                          pl.BlockSpec((TOP_K, tp), lambda i, off: (0, i))],
                out_specs=pl.BlockSpec((TOP_K, tp), lambda i, off: (0, i))),
            out_shape=kt(jnp.int32),
            name="positions",
        )((tile_start * tr).astype(jnp.int32), idx_k, rank_k)
        return _sc_scatter_rows(x1p, pos, n_rows), pos, tile_expert, tile_valid

    wg, wu, wd = w_gate[l], w_up[l], w_down[l]
    ff = wg.shape[-1]

    def experts(xs, tile_expert, tile_valid):
        return pl.pallas_call(
            _experts_kernel,
            grid_spec=pltpu.PrefetchScalarGridSpec(
                num_scalar_prefetch=2, grid=(n_tiles,),
                in_specs=[pl.BlockSpec((tr, half), lambda i, te, tv: (i, 0)),
                          pl.BlockSpec((1, D, ff), lambda i, te, tv: (te[i], 0, 0)),
                          pl.BlockSpec((1, D, ff), lambda i, te, tv: (te[i], 0, 0)),
                          pl.BlockSpec((1, ff, D), lambda i, te, tv: (te[i], 0, 0))],
                out_specs=pl.BlockSpec((tr, half), lambda i, te, tv: (i, 0)),
                scratch_shapes=[pltpu.VMEM((D, ff), BF16), pltpu.VMEM((D, ff), BF16),
                                pltpu.VMEM((ff, D), BF16)]),
            out_shape=jax.ShapeDtypeStruct((n_rows, half), jnp.int32),
            compiler_params=cparams(dimension_semantics=("arbitrary",)),
            name="experts",
        )(tile_expert, tile_valid, xs, wg, wu, wd)

    shared = (ws_gate[l].astype(BF16), ws_up[l].astype(BF16), ws_down[l].astype(BF16))

    def combine(s, out_so_far, yg, gate_k, x1):
        specs = [pl.BlockSpec((TOP_K, tm, half), lambda i: (0, i, 0)),
                 pl.BlockSpec((tm, TOP_K), lambda i: (i, 0)), tok(D),
                 _const_spec((D, ff)), _const_spec((D, ff)), _const_spec((ff, D)),
                 _const_spec((1, D)), _const_spec((1, D))]
        args = (yg.reshape(TOP_K, ts, half), gate_k.T, x1, *shared, _row(ln2_g[l]), _row(ln2_b[l]))
        body, aliases = _combine_kernel, {}
        if out_so_far is not None:
            specs = [pl.BlockSpec(memory_space=pl.ANY)] + specs
            args = (out_so_far,) + args
            body, aliases = _combine_into_kernel, {0: 0}
        return pl.pallas_call(
            body,
            grid=(nt,),
            in_specs=specs,
            out_specs=pl.BlockSpec((tm, D), lambda i: (i + s * nt, 0)),
            out_shape=jax.ShapeDtypeStruct((T, D), F32),
            input_output_aliases=aliases,
            compiler_params=cparams(dimension_semantics=("parallel",)),
            name="combine",
        )(*args)

    routed, moved = [], []
    for s in range(n_slabs):
        x1, x1p, idx_k, gate_k, rank_k, counts = route(s)
        routed.append((x1, gate_k))
        moved.append(dispatch(x1p, idx_k, rank_k, counts))
    gathered = []
    for xs, pos, tile_expert, tile_valid in moved:
        gathered.append(_sc_gather_rows(experts(xs, tile_expert, tile_valid), pos.reshape(TOP_K * ts)))
    out = None
    for s in range(n_slabs):
        out = combine(s, out, gathered[s], routed[s][1], routed[s][0])
    return out.reshape(B, S, D)
```

```python
import functools

import jax
import jax.numpy as jnp
from jax import lax
from jax.experimental import pallas as pl
from jax.experimental.pallas import tpu as pltpu
from jax.experimental.pallas import tpu_sc as plsc

CHUNK = 64
MLA_HEADS = 8
QK_NOPE = 64
QK_ROPE = 32
V_DIM = 64
Q_RANK = 256
KV_RANK = 128
ROPE_THETA = 10000.0
SSM_GROUP = 16
SSM_STATE = 64
N_EXPERTS = 64
TOP_K = 8
N_GROUPS = 8
TOP_GROUPS = 4
ROUTED_SCALE = 2.5
DEPTH = 1
ALPHA = (2.0 * DEPTH) ** 0.25
EPS = 1e-5
LOG2_E = 1.4426950408889634

LANES = 128
HEAD_PAD = LANES
VMEM_LIMIT = 56 * 1024 * 1024
EXPERT_ROW_TILE = 1024
MOE_SLABS = 2

BF16 = jnp.bfloat16
F32 = jnp.float32
NT_DIMS = (((1,), (1,)), ((), ()))


def _dot(a, b):
    return jnp.dot(a, b, preferred_element_type=F32)


def _layer_norm(x, g, b):
    mu = jnp.mean(x, axis=-1, keepdims=True)
    xc = x - mu
    var = jnp.mean(xc * xc, axis=-1, keepdims=True)
    return xc * lax.rsqrt(var + EPS) * g + b


def _rms_norm(x, g):
    return x * lax.rsqrt(jnp.mean(x * x, axis=-1, keepdims=True) + EPS) * g


def _inproj_kernel(x_ref, pos_ref, lng_ref, lnb_ref, w1_ref, qg_ref, kvg_ref,
                   wq_ref, wqr_ref, wk_ref, wv_ref, freq_ref,
                   xn_ref, q_ref, k_ref, v_ref, u_ref, *, ssm_width):
    xn = _layer_norm(x_ref[...], lng_ref[...], lnb_ref[...])
    xb = xn.astype(BF16)
    xn_ref[...] = xb
    h = _dot(xb, w1_ref[...])
    o1 = Q_RANK
    o2 = o1 + KV_RANK
    o3 = o2 + ssm_width
    o4 = o3 + HEAD_PAD
    cq = h[:, :o1]
    ckv = h[:, o1:o2]
    u_ref[...] = h[:, o2:o3].astype(u_ref.dtype)
    kr_raw = h[:, o3:o4]
    kr_rot = h[:, o4:o4 + HEAD_PAD]
    cqn = _rms_norm(cq, qg_ref[...]).astype(BF16)
    ckvn = _rms_norm(ckv, kvg_ref[...]).astype(BF16)

    ang = pos_ref[...] * freq_ref[...]
    c = jnp.cos(ang)
    s = jnp.sin(ang)
    lane = lax.broadcasted_iota(jnp.int32, (1, HEAD_PAD), 1)
    scale = (QK_NOPE + QK_ROPE) ** -0.5 * LOG2_E
    is_rope = (lane >= QK_NOPE) & (lane < QK_NOPE + QK_ROPE)
    cos1 = jnp.where(lane < QK_NOPE, 1.0, jnp.where(is_rope, c, 0.0)) * scale
    sin1 = jnp.where(is_rope, s, 0.0) * scale
    cos_t = jnp.concatenate([cos1] * MLA_HEADS, axis=1)
    sin_t = jnp.concatenate([sin1] * MLA_HEADS, axis=1)
    q = _dot(cqn, wq_ref[...]) * cos_t + _dot(cqn, wqr_ref[...]) * sin_t
    q_ref[...] = q.astype(BF16)

    kr = kr_raw * c + kr_rot * s
    k = _dot(ckvn, wk_ref[...]) + jnp.concatenate([kr] * MLA_HEADS, axis=1)
    k_ref[...] = k.astype(BF16)
    ones_col = jnp.concatenate([jnp.where(lane == V_DIM, 1.0, 0.0)] * MLA_HEADS, axis=1)
    v_ref[...] = (_dot(ckvn, wv_ref[...]) + ones_col).astype(BF16)


def _attn_kernel(q_ref, k_ref, v_ref, o_ref, s_ref, mx_ref, acc_ref, *, tq, heads):
    qi = pl.program_id(1)
    row_chunk = lax.broadcasted_iota(jnp.int32, (tq, tq), 0) // CHUNK
    col_chunk = lax.broadcasted_iota(jnp.int32, (tq, tq), 1) // CHUNK
    diag_mask = row_chunk >= col_chunk
    head_cols = [slice(h * HEAD_PAD, (h + 1) * HEAD_PAD) for h in range(heads)]
    mx_ref[...] = jnp.full(mx_ref.shape, -jnp.inf, F32)
    acc_ref[...] = jnp.zeros(acc_ref.shape, F32)

    def scores(j, masked):
        start = pl.multiple_of(j * tq, tq)
        for h in range(heads):
            s = lax.dot_general(q_ref[:, head_cols[h]], k_ref[pl.ds(start, tq), head_cols[h]], NT_DIMS,
                                preferred_element_type=F32)
            if masked:
                s = jnp.where(diag_mask, s, -jnp.inf)
            s_ref[h, j] = s
            lane_max = s[:, :LANES]
            for c in range(1, tq // LANES):
                lane_max = jnp.maximum(lane_max, s[:, c * LANES:(c + 1) * LANES])
            mx_ref[h] = jnp.maximum(mx_ref[h], lane_max)

    def scores_step(j, carry):
        scores(j, False)
        return carry

    lax.fori_loop(0, qi, scores_step, 0)
    scores(qi, True)
    row_max = [jnp.max(mx_ref[h], axis=-1, keepdims=True) for h in range(heads)]

    def values_step(j, carry):
        start = pl.multiple_of(j * tq, tq)
        for h in range(heads):
            p = jnp.exp2(s_ref[h, j] - row_max[h]).astype(BF16)
            acc_ref[h] += _dot(p, v_ref[pl.ds(start, tq), head_cols[h]])
        return carry

    lax.fori_loop(0, qi + 1, values_step, 0)
    outs = [acc_ref[h][:, :V_DIM] / acc_ref[h][:, V_DIM:V_DIM + 1] for h in range(heads)]
    o_ref[...] = jnp.concatenate(outs, axis=1).astype(o_ref.dtype)


def _s5_kernel(u_ref, win_ref, are_ref, aim_ref, cre_ref, cim_ref, dskip_ref,
               wglu_ref, bglu_ref, o_ref, vx_ref, hre_ref, him_ref, io_ref, utm_ref, *,
               batch, lt, n_state, scan_tiles):
    ti = pl.program_id(0)

    @pl.when(ti == 0)
    def _():
        hre_ref[...] = jnp.zeros_like(hre_ref)
        him_ref[...] = jnp.zeros_like(him_ref)

    width = u_ref.shape[-1]
    n_slab = width // LANES
    n_tiles = n_state // LANES
    slab_tiles = n_tiles // n_slab
    for b in range(batch):
        for c in range(n_slab):
            io_ref[c, b * lt:(b + 1) * lt, :] = u_ref[b, :, c * LANES:(c + 1) * LANES].astype(F32)

    def to_time_major(t, _):
        dst = pl.ds(pl.multiple_of(t * batch, batch), batch)
        for c in range(n_slab):
            utm_ref[dst, c * LANES:(c + 1) * LANES] = io_ref[c, pl.ds(t, batch, stride=lt), :]
        return 0

    lax.fori_loop(0, lt, to_time_major, 0, unroll=4)
    u2 = utm_ref[...]
    ub = u2.astype(BF16)
    for j in range(n_slab):
        vj = _dot(ub[:, j * LANES:(j + 1) * LANES], win_ref[j])
        for i in range(slab_tiles):
            vx_ref[j * slab_tiles + i] = vj[:, i * LANES:(i + 1) * LANES]
            vx_ref[n_tiles + j * slab_tiles + i] = vj[:, (slab_tiles + i) * LANES:(slab_tiles + i + 1) * LANES]

    for c0 in range(0, n_tiles, scan_tiles):
        tiles = range(c0, c0 + scan_tiles)
        ar = [jnp.broadcast_to(are_ref[:, c * LANES:(c + 1) * LANES], (batch, LANES)) for c in tiles]
        ai = [jnp.broadcast_to(aim_ref[:, c * LANES:(c + 1) * LANES], (batch, LANES)) for c in tiles]

        def step(t, carry, tiles=tiles, ar=ar, ai=ai):
            rows = pl.ds(pl.multiple_of(t * batch, batch), batch)
            out = []
            for n, c in enumerate(tiles):
                hr, hi = carry[2 * n], carry[2 * n + 1]
                nr = ar[n] * hr - ai[n] * hi + vx_ref[c, rows, :]
                ni = ar[n] * hi + ai[n] * hr + vx_ref[n_tiles + c, rows, :]
                vx_ref[c, rows, :] = nr
                vx_ref[n_tiles + c, rows, :] = ni
                out += [nr, ni]
            return tuple(out)

        init = []
        for c in tiles:
            init += [hre_ref[c], him_ref[c]]
        fin = lax.fori_loop(0, lt, step, tuple(init), unroll=4)
        for n, c in enumerate(tiles):
            hre_ref[c] = fin[2 * n]
            him_ref[c] = fin[2 * n + 1]

    ys = []
    for j in range(n_slab):
        xr = jnp.concatenate([vx_ref[j * slab_tiles + i].astype(BF16) for i in range(slab_tiles)], axis=1)
        xi = jnp.concatenate([vx_ref[n_tiles + j * slab_tiles + i].astype(BF16) for i in range(slab_tiles)], axis=1)
        ys.append(_dot(xr, cre_ref[j]) + _dot(xi, cim_ref[j]))
    y = jnp.concatenate(ys, axis=1) + dskip_ref[...] * u2
    y = jax.nn.gelu(y)
    z = _dot(y.astype(BF16), wglu_ref[...]) + bglu_ref[...]
    out = y * jax.nn.sigmoid(z)
    for c in range(n_slab):
        io_ref[c] = out[:, c * LANES:(c + 1) * LANES]
    for b in range(batch):
        for c in range(n_slab):
            o_ref[b, :, c * LANES:(c + 1) * LANES] = io_ref[c, pl.ds(b, lt, stride=batch), :].astype(o_ref.dtype)


def _router_gates(logits_t, rbias):
    n_exp, tm = logits_t.shape
    per_group = n_exp // N_GROUPS
    scores = jax.nn.sigmoid(logits_t)
    sel = scores + rbias
    neg_inf = -jnp.inf
    sub_iota = lax.broadcasted_iota(jnp.int32, (per_group, tm), 0).astype(F32)
    group_score = []
    for g in range(N_GROUPS):
        sg = sel[g * per_group:(g + 1) * per_group, :]
        m1 = jnp.max(sg, axis=0, keepdims=True)
        first = jnp.min(jnp.where(sg == m1, sub_iota, float(per_group)), axis=0, keepdims=True)
        m2 = jnp.max(jnp.where(sub_iota == first, neg_inf, sg), axis=0, keepdims=True)
        group_score.append(m1 + m2)
    masked = []
    for g in range(N_GROUPS):
        rank = jnp.zeros((1, tm), F32)
        for g2 in range(N_GROUPS):
            if g2 == g:
                continue
            ahead = (group_score[g2] >= group_score[g]) if g2 < g else (group_score[g2] > group_score[g])
            rank = rank + jnp.where(ahead, 1.0, 0.0)
        keep = rank < float(TOP_GROUPS)
        masked.append(jnp.where(keep, sel[g * per_group:(g + 1) * per_group, :], neg_inf))
    cur = jnp.concatenate(masked, axis=0)
    iota = lax.broadcasted_iota(jnp.int32, (n_exp, tm), 0).astype(F32)
    chosen = jnp.zeros((n_exp, tm), F32)
    picks, weights = [], []
    for _ in range(TOP_K):
        m = jnp.max(cur, axis=0, keepdims=True)
        idx = jnp.min(jnp.where(cur == m, iota, float(n_exp)), axis=0, keepdims=True)
        pick = iota == idx
        chosen = jnp.where(pick, 1.0, chosen)
        cur = jnp.where(pick, neg_inf, cur)
        picks.append(idx)
        weights.append(jnp.sum(jnp.where(pick, scores, 0.0), axis=0, keepdims=True))
    idx_k = jnp.concatenate(picks, axis=0)
    w_k = jnp.concatenate(weights, axis=0)
    gate_k = w_k / jnp.sum(w_k, axis=0, keepdims=True) * ROUTED_SCALE
    return idx_k, gate_k, chosen


def _pack_bf16_pairs(x):
    n = x.shape[1] // 2
    hi = lax.bitcast_convert_type(x[:, :n].astype(BF16).astype(F32), jnp.int32)
    lo = lax.bitcast_convert_type(x[:, n:].astype(BF16).astype(F32), jnp.int32)
    return hi | lax.shift_right_logical(lo, 16)


def _unpack_bf16_pairs(p):
    hi = lax.bitcast_convert_type(p & jnp.int32(-65536), F32).astype(BF16)
    lo = lax.bitcast_convert_type(lax.shift_left(p, 16), F32).astype(BF16)
    return hi, lo


def _mix_kernel(att_ref, ssm_ref, xn_ref, ag_ref, sg_ref, woa_ref, wos_ref,
                g_ref, b_ref, wrh_ref, wrl_ref, rb_ref,
                x1p_ref, idx_ref, gate_ref, rank_ref, cnt_ref, carry_ref):
    @pl.when(pl.program_id(0) == 0)
    def _():
        carry_ref[...] = jnp.zeros_like(carry_ref)

    an = _rms_norm(att_ref[...].astype(F32), ag_ref[...]).astype(BF16)
    sn = _rms_norm(ssm_ref[...].astype(F32), sg_ref[...]).astype(BF16)
    mix = _dot(an, woa_ref[...]) + _dot(sn, wos_ref[...])
    x1 = _layer_norm(ALPHA * xn_ref[...].astype(F32) + mix, g_ref[...], b_ref[...])
    x1p_ref[...] = _pack_bf16_pairs(x1)
    x_hi = x1.astype(BF16)
    x_lo = (x1 - x_hi.astype(F32)).astype(BF16)
    dg = functools.partial(lax.dot_general, dimension_numbers=NT_DIMS, preferred_element_type=F32)
    logits_t = dg(wrh_ref[...], x_hi) + dg(wrl_ref[...], x_hi) + dg(wrh_ref[...], x_lo)
    idx_k, gate_k, chosen = _router_gates(logits_t, rb_ref[...])
    idx_ref[...] = idx_k.astype(jnp.int32)
    gate_ref[...] = gate_k

    n_exp, tm = chosen.shape
    before = (lax.broadcasted_iota(jnp.int32, (tm, tm), 0)
              < lax.broadcasted_iota(jnp.int32, (tm, tm), 1))
    excl = _dot(chosen.astype(BF16), jnp.where(before, 1.0, 0.0).astype(BF16))
    rank_full = carry_ref[...] + excl
    iota = lax.broadcasted_iota(jnp.int32, (n_exp, tm), 0).astype(F32)
    ranks = [jnp.sum(jnp.where(iota == idx_k[k:k + 1, :], rank_full, 0.0), axis=0, keepdims=True)
             for k in range(TOP_K)]
    rank_ref[...] = jnp.concatenate(ranks, axis=0).astype(jnp.int32)
    total = carry_ref[...] + jnp.sum(chosen, axis=1, keepdims=True)
    carry_ref[...] = total
    cnt_ref[...] = jnp.broadcast_to(total, cnt_ref.shape)


def _positions_kernel(off_ref, idx_ref, rank_ref, pos_ref):
    idx = idx_ref[...]
    base = jnp.zeros(idx.shape, jnp.int32)
    for e in range(N_EXPERTS):
        base = jnp.where(idx == e, off_ref[e], base)
    pos_ref[...] = rank_ref[...] + base


def _swiglu(x_hi, x_lo, wg_ref, wu_ref, wd_ref):
    half = x_hi.shape[1]
    hg = _dot(x_hi, wg_ref[:half, :]) + _dot(x_lo, wg_ref[half:, :])
    hu = _dot(x_hi, wu_ref[:half, :]) + _dot(x_lo, wu_ref[half:, :])
    h = jax.nn.silu(hg) * hu
    return _dot(h.astype(BF16), wd_ref[...])


def _experts_kernel(te_ref, valid_ref, xs_ref, wg_ref, wu_ref, wd_ref, ys_ref,
                    wgb_ref, wub_ref, wdb_ref):
    i = pl.program_id(0)
    valid = valid_ref[i]

    @pl.when((i == 0) | (te_ref[i] != te_ref[jnp.maximum(i - 1, 0)]))
    def _():
        wgb_ref[...] = wg_ref[0].astype(BF16)
        wub_ref[...] = wu_ref[0].astype(BF16)
        wdb_ref[...] = wd_ref[0].astype(BF16)

    @pl.when(valid > 0)
    def _():
        rows = lax.broadcasted_iota(jnp.int32, xs_ref.shape, 0)
        x_hi, x_lo = _unpack_bf16_pairs(jnp.where(rows < valid, xs_ref[...], 0))
        ys_ref[...] = _pack_bf16_pairs(_swiglu(x_hi, x_lo, wgb_ref, wub_ref, wdb_ref))

    @pl.when(valid == 0)
    def _():
        ys_ref[...] = jnp.zeros_like(ys_ref)


def _combine_kernel(yg_ref, gate_ref, x1p_ref, wsg_ref, wsu_ref, wsd_ref, g_ref, b_ref, o_ref):
    x_hi, x_lo = _unpack_bf16_pairs(x1p_ref[...])
    half = x_hi.shape[1]
    x1 = jnp.concatenate([x_hi.astype(F32), x_lo.astype(F32)], axis=1)
    acc = _swiglu(x_hi, x_lo, wsg_ref, wsu_ref, wsd_ref)
    gates = gate_ref[...]
    acc_hi, acc_lo = acc[:, :half], acc[:, half:]
    for k in range(TOP_K):
        y_hi, y_lo = _unpack_bf16_pairs(yg_ref[k])
        gk = gates[:, k:k + 1]
        acc_hi = acc_hi + gk * y_hi.astype(F32)
        acc_lo = acc_lo + gk * y_lo.astype(F32)
    ffn = jnp.concatenate([acc_hi, acc_lo], axis=1)
    o_ref[...] = _layer_norm(ALPHA * x1 + ffn, g_ref[...], b_ref[...])


def _combine_into_kernel(prev_ref, *refs):
    del prev_ref
    _combine_kernel(*refs)


SC_CORES = 2
SC_SUBCORES = 16
SC_WORKERS = SC_CORES * SC_SUBCORES
SC_CHUNK = 64


def _sc_mesh():
    return plsc.VectorSubcoreMesh(core_axis_name="c", subcore_axis_name="s")


def _sc_gather_rows(table, idx):
    n = idx.shape[0]
    d = table.shape[1]
    per_w = n // SC_WORKERS
    n_ch = per_w // SC_CHUNK

    @functools.partial(
        pl.kernel, mesh=_sc_mesh(),
        out_type=jax.ShapeDtypeStruct((n, d), table.dtype),
        scratch_types=[pltpu.VMEM((n_ch, SC_CHUNK), jnp.int32),
                       pltpu.VMEM((2, SC_CHUNK, d), table.dtype),
                       pltpu.SemaphoreType.DMA((2,)),
                       pltpu.SemaphoreType.DMA((2,))],
    )
    def k(table_hbm, idx_hbm, out_hbm, idx_v, buf, gsem, osem):
        wid = lax.axis_index("s") * SC_CORES + lax.axis_index("c")
        base = wid * per_w
        pltpu.sync_copy(idx_hbm.at[wid], idx_v)

        def gather(c, b):
            return pltpu.make_async_copy(table_hbm.at[idx_v.at[c]], buf.at[b], gsem.at[b])

        def put(c, b):
            return pltpu.make_async_copy(buf.at[b], out_hbm.at[pl.ds(base + c * SC_CHUNK, SC_CHUNK)],
                                         osem.at[b])

        gather(0, 0).start()

        @pl.loop(0, n_ch, step=2)
        def _(c):
            for b in range(2):
                cc = c + b
                gather(cc, b).wait()

                @pl.when(cc + 1 < n_ch)
                def _():
                    @pl.when(cc >= 1)
                    def _():
                        put(cc - 1, 1 - b).wait()
                    gather(cc + 1, 1 - b).start()

                put(cc, b).start()

        put(n_ch - 2, 0).wait()
        put(n_ch - 1, 1).wait()

    return k(table, idx.reshape(SC_WORKERS, n_ch, SC_CHUNK))


def _sc_scatter_rows(x, pos, n_out):
    t, d = x.shape
    kk = pos.shape[0]
    per_w = t // SC_WORKERS
    n_ch = per_w // SC_CHUNK
    pos_w = pos.reshape(kk, SC_WORKERS, n_ch, SC_CHUNK).transpose(1, 2, 0, 3)
    pos_w = pos_w.reshape(SC_WORKERS, n_ch * kk, SC_CHUNK)

    @functools.partial(
        pl.kernel, mesh=_sc_mesh(),
        out_type=jax.ShapeDtypeStruct((n_out, d), x.dtype),
        scratch_types=[pltpu.VMEM((n_ch * kk, SC_CHUNK), jnp.int32),
                       pltpu.VMEM((2, SC_CHUNK, d), x.dtype),
                       pltpu.SemaphoreType.DMA((2,)),
                       pltpu.SemaphoreType.DMA((2,))],
    )
    def k(x_hbm, pos_hbm, out_hbm, idx_v, buf, isem, osem):
        wid = lax.axis_index("s") * SC_CORES + lax.axis_index("c")
        base = wid * per_w
        pltpu.sync_copy(pos_hbm.at[wid], idx_v)

        def get(c, b):
            return pltpu.make_async_copy(x_hbm.at[pl.ds(base + c * SC_CHUNK, SC_CHUNK)], buf.at[b],
                                         isem.at[b])

        def put(c, j, b):
            return pltpu.make_async_copy(buf.at[b], out_hbm.at[idx_v.at[c * kk + j]], osem.at[b])

        get(0, 0).start()

        @pl.loop(0, n_ch, step=2)
        def _(c):
            for b in range(2):
                cc = c + b
                get(cc, b).wait()

                @pl.when(cc + 1 < n_ch)
                def _():
                    @pl.when(cc >= 1)
                    def _():
                        for j in range(kk):
                            put(cc - 1, j, 1 - b).wait()
                    get(cc + 1, 1 - b).start()

                for j in range(kk):
                    put(cc, j, b).start()

        for j in range(kk):
            put(n_ch - 2, j, 0).wait()
        for j in range(kk):
            put(n_ch - 1, j, 1).wait()

    return k(x, pos_w)


def _row(v):
    return v.reshape(1, -1).astype(F32)


def _const_spec(shape):
    nd = len(shape)
    return pl.BlockSpec(shape, lambda *_: (0,) * nd)


def _pad_heads(w, width):
    r, h, _ = w.shape
    return jnp.pad(w, ((0, 0), (0, 0), (0, HEAD_PAD - width))).reshape(r, h * HEAD_PAD)


def _half_rotate(w):
    half = QK_ROPE // 2
    return jnp.concatenate([-w[..., half:], w[..., :half]], axis=-1)


def kernel(x, positions, ln_in_g, ln_in_b, w_in, q_norm_g, kv_norm_g, w_uq, w_ukv, lambda_re, lambda_im, log_step, b_re, b_im, c_re, c_im, d_skip, w_glu, b_glu, attn_out_g, ssm_out_g, w_o, ln1_g, ln1_b, w_router, router_bias, w_gate, w_up, w_down, ws_gate, ws_up, ws_down, ln2_g, ln2_b):
    B, S, D = x.shape
    T = B * S
    l = 0
    ssm_width = w_glu.shape[-1]
    n_groups = ssm_width // SSM_GROUP
    n_state = n_groups * SSM_STATE
    mla_width = MLA_HEADS * V_DIM
    qk_pad = MLA_HEADS * HEAD_PAD
    cparams = functools.partial(pltpu.CompilerParams, vmem_limit_bytes=VMEM_LIMIT)

    s1, s2, s3 = Q_RANK, Q_RANK + KV_RANK, Q_RANK + KV_RANK + QK_ROPE
    wi = w_in[l]
    w_kr = wi[:, s2:s3]
    pad_rope = lambda w: jnp.pad(w, ((0, 0), (QK_NOPE, HEAD_PAD - QK_NOPE - QK_ROPE)))
    w1 = jnp.concatenate([wi[:, :s2], wi[:, s3:], pad_rope(w_kr), pad_rope(_half_rotate(w_kr))],
                         axis=1).astype(BF16)
    wq = w_uq[l]
    zeros_nope = jnp.zeros(wq.shape[:2] + (QK_NOPE,), wq.dtype)
    wq_main = _pad_heads(wq, QK_NOPE + QK_ROPE).astype(BF16)
    wq_rot = _pad_heads(jnp.concatenate([zeros_nope, _half_rotate(wq[..., QK_NOPE:])], axis=-1),
                        QK_NOPE + QK_ROPE).astype(BF16)
    wkv = w_ukv[l]
    wk = _pad_heads(wkv[..., :QK_NOPE], QK_NOPE).astype(BF16)
    wv = _pad_heads(wkv[..., QK_NOPE:], V_DIM).astype(BF16)
    half = QK_ROPE // 2
    inv_freq = ROPE_THETA ** (-jnp.arange(half, dtype=F32) / half)
    freq = jnp.pad(jnp.concatenate([inv_freq, inv_freq]),
                   (QK_NOPE, HEAD_PAD - QK_NOPE - QK_ROPE)).reshape(1, HEAD_PAD)
    pos_f = positions.astype(F32).reshape(T, 1)

    tm = min(512, T)
    w1_cols = w1.shape[1]
    tok = lambda width: pl.BlockSpec((tm, width), lambda i: (i, 0))
    xn, q, k, v, u = pl.pallas_call(
        functools.partial(_inproj_kernel, ssm_width=ssm_width),
        grid=(T // tm,),
        in_specs=[tok(D), tok(1), _const_spec((1, D)), _const_spec((1, D)),
                  _const_spec((D, w1_cols)), _const_spec((1, Q_RANK)), _const_spec((1, KV_RANK)),
                  _const_spec((Q_RANK, qk_pad)), _const_spec((Q_RANK, qk_pad)),
                  _const_spec((KV_RANK, qk_pad)), _const_spec((KV_RANK, qk_pad)),
                  _const_spec((1, HEAD_PAD))],
        out_specs=[tok(D), tok(qk_pad), tok(qk_pad), tok(qk_pad), tok(ssm_width)],
        out_shape=[jax.ShapeDtypeStruct((T, D), BF16), jax.ShapeDtypeStruct((T, qk_pad), BF16),
                   jax.ShapeDtypeStruct((T, qk_pad), BF16), jax.ShapeDtypeStruct((T, qk_pad), BF16),
                   jax.ShapeDtypeStruct((T, ssm_width), BF16)],
        compiler_params=cparams(dimension_semantics=("parallel",)),
        name="inproj",
    )(x.reshape(T, D), pos_f, _row(ln_in_g), _row(ln_in_b), w1, _row(q_norm_g[l]), _row(kv_norm_g[l]),
      wq_main, wq_rot, wk, wv, freq)

    tq = min(256, S)
    nq = S // tq
    att = pl.pallas_call(
        functools.partial(_attn_kernel, tq=tq, heads=MLA_HEADS),
        grid=(B, nq),
        in_specs=[pl.BlockSpec((tq, qk_pad), lambda b, i: (b * nq + i, 0)),
                  pl.BlockSpec((S, qk_pad), lambda b, i: (b, 0)),
                  pl.BlockSpec((S, qk_pad), lambda b, i: (b, 0))],
        out_specs=pl.BlockSpec((tq, mla_width), lambda b, i: (b * nq + i, 0)),
        out_shape=jax.ShapeDtypeStruct((T, mla_width), BF16),
        scratch_shapes=[pltpu.VMEM((MLA_HEADS, nq, tq, tq), F32),
                        pltpu.VMEM((MLA_HEADS, tq, LANES), F32),
                        pltpu.VMEM((MLA_HEADS, tq, HEAD_PAD), F32)],
        compiler_params=cparams(dimension_semantics=("parallel", "arbitrary")),
        name="attention",
    )(q, k, v)

    lam = lax.complex(jnp.minimum(lambda_re[l].astype(F32), -1e-4), lambda_im[l].astype(F32))
    step = jnp.exp(log_step[l].astype(F32))[:, None]
    lam_bar = jnp.exp(lam * step)
    b_bar = ((lam_bar - 1.0) / lam)[..., None] * lax.complex(b_re[l].astype(F32), b_im[l].astype(F32))
    n_slab = ssm_width // LANES
    g_per_slab = n_groups // n_slab
    eye = jnp.eye(n_groups, dtype=F32)

    def expand_in(bpart):
        return jnp.einsum('gpc,gh->gchp', bpart, eye).reshape(ssm_width, n_state)

    def expand_out(cpart):
        return jnp.einsum('gcp,gh->gphc', cpart, eye).reshape(n_state, ssm_width)

    slab = n_state // n_slab
    win_re, win_im = expand_in(jnp.real(b_bar)), expand_in(jnp.imag(b_bar))
    win = jnp.stack([jnp.concatenate([win_re[j * LANES:(j + 1) * LANES, j * slab:(j + 1) * slab],
                                      win_im[j * LANES:(j + 1) * LANES, j * slab:(j + 1) * slab]], axis=1)
                     for j in range(n_slab)]).astype(BF16)
    wc_re, wc_im = expand_out(c_re[l].astype(F32)), expand_out(-c_im[l].astype(F32))
    cre = jnp.stack([wc_re[j * slab:(j + 1) * slab, j * LANES:(j + 1) * LANES] for j in range(n_slab)]).astype(BF16)
    cim = jnp.stack([wc_im[j * slab:(j + 1) * slab, j * LANES:(j + 1) * LANES] for j in range(n_slab)]).astype(BF16)
    a_re = jnp.real(lam_bar).reshape(1, n_state)
    a_im = jnp.imag(lam_bar).reshape(1, n_state)

    lt = min(64, S)
    ssm = pl.pallas_call(
        functools.partial(_s5_kernel, batch=B, lt=lt, n_state=n_state, scan_tiles=4),
        grid=(S // lt,),
        in_specs=[pl.BlockSpec((B, lt, ssm_width), lambda t: (0, t, 0)),
                  _const_spec(win.shape), _const_spec((1, n_state)), _const_spec((1, n_state)),
                  _const_spec(cre.shape), _const_spec(cim.shape), _const_spec((1, ssm_width)),
                  _const_spec((ssm_width, ssm_width)), _const_spec((1, ssm_width))],
        out_specs=pl.BlockSpec((B, lt, ssm_width), lambda t: (0, t, 0)),
        out_shape=jax.ShapeDtypeStruct((B, S, ssm_width), BF16),
        scratch_shapes=[pltpu.VMEM((2 * n_state // LANES, B * lt, LANES), F32),
                        pltpu.VMEM((n_state // LANES, B, LANES), F32),
                        pltpu.VMEM((n_state // LANES, B, LANES), F32),
                        pltpu.VMEM((ssm_width // LANES, B * lt, LANES), F32),
                        pltpu.VMEM((B * lt, ssm_width), F32)],
        compiler_params=cparams(dimension_semantics=("arbitrary",)),
        name="s5",
    )(u.reshape(B, S, ssm_width), win, a_re, a_im, cre, cim, _row(d_skip[l]),
      w_glu[l].astype(BF16), _row(b_glu[l]))

    wo = w_o[l].astype(BF16)
    wr_t = w_router[l].T.astype(F32)
    wr_hi = wr_t.astype(BF16)
    wr_lo = (wr_t - wr_hi.astype(F32)).astype(BF16)
    half = D // 2
    n_slabs = MOE_SLABS
    ts = T // n_slabs
    nt = ts // tm
    ssm2 = ssm.reshape(T, ssm_width)
    rbias = router_bias[l].astype(F32).reshape(N_EXPERTS, 1)
    kt = lambda dt: jax.ShapeDtypeStruct((TOP_K, ts), dt)
    k_spec = pl.BlockSpec((TOP_K, tm), lambda i: (0, i))

    def route(s):
        tok_s = lambda width: pl.BlockSpec((tm, width), lambda i: (i + s * nt, 0))
        return pl.pallas_call(
            _mix_kernel,
            grid=(nt,),
            in_specs=[tok_s(mla_width), tok_s(ssm_width), tok_s(D), _const_spec((1, mla_width)),
                      _const_spec((1, ssm_width)), _const_spec((mla_width, D)),
                      _const_spec((ssm_width, D)), _const_spec((1, D)), _const_spec((1, D)),
                      _const_spec((N_EXPERTS, D)), _const_spec((N_EXPERTS, D)),
                      _const_spec((N_EXPERTS, 1))],
            out_specs=[tok(half), k_spec, k_spec, k_spec, _const_spec((N_EXPERTS, LANES))],
            out_shape=[jax.ShapeDtypeStruct((ts, half), jnp.int32),
                       kt(jnp.int32), kt(F32), kt(jnp.int32),
                       jax.ShapeDtypeStruct((N_EXPERTS, LANES), F32)],
            scratch_shapes=[pltpu.VMEM((N_EXPERTS, 1), F32)],
            compiler_params=cparams(dimension_semantics=("arbitrary",)),
            name="mix_router",
        )(att, ssm2, xn, _row(attn_out_g[l]), _row(ssm_out_g[l]), wo[:mla_width], wo[mla_width:],
          _row(ln1_g[l]), _row(ln1_b[l]), wr_hi, wr_lo, rbias)

    tr = EXPERT_ROW_TILE
    n_tiles = (ts * TOP_K) // tr + N_EXPERTS
    n_rows = n_tiles * tr
    tp = min(2048, ts)

    def dispatch(x1p, idx_k, rank_k, counts):
        cnt = counts[:, 0].astype(jnp.int32)
        tiles_e = (cnt + tr - 1) // tr
        tile_end = jnp.cumsum(tiles_e)
        tile_start = tile_end - tiles_e
        tile_ids = jnp.arange(n_tiles, dtype=jnp.int32)
        tile_expert = jnp.sum((tile_end[None, :] <= tile_ids[:, None]).astype(jnp.int32), axis=1)
        tile_expert = jnp.minimum(tile_expert, N_EXPERTS - 1)
        owner = (tile_start[None, :] <= tile_ids[:, None]) & (tile_ids[:, None] < tile_end[None, :])
        left = jnp.sum(jnp.where(owner, cnt[None, :] - (tile_ids[:, None] - tile_start[None, :]) * tr, 0),
                       axis=1)
        tile_valid = jnp.clip(left, 0, tr).astype(jnp.int32)
        pos = pl.pallas_call(
            _positions_kernel,
            grid_spec=pltpu.PrefetchScalarGridSpec(
                num_scalar_prefetch=1, grid=(ts // tp,),
                in_specs=[pl.BlockSpec((TOP_K, tp), lambda i, off: (0, i)),
                          pl.BlockSpec((TOP_K, tp), lambda i, off: (0, i))],
                out_specs=pl.BlockSpec((TOP_K, tp), lambda i, off: (0, i))),
            out_shape=kt(jnp.int32),
            name="positions",
        )((tile_start * tr).astype(jnp.int32), idx_k, rank_k)
        return _sc_scatter_rows(x1p, pos, n_rows), pos, tile_expert, tile_valid

    wg, wu, wd = w_gate[l], w_up[l], w_down[l]
    ff = wg.shape[-1]

    def experts(xs, tile_expert, tile_valid):
        return pl.pallas_call(
            _experts_kernel,
            grid_spec=pltpu.PrefetchScalarGridSpec(
                num_scalar_prefetch=2, grid=(n_tiles,),
                in_specs=[pl.BlockSpec((tr, half), lambda i, te, tv: (i, 0)),
                          pl.BlockSpec((1, D, ff), lambda i, te, tv: (te[i], 0, 0)),
                          pl.BlockSpec((1, D, ff), lambda i, te, tv: (te[i], 0, 0)),
                          pl.BlockSpec((1, ff, D), lambda i, te, tv: (te[i], 0, 0))],
                out_specs=pl.BlockSpec((tr, half), lambda i, te, tv: (i, 0)),
                scratch_shapes=[pltpu.VMEM((D, ff), BF16), pltpu.VMEM((D, ff), BF16),
                                pltpu.VMEM((ff, D), BF16)]),
            out_shape=jax.ShapeDtypeStruct((n_rows, half), jnp.int32),
            compiler_params=cparams(dimension_semantics=("arbitrary",)),
            name="experts",
        )(tile_expert, tile_valid, xs, wg, wu, wd)

    shared = (ws_gate[l].astype(BF16), ws_up[l].astype(BF16), ws_down[l].astype(BF16))

    def combine(s, out_so_far, yg, gate_k, x1p):
        specs = [pl.BlockSpec((TOP_K, tm, half), lambda i: (0, i, 0)),
                 pl.BlockSpec((tm, TOP_K), lambda i: (i, 0)), tok(half),
                 _const_spec((D, ff)), _const_spec((D, ff)), _const_spec((ff, D)),
                 _const_spec((1, D)), _const_spec((1, D))]
        args = (yg.reshape(TOP_K, ts, half), gate_k.T, x1p, *shared, _row(ln2_g[l]), _row(ln2_b[l]))
        body, aliases = _combine_kernel, {}
        if out_so_far is not None:
            specs = [pl.BlockSpec(memory_space=pl.ANY)] + specs
            args = (out_so_far,) + args
            body, aliases = _combine_into_kernel, {0: 0}
        return pl.pallas_call(
            body,
            grid=(nt,),
            in_specs=specs,
            out_specs=pl.BlockSpec((tm, D), lambda i: (i + s * nt, 0)),
            out_shape=jax.ShapeDtypeStruct((T, D), F32),
            input_output_aliases=aliases,
            compiler_params=cparams(dimension_semantics=("parallel",)),
            name="combine",
        )(*args)

    routed, moved = [], []
    for s in range(n_slabs):
        x1p, idx_k, gate_k, rank_k, counts = route(s)
        routed.append((x1p, gate_k))
        moved.append(dispatch(x1p, idx_k, rank_k, counts))
    gathered = []
    for xs, pos, tile_expert, tile_valid in moved:
        gathered.append(_sc_gather_rows(experts(xs, tile_expert, tile_valid), pos.reshape(TOP_K * ts)))
    out = None
    for s in range(n_slabs):
        out = combine(s, out, gathered[s], routed[s][1], routed[s][0])
    return out.reshape(B, S, D)
```

```python
import functools

import jax
import jax.numpy as jnp
from jax import lax
from jax.experimental import pallas as pl
from jax.experimental.pallas import tpu as pltpu
from jax.experimental.pallas import tpu_sc as plsc

CHUNK = 64
MLA_HEADS = 8
QK_NOPE = 64
QK_ROPE = 32
V_DIM = 64
Q_RANK = 256
KV_RANK = 128
ROPE_THETA = 10000.0
SSM_GROUP = 16
SSM_STATE = 64
N_EXPERTS = 64
TOP_K = 8
N_GROUPS = 8
TOP_GROUPS = 4
ROUTED_SCALE = 2.5
DEPTH = 1
ALPHA = (2.0 * DEPTH) ** 0.25
EPS = 1e-5
LOG2_E = 1.4426950408889634

LANES = 128
HEAD_PAD = LANES
VMEM_LIMIT = 56 * 1024 * 1024
EXPERT_ROW_TILE = 1024
MOE_SLABS = 2
COMBINE_CHUNKS = 4

BF16 = jnp.bfloat16
F32 = jnp.float32
NT_DIMS = (((1,), (1,)), ((), ()))


def _dot(a, b):
    return jnp.dot(a, b, preferred_element_type=F32)


def _layer_norm(x, g, b):
    mu = jnp.mean(x, axis=-1, keepdims=True)
    xc = x - mu
    var = jnp.mean(xc * xc, axis=-1, keepdims=True)
    return xc * lax.rsqrt(var + EPS) * g + b


def _rms_norm(x, g):
    return x * lax.rsqrt(jnp.mean(x * x, axis=-1, keepdims=True) + EPS) * g


def _inproj_kernel(x_ref, pos_ref, lng_ref, lnb_ref, w1_ref, qg_ref, kvg_ref,
                   wq_ref, wqr_ref, wk_ref, wv_ref, freq_ref,
                   xn_ref, q_ref, k_ref, v_ref, u_ref, *, ssm_width):
    xn = _layer_norm(x_ref[...], lng_ref[...], lnb_ref[...])
    xn_ref[...] = xn
    h = _dot(xn.astype(BF16), w1_ref[...])
    o1 = Q_RANK
    o2 = o1 + KV_RANK
    o3 = o2 + ssm_width
    o4 = o3 + HEAD_PAD
    cq = h[:, :o1]
    ckv = h[:, o1:o2]
    u_ref[...] = h[:, o2:o3]
    kr_raw = h[:, o3:o4]
    kr_rot = h[:, o4:o4 + HEAD_PAD]
    cqn = _rms_norm(cq, qg_ref[...]).astype(BF16)
    ckvn = _rms_norm(ckv, kvg_ref[...]).astype(BF16)

    ang = pos_ref[...] * freq_ref[...]
    c = jnp.cos(ang)
    s = jnp.sin(ang)
    lane = lax.broadcasted_iota(jnp.int32, (1, HEAD_PAD), 1)
    scale = (QK_NOPE + QK_ROPE) ** -0.5 * LOG2_E
    is_rope = (lane >= QK_NOPE) & (lane < QK_NOPE + QK_ROPE)
    cos1 = jnp.where(lane < QK_NOPE, 1.0, jnp.where(is_rope, c, 0.0)) * scale
    sin1 = jnp.where(is_rope, s, 0.0) * scale
    cos_t = jnp.concatenate([cos1] * MLA_HEADS, axis=1)
    sin_t = jnp.concatenate([sin1] * MLA_HEADS, axis=1)
    q = _dot(cqn, wq_ref[...]) * cos_t + _dot(cqn, wqr_ref[...]) * sin_t
    q_ref[...] = q.astype(BF16)

    kr = kr_raw * c + kr_rot * s
    k = _dot(ckvn, wk_ref[...]) + jnp.concatenate([kr] * MLA_HEADS, axis=1)
    k_ref[...] = k.astype(BF16)
    ones_col = jnp.concatenate([jnp.where(lane == V_DIM, 1.0, 0.0)] * MLA_HEADS, axis=1)
    v_ref[...] = (_dot(ckvn, wv_ref[...]) + ones_col).astype(BF16)


def _attn_kernel(q_ref, k_ref, v_ref, o_ref, s_ref, mx_ref, acc_ref, *, tq, heads):
    qi = pl.program_id(1)
    row_chunk = lax.broadcasted_iota(jnp.int32, (tq, tq), 0) // CHUNK
    col_chunk = lax.broadcasted_iota(jnp.int32, (tq, tq), 1) // CHUNK
    diag_mask = row_chunk >= col_chunk
    head_cols = [slice(h * HEAD_PAD, (h + 1) * HEAD_PAD) for h in range(heads)]
    mx_ref[...] = jnp.full(mx_ref.shape, -jnp.inf, F32)
    acc_ref[...] = jnp.zeros(acc_ref.shape, F32)

    def scores(j, masked):
        start = pl.multiple_of(j * tq, tq)
        for h in range(heads):
            s = lax.dot_general(q_ref[:, head_cols[h]], k_ref[pl.ds(start, tq), head_cols[h]], NT_DIMS,
                                preferred_element_type=F32)
            if masked:
                s = jnp.where(diag_mask, s, -jnp.inf)
            s_ref[h, j] = s
            lane_max = s[:, :LANES]
            for c in range(1, tq // LANES):
                lane_max = jnp.maximum(lane_max, s[:, c * LANES:(c + 1) * LANES])
            mx_ref[h] = jnp.maximum(mx_ref[h], lane_max)

    def scores_step(j, carry):
        scores(j, False)
        return carry

    lax.fori_loop(0, qi, scores_step, 0)
    scores(qi, True)
    row_max = [jnp.max(mx_ref[h], axis=-1, keepdims=True) for h in range(heads)]

    def values_step(j, carry):
        start = pl.multiple_of(j * tq, tq)
        for h in range(heads):
            p = jnp.exp2(s_ref[h, j] - row_max[h]).astype(BF16)
            acc_ref[h] += _dot(p, v_ref[pl.ds(start, tq), head_cols[h]])
        return carry

    lax.fori_loop(0, qi + 1, values_step, 0)
    outs = [acc_ref[h][:, :V_DIM] / acc_ref[h][:, V_DIM:V_DIM + 1] for h in range(heads)]
    o_ref[...] = jnp.concatenate(outs, axis=1).astype(o_ref.dtype)


def _s5_kernel(u_ref, win_ref, are_ref, aim_ref, cre_ref, cim_ref, dskip_ref,
               wglu_ref, bglu_ref, o_ref, vx_ref, hre_ref, him_ref, io_ref, utm_ref, *,
               batch, lt, n_state, scan_tiles):
    ti = pl.program_id(0)

    @pl.when(ti == 0)
    def _():
        hre_ref[...] = jnp.zeros_like(hre_ref)
        him_ref[...] = jnp.zeros_like(him_ref)

    width = u_ref.shape[-1]
    n_slab = width // LANES
    n_tiles = n_state // LANES
    slab_tiles = n_tiles // n_slab
    for b in range(batch):
        for c in range(n_slab):
            io_ref[c, b * lt:(b + 1) * lt, :] = u_ref[b, :, c * LANES:(c + 1) * LANES]

    def to_time_major(t, _):
        dst = pl.ds(pl.multiple_of(t * batch, batch), batch)
        for c in range(n_slab):
            utm_ref[dst, c * LANES:(c + 1) * LANES] = io_ref[c, pl.ds(t, batch, stride=lt), :]
        return 0

    lax.fori_loop(0, lt, to_time_major, 0, unroll=4)
    u2 = utm_ref[...]
    ub = u2.astype(BF16)
    for j in range(n_slab):
        vj = _dot(ub[:, j * LANES:(j + 1) * LANES], win_ref[j])
        for i in range(slab_tiles):
            vx_ref[j * slab_tiles + i] = vj[:, i * LANES:(i + 1) * LANES]
            vx_ref[n_tiles + j * slab_tiles + i] = vj[:, (slab_tiles + i) * LANES:(slab_tiles + i + 1) * LANES]

    for c0 in range(0, n_tiles, scan_tiles):
        tiles = range(c0, c0 + scan_tiles)
        ar = [jnp.broadcast_to(are_ref[:, c * LANES:(c + 1) * LANES], (batch, LANES)) for c in tiles]
        ai = [jnp.broadcast_to(aim_ref[:, c * LANES:(c + 1) * LANES], (batch, LANES)) for c in tiles]

        def step(t, carry, tiles=tiles, ar=ar, ai=ai):
            rows = pl.ds(pl.multiple_of(t * batch, batch), batch)
            out = []
            for n, c in enumerate(tiles):
                hr, hi = carry[2 * n], carry[2 * n + 1]
                nr = ar[n] * hr - ai[n] * hi + vx_ref[c, rows, :]
                ni = ar[n] * hi + ai[n] * hr + vx_ref[n_tiles + c, rows, :]
                vx_ref[c, rows, :] = nr
                vx_ref[n_tiles + c, rows, :] = ni
                out += [nr, ni]
            return tuple(out)

        init = []
        for c in tiles:
            init += [hre_ref[c], him_ref[c]]
        fin = lax.fori_loop(0, lt, step, tuple(init), unroll=4)
        for n, c in enumerate(tiles):
            hre_ref[c] = fin[2 * n]
            him_ref[c] = fin[2 * n + 1]

    ys = []
    for j in range(n_slab):
        xr = jnp.concatenate([vx_ref[j * slab_tiles + i].astype(BF16) for i in range(slab_tiles)], axis=1)
        xi = jnp.concatenate([vx_ref[n_tiles + j * slab_tiles + i].astype(BF16) for i in range(slab_tiles)], axis=1)
        ys.append(_dot(xr, cre_ref[j]) + _dot(xi, cim_ref[j]))
    y = jnp.concatenate(ys, axis=1) + dskip_ref[...] * u2
    y = jax.nn.gelu(y)
    z = _dot(y.astype(BF16), wglu_ref[...]) + bglu_ref[...]
    out = y * jax.nn.sigmoid(z)
    for c in range(n_slab):
        io_ref[c] = out[:, c * LANES:(c + 1) * LANES]
    for b in range(batch):
        for c in range(n_slab):
            o_ref[b, :, c * LANES:(c + 1) * LANES] = io_ref[c, pl.ds(b, lt, stride=batch), :].astype(o_ref.dtype)


def _router_gates(logits_t, rbias):
    n_exp, tm = logits_t.shape
    per_group = n_exp // N_GROUPS
    scores = jax.nn.sigmoid(logits_t)
    sel = scores + rbias
    neg_inf = -jnp.inf
    sub_iota = lax.broadcasted_iota(jnp.int32, (per_group, tm), 0).astype(F32)
    group_score = []
    for g in range(N_GROUPS):
        sg = sel[g * per_group:(g + 1) * per_group, :]
        m1 = jnp.max(sg, axis=0, keepdims=True)
        first = jnp.min(jnp.where(sg == m1, sub_iota, float(per_group)), axis=0, keepdims=True)
        m2 = jnp.max(jnp.where(sub_iota == first, neg_inf, sg), axis=0, keepdims=True)
        group_score.append(m1 + m2)
    masked = []
    for g in range(N_GROUPS):
        rank = jnp.zeros((1, tm), F32)
        for g2 in range(N_GROUPS):
            if g2 == g:
                continue
            ahead = (group_score[g2] >= group_score[g]) if g2 < g else (group_score[g2] > group_score[g])
            rank = rank + jnp.where(ahead, 1.0, 0.0)
        keep = rank < float(TOP_GROUPS)
        masked.append(jnp.where(keep, sel[g * per_group:(g + 1) * per_group, :], neg_inf))
    cur = jnp.concatenate(masked, axis=0)
    iota = lax.broadcasted_iota(jnp.int32, (n_exp, tm), 0).astype(F32)
    chosen = jnp.zeros((n_exp, tm), F32)
    picks, weights = [], []
    for _ in range(TOP_K):
        m = jnp.max(cur, axis=0, keepdims=True)
        idx = jnp.min(jnp.where(cur == m, iota, float(n_exp)), axis=0, keepdims=True)
        pick = iota == idx
        chosen = jnp.where(pick, 1.0, chosen)
        cur = jnp.where(pick, neg_inf, cur)
        picks.append(idx)
        weights.append(jnp.sum(jnp.where(pick, scores, 0.0), axis=0, keepdims=True))
    idx_k = jnp.concatenate(picks, axis=0)
    w_k = jnp.concatenate(weights, axis=0)
    gate_k = w_k / jnp.sum(w_k, axis=0, keepdims=True) * ROUTED_SCALE
    return idx_k, gate_k, chosen


def _pack_bf16_pairs(x):
    n = x.shape[1] // 2
    hi = lax.bitcast_convert_type(x[:, :n].astype(BF16).astype(F32), jnp.int32)
    lo = lax.bitcast_convert_type(x[:, n:].astype(BF16).astype(F32), jnp.int32)
    return hi | lax.shift_right_logical(lo, 16)


def _unpack_bf16_pairs(p):
    hi = lax.bitcast_convert_type(p & jnp.int32(-65536), F32).astype(BF16)
    lo = lax.bitcast_convert_type(lax.shift_left(p, 16), F32).astype(BF16)
    return hi, lo


def _mix_kernel(att_ref, ssm_ref, xn_ref, ag_ref, sg_ref, woa_ref, wos_ref,
                g_ref, b_ref, wrh_ref, wrl_ref, rb_ref,
                x1_ref, x1p_ref, idx_ref, gate_ref, rank_ref, cnt_ref, carry_ref):
    @pl.when(pl.program_id(0) == 0)
    def _():
        carry_ref[...] = jnp.zeros_like(carry_ref)

    an = _rms_norm(att_ref[...].astype(F32), ag_ref[...]).astype(BF16)
    sn = _rms_norm(ssm_ref[...].astype(F32), sg_ref[...]).astype(BF16)
    mix = _dot(an, woa_ref[...]) + _dot(sn, wos_ref[...])
    x1 = _layer_norm(ALPHA * xn_ref[...] + mix, g_ref[...], b_ref[...])
    x1_ref[...] = x1
    x1p_ref[...] = _pack_bf16_pairs(x1)
    x_hi = x1.astype(BF16)
    x_lo = (x1 - x_hi.astype(F32)).astype(BF16)
    dg = functools.partial(lax.dot_general, dimension_numbers=NT_DIMS, preferred_element_type=F32)
    logits_t = dg(wrh_ref[...], x_hi) + dg(wrl_ref[...], x_hi) + dg(wrh_ref[...], x_lo)
    idx_k, gate_k, chosen = _router_gates(logits_t, rb_ref[...])
    idx_ref[...] = idx_k.astype(jnp.int32)
    gate_ref[...] = gate_k

    n_exp, tm = chosen.shape
    before = (lax.broadcasted_iota(jnp.int32, (tm, tm), 0)
              < lax.broadcasted_iota(jnp.int32, (tm, tm), 1))
    excl = _dot(chosen.astype(BF16), jnp.where(before, 1.0, 0.0).astype(BF16))
    rank_full = carry_ref[...] + excl
    iota = lax.broadcasted_iota(jnp.int32, (n_exp, tm), 0).astype(F32)
    ranks = [jnp.sum(jnp.where(iota == idx_k[k:k + 1, :], rank_full, 0.0), axis=0, keepdims=True)
             for k in range(TOP_K)]
    rank_ref[...] = jnp.concatenate(ranks, axis=0).astype(jnp.int32)
    total = carry_ref[...] + jnp.sum(chosen, axis=1, keepdims=True)
    carry_ref[...] = total
    cnt_ref[...] = jnp.broadcast_to(total, cnt_ref.shape)


def _positions_kernel(off_ref, idx_ref, rank_ref, pos_ref):
    idx = idx_ref[...]
    base = jnp.zeros(idx.shape, jnp.int32)
    for e in range(N_EXPERTS):
        base = jnp.where(idx == e, off_ref[e], base)
    pos_ref[...] = rank_ref[...] + base


def _swiglu(x_hi, x_lo, wg_ref, wu_ref, wd_ref):
    half = x_hi.shape[1]
    hg = _dot(x_hi, wg_ref[:half, :]) + _dot(x_lo, wg_ref[half:, :])
    hu = _dot(x_hi, wu_ref[:half, :]) + _dot(x_lo, wu_ref[half:, :])
    h = jax.nn.silu(hg) * hu
    return _dot(h.astype(BF16), wd_ref[...])


def _experts_kernel(te_ref, valid_ref, xs_ref, wg_ref, wu_ref, wd_ref, ys_ref,
                    wgb_ref, wub_ref, wdb_ref):
    i = pl.program_id(0)
    valid = valid_ref[i]

    @pl.when((i == 0) | (te_ref[i] != te_ref[jnp.maximum(i - 1, 0)]))
    def _():
        wgb_ref[...] = wg_ref[0].astype(BF16)
        wub_ref[...] = wu_ref[0].astype(BF16)
        wdb_ref[...] = wd_ref[0].astype(BF16)

    @pl.when(valid > 0)
    def _():
        rows = lax.broadcasted_iota(jnp.int32, xs_ref.shape, 0)
        x_hi, x_lo = _unpack_bf16_pairs(jnp.where(rows < valid, xs_ref[...], 0))
        ys_ref[...] = _pack_bf16_pairs(_swiglu(x_hi, x_lo, wgb_ref, wub_ref, wdb_ref))

    @pl.when(valid == 0)
    def _():
        ys_ref[...] = jnp.zeros_like(ys_ref)


def _combine_kernel(yg_ref, gate_ref, x1_ref, wsg_ref, wsu_ref, wsd_ref, g_ref, b_ref, o_ref):
    x1 = x1_ref[...]
    half = x1.shape[1] // 2
    acc = _swiglu(x1[:, :half].astype(BF16), x1[:, half:].astype(BF16), wsg_ref, wsu_ref, wsd_ref)
    gates = gate_ref[...]
    acc_hi, acc_lo = acc[:, :half], acc[:, half:]
    for k in range(TOP_K):
        y_hi, y_lo = _unpack_bf16_pairs(yg_ref[k])
        gk = gates[:, k:k + 1]
        acc_hi = acc_hi + gk * y_hi.astype(F32)
        acc_lo = acc_lo + gk * y_lo.astype(F32)
    ffn = jnp.concatenate([acc_hi, acc_lo], axis=1)
    o_ref[...] = _layer_norm(ALPHA * x1 + ffn, g_ref[...], b_ref[...])


def _combine_into_kernel(prev_ref, *refs):
    del prev_ref
    _combine_kernel(*refs)


SC_CORES = 2
SC_SUBCORES = 16
SC_WORKERS = SC_CORES * SC_SUBCORES
SC_CHUNK = 64


def _sc_mesh():
    return plsc.VectorSubcoreMesh(core_axis_name="c", subcore_axis_name="s")


def _sc_gather_rows(table, idx):
    n = idx.shape[0]
    d = table.shape[1]
    per_w = n // SC_WORKERS
    n_ch = per_w // SC_CHUNK

    @functools.partial(
        pl.kernel, mesh=_sc_mesh(),
        out_type=jax.ShapeDtypeStruct((n, d), table.dtype),
        scratch_types=[pltpu.VMEM((n_ch, SC_CHUNK), jnp.int32),
                       pltpu.VMEM((2, SC_CHUNK, d), table.dtype),
                       pltpu.SemaphoreType.DMA((2,)),
                       pltpu.SemaphoreType.DMA((2,))],
    )
    def k(table_hbm, idx_hbm, out_hbm, idx_v, buf, gsem, osem):
        wid = lax.axis_index("s") * SC_CORES + lax.axis_index("c")
        base = wid * per_w
        pltpu.sync_copy(idx_hbm.at[wid], idx_v)

        def gather(c, b):
            return pltpu.make_async_copy(table_hbm.at[idx_v.at[c]], buf.at[b], gsem.at[b])

        def put(c, b):
            return pltpu.make_async_copy(buf.at[b], out_hbm.at[pl.ds(base + c * SC_CHUNK, SC_CHUNK)],
                                         osem.at[b])

        gather(0, 0).start()

        @pl.loop(0, n_ch, step=2)
        def _(c):
            for b in range(2):
                cc = c + b
                gather(cc, b).wait()

                @pl.when(cc + 1 < n_ch)
                def _():
                    @pl.when(cc >= 1)
                    def _():
                        put(cc - 1, 1 - b).wait()
                    gather(cc + 1, 1 - b).start()

                put(cc, b).start()

        put(n_ch - 2, 0).wait()
        put(n_ch - 1, 1).wait()

    return k(table, idx.reshape(SC_WORKERS, n_ch, SC_CHUNK))


def _sc_scatter_rows(x, pos, n_out):
    t, d = x.shape
    kk = pos.shape[0]
    per_w = t // SC_WORKERS
    n_ch = per_w // SC_CHUNK
    pos_w = pos.reshape(kk, SC_WORKERS, n_ch, SC_CHUNK).transpose(1, 2, 0, 3)
    pos_w = pos_w.reshape(SC_WORKERS, n_ch * kk, SC_CHUNK)

    @functools.partial(
        pl.kernel, mesh=_sc_mesh(),
        out_type=jax.ShapeDtypeStruct((n_out, d), x.dtype),
        scratch_types=[pltpu.VMEM((n_ch * kk, SC_CHUNK), jnp.int32),
                       pltpu.VMEM((2, SC_CHUNK, d), x.dtype),
                       pltpu.SemaphoreType.DMA((2,)),
                       pltpu.SemaphoreType.DMA((2,))],
    )
    def k(x_hbm, pos_hbm, out_hbm, idx_v, buf, isem, osem):
        wid = lax.axis_index("s") * SC_CORES + lax.axis_index("c")
        base = wid * per_w
        pltpu.sync_copy(pos_hbm.at[wid], idx_v)

        def get(c, b):
            return pltpu.make_async_copy(x_hbm.at[pl.ds(base + c * SC_CHUNK, SC_CHUNK)], buf.at[b],
                                         isem.at[b])

        def put(c, j, b):
            return pltpu.make_async_copy(buf.at[b], out_hbm.at[idx_v.at[c * kk + j]], osem.at[b])

        get(0, 0).start()

        @pl.loop(0, n_ch, step=2)
        def _(c):
            for b in range(2):
                cc = c + b
                get(cc, b).wait()

                @pl.when(cc + 1 < n_ch)
                def _():
                    @pl.when(cc >= 1)
                    def _():
                        for j in range(kk):
                            put(cc - 1, j, 1 - b).wait()
                    get(cc + 1, 1 - b).start()

                for j in range(kk):
                    put(cc, j, b).start()

        for j in range(kk):
            put(n_ch - 2, j, 0).wait()
        for j in range(kk):
            put(n_ch - 1, j, 1).wait()

    return k(x, pos_w)


def _row(v):
    return v.reshape(1, -1).astype(F32)


def _const_spec(shape):
    nd = len(shape)
    return pl.BlockSpec(shape, lambda *_: (0,) * nd)


def _pad_heads(w, width):
    r, h, _ = w.shape
    return jnp.pad(w, ((0, 0), (0, 0), (0, HEAD_PAD - width))).reshape(r, h * HEAD_PAD)


def _half_rotate(w):
    half = QK_ROPE // 2
    return jnp.concatenate([-w[..., half:], w[..., :half]], axis=-1)


def kernel(x, positions, ln_in_g, ln_in_b, w_in, q_norm_g, kv_norm_g, w_uq, w_ukv, lambda_re, lambda_im, log_step, b_re, b_im, c_re, c_im, d_skip, w_glu, b_glu, attn_out_g, ssm_out_g, w_o, ln1_g, ln1_b, w_router, router_bias, w_gate, w_up, w_down, ws_gate, ws_up, ws_down, ln2_g, ln2_b):
    B, S, D = x.shape
    T = B * S
    l = 0
    ssm_width = w_glu.shape[-1]
    n_groups = ssm_width // SSM_GROUP
    n_state = n_groups * SSM_STATE
    mla_width = MLA_HEADS * V_DIM
    qk_pad = MLA_HEADS * HEAD_PAD
    cparams = functools.partial(pltpu.CompilerParams, vmem_limit_bytes=VMEM_LIMIT)

    s1, s2, s3 = Q_RANK, Q_RANK + KV_RANK, Q_RANK + KV_RANK + QK_ROPE
    wi = w_in[l]
    w_kr = wi[:, s2:s3]
    pad_rope = lambda w: jnp.pad(w, ((0, 0), (QK_NOPE, HEAD_PAD - QK_NOPE - QK_ROPE)))
    w1 = jnp.concatenate([wi[:, :s2], wi[:, s3:], pad_rope(w_kr), pad_rope(_half_rotate(w_kr))],
                         axis=1).astype(BF16)
    wq = w_uq[l]
    zeros_nope = jnp.zeros(wq.shape[:2] + (QK_NOPE,), wq.dtype)
    wq_main = _pad_heads(wq, QK_NOPE + QK_ROPE).astype(BF16)
    wq_rot = _pad_heads(jnp.concatenate([zeros_nope, _half_rotate(wq[..., QK_NOPE:])], axis=-1),
                        QK_NOPE + QK_ROPE).astype(BF16)
    wkv = w_ukv[l]
    wk = _pad_heads(wkv[..., :QK_NOPE], QK_NOPE).astype(BF16)
    wv = _pad_heads(wkv[..., QK_NOPE:], V_DIM).astype(BF16)
    half = QK_ROPE // 2
    inv_freq = ROPE_THETA ** (-jnp.arange(half, dtype=F32) / half)
    freq = jnp.pad(jnp.concatenate([inv_freq, inv_freq]),
                   (QK_NOPE, HEAD_PAD - QK_NOPE - QK_ROPE)).reshape(1, HEAD_PAD)
    pos_f = positions.astype(F32).reshape(T, 1)

    tm = min(512, T)
    w1_cols = w1.shape[1]
    tok = lambda width: pl.BlockSpec((tm, width), lambda i: (i, 0))
    xn, q, k, v, u = pl.pallas_call(
        functools.partial(_inproj_kernel, ssm_width=ssm_width),
        grid=(T // tm,),
        in_specs=[tok(D), tok(1), _const_spec((1, D)), _const_spec((1, D)),
                  _const_spec((D, w1_cols)), _const_spec((1, Q_RANK)), _const_spec((1, KV_RANK)),
                  _const_spec((Q_RANK, qk_pad)), _const_spec((Q_RANK, qk_pad)),
                  _const_spec((KV_RANK, qk_pad)), _const_spec((KV_RANK, qk_pad)),
                  _const_spec((1, HEAD_PAD))],
        out_specs=[tok(D), tok(qk_pad), tok(qk_pad), tok(qk_pad), tok(ssm_width)],
        out_shape=[jax.ShapeDtypeStruct((T, D), F32), jax.ShapeDtypeStruct((T, qk_pad), BF16),
                   jax.ShapeDtypeStruct((T, qk_pad), BF16), jax.ShapeDtypeStruct((T, qk_pad), BF16),
                   jax.ShapeDtypeStruct((T, ssm_width), F32)],
        compiler_params=cparams(dimension_semantics=("parallel",)),
        name="inproj",
    )(x.reshape(T, D), pos_f, _row(ln_in_g), _row(ln_in_b), w1, _row(q_norm_g[l]), _row(kv_norm_g[l]),
      wq_main, wq_rot, wk, wv, freq)

    tq = min(256, S)
    nq = S // tq
    att = pl.pallas_call(
        functools.partial(_attn_kernel, tq=tq, heads=MLA_HEADS),
        grid=(B, nq),
        in_specs=[pl.BlockSpec((tq, qk_pad), lambda b, i: (b * nq + i, 0)),
                  pl.BlockSpec((S, qk_pad), lambda b, i: (b, 0)),
                  pl.BlockSpec((S, qk_pad), lambda b, i: (b, 0))],
        out_specs=pl.BlockSpec((tq, mla_width), lambda b, i: (b * nq + i, 0)),
        out_shape=jax.ShapeDtypeStruct((T, mla_width), BF16),
        scratch_shapes=[pltpu.VMEM((MLA_HEADS, nq, tq, tq), F32),
                        pltpu.VMEM((MLA_HEADS, tq, LANES), F32),
                        pltpu.VMEM((MLA_HEADS, tq, HEAD_PAD), F32)],
        compiler_params=cparams(dimension_semantics=("parallel", "arbitrary")),
        name="attention",
    )(q, k, v)

    lam = lax.complex(jnp.minimum(lambda_re[l].astype(F32), -1e-4), lambda_im[l].astype(F32))
    step = jnp.exp(log_step[l].astype(F32))[:, None]
    lam_bar = jnp.exp(lam * step)
    b_bar = ((lam_bar - 1.0) / lam)[..., None] * lax.complex(b_re[l].astype(F32), b_im[l].astype(F32))
    n_slab = ssm_width // LANES
    g_per_slab = n_groups // n_slab
    eye = jnp.eye(n_groups, dtype=F32)

    def expand_in(bpart):
        return jnp.einsum('gpc,gh->gchp', bpart, eye).reshape(ssm_width, n_state)

    def expand_out(cpart):
        return jnp.einsum('gcp,gh->gphc', cpart, eye).reshape(n_state, ssm_width)

    slab = n_state // n_slab
    win_re, win_im = expand_in(jnp.real(b_bar)), expand_in(jnp.imag(b_bar))
    win = jnp.stack([jnp.concatenate([win_re[j * LANES:(j + 1) * LANES, j * slab:(j + 1) * slab],
                                      win_im[j * LANES:(j + 1) * LANES, j * slab:(j + 1) * slab]], axis=1)
                     for j in range(n_slab)]).astype(BF16)
    wc_re, wc_im = expand_out(c_re[l].astype(F32)), expand_out(-c_im[l].astype(F32))
    cre = jnp.stack([wc_re[j * slab:(j + 1) * slab, j * LANES:(j + 1) * LANES] for j in range(n_slab)]).astype(BF16)
    cim = jnp.stack([wc_im[j * slab:(j + 1) * slab, j * LANES:(j + 1) * LANES] for j in range(n_slab)]).astype(BF16)
    a_re = jnp.real(lam_bar).reshape(1, n_state)
    a_im = jnp.imag(lam_bar).reshape(1, n_state)

    lt = min(64, S)
    ssm = pl.pallas_call(
        functools.partial(_s5_kernel, batch=B, lt=lt, n_state=n_state, scan_tiles=4),
        grid=(S // lt,),
        in_specs=[pl.BlockSpec((B, lt, ssm_width), lambda t: (0, t, 0)),
                  _const_spec(win.shape), _const_spec((1, n_state)), _const_spec((1, n_state)),
                  _const_spec(cre.shape), _const_spec(cim.shape), _const_spec((1, ssm_width)),
                  _const_spec((ssm_width, ssm_width)), _const_spec((1, ssm_width))],
        out_specs=pl.BlockSpec((B, lt, ssm_width), lambda t: (0, t, 0)),
        out_shape=jax.ShapeDtypeStruct((B, S, ssm_width), BF16),
        scratch_shapes=[pltpu.VMEM((2 * n_state // LANES, B * lt, LANES), F32),
                        pltpu.VMEM((n_state // LANES, B, LANES), F32),
                        pltpu.VMEM((n_state // LANES, B, LANES), F32),
                        pltpu.VMEM((ssm_width // LANES, B * lt, LANES), F32),
                        pltpu.VMEM((B * lt, ssm_width), F32)],
        compiler_params=cparams(dimension_semantics=("arbitrary",)),
        name="s5",
    )(u.reshape(B, S, ssm_width), win, a_re, a_im, cre, cim, _row(d_skip[l]),
      w_glu[l].astype(BF16), _row(b_glu[l]))

    wo = w_o[l].astype(BF16)
    wr_t = w_router[l].T.astype(F32)
    wr_hi = wr_t.astype(BF16)
    wr_lo = (wr_t - wr_hi.astype(F32)).astype(BF16)
    half = D // 2
    n_slabs = MOE_SLABS
    ts = T // n_slabs
    nt = ts // tm
    ssm2 = ssm.reshape(T, ssm_width)
    rbias = router_bias[l].astype(F32).reshape(N_EXPERTS, 1)
    kt = lambda dt: jax.ShapeDtypeStruct((TOP_K, ts), dt)
    k_spec = pl.BlockSpec((TOP_K, tm), lambda i: (0, i))

    def route(s):
        tok_s = lambda width: pl.BlockSpec((tm, width), lambda i: (i + s * nt, 0))
        return pl.pallas_call(
            _mix_kernel,
            grid=(nt,),
            in_specs=[tok_s(mla_width), tok_s(ssm_width), tok_s(D), _const_spec((1, mla_width)),
                      _const_spec((1, ssm_width)), _const_spec((mla_width, D)),
                      _const_spec((ssm_width, D)), _const_spec((1, D)), _const_spec((1, D)),
                      _const_spec((N_EXPERTS, D)), _const_spec((N_EXPERTS, D)),
                      _const_spec((N_EXPERTS, 1))],
            out_specs=[tok(D), tok(half), k_spec, k_spec, k_spec, _const_spec((N_EXPERTS, LANES))],
            out_shape=[jax.ShapeDtypeStruct((ts, D), F32), jax.ShapeDtypeStruct((ts, half), jnp.int32),
                       kt(jnp.int32), kt(F32), kt(jnp.int32),
                       jax.ShapeDtypeStruct((N_EXPERTS, LANES), F32)],
            scratch_shapes=[pltpu.VMEM((N_EXPERTS, 1), F32)],
            compiler_params=cparams(dimension_semantics=("arbitrary",)),
            name="mix_router",
        )(att, ssm2, xn, _row(attn_out_g[l]), _row(ssm_out_g[l]), wo[:mla_width], wo[mla_width:],
          _row(ln1_g[l]), _row(ln1_b[l]), wr_hi, wr_lo, rbias)

    tr = EXPERT_ROW_TILE
    n_tiles = (ts * TOP_K) // tr + N_EXPERTS
    n_rows = n_tiles * tr
    tp = min(2048, ts)

    def dispatch(x1p, idx_k, rank_k, counts):
        cnt = counts[:, 0].astype(jnp.int32)
        tiles_e = (cnt + tr - 1) // tr
        tile_end = jnp.cumsum(tiles_e)
        tile_start = tile_end - tiles_e
        tile_ids = jnp.arange(n_tiles, dtype=jnp.int32)
        tile_expert = jnp.sum((tile_end[None, :] <= tile_ids[:, None]).astype(jnp.int32), axis=1)
        tile_expert = jnp.minimum(tile_expert, N_EXPERTS - 1)
        owner = (tile_start[None, :] <= tile_ids[:, None]) & (tile_ids[:, None] < tile_end[None, :])
        left = jnp.sum(jnp.where(owner, cnt[None, :] - (tile_ids[:, None] - tile_start[None, :]) * tr, 0),
                       axis=1)
        tile_valid = jnp.clip(left, 0, tr).astype(jnp.int32)
        pos = pl.pallas_call(
            _positions_kernel,
            grid_spec=pltpu.PrefetchScalarGridSpec(
                num_scalar_prefetch=1, grid=(ts // tp,),
                in_specs=[pl.BlockSpec((TOP_K, tp), lambda i, off: (0, i)),
                          pl.BlockSpec((TOP_K, tp), lambda i, off: (0, i))],
                out_specs=pl.BlockSpec((TOP_K, tp), lambda i, off: (0, i))),
            out_shape=kt(jnp.int32),
            name="positions",
        )((tile_start * tr).astype(jnp.int32), idx_k, rank_k)
        return _sc_scatter_rows(x1p, pos, n_rows), pos, tile_expert, tile_valid

    wg, wu, wd = w_gate[l], w_up[l], w_down[l]
    ff = wg.shape[-1]

    def experts(xs, tile_expert, tile_valid):
        return pl.pallas_call(
            _experts_kernel,
            grid_spec=pltpu.PrefetchScalarGridSpec(
                num_scalar_prefetch=2, grid=(n_tiles,),
                in_specs=[pl.BlockSpec((tr, half), lambda i, te, tv: (i, 0)),
                          pl.BlockSpec((1, D, ff), lambda i, te, tv: (te[i], 0, 0)),
                          pl.BlockSpec((1, D, ff), lambda i, te, tv: (te[i], 0, 0)),
                          pl.BlockSpec((1, ff, D), lambda i, te, tv: (te[i], 0, 0))],
                out_specs=pl.BlockSpec((tr, half), lambda i, te, tv: (i, 0)),
                scratch_shapes=[pltpu.VMEM((D, ff), BF16), pltpu.VMEM((D, ff), BF16),
                                pltpu.VMEM((ff, D), BF16)]),
            out_shape=jax.ShapeDtypeStruct((n_rows, half), jnp.int32),
            compiler_params=cparams(dimension_semantics=("arbitrary",)),
            name="experts",
        )(tile_expert, tile_valid, xs, wg, wu, wd)

    shared = (ws_gate[l].astype(BF16), ws_up[l].astype(BF16), ws_down[l].astype(BF16))

    n_chunks = COMBINE_CHUNKS
    tc = ts // n_chunks
    ntc = nt // n_chunks

    def combine(s, c, out_so_far, yg, gate_k, x1):
        specs = [pl.BlockSpec((TOP_K, tm, half), lambda i: (0, i, 0)),
                 pl.BlockSpec((tm, TOP_K), lambda i: (i, 0)),
                 pl.BlockSpec((tm, D), lambda i: (i + c * ntc, 0)),
                 _const_spec((D, ff)), _const_spec((D, ff)), _const_spec((ff, D)),
                 _const_spec((1, D)), _const_spec((1, D))]
        args = (yg.reshape(TOP_K, tc, half), gate_k.T, x1, *shared, _row(ln2_g[l]), _row(ln2_b[l]))
        body, aliases = _combine_kernel, {}
        if out_so_far is not None:
            specs = [pl.BlockSpec(memory_space=pl.ANY)] + specs
            args = (out_so_far,) + args
            body, aliases = _combine_into_kernel, {0: 0}
        return pl.pallas_call(
            body,
            grid=(ntc,),
            in_specs=specs,
            out_specs=pl.BlockSpec((tm, D), lambda i: (i + s * nt + c * ntc, 0)),
            out_shape=jax.ShapeDtypeStruct((T, D), F32),
            input_output_aliases=aliases,
            compiler_params=cparams(dimension_semantics=("parallel",)),
            name="combine",
        )(*args)

    routed, moved = [], []
    for s in range(n_slabs):
        x1, x1p, idx_k, gate_k, rank_k, counts = route(s)
        routed.append((x1, gate_k))
        moved.append(dispatch(x1p, idx_k, rank_k, counts))
    gathered = []
    for xs, pos, tile_expert, tile_valid in moved:
        ys = experts(xs, tile_expert, tile_valid)
        gathered.append([_sc_gather_rows(ys, pos[:, c * tc:(c + 1) * tc].reshape(TOP_K * tc))
                         for c in range(n_chunks)])
    out = None
    for s in range(n_slabs):
        x1, gate_k = routed[s]
        for c in range(n_chunks):
            out = combine(s, c, out, gathered[s][c], gate_k[:, c * tc:(c + 1) * tc], x1)
    return out.reshape(B, S, D)
```

```python
import functools

import jax
import jax.numpy as jnp
from jax import lax
from jax.experimental import pallas as pl
from jax.experimental.pallas import tpu as pltpu
from jax.experimental.pallas import tpu_sc as plsc

CHUNK = 64
MLA_HEADS = 8
QK_NOPE = 64
QK_ROPE = 32
V_DIM = 64
Q_RANK = 256
KV_RANK = 128
ROPE_THETA = 10000.0
SSM_GROUP = 16
SSM_STATE = 64
N_EXPERTS = 64
TOP_K = 8
N_GROUPS = 8
TOP_GROUPS = 4
ROUTED_SCALE = 2.5
DEPTH = 1
ALPHA = (2.0 * DEPTH) ** 0.25
EPS = 1e-5
LOG2_E = 1.4426950408889634

LANES = 128
HEAD_PAD = LANES
VMEM_LIMIT = 56 * 1024 * 1024
EXPERT_ROW_TILE = 1024
MOE_SLABS = 2

BF16 = jnp.bfloat16
F32 = jnp.float32
NT_DIMS = (((1,), (1,)), ((), ()))


def _dot(a, b):
    return jnp.dot(a, b, preferred_element_type=F32)


def _layer_norm(x, g, b):
    mu = jnp.mean(x, axis=-1, keepdims=True)
    xc = x - mu
    var = jnp.mean(xc * xc, axis=-1, keepdims=True)
    return xc * lax.rsqrt(var + EPS) * g + b


def _rms_norm(x, g):
    return x * lax.rsqrt(jnp.mean(x * x, axis=-1, keepdims=True) + EPS) * g


def _inproj_kernel(x_ref, pos_ref, lng_ref, lnb_ref, w1_ref, qg_ref, kvg_ref,
                   wq_ref, wqr_ref, wk_ref, wv_ref, freq_ref,
                   xn_ref, q_ref, k_ref, v_ref, u_ref, *, ssm_width):
    xn = _layer_norm(x_ref[...], lng_ref[...], lnb_ref[...])
    xn_ref[...] = xn
    h = _dot(xn.astype(BF16), w1_ref[...])
    o1 = Q_RANK
    o2 = o1 + KV_RANK
    o3 = o2 + ssm_width
    o4 = o3 + HEAD_PAD
    cq = h[:, :o1]
    ckv = h[:, o1:o2]
    u_ref[...] = h[:, o2:o3]
    kr_raw = h[:, o3:o4]
    kr_rot = h[:, o4:o4 + HEAD_PAD]
    cqn = _rms_norm(cq, qg_ref[...]).astype(BF16)
    ckvn = _rms_norm(ckv, kvg_ref[...]).astype(BF16)

    ang = pos_ref[...] * freq_ref[...]
    c = jnp.cos(ang)
    s = jnp.sin(ang)
    lane = lax.broadcasted_iota(jnp.int32, (1, HEAD_PAD), 1)
    scale = (QK_NOPE + QK_ROPE) ** -0.5 * LOG2_E
    is_rope = (lane >= QK_NOPE) & (lane < QK_NOPE + QK_ROPE)
    cos1 = jnp.where(lane < QK_NOPE, 1.0, jnp.where(is_rope, c, 0.0)) * scale
    sin1 = jnp.where(is_rope, s, 0.0) * scale
    cos_t = jnp.concatenate([cos1] * MLA_HEADS, axis=1)
    sin_t = jnp.concatenate([sin1] * MLA_HEADS, axis=1)
    q = _dot(cqn, wq_ref[...]) * cos_t + _dot(cqn, wqr_ref[...]) * sin_t
    q_ref[...] = q.astype(BF16)

    kr = kr_raw * c + kr_rot * s
    k = _dot(ckvn, wk_ref[...]) + jnp.concatenate([kr] * MLA_HEADS, axis=1)
    k_ref[...] = k.astype(BF16)
    ones_col = jnp.concatenate([jnp.where(lane == V_DIM, 1.0, 0.0)] * MLA_HEADS, axis=1)
    v_ref[...] = (_dot(ckvn, wv_ref[...]) + ones_col).astype(BF16)


def _attn_kernel(q_ref, k_ref, v_ref, o_ref, s_ref, mx_ref, acc_ref, *, tq, heads):
    qi = pl.program_id(1)
    row_chunk = lax.broadcasted_iota(jnp.int32, (tq, tq), 0) // CHUNK
    col_chunk = lax.broadcasted_iota(jnp.int32, (tq, tq), 1) // CHUNK
    diag_mask = row_chunk >= col_chunk
    head_cols = [slice(h * HEAD_PAD, (h + 1) * HEAD_PAD) for h in range(heads)]
    mx_ref[...] = jnp.full(mx_ref.shape, -jnp.inf, F32)
    acc_ref[...] = jnp.zeros(acc_ref.shape, F32)

    def scores(j, nblk, masked):
        start = pl.multiple_of(j * tq, tq)
        for h in range(heads):
            s = lax.dot_general(q_ref[:, head_cols[h]], k_ref[pl.ds(start, nblk * tq), head_cols[h]],
                                NT_DIMS, preferred_element_type=F32)
            if masked:
                s = jnp.where(diag_mask, s, -jnp.inf)
            for b in range(nblk):
                s_ref[h, j + b] = s[:, b * tq:(b + 1) * tq]
            lane_max = s[:, :LANES]
            for c in range(1, nblk * tq // LANES):
                lane_max = jnp.maximum(lane_max, s[:, c * LANES:(c + 1) * LANES])
            mx_ref[h] = jnp.maximum(mx_ref[h], lane_max)

    def values(j, nblk):
        start = pl.multiple_of(j * tq, tq)
        for h in range(heads):
            s = jnp.concatenate([s_ref[h, j + b] for b in range(nblk)], axis=1)
            p = jnp.exp2(s - row_max[h]).astype(BF16)
            acc_ref[h] += _dot(p, v_ref[pl.ds(start, nblk * tq), head_cols[h]])

    def pairs(fn):
        def step(pair, carry):
            fn(2 * pair, 2)
            return carry
        return step

    lax.fori_loop(0, qi // 2, pairs(lambda j, n: scores(j, n, False)), 0)

    @pl.when(qi % 2 == 1)
    def _():
        scores(qi - 1, 1, False)

    scores(qi, 1, True)
    row_max = [jnp.max(mx_ref[h], axis=-1, keepdims=True) for h in range(heads)]
    lax.fori_loop(0, (qi + 1) // 2, pairs(values), 0)

    @pl.when(qi % 2 == 0)
    def _():
        values(qi, 1)
    outs = [acc_ref[h][:, :V_DIM] / acc_ref[h][:, V_DIM:V_DIM + 1] for h in range(heads)]
    o_ref[...] = jnp.concatenate(outs, axis=1).astype(o_ref.dtype)


def _s5_kernel(u_ref, win_ref, are_ref, aim_ref, cre_ref, cim_ref, dskip_ref,
               wglu_ref, bglu_ref, o_ref, vx_ref, hre_ref, him_ref, io_ref, utm_ref, *,
               batch, lt, n_state, scan_tiles):
    ti = pl.program_id(0)

    @pl.when(ti == 0)
    def _():
        hre_ref[...] = jnp.zeros_like(hre_ref)
        him_ref[...] = jnp.zeros_like(him_ref)

    width = u_ref.shape[-1]
    n_slab = width // LANES
    n_tiles = n_state // LANES
    slab_tiles = n_tiles // n_slab
    for b in range(batch):
        for c in range(n_slab):
            io_ref[c, b * lt:(b + 1) * lt, :] = u_ref[b, :, c * LANES:(c + 1) * LANES]

    def to_time_major(t, _):
        dst = pl.ds(pl.multiple_of(t * batch, batch), batch)
        for c in range(n_slab):
            utm_ref[dst, c * LANES:(c + 1) * LANES] = io_ref[c, pl.ds(t, batch, stride=lt), :]
        return 0

    lax.fori_loop(0, lt, to_time_major, 0, unroll=4)
    u2 = utm_ref[...]
    ub = u2.astype(BF16)
    for j in range(n_slab):
        vj = _dot(ub[:, j * LANES:(j + 1) * LANES], win_ref[j])
        for i in range(slab_tiles):
            vx_ref[j * slab_tiles + i] = vj[:, i * LANES:(i + 1) * LANES]
            vx_ref[n_tiles + j * slab_tiles + i] = vj[:, (slab_tiles + i) * LANES:(slab_tiles + i + 1) * LANES]

    for c0 in range(0, n_tiles, scan_tiles):
        tiles = range(c0, c0 + scan_tiles)
        ar = [jnp.broadcast_to(are_ref[:, c * LANES:(c + 1) * LANES], (batch, LANES)) for c in tiles]
        ai = [jnp.broadcast_to(aim_ref[:, c * LANES:(c + 1) * LANES], (batch, LANES)) for c in tiles]

        def step(t, carry, tiles=tiles, ar=ar, ai=ai):
            rows = pl.ds(pl.multiple_of(t * batch, batch), batch)
            out = []
            for n, c in enumerate(tiles):
                hr, hi = carry[2 * n], carry[2 * n + 1]
                nr = ar[n] * hr - ai[n] * hi + vx_ref[c, rows, :]
                ni = ar[n] * hi + ai[n] * hr + vx_ref[n_tiles + c, rows, :]
                vx_ref[c, rows, :] = nr
                vx_ref[n_tiles + c, rows, :] = ni
                out += [nr, ni]
            return tuple(out)

        init = []
        for c in tiles:
            init += [hre_ref[c], him_ref[c]]
        fin = lax.fori_loop(0, lt, step, tuple(init), unroll=4)
        for n, c in enumerate(tiles):
            hre_ref[c] = fin[2 * n]
            him_ref[c] = fin[2 * n + 1]

    ys = []
    for j in range(n_slab):
        xr = jnp.concatenate([vx_ref[j * slab_tiles + i].astype(BF16) for i in range(slab_tiles)], axis=1)
        xi = jnp.concatenate([vx_ref[n_tiles + j * slab_tiles + i].astype(BF16) for i in range(slab_tiles)], axis=1)
        ys.append(_dot(xr, cre_ref[j]) + _dot(xi, cim_ref[j]))
    y = jnp.concatenate(ys, axis=1) + dskip_ref[...] * u2
    y = jax.nn.gelu(y)
    z = _dot(y.astype(BF16), wglu_ref[...]) + bglu_ref[...]
    out = y * jax.nn.sigmoid(z)
    for c in range(n_slab):
        io_ref[c] = out[:, c * LANES:(c + 1) * LANES]
    for b in range(batch):
        for c in range(n_slab):
            o_ref[b, :, c * LANES:(c + 1) * LANES] = io_ref[c, pl.ds(b, lt, stride=batch), :].astype(o_ref.dtype)


def _router_gates(logits_t, rbias):
    n_exp, tm = logits_t.shape
    per_group = n_exp // N_GROUPS
    scores = jax.nn.sigmoid(logits_t)
    sel = scores + rbias
    neg_inf = -jnp.inf
    sub_iota = lax.broadcasted_iota(jnp.int32, (per_group, tm), 0).astype(F32)
    group_score = []
    for g in range(N_GROUPS):
        sg = sel[g * per_group:(g + 1) * per_group, :]
        m1 = jnp.max(sg, axis=0, keepdims=True)
        first = jnp.min(jnp.where(sg == m1, sub_iota, float(per_group)), axis=0, keepdims=True)
        m2 = jnp.max(jnp.where(sub_iota == first, neg_inf, sg), axis=0, keepdims=True)
        group_score.append(m1 + m2)
    masked = []
    for g in range(N_GROUPS):
        rank = jnp.zeros((1, tm), F32)
        for g2 in range(N_GROUPS):
            if g2 == g:
                continue
            ahead = (group_score[g2] >= group_score[g]) if g2 < g else (group_score[g2] > group_score[g])
            rank = rank + jnp.where(ahead, 1.0, 0.0)
        keep = rank < float(TOP_GROUPS)
        masked.append(jnp.where(keep, sel[g * per_group:(g + 1) * per_group, :], neg_inf))
    cur = jnp.concatenate(masked, axis=0)
    iota = lax.broadcasted_iota(jnp.int32, (n_exp, tm), 0).astype(F32)
    chosen = jnp.zeros((n_exp, tm), F32)
    picks, weights = [], []
    for _ in range(TOP_K):
        m = jnp.max(cur, axis=0, keepdims=True)
        idx = jnp.min(jnp.where(cur == m, iota, float(n_exp)), axis=0, keepdims=True)
        pick = iota == idx
        chosen = jnp.where(pick, 1.0, chosen)
        cur = jnp.where(pick, neg_inf, cur)
        picks.append(idx)
        weights.append(jnp.sum(jnp.where(pick, scores, 0.0), axis=0, keepdims=True))
    idx_k = jnp.concatenate(picks, axis=0)
    w_k = jnp.concatenate(weights, axis=0)
    gate_k = w_k / jnp.sum(w_k, axis=0, keepdims=True) * ROUTED_SCALE
    return idx_k, gate_k, chosen


def _pack_bf16_pairs(x):
    n = x.shape[1] // 2
    hi = lax.bitcast_convert_type(x[:, :n].astype(BF16).astype(F32), jnp.int32)
    lo = lax.bitcast_convert_type(x[:, n:].astype(BF16).astype(F32), jnp.int32)
    return hi | lax.shift_right_logical(lo, 16)


def _unpack_bf16_pairs(p):
    hi = lax.bitcast_convert_type(p & jnp.int32(-65536), F32).astype(BF16)
    lo = lax.bitcast_convert_type(lax.shift_left(p, 16), F32).astype(BF16)
    return hi, lo


def _mix_kernel(att_ref, ssm_ref, xn_ref, ag_ref, sg_ref, woa_ref, wos_ref,
                g_ref, b_ref, wrh_ref, wrl_ref, rb_ref,
                x1_ref, x1p_ref, idx_ref, gate_ref, rank_ref, cnt_ref, carry_ref, before_ref):
    @pl.when(pl.program_id(0) == 0)
    def _():
        carry_ref[...] = jnp.zeros_like(carry_ref)
        tm = before_ref.shape[0]
        before = (lax.broadcasted_iota(jnp.int32, (tm, tm), 0)
                  < lax.broadcasted_iota(jnp.int32, (tm, tm), 1))
        before_ref[...] = jnp.where(before, 1.0, 0.0).astype(BF16)

    an = _rms_norm(att_ref[...].astype(F32), ag_ref[...]).astype(BF16)
    sn = _rms_norm(ssm_ref[...].astype(F32), sg_ref[...]).astype(BF16)
    mix = _dot(an, woa_ref[...]) + _dot(sn, wos_ref[...])
    x1 = _layer_norm(ALPHA * xn_ref[...] + mix, g_ref[...], b_ref[...])
    x1_ref[...] = x1
    x1p_ref[...] = _pack_bf16_pairs(x1)
    x_hi = x1.astype(BF16)
    x_lo = (x1 - x_hi.astype(F32)).astype(BF16)
    dg = functools.partial(lax.dot_general, dimension_numbers=NT_DIMS, preferred_element_type=F32)
    logits_t = dg(wrh_ref[...], x_hi) + dg(wrl_ref[...], x_hi) + dg(wrh_ref[...], x_lo)
    idx_k, gate_k, chosen = _router_gates(logits_t, rb_ref[...])
    idx_ref[...] = idx_k.astype(jnp.int32)
    gate_ref[...] = gate_k

    n_exp, tm = chosen.shape
    excl = _dot(chosen.astype(BF16), before_ref[...])
    rank_full = carry_ref[...] + excl
    iota = lax.broadcasted_iota(jnp.int32, (n_exp, tm), 0).astype(F32)
    ranks = [jnp.sum(jnp.where(iota == idx_k[k:k + 1, :], rank_full, 0.0), axis=0, keepdims=True)
             for k in range(TOP_K)]
    rank_ref[...] = jnp.concatenate(ranks, axis=0).astype(jnp.int32)
    total = carry_ref[...] + jnp.sum(chosen, axis=1, keepdims=True)
    carry_ref[...] = total
    cnt_ref[...] = jnp.broadcast_to(total, cnt_ref.shape)


def _positions_kernel(off_ref, idx_ref, rank_ref, pos_ref):
    idx = idx_ref[...]
    base = jnp.zeros(idx.shape, jnp.int32)
    for e in range(N_EXPERTS):
        base = jnp.where(idx == e, off_ref[e], base)
    pos_ref[...] = rank_ref[...] + base


def _swiglu(x_hi, x_lo, wg_ref, wu_ref, wd_ref):
    half = x_hi.shape[1]
    hg = _dot(x_hi, wg_ref[:half, :]) + _dot(x_lo, wg_ref[half:, :])
    hu = _dot(x_hi, wu_ref[:half, :]) + _dot(x_lo, wu_ref[half:, :])
    h = jax.nn.silu(hg) * hu
    return _dot(h.astype(BF16), wd_ref[...])


def _experts_kernel(te_ref, valid_ref, xs_ref, wg_ref, wu_ref, wd_ref, ys_ref,
                    wgb_ref, wub_ref, wdb_ref):
    i = pl.program_id(0)
    valid = valid_ref[i]

    @pl.when((i == 0) | (te_ref[i] != te_ref[jnp.maximum(i - 1, 0)]))
    def _():
        wgb_ref[...] = wg_ref[0].astype(BF16)
        wub_ref[...] = wu_ref[0].astype(BF16)
        wdb_ref[...] = wd_ref[0].astype(BF16)

    @pl.when(valid > 0)
    def _():
        rows = lax.broadcasted_iota(jnp.int32, xs_ref.shape, 0)
        x_hi, x_lo = _unpack_bf16_pairs(jnp.where(rows < valid, xs_ref[...], 0))
        ys_ref[...] = _pack_bf16_pairs(_swiglu(x_hi, x_lo, wgb_ref, wub_ref, wdb_ref))

    @pl.when(valid == 0)
    def _():
        ys_ref[...] = jnp.zeros_like(ys_ref)


def _combine_kernel(yg_ref, gate_ref, x1_ref, wsg_ref, wsu_ref, wsd_ref, g_ref, b_ref, o_ref):
    x1 = x1_ref[...]
    half = x1.shape[1] // 2
    acc = _swiglu(x1[:, :half].astype(BF16), x1[:, half:].astype(BF16), wsg_ref, wsu_ref, wsd_ref)
    gates = gate_ref[...]
    acc_hi, acc_lo = acc[:, :half], acc[:, half:]
    for k in range(TOP_K):
        y_hi, y_lo = _unpack_bf16_pairs(yg_ref[k])
        gk = gates[:, k:k + 1]
        acc_hi = acc_hi + gk * y_hi.astype(F32)
        acc_lo = acc_lo + gk * y_lo.astype(F32)
    ffn = jnp.concatenate([acc_hi, acc_lo], axis=1)
    o_ref[...] = _layer_norm(ALPHA * x1 + ffn, g_ref[...], b_ref[...])


def _combine_into_kernel(prev_ref, *refs):
    del prev_ref
    _combine_kernel(*refs)


SC_CORES = 2
SC_SUBCORES = 16
SC_WORKERS = SC_CORES * SC_SUBCORES
SC_CHUNK = 64


def _sc_mesh():
    return plsc.VectorSubcoreMesh(core_axis_name="c", subcore_axis_name="s")


def _sc_gather_rows(table, idx):
    n = idx.shape[0]
    d = table.shape[1]
    per_w = n // SC_WORKERS
    n_ch = per_w // SC_CHUNK

    @functools.partial(
        pl.kernel, mesh=_sc_mesh(),
        out_type=jax.ShapeDtypeStruct((n, d), table.dtype),
        scratch_types=[pltpu.VMEM((n_ch, SC_CHUNK), jnp.int32),
                       pltpu.VMEM((2, SC_CHUNK, d), table.dtype),
                       pltpu.SemaphoreType.DMA((2,)),
                       pltpu.SemaphoreType.DMA((2,))],
    )
    def k(table_hbm, idx_hbm, out_hbm, idx_v, buf, gsem, osem):
        wid = lax.axis_index("s") * SC_CORES + lax.axis_index("c")
        base = wid * per_w
        pltpu.sync_copy(idx_hbm.at[wid], idx_v)

        def gather(c, b):
            return pltpu.make_async_copy(table_hbm.at[idx_v.at[c]], buf.at[b], gsem.at[b])

        def put(c, b):
            return pltpu.make_async_copy(buf.at[b], out_hbm.at[pl.ds(base + c * SC_CHUNK, SC_CHUNK)],
                                         osem.at[b])

        gather(0, 0).start()

        @pl.loop(0, n_ch, step=2)
        def _(c):
            for b in range(2):
                cc = c + b
                gather(cc, b).wait()

                @pl.when(cc + 1 < n_ch)
                def _():
                    @pl.when(cc >= 1)
                    def _():
                        put(cc - 1, 1 - b).wait()
                    gather(cc + 1, 1 - b).start()

                put(cc, b).start()

        put(n_ch - 2, 0).wait()
        put(n_ch - 1, 1).wait()

    return k(table, idx.reshape(SC_WORKERS, n_ch, SC_CHUNK))


def _sc_scatter_rows(x, pos, n_out):
    t, d = x.shape
    kk = pos.shape[0]
    per_w = t // SC_WORKERS
    n_ch = per_w // SC_CHUNK
    pos_w = pos.reshape(kk, SC_WORKERS, n_ch, SC_CHUNK).transpose(1, 2, 0, 3)
    pos_w = pos_w.reshape(SC_WORKERS, n_ch * kk, SC_CHUNK)

    @functools.partial(
        pl.kernel, mesh=_sc_mesh(),
        out_type=jax.ShapeDtypeStruct((n_out, d), x.dtype),
        scratch_types=[pltpu.VMEM((n_ch * kk, SC_CHUNK), jnp.int32),
                       pltpu.VMEM((2, SC_CHUNK, d), x.dtype),
                       pltpu.SemaphoreType.DMA((2,)),
                       pltpu.SemaphoreType.DMA((2,))],
    )
    def k(x_hbm, pos_hbm, out_hbm, idx_v, buf, isem, osem):
        wid = lax.axis_index("s") * SC_CORES + lax.axis_index("c")
        base = wid * per_w
        pltpu.sync_copy(pos_hbm.at[wid], idx_v)

        def get(c, b):
            return pltpu.make_async_copy(x_hbm.at[pl.ds(base + c * SC_CHUNK, SC_CHUNK)], buf.at[b],
                                         isem.at[b])

        def put(c, j, b):
            return pltpu.make_async_copy(buf.at[b], out_hbm.at[idx_v.at[c * kk + j]], osem.at[b])

        get(0, 0).start()

        @pl.loop(0, n_ch, step=2)
        def _(c):
            for b in range(2):
                cc = c + b
                get(cc, b).wait()

                @pl.when(cc + 1 < n_ch)
                def _():
                    @pl.when(cc >= 1)
                    def _():
                        for j in range(kk):
                            put(cc - 1, j, 1 - b).wait()
                    get(cc + 1, 1 - b).start()

                for j in range(kk):
                    put(cc, j, b).start()

        for j in range(kk):
            put(n_ch - 2, j, 0).wait()
        for j in range(kk):
            put(n_ch - 1, j, 1).wait()

    return k(x, pos_w)


def _row(v):
    return v.reshape(1, -1).astype(F32)


def _const_spec(shape):
    nd = len(shape)
    return pl.BlockSpec(shape, lambda *_: (0,) * nd)


def _pad_heads(w, width):
    r, h, _ = w.shape
    return jnp.pad(w, ((0, 0), (0, 0), (0, HEAD_PAD - width))).reshape(r, h * HEAD_PAD)


def _half_rotate(w):
    half = QK_ROPE // 2
    return jnp.concatenate([-w[..., half:], w[..., :half]], axis=-1)


def kernel(x, positions, ln_in_g, ln_in_b, w_in, q_norm_g, kv_norm_g, w_uq, w_ukv, lambda_re, lambda_im, log_step, b_re, b_im, c_re, c_im, d_skip, w_glu, b_glu, attn_out_g, ssm_out_g, w_o, ln1_g, ln1_b, w_router, router_bias, w_gate, w_up, w_down, ws_gate, ws_up, ws_down, ln2_g, ln2_b):
    B, S, D = x.shape
    T = B * S
    l = 0
    ssm_width = w_glu.shape[-1]
    n_groups = ssm_width // SSM_GROUP
    n_state = n_groups * SSM_STATE
    mla_width = MLA_HEADS * V_DIM
    qk_pad = MLA_HEADS * HEAD_PAD
    cparams = functools.partial(pltpu.CompilerParams, vmem_limit_bytes=VMEM_LIMIT)

    s1, s2, s3 = Q_RANK, Q_RANK + KV_RANK, Q_RANK + KV_RANK + QK_ROPE
    wi = w_in[l]
    w_kr = wi[:, s2:s3]
    pad_rope = lambda w: jnp.pad(w, ((0, 0), (QK_NOPE, HEAD_PAD - QK_NOPE - QK_ROPE)))
    w1 = jnp.concatenate([wi[:, :s2], wi[:, s3:], pad_rope(w_kr), pad_rope(_half_rotate(w_kr))],
                         axis=1).astype(BF16)
    wq = w_uq[l]
    zeros_nope = jnp.zeros(wq.shape[:2] + (QK_NOPE,), wq.dtype)
    wq_main = _pad_heads(wq, QK_NOPE + QK_ROPE).astype(BF16)
    wq_rot = _pad_heads(jnp.concatenate([zeros_nope, _half_rotate(wq[..., QK_NOPE:])], axis=-1),
                        QK_NOPE + QK_ROPE).astype(BF16)
    wkv = w_ukv[l]
    wk = _pad_heads(wkv[..., :QK_NOPE], QK_NOPE).astype(BF16)
    wv = _pad_heads(wkv[..., QK_NOPE:], V_DIM).astype(BF16)
    half = QK_ROPE // 2
    inv_freq = ROPE_THETA ** (-jnp.arange(half, dtype=F32) / half)
    freq = jnp.pad(jnp.concatenate([inv_freq, inv_freq]),
                   (QK_NOPE, HEAD_PAD - QK_NOPE - QK_ROPE)).reshape(1, HEAD_PAD)
    pos_f = positions.astype(F32).reshape(T, 1)

    tm = min(512, T)
    w1_cols = w1.shape[1]
    tok = lambda width: pl.BlockSpec((tm, width), lambda i: (i, 0))
    xn, q, k, v, u = pl.pallas_call(
        functools.partial(_inproj_kernel, ssm_width=ssm_width),
        grid=(T // tm,),
        in_specs=[tok(D), tok(1), _const_spec((1, D)), _const_spec((1, D)),
                  _const_spec((D, w1_cols)), _const_spec((1, Q_RANK)), _const_spec((1, KV_RANK)),
                  _const_spec((Q_RANK, qk_pad)), _const_spec((Q_RANK, qk_pad)),
                  _const_spec((KV_RANK, qk_pad)), _const_spec((KV_RANK, qk_pad)),
                  _const_spec((1, HEAD_PAD))],
        out_specs=[tok(D), tok(qk_pad), tok(qk_pad), tok(qk_pad), tok(ssm_width)],
        out_shape=[jax.ShapeDtypeStruct((T, D), F32), jax.ShapeDtypeStruct((T, qk_pad), BF16),
                   jax.ShapeDtypeStruct((T, qk_pad), BF16), jax.ShapeDtypeStruct((T, qk_pad), BF16),
                   jax.ShapeDtypeStruct((T, ssm_width), F32)],
        compiler_params=cparams(dimension_semantics=("parallel",)),
        name="inproj",
    )(x.reshape(T, D), pos_f, _row(ln_in_g), _row(ln_in_b), w1, _row(q_norm_g[l]), _row(kv_norm_g[l]),
      wq_main, wq_rot, wk, wv, freq)

    tq = min(256, S)
    nq = S // tq
    att = pl.pallas_call(
        functools.partial(_attn_kernel, tq=tq, heads=MLA_HEADS),
        grid=(B, nq),
        in_specs=[pl.BlockSpec((tq, qk_pad), lambda b, i: (b * nq + i, 0)),
                  pl.BlockSpec((S, qk_pad), lambda b, i: (b, 0)),
                  pl.BlockSpec((S, qk_pad), lambda b, i: (b, 0))],
        out_specs=pl.BlockSpec((tq, mla_width), lambda b, i: (b * nq + i, 0)),
        out_shape=jax.ShapeDtypeStruct((T, mla_width), BF16),
        scratch_shapes=[pltpu.VMEM((MLA_HEADS, nq, tq, tq), F32),
                        pltpu.VMEM((MLA_HEADS, tq, LANES), F32),
                        pltpu.VMEM((MLA_HEADS, tq, HEAD_PAD), F32)],
        compiler_params=cparams(dimension_semantics=("parallel", "arbitrary")),
        name="attention",
    )(q, k, v)

    lam = lax.complex(jnp.minimum(lambda_re[l].astype(F32), -1e-4), lambda_im[l].astype(F32))
    step = jnp.exp(log_step[l].astype(F32))[:, None]
    lam_bar = jnp.exp(lam * step)
    b_bar = ((lam_bar - 1.0) / lam)[..., None] * lax.complex(b_re[l].astype(F32), b_im[l].astype(F32))
    n_slab = ssm_width // LANES
    g_per_slab = n_groups // n_slab
    eye = jnp.eye(n_groups, dtype=F32)

    def expand_in(bpart):
        return jnp.einsum('gpc,gh->gchp', bpart, eye).reshape(ssm_width, n_state)

    def expand_out(cpart):
        return jnp.einsum('gcp,gh->gphc', cpart, eye).reshape(n_state, ssm_width)

    slab = n_state // n_slab
    win_re, win_im = expand_in(jnp.real(b_bar)), expand_in(jnp.imag(b_bar))
    win = jnp.stack([jnp.concatenate([win_re[j * LANES:(j + 1) * LANES, j * slab:(j + 1) * slab],
                                      win_im[j * LANES:(j + 1) * LANES, j * slab:(j + 1) * slab]], axis=1)
                     for j in range(n_slab)]).astype(BF16)
    wc_re, wc_im = expand_out(c_re[l].astype(F32)), expand_out(-c_im[l].astype(F32))
    cre = jnp.stack([wc_re[j * slab:(j + 1) * slab, j * LANES:(j + 1) * LANES] for j in range(n_slab)]).astype(BF16)
    cim = jnp.stack([wc_im[j * slab:(j + 1) * slab, j * LANES:(j + 1) * LANES] for j in range(n_slab)]).astype(BF16)
    a_re = jnp.real(lam_bar).reshape(1, n_state)
    a_im = jnp.imag(lam_bar).reshape(1, n_state)

    lt = min(64, S)
    ssm = pl.pallas_call(
        functools.partial(_s5_kernel, batch=B, lt=lt, n_state=n_state, scan_tiles=4),
        grid=(S // lt,),
        in_specs=[pl.BlockSpec((B, lt, ssm_width), lambda t: (0, t, 0)),
                  _const_spec(win.shape), _const_spec((1, n_state)), _const_spec((1, n_state)),
                  _const_spec(cre.shape), _const_spec(cim.shape), _const_spec((1, ssm_width)),
                  _const_spec((ssm_width, ssm_width)), _const_spec((1, ssm_width))],
        out_specs=pl.BlockSpec((B, lt, ssm_width), lambda t: (0, t, 0)),
        out_shape=jax.ShapeDtypeStruct((B, S, ssm_width), BF16),
        scratch_shapes=[pltpu.VMEM((2 * n_state // LANES, B * lt, LANES), F32),
                        pltpu.VMEM((n_state // LANES, B, LANES), F32),
                        pltpu.VMEM((n_state // LANES, B, LANES), F32),
                        pltpu.VMEM((ssm_width // LANES, B * lt, LANES), F32),
                        pltpu.VMEM((B * lt, ssm_width), F32)],
        compiler_params=cparams(dimension_semantics=("arbitrary",)),
        name="s5",
    )(u.reshape(B, S, ssm_width), win, a_re, a_im, cre, cim, _row(d_skip[l]),
      w_glu[l].astype(BF16), _row(b_glu[l]))

    wo = w_o[l].astype(BF16)
    wr_t = w_router[l].T.astype(F32)
    wr_hi = wr_t.astype(BF16)
    wr_lo = (wr_t - wr_hi.astype(F32)).astype(BF16)
    half = D // 2
    n_slabs = MOE_SLABS
    ts = T // n_slabs
    nt = ts // tm
    ssm2 = ssm.reshape(T, ssm_width)
    rbias = router_bias[l].astype(F32).reshape(N_EXPERTS, 1)
    kt = lambda dt: jax.ShapeDtypeStruct((TOP_K, ts), dt)
    k_spec = pl.BlockSpec((TOP_K, tm), lambda i: (0, i))

    def route(s):
        tok_s = lambda width: pl.BlockSpec((tm, width), lambda i: (i + s * nt, 0))
        return pl.pallas_call(
            _mix_kernel,
            grid=(nt,),
            in_specs=[tok_s(mla_width), tok_s(ssm_width), tok_s(D), _const_spec((1, mla_width)),
                      _const_spec((1, ssm_width)), _const_spec((mla_width, D)),
                      _const_spec((ssm_width, D)), _const_spec((1, D)), _const_spec((1, D)),
                      _const_spec((N_EXPERTS, D)), _const_spec((N_EXPERTS, D)),
                      _const_spec((N_EXPERTS, 1))],
            out_specs=[tok(D), tok(half), k_spec, k_spec, k_spec, _const_spec((N_EXPERTS, LANES))],
            out_shape=[jax.ShapeDtypeStruct((ts, D), F32), jax.ShapeDtypeStruct((ts, half), jnp.int32),
                       kt(jnp.int32), kt(F32), kt(jnp.int32),
                       jax.ShapeDtypeStruct((N_EXPERTS, LANES), F32)],
            scratch_shapes=[pltpu.VMEM((N_EXPERTS, 1), F32), pltpu.VMEM((tm, tm), BF16)],
            compiler_params=cparams(dimension_semantics=("arbitrary",)),
            name="mix_router",
        )(att, ssm2, xn, _row(attn_out_g[l]), _row(ssm_out_g[l]), wo[:mla_width], wo[mla_width:],
          _row(ln1_g[l]), _row(ln1_b[l]), wr_hi, wr_lo, rbias)

    tr = EXPERT_ROW_TILE
    n_tiles = (ts * TOP_K) // tr + N_EXPERTS
    n_rows = n_tiles * tr
    tp = min(2048, ts)

    def dispatch(x1p, idx_k, rank_k, counts):
        cnt = counts[:, 0].astype(jnp.int32)
        tiles_e = (cnt + tr - 1) // tr
        tile_end = jnp.cumsum(tiles_e)
        tile_start = tile_end - tiles_e
        tile_ids = jnp.arange(n_tiles, dtype=jnp.int32)
        tile_expert = jnp.sum((tile_end[None, :] <= tile_ids[:, None]).astype(jnp.int32), axis=1)
        tile_expert = jnp.minimum(tile_expert, N_EXPERTS - 1)
        owner = (tile_start[None, :] <= tile_ids[:, None]) & (tile_ids[:, None] < tile_end[None, :])
        left = jnp.sum(jnp.where(owner, cnt[None, :] - (tile_ids[:, None] - tile_start[None, :]) * tr, 0),
                       axis=1)
        tile_valid = jnp.clip(left, 0, tr).astype(jnp.int32)
        pos = pl.pallas_call(
            _positions_kernel,
            grid_spec=pltpu.PrefetchScalarGridSpec(
                num_scalar_prefetch=1, grid=(ts // tp,),
                in_specs=[pl.BlockSpec((TOP_K, tp), lambda i, off: (0, i)),
                          pl.BlockSpec((TOP_K, tp), lambda i, off: (0, i))],
                out_specs=pl.BlockSpec((TOP_K, tp), lambda i, off: (0, i))),
            out_shape=kt(jnp.int32),
            name="positions",
        )((tile_start * tr).astype(jnp.int32), idx_k, rank_k)
        return _sc_scatter_rows(x1p, pos, n_rows), pos, tile_expert, tile_valid

    wg, wu, wd = w_gate[l], w_up[l], w_down[l]
    ff = wg.shape[-1]

    def experts(xs, tile_expert, tile_valid):
        return pl.pallas_call(
            _experts_kernel,
            grid_spec=pltpu.PrefetchScalarGridSpec(
                num_scalar_prefetch=2, grid=(n_tiles,),
                in_specs=[pl.BlockSpec((tr, half), lambda i, te, tv: (i, 0)),
                          pl.BlockSpec((1, D, ff), lambda i, te, tv: (te[i], 0, 0)),
                          pl.BlockSpec((1, D, ff), lambda i, te, tv: (te[i], 0, 0)),
                          pl.BlockSpec((1, ff, D), lambda i, te, tv: (te[i], 0, 0))],
                out_specs=pl.BlockSpec((tr, half), lambda i, te, tv: (i, 0)),
                scratch_shapes=[pltpu.VMEM((D, ff), BF16), pltpu.VMEM((D, ff), BF16),
                                pltpu.VMEM((ff, D), BF16)]),
            out_shape=jax.ShapeDtypeStruct((n_rows, half), jnp.int32),
            compiler_params=cparams(dimension_semantics=("arbitrary",)),
            name="experts",
        )(tile_expert, tile_valid, xs, wg, wu, wd)

    shared = (ws_gate[l].astype(BF16), ws_up[l].astype(BF16), ws_down[l].astype(BF16))

    def combine(s, out_so_far, yg, gate_k, x1):
        specs = [pl.BlockSpec((TOP_K, tm, half), lambda i: (0, i, 0)),
                 pl.BlockSpec((tm, TOP_K), lambda i: (i, 0)), tok(D),
                 _const_spec((D, ff)), _const_spec((D, ff)), _const_spec((ff, D)),
                 _const_spec((1, D)), _const_spec((1, D))]
        args = (yg.reshape(TOP_K, ts, half), gate_k.T, x1, *shared, _row(ln2_g[l]), _row(ln2_b[l]))
        body, aliases = _combine_kernel, {}
        if out_so_far is not None:
            specs = [pl.BlockSpec(memory_space=pl.ANY)] + specs
            args = (out_so_far,) + args
            body, aliases = _combine_into_kernel, {0: 0}
        return pl.pallas_call(
            body,
            grid=(nt,),
            in_specs=specs,
            out_specs=pl.BlockSpec((tm, D), lambda i: (i + s * nt, 0)),
            out_shape=jax.ShapeDtypeStruct((T, D), F32),
            input_output_aliases=aliases,
            compiler_params=cparams(dimension_semantics=("parallel",)),
            name="combine",
        )(*args)

    routed, moved = [], []
    for s in range(n_slabs):
        x1, x1p, idx_k, gate_k, rank_k, counts = route(s)
        routed.append((x1, gate_k))
        moved.append(dispatch(x1p, idx_k, rank_k, counts))
    gathered = []
    for xs, pos, tile_expert, tile_valid in moved:
        gathered.append(_sc_gather_rows(experts(xs, tile_expert, tile_valid), pos.reshape(TOP_K * ts)))
    out = None
    for s in range(n_slabs):
        out = combine(s, out, gathered[s], routed[s][1], routed[s][0])
    return out.reshape(B, S, D)
```

```python
import functools

import jax
import jax.numpy as jnp
from jax import lax
from jax.experimental import pallas as pl
from jax.experimental.pallas import tpu as pltpu
from jax.experimental.pallas import tpu_sc as plsc

CHUNK = 64
MLA_HEADS = 8
QK_NOPE = 64
QK_ROPE = 32
V_DIM = 64
Q_RANK = 256
KV_RANK = 128
ROPE_THETA = 10000.0
SSM_GROUP = 16
SSM_STATE = 64
N_EXPERTS = 64
TOP_K = 8
N_GROUPS = 8
TOP_GROUPS = 4
ROUTED_SCALE = 2.5
DEPTH = 1
ALPHA = (2.0 * DEPTH) ** 0.25
EPS = 1e-5
LOG2_E = 1.4426950408889634

LANES = 128
HEAD_PAD = LANES
VMEM_LIMIT = 56 * 1024 * 1024
EXPERT_ROW_TILE = 1024
MOE_SLABS = 2

BF16 = jnp.bfloat16
F32 = jnp.float32
NT_DIMS = (((1,), (1,)), ((), ()))


def _dot(a, b):
    return jnp.dot(a, b, preferred_element_type=F32)


def _layer_norm(x, g, b):
    mu = jnp.mean(x, axis=-1, keepdims=True)
    xc = x - mu
    var = jnp.mean(xc * xc, axis=-1, keepdims=True)
    return xc * lax.rsqrt(var + EPS) * g + b


def _rms_norm(x, g):
    return x * lax.rsqrt(jnp.mean(x * x, axis=-1, keepdims=True) + EPS) * g


def _inproj_kernel(x_ref, pos_ref, lng_ref, lnb_ref, w1_ref, qg_ref, kvg_ref,
                   wq_ref, wqr_ref, wk_ref, wv_ref, freq_ref,
                   xn_ref, q_ref, k_ref, v_ref, u_ref, *, ssm_width):
    xn = _layer_norm(x_ref[...], lng_ref[...], lnb_ref[...])
    xn_ref[...] = xn
    h = _dot(xn.astype(BF16), w1_ref[...])
    o1 = Q_RANK
    o2 = o1 + KV_RANK
    o3 = o2 + ssm_width
    o4 = o3 + HEAD_PAD
    cq = h[:, :o1]
    ckv = h[:, o1:o2]
    u_ref[...] = h[:, o2:o3]
    kr_raw = h[:, o3:o4]
    kr_rot = h[:, o4:o4 + HEAD_PAD]
    cqn = _rms_norm(cq, qg_ref[...]).astype(BF16)
    ckvn = _rms_norm(ckv, kvg_ref[...]).astype(BF16)

    ang = pos_ref[...] * freq_ref[...]
    c = jnp.cos(ang)
    s = jnp.sin(ang)
    lane = lax.broadcasted_iota(jnp.int32, (1, HEAD_PAD), 1)
    scale = (QK_NOPE + QK_ROPE) ** -0.5 * LOG2_E
    is_rope = (lane >= QK_NOPE) & (lane < QK_NOPE + QK_ROPE)
    cos1 = jnp.where(lane < QK_NOPE, 1.0, jnp.where(is_rope, c, 0.0)) * scale
    sin1 = jnp.where(is_rope, s, 0.0) * scale
    cos_t = jnp.concatenate([cos1] * MLA_HEADS, axis=1)
    sin_t = jnp.concatenate([sin1] * MLA_HEADS, axis=1)
    q = _dot(cqn, wq_ref[...]) * cos_t + _dot(cqn, wqr_ref[...]) * sin_t
    q_ref[...] = q.astype(BF16)

    kr = kr_raw * c + kr_rot * s
    k = _dot(ckvn, wk_ref[...]) + jnp.concatenate([kr] * MLA_HEADS, axis=1)
    k_ref[...] = k.astype(BF16)
    ones_col = jnp.concatenate([jnp.where(lane == V_DIM, 1.0, 0.0)] * MLA_HEADS, axis=1)
    v_ref[...] = (_dot(ckvn, wv_ref[...]) + ones_col).astype(BF16)


def _attn_kernel(q_ref, k_ref, v_ref, o_ref, s_ref, mx_ref, acc_ref, *, tq, heads):
    qi = pl.program_id(1)
    row_chunk = lax.broadcasted_iota(jnp.int32, (tq, tq), 0) // CHUNK
    col_chunk = lax.broadcasted_iota(jnp.int32, (tq, tq), 1) // CHUNK
    diag_mask = row_chunk >= col_chunk
    head_cols = [slice(h * HEAD_PAD, (h + 1) * HEAD_PAD) for h in range(heads)]
    mx_ref[...] = jnp.full(mx_ref.shape, -jnp.inf, F32)
    acc_ref[...] = jnp.zeros(acc_ref.shape, F32)

    def scores(j, nblk, masked):
        start = pl.multiple_of(j * tq, tq)
        for h in range(heads):
            s = lax.dot_general(q_ref[:, head_cols[h]], k_ref[pl.ds(start, nblk * tq), head_cols[h]],
                                NT_DIMS, preferred_element_type=F32)
            if masked:
                s = jnp.where(diag_mask, s, -jnp.inf)
            for b in range(nblk):
                s_ref[h, j + b] = s[:, b * tq:(b + 1) * tq]
            lane_max = s[:, :LANES]
            for c in range(1, nblk * tq // LANES):
                lane_max = jnp.maximum(lane_max, s[:, c * LANES:(c + 1) * LANES])
            mx_ref[h] = jnp.maximum(mx_ref[h], lane_max)

    def values(j, nblk):
        start = pl.multiple_of(j * tq, tq)
        for h in range(heads):
            s = jnp.concatenate([s_ref[h, j + b] for b in range(nblk)], axis=1)
            p = jnp.exp2(s - row_max[h]).astype(BF16)
            acc_ref[h] += _dot(p, v_ref[pl.ds(start, nblk * tq), head_cols[h]])

    def pairs(fn):
        def step(pair, carry):
            fn(2 * pair, 2)
            return carry
        return step

    lax.fori_loop(0, qi // 2, pairs(lambda j, n: scores(j, n, False)), 0)

    @pl.when(qi % 2 == 1)
    def _():
        scores(qi - 1, 1, False)

    scores(qi, 1, True)
    row_max = [jnp.max(mx_ref[h], axis=-1, keepdims=True) for h in range(heads)]
    lax.fori_loop(0, (qi + 1) // 2, pairs(values), 0)

    @pl.when(qi % 2 == 0)
    def _():
        values(qi, 1)
    outs = [acc_ref[h][:, :V_DIM] / acc_ref[h][:, V_DIM:V_DIM + 1] for h in range(heads)]
    o_ref[...] = jnp.concatenate(outs, axis=1).astype(o_ref.dtype)


def _s5_kernel(u_ref, win_ref, are_ref, aim_ref, cre_ref, cim_ref, dskip_ref,
               wglu_ref, bglu_ref, o_ref, vx_ref, hre_ref, him_ref, io_ref, utm_ref, *,
               batch, lt, n_state):
    ti = pl.program_id(0)

    @pl.when(ti == 0)
    def _():
        hre_ref[...] = jnp.zeros_like(hre_ref)
        him_ref[...] = jnp.zeros_like(him_ref)

    width = u_ref.shape[-1]
    n_slab = width // LANES
    n_tiles = n_state // LANES
    slab_tiles = n_tiles // n_slab
    for b in range(batch):
        for c in range(n_slab):
            io_ref[c, b * lt:(b + 1) * lt, :] = u_ref[b, :, c * LANES:(c + 1) * LANES]

    ys = []
    for j in range(n_slab):
        tiles = range(j * slab_tiles, (j + 1) * slab_tiles)
        for t in range(lt):
            utm_ref[t * batch:(t + 1) * batch, j * LANES:(j + 1) * LANES] = (
                io_ref[j, pl.ds(t, batch, stride=lt), :])
        ub = utm_ref[:, j * LANES:(j + 1) * LANES].astype(BF16)
        vj = _dot(ub, win_ref[j])
        for i, c in enumerate(tiles):
            vx_ref[c] = vj[:, i * LANES:(i + 1) * LANES]
            vx_ref[n_tiles + c] = vj[:, (slab_tiles + i) * LANES:(slab_tiles + i + 1) * LANES]

        ar = [jnp.broadcast_to(are_ref[:, c * LANES:(c + 1) * LANES], (batch, LANES)) for c in tiles]
        ai = [jnp.broadcast_to(aim_ref[:, c * LANES:(c + 1) * LANES], (batch, LANES)) for c in tiles]
        hr = [hre_ref[c] for c in tiles]
        hi = [him_ref[c] for c in tiles]
        for t in range(lt):
            rows = slice(t * batch, (t + 1) * batch)
            for n, c in enumerate(tiles):
                nr = ar[n] * hr[n] - ai[n] * hi[n] + vx_ref[c, rows, :]
                ni = ar[n] * hi[n] + ai[n] * hr[n] + vx_ref[n_tiles + c, rows, :]
                vx_ref[c, rows, :] = nr
                vx_ref[n_tiles + c, rows, :] = ni
                hr[n], hi[n] = nr, ni
        for n, c in enumerate(tiles):
            hre_ref[c] = hr[n]
            him_ref[c] = hi[n]

        xr = jnp.concatenate([vx_ref[c].astype(BF16) for c in tiles], axis=1)
        xi = jnp.concatenate([vx_ref[n_tiles + c].astype(BF16) for c in tiles], axis=1)
        ys.append(_dot(xr, cre_ref[j]) + _dot(xi, cim_ref[j]))
    y = jnp.concatenate(ys, axis=1) + dskip_ref[...] * utm_ref[...]
    y = jax.nn.gelu(y)
    z = _dot(y.astype(BF16), wglu_ref[...]) + bglu_ref[...]
    out = y * jax.nn.sigmoid(z)
    for c in range(n_slab):
        io_ref[c] = out[:, c * LANES:(c + 1) * LANES]
    for b in range(batch):
        for c in range(n_slab):
            o_ref[b, :, c * LANES:(c + 1) * LANES] = io_ref[c, pl.ds(b, lt, stride=batch), :].astype(o_ref.dtype)


def _router_gates(logits_t, rbias):
    n_exp, tm = logits_t.shape
    per_group = n_exp // N_GROUPS
    scores = jax.nn.sigmoid(logits_t)
    sel = scores + rbias
    neg_inf = -jnp.inf
    sub_iota = lax.broadcasted_iota(jnp.int32, (per_group, tm), 0).astype(F32)
    group_score = []
    for g in range(N_GROUPS):
        sg = sel[g * per_group:(g + 1) * per_group, :]
        m1 = jnp.max(sg, axis=0, keepdims=True)
        first = jnp.min(jnp.where(sg == m1, sub_iota, float(per_group)), axis=0, keepdims=True)
        m2 = jnp.max(jnp.where(sub_iota == first, neg_inf, sg), axis=0, keepdims=True)
        group_score.append(m1 + m2)
    masked = []
    for g in range(N_GROUPS):
        rank = jnp.zeros((1, tm), F32)
        for g2 in range(N_GROUPS):
            if g2 == g:
                continue
            ahead = (group_score[g2] >= group_score[g]) if g2 < g else (group_score[g2] > group_score[g])
            rank = rank + jnp.where(ahead, 1.0, 0.0)
        keep = rank < float(TOP_GROUPS)
        masked.append(jnp.where(keep, sel[g * per_group:(g + 1) * per_group, :], neg_inf))
    cur = jnp.concatenate(masked, axis=0)
    iota = lax.broadcasted_iota(jnp.int32, (n_exp, tm), 0).astype(F32)
    chosen = jnp.zeros((n_exp, tm), F32)
    picks, weights = [], []
    for _ in range(TOP_K):
        m = jnp.max(cur, axis=0, keepdims=True)
        idx = jnp.min(jnp.where(cur == m, iota, float(n_exp)), axis=0, keepdims=True)
        pick = iota == idx
        chosen = jnp.where(pick, 1.0, chosen)
        cur = jnp.where(pick, neg_inf, cur)
        picks.append(idx)
        weights.append(jnp.sum(jnp.where(pick, scores, 0.0), axis=0, keepdims=True))
    idx_k = jnp.concatenate(picks, axis=0)
    w_k = jnp.concatenate(weights, axis=0)
    gate_k = w_k / jnp.sum(w_k, axis=0, keepdims=True) * ROUTED_SCALE
    return idx_k, gate_k, chosen


def _pack_bf16_pairs(x):
    n = x.shape[1] // 2
    hi = lax.bitcast_convert_type(x[:, :n].astype(BF16).astype(F32), jnp.int32)
    lo = lax.bitcast_convert_type(x[:, n:].astype(BF16).astype(F32), jnp.int32)
    return hi | lax.shift_right_logical(lo, 16)


def _unpack_bf16_pairs(p):
    hi = lax.bitcast_convert_type(p & jnp.int32(-65536), F32).astype(BF16)
    lo = lax.bitcast_convert_type(lax.shift_left(p, 16), F32).astype(BF16)
    return hi, lo


def _mix_kernel(att_ref, ssm_ref, xn_ref, ag_ref, sg_ref, woa_ref, wos_ref,
                g_ref, b_ref, wrh_ref, wrl_ref, rb_ref,
                x1_ref, x1p_ref, idx_ref, gate_ref, rank_ref, cnt_ref, carry_ref, before_ref):
    @pl.when(pl.program_id(0) == 0)
    def _():
        carry_ref[...] = jnp.zeros_like(carry_ref)
        tm = before_ref.shape[0]
        before = (lax.broadcasted_iota(jnp.int32, (tm, tm), 0)
                  < lax.broadcasted_iota(jnp.int32, (tm, tm), 1))
        before_ref[...] = jnp.where(before, 1.0, 0.0).astype(BF16)

    an = _rms_norm(att_ref[...].astype(F32), ag_ref[...]).astype(BF16)
    sn = _rms_norm(ssm_ref[...].astype(F32), sg_ref[...]).astype(BF16)
    mix = _dot(an, woa_ref[...]) + _dot(sn, wos_ref[...])
    x1 = _layer_norm(ALPHA * xn_ref[...] + mix, g_ref[...], b_ref[...])
    x1_ref[...] = x1
    x1p_ref[...] = _pack_bf16_pairs(x1)
    x_hi = x1.astype(BF16)
    x_lo = (x1 - x_hi.astype(F32)).astype(BF16)
    dg = functools.partial(lax.dot_general, dimension_numbers=NT_DIMS, preferred_element_type=F32)
    logits_t = dg(wrh_ref[...], x_hi) + dg(wrl_ref[...], x_hi) + dg(wrh_ref[...], x_lo)
    idx_k, gate_k, chosen = _router_gates(logits_t, rb_ref[...])
    idx_ref[...] = idx_k.astype(jnp.int32)
    gate_ref[...] = gate_k.T

    n_exp, tm = chosen.shape
    excl = _dot(chosen.astype(BF16), before_ref[...])
    rank_full = carry_ref[...] + excl
    iota = lax.broadcasted_iota(jnp.int32, (n_exp, tm), 0).astype(F32)
    ranks = [jnp.sum(jnp.where(iota == idx_k[k:k + 1, :], rank_full, 0.0), axis=0, keepdims=True)
             for k in range(TOP_K)]
    rank_ref[...] = jnp.concatenate(ranks, axis=0).astype(jnp.int32)
    total = carry_ref[...] + jnp.sum(chosen, axis=1, keepdims=True)
    carry_ref[...] = total
    cnt_ref[...] = jnp.broadcast_to(total, cnt_ref.shape)


def _positions_kernel(off_ref, idx_ref, rank_ref, pos_ref):
    idx = idx_ref[...]
    base = jnp.zeros(idx.shape, jnp.int32)
    for e in range(N_EXPERTS):
        base = jnp.where(idx == e, off_ref[e], base)
    pos_ref[...] = rank_ref[...] + base


def _swiglu(x_hi, x_lo, wg_ref, wu_ref, wd_ref):
    half = x_hi.shape[1]
    hg = _dot(x_hi, wg_ref[:half, :]) + _dot(x_lo, wg_ref[half:, :])
    hu = _dot(x_hi, wu_ref[:half, :]) + _dot(x_lo, wu_ref[half:, :])
    h = jax.nn.silu(hg) * hu
    return _dot(h.astype(BF16), wd_ref[...])


def _experts_kernel(te_ref, valid_ref, xs_ref, wg_ref, wu_ref, wd_ref, ys_ref,
                    wgb_ref, wub_ref, wdb_ref):
    i = pl.program_id(0)
    valid = valid_ref[i]

    @pl.when((i == 0) | (te_ref[i] != te_ref[jnp.maximum(i - 1, 0)]))
    def _():
        wgb_ref[...] = wg_ref[0].astype(BF16)
        wub_ref[...] = wu_ref[0].astype(BF16)
        wdb_ref[...] = wd_ref[0].astype(BF16)

    @pl.when(valid > 0)
    def _():
        rows = lax.broadcasted_iota(jnp.int32, xs_ref.shape, 0)
        x_hi, x_lo = _unpack_bf16_pairs(jnp.where(rows < valid, xs_ref[...], 0))
        ys_ref[...] = _pack_bf16_pairs(_swiglu(x_hi, x_lo, wgb_ref, wub_ref, wdb_ref))

    @pl.when(valid == 0)
    def _():
        ys_ref[...] = jnp.zeros_like(ys_ref)


def _combine_kernel(yg_ref, gate_ref, x1_ref, wsg_ref, wsu_ref, wsd_ref, g_ref, b_ref, o_ref):
    x1 = x1_ref[...]
    half = x1.shape[1] // 2
    acc = _swiglu(x1[:, :half].astype(BF16), x1[:, half:].astype(BF16), wsg_ref, wsu_ref, wsd_ref)
    gates = gate_ref[...]
    acc_hi, acc_lo = acc[:, :half], acc[:, half:]
    for k in range(TOP_K):
        y_hi, y_lo = _unpack_bf16_pairs(yg_ref[k])
        gk = gates[:, k:k + 1]
        acc_hi = acc_hi + gk * y_hi.astype(F32)
        acc_lo = acc_lo + gk * y_lo.astype(F32)
    ffn = jnp.concatenate([acc_hi, acc_lo], axis=1)
    o_ref[...] = _layer_norm(ALPHA * x1 + ffn, g_ref[...], b_ref[...])


def _combine_into_kernel(prev_ref, *refs):
    del prev_ref
    _combine_kernel(*refs)


SC_CORES = 2
SC_SUBCORES = 16
SC_WORKERS = SC_CORES * SC_SUBCORES
SC_CHUNK = 64


def _sc_mesh():
    return plsc.VectorSubcoreMesh(core_axis_name="c", subcore_axis_name="s")


def _sc_gather_rows(table, idx):
    n = idx.shape[0]
    d = table.shape[1]
    per_w = n // SC_WORKERS
    n_ch = per_w // SC_CHUNK

    @functools.partial(
        pl.kernel, mesh=_sc_mesh(),
        out_type=jax.ShapeDtypeStruct((n, d), table.dtype),
        scratch_types=[pltpu.VMEM((n_ch, SC_CHUNK), jnp.int32),
                       pltpu.VMEM((2, SC_CHUNK, d), table.dtype),
                       pltpu.SemaphoreType.DMA((2,)),
                       pltpu.SemaphoreType.DMA((2,))],
    )
    def k(table_hbm, idx_hbm, out_hbm, idx_v, buf, gsem, osem):
        wid = lax.axis_index("s") * SC_CORES + lax.axis_index("c")
        base = wid * per_w
        pltpu.sync_copy(idx_hbm.at[wid], idx_v)

        def gather(c, b):
            return pltpu.make_async_copy(table_hbm.at[idx_v.at[c]], buf.at[b], gsem.at[b])

        def put(c, b):
            return pltpu.make_async_copy(buf.at[b], out_hbm.at[pl.ds(base + c * SC_CHUNK, SC_CHUNK)],
                                         osem.at[b])

        gather(0, 0).start()

        @pl.loop(0, n_ch, step=2)
        def _(c):
            for b in range(2):
                cc = c + b
                gather(cc, b).wait()

                @pl.when(cc + 1 < n_ch)
                def _():
                    @pl.when(cc >= 1)
                    def _():
                        put(cc - 1, 1 - b).wait()
                    gather(cc + 1, 1 - b).start()

                put(cc, b).start()

        put(n_ch - 2, 0).wait()
        put(n_ch - 1, 1).wait()

    return k(table, idx.reshape(SC_WORKERS, n_ch, SC_CHUNK))


def _sc_scatter_rows(x, pos, n_out):
    t, d = x.shape
    kk = pos.shape[0]
    per_w = t // SC_WORKERS
    n_ch = per_w // SC_CHUNK
    pos_w = pos.reshape(kk, SC_WORKERS, n_ch, SC_CHUNK).transpose(1, 2, 0, 3)
    pos_w = pos_w.reshape(SC_WORKERS, n_ch * kk, SC_CHUNK)

    @functools.partial(
        pl.kernel, mesh=_sc_mesh(),
        out_type=jax.ShapeDtypeStruct((n_out, d), x.dtype),
        scratch_types=[pltpu.VMEM((n_ch * kk, SC_CHUNK), jnp.int32),
                       pltpu.VMEM((2, SC_CHUNK, d), x.dtype),
                       pltpu.SemaphoreType.DMA((2,)),
                       pltpu.SemaphoreType.DMA((2,))],
    )
    def k(x_hbm, pos_hbm, out_hbm, idx_v, buf, isem, osem):
        wid = lax.axis_index("s") * SC_CORES + lax.axis_index("c")
        base = wid * per_w
        pltpu.sync_copy(pos_hbm.at[wid], idx_v)

        def get(c, b):
            return pltpu.make_async_copy(x_hbm.at[pl.ds(base + c * SC_CHUNK, SC_CHUNK)], buf.at[b],
                                         isem.at[b])

        def put(c, j, b):
            return pltpu.make_async_copy(buf.at[b], out_hbm.at[idx_v.at[c * kk + j]], osem.at[b])

        get(0, 0).start()

        @pl.loop(0, n_ch, step=2)
        def _(c):
            for b in range(2):
                cc = c + b
                get(cc, b).wait()

                @pl.when(cc + 1 < n_ch)
                def _():
                    @pl.when(cc >= 1)
                    def _():
                        for j in range(kk):
                            put(cc - 1, j, 1 - b).wait()
                    get(cc + 1, 1 - b).start()

                for j in range(kk):
                    put(cc, j, b).start()

        for j in range(kk):
            put(n_ch - 2, j, 0).wait()
        for j in range(kk):
            put(n_ch - 1, j, 1).wait()

    return k(x, pos_w)


def _row(v):
    return v.reshape(1, -1).astype(F32)


def _const_spec(shape):
    nd = len(shape)
    return pl.BlockSpec(shape, lambda *_: (0,) * nd)


def _pad_heads(w, width):
    r, h, _ = w.shape
    return jnp.pad(w, ((0, 0), (0, 0), (0, HEAD_PAD - width))).reshape(r, h * HEAD_PAD)


def _half_rotate(w):
    half = QK_ROPE // 2
    return jnp.concatenate([-w[..., half:], w[..., :half]], axis=-1)


def kernel(x, positions, ln_in_g, ln_in_b, w_in, q_norm_g, kv_norm_g, w_uq, w_ukv, lambda_re, lambda_im, log_step, b_re, b_im, c_re, c_im, d_skip, w_glu, b_glu, attn_out_g, ssm_out_g, w_o, ln1_g, ln1_b, w_router, router_bias, w_gate, w_up, w_down, ws_gate, ws_up, ws_down, ln2_g, ln2_b):
    B, S, D = x.shape
    T = B * S
    l = 0
    ssm_width = w_glu.shape[-1]
    n_groups = ssm_width // SSM_GROUP
    n_state = n_groups * SSM_STATE
    mla_width = MLA_HEADS * V_DIM
    qk_pad = MLA_HEADS * HEAD_PAD
    cparams = functools.partial(pltpu.CompilerParams, vmem_limit_bytes=VMEM_LIMIT)

    s1, s2, s3 = Q_RANK, Q_RANK + KV_RANK, Q_RANK + KV_RANK + QK_ROPE
    wi = w_in[l]
    w_kr = wi[:, s2:s3]
    pad_rope = lambda w: jnp.pad(w, ((0, 0), (QK_NOPE, HEAD_PAD - QK_NOPE - QK_ROPE)))
    w1 = jnp.concatenate([wi[:, :s2], wi[:, s3:], pad_rope(w_kr), pad_rope(_half_rotate(w_kr))],
                         axis=1).astype(BF16)
    wq = w_uq[l]
    zeros_nope = jnp.zeros(wq.shape[:2] + (QK_NOPE,), wq.dtype)
    wq_main = _pad_heads(wq, QK_NOPE + QK_ROPE).astype(BF16)
    wq_rot = _pad_heads(jnp.concatenate([zeros_nope, _half_rotate(wq[..., QK_NOPE:])], axis=-1),
                        QK_NOPE + QK_ROPE).astype(BF16)
    wkv = w_ukv[l]
    wk = _pad_heads(wkv[..., :QK_NOPE], QK_NOPE).astype(BF16)
    wv = _pad_heads(wkv[..., QK_NOPE:], V_DIM).astype(BF16)
    half = QK_ROPE // 2
    inv_freq = ROPE_THETA ** (-jnp.arange(half, dtype=F32) / half)
    freq = jnp.pad(jnp.concatenate([inv_freq, inv_freq]),
                   (QK_NOPE, HEAD_PAD - QK_NOPE - QK_ROPE)).reshape(1, HEAD_PAD)
    pos_f = positions.astype(F32).reshape(T, 1)

    tm = min(512, T)
    w1_cols = w1.shape[1]
    tok = lambda width: pl.BlockSpec((tm, width), lambda i: (i, 0))
    xn, q, k, v, u = pl.pallas_call(
        functools.partial(_inproj_kernel, ssm_width=ssm_width),
        grid=(T // tm,),
        in_specs=[tok(D), tok(1), _const_spec((1, D)), _const_spec((1, D)),
                  _const_spec((D, w1_cols)), _const_spec((1, Q_RANK)), _const_spec((1, KV_RANK)),
                  _const_spec((Q_RANK, qk_pad)), _const_spec((Q_RANK, qk_pad)),
                  _const_spec((KV_RANK, qk_pad)), _const_spec((KV_RANK, qk_pad)),
                  _const_spec((1, HEAD_PAD))],
        out_specs=[tok(D), tok(qk_pad), tok(qk_pad), tok(qk_pad), tok(ssm_width)],
        out_shape=[jax.ShapeDtypeStruct((T, D), F32), jax.ShapeDtypeStruct((T, qk_pad), BF16),
                   jax.ShapeDtypeStruct((T, qk_pad), BF16), jax.ShapeDtypeStruct((T, qk_pad), BF16),
                   jax.ShapeDtypeStruct((T, ssm_width), F32)],
        compiler_params=cparams(dimension_semantics=("parallel",)),
        name="inproj",
    )(x.reshape(T, D), pos_f, _row(ln_in_g), _row(ln_in_b), w1, _row(q_norm_g[l]), _row(kv_norm_g[l]),
      wq_main, wq_rot, wk, wv, freq)

    tq = min(256, S)
    nq = S // tq
    att = pl.pallas_call(
        functools.partial(_attn_kernel, tq=tq, heads=MLA_HEADS),
        grid=(B, nq),
        in_specs=[pl.BlockSpec((tq, qk_pad), lambda b, i: (b * nq + i, 0)),
                  pl.BlockSpec((S, qk_pad), lambda b, i: (b, 0)),
                  pl.BlockSpec((S, qk_pad), lambda b, i: (b, 0))],
        out_specs=pl.BlockSpec((tq, mla_width), lambda b, i: (b * nq + i, 0)),
        out_shape=jax.ShapeDtypeStruct((T, mla_width), BF16),
        scratch_shapes=[pltpu.VMEM((MLA_HEADS, nq, tq, tq), F32),
                        pltpu.VMEM((MLA_HEADS, tq, LANES), F32),
                        pltpu.VMEM((MLA_HEADS, tq, HEAD_PAD), F32)],
        compiler_params=cparams(dimension_semantics=("parallel", "arbitrary")),
        name="attention",
    )(q, k, v)

    lam = lax.complex(jnp.minimum(lambda_re[l].astype(F32), -1e-4), lambda_im[l].astype(F32))
    step = jnp.exp(log_step[l].astype(F32))[:, None]
    lam_bar = jnp.exp(lam * step)
    b_bar = ((lam_bar - 1.0) / lam)[..., None] * lax.complex(b_re[l].astype(F32), b_im[l].astype(F32))
    n_slab = ssm_width // LANES
    g_per_slab = n_groups // n_slab
    eye = jnp.eye(n_groups, dtype=F32)

    def expand_in(bpart):
        return jnp.einsum('gpc,gh->gchp', bpart, eye).reshape(ssm_width, n_state)

    def expand_out(cpart):
        return jnp.einsum('gcp,gh->gphc', cpart, eye).reshape(n_state, ssm_width)

    slab = n_state // n_slab
    win_re, win_im = expand_in(jnp.real(b_bar)), expand_in(jnp.imag(b_bar))
    win = jnp.stack([jnp.concatenate([win_re[j * LANES:(j + 1) * LANES, j * slab:(j + 1) * slab],
                                      win_im[j * LANES:(j + 1) * LANES, j * slab:(j + 1) * slab]], axis=1)
                     for j in range(n_slab)]).astype(BF16)
    wc_re, wc_im = expand_out(c_re[l].astype(F32)), expand_out(-c_im[l].astype(F32))
    cre = jnp.stack([wc_re[j * slab:(j + 1) * slab, j * LANES:(j + 1) * LANES] for j in range(n_slab)]).astype(BF16)
    cim = jnp.stack([wc_im[j * slab:(j + 1) * slab, j * LANES:(j + 1) * LANES] for j in range(n_slab)]).astype(BF16)
    a_re = jnp.real(lam_bar).reshape(1, n_state)
    a_im = jnp.imag(lam_bar).reshape(1, n_state)

    lt = min(64, S)
    ssm = pl.pallas_call(
        functools.partial(_s5_kernel, batch=B, lt=lt, n_state=n_state),
        grid=(S // lt,),
        in_specs=[pl.BlockSpec((B, lt, ssm_width), lambda t: (0, t, 0)),
                  _const_spec(win.shape), _const_spec((1, n_state)), _const_spec((1, n_state)),
                  _const_spec(cre.shape), _const_spec(cim.shape), _const_spec((1, ssm_width)),
                  _const_spec((ssm_width, ssm_width)), _const_spec((1, ssm_width))],
        out_specs=pl.BlockSpec((B, lt, ssm_width), lambda t: (0, t, 0)),
        out_shape=jax.ShapeDtypeStruct((B, S, ssm_width), BF16),
        scratch_shapes=[pltpu.VMEM((2 * n_state // LANES, B * lt, LANES), F32),
                        pltpu.VMEM((n_state // LANES, B, LANES), F32),
                        pltpu.VMEM((n_state // LANES, B, LANES), F32),
                        pltpu.VMEM((ssm_width // LANES, B * lt, LANES), F32),
                        pltpu.VMEM((B * lt, ssm_width), F32)],
        compiler_params=cparams(dimension_semantics=("arbitrary",)),
        name="s5",
    )(u.reshape(B, S, ssm_width), win, a_re, a_im, cre, cim, _row(d_skip[l]),
      w_glu[l].astype(BF16), _row(b_glu[l]))

    wo = w_o[l].astype(BF16)
    wr_t = w_router[l].T.astype(F32)
    wr_hi = wr_t.astype(BF16)
    wr_lo = (wr_t - wr_hi.astype(F32)).astype(BF16)
    half = D // 2
    n_slabs = MOE_SLABS
    ts = T // n_slabs
    nt = ts // tm
    ssm2 = ssm.reshape(T, ssm_width)
    rbias = router_bias[l].astype(F32).reshape(N_EXPERTS, 1)
    kt = lambda dt: jax.ShapeDtypeStruct((TOP_K, ts), dt)
    k_spec = pl.BlockSpec((TOP_K, tm), lambda i: (0, i))

    def route(s):
        tok_s = lambda width: pl.BlockSpec((tm, width), lambda i: (i + s * nt, 0))
        return pl.pallas_call(
            _mix_kernel,
            grid=(nt,),
            in_specs=[tok_s(mla_width), tok_s(ssm_width), tok_s(D), _const_spec((1, mla_width)),
                      _const_spec((1, ssm_width)), _const_spec((mla_width, D)),
                      _const_spec((ssm_width, D)), _const_spec((1, D)), _const_spec((1, D)),
                      _const_spec((N_EXPERTS, D)), _const_spec((N_EXPERTS, D)),
                      _const_spec((N_EXPERTS, 1))],
            out_specs=[tok(D), tok(half), k_spec, tok(TOP_K), k_spec, _const_spec((N_EXPERTS, LANES))],
            out_shape=[jax.ShapeDtypeStruct((ts, D), F32), jax.ShapeDtypeStruct((ts, half), jnp.int32),
                       kt(jnp.int32), jax.ShapeDtypeStruct((ts, TOP_K), F32), kt(jnp.int32),
                       jax.ShapeDtypeStruct((N_EXPERTS, LANES), F32)],
            scratch_shapes=[pltpu.VMEM((N_EXPERTS, 1), F32), pltpu.VMEM((tm, tm), BF16)],
            compiler_params=cparams(dimension_semantics=("arbitrary",)),
            name="mix_router",
        )(att, ssm2, xn, _row(attn_out_g[l]), _row(ssm_out_g[l]), wo[:mla_width], wo[mla_width:],
          _row(ln1_g[l]), _row(ln1_b[l]), wr_hi, wr_lo, rbias)

    tr = EXPERT_ROW_TILE
    n_tiles = (ts * TOP_K) // tr + N_EXPERTS
    n_rows = n_tiles * tr
    tp = min(2048, ts)

    def dispatch(x1p, idx_k, rank_k, counts):
        cnt = counts[:, 0].astype(jnp.int32)
        tiles_e = (cnt + tr - 1) // tr
        tile_end = jnp.cumsum(tiles_e)
        tile_start = tile_end - tiles_e
        tile_ids = jnp.arange(n_tiles, dtype=jnp.int32)
        tile_expert = jnp.sum((tile_end[None, :] <= tile_ids[:, None]).astype(jnp.int32), axis=1)
        tile_expert = jnp.minimum(tile_expert, N_EXPERTS - 1)
        owner = (tile_start[None, :] <= tile_ids[:, None]) & (tile_ids[:, None] < tile_end[None, :])
        left = jnp.sum(jnp.where(owner, cnt[None, :] - (tile_ids[:, None] - tile_start[None, :]) * tr, 0),
                       axis=1)
        tile_valid = jnp.clip(left, 0, tr).astype(jnp.int32)
        pos = pl.pallas_call(
            _positions_kernel,
            grid_spec=pltpu.PrefetchScalarGridSpec(
                num_scalar_prefetch=1, grid=(ts // tp,),
                in_specs=[pl.BlockSpec((TOP_K, tp), lambda i, off: (0, i)),
                          pl.BlockSpec((TOP_K, tp), lambda i, off: (0, i))],
                out_specs=pl.BlockSpec((TOP_K, tp), lambda i, off: (0, i))),
            out_shape=kt(jnp.int32),
            name="positions",
        )((tile_start * tr).astype(jnp.int32), idx_k, rank_k)
        return _sc_scatter_rows(x1p, pos, n_rows), pos, tile_expert, tile_valid

    wg, wu, wd = w_gate[l], w_up[l], w_down[l]
    ff = wg.shape[-1]

    def experts(xs, tile_expert, tile_valid):
        return pl.pallas_call(
            _experts_kernel,
            grid_spec=pltpu.PrefetchScalarGridSpec(
                num_scalar_prefetch=2, grid=(n_tiles,),
                in_specs=[pl.BlockSpec((tr, half), lambda i, te, tv: (i, 0)),
                          pl.BlockSpec((1, D, ff), lambda i, te, tv: (te[i], 0, 0)),
                          pl.BlockSpec((1, D, ff), lambda i, te, tv: (te[i], 0, 0)),
                          pl.BlockSpec((1, ff, D), lambda i, te, tv: (te[i], 0, 0))],
                out_specs=pl.BlockSpec((tr, half), lambda i, te, tv: (i, 0)),
                scratch_shapes=[pltpu.VMEM((D, ff), BF16), pltpu.VMEM((D, ff), BF16),
                                pltpu.VMEM((ff, D), BF16)]),
            out_shape=jax.ShapeDtypeStruct((n_rows, half), jnp.int32),
            compiler_params=cparams(dimension_semantics=("arbitrary",)),
            name="experts",
        )(tile_expert, tile_valid, xs, wg, wu, wd)

    shared = (ws_gate[l].astype(BF16), ws_up[l].astype(BF16), ws_down[l].astype(BF16))

    def combine(s, out_so_far, yg, gate_k, x1):
        specs = [pl.BlockSpec((TOP_K, tm, half), lambda i: (0, i, 0)),
                 pl.BlockSpec((tm, TOP_K), lambda i: (i, 0)), tok(D),
                 _const_spec((D, ff)), _const_spec((D, ff)), _const_spec((ff, D)),
                 _const_spec((1, D)), _const_spec((1, D))]
        args = (yg.reshape(TOP_K, ts, half), gate_k, x1, *shared, _row(ln2_g[l]), _row(ln2_b[l]))
        body, aliases = _combine_kernel, {}
        if out_so_far is not None:
            specs = [pl.BlockSpec(memory_space=pl.ANY)] + specs
            args = (out_so_far,) + args
            body, aliases = _combine_into_kernel, {0: 0}
        return pl.pallas_call(
            body,
            grid=(nt,),
            in_specs=specs,
            out_specs=pl.BlockSpec((tm, D), lambda i: (i + s * nt, 0)),
            out_shape=jax.ShapeDtypeStruct((T, D), F32),
            input_output_aliases=aliases,
            compiler_params=cparams(dimension_semantics=("parallel",)),
            name="combine",
        )(*args)

    routed, moved = [], []
    for s in range(n_slabs):
        x1, x1p, idx_k, gate_k, rank_k, counts = route(s)
        routed.append((x1, gate_k))
        moved.append(dispatch(x1p, idx_k, rank_k, counts))
    gathered = []
    for xs, pos, tile_expert, tile_valid in moved:
        gathered.append(_sc_gather_rows(experts(xs, tile_expert, tile_valid), pos.reshape(TOP_K * ts)))
    out = None
    for s in range(n_slabs):
        out = combine(s, out, gathered[s], routed[s][1], routed[s][0])
    return out.reshape(B, S, D)
```

```python
import functools

import jax
import jax.numpy as jnp
from jax import lax
from jax.experimental import pallas as pl
from jax.experimental.pallas import tpu as pltpu
from jax.experimental.pallas import tpu_sc as plsc

CHUNK = 64
MLA_HEADS = 8
QK_NOPE = 64
QK_ROPE = 32
V_DIM = 64
Q_RANK = 256
KV_RANK = 128
ROPE_THETA = 10000.0
SSM_GROUP = 16
SSM_STATE = 64
N_EXPERTS = 64
TOP_K = 8
N_GROUPS = 8
TOP_GROUPS = 4
ROUTED_SCALE = 2.5
DEPTH = 1
ALPHA = (2.0 * DEPTH) ** 0.25
EPS = 1e-5
LOG2_E = 1.4426950408889634

LANES = 128
HEAD_PAD = LANES
VMEM_LIMIT = 56 * 1024 * 1024
EXPERT_ROW_TILE = 1024
MOE_SLABS = 2

BF16 = jnp.bfloat16
F32 = jnp.float32
NT_DIMS = (((1,), (1,)), ((), ()))


def _dot(a, b):
    return jnp.dot(a, b, preferred_element_type=F32)


def _layer_norm(x, g, b):
    mu = jnp.mean(x, axis=-1, keepdims=True)
    xc = x - mu
    var = jnp.mean(xc * xc, axis=-1, keepdims=True)
    return xc * lax.rsqrt(var + EPS) * g + b


def _rms_norm(x, g):
    return x * lax.rsqrt(jnp.mean(x * x, axis=-1, keepdims=True) + EPS) * g


def _inproj_kernel(x_ref, pos_ref, lng_ref, lnb_ref, w1_ref, qg_ref, kvg_ref,
                   wq_ref, wqr_ref, wk_ref, wv_ref, freq_ref,
                   xn_ref, q_ref, k_ref, v_ref, u_ref, *, ssm_width):
    xn = _layer_norm(x_ref[...], lng_ref[...], lnb_ref[...])
    xn_ref[...] = xn
    h = _dot(xn.astype(BF16), w1_ref[...])
    o1 = Q_RANK
    o2 = o1 + KV_RANK
    o3 = o2 + ssm_width
    o4 = o3 + HEAD_PAD
    cq = h[:, :o1]
    ckv = h[:, o1:o2]
    u_ref[...] = h[:, o2:o3]
    kr_raw = h[:, o3:o4]
    kr_rot = h[:, o4:o4 + HEAD_PAD]
    cqn = _rms_norm(cq, qg_ref[...]).astype(BF16)
    ckvn = _rms_norm(ckv, kvg_ref[...]).astype(BF16)

    ang = pos_ref[...] * freq_ref[...]
    c = jnp.cos(ang)
    s = jnp.sin(ang)
    lane = lax.broadcasted_iota(jnp.int32, (1, HEAD_PAD), 1)
    scale = (QK_NOPE + QK_ROPE) ** -0.5 * LOG2_E
    is_rope = (lane >= QK_NOPE) & (lane < QK_NOPE + QK_ROPE)
    cos1 = jnp.where(lane < QK_NOPE, 1.0, jnp.where(is_rope, c, 0.0)) * scale
    sin1 = jnp.where(is_rope, s, 0.0) * scale
    cos_t = jnp.concatenate([cos1] * MLA_HEADS, axis=1)
    sin_t = jnp.concatenate([sin1] * MLA_HEADS, axis=1)
    q = _dot(cqn, wq_ref[...]) * cos_t + _dot(cqn, wqr_ref[...]) * sin_t
    q_ref[...] = q.astype(BF16)

    kr = kr_raw * c + kr_rot * s
    k = _dot(ckvn, wk_ref[...]) + jnp.concatenate([kr] * MLA_HEADS, axis=1)
    k_ref[...] = k.astype(BF16)
    ones_col = jnp.concatenate([jnp.where(lane == V_DIM, 1.0, 0.0)] * MLA_HEADS, axis=1)
    v_ref[...] = (_dot(ckvn, wv_ref[...]) + ones_col).astype(BF16)


def _attn_kernel(q_ref, k_ref, v_ref, o_ref, *, tq, heads, n_qtiles):
    qi = pl.program_id(1)
    row_chunk = lax.broadcasted_iota(jnp.int32, (tq, tq), 0) // CHUNK
    col_chunk = lax.broadcasted_iota(jnp.int32, (tq, tq), 1) // CHUNK
    diag_mask = row_chunk >= col_chunk

    def tile(n_blocks):
        keys = n_blocks * tq
        cols = [slice(h * HEAD_PAD, (h + 1) * HEAD_PAD) for h in range(heads)]

        def mask_diag(s):
            s_diag = jnp.where(diag_mask, s[:, keys - tq:], -jnp.inf)
            return s_diag if n_blocks == 1 else jnp.concatenate([s[:, :keys - tq], s_diag], axis=1)

        ss = [lax.dot_general(q_ref[:, c], k_ref[:keys, c], NT_DIMS, preferred_element_type=F32)
              for c in cols]
        ss = [mask_diag(s) for s in ss]
        ms = [jnp.max(s, axis=-1, keepdims=True) for s in ss]
        ps = [jnp.exp2(s - m).astype(BF16) for s, m in zip(ss, ms)]
        accs = [_dot(p, v_ref[:keys, c]) for p, c in zip(ps, cols)]
        outs = [acc[:, :V_DIM] / acc[:, V_DIM:V_DIM + 1] for acc in accs]
        o_ref[...] = jnp.concatenate(outs, axis=1).astype(o_ref.dtype)

    for c in range(n_qtiles):
        pl.when(qi == c)(functools.partial(tile, c + 1))


def _s5_kernel(u_ref, win_ref, are_ref, aim_ref, cre_ref, cim_ref, dskip_ref,
               wglu_ref, bglu_ref, o_ref, vx_ref, hre_ref, him_ref, io_ref, utm_ref, *,
               batch, lt, n_state):
    ti = pl.program_id(0)

    @pl.when(ti == 0)
    def _():
        hre_ref[...] = jnp.zeros_like(hre_ref)
        him_ref[...] = jnp.zeros_like(him_ref)

    width = u_ref.shape[-1]
    n_slab = width // LANES
    n_tiles = n_state // LANES
    slab_tiles = n_tiles // n_slab
    for b in range(batch):
        for c in range(n_slab):
            io_ref[c, b * lt:(b + 1) * lt, :] = u_ref[b, :, c * LANES:(c + 1) * LANES]

    ys = []
    for j in range(n_slab):
        tiles = range(j * slab_tiles, (j + 1) * slab_tiles)
        for t in range(lt):
            utm_ref[t * batch:(t + 1) * batch, j * LANES:(j + 1) * LANES] = (
                io_ref[j, pl.ds(t, batch, stride=lt), :])
        ub = utm_ref[:, j * LANES:(j + 1) * LANES].astype(BF16)
        vj = _dot(ub, win_ref[j])
        for i, c in enumerate(tiles):
            vx_ref[c] = vj[:, i * LANES:(i + 1) * LANES]
            vx_ref[n_tiles + c] = vj[:, (slab_tiles + i) * LANES:(slab_tiles + i + 1) * LANES]

        ar = [jnp.broadcast_to(are_ref[:, c * LANES:(c + 1) * LANES], (batch, LANES)) for c in tiles]
        ai = [jnp.broadcast_to(aim_ref[:, c * LANES:(c + 1) * LANES], (batch, LANES)) for c in tiles]
        hr = [hre_ref[c] for c in tiles]
        hi = [him_ref[c] for c in tiles]
        for t in range(lt):
            rows = slice(t * batch, (t + 1) * batch)
            for n, c in enumerate(tiles):
                nr = ar[n] * hr[n] - ai[n] * hi[n] + vx_ref[c, rows, :]
                ni = ar[n] * hi[n] + ai[n] * hr[n] + vx_ref[n_tiles + c, rows, :]
                vx_ref[c, rows, :] = nr
                vx_ref[n_tiles + c, rows, :] = ni
                hr[n], hi[n] = nr, ni
        for n, c in enumerate(tiles):
            hre_ref[c] = hr[n]
            him_ref[c] = hi[n]

        xr = jnp.concatenate([vx_ref[c].astype(BF16) for c in tiles], axis=1)
        xi = jnp.concatenate([vx_ref[n_tiles + c].astype(BF16) for c in tiles], axis=1)
        ys.append(_dot(xr, cre_ref[j]) + _dot(xi, cim_ref[j]))
    y = jnp.concatenate(ys, axis=1) + dskip_ref[...] * utm_ref[...]
    y = jax.nn.gelu(y)
    z = _dot(y.astype(BF16), wglu_ref[...]) + bglu_ref[...]
    out = y * jax.nn.sigmoid(z)
    for c in range(n_slab):
        io_ref[c] = out[:, c * LANES:(c + 1) * LANES]
    for b in range(batch):
        for c in range(n_slab):
            o_ref[b, :, c * LANES:(c + 1) * LANES] = io_ref[c, pl.ds(b, lt, stride=batch), :].astype(o_ref.dtype)


def _router_gates(logits_t, rbias):
    n_exp, tm = logits_t.shape
    per_group = n_exp // N_GROUPS
    scores = jax.nn.sigmoid(logits_t)
    sel = scores + rbias
    neg_inf = -jnp.inf
    sub_iota = lax.broadcasted_iota(jnp.int32, (per_group, tm), 0).astype(F32)
    group_score = []
    for g in range(N_GROUPS):
        sg = sel[g * per_group:(g + 1) * per_group, :]
        m1 = jnp.max(sg, axis=0, keepdims=True)
        first = jnp.min(jnp.where(sg == m1, sub_iota, float(per_group)), axis=0, keepdims=True)
        m2 = jnp.max(jnp.where(sub_iota == first, neg_inf, sg), axis=0, keepdims=True)
        group_score.append(m1 + m2)
    masked = []
    for g in range(N_GROUPS):
        rank = jnp.zeros((1, tm), F32)
        for g2 in range(N_GROUPS):
            if g2 == g:
                continue
            ahead = (group_score[g2] >= group_score[g]) if g2 < g else (group_score[g2] > group_score[g])
            rank = rank + jnp.where(ahead, 1.0, 0.0)
        keep = rank < float(TOP_GROUPS)
        masked.append(jnp.where(keep, sel[g * per_group:(g + 1) * per_group, :], neg_inf))
    cur = jnp.concatenate(masked, axis=0)
    iota = lax.broadcasted_iota(jnp.int32, (n_exp, tm), 0).astype(F32)
    chosen = jnp.zeros((n_exp, tm), F32)
    picks, weights = [], []
    for _ in range(TOP_K):
        m = jnp.max(cur, axis=0, keepdims=True)
        idx = jnp.min(jnp.where(cur == m, iota, float(n_exp)), axis=0, keepdims=True)
        pick = iota == idx
        chosen = jnp.where(pick, 1.0, chosen)
        cur = jnp.where(pick, neg_inf, cur)
        picks.append(idx)
        weights.append(jnp.sum(jnp.where(pick, scores, 0.0), axis=0, keepdims=True))
    idx_k = jnp.concatenate(picks, axis=0)
    w_k = jnp.concatenate(weights, axis=0)
    gate_k = w_k / jnp.sum(w_k, axis=0, keepdims=True) * ROUTED_SCALE
    return idx_k, gate_k, chosen


def _pack_bf16_pairs(x):
    n = x.shape[1] // 2
    hi = lax.bitcast_convert_type(x[:, :n].astype(BF16).astype(F32), jnp.int32)
    lo = lax.bitcast_convert_type(x[:, n:].astype(BF16).astype(F32), jnp.int32)
    return hi | lax.shift_right_logical(lo, 16)


def _unpack_bf16_pairs(p):
    hi = lax.bitcast_convert_type(p & jnp.int32(-65536), F32).astype(BF16)
    lo = lax.bitcast_convert_type(lax.shift_left(p, 16), F32).astype(BF16)
    return hi, lo


def _mix_kernel(att_ref, ssm_ref, xn_ref, ag_ref, sg_ref, woa_ref, wos_ref,
                g_ref, b_ref, wrh_ref, wrl_ref, rb_ref,
                x1_ref, x1p_ref, idx_ref, gate_ref, rank_ref, cnt_ref, carry_ref, before_ref):
    @pl.when(pl.program_id(0) == 0)
    def _():
        carry_ref[...] = jnp.zeros_like(carry_ref)
        tm = before_ref.shape[0]
        before = (lax.broadcasted_iota(jnp.int32, (tm, tm), 0)
                  < lax.broadcasted_iota(jnp.int32, (tm, tm), 1))
        before_ref[...] = jnp.where(before, 1.0, 0.0).astype(BF16)

    an = _rms_norm(att_ref[...].astype(F32), ag_ref[...]).astype(BF16)
    sn = _rms_norm(ssm_ref[...].astype(F32), sg_ref[...]).astype(BF16)
    mix = _dot(an, woa_ref[...]) + _dot(sn, wos_ref[...])
    x1 = _layer_norm(ALPHA * xn_ref[...] + mix, g_ref[...], b_ref[...])
    x1_ref[...] = x1
    x1p_ref[...] = _pack_bf16_pairs(x1)
    x_hi = x1.astype(BF16)
    x_lo = (x1 - x_hi.astype(F32)).astype(BF16)
    dg = functools.partial(lax.dot_general, dimension_numbers=NT_DIMS, preferred_element_type=F32)
    logits_t = dg(wrh_ref[...], x_hi) + dg(wrl_ref[...], x_hi) + dg(wrh_ref[...], x_lo)
    idx_k, gate_k, chosen = _router_gates(logits_t, rb_ref[...])
    idx_ref[...] = idx_k.astype(jnp.int32)
    gate_ref[...] = gate_k.T

    n_exp, tm = chosen.shape
    excl = _dot(chosen.astype(BF16), before_ref[...])
    rank_full = carry_ref[...] + excl
    iota = lax.broadcasted_iota(jnp.int32, (n_exp, tm), 0).astype(F32)
    ranks = [jnp.sum(jnp.where(iota == idx_k[k:k + 1, :], rank_full, 0.0), axis=0, keepdims=True)
             for k in range(TOP_K)]
    rank_ref[...] = jnp.concatenate(ranks, axis=0).astype(jnp.int32)
    total = carry_ref[...] + jnp.sum(chosen, axis=1, keepdims=True)
    carry_ref[...] = total
    cnt_ref[...] = jnp.broadcast_to(total, cnt_ref.shape)


def _positions_kernel(off_ref, idx_ref, rank_ref, pos_ref):
    idx = idx_ref[...]
    base = jnp.zeros(idx.shape, jnp.int32)
    for e in range(N_EXPERTS):
        base = jnp.where(idx == e, off_ref[e], base)
    pos_ref[...] = rank_ref[...] + base


def _swiglu(x_hi, x_lo, wg_ref, wu_ref, wd_ref):
    half = x_hi.shape[1]
    hg = _dot(x_hi, wg_ref[:half, :]) + _dot(x_lo, wg_ref[half:, :])
    hu = _dot(x_hi, wu_ref[:half, :]) + _dot(x_lo, wu_ref[half:, :])
    h = jax.nn.silu(hg) * hu
    return _dot(h.astype(BF16), wd_ref[...])


def _experts_kernel(te_ref, valid_ref, xs_ref, wg_ref, wu_ref, wd_ref, ys_ref,
                    wgb_ref, wub_ref, wdb_ref):
    i = pl.program_id(0)
    valid = valid_ref[i]

    @pl.when((i == 0) | (te_ref[i] != te_ref[jnp.maximum(i - 1, 0)]))
    def _():
        wgb_ref[...] = wg_ref[0].astype(BF16)
        wub_ref[...] = wu_ref[0].astype(BF16)
        wdb_ref[...] = wd_ref[0].astype(BF16)

    @pl.when(valid > 0)
    def _():
        rows = lax.broadcasted_iota(jnp.int32, xs_ref.shape, 0)
        x_hi, x_lo = _unpack_bf16_pairs(jnp.where(rows < valid, xs_ref[...], 0))
        ys_ref[...] = _pack_bf16_pairs(_swiglu(x_hi, x_lo, wgb_ref, wub_ref, wdb_ref))

    @pl.when(valid == 0)
    def _():
        ys_ref[...] = jnp.zeros_like(ys_ref)


def _combine_kernel(yg_ref, gate_ref, x1_ref, wsg_ref, wsu_ref, wsd_ref, g_ref, b_ref, o_ref):
    x1 = x1_ref[...]
    half = x1.shape[1] // 2
    acc = _swiglu(x1[:, :half].astype(BF16), x1[:, half:].astype(BF16), wsg_ref, wsu_ref, wsd_ref)
    gates = gate_ref[...]
    acc_hi, acc_lo = acc[:, :half], acc[:, half:]
    for k in range(TOP_K):
        y_hi, y_lo = _unpack_bf16_pairs(yg_ref[k])
        gk = gates[:, k:k + 1]
        acc_hi = acc_hi + gk * y_hi.astype(F32)
        acc_lo = acc_lo + gk * y_lo.astype(F32)
    ffn = jnp.concatenate([acc_hi, acc_lo], axis=1)
    o_ref[...] = _layer_norm(ALPHA * x1 + ffn, g_ref[...], b_ref[...])


def _combine_into_kernel(prev_ref, *refs):
    del prev_ref
    _combine_kernel(*refs)


SC_CORES = 2
SC_SUBCORES = 16
SC_WORKERS = SC_CORES * SC_SUBCORES
SC_CHUNK = 64


def _sc_mesh():
    return plsc.VectorSubcoreMesh(core_axis_name="c", subcore_axis_name="s")


def _sc_gather_rows(table, idx):
    n = idx.shape[0]
    d = table.shape[1]
    per_w = n // SC_WORKERS
    n_ch = per_w // SC_CHUNK

    @functools.partial(
        pl.kernel, mesh=_sc_mesh(),
        out_type=jax.ShapeDtypeStruct((n, d), table.dtype),
        scratch_types=[pltpu.VMEM((n_ch, SC_CHUNK), jnp.int32),
                       pltpu.VMEM((2, SC_CHUNK, d), table.dtype),
                       pltpu.SemaphoreType.DMA((2,)),
                       pltpu.SemaphoreType.DMA((2,))],
    )
    def k(table_hbm, idx_hbm, out_hbm, idx_v, buf, gsem, osem):
        wid = lax.axis_index("s") * SC_CORES + lax.axis_index("c")
        base = wid * per_w
        pltpu.sync_copy(idx_hbm.at[wid], idx_v)

        def gather(c, b):
            return pltpu.make_async_copy(table_hbm.at[idx_v.at[c]], buf.at[b], gsem.at[b])

        def put(c, b):
            return pltpu.make_async_copy(buf.at[b], out_hbm.at[pl.ds(base + c * SC_CHUNK, SC_CHUNK)],
                                         osem.at[b])

        gather(0, 0).start()

        @pl.loop(0, n_ch, step=2)
        def _(c):
            for b in range(2):
                cc = c + b
                gather(cc, b).wait()

                @pl.when(cc + 1 < n_ch)
                def _():
                    @pl.when(cc >= 1)
                    def _():
                        put(cc - 1, 1 - b).wait()
                    gather(cc + 1, 1 - b).start()

                put(cc, b).start()

        put(n_ch - 2, 0).wait()
        put(n_ch - 1, 1).wait()

    return k(table, idx.reshape(SC_WORKERS, n_ch, SC_CHUNK))


def _sc_scatter_rows(x, pos, n_out):
    t, d = x.shape
    kk = pos.shape[0]
    per_w = t // SC_WORKERS
    n_ch = per_w // SC_CHUNK
    pos_w = pos.reshape(kk, SC_WORKERS, n_ch, SC_CHUNK).transpose(1, 2, 0, 3)
    pos_w = pos_w.reshape(SC_WORKERS, n_ch * kk, SC_CHUNK)

    @functools.partial(
        pl.kernel, mesh=_sc_mesh(),
        out_type=jax.ShapeDtypeStruct((n_out, d), x.dtype),
        scratch_types=[pltpu.VMEM((n_ch * kk, SC_CHUNK), jnp.int32),
                       pltpu.VMEM((2, SC_CHUNK, d), x.dtype),
                       pltpu.SemaphoreType.DMA((2,)),
                       pltpu.SemaphoreType.DMA((2,))],
    )
    def k(x_hbm, pos_hbm, out_hbm, idx_v, buf, isem, osem):
        wid = lax.axis_index("s") * SC_CORES + lax.axis_index("c")
        base = wid * per_w
        pltpu.sync_copy(pos_hbm.at[wid], idx_v)

        def get(c, b):
            return pltpu.make_async_copy(x_hbm.at[pl.ds(base + c * SC_CHUNK, SC_CHUNK)], buf.at[b],
                                         isem.at[b])

        def put(c, j, b):
            return pltpu.make_async_copy(buf.at[b], out_hbm.at[idx_v.at[c * kk + j]], osem.at[b])

        get(0, 0).start()

        @pl.loop(0, n_ch, step=2)
        def _(c):
            for b in range(2):
                cc = c + b
                get(cc, b).wait()

                @pl.when(cc + 1 < n_ch)
                def _():
                    @pl.when(cc >= 1)
                    def _():
                        for j in range(kk):
                            put(cc - 1, j, 1 - b).wait()
                    get(cc + 1, 1 - b).start()

                for j in range(kk):
                    put(cc, j, b).start()

        for j in range(kk):
            put(n_ch - 2, j, 0).wait()
        for j in range(kk):
            put(n_ch - 1, j, 1).wait()

    return k(x, pos_w)


def _row(v):
    return v.reshape(1, -1).astype(F32)


def _const_spec(shape):
    nd = len(shape)
    return pl.BlockSpec(shape, lambda *_: (0,) * nd)


def _pad_heads(w, width):
    r, h, _ = w.shape
    return jnp.pad(w, ((0, 0), (0, 0), (0, HEAD_PAD - width))).reshape(r, h * HEAD_PAD)


def _half_rotate(w):
    half = QK_ROPE // 2
    return jnp.concatenate([-w[..., half:], w[..., :half]], axis=-1)


def kernel(x, positions, ln_in_g, ln_in_b, w_in, q_norm_g, kv_norm_g, w_uq, w_ukv, lambda_re, lambda_im, log_step, b_re, b_im, c_re, c_im, d_skip, w_glu, b_glu, attn_out_g, ssm_out_g, w_o, ln1_g, ln1_b, w_router, router_bias, w_gate, w_up, w_down, ws_gate, ws_up, ws_down, ln2_g, ln2_b):
    B, S, D = x.shape
    T = B * S
    l = 0
    ssm_width = w_glu.shape[-1]
    n_groups = ssm_width // SSM_GROUP
    n_state = n_groups * SSM_STATE
    mla_width = MLA_HEADS * V_DIM
    qk_pad = MLA_HEADS * HEAD_PAD
    cparams = functools.partial(pltpu.CompilerParams, vmem_limit_bytes=VMEM_LIMIT)

    s1, s2, s3 = Q_RANK, Q_RANK + KV_RANK, Q_RANK + KV_RANK + QK_ROPE
    wi = w_in[l]
    w_kr = wi[:, s2:s3]
    pad_rope = lambda w: jnp.pad(w, ((0, 0), (QK_NOPE, HEAD_PAD - QK_NOPE - QK_ROPE)))
    w1 = jnp.concatenate([wi[:, :s2], wi[:, s3:], pad_rope(w_kr), pad_rope(_half_rotate(w_kr))],
                         axis=1).astype(BF16)
    wq = w_uq[l]
    zeros_nope = jnp.zeros(wq.shape[:2] + (QK_NOPE,), wq.dtype)
    wq_main = _pad_heads(wq, QK_NOPE + QK_ROPE).astype(BF16)
    wq_rot = _pad_heads(jnp.concatenate([zeros_nope, _half_rotate(wq[..., QK_NOPE:])], axis=-1),
                        QK_NOPE + QK_ROPE).astype(BF16)
    wkv = w_ukv[l]
    wk = _pad_heads(wkv[..., :QK_NOPE], QK_NOPE).astype(BF16)
    wv = _pad_heads(wkv[..., QK_NOPE:], V_DIM).astype(BF16)
    half = QK_ROPE // 2
    inv_freq = ROPE_THETA ** (-jnp.arange(half, dtype=F32) / half)
    freq = jnp.pad(jnp.concatenate([inv_freq, inv_freq]),
                   (QK_NOPE, HEAD_PAD - QK_NOPE - QK_ROPE)).reshape(1, HEAD_PAD)
    pos_f = positions.astype(F32).reshape(T, 1)

    tm = min(512, T)
    w1_cols = w1.shape[1]
    tok = lambda width: pl.BlockSpec((tm, width), lambda i: (i, 0))
    xn, q, k, v, u = pl.pallas_call(
        functools.partial(_inproj_kernel, ssm_width=ssm_width),
        grid=(T // tm,),
        in_specs=[tok(D), tok(1), _const_spec((1, D)), _const_spec((1, D)),
                  _const_spec((D, w1_cols)), _const_spec((1, Q_RANK)), _const_spec((1, KV_RANK)),
                  _const_spec((Q_RANK, qk_pad)), _const_spec((Q_RANK, qk_pad)),
                  _const_spec((KV_RANK, qk_pad)), _const_spec((KV_RANK, qk_pad)),
                  _const_spec((1, HEAD_PAD))],
        out_specs=[tok(D), tok(qk_pad), tok(qk_pad), tok(qk_pad), tok(ssm_width)],
        out_shape=[jax.ShapeDtypeStruct((T, D), F32), jax.ShapeDtypeStruct((T, qk_pad), BF16),
                   jax.ShapeDtypeStruct((T, qk_pad), BF16), jax.ShapeDtypeStruct((T, qk_pad), BF16),
                   jax.ShapeDtypeStruct((T, ssm_width), F32)],
        compiler_params=cparams(dimension_semantics=("parallel",)),
        name="inproj",
    )(x.reshape(T, D), pos_f, _row(ln_in_g), _row(ln_in_b), w1, _row(q_norm_g[l]), _row(kv_norm_g[l]),
      wq_main, wq_rot, wk, wv, freq)

    tq = min(256, S)
    nq = S // tq
    att = pl.pallas_call(
        functools.partial(_attn_kernel, tq=tq, heads=MLA_HEADS, n_qtiles=nq),
        grid=(B, nq),
        in_specs=[pl.BlockSpec((tq, qk_pad), lambda b, i: (b * nq + i, 0)),
                  pl.BlockSpec((S, qk_pad), lambda b, i: (b, 0)),
                  pl.BlockSpec((S, qk_pad), lambda b, i: (b, 0))],
        out_specs=pl.BlockSpec((tq, mla_width), lambda b, i: (b * nq + i, 0)),
        out_shape=jax.ShapeDtypeStruct((T, mla_width), BF16),
        compiler_params=cparams(dimension_semantics=("parallel", "arbitrary")),
        name="attention",
    )(q, k, v)

    lam = lax.complex(jnp.minimum(lambda_re[l].astype(F32), -1e-4), lambda_im[l].astype(F32))
    step = jnp.exp(log_step[l].astype(F32))[:, None]
    lam_bar = jnp.exp(lam * step)
    b_bar = ((lam_bar - 1.0) / lam)[..., None] * lax.complex(b_re[l].astype(F32), b_im[l].astype(F32))
    n_slab = ssm_width // LANES
    g_per_slab = n_groups // n_slab
    eye = jnp.eye(n_groups, dtype=F32)

    def expand_in(bpart):
        return jnp.einsum('gpc,gh->gchp', bpart, eye).reshape(ssm_width, n_state)

    def expand_out(cpart):
        return jnp.einsum('gcp,gh->gphc', cpart, eye).reshape(n_state, ssm_width)

    slab = n_state // n_slab
    win_re, win_im = expand_in(jnp.real(b_bar)), expand_in(jnp.imag(b_bar))
    win = jnp.stack([jnp.concatenate([win_re[j * LANES:(j + 1) * LANES, j * slab:(j + 1) * slab],
                                      win_im[j * LANES:(j + 1) * LANES, j * slab:(j + 1) * slab]], axis=1)
                     for j in range(n_slab)]).astype(BF16)
    wc_re, wc_im = expand_out(c_re[l].astype(F32)), expand_out(-c_im[l].astype(F32))
    cre = jnp.stack([wc_re[j * slab:(j + 1) * slab, j * LANES:(j + 1) * LANES] for j in range(n_slab)]).astype(BF16)
    cim = jnp.stack([wc_im[j * slab:(j + 1) * slab, j * LANES:(j + 1) * LANES] for j in range(n_slab)]).astype(BF16)
    a_re = jnp.real(lam_bar).reshape(1, n_state)
    a_im = jnp.imag(lam_bar).reshape(1, n_state)

    lt = min(64, S)
    ssm = pl.pallas_call(
        functools.partial(_s5_kernel, batch=B, lt=lt, n_state=n_state),
        grid=(S // lt,),
        in_specs=[pl.BlockSpec((B, lt, ssm_width), lambda t: (0, t, 0)),
                  _const_spec(win.shape), _const_spec((1, n_state)), _const_spec((1, n_state)),
                  _const_spec(cre.shape), _const_spec(cim.shape), _const_spec((1, ssm_width)),
                  _const_spec((ssm_width, ssm_width)), _const_spec((1, ssm_width))],
        out_specs=pl.BlockSpec((B, lt, ssm_width), lambda t: (0, t, 0)),
        out_shape=jax.ShapeDtypeStruct((B, S, ssm_width), BF16),
        scratch_shapes=[pltpu.VMEM((2 * n_state // LANES, B * lt, LANES), F32),
                        pltpu.VMEM((n_state // LANES, B, LANES), F32),
                        pltpu.VMEM((n_state // LANES, B, LANES), F32),
                        pltpu.VMEM((ssm_width // LANES, B * lt, LANES), F32),
                        pltpu.VMEM((B * lt, ssm_width), F32)],
        compiler_params=cparams(dimension_semantics=("arbitrary",)),
        name="s5",
    )(u.reshape(B, S, ssm_width), win, a_re, a_im, cre, cim, _row(d_skip[l]),
      w_glu[l].astype(BF16), _row(b_glu[l]))

    wo = w_o[l].astype(BF16)
    wr_t = w_router[l].T.astype(F32)
    wr_hi = wr_t.astype(BF16)
    wr_lo = (wr_t - wr_hi.astype(F32)).astype(BF16)
    half = D // 2
    n_slabs = MOE_SLABS
    ts = T // n_slabs
    nt = ts // tm
    ssm2 = ssm.reshape(T, ssm_width)
    rbias = router_bias[l].astype(F32).reshape(N_EXPERTS, 1)
    kt = lambda dt: jax.ShapeDtypeStruct((TOP_K, ts), dt)
    k_spec = pl.BlockSpec((TOP_K, tm), lambda i: (0, i))

    def route(s):
        tok_s = lambda width: pl.BlockSpec((tm, width), lambda i: (i + s * nt, 0))
        return pl.pallas_call(
            _mix_kernel,
            grid=(nt,),
            in_specs=[tok_s(mla_width), tok_s(ssm_width), tok_s(D), _const_spec((1, mla_width)),
                      _const_spec((1, ssm_width)), _const_spec((mla_width, D)),
                      _const_spec((ssm_width, D)), _const_spec((1, D)), _const_spec((1, D)),
                      _const_spec((N_EXPERTS, D)), _const_spec((N_EXPERTS, D)),
                      _const_spec((N_EXPERTS, 1))],
            out_specs=[tok(D), tok(half), k_spec, tok(TOP_K), k_spec, _const_spec((N_EXPERTS, LANES))],
            out_shape=[jax.ShapeDtypeStruct((ts, D), F32), jax.ShapeDtypeStruct((ts, half), jnp.int32),
                       kt(jnp.int32), jax.ShapeDtypeStruct((ts, TOP_K), F32), kt(jnp.int32),
                       jax.ShapeDtypeStruct((N_EXPERTS, LANES), F32)],
            scratch_shapes=[pltpu.VMEM((N_EXPERTS, 1), F32), pltpu.VMEM((tm, tm), BF16)],
            compiler_params=cparams(dimension_semantics=("arbitrary",)),
            name="mix_router",
        )(att, ssm2, xn, _row(attn_out_g[l]), _row(ssm_out_g[l]), wo[:mla_width], wo[mla_width:],
          _row(ln1_g[l]), _row(ln1_b[l]), wr_hi, wr_lo, rbias)

    tr = EXPERT_ROW_TILE
    n_tiles = (ts * TOP_K) // tr + N_EXPERTS
    n_rows = n_tiles * tr
    tp = min(2048, ts)

    def dispatch(x1p, idx_k, rank_k, counts):
        cnt = counts[:, 0].astype(jnp.int32)
        tiles_e = (cnt + tr - 1) // tr
        tile_end = jnp.cumsum(tiles_e)
        tile_start = tile_end - tiles_e
        tile_ids = jnp.arange(n_tiles, dtype=jnp.int32)
        tile_expert = jnp.sum((tile_end[None, :] <= tile_ids[:, None]).astype(jnp.int32), axis=1)
        tile_expert = jnp.minimum(tile_expert, N_EXPERTS - 1)
        owner = (tile_start[None, :] <= tile_ids[:, None]) & (tile_ids[:, None] < tile_end[None, :])
        left = jnp.sum(jnp.where(owner, cnt[None, :] - (tile_ids[:, None] - tile_start[None, :]) * tr, 0),
                       axis=1)
        tile_valid = jnp.clip(left, 0, tr).astype(jnp.int32)
        pos = pl.pallas_call(
            _positions_kernel,
            grid_spec=pltpu.PrefetchScalarGridSpec(
                num_scalar_prefetch=1, grid=(ts // tp,),
                in_specs=[pl.BlockSpec((TOP_K, tp), lambda i, off: (0, i)),
                          pl.BlockSpec((TOP_K, tp), lambda i, off: (0, i))],
                out_specs=pl.BlockSpec((TOP_K, tp), lambda i, off: (0, i))),
            out_shape=kt(jnp.int32),
            name="positions",
        )((tile_start * tr).astype(jnp.int32), idx_k, rank_k)
        return _sc_scatter_rows(x1p, pos, n_rows), pos, tile_expert, tile_valid

    wg, wu, wd = w_gate[l], w_up[l], w_down[l]
    ff = wg.shape[-1]

    def experts(xs, tile_expert, tile_valid):
        return pl.pallas_call(
            _experts_kernel,
            grid_spec=pltpu.PrefetchScalarGridSpec(
                num_scalar_prefetch=2, grid=(n_tiles,),
                in_specs=[pl.BlockSpec((tr, half), lambda i, te, tv: (i, 0)),
                          pl.BlockSpec((1, D, ff), lambda i, te, tv: (te[i], 0, 0)),
                          pl.BlockSpec((1, D, ff), lambda i, te, tv: (te[i], 0, 0)),
                          pl.BlockSpec((1, ff, D), lambda i, te, tv: (te[i], 0, 0))],
                out_specs=pl.BlockSpec((tr, half), lambda i, te, tv: (i, 0)),
                scratch_shapes=[pltpu.VMEM((D, ff), BF16), pltpu.VMEM((D, ff), BF16),
                                pltpu.VMEM((ff, D), BF16)]),
            out_shape=jax.ShapeDtypeStruct((n_rows, half), jnp.int32),
            compiler_params=cparams(dimension_semantics=("arbitrary",)),
            name="experts",
        )(tile_expert, tile_valid, xs, wg, wu, wd)

    shared = (ws_gate[l].astype(BF16), ws_up[l].astype(BF16), ws_down[l].astype(BF16))

    def combine(s, out_so_far, yg, gate_k, x1):
        specs = [pl.BlockSpec((TOP_K, tm, half), lambda i: (0, i, 0)),
                 pl.BlockSpec((tm, TOP_K), lambda i: (i, 0)), tok(D),
                 _const_spec((D, ff)), _const_spec((D, ff)), _const_spec((ff, D)),
                 _const_spec((1, D)), _const_spec((1, D))]
        args = (yg.reshape(TOP_K, ts, half), gate_k, x1, *shared, _row(ln2_g[l]), _row(ln2_b[l]))
        body, aliases = _combine_kernel, {}
        if out_so_far is not None:
            specs = [pl.BlockSpec(memory_space=pl.ANY)] + specs
            args = (out_so_far,) + args
            body, aliases = _combine_into_kernel, {0: 0}
        return pl.pallas_call(
            body,
            grid=(nt,),
            in_specs=specs,
            out_specs=pl.BlockSpec((tm, D), lambda i: (i + s * nt, 0)),
            out_shape=jax.ShapeDtypeStruct((T, D), F32),
            input_output_aliases=aliases,
            compiler_params=cparams(dimension_semantics=("parallel",)),
            name="combine",
        )(*args)

    routed, moved = [], []
    for s in range(n_slabs):
        x1, x1p, idx_k, gate_k, rank_k, counts = route(s)
        routed.append((x1, gate_k))
        moved.append(dispatch(x1p, idx_k, rank_k, counts))
    gathered = []
    for xs, pos, tile_expert, tile_valid in moved:
        gathered.append(_sc_gather_rows(experts(xs, tile_expert, tile_valid), pos.reshape(TOP_K * ts)))
    out = None
    for s in range(n_slabs):
        out = combine(s, out, gathered[s], routed[s][1], routed[s][0])
    return out.reshape(B, S, D)
```

```python
import functools

import jax
import jax.numpy as jnp
from jax import lax
from jax.experimental import pallas as pl
from jax.experimental.pallas import tpu as pltpu
from jax.experimental.pallas import tpu_sc as plsc

CHUNK = 64
MLA_HEADS = 8
QK_NOPE = 64
QK_ROPE = 32
V_DIM = 64
Q_RANK = 256
KV_RANK = 128
ROPE_THETA = 10000.0
SSM_GROUP = 16
SSM_STATE = 64
N_EXPERTS = 64
TOP_K = 8
N_GROUPS = 8
TOP_GROUPS = 4
ROUTED_SCALE = 2.5
DEPTH = 1
ALPHA = (2.0 * DEPTH) ** 0.25
EPS = 1e-5
LOG2_E = 1.4426950408889634

LANES = 128
HEAD_PAD = LANES
VMEM_LIMIT = 56 * 1024 * 1024
EXPERT_ROW_TILE = 1024
MOE_SLABS = 2

BF16 = jnp.bfloat16
F32 = jnp.float32
NT_DIMS = (((1,), (1,)), ((), ()))


def _dot(a, b):
    return jnp.dot(a, b, preferred_element_type=F32)


def _layer_norm(x, g, b):
    mu = jnp.mean(x, axis=-1, keepdims=True)
    xc = x - mu
    var = jnp.mean(xc * xc, axis=-1, keepdims=True)
    return xc * lax.rsqrt(var + EPS) * g + b


def _rms_norm(x, g):
    return x * lax.rsqrt(jnp.mean(x * x, axis=-1, keepdims=True) + EPS) * g


def _inproj_kernel(x_ref, pos_ref, lng_ref, lnb_ref, w1_ref, qg_ref, kvg_ref,
                   wq_ref, wqr_ref, wk_ref, wv_ref, freq_ref,
                   xn_ref, q_ref, k_ref, v_ref, u_ref, *, ssm_width):
    xn = _layer_norm(x_ref[...], lng_ref[...], lnb_ref[...])
    xn_ref[...] = xn
    h = _dot(xn.astype(BF16), w1_ref[...])
    o1 = Q_RANK
    o2 = o1 + KV_RANK
    o3 = o2 + ssm_width
    o4 = o3 + HEAD_PAD
    cq = h[:, :o1]
    ckv = h[:, o1:o2]
    u_ref[...] = h[:, o2:o3]
    kr_raw = h[:, o3:o4]
    kr_rot = h[:, o4:o4 + HEAD_PAD]
    cqn = _rms_norm(cq, qg_ref[...]).astype(BF16)
    ckvn = _rms_norm(ckv, kvg_ref[...]).astype(BF16)

    ang = pos_ref[...] * freq_ref[...]
    c = jnp.cos(ang)
    s = jnp.sin(ang)
    lane = lax.broadcasted_iota(jnp.int32, (1, HEAD_PAD), 1)
    scale = (QK_NOPE + QK_ROPE) ** -0.5 * LOG2_E
    is_rope = (lane >= QK_NOPE) & (lane < QK_NOPE + QK_ROPE)
    cos1 = jnp.where(lane < QK_NOPE, 1.0, jnp.where(is_rope, c, 0.0)) * scale
    sin1 = jnp.where(is_rope, s, 0.0) * scale
    cos_t = jnp.concatenate([cos1] * MLA_HEADS, axis=1)
    sin_t = jnp.concatenate([sin1] * MLA_HEADS, axis=1)
    q = _dot(cqn, wq_ref[...]) * cos_t + _dot(cqn, wqr_ref[...]) * sin_t
    q_ref[...] = q.astype(BF16)

    kr = kr_raw * c + kr_rot * s
    k = _dot(ckvn, wk_ref[...]) + jnp.concatenate([kr] * MLA_HEADS, axis=1)
    k_ref[...] = k.astype(BF16)
    ones_col = jnp.concatenate([jnp.where(lane == V_DIM, 1.0, 0.0)] * MLA_HEADS, axis=1)
    v_ref[...] = (_dot(ckvn, wv_ref[...]) + ones_col).astype(BF16)


def _attn_kernel(q_ref, k_ref, v_ref, o_ref, *, tq, heads, n_qtiles):
    qi = pl.program_id(1)
    row_chunk = lax.broadcasted_iota(jnp.int32, (tq, tq), 0) // CHUNK
    col_chunk = lax.broadcasted_iota(jnp.int32, (tq, tq), 1) // CHUNK
    diag_mask = row_chunk >= col_chunk

    def tile(n_blocks):
        keys = n_blocks * tq
        cols = [slice(h * HEAD_PAD, (h + 1) * HEAD_PAD) for h in range(heads)]

        def mask_diag(s):
            s_diag = jnp.where(diag_mask, s[:, keys - tq:], -jnp.inf)
            return s_diag if n_blocks == 1 else jnp.concatenate([s[:, :keys - tq], s_diag], axis=1)

        ss = [lax.dot_general(q_ref[:, c], k_ref[:keys, c], NT_DIMS, preferred_element_type=F32)
              for c in cols]
        ss = [mask_diag(s) for s in ss]
        ms = [jnp.max(s, axis=-1, keepdims=True) for s in ss]
        ps = [jnp.exp2(s - m).astype(BF16) for s, m in zip(ss, ms)]
        accs = [_dot(p, v_ref[:keys, c]) for p, c in zip(ps, cols)]
        outs = [acc[:, :V_DIM] / acc[:, V_DIM:V_DIM + 1] for acc in accs]
        o_ref[...] = jnp.concatenate(outs, axis=1).astype(o_ref.dtype)

    for c in range(n_qtiles):
        pl.when(qi == c)(functools.partial(tile, c + 1))


def _s5_kernel(u_ref, win_ref, are_ref, aim_ref, cre_ref, cim_ref, dskip_ref,
               wglu_ref, bglu_ref, o_ref, vx_ref, hre_ref, him_ref, io_ref, utm_ref, *,
               batch, lt, n_state):
    ti = pl.program_id(0)

    @pl.when(ti == 0)
    def _():
        hre_ref[...] = jnp.zeros_like(hre_ref)
        him_ref[...] = jnp.zeros_like(him_ref)

    width = u_ref.shape[-1]
    n_slab = width // LANES
    n_tiles = n_state // LANES
    slab_tiles = n_tiles // n_slab
    for b in range(batch):
        for c in range(n_slab):
            io_ref[c, b * lt:(b + 1) * lt, :] = u_ref[b, :, c * LANES:(c + 1) * LANES]

    def slab_tiles_of(j):
        return range(j * slab_tiles, (j + 1) * slab_tiles)

    def expand(j):
        for t in range(lt):
            utm_ref[t * batch:(t + 1) * batch, j * LANES:(j + 1) * LANES] = (
                io_ref[j, pl.ds(t, batch, stride=lt), :])
        ub = utm_ref[:, j * LANES:(j + 1) * LANES].astype(BF16)
        vj = _dot(ub, win_ref[j])
        for i, c in enumerate(slab_tiles_of(j)):
            vx_ref[c] = vj[:, i * LANES:(i + 1) * LANES]
            vx_ref[n_tiles + c] = vj[:, (slab_tiles + i) * LANES:(slab_tiles + i + 1) * LANES]

    def scan(j):
        tiles = slab_tiles_of(j)
        ar = [jnp.broadcast_to(are_ref[:, c * LANES:(c + 1) * LANES], (batch, LANES)) for c in tiles]
        ai = [jnp.broadcast_to(aim_ref[:, c * LANES:(c + 1) * LANES], (batch, LANES)) for c in tiles]
        hr = [hre_ref[c] for c in tiles]
        hi = [him_ref[c] for c in tiles]
        for t in range(lt):
            rows = slice(t * batch, (t + 1) * batch)
            for n, c in enumerate(tiles):
                nr = ar[n] * hr[n] - ai[n] * hi[n] + vx_ref[c, rows, :]
                ni = ar[n] * hi[n] + ai[n] * hr[n] + vx_ref[n_tiles + c, rows, :]
                vx_ref[c, rows, :] = nr
                vx_ref[n_tiles + c, rows, :] = ni
                hr[n], hi[n] = nr, ni
        for n, c in enumerate(tiles):
            hre_ref[c] = hr[n]
            him_ref[c] = hi[n]

    def project(j):
        tiles = slab_tiles_of(j)
        xr = jnp.concatenate([vx_ref[c].astype(BF16) for c in tiles], axis=1)
        xi = jnp.concatenate([vx_ref[n_tiles + c].astype(BF16) for c in tiles], axis=1)
        return _dot(xr, cre_ref[j]) + _dot(xi, cim_ref[j])

    ys = [None] * n_slab
    for j in range(n_slab + 2):
        if j < n_slab:
            expand(j)
        if 1 <= j <= n_slab:
            scan(j - 1)
        if j >= 2:
            ys[j - 2] = project(j - 2)
    y = jnp.concatenate(ys, axis=1) + dskip_ref[...] * utm_ref[...]
    y = jax.nn.gelu(y)
    z = _dot(y.astype(BF16), wglu_ref[...]) + bglu_ref[...]
    out = y * jax.nn.sigmoid(z)
    for c in range(n_slab):
        io_ref[c] = out[:, c * LANES:(c + 1) * LANES]
    for b in range(batch):
        for c in range(n_slab):
            o_ref[b, :, c * LANES:(c + 1) * LANES] = io_ref[c, pl.ds(b, lt, stride=batch), :].astype(o_ref.dtype)


def _router_gates(logits_t, rbias):
    n_exp, tm = logits_t.shape
    per_group = n_exp // N_GROUPS
    scores = jax.nn.sigmoid(logits_t)
    sel = scores + rbias
    neg_inf = -jnp.inf
    sub_iota = lax.broadcasted_iota(jnp.int32, (per_group, tm), 0).astype(F32)
    group_score = []
    for g in range(N_GROUPS):
        sg = sel[g * per_group:(g + 1) * per_group, :]
        m1 = jnp.max(sg, axis=0, keepdims=True)
        first = jnp.min(jnp.where(sg == m1, sub_iota, float(per_group)), axis=0, keepdims=True)
        m2 = jnp.max(jnp.where(sub_iota == first, neg_inf, sg), axis=0, keepdims=True)
        group_score.append(m1 + m2)
    masked = []
    for g in range(N_GROUPS):
        rank = jnp.zeros((1, tm), F32)
        for g2 in range(N_GROUPS):
            if g2 == g:
                continue
            ahead = (group_score[g2] >= group_score[g]) if g2 < g else (group_score[g2] > group_score[g])
            rank = rank + jnp.where(ahead, 1.0, 0.0)
        keep = rank < float(TOP_GROUPS)
        masked.append(jnp.where(keep, sel[g * per_group:(g + 1) * per_group, :], neg_inf))
    cur = jnp.concatenate(masked, axis=0)
    iota = lax.broadcasted_iota(jnp.int32, (n_exp, tm), 0).astype(F32)
    chosen = jnp.zeros((n_exp, tm), F32)
    picks, weights = [], []
    for _ in range(TOP_K):
        m = jnp.max(cur, axis=0, keepdims=True)
        idx = jnp.min(jnp.where(cur == m, iota, float(n_exp)), axis=0, keepdims=True)
        pick = iota == idx
        chosen = jnp.where(pick, 1.0, chosen)
        cur = jnp.where(pick, neg_inf, cur)
        picks.append(idx)
        weights.append(jnp.sum(jnp.where(pick, scores, 0.0), axis=0, keepdims=True))
    idx_k = jnp.concatenate(picks, axis=0)
    w_k = jnp.concatenate(weights, axis=0)
    gate_k = w_k / jnp.sum(w_k, axis=0, keepdims=True) * ROUTED_SCALE
    return idx_k, gate_k, chosen


def _pack_bf16_pairs(x):
    n = x.shape[1] // 2
    hi = lax.bitcast_convert_type(x[:, :n].astype(BF16).astype(F32), jnp.int32)
    lo = lax.bitcast_convert_type(x[:, n:].astype(BF16).astype(F32), jnp.int32)
    return hi | lax.shift_right_logical(lo, 16)


def _unpack_bf16_pairs(p):
    hi = lax.bitcast_convert_type(p & jnp.int32(-65536), F32).astype(BF16)
    lo = lax.bitcast_convert_type(lax.shift_left(p, 16), F32).astype(BF16)
    return hi, lo


def _mix_kernel(att_ref, ssm_ref, xn_ref, ag_ref, sg_ref, woa_ref, wos_ref,
                g_ref, b_ref, wrh_ref, wrl_ref, rb_ref,
                x1_ref, x1p_ref, idx_ref, gate_ref, rank_ref, cnt_ref, carry_ref, before_ref):
    @pl.when(pl.program_id(0) == 0)
    def _():
        carry_ref[...] = jnp.zeros_like(carry_ref)
        tm = before_ref.shape[0]
        before = (lax.broadcasted_iota(jnp.int32, (tm, tm), 0)
                  < lax.broadcasted_iota(jnp.int32, (tm, tm), 1))
        before_ref[...] = jnp.where(before, 1.0, 0.0).astype(BF16)

    an = _rms_norm(att_ref[...].astype(F32), ag_ref[...]).astype(BF16)
    sn = _rms_norm(ssm_ref[...].astype(F32), sg_ref[...]).astype(BF16)
    mix = _dot(an, woa_ref[...]) + _dot(sn, wos_ref[...])
    x1 = _layer_norm(ALPHA * xn_ref[...] + mix, g_ref[...], b_ref[...])
    x1_ref[...] = x1
    x1p_ref[...] = _pack_bf16_pairs(x1)
    x_hi = x1.astype(BF16)
    x_lo = (x1 - x_hi.astype(F32)).astype(BF16)
    dg = functools.partial(lax.dot_general, dimension_numbers=NT_DIMS, preferred_element_type=F32)
    logits_t = dg(wrh_ref[...], x_hi) + dg(wrl_ref[...], x_hi) + dg(wrh_ref[...], x_lo)
    idx_k, gate_k, chosen = _router_gates(logits_t, rb_ref[...])
    idx_ref[...] = idx_k.astype(jnp.int32)
    gate_ref[...] = gate_k.T

    n_exp, tm = chosen.shape
    excl = _dot(chosen.astype(BF16), before_ref[...])
    rank_full = carry_ref[...] + excl
    iota = lax.broadcasted_iota(jnp.int32, (n_exp, tm), 0).astype(F32)
    ranks = [jnp.sum(jnp.where(iota == idx_k[k:k + 1, :], rank_full, 0.0), axis=0, keepdims=True)
             for k in range(TOP_K)]
    rank_ref[...] = jnp.concatenate(ranks, axis=0).astype(jnp.int32)
    total = carry_ref[...] + jnp.sum(chosen, axis=1, keepdims=True)
    carry_ref[...] = total
    cnt_ref[...] = jnp.broadcast_to(total, cnt_ref.shape)


def _positions_kernel(off_ref, idx_ref, rank_ref, pos_ref):
    idx = idx_ref[...]
    base = jnp.zeros(idx.shape, jnp.int32)
    for e in range(N_EXPERTS):
        base = jnp.where(idx == e, off_ref[e], base)
    pos_ref[...] = rank_ref[...] + base


def _swiglu(x_hi, x_lo, wg_ref, wu_ref, wd_ref):
    half = x_hi.shape[1]
    hg = _dot(x_hi, wg_ref[:half, :]) + _dot(x_lo, wg_ref[half:, :])
    hu = _dot(x_hi, wu_ref[:half, :]) + _dot(x_lo, wu_ref[half:, :])
    h = jax.nn.silu(hg) * hu
    return _dot(h.astype(BF16), wd_ref[...])


def _experts_kernel(te_ref, valid_ref, xs_ref, wg_ref, wu_ref, wd_ref, ys_ref,
                    wgb_ref, wub_ref, wdb_ref):
    i = pl.program_id(0)
    valid = valid_ref[i]

    @pl.when((i == 0) | (te_ref[i] != te_ref[jnp.maximum(i - 1, 0)]))
    def _():
        wgb_ref[...] = wg_ref[0].astype(BF16)
        wub_ref[...] = wu_ref[0].astype(BF16)
        wdb_ref[...] = wd_ref[0].astype(BF16)

    @pl.when(valid > 0)
    def _():
        rows = lax.broadcasted_iota(jnp.int32, xs_ref.shape, 0)
        x_hi, x_lo = _unpack_bf16_pairs(jnp.where(rows < valid, xs_ref[...], 0))
        ys_ref[...] = _pack_bf16_pairs(_swiglu(x_hi, x_lo, wgb_ref, wub_ref, wdb_ref))

    @pl.when(valid == 0)
    def _():
        ys_ref[...] = jnp.zeros_like(ys_ref)


def _combine_kernel(yg_ref, gate_ref, x1_ref, wsg_ref, wsu_ref, wsd_ref, g_ref, b_ref, o_ref):
    x1 = x1_ref[...]
    half = x1.shape[1] // 2
    acc = _swiglu(x1[:, :half].astype(BF16), x1[:, half:].astype(BF16), wsg_ref, wsu_ref, wsd_ref)
    gates = gate_ref[...]
    acc_hi, acc_lo = acc[:, :half], acc[:, half:]
    for k in range(TOP_K):
        y_hi, y_lo = _unpack_bf16_pairs(yg_ref[k])
        gk = gates[:, k:k + 1]
        acc_hi = acc_hi + gk * y_hi.astype(F32)
        acc_lo = acc_lo + gk * y_lo.astype(F32)
    ffn = jnp.concatenate([acc_hi, acc_lo], axis=1)
    o_ref[...] = _layer_norm(ALPHA * x1 + ffn, g_ref[...], b_ref[...])


def _combine_into_kernel(prev_ref, *refs):
    del prev_ref
    _combine_kernel(*refs)


SC_CORES = 2
SC_SUBCORES = 16
SC_WORKERS = SC_CORES * SC_SUBCORES
SC_CHUNK = 64


def _sc_mesh():
    return plsc.VectorSubcoreMesh(core_axis_name="c", subcore_axis_name="s")


def _sc_gather_rows(table, idx):
    n = idx.shape[0]
    d = table.shape[1]
    per_w = n // SC_WORKERS
    n_ch = per_w // SC_CHUNK

    @functools.partial(
        pl.kernel, mesh=_sc_mesh(),
        out_type=jax.ShapeDtypeStruct((n, d), table.dtype),
        scratch_types=[pltpu.VMEM((n_ch, SC_CHUNK), jnp.int32),
                       pltpu.VMEM((2, SC_CHUNK, d), table.dtype),
                       pltpu.SemaphoreType.DMA((2,)),
                       pltpu.SemaphoreType.DMA((2,))],
    )
    def k(table_hbm, idx_hbm, out_hbm, idx_v, buf, gsem, osem):
        wid = lax.axis_index("s") * SC_CORES + lax.axis_index("c")
        base = wid * per_w
        pltpu.sync_copy(idx_hbm.at[wid], idx_v)

        def gather(c, b):
            return pltpu.make_async_copy(table_hbm.at[idx_v.at[c]], buf.at[b], gsem.at[b])

        def put(c, b):
            return pltpu.make_async_copy(buf.at[b], out_hbm.at[pl.ds(base + c * SC_CHUNK, SC_CHUNK)],
                                         osem.at[b])

        gather(0, 0).start()

        @pl.loop(0, n_ch, step=2)
        def _(c):
            for b in range(2):
                cc = c + b
                gather(cc, b).wait()

                @pl.when(cc + 1 < n_ch)
                def _():
                    @pl.when(cc >= 1)
                    def _():
                        put(cc - 1, 1 - b).wait()
                    gather(cc + 1, 1 - b).start()

                put(cc, b).start()

        put(n_ch - 2, 0).wait()
        put(n_ch - 1, 1).wait()

    return k(table, idx.reshape(SC_WORKERS, n_ch, SC_CHUNK))


def _sc_scatter_rows(x, pos, n_out):
    t, d = x.shape
    kk = pos.shape[0]
    per_w = t // SC_WORKERS
    n_ch = per_w // SC_CHUNK
    pos_w = pos.reshape(kk, SC_WORKERS, n_ch, SC_CHUNK).transpose(1, 2, 0, 3)
    pos_w = pos_w.reshape(SC_WORKERS, n_ch * kk, SC_CHUNK)

    @functools.partial(
        pl.kernel, mesh=_sc_mesh(),
        out_type=jax.ShapeDtypeStruct((n_out, d), x.dtype),
        scratch_types=[pltpu.VMEM((n_ch * kk, SC_CHUNK), jnp.int32),
                       pltpu.VMEM((2, SC_CHUNK, d), x.dtype),
                       pltpu.SemaphoreType.DMA((2,)),
                       pltpu.SemaphoreType.DMA((2,))],
    )
    def k(x_hbm, pos_hbm, out_hbm, idx_v, buf, isem, osem):
        wid = lax.axis_index("s") * SC_CORES + lax.axis_index("c")
        base = wid * per_w
        pltpu.sync_copy(pos_hbm.at[wid], idx_v)

        def get(c, b):
            return pltpu.make_async_copy(x_hbm.at[pl.ds(base + c * SC_CHUNK, SC_CHUNK)], buf.at[b],
                                         isem.at[b])

        def put(c, j, b):
            return pltpu.make_async_copy(buf.at[b], out_hbm.at[idx_v.at[c * kk + j]], osem.at[b])

        get(0, 0).start()

        @pl.loop(0, n_ch, step=2)
        def _(c):
            for b in range(2):
                cc = c + b
                get(cc, b).wait()

                @pl.when(cc + 1 < n_ch)
                def _():
                    @pl.when(cc >= 1)
                    def _():
                        for j in range(kk):
                            put(cc - 1, j, 1 - b).wait()
                    get(cc + 1, 1 - b).start()

                for j in range(kk):
                    put(cc, j, b).start()

        for j in range(kk):
            put(n_ch - 2, j, 0).wait()
        for j in range(kk):
            put(n_ch - 1, j, 1).wait()

    return k(x, pos_w)


def _row(v):
    return v.reshape(1, -1).astype(F32)


def _const_spec(shape):
    nd = len(shape)
    return pl.BlockSpec(shape, lambda *_: (0,) * nd)


def _pad_heads(w, width):
    r, h, _ = w.shape
    return jnp.pad(w, ((0, 0), (0, 0), (0, HEAD_PAD - width))).reshape(r, h * HEAD_PAD)


def _half_rotate(w):
    half = QK_ROPE // 2
    return jnp.concatenate([-w[..., half:], w[..., :half]], axis=-1)


def kernel(x, positions, ln_in_g, ln_in_b, w_in, q_norm_g, kv_norm_g, w_uq, w_ukv, lambda_re, lambda_im, log_step, b_re, b_im, c_re, c_im, d_skip, w_glu, b_glu, attn_out_g, ssm_out_g, w_o, ln1_g, ln1_b, w_router, router_bias, w_gate, w_up, w_down, ws_gate, ws_up, ws_down, ln2_g, ln2_b):
    B, S, D = x.shape
    T = B * S
    l = 0
    ssm_width = w_glu.shape[-1]
    n_groups = ssm_width // SSM_GROUP
    n_state = n_groups * SSM_STATE
    mla_width = MLA_HEADS * V_DIM
    qk_pad = MLA_HEADS * HEAD_PAD
    cparams = functools.partial(pltpu.CompilerParams, vmem_limit_bytes=VMEM_LIMIT)

    s1, s2, s3 = Q_RANK, Q_RANK + KV_RANK, Q_RANK + KV_RANK + QK_ROPE
    wi = w_in[l]
    w_kr = wi[:, s2:s3]
    pad_rope = lambda w: jnp.pad(w, ((0, 0), (QK_NOPE, HEAD_PAD - QK_NOPE - QK_ROPE)))
    w1 = jnp.concatenate([wi[:, :s2], wi[:, s3:], pad_rope(w_kr), pad_rope(_half_rotate(w_kr))],
                         axis=1).astype(BF16)
    wq = w_uq[l]
    zeros_nope = jnp.zeros(wq.shape[:2] + (QK_NOPE,), wq.dtype)
    wq_main = _pad_heads(wq, QK_NOPE + QK_ROPE).astype(BF16)
    wq_rot = _pad_heads(jnp.concatenate([zeros_nope, _half_rotate(wq[..., QK_NOPE:])], axis=-1),
                        QK_NOPE + QK_ROPE).astype(BF16)
    wkv = w_ukv[l]
    wk = _pad_heads(wkv[..., :QK_NOPE], QK_NOPE).astype(BF16)
    wv = _pad_heads(wkv[..., QK_NOPE:], V_DIM).astype(BF16)
    half = QK_ROPE // 2
    inv_freq = ROPE_THETA ** (-jnp.arange(half, dtype=F32) / half)
    freq = jnp.pad(jnp.concatenate([inv_freq, inv_freq]),
                   (QK_NOPE, HEAD_PAD - QK_NOPE - QK_ROPE)).reshape(1, HEAD_PAD)
    pos_f = positions.astype(F32).reshape(T, 1)

    tm = min(512, T)
    w1_cols = w1.shape[1]
    tok = lambda width: pl.BlockSpec((tm, width), lambda i: (i, 0))
    xn, q, k, v, u = pl.pallas_call(
        functools.partial(_inproj_kernel, ssm_width=ssm_width),
        grid=(T // tm,),
        in_specs=[tok(D), tok(1), _const_spec((1, D)), _const_spec((1, D)),
                  _const_spec((D, w1_cols)), _const_spec((1, Q_RANK)), _const_spec((1, KV_RANK)),
                  _const_spec((Q_RANK, qk_pad)), _const_spec((Q_RANK, qk_pad)),
                  _const_spec((KV_RANK, qk_pad)), _const_spec((KV_RANK, qk_pad)),
                  _const_spec((1, HEAD_PAD))],
        out_specs=[tok(D), tok(qk_pad), tok(qk_pad), tok(qk_pad), tok(ssm_width)],
        out_shape=[jax.ShapeDtypeStruct((T, D), F32), jax.ShapeDtypeStruct((T, qk_pad), BF16),
                   jax.ShapeDtypeStruct((T, qk_pad), BF16), jax.ShapeDtypeStruct((T, qk_pad), BF16),
                   jax.ShapeDtypeStruct((T, ssm_width), F32)],
        compiler_params=cparams(dimension_semantics=("parallel",)),
        name="inproj",
    )(x.reshape(T, D), pos_f, _row(ln_in_g), _row(ln_in_b), w1, _row(q_norm_g[l]), _row(kv_norm_g[l]),
      wq_main, wq_rot, wk, wv, freq)

    tq = min(256, S)
    nq = S // tq
    att = pl.pallas_call(
        functools.partial(_attn_kernel, tq=tq, heads=MLA_HEADS, n_qtiles=nq),
        grid=(B, nq),
        in_specs=[pl.BlockSpec((tq, qk_pad), lambda b, i: (b * nq + i, 0)),
                  pl.BlockSpec((S, qk_pad), lambda b, i: (b, 0)),
                  pl.BlockSpec((S, qk_pad), lambda b, i: (b, 0))],
        out_specs=pl.BlockSpec((tq, mla_width), lambda b, i: (b * nq + i, 0)),
        out_shape=jax.ShapeDtypeStruct((T, mla_width), BF16),
        compiler_params=cparams(dimension_semantics=("parallel", "arbitrary")),
        name="attention",
    )(q, k, v)

    lam = lax.complex(jnp.minimum(lambda_re[l].astype(F32), -1e-4), lambda_im[l].astype(F32))
    step = jnp.exp(log_step[l].astype(F32))[:, None]
    lam_bar = jnp.exp(lam * step)
    b_bar = ((lam_bar - 1.0) / lam)[..., None] * lax.complex(b_re[l].astype(F32), b_im[l].astype(F32))
    n_slab = ssm_width // LANES
    g_per_slab = n_groups // n_slab
    eye = jnp.eye(n_groups, dtype=F32)

    def expand_in(bpart):
        return jnp.einsum('gpc,gh->gchp', bpart, eye).reshape(ssm_width, n_state)

    def expand_out(cpart):
        return jnp.einsum('gcp,gh->gphc', cpart, eye).reshape(n_state, ssm_width)

    slab = n_state // n_slab
    win_re, win_im = expand_in(jnp.real(b_bar)), expand_in(jnp.imag(b_bar))
    win = jnp.stack([jnp.concatenate([win_re[j * LANES:(j + 1) * LANES, j * slab:(j + 1) * slab],
                                      win_im[j * LANES:(j + 1) * LANES, j * slab:(j + 1) * slab]], axis=1)
                     for j in range(n_slab)]).astype(BF16)
    wc_re, wc_im = expand_out(c_re[l].astype(F32)), expand_out(-c_im[l].astype(F32))
    cre = jnp.stack([wc_re[j * slab:(j + 1) * slab, j * LANES:(j + 1) * LANES] for j in range(n_slab)]).astype(BF16)
    cim = jnp.stack([wc_im[j * slab:(j + 1) * slab, j * LANES:(j + 1) * LANES] for j in range(n_slab)]).astype(BF16)
    a_re = jnp.real(lam_bar).reshape(1, n_state)
    a_im = jnp.imag(lam_bar).reshape(1, n_state)

    lt = min(64, S)
    ssm = pl.pallas_call(
        functools.partial(_s5_kernel, batch=B, lt=lt, n_state=n_state),
        grid=(S // lt,),
        in_specs=[pl.BlockSpec((B, lt, ssm_width), lambda t: (0, t, 0)),
                  _const_spec(win.shape), _const_spec((1, n_state)), _const_spec((1, n_state)),
                  _const_spec(cre.shape), _const_spec(cim.shape), _const_spec((1, ssm_width)),
                  _const_spec((ssm_width, ssm_width)), _const_spec((1, ssm_width))],
        out_specs=pl.BlockSpec((B, lt, ssm_width), lambda t: (0, t, 0)),
        out_shape=jax.ShapeDtypeStruct((B, S, ssm_width), BF16),
        scratch_shapes=[pltpu.VMEM((2 * n_state // LANES, B * lt, LANES), F32),
                        pltpu.VMEM((n_state // LANES, B, LANES), F32),
                        pltpu.VMEM((n_state // LANES, B, LANES), F32),
                        pltpu.VMEM((ssm_width // LANES, B * lt, LANES), F32),
                        pltpu.VMEM((B * lt, ssm_width), F32)],
        compiler_params=cparams(dimension_semantics=("arbitrary",)),
        name="s5",
    )(u.reshape(B, S, ssm_width), win, a_re, a_im, cre, cim, _row(d_skip[l]),
      w_glu[l].astype(BF16), _row(b_glu[l]))

    wo = w_o[l].astype(BF16)
    wr_t = w_router[l].T.astype(F32)
    wr_hi = wr_t.astype(BF16)
    wr_lo = (wr_t - wr_hi.astype(F32)).astype(BF16)
    half = D // 2
    n_slabs = MOE_SLABS
    ts = T // n_slabs
    nt = ts // tm
    ssm2 = ssm.reshape(T, ssm_width)
    rbias = router_bias[l].astype(F32).reshape(N_EXPERTS, 1)
    kt = lambda dt: jax.ShapeDtypeStruct((TOP_K, ts), dt)
    k_spec = pl.BlockSpec((TOP_K, tm), lambda i: (0, i))

    def route(s):
        tok_s = lambda width: pl.BlockSpec((tm, width), lambda i: (i + s * nt, 0))
        return pl.pallas_call(
            _mix_kernel,
            grid=(nt,),
            in_specs=[tok_s(mla_width), tok_s(ssm_width), tok_s(D), _const_spec((1, mla_width)),
                      _const_spec((1, ssm_width)), _const_spec((mla_width, D)),
                      _const_spec((ssm_width, D)), _const_spec((1, D)), _const_spec((1, D)),
                      _const_spec((N_EXPERTS, D)), _const_spec((N_EXPERTS, D)),
                      _const_spec((N_EXPERTS, 1))],
            out_specs=[tok(D), tok(half), k_spec, tok(TOP_K), k_spec, _const_spec((N_EXPERTS, LANES))],
            out_shape=[jax.ShapeDtypeStruct((ts, D), F32), jax.ShapeDtypeStruct((ts, half), jnp.int32),
                       kt(jnp.int32), jax.ShapeDtypeStruct((ts, TOP_K), F32), kt(jnp.int32),
                       jax.ShapeDtypeStruct((N_EXPERTS, LANES), F32)],
            scratch_shapes=[pltpu.VMEM((N_EXPERTS, 1), F32), pltpu.VMEM((tm, tm), BF16)],
            compiler_params=cparams(dimension_semantics=("arbitrary",)),
            name="mix_router",
        )(att, ssm2, xn, _row(attn_out_g[l]), _row(ssm_out_g[l]), wo[:mla_width], wo[mla_width:],
          _row(ln1_g[l]), _row(ln1_b[l]), wr_hi, wr_lo, rbias)

    tr = EXPERT_ROW_TILE
    n_tiles = (ts * TOP_K) // tr + N_EXPERTS
    n_rows = n_tiles * tr
    tp = min(2048, ts)

    def dispatch(x1p, idx_k, rank_k, counts):
        cnt = counts[:, 0].astype(jnp.int32)
        tiles_e = (cnt + tr - 1) // tr
        tile_end = jnp.cumsum(tiles_e)
        tile_start = tile_end - tiles_e
        tile_ids = jnp.arange(n_tiles, dtype=jnp.int32)
        tile_expert = jnp.sum((tile_end[None, :] <= tile_ids[:, None]).astype(jnp.int32), axis=1)
        tile_expert = jnp.minimum(tile_expert, N_EXPERTS - 1)
        owner = (tile_start[None, :] <= tile_ids[:, None]) & (tile_ids[:, None] < tile_end[None, :])
        left = jnp.sum(jnp.where(owner, cnt[None, :] - (tile_ids[:, None] - tile_start[None, :]) * tr, 0),
                       axis=1)
        tile_valid = jnp.clip(left, 0, tr).astype(jnp.int32)
        pos = pl.pallas_call(
            _positions_kernel,
            grid_spec=pltpu.PrefetchScalarGridSpec(
                num_scalar_prefetch=1, grid=(ts // tp,),
                in_specs=[pl.BlockSpec((TOP_K, tp), lambda i, off: (0, i)),
                          pl.BlockSpec((TOP_K, tp), lambda i, off: (0, i))],
                out_specs=pl.BlockSpec((TOP_K, tp), lambda i, off: (0, i))),
            out_shape=kt(jnp.int32),
            name="positions",
        )((tile_start * tr).astype(jnp.int32), idx_k, rank_k)
        return _sc_scatter_rows(x1p, pos, n_rows), pos, tile_expert, tile_valid

    wg, wu, wd = w_gate[l], w_up[l], w_down[l]
    ff = wg.shape[-1]

    def experts(xs, tile_expert, tile_valid):
        return pl.pallas_call(
            _experts_kernel,
            grid_spec=pltpu.PrefetchScalarGridSpec(
                num_scalar_prefetch=2, grid=(n_tiles,),
                in_specs=[pl.BlockSpec((tr, half), lambda i, te, tv: (i, 0)),
                          pl.BlockSpec((1, D, ff), lambda i, te, tv: (te[i], 0, 0)),
                          pl.BlockSpec((1, D, ff), lambda i, te, tv: (te[i], 0, 0)),
                          pl.BlockSpec((1, ff, D), lambda i, te, tv: (te[i], 0, 0))],
                out_specs=pl.BlockSpec((tr, half), lambda i, te, tv: (i, 0)),
                scratch_shapes=[pltpu.VMEM((D, ff), BF16), pltpu.VMEM((D, ff), BF16),
                                pltpu.VMEM((ff, D), BF16)]),
            out_shape=jax.ShapeDtypeStruct((n_rows, half), jnp.int32),
            compiler_params=cparams(dimension_semantics=("arbitrary",)),
            name="experts",
        )(tile_expert, tile_valid, xs, wg, wu, wd)

    shared = (ws_gate[l].astype(BF16), ws_up[l].astype(BF16), ws_down[l].astype(BF16))

    def combine(s, out_so_far, yg, gate_k, x1):
        specs = [pl.BlockSpec((TOP_K, tm, half), lambda i: (0, i, 0)),
                 pl.BlockSpec((tm, TOP_K), lambda i: (i, 0)), tok(D),
                 _const_spec((D, ff)), _const_spec((D, ff)), _const_spec((ff, D)),
                 _const_spec((1, D)), _const_spec((1, D))]
        args = (yg.reshape(TOP_K, ts, half), gate_k, x1, *shared, _row(ln2_g[l]), _row(ln2_b[l]))
        body, aliases = _combine_kernel, {}
        if out_so_far is not None:
            specs = [pl.BlockSpec(memory_space=pl.ANY)] + specs
            args = (out_so_far,) + args
            body, aliases = _combine_into_kernel, {0: 0}
        return pl.pallas_call(
            body,
            grid=(nt,),
            in_specs=specs,
            out_specs=pl.BlockSpec((tm, D), lambda i: (i + s * nt, 0)),
            out_shape=jax.ShapeDtypeStruct((T, D), F32),
            input_output_aliases=aliases,
            compiler_params=cparams(dimension_semantics=("parallel",)),
            name="combine",
        )(*args)

    routed, moved = [], []
    for s in range(n_slabs):
        x1, x1p, idx_k, gate_k, rank_k, counts = route(s)
        routed.append((x1, gate_k))
        moved.append(dispatch(x1p, idx_k, rank_k, counts))
    gathered = []
    for xs, pos, tile_expert, tile_valid in moved:
        gathered.append(_sc_gather_rows(experts(xs, tile_expert, tile_valid), pos.reshape(TOP_K * ts)))
    out = None
    for s in range(n_slabs):
        out = combine(s, out, gathered[s], routed[s][1], routed[s][0])
    return out.reshape(B, S, D)
```

```python
import functools

import jax
import jax.numpy as jnp
from jax import lax
from jax.experimental import pallas as pl
from jax.experimental.pallas import tpu as pltpu
from jax.experimental.pallas import tpu_sc as plsc

CHUNK = 64
MLA_HEADS = 8
QK_NOPE = 64
QK_ROPE = 32
V_DIM = 64
Q_RANK = 256
KV_RANK = 128
ROPE_THETA = 10000.0
SSM_GROUP = 16
SSM_STATE = 64
N_EXPERTS = 64
TOP_K = 8
N_GROUPS = 8
TOP_GROUPS = 4
ROUTED_SCALE = 2.5
DEPTH = 1
ALPHA = (2.0 * DEPTH) ** 0.25
EPS = 1e-5
LOG2_E = 1.4426950408889634

LANES = 128
HEAD_PAD = LANES
VMEM_LIMIT = 56 * 1024 * 1024
TOKEN_TILE = 512
QUERY_TILE = 256
S5_TIME_TILE = 64
POSITIONS_TILE = 2048
EXPERT_ROW_TILE = 1024
MOE_SLABS = 2

BF16 = jnp.bfloat16
F32 = jnp.float32
NT_DIMS = (((1,), (1,)), ((), ()))


def _dot(a, b):
    return jnp.dot(a, b, preferred_element_type=F32)


def _layer_norm(x, g, b):
    mu = jnp.mean(x, axis=-1, keepdims=True)
    xc = x - mu
    var = jnp.mean(xc * xc, axis=-1, keepdims=True)
    return xc * lax.rsqrt(var + EPS) * g + b


def _rms_norm(x, g):
    return x * lax.rsqrt(jnp.mean(x * x, axis=-1, keepdims=True) + EPS) * g


def _inproj_kernel(x_ref, pos_ref, lng_ref, lnb_ref, w1_ref, qg_ref, kvg_ref,
                   wq_ref, wqr_ref, wk_ref, wv_ref, freq_ref,
                   xn_ref, q_ref, k_ref, v_ref, u_ref, *, ssm_width):
    xn = _layer_norm(x_ref[...], lng_ref[...], lnb_ref[...])
    xn_ref[...] = xn
    h = _dot(xn.astype(BF16), w1_ref[...])
    o1 = Q_RANK
    o2 = o1 + KV_RANK
    o3 = o2 + ssm_width
    o4 = o3 + HEAD_PAD
    cq = h[:, :o1]
    ckv = h[:, o1:o2]
    u_ref[...] = h[:, o2:o3]
    kr_raw = h[:, o3:o4]
    kr_rot = h[:, o4:o4 + HEAD_PAD]
    cqn = _rms_norm(cq, qg_ref[...]).astype(BF16)
    ckvn = _rms_norm(ckv, kvg_ref[...]).astype(BF16)

    tm = x_ref.shape[0]
    ang_t = freq_ref[...] * pos_ref[...]

    def to_token_rows(t):
        padded = jnp.concatenate([jnp.zeros((QK_NOPE, tm), F32), t,
                                  jnp.zeros((HEAD_PAD - QK_NOPE - QK_ROPE, tm), F32)], axis=0)
        return padded.T

    c = to_token_rows(jnp.cos(ang_t))
    s = to_token_rows(jnp.sin(ang_t))
    lane = lax.broadcasted_iota(jnp.int32, (1, HEAD_PAD), 1)
    scale = (QK_NOPE + QK_ROPE) ** -0.5 * LOG2_E
    cos1 = (c + jnp.where(lane < QK_NOPE, 1.0, 0.0)) * scale
    sin1 = s * scale
    cos_t = jnp.concatenate([cos1] * MLA_HEADS, axis=1)
    sin_t = jnp.concatenate([sin1] * MLA_HEADS, axis=1)
    q = _dot(cqn, wq_ref[...]) * cos_t + _dot(cqn, wqr_ref[...]) * sin_t
    q_ref[...] = q.astype(BF16)

    kr = kr_raw * c + kr_rot * s
    k = _dot(ckvn, wk_ref[...]) + jnp.concatenate([kr] * MLA_HEADS, axis=1)
    k_ref[...] = k.astype(BF16)
    ones_col = jnp.concatenate([jnp.where(lane == V_DIM, 1.0, 0.0)] * MLA_HEADS, axis=1)
    v_ref[...] = (_dot(ckvn, wv_ref[...]) + ones_col).astype(BF16)


def _attn_kernel(q_ref, k_ref, v_ref, o_ref, *, tq, heads, n_qtiles):
    qi = pl.program_id(1)
    row_chunk = lax.broadcasted_iota(jnp.int32, (tq, tq), 0) // CHUNK
    col_chunk = lax.broadcasted_iota(jnp.int32, (tq, tq), 1) // CHUNK
    diag_mask = row_chunk >= col_chunk

    def tile(n_blocks):
        keys = n_blocks * tq
        cols = [slice(h * HEAD_PAD, (h + 1) * HEAD_PAD) for h in range(heads)]

        def mask_diag(s):
            s_diag = jnp.where(diag_mask, s[:, keys - tq:], -jnp.inf)
            return s_diag if n_blocks == 1 else jnp.concatenate([s[:, :keys - tq], s_diag], axis=1)

        ss = [lax.dot_general(q_ref[:, c], k_ref[:keys, c], NT_DIMS, preferred_element_type=F32)
              for c in cols]
        ss = [mask_diag(s) for s in ss]
        ms = [jnp.max(s, axis=-1, keepdims=True) for s in ss]
        ps = [jnp.exp2(s - m).astype(BF16) for s, m in zip(ss, ms)]
        accs = [_dot(p, v_ref[:keys, c]) for p, c in zip(ps, cols)]
        outs = [acc[:, :V_DIM] / acc[:, V_DIM:V_DIM + 1] for acc in accs]
        o_ref[...] = jnp.concatenate(outs, axis=1).astype(o_ref.dtype)

    for c in range(n_qtiles):
        pl.when(qi == c)(functools.partial(tile, c + 1))


def _s5_kernel(u_ref, win_ref, are_ref, aim_ref, cre_ref, cim_ref, dskip_ref,
               wglu_ref, bglu_ref, o_ref, vx_ref, hre_ref, him_ref, io_ref, utm_ref, *,
               batch, lt, n_state):
    ti = pl.program_id(0)

    @pl.when(ti == 0)
    def _():
        hre_ref[...] = jnp.zeros_like(hre_ref)
        him_ref[...] = jnp.zeros_like(him_ref)

    width = u_ref.shape[-1]
    n_slab = width // LANES
    n_tiles = n_state // LANES
    slab_tiles = n_tiles // n_slab
    for b in range(batch):
        for c in range(n_slab):
            io_ref[c, b * lt:(b + 1) * lt, :] = u_ref[b, :, c * LANES:(c + 1) * LANES]

    def slab_tiles_of(j):
        return range(j * slab_tiles, (j + 1) * slab_tiles)

    def expand(j):
        for t in range(lt):
            utm_ref[t * batch:(t + 1) * batch, j * LANES:(j + 1) * LANES] = (
                io_ref[j, pl.ds(t, batch, stride=lt), :])
        ub = utm_ref[:, j * LANES:(j + 1) * LANES].astype(BF16)
        vj = _dot(ub, win_ref[j])
        for i, c in enumerate(slab_tiles_of(j)):
            vx_ref[c] = vj[:, i * LANES:(i + 1) * LANES]
            vx_ref[n_tiles + c] = vj[:, (slab_tiles + i) * LANES:(slab_tiles + i + 1) * LANES]

    def scan(j):
        tiles = slab_tiles_of(j)
        ar = [jnp.broadcast_to(are_ref[:, c * LANES:(c + 1) * LANES], (batch, LANES)) for c in tiles]
        ai = [jnp.broadcast_to(aim_ref[:, c * LANES:(c + 1) * LANES], (batch, LANES)) for c in tiles]
        hr = [hre_ref[c] for c in tiles]
        hi = [him_ref[c] for c in tiles]
        for t in range(lt):
            rows = slice(t * batch, (t + 1) * batch)
            for n, c in enumerate(tiles):
                nr = ar[n] * hr[n] - ai[n] * hi[n] + vx_ref[c, rows, :]
                ni = ar[n] * hi[n] + ai[n] * hr[n] + vx_ref[n_tiles + c, rows, :]
                vx_ref[c, rows, :] = nr
                vx_ref[n_tiles + c, rows, :] = ni
                hr[n], hi[n] = nr, ni
        for n, c in enumerate(tiles):
            hre_ref[c] = hr[n]
            him_ref[c] = hi[n]

    def project(j):
        tiles = slab_tiles_of(j)
        xr = jnp.concatenate([vx_ref[c].astype(BF16) for c in tiles], axis=1)
        xi = jnp.concatenate([vx_ref[n_tiles + c].astype(BF16) for c in tiles], axis=1)
        return _dot(xr, cre_ref[j]) + _dot(xi, cim_ref[j])

    ys = [None] * n_slab
    for j in range(n_slab + 2):
        if j < n_slab:
            expand(j)
        if 1 <= j <= n_slab:
            scan(j - 1)
        if j >= 2:
            ys[j - 2] = project(j - 2)
    y = jnp.concatenate(ys, axis=1) + dskip_ref[...] * utm_ref[...]
    y = jax.nn.gelu(y)
    z = _dot(y.astype(BF16), wglu_ref[...]) + bglu_ref[...]
    out = y * jax.nn.sigmoid(z)
    for c in range(n_slab):
        io_ref[c] = out[:, c * LANES:(c + 1) * LANES]
    for b in range(batch):
        for c in range(n_slab):
            o_ref[b, :, c * LANES:(c + 1) * LANES] = io_ref[c, pl.ds(b, lt, stride=batch), :].astype(o_ref.dtype)


def _router_gates(logits_t, rbias):
    n_exp, tm = logits_t.shape
    per_group = n_exp // N_GROUPS
    scores = jax.nn.sigmoid(logits_t)
    sel = scores + rbias
    neg_inf = -jnp.inf
    sub_iota = lax.broadcasted_iota(jnp.int32, (per_group, tm), 0).astype(F32)
    group_score = []
    for g in range(N_GROUPS):
        sg = sel[g * per_group:(g + 1) * per_group, :]
        m1 = jnp.max(sg, axis=0, keepdims=True)
        first = jnp.min(jnp.where(sg == m1, sub_iota, float(per_group)), axis=0, keepdims=True)
        m2 = jnp.max(jnp.where(sub_iota == first, neg_inf, sg), axis=0, keepdims=True)
        group_score.append(m1 + m2)
    masked = []
    for g in range(N_GROUPS):
        rank = jnp.zeros((1, tm), F32)
        for g2 in range(N_GROUPS):
            if g2 == g:
                continue
            ahead = (group_score[g2] >= group_score[g]) if g2 < g else (group_score[g2] > group_score[g])
            rank = rank + jnp.where(ahead, 1.0, 0.0)
        keep = rank < float(TOP_GROUPS)
        masked.append(jnp.where(keep, sel[g * per_group:(g + 1) * per_group, :], neg_inf))
    cur = jnp.concatenate(masked, axis=0)
    iota = lax.broadcasted_iota(jnp.int32, (n_exp, tm), 0).astype(F32)
    chosen = jnp.zeros((n_exp, tm), F32)
    picks, weights = [], []
    for _ in range(TOP_K):
        m = jnp.max(cur, axis=0, keepdims=True)
        idx = jnp.min(jnp.where(cur == m, iota, float(n_exp)), axis=0, keepdims=True)
        pick = iota == idx
        chosen = jnp.where(pick, 1.0, chosen)
        cur = jnp.where(pick, neg_inf, cur)
        picks.append(idx)
        weights.append(jnp.sum(jnp.where(pick, scores, 0.0), axis=0, keepdims=True))
    idx_k = jnp.concatenate(picks, axis=0)
    w_k = jnp.concatenate(weights, axis=0)
    gate_k = w_k / jnp.sum(w_k, axis=0, keepdims=True) * ROUTED_SCALE
    return idx_k, gate_k, chosen


def _pack_bf16_pairs(x):
    n = x.shape[1] // 2
    hi = lax.bitcast_convert_type(x[:, :n].astype(BF16).astype(F32), jnp.int32)
    lo = lax.bitcast_convert_type(x[:, n:].astype(BF16).astype(F32), jnp.int32)
    return hi | lax.shift_right_logical(lo, 16)


def _unpack_bf16_pairs(p):
    hi = lax.bitcast_convert_type(p & jnp.int32(-65536), F32).astype(BF16)
    lo = lax.bitcast_convert_type(lax.shift_left(p, 16), F32).astype(BF16)
    return hi, lo


def _mix_kernel(att_ref, ssm_ref, xn_ref, ag_ref, sg_ref, woa_ref, wos_ref,
                g_ref, b_ref, wrh_ref, wrl_ref, rb_ref,
                x1_ref, x1p_ref, idx_ref, gate_ref, rank_ref, cnt_ref, carry_ref, before_ref):
    @pl.when(pl.program_id(0) == 0)
    def _():
        carry_ref[...] = jnp.zeros_like(carry_ref)
        tm = before_ref.shape[0]
        before = (lax.broadcasted_iota(jnp.int32, (tm, tm), 0)
                  < lax.broadcasted_iota(jnp.int32, (tm, tm), 1))
        before_ref[...] = jnp.where(before, 1.0, 0.0).astype(BF16)

    an = _rms_norm(att_ref[...].astype(F32), ag_ref[...]).astype(BF16)
    sn = _rms_norm(ssm_ref[...].astype(F32), sg_ref[...]).astype(BF16)
    mix = _dot(an, woa_ref[...]) + _dot(sn, wos_ref[...])
    x1 = _layer_norm(ALPHA * xn_ref[...] + mix, g_ref[...], b_ref[...])
    x1_ref[...] = x1
    x1p_ref[...] = _pack_bf16_pairs(x1)
    x_hi = x1.astype(BF16)
    x_lo = (x1 - x_hi.astype(F32)).astype(BF16)
    dg = functools.partial(lax.dot_general, dimension_numbers=NT_DIMS, preferred_element_type=F32)
    logits_t = dg(wrh_ref[...], x_hi) + dg(wrl_ref[...], x_hi) + dg(wrh_ref[...], x_lo)
    idx_k, gate_k, chosen = _router_gates(logits_t, rb_ref[...])
    idx_ref[...] = idx_k.astype(jnp.int32)
    gate_ref[...] = gate_k.T

    n_exp, tm = chosen.shape
    excl = _dot(chosen.astype(BF16), before_ref[...])
    rank_full = carry_ref[...] + excl
    iota = lax.broadcasted_iota(jnp.int32, (n_exp, tm), 0).astype(F32)
    ranks = [jnp.sum(jnp.where(iota == idx_k[k:k + 1, :], rank_full, 0.0), axis=0, keepdims=True)
             for k in range(TOP_K)]
    rank_ref[...] = jnp.concatenate(ranks, axis=0).astype(jnp.int32)
    total = carry_ref[...] + jnp.sum(chosen, axis=1, keepdims=True)
    carry_ref[...] = total
    cnt_ref[...] = jnp.broadcast_to(total, cnt_ref.shape)


def _positions_kernel(off_ref, idx_ref, rank_ref, pos_ref):
    idx = idx_ref[...]
    base = jnp.zeros(idx.shape, jnp.int32)
    for e in range(N_EXPERTS):
        base = jnp.where(idx == e, off_ref[e], base)
    pos_ref[...] = rank_ref[...] + base


def _swiglu(x_hi, x_lo, wg_ref, wu_ref, wd_ref):
    half = x_hi.shape[1]
    hg = _dot(x_hi, wg_ref[:half, :]) + _dot(x_lo, wg_ref[half:, :])
    hu = _dot(x_hi, wu_ref[:half, :]) + _dot(x_lo, wu_ref[half:, :])
    h = jax.nn.silu(hg) * hu
    return _dot(h.astype(BF16), wd_ref[...])


def _experts_kernel(te_ref, valid_ref, xs_ref, wg_ref, wu_ref, wd_ref, ys_ref,
                    wgb_ref, wub_ref, wdb_ref):
    i = pl.program_id(0)
    valid = valid_ref[i]

    @pl.when((i == 0) | (te_ref[i] != te_ref[jnp.maximum(i - 1, 0)]))
    def _():
        wgb_ref[...] = wg_ref[0].astype(BF16)
        wub_ref[...] = wu_ref[0].astype(BF16)
        wdb_ref[...] = wd_ref[0].astype(BF16)

    @pl.when(valid > 0)
    def _():
        rows = lax.broadcasted_iota(jnp.int32, xs_ref.shape, 0)
        x_hi, x_lo = _unpack_bf16_pairs(jnp.where(rows < valid, xs_ref[...], 0))
        ys_ref[...] = _pack_bf16_pairs(_swiglu(x_hi, x_lo, wgb_ref, wub_ref, wdb_ref))

    @pl.when(valid == 0)
    def _():
        ys_ref[...] = jnp.zeros_like(ys_ref)


def _combine_kernel(yg_ref, gate_ref, x1_ref, wsg_ref, wsu_ref, wsd_ref, g_ref, b_ref, o_ref):
    x1 = x1_ref[...]
    half = x1.shape[1] // 2
    acc = _swiglu(x1[:, :half].astype(BF16), x1[:, half:].astype(BF16), wsg_ref, wsu_ref, wsd_ref)
    gates = gate_ref[...]
    acc_hi, acc_lo = acc[:, :half], acc[:, half:]
    for k in range(TOP_K):
        y_hi, y_lo = _unpack_bf16_pairs(yg_ref[k])
        gk = gates[:, k:k + 1]
        acc_hi = acc_hi + gk * y_hi.astype(F32)
        acc_lo = acc_lo + gk * y_lo.astype(F32)
    ffn = jnp.concatenate([acc_hi, acc_lo], axis=1)
    o_ref[...] = _layer_norm(ALPHA * x1 + ffn, g_ref[...], b_ref[...])


def _combine_into_kernel(prev_ref, *refs):
    del prev_ref
    _combine_kernel(*refs)


SC_CORES = 2
SC_SUBCORES = 16
SC_WORKERS = SC_CORES * SC_SUBCORES
SC_CHUNK = 64


def _sc_mesh():
    return plsc.VectorSubcoreMesh(core_axis_name="c", subcore_axis_name="s")


def _sc_gather_rows(table, idx):
    n = idx.shape[0]
    d = table.shape[1]
    assert n % (SC_WORKERS * 2 * SC_CHUNK) == 0, n
    per_w = n // SC_WORKERS
    n_ch = per_w // SC_CHUNK

    @functools.partial(
        pl.kernel, mesh=_sc_mesh(),
        out_type=jax.ShapeDtypeStruct((n, d), table.dtype),
        scratch_types=[pltpu.VMEM((n_ch, SC_CHUNK), jnp.int32),
                       pltpu.VMEM((2, SC_CHUNK, d), table.dtype),
                       pltpu.SemaphoreType.DMA((2,)),
                       pltpu.SemaphoreType.DMA((2,))],
    )
    def k(table_hbm, idx_hbm, out_hbm, idx_v, buf, gsem, osem):
        wid = lax.axis_index("s") * SC_CORES + lax.axis_index("c")
        base = wid * per_w
        pltpu.sync_copy(idx_hbm.at[wid], idx_v)

        def gather(c, b):
            return pltpu.make_async_copy(table_hbm.at[idx_v.at[c]], buf.at[b], gsem.at[b])

        def put(c, b):
            return pltpu.make_async_copy(buf.at[b], out_hbm.at[pl.ds(base + c * SC_CHUNK, SC_CHUNK)],
                                         osem.at[b])

        gather(0, 0).start()

        @pl.loop(0, n_ch, step=2)
        def _(c):
            for b in range(2):
                cc = c + b
                gather(cc, b).wait()

                @pl.when(cc + 1 < n_ch)
                def _():
                    @pl.when(cc >= 1)
                    def _():
                        put(cc - 1, 1 - b).wait()
                    gather(cc + 1, 1 - b).start()

                put(cc, b).start()

        put(n_ch - 2, 0).wait()
        put(n_ch - 1, 1).wait()

    return k(table, idx.reshape(SC_WORKERS, n_ch, SC_CHUNK))


def _sc_scatter_rows(x, pos, n_out):
    t, d = x.shape
    kk = pos.shape[0]
    assert t % (SC_WORKERS * 2 * SC_CHUNK) == 0, t
    per_w = t // SC_WORKERS
    n_ch = per_w // SC_CHUNK
    pos_w = pos.reshape(kk, SC_WORKERS, n_ch, SC_CHUNK).transpose(1, 2, 0, 3)
    pos_w = pos_w.reshape(SC_WORKERS, n_ch * kk, SC_CHUNK)

    @functools.partial(
        pl.kernel, mesh=_sc_mesh(),
        out_type=jax.ShapeDtypeStruct((n_out, d), x.dtype),
        scratch_types=[pltpu.VMEM((n_ch * kk, SC_CHUNK), jnp.int32),
                       pltpu.VMEM((2, SC_CHUNK, d), x.dtype),
                       pltpu.SemaphoreType.DMA((2,)),
                       pltpu.SemaphoreType.DMA((2,))],
    )
    def k(x_hbm, pos_hbm, out_hbm, idx_v, buf, isem, osem):
        wid = lax.axis_index("s") * SC_CORES + lax.axis_index("c")
        base = wid * per_w
        pltpu.sync_copy(pos_hbm.at[wid], idx_v)

        def get(c, b):
            return pltpu.make_async_copy(x_hbm.at[pl.ds(base + c * SC_CHUNK, SC_CHUNK)], buf.at[b],
                                         isem.at[b])

        def put(c, j, b):
            return pltpu.make_async_copy(buf.at[b], out_hbm.at[idx_v.at[c * kk + j]], osem.at[b])

        get(0, 0).start()

        @pl.loop(0, n_ch, step=2)
        def _(c):
            for b in range(2):
                cc = c + b
                get(cc, b).wait()

                @pl.when(cc + 1 < n_ch)
                def _():
                    @pl.when(cc >= 1)
                    def _():
                        for j in range(kk):
                            put(cc - 1, j, 1 - b).wait()
                    get(cc + 1, 1 - b).start()

                for j in range(kk):
                    put(cc, j, b).start()

        for j in range(kk):
            put(n_ch - 2, j, 0).wait()
        for j in range(kk):
            put(n_ch - 1, j, 1).wait()

    return k(x, pos_w)


def _row(v):
    return v.reshape(1, -1).astype(F32)


def _const_spec(shape):
    nd = len(shape)
    return pl.BlockSpec(shape, lambda *_: (0,) * nd)


def _pad_heads(w, width):
    r, h, _ = w.shape
    return jnp.pad(w, ((0, 0), (0, 0), (0, HEAD_PAD - width))).reshape(r, h * HEAD_PAD)


def _half_rotate(w):
    half = QK_ROPE // 2
    return jnp.concatenate([-w[..., half:], w[..., :half]], axis=-1)


def kernel(x, positions, ln_in_g, ln_in_b, w_in, q_norm_g, kv_norm_g, w_uq, w_ukv, lambda_re, lambda_im, log_step, b_re, b_im, c_re, c_im, d_skip, w_glu, b_glu, attn_out_g, ssm_out_g, w_o, ln1_g, ln1_b, w_router, router_bias, w_gate, w_up, w_down, ws_gate, ws_up, ws_down, ln2_g, ln2_b):
    B, S, D = x.shape
    T = B * S
    assert DEPTH == 1 and w_in.shape[0] == DEPTH
    assert S % QUERY_TILE == 0 and S % TOKEN_TILE == 0 and S % S5_TIME_TILE == 0, S
    assert T % (MOE_SLABS * TOKEN_TILE) == 0 and (T // MOE_SLABS * TOP_K) % EXPERT_ROW_TILE == 0, T
    l = 0
    ssm_width = w_glu.shape[-1]
    n_groups = ssm_width // SSM_GROUP
    n_state = n_groups * SSM_STATE
    mla_width = MLA_HEADS * V_DIM
    qk_pad = MLA_HEADS * HEAD_PAD
    cparams = functools.partial(pltpu.CompilerParams, vmem_limit_bytes=VMEM_LIMIT)

    s1, s2, s3 = Q_RANK, Q_RANK + KV_RANK, Q_RANK + KV_RANK + QK_ROPE
    wi = w_in[l]
    w_kr = wi[:, s2:s3]
    pad_rope = lambda w: jnp.pad(w, ((0, 0), (QK_NOPE, HEAD_PAD - QK_NOPE - QK_ROPE)))
    w1 = jnp.concatenate([wi[:, :s2], wi[:, s3:], pad_rope(w_kr), pad_rope(_half_rotate(w_kr))],
                         axis=1).astype(BF16)
    wq = w_uq[l]
    zeros_nope = jnp.zeros(wq.shape[:2] + (QK_NOPE,), wq.dtype)
    wq_main = _pad_heads(wq, QK_NOPE + QK_ROPE).astype(BF16)
    wq_rot = _pad_heads(jnp.concatenate([zeros_nope, _half_rotate(wq[..., QK_NOPE:])], axis=-1),
                        QK_NOPE + QK_ROPE).astype(BF16)
    wkv = w_ukv[l]
    wk = _pad_heads(wkv[..., :QK_NOPE], QK_NOPE).astype(BF16)
    wv = _pad_heads(wkv[..., QK_NOPE:], V_DIM).astype(BF16)
    half = QK_ROPE // 2
    inv_freq = ROPE_THETA ** (-jnp.arange(half, dtype=F32) / half)
    freq = jnp.concatenate([inv_freq, inv_freq]).reshape(QK_ROPE, 1)
    pos_f = positions.astype(F32).reshape(1, T)

    tm = min(TOKEN_TILE, T)
    w1_cols = w1.shape[1]
    tok = lambda width: pl.BlockSpec((tm, width), lambda i: (i, 0))
    xn, q, k, v, u = pl.pallas_call(
        functools.partial(_inproj_kernel, ssm_width=ssm_width),
        grid=(T // tm,),
        in_specs=[tok(D), pl.BlockSpec((1, tm), lambda i: (0, i)), _const_spec((1, D)), _const_spec((1, D)),
                  _const_spec((D, w1_cols)), _const_spec((1, Q_RANK)), _const_spec((1, KV_RANK)),
                  _const_spec((Q_RANK, qk_pad)), _const_spec((Q_RANK, qk_pad)),
                  _const_spec((KV_RANK, qk_pad)), _const_spec((KV_RANK, qk_pad)),
                  _const_spec((QK_ROPE, 1))],
        out_specs=[tok(D), tok(qk_pad), tok(qk_pad), tok(qk_pad), tok(ssm_width)],
        out_shape=[jax.ShapeDtypeStruct((T, D), F32), jax.ShapeDtypeStruct((T, qk_pad), BF16),
                   jax.ShapeDtypeStruct((T, qk_pad), BF16), jax.ShapeDtypeStruct((T, qk_pad), BF16),
                   jax.ShapeDtypeStruct((T, ssm_width), F32)],
        compiler_params=cparams(dimension_semantics=("parallel",)),
        name="inproj",
    )(x.reshape(T, D), pos_f, _row(ln_in_g), _row(ln_in_b), w1, _row(q_norm_g[l]), _row(kv_norm_g[l]),
      wq_main, wq_rot, wk, wv, freq)

    tq = min(QUERY_TILE, S)
    nq = S // tq
    att = pl.pallas_call(
        functools.partial(_attn_kernel, tq=tq, heads=MLA_HEADS, n_qtiles=nq),
        grid=(B, nq),
        in_specs=[pl.BlockSpec((tq, qk_pad), lambda b, i: (b * nq + i, 0)),
                  pl.BlockSpec((S, qk_pad), lambda b, i: (b, 0)),
                  pl.BlockSpec((S, qk_pad), lambda b, i: (b, 0))],
        out_specs=pl.BlockSpec((tq, mla_width), lambda b, i: (b * nq + i, 0)),
        out_shape=jax.ShapeDtypeStruct((T, mla_width), BF16),
        compiler_params=cparams(dimension_semantics=("parallel", "arbitrary")),
        name="attention",
    )(q, k, v)

    lam = lax.complex(jnp.minimum(lambda_re[l].astype(F32), -1e-4), lambda_im[l].astype(F32))
    step = jnp.exp(log_step[l].astype(F32))[:, None]
    lam_bar = jnp.exp(lam * step)
    b_bar = ((lam_bar - 1.0) / lam)[..., None] * lax.complex(b_re[l].astype(F32), b_im[l].astype(F32))
    n_slab = ssm_width // LANES
    g_per_slab = n_groups // n_slab
    eye = jnp.eye(n_groups, dtype=F32)

    def expand_in(bpart):
        return jnp.einsum('gpc,gh->gchp', bpart, eye).reshape(ssm_width, n_state)

    def expand_out(cpart):
        return jnp.einsum('gcp,gh->gphc', cpart, eye).reshape(n_state, ssm_width)

    slab = n_state // n_slab
    win_re, win_im = expand_in(jnp.real(b_bar)), expand_in(jnp.imag(b_bar))
    win = jnp.stack([jnp.concatenate([win_re[j * LANES:(j + 1) * LANES, j * slab:(j + 1) * slab],
                                      win_im[j * LANES:(j + 1) * LANES, j * slab:(j + 1) * slab]], axis=1)
                     for j in range(n_slab)]).astype(BF16)
    wc_re, wc_im = expand_out(c_re[l].astype(F32)), expand_out(-c_im[l].astype(F32))
    cre = jnp.stack([wc_re[j * slab:(j + 1) * slab, j * LANES:(j + 1) * LANES] for j in range(n_slab)]).astype(BF16)
    cim = jnp.stack([wc_im[j * slab:(j + 1) * slab, j * LANES:(j + 1) * LANES] for j in range(n_slab)]).astype(BF16)
    a_re = jnp.real(lam_bar).reshape(1, n_state)
    a_im = jnp.imag(lam_bar).reshape(1, n_state)

    lt = min(S5_TIME_TILE, S)
    ssm = pl.pallas_call(
        functools.partial(_s5_kernel, batch=B, lt=lt, n_state=n_state),
        grid=(S // lt,),
        in_specs=[pl.BlockSpec((B, lt, ssm_width), lambda t: (0, t, 0)),
                  _const_spec(win.shape), _const_spec((1, n_state)), _const_spec((1, n_state)),
                  _const_spec(cre.shape), _const_spec(cim.shape), _const_spec((1, ssm_width)),
                  _const_spec((ssm_width, ssm_width)), _const_spec((1, ssm_width))],
        out_specs=pl.BlockSpec((B, lt, ssm_width), lambda t: (0, t, 0)),
        out_shape=jax.ShapeDtypeStruct((B, S, ssm_width), BF16),
        scratch_shapes=[pltpu.VMEM((2 * n_state // LANES, B * lt, LANES), F32),
                        pltpu.VMEM((n_state // LANES, B, LANES), F32),
                        pltpu.VMEM((n_state // LANES, B, LANES), F32),
                        pltpu.VMEM((ssm_width // LANES, B * lt, LANES), F32),
                        pltpu.VMEM((B * lt, ssm_width), F32)],
        compiler_params=cparams(dimension_semantics=("arbitrary",)),
        name="s5",
    )(u.reshape(B, S, ssm_width), win, a_re, a_im, cre, cim, _row(d_skip[l]),
      w_glu[l].astype(BF16), _row(b_glu[l]))

    wo = w_o[l].astype(BF16)
    wr_t = w_router[l].T.astype(F32)
    wr_hi = wr_t.astype(BF16)
    wr_lo = (wr_t - wr_hi.astype(F32)).astype(BF16)
    half = D // 2
    n_slabs = MOE_SLABS
    ts = T // n_slabs
    nt = ts // tm
    ssm2 = ssm.reshape(T, ssm_width)
    rbias = router_bias[l].astype(F32).reshape(N_EXPERTS, 1)
    kt = lambda dt: jax.ShapeDtypeStruct((TOP_K, ts), dt)
    k_spec = pl.BlockSpec((TOP_K, tm), lambda i: (0, i))

    def route(s):
        tok_s = lambda width: pl.BlockSpec((tm, width), lambda i: (i + s * nt, 0))
        return pl.pallas_call(
            _mix_kernel,
            grid=(nt,),
            in_specs=[tok_s(mla_width), tok_s(ssm_width), tok_s(D), _const_spec((1, mla_width)),
                      _const_spec((1, ssm_width)), _const_spec((mla_width, D)),
                      _const_spec((ssm_width, D)), _const_spec((1, D)), _const_spec((1, D)),
                      _const_spec((N_EXPERTS, D)), _const_spec((N_EXPERTS, D)),
                      _const_spec((N_EXPERTS, 1))],
            out_specs=[tok(D), tok(half), k_spec, tok(TOP_K), k_spec, _const_spec((N_EXPERTS, LANES))],
            out_shape=[jax.ShapeDtypeStruct((ts, D), F32), jax.ShapeDtypeStruct((ts, half), jnp.int32),
                       kt(jnp.int32), jax.ShapeDtypeStruct((ts, TOP_K), F32), kt(jnp.int32),
                       jax.ShapeDtypeStruct((N_EXPERTS, LANES), F32)],
            scratch_shapes=[pltpu.VMEM((N_EXPERTS, 1), F32), pltpu.VMEM((tm, tm), BF16)],
            compiler_params=cparams(dimension_semantics=("arbitrary",)),
            name="mix_router",
        )(att, ssm2, xn, _row(attn_out_g[l]), _row(ssm_out_g[l]), wo[:mla_width], wo[mla_width:],
          _row(ln1_g[l]), _row(ln1_b[l]), wr_hi, wr_lo, rbias)

    tr = EXPERT_ROW_TILE
    n_tiles = (ts * TOP_K) // tr + N_EXPERTS
    n_rows = n_tiles * tr
    tp = min(POSITIONS_TILE, ts)

    def dispatch(x1p, idx_k, rank_k, counts):
        cnt = counts[:, 0].astype(jnp.int32)
        tiles_e = (cnt + tr - 1) // tr
        tile_end = jnp.cumsum(tiles_e)
        tile_start = tile_end - tiles_e
        tile_ids = jnp.arange(n_tiles, dtype=jnp.int32)
        tile_expert = jnp.sum((tile_end[None, :] <= tile_ids[:, None]).astype(jnp.int32), axis=1)
        tile_expert = jnp.minimum(tile_expert, N_EXPERTS - 1)
        owner = (tile_start[None, :] <= tile_ids[:, None]) & (tile_ids[:, None] < tile_end[None, :])
        left = jnp.sum(jnp.where(owner, cnt[None, :] - (tile_ids[:, None] - tile_start[None, :]) * tr, 0),
                       axis=1)
        tile_valid = jnp.clip(left, 0, tr).astype(jnp.int32)
        pos = pl.pallas_call(
            _positions_kernel,
            grid_spec=pltpu.PrefetchScalarGridSpec(
                num_scalar_prefetch=1, grid=(ts // tp,),
                in_specs=[pl.BlockSpec((TOP_K, tp), lambda i, off: (0, i)),
                          pl.BlockSpec((TOP_K, tp), lambda i, off: (0, i))],
                out_specs=pl.BlockSpec((TOP_K, tp), lambda i, off: (0, i))),
            out_shape=kt(jnp.int32),
            name="positions",
        )((tile_start * tr).astype(jnp.int32), idx_k, rank_k)
        return _sc_scatter_rows(x1p, pos, n_rows), pos, tile_expert, tile_valid

    wg, wu, wd = w_gate[l], w_up[l], w_down[l]
    ff = wg.shape[-1]

    def experts(xs, tile_expert, tile_valid):
        return pl.pallas_call(
            _experts_kernel,
            grid_spec=pltpu.PrefetchScalarGridSpec(
                num_scalar_prefetch=2, grid=(n_tiles,),
                in_specs=[pl.BlockSpec((tr, half), lambda i, te, tv: (i, 0)),
                          pl.BlockSpec((1, D, ff), lambda i, te, tv: (te[i], 0, 0)),
                          pl.BlockSpec((1, D, ff), lambda i, te, tv: (te[i], 0, 0)),
                          pl.BlockSpec((1, ff, D), lambda i, te, tv: (te[i], 0, 0))],
                out_specs=pl.BlockSpec((tr, half), lambda i, te, tv: (i, 0)),
                scratch_shapes=[pltpu.VMEM((D, ff), BF16), pltpu.VMEM((D, ff), BF16),
                                pltpu.VMEM((ff, D), BF16)]),
            out_shape=jax.ShapeDtypeStruct((n_rows, half), jnp.int32),
            compiler_params=cparams(dimension_semantics=("arbitrary",)),
            name="experts",
        )(tile_expert, tile_valid, xs, wg, wu, wd)

    shared = (ws_gate[l].astype(BF16), ws_up[l].astype(BF16), ws_down[l].astype(BF16))

    def combine(s, out_so_far, yg, gate_k, x1):
        specs = [pl.BlockSpec((TOP_K, tm, half), lambda i: (0, i, 0)),
                 pl.BlockSpec((tm, TOP_K), lambda i: (i, 0)), tok(D),
                 _const_spec((D, ff)), _const_spec((D, ff)), _const_spec((ff, D)),
                 _const_spec((1, D)), _const_spec((1, D))]
        args = (yg.reshape(TOP_K, ts, half), gate_k, x1, *shared, _row(ln2_g[l]), _row(ln2_b[l]))
        body, aliases = _combine_kernel, {}
        if out_so_far is not None:
            specs = [pl.BlockSpec(memory_space=pl.ANY)] + specs
            args = (out_so_far,) + args
            body, aliases = _combine_into_kernel, {0: 0}
        return pl.pallas_call(
            body,
            grid=(nt,),
            in_specs=specs,
            out_specs=pl.BlockSpec((tm, D), lambda i: (i + s * nt, 0)),
            out_shape=jax.ShapeDtypeStruct((T, D), F32),
            input_output_aliases=aliases,
            compiler_params=cparams(dimension_semantics=("parallel",)),
            name="combine",
        )(*args)

    routed, moved = [], []
    for s in range(n_slabs):
        x1, x1p, idx_k, gate_k, rank_k, counts = route(s)
        routed.append((x1, gate_k))
        moved.append(dispatch(x1p, idx_k, rank_k, counts))
    gathered = []
    for xs, pos, tile_expert, tile_valid in moved:
        gathered.append(_sc_gather_rows(experts(xs, tile_expert, tile_valid), pos.reshape(TOP_K * ts)))
    out = None
    for s in range(n_slabs):
        out = combine(s, out, gathered[s], routed[s][1], routed[s][0])
    return out.reshape(B, S, D)
```

```python
import functools

import jax
import jax.numpy as jnp
from jax import lax
from jax.experimental import pallas as pl
from jax.experimental.pallas import tpu as pltpu
from jax.experimental.pallas import tpu_sc as plsc

CHUNK = 64
MLA_HEADS = 8
QK_NOPE = 64
QK_ROPE = 32
V_DIM = 64
Q_RANK = 256
KV_RANK = 128
ROPE_THETA = 10000.0
SSM_GROUP = 16
SSM_STATE = 64
N_EXPERTS = 64
TOP_K = 8
N_GROUPS = 8
TOP_GROUPS = 4
ROUTED_SCALE = 2.5
DEPTH = 1
ALPHA = (2.0 * DEPTH) ** 0.25
EPS = 1e-5
LOG2_E = 1.4426950408889634

LANES = 128
HEAD_PAD = LANES
VMEM_LIMIT = 56 * 1024 * 1024
TOKEN_TILE = 512
QUERY_TILE = 256
S5_TIME_TILE = 64
POSITIONS_TILE = 2048
EXPERT_ROW_TILE = 1024
MOE_SLABS = 2

BF16 = jnp.bfloat16
F32 = jnp.float32
NT_DIMS = (((1,), (1,)), ((), ()))


def _dot(a, b):
    return jnp.dot(a, b, preferred_element_type=F32)


def _layer_norm(x, g, b):
    mu = jnp.mean(x, axis=-1, keepdims=True)
    xc = x - mu
    var = jnp.mean(xc * xc, axis=-1, keepdims=True)
    return xc * lax.rsqrt(var + EPS) * g + b


def _rms_norm(x, g):
    return x * lax.rsqrt(jnp.mean(x * x, axis=-1, keepdims=True) + EPS) * g


def _inproj_kernel(x_ref, pos_ref, lng_ref, lnb_ref, w1_ref, qg_ref, kvg_ref,
                   wq_ref, wqr_ref, wk_ref, wv_ref, freq_ref,
                   xn_ref, q_ref, k_ref, v_ref, u_ref, *, ssm_width):
    xn = _layer_norm(x_ref[...], lng_ref[...], lnb_ref[...])
    xn_ref[...] = xn
    h = _dot(xn.astype(BF16), w1_ref[...])
    o1 = Q_RANK
    o2 = o1 + KV_RANK
    o3 = o2 + ssm_width
    o4 = o3 + HEAD_PAD
    cq = h[:, :o1]
    ckv = h[:, o1:o2]
    u_ref[...] = h[:, o2:o3]
    kr_raw = h[:, o3:o4]
    kr_rot = h[:, o4:o4 + HEAD_PAD]
    cqn = _rms_norm(cq, qg_ref[...]).astype(BF16)
    ckvn = _rms_norm(ckv, kvg_ref[...]).astype(BF16)

    tm = x_ref.shape[0]
    ang_t = freq_ref[...] * pos_ref[...]

    def to_token_rows(t):
        padded = jnp.concatenate([jnp.zeros((QK_NOPE, tm), F32), t,
                                  jnp.zeros((HEAD_PAD - QK_NOPE - QK_ROPE, tm), F32)], axis=0)
        return padded.T

    c = to_token_rows(jnp.cos(ang_t))
    s = to_token_rows(jnp.sin(ang_t))
    lane = lax.broadcasted_iota(jnp.int32, (1, HEAD_PAD), 1)
    scale = (QK_NOPE + QK_ROPE) ** -0.5 * LOG2_E
    cos1 = (c + jnp.where(lane < QK_NOPE, 1.0, 0.0)) * scale
    sin1 = s * scale
    cos_t = jnp.concatenate([cos1] * MLA_HEADS, axis=1)
    sin_t = jnp.concatenate([sin1] * MLA_HEADS, axis=1)
    q = _dot(cqn, wq_ref[...]) * cos_t + _dot(cqn, wqr_ref[...]) * sin_t
    q_ref[...] = q.astype(BF16)

    kr = kr_raw * c + kr_rot * s
    k = _dot(ckvn, wk_ref[...]) + jnp.concatenate([kr] * MLA_HEADS, axis=1)
    k_ref[...] = k.astype(BF16)
    ones_col = jnp.concatenate([jnp.where(lane == V_DIM, 1.0, 0.0)] * MLA_HEADS, axis=1)
    v_ref[...] = (_dot(ckvn, wv_ref[...]) + ones_col).astype(BF16)


def _attn_kernel(q_ref, k_ref, v_ref, o_ref, *, tq, heads, n_qtiles):
    qi = pl.program_id(1)
    row_chunk = lax.broadcasted_iota(jnp.int32, (tq, tq), 0) // CHUNK
    col_chunk = lax.broadcasted_iota(jnp.int32, (tq, tq), 1) // CHUNK
    diag_mask = row_chunk >= col_chunk

    def tile(n_blocks):
        keys = n_blocks * tq
        cols = [slice(h * HEAD_PAD, (h + 1) * HEAD_PAD) for h in range(heads)]

        def mask_diag(s):
            s_diag = jnp.where(diag_mask, s[:, keys - tq:], -jnp.inf)
            return s_diag if n_blocks == 1 else jnp.concatenate([s[:, :keys - tq], s_diag], axis=1)

        ss = [lax.dot_general(q_ref[:, c], k_ref[:keys, c], NT_DIMS, preferred_element_type=F32)
              for c in cols]
        ss = [mask_diag(s) for s in ss]
        ms = [jnp.max(s, axis=-1, keepdims=True) for s in ss]
        ps = [jnp.exp2(s - m).astype(BF16) for s, m in zip(ss, ms)]
        accs = [_dot(p, v_ref[:keys, c]) for p, c in zip(ps, cols)]
        outs = [acc[:, :V_DIM] / acc[:, V_DIM:V_DIM + 1] for acc in accs]
        o_ref[...] = jnp.concatenate(outs, axis=1).astype(o_ref.dtype)

    for c in range(n_qtiles):
        pl.when(qi == c)(functools.partial(tile, c + 1))


def _s5_kernel(u_ref, win_ref, are_ref, aim_ref, cre_ref, cim_ref, dskip_ref,
               wglu_ref, bglu_ref, o_ref, vx_ref, hre_ref, him_ref, io_ref, utm_ref, *,
               batch, lt, n_state):
    ti = pl.program_id(0)

    @pl.when(ti == 0)
    def _():
        hre_ref[...] = jnp.zeros_like(hre_ref)
        him_ref[...] = jnp.zeros_like(him_ref)

    width = u_ref.shape[-1]
    n_slab = width // LANES
    n_tiles = n_state // LANES
    slab_tiles = n_tiles // n_slab
    for b in range(batch):
        for c in range(n_slab):
            io_ref[c, b * lt:(b + 1) * lt, :] = u_ref[b, :, c * LANES:(c + 1) * LANES]

    def slab_tiles_of(j):
        return range(j * slab_tiles, (j + 1) * slab_tiles)

    def expand(j):
        for t in range(lt):
            utm_ref[t * batch:(t + 1) * batch, j * LANES:(j + 1) * LANES] = (
                io_ref[j, pl.ds(t, batch, stride=lt), :])
        ub = utm_ref[:, j * LANES:(j + 1) * LANES].astype(BF16)
        vj = _dot(ub, win_ref[j])
        for i, c in enumerate(slab_tiles_of(j)):
            vx_ref[c] = vj[:, i * LANES:(i + 1) * LANES]
            vx_ref[n_tiles + c] = vj[:, (slab_tiles + i) * LANES:(slab_tiles + i + 1) * LANES]

    def scan(j):
        tiles = slab_tiles_of(j)
        ar = [jnp.broadcast_to(are_ref[:, c * LANES:(c + 1) * LANES], (batch, LANES)) for c in tiles]
        ai = [jnp.broadcast_to(aim_ref[:, c * LANES:(c + 1) * LANES], (batch, LANES)) for c in tiles]
        hr = [hre_ref[c] for c in tiles]
        hi = [him_ref[c] for c in tiles]
        for t in range(lt):
            rows = slice(t * batch, (t + 1) * batch)
            for n, c in enumerate(tiles):
                nr = ar[n] * hr[n] - ai[n] * hi[n] + vx_ref[c, rows, :]
                ni = ar[n] * hi[n] + ai[n] * hr[n] + vx_ref[n_tiles + c, rows, :]
                vx_ref[c, rows, :] = nr
                vx_ref[n_tiles + c, rows, :] = ni
                hr[n], hi[n] = nr, ni
        for n, c in enumerate(tiles):
            hre_ref[c] = hr[n]
            him_ref[c] = hi[n]

    def project(j):
        tiles = slab_tiles_of(j)
        xr = jnp.concatenate([vx_ref[c].astype(BF16) for c in tiles], axis=1)
        xi = jnp.concatenate([vx_ref[n_tiles + c].astype(BF16) for c in tiles], axis=1)
        return _dot(xr, cre_ref[j]) + _dot(xi, cim_ref[j])

    ys = [None] * n_slab
    for j in range(n_slab + 2):
        if j < n_slab:
            expand(j)
        if 1 <= j <= n_slab:
            scan(j - 1)
        if j >= 2:
            ys[j - 2] = project(j - 2)
    y = jnp.concatenate(ys, axis=1) + dskip_ref[...] * utm_ref[...]
    y = jax.nn.gelu(y)
    z = _dot(y.astype(BF16), wglu_ref[...]) + bglu_ref[...]
    out = y * jax.nn.sigmoid(z)
    for c in range(n_slab):
        io_ref[c] = out[:, c * LANES:(c + 1) * LANES]
    for b in range(batch):
        for c in range(n_slab):
            o_ref[b, :, c * LANES:(c + 1) * LANES] = io_ref[c, pl.ds(b, lt, stride=batch), :].astype(o_ref.dtype)


def _router_gates(logits_t, rbias):
    n_exp, tm = logits_t.shape
    per_group = n_exp // N_GROUPS
    scores = jax.nn.sigmoid(logits_t)
    sel = scores + rbias
    neg_inf = -jnp.inf
    sub_iota = lax.broadcasted_iota(jnp.int32, (per_group, tm), 0).astype(F32)
    group_score = []
    for g in range(N_GROUPS):
        sg = sel[g * per_group:(g + 1) * per_group, :]
        m1 = jnp.max(sg, axis=0, keepdims=True)
        first = jnp.min(jnp.where(sg == m1, sub_iota, float(per_group)), axis=0, keepdims=True)
        m2 = jnp.max(jnp.where(sub_iota == first, neg_inf, sg), axis=0, keepdims=True)
        group_score.append(m1 + m2)
    masked = []
    for g in range(N_GROUPS):
        rank = jnp.zeros((1, tm), F32)
        for g2 in range(N_GROUPS):
            if g2 == g:
                continue
            ahead = (group_score[g2] >= group_score[g]) if g2 < g else (group_score[g2] > group_score[g])
            rank = rank + jnp.where(ahead, 1.0, 0.0)
        keep = rank < float(TOP_GROUPS)
        masked.append(jnp.where(keep, sel[g * per_group:(g + 1) * per_group, :], neg_inf))
    cur = jnp.concatenate(masked, axis=0)
    iota = lax.broadcasted_iota(jnp.int32, (n_exp, tm), 0).astype(F32)
    chosen = jnp.zeros((n_exp, tm), F32)
    picks, weights = [], []
    for _ in range(TOP_K):
        m = jnp.max(cur, axis=0, keepdims=True)
        idx = jnp.min(jnp.where(cur == m, iota, float(n_exp)), axis=0, keepdims=True)
        pick = iota == idx
        chosen = jnp.where(pick, 1.0, chosen)
        cur = jnp.where(pick, neg_inf, cur)
        picks.append(idx)
        weights.append(jnp.sum(jnp.where(pick, scores, 0.0), axis=0, keepdims=True))
    idx_k = jnp.concatenate(picks, axis=0)
    w_k = jnp.concatenate(weights, axis=0)
    gate_k = w_k / jnp.sum(w_k, axis=0, keepdims=True) * ROUTED_SCALE
    return idx_k, gate_k, chosen


def _pack_bf16_pairs(x):
    n = x.shape[1] // 2
    hi = lax.bitcast_convert_type(x[:, :n].astype(BF16).astype(F32), jnp.int32)
    lo = lax.bitcast_convert_type(x[:, n:].astype(BF16).astype(F32), jnp.int32)
    return hi | lax.shift_right_logical(lo, 16)


def _unpack_bf16_pairs(p):
    hi = lax.bitcast_convert_type(p & jnp.int32(-65536), F32).astype(BF16)
    lo = lax.bitcast_convert_type(lax.shift_left(p, 16), F32).astype(BF16)
    return hi, lo


def _mix_kernel(att_ref, ssm_ref, xn_ref, ag_ref, sg_ref, woa_ref, wos_ref,
                g_ref, b_ref, wrh_ref, wrl_ref, rb_ref,
                x1_ref, x1p_ref, idx_ref, gate_ref, rank_ref, cnt_ref, carry_ref, before_ref):
    @pl.when(pl.program_id(0) == 0)
    def _():
        carry_ref[...] = jnp.zeros_like(carry_ref)
        tm = before_ref.shape[0]
        before = (lax.broadcasted_iota(jnp.int32, (tm, tm), 0)
                  < lax.broadcasted_iota(jnp.int32, (tm, tm), 1))
        before_ref[...] = jnp.where(before, 1.0, 0.0).astype(BF16)

    an = _rms_norm(att_ref[...].astype(F32), ag_ref[...]).astype(BF16)
    sn = _rms_norm(ssm_ref[...].astype(F32), sg_ref[...]).astype(BF16)
    mix = _dot(an, woa_ref[...]) + _dot(sn, wos_ref[...])
    x1 = _layer_norm(ALPHA * xn_ref[...] + mix, g_ref[...], b_ref[...])
    x1_ref[...] = x1
    x1p_ref[...] = _pack_bf16_pairs(x1)
    x_hi = x1.astype(BF16)
    x_lo = (x1 - x_hi.astype(F32)).astype(BF16)
    dg = functools.partial(lax.dot_general, dimension_numbers=NT_DIMS, preferred_element_type=F32)
    logits_t = dg(wrh_ref[...], x_hi) + dg(wrl_ref[...], x_hi) + dg(wrh_ref[...], x_lo)
    idx_k, gate_k, chosen = _router_gates(logits_t, rb_ref[...])
    idx_ref[...] = idx_k.astype(jnp.int32)
    gate_ref[...] = gate_k.T

    n_exp, tm = chosen.shape
    excl = _dot(chosen.astype(BF16), before_ref[...])
    rank_full = carry_ref[...] + excl
    iota = lax.broadcasted_iota(jnp.int32, (n_exp, tm), 0).astype(F32)
    ranks = [jnp.sum(jnp.where(iota == idx_k[k:k + 1, :], rank_full, 0.0), axis=0, keepdims=True)
             for k in range(TOP_K)]
    rank_ref[...] = jnp.concatenate(ranks, axis=0).astype(jnp.int32)
    total = carry_ref[...] + jnp.sum(chosen, axis=1, keepdims=True)
    carry_ref[...] = total
    cnt_ref[...] = jnp.broadcast_to(total, cnt_ref.shape)


def _positions_kernel(off_ref, idx_ref, rank_ref, pos_ref):
    idx = idx_ref[...]
    base = jnp.zeros(idx.shape, jnp.int32)
    for e in range(N_EXPERTS):
        base = jnp.where(idx == e, off_ref[e], base)
    pos_ref[...] = rank_ref[...] + base


def _swiglu(x_hi, x_lo, wg_ref, wu_ref, wd_ref):
    half = x_hi.shape[1]
    hg = _dot(x_hi, wg_ref[:half, :]) + _dot(x_lo, wg_ref[half:, :])
    hu = _dot(x_hi, wu_ref[:half, :]) + _dot(x_lo, wu_ref[half:, :])
    h = jax.nn.silu(hg) * hu
    return _dot(h.astype(BF16), wd_ref[...])


def _experts_kernel(te_ref, valid_ref, xs_ref, wg_ref, wu_ref, wd_ref, ys_ref,
                    wgb_ref, wub_ref, wdb_ref):
    i = pl.program_id(0)
    valid = valid_ref[i]

    @pl.when((i == 0) | (te_ref[i] != te_ref[jnp.maximum(i - 1, 0)]))
    def _():
        wgb_ref[...] = wg_ref[0].astype(BF16)
        wub_ref[...] = wu_ref[0].astype(BF16)
        wdb_ref[...] = wd_ref[0].astype(BF16)

    @pl.when(valid > 0)
    def _():
        rows = lax.broadcasted_iota(jnp.int32, xs_ref.shape, 0)
        x_hi, x_lo = _unpack_bf16_pairs(jnp.where(rows < valid, xs_ref[...], 0))
        ys_ref[...] = _pack_bf16_pairs(_swiglu(x_hi, x_lo, wgb_ref, wub_ref, wdb_ref))

    @pl.when(valid == 0)
    def _():
        ys_ref[...] = jnp.zeros_like(ys_ref)


def _combine_kernel(yg_ref, gate_ref, x1_ref, wsg_ref, wsu_ref, wsd_ref, g_ref, b_ref, o_ref):
    x1 = x1_ref[...]
    half = x1.shape[1] // 2
    acc = _swiglu(x1[:, :half].astype(BF16), x1[:, half:].astype(BF16), wsg_ref, wsu_ref, wsd_ref)
    gates = gate_ref[...]
    acc_hi, acc_lo = acc[:, :half], acc[:, half:]
    for k in range(TOP_K):
        y_hi, y_lo = _unpack_bf16_pairs(yg_ref[k])
        gk = gates[:, k:k + 1]
        acc_hi = acc_hi + gk * y_hi.astype(F32)
        acc_lo = acc_lo + gk * y_lo.astype(F32)
    ffn = jnp.concatenate([acc_hi, acc_lo], axis=1)
    o_ref[...] = _layer_norm(ALPHA * x1 + ffn, g_ref[...], b_ref[...])


def _combine_into_kernel(prev_ref, *refs):
    del prev_ref
    _combine_kernel(*refs)


SC_CORES = 2
SC_SUBCORES = 16
SC_WORKERS = SC_CORES * SC_SUBCORES
SC_CHUNK = 64


def _sc_mesh():
    return plsc.VectorSubcoreMesh(core_axis_name="c", subcore_axis_name="s")


def _sc_gather_rows(table, idx):
    n = idx.shape[0]
    d = table.shape[1]
    assert n % (SC_WORKERS * 2 * SC_CHUNK) == 0, n
    per_w = n // SC_WORKERS
    n_ch = per_w // SC_CHUNK

    @functools.partial(
        pl.kernel, mesh=_sc_mesh(),
        out_type=jax.ShapeDtypeStruct((n, d), table.dtype),
        scratch_types=[pltpu.VMEM((n_ch, SC_CHUNK), jnp.int32),
                       pltpu.VMEM((2, SC_CHUNK, d), table.dtype),
                       pltpu.SemaphoreType.DMA((2,)),
                       pltpu.SemaphoreType.DMA((2,))],
    )
    def k(table_hbm, idx_hbm, out_hbm, idx_v, buf, gsem, osem):
        wid = lax.axis_index("s") * SC_CORES + lax.axis_index("c")
        base = wid * per_w
        pltpu.sync_copy(idx_hbm.at[wid], idx_v)

        def gather(c, b):
            return pltpu.make_async_copy(table_hbm.at[idx_v.at[c]], buf.at[b], gsem.at[b])

        def put(c, b):
            return pltpu.make_async_copy(buf.at[b], out_hbm.at[pl.ds(base + c * SC_CHUNK, SC_CHUNK)],
                                         osem.at[b])

        gather(0, 0).start()

        @pl.loop(0, n_ch, step=2)
        def _(c):
            for b in range(2):
                cc = c + b
                gather(cc, b).wait()

                @pl.when(cc + 1 < n_ch)
                def _():
                    @pl.when(cc >= 1)
                    def _():
                        put(cc - 1, 1 - b).wait()
                    gather(cc + 1, 1 - b).start()

                put(cc, b).start()

        put(n_ch - 2, 0).wait()
        put(n_ch - 1, 1).wait()

    return k(table, idx.reshape(SC_WORKERS, n_ch, SC_CHUNK))


def _sc_scatter_rows(x, pos, n_out):
    t, d = x.shape
    kk = pos.shape[0]
    assert t % (SC_WORKERS * 2 * SC_CHUNK) == 0, t
    per_w = t // SC_WORKERS
    n_ch = per_w // SC_CHUNK
    pos_w = pos.reshape(kk, SC_WORKERS, n_ch, SC_CHUNK).transpose(1, 2, 0, 3)
    pos_w = pos_w.reshape(SC_WORKERS, n_ch * kk, SC_CHUNK)

    @functools.partial(
        pl.kernel, mesh=_sc_mesh(),
        out_type=jax.ShapeDtypeStruct((n_out, d), x.dtype),
        scratch_types=[pltpu.VMEM((n_ch * kk, SC_CHUNK), jnp.int32),
                       pltpu.VMEM((2, SC_CHUNK, d), x.dtype),
                       pltpu.SemaphoreType.DMA((2,)),
                       pltpu.SemaphoreType.DMA((2,))],
    )
    def k(x_hbm, pos_hbm, out_hbm, idx_v, buf, isem, osem):
        wid = lax.axis_index("s") * SC_CORES + lax.axis_index("c")
        base = wid * per_w
        pltpu.sync_copy(pos_hbm.at[wid], idx_v)

        def get(c, b):
            return pltpu.make_async_copy(x_hbm.at[pl.ds(base + c * SC_CHUNK, SC_CHUNK)], buf.at[b],
                                         isem.at[b])

        def put(c, j, b):
            return pltpu.make_async_copy(buf.at[b], out_hbm.at[idx_v.at[c * kk + j]], osem.at[b])

        get(0, 0).start()

        @pl.loop(0, n_ch, step=2)
        def _(c):
            for b in range(2):
                cc = c + b
                get(cc, b).wait()

                @pl.when(cc + 1 < n_ch)
                def _():
                    @pl.when(cc >= 1)
                    def _():
                        for j in range(kk):
                            put(cc - 1, j, 1 - b).wait()
                    get(cc + 1, 1 - b).start()

                for j in range(kk):
                    put(cc, j, b).start()

        for j in range(kk):
            put(n_ch - 2, j, 0).wait()
        for j in range(kk):
            put(n_ch - 1, j, 1).wait()

    return k(x, pos_w)


def _row(v):
    return v.reshape(1, -1).astype(F32)


def _const_spec(shape):
    nd = len(shape)
    return pl.BlockSpec(shape, lambda *_: (0,) * nd)


def _pad_heads(w, width):
    r, h, _ = w.shape
    return jnp.pad(w, ((0, 0), (0, 0), (0, HEAD_PAD - width))).reshape(r, h * HEAD_PAD)


def _half_rotate(w):
    half = QK_ROPE // 2
    return jnp.concatenate([-w[..., half:], w[..., :half]], axis=-1)


def kernel(x, positions, ln_in_g, ln_in_b, w_in, q_norm_g, kv_norm_g, w_uq, w_ukv, lambda_re, lambda_im, log_step, b_re, b_im, c_re, c_im, d_skip, w_glu, b_glu, attn_out_g, ssm_out_g, w_o, ln1_g, ln1_b, w_router, router_bias, w_gate, w_up, w_down, ws_gate, ws_up, ws_down, ln2_g, ln2_b):
    B, S, D = x.shape
    T = B * S
    assert DEPTH == 1 and w_in.shape[0] == DEPTH
    assert S % QUERY_TILE == 0 and S % TOKEN_TILE == 0 and S % S5_TIME_TILE == 0, S
    assert T % (MOE_SLABS * TOKEN_TILE) == 0 and (T // MOE_SLABS * TOP_K) % EXPERT_ROW_TILE == 0, T
    l = 0
    ssm_width = w_glu.shape[-1]
    n_groups = ssm_width // SSM_GROUP
    n_state = n_groups * SSM_STATE
    mla_width = MLA_HEADS * V_DIM
    qk_pad = MLA_HEADS * HEAD_PAD
    cparams = functools.partial(pltpu.CompilerParams, vmem_limit_bytes=VMEM_LIMIT)

    s1, s2, s3 = Q_RANK, Q_RANK + KV_RANK, Q_RANK + KV_RANK + QK_ROPE
    wi = w_in[l]
    w_kr = wi[:, s2:s3]
    pad_rope = lambda w: jnp.pad(w, ((0, 0), (QK_NOPE, HEAD_PAD - QK_NOPE - QK_ROPE)))
    w1 = jnp.concatenate([wi[:, :s2], wi[:, s3:], pad_rope(w_kr), pad_rope(_half_rotate(w_kr))],
                         axis=1).astype(BF16)
    wq = w_uq[l]
    zeros_nope = jnp.zeros(wq.shape[:2] + (QK_NOPE,), wq.dtype)
    wq_main = _pad_heads(wq, QK_NOPE + QK_ROPE).astype(BF16)
    wq_rot = _pad_heads(jnp.concatenate([zeros_nope, _half_rotate(wq[..., QK_NOPE:])], axis=-1),
                        QK_NOPE + QK_ROPE).astype(BF16)
    wkv = w_ukv[l]
    wk = _pad_heads(wkv[..., :QK_NOPE], QK_NOPE).astype(BF16)
    wv = _pad_heads(wkv[..., QK_NOPE:], V_DIM).astype(BF16)
    half = QK_ROPE // 2
    inv_freq = ROPE_THETA ** (-jnp.arange(half, dtype=F32) / half)
    freq = jnp.concatenate([inv_freq, inv_freq]).reshape(QK_ROPE, 1)
    pos_f = positions.astype(F32).reshape(1, T)

    tm = min(TOKEN_TILE, T)
    w1_cols = w1.shape[1]
    tok = lambda width: pl.BlockSpec((tm, width), lambda i: (i, 0))
    xn, q, k, v, u = pl.pallas_call(
        functools.partial(_inproj_kernel, ssm_width=ssm_width),
        grid=(T // tm,),
        in_specs=[tok(D), pl.BlockSpec((1, tm), lambda i: (0, i)), _const_spec((1, D)), _const_spec((1, D)),
                  _const_spec((D, w1_cols)), _const_spec((1, Q_RANK)), _const_spec((1, KV_RANK)),
                  _const_spec((Q_RANK, qk_pad)), _const_spec((Q_RANK, qk_pad)),
                  _const_spec((KV_RANK, qk_pad)), _const_spec((KV_RANK, qk_pad)),
                  _const_spec((QK_ROPE, 1))],
        out_specs=[tok(D), tok(qk_pad), tok(qk_pad), tok(qk_pad), tok(ssm_width)],
        out_shape=[jax.ShapeDtypeStruct((T, D), F32), jax.ShapeDtypeStruct((T, qk_pad), BF16),
                   jax.ShapeDtypeStruct((T, qk_pad), BF16), jax.ShapeDtypeStruct((T, qk_pad), BF16),
                   jax.ShapeDtypeStruct((T, ssm_width), F32)],
        compiler_params=cparams(dimension_semantics=("parallel",)),
        name="inproj",
    )(x.reshape(T, D), pos_f, _row(ln_in_g), _row(ln_in_b), w1, _row(q_norm_g[l]), _row(kv_norm_g[l]),
      wq_main, wq_rot, wk, wv, freq)

    tq = min(QUERY_TILE, S)
    nq = S // tq
    att = pl.pallas_call(
        functools.partial(_attn_kernel, tq=tq, heads=MLA_HEADS, n_qtiles=nq),
        grid=(B, nq),
        in_specs=[pl.BlockSpec((tq, qk_pad), lambda b, i: (b * nq + i, 0)),
                  pl.BlockSpec((S, qk_pad), lambda b, i: (b, 0)),
                  pl.BlockSpec((S, qk_pad), lambda b, i: (b, 0))],
        out_specs=pl.BlockSpec((tq, mla_width), lambda b, i: (b * nq + i, 0)),
        out_shape=jax.ShapeDtypeStruct((T, mla_width), BF16),
        compiler_params=cparams(dimension_semantics=("parallel", "arbitrary")),
        name="attention",
    )(q, k, v)

    lam = lax.complex(jnp.minimum(lambda_re[l].astype(F32), -1e-4), lambda_im[l].astype(F32))
    step = jnp.exp(log_step[l].astype(F32))[:, None]
    lam_bar = jnp.exp(lam * step)
    b_bar = ((lam_bar - 1.0) / lam)[..., None] * lax.complex(b_re[l].astype(F32), b_im[l].astype(F32))
    n_slab = ssm_width // LANES
    g_per_slab = n_groups // n_slab
    slab = n_state // n_slab
    eye = jnp.eye(g_per_slab, dtype=F32)

    def expand_in(bpart):
        bt = bpart.transpose(0, 2, 1).reshape(n_slab, g_per_slab, SSM_GROUP, 1, SSM_STATE)
        return (bt * eye[None, :, None, :, None]).reshape(n_slab, LANES, slab)

    def expand_out(cpart):
        ct = cpart.transpose(0, 2, 1).reshape(n_slab, g_per_slab, SSM_STATE, 1, SSM_GROUP)
        return (ct * eye[None, :, None, :, None]).reshape(n_slab, slab, LANES)

    win = jnp.concatenate([expand_in(jnp.real(b_bar)), expand_in(jnp.imag(b_bar))],
                          axis=2).astype(BF16)
    cre = expand_out(c_re[l].astype(F32)).astype(BF16)
    cim = expand_out(-c_im[l].astype(F32)).astype(BF16)
    a_re = jnp.real(lam_bar).reshape(1, n_state)
    a_im = jnp.imag(lam_bar).reshape(1, n_state)

    lt = min(S5_TIME_TILE, S)
    ssm = pl.pallas_call(
        functools.partial(_s5_kernel, batch=B, lt=lt, n_state=n_state),
        grid=(S // lt,),
        in_specs=[pl.BlockSpec((B, lt, ssm_width), lambda t: (0, t, 0)),
                  _const_spec(win.shape), _const_spec((1, n_state)), _const_spec((1, n_state)),
                  _const_spec(cre.shape), _const_spec(cim.shape), _const_spec((1, ssm_width)),
                  _const_spec((ssm_width, ssm_width)), _const_spec((1, ssm_width))],
        out_specs=pl.BlockSpec((B, lt, ssm_width), lambda t: (0, t, 0)),
        out_shape=jax.ShapeDtypeStruct((B, S, ssm_width), BF16),
        scratch_shapes=[pltpu.VMEM((2 * n_state // LANES, B * lt, LANES), F32),
                        pltpu.VMEM((n_state // LANES, B, LANES), F32),
                        pltpu.VMEM((n_state // LANES, B, LANES), F32),
                        pltpu.VMEM((ssm_width // LANES, B * lt, LANES), F32),
                        pltpu.VMEM((B * lt, ssm_width), F32)],
        compiler_params=cparams(dimension_semantics=("arbitrary",)),
        name="s5",
    )(u.reshape(B, S, ssm_width), win, a_re, a_im, cre, cim, _row(d_skip[l]),
      w_glu[l].astype(BF16), _row(b_glu[l]))

    wo = w_o[l].astype(BF16)
    assert mla_width == ssm_width
    wr_t = w_router[l].T.astype(F32)
    wr_hi = wr_t.astype(BF16)
    wr_lo = (wr_t - wr_hi.astype(F32)).astype(BF16)
    half = D // 2
    n_slabs = MOE_SLABS
    ts = T // n_slabs
    nt = ts // tm
    ssm2 = ssm.reshape(T, ssm_width)
    rbias = router_bias[l].astype(F32).reshape(N_EXPERTS, 1)
    kt = lambda dt: jax.ShapeDtypeStruct((TOP_K, ts), dt)
    k_spec = pl.BlockSpec((TOP_K, tm), lambda i: (0, i))

    def route(s):
        tok_s = lambda width: pl.BlockSpec((tm, width), lambda i: (i + s * nt, 0))
        return pl.pallas_call(
            _mix_kernel,
            grid=(nt,),
            in_specs=[tok_s(mla_width), tok_s(ssm_width), tok_s(D), _const_spec((1, mla_width)),
                      _const_spec((1, ssm_width)), pl.BlockSpec((mla_width, D), lambda i: (0, 0)),
                      pl.BlockSpec((ssm_width, D), lambda i: (1, 0)), _const_spec((1, D)), _const_spec((1, D)),
                      _const_spec((N_EXPERTS, D)), _const_spec((N_EXPERTS, D)),
                      _const_spec((N_EXPERTS, 1))],
            out_specs=[tok(D), tok(half), k_spec, tok(TOP_K), k_spec, _const_spec((N_EXPERTS, LANES))],
            out_shape=[jax.ShapeDtypeStruct((ts, D), F32), jax.ShapeDtypeStruct((ts, half), jnp.int32),
                       kt(jnp.int32), jax.ShapeDtypeStruct((ts, TOP_K), F32), kt(jnp.int32),
                       jax.ShapeDtypeStruct((N_EXPERTS, LANES), F32)],
            scratch_shapes=[pltpu.VMEM((N_EXPERTS, 1), F32), pltpu.VMEM((tm, tm), BF16)],
            compiler_params=cparams(dimension_semantics=("arbitrary",)),
            name="mix_router",
        )(att, ssm2, xn, _row(attn_out_g[l]), _row(ssm_out_g[l]), wo, wo,
          _row(ln1_g[l]), _row(ln1_b[l]), wr_hi, wr_lo, rbias)

    tr = EXPERT_ROW_TILE
    n_tiles = (ts * TOP_K) // tr + N_EXPERTS
    n_rows = n_tiles * tr
    tp = min(POSITIONS_TILE, ts)

    def dispatch(x1p, idx_k, rank_k, counts):
        cnt = counts[:, 0].astype(jnp.int32)
        tiles_e = (cnt + tr - 1) // tr
        tile_end = jnp.cumsum(tiles_e)
        tile_start = tile_end - tiles_e
        tile_ids = jnp.arange(n_tiles, dtype=jnp.int32)
        tile_expert = jnp.sum((tile_end[None, :] <= tile_ids[:, None]).astype(jnp.int32), axis=1)
        tile_expert = jnp.minimum(tile_expert, N_EXPERTS - 1)
        owner = (tile_start[None, :] <= tile_ids[:, None]) & (tile_ids[:, None] < tile_end[None, :])
        left = jnp.sum(jnp.where(owner, cnt[None, :] - (tile_ids[:, None] - tile_start[None, :]) * tr, 0),
                       axis=1)
        tile_valid = jnp.clip(left, 0, tr).astype(jnp.int32)
        pos = pl.pallas_call(
            _positions_kernel,
            grid_spec=pltpu.PrefetchScalarGridSpec(
                num_scalar_prefetch=1, grid=(ts // tp,),
                in_specs=[pl.BlockSpec((TOP_K, tp), lambda i, off: (0, i)),
                          pl.BlockSpec((TOP_K, tp), lambda i, off: (0, i))],
                out_specs=pl.BlockSpec((TOP_K, tp), lambda i, off: (0, i))),
            out_shape=kt(jnp.int32),
            name="positions",
        )((tile_start * tr).astype(jnp.int32), idx_k, rank_k)
        return _sc_scatter_rows(x1p, pos, n_rows), pos, tile_expert, tile_valid

    wg, wu, wd = w_gate[l], w_up[l], w_down[l]
    ff = wg.shape[-1]

    def experts(xs, tile_expert, tile_valid):
        return pl.pallas_call(
            _experts_kernel,
            grid_spec=pltpu.PrefetchScalarGridSpec(
                num_scalar_prefetch=2, grid=(n_tiles,),
                in_specs=[pl.BlockSpec((tr, half), lambda i, te, tv: (i, 0)),
                          pl.BlockSpec((1, D, ff), lambda i, te, tv: (te[i], 0, 0)),
                          pl.BlockSpec((1, D, ff), lambda i, te, tv: (te[i], 0, 0)),
                          pl.BlockSpec((1, ff, D), lambda i, te, tv: (te[i], 0, 0))],
                out_specs=pl.BlockSpec((tr, half), lambda i, te, tv: (i, 0)),
                scratch_shapes=[pltpu.VMEM((D, ff), BF16), pltpu.VMEM((D, ff), BF16),
                                pltpu.VMEM((ff, D), BF16)]),
            out_shape=jax.ShapeDtypeStruct((n_rows, half), jnp.int32),
            compiler_params=cparams(dimension_semantics=("arbitrary",)),
            name="experts",
        )(tile_expert, tile_valid, xs, wg, wu, wd)

    shared = (ws_gate[l].astype(BF16), ws_up[l].astype(BF16), ws_down[l].astype(BF16))

    def combine(s, out_so_far, yg, gate_k, x1):
        specs = [pl.BlockSpec((TOP_K, tm, half), lambda i: (0, i, 0)),
                 pl.BlockSpec((tm, TOP_K), lambda i: (i, 0)), tok(D),
                 _const_spec((D, ff)), _const_spec((D, ff)), _const_spec((ff, D)),
                 _const_spec((1, D)), _const_spec((1, D))]
        args = (yg.reshape(TOP_K, ts, half), gate_k, x1, *shared, _row(ln2_g[l]), _row(ln2_b[l]))
        body, aliases = _combine_kernel, {}
        if out_so_far is not None:
            specs = [pl.BlockSpec(memory_space=pl.ANY)] + specs
            args = (out_so_far,) + args
            body, aliases = _combine_into_kernel, {0: 0}
        return pl.pallas_call(
            body,
            grid=(nt,),
            in_specs=specs,
            out_specs=pl.BlockSpec((tm, D), lambda i: (i + s * nt, 0)),
            out_shape=jax.ShapeDtypeStruct((T, D), F32),
            input_output_aliases=aliases,
            compiler_params=cparams(dimension_semantics=("parallel",)),
            name="combine",
        )(*args)

    routed, moved = [], []
    for s in range(n_slabs):
        x1, x1p, idx_k, gate_k, rank_k, counts = route(s)
        routed.append((x1, gate_k))
        moved.append(dispatch(x1p, idx_k, rank_k, counts))
    gathered = []
    for xs, pos, tile_expert, tile_valid in moved:
        gathered.append(_sc_gather_rows(experts(xs, tile_expert, tile_valid), pos.reshape(TOP_K * ts)))
    out = None
    for s in range(n_slabs):
        out = combine(s, out, gathered[s], routed[s][1], routed[s][0])
    return out.reshape(B, S, D)
```

```python
import functools

import jax
import jax.numpy as jnp
from jax import lax
from jax.experimental import pallas as pl
from jax.experimental.pallas import tpu as pltpu
from jax.experimental.pallas import tpu_sc as plsc

CHUNK = 64
MLA_HEADS = 8
QK_NOPE = 64
QK_ROPE = 32
V_DIM = 64
Q_RANK = 256
KV_RANK = 128
ROPE_THETA = 10000.0
SSM_GROUP = 16
SSM_STATE = 64
N_EXPERTS = 64
TOP_K = 8
N_GROUPS = 8
TOP_GROUPS = 4
ROUTED_SCALE = 2.5
DEPTH = 1
ALPHA = (2.0 * DEPTH) ** 0.25
EPS = 1e-5
LOG2_E = 1.4426950408889634

LANES = 128
HEAD_PAD = LANES
VMEM_LIMIT = 56 * 1024 * 1024
TOKEN_TILE = 512
QUERY_TILE = 256
S5_TIME_TILE = 64
POSITIONS_TILE = 2048
EXPERT_ROW_TILE = 1024
MOE_SLABS = 2

BF16 = jnp.bfloat16
F32 = jnp.float32
NT_DIMS = (((1,), (1,)), ((), ()))


def _dot(a, b):
    return jnp.dot(a, b, preferred_element_type=F32)


def _layer_norm(x, g, b):
    mu = jnp.mean(x, axis=-1, keepdims=True)
    xc = x - mu
    var = jnp.mean(xc * xc, axis=-1, keepdims=True)
    return xc * lax.rsqrt(var + EPS) * g + b


def _rms_norm(x, g):
    return x * lax.rsqrt(jnp.mean(x * x, axis=-1, keepdims=True) + EPS) * g


def _inproj_kernel(x_ref, pos_ref, lng_ref, lnb_ref, w1_ref, qg_ref, kvg_ref,
                   wq_ref, wqr_ref, wk_ref, wv_ref, freq_ref,
                   xn_ref, q_ref, k_ref, v_ref, u_ref, *, ssm_width):
    xn = _layer_norm(x_ref[...], lng_ref[...], lnb_ref[...])
    xn_ref[...] = xn
    h = _dot(xn.astype(BF16), w1_ref[...])
    o1 = Q_RANK
    o2 = o1 + KV_RANK
    o3 = o2 + ssm_width
    o4 = o3 + HEAD_PAD
    cq = h[:, :o1]
    ckv = h[:, o1:o2]
    u_ref[...] = h[:, o2:o3].astype(u_ref.dtype)
    kr_raw = h[:, o3:o4]
    kr_rot = h[:, o4:o4 + HEAD_PAD]
    cqn = _rms_norm(cq, qg_ref[...]).astype(BF16)
    ckvn = _rms_norm(ckv, kvg_ref[...]).astype(BF16)

    tm = x_ref.shape[0]
    ang_t = freq_ref[...] * pos_ref[...]

    def to_token_rows(t):
        padded = jnp.concatenate([jnp.zeros((QK_NOPE, tm), F32), t,
                                  jnp.zeros((HEAD_PAD - QK_NOPE - QK_ROPE, tm), F32)], axis=0)
        return padded.T

    c = to_token_rows(jnp.cos(ang_t))
    s = to_token_rows(jnp.sin(ang_t))
    lane = lax.broadcasted_iota(jnp.int32, (1, HEAD_PAD), 1)
    scale = (QK_NOPE + QK_ROPE) ** -0.5 * LOG2_E
    cos1 = (c + jnp.where(lane < QK_NOPE, 1.0, 0.0)) * scale
    sin1 = s * scale
    cos_t = jnp.concatenate([cos1] * MLA_HEADS, axis=1)
    sin_t = jnp.concatenate([sin1] * MLA_HEADS, axis=1)
    q = _dot(cqn, wq_ref[...]) * cos_t + _dot(cqn, wqr_ref[...]) * sin_t
    q_ref[...] = q.astype(BF16)

    kr = kr_raw * c + kr_rot * s
    k = _dot(ckvn, wk_ref[...]) + jnp.concatenate([kr] * MLA_HEADS, axis=1)
    k_ref[...] = k.astype(BF16)
    ones_col = jnp.concatenate([jnp.where(lane == V_DIM, 1.0, 0.0)] * MLA_HEADS, axis=1)
    v_ref[...] = (_dot(ckvn, wv_ref[...]) + ones_col).astype(BF16)


def _attn_kernel(q_ref, k_ref, v_ref, o_ref, *, tq, heads, n_qtiles):
    qi = pl.program_id(1)
    row_chunk = lax.broadcasted_iota(jnp.int32, (tq, tq), 0) // CHUNK
    col_chunk = lax.broadcasted_iota(jnp.int32, (tq, tq), 1) // CHUNK
    diag_mask = row_chunk >= col_chunk

    def tile(n_blocks):
        keys = n_blocks * tq
        cols = [slice(h * HEAD_PAD, (h + 1) * HEAD_PAD) for h in range(heads)]

        def mask_diag(s):
            s_diag = jnp.where(diag_mask, s[:, keys - tq:], -jnp.inf)
            return s_diag if n_blocks == 1 else jnp.concatenate([s[:, :keys - tq], s_diag], axis=1)

        ss = [lax.dot_general(q_ref[:, c], k_ref[:keys, c], NT_DIMS, preferred_element_type=F32)
              for c in cols]
        ss = [mask_diag(s) for s in ss]
        ms = [jnp.max(s, axis=-1, keepdims=True) for s in ss]
        ps = [jnp.exp2(s - m).astype(BF16) for s, m in zip(ss, ms)]
        accs = [_dot(p, v_ref[:keys, c]) for p, c in zip(ps, cols)]
        outs = [acc[:, :V_DIM] / acc[:, V_DIM:V_DIM + 1] for acc in accs]
        o_ref[...] = jnp.concatenate(outs, axis=1).astype(o_ref.dtype)

    for c in range(n_qtiles):
        pl.when(qi == c)(functools.partial(tile, c + 1))


def _s5_kernel(u_ref, win_ref, are_ref, aim_ref, cre_ref, cim_ref, dskip_ref,
               wglu_ref, bglu_ref, o_ref, vx_ref, hre_ref, him_ref, io_ref, utm_ref, *,
               batch, lt, n_state):
    ti = pl.program_id(0)

    @pl.when(ti == 0)
    def _():
        hre_ref[...] = jnp.zeros_like(hre_ref)
        him_ref[...] = jnp.zeros_like(him_ref)

    width = u_ref.shape[-1]
    n_slab = width // LANES
    n_tiles = n_state // LANES
    slab_tiles = n_tiles // n_slab
    for b in range(batch):
        for c in range(n_slab):
            io_ref[c, b * lt:(b + 1) * lt, :] = u_ref[b, :, c * LANES:(c + 1) * LANES].astype(F32)

    def slab_tiles_of(j):
        return range(j * slab_tiles, (j + 1) * slab_tiles)

    def expand(j):
        for t in range(lt):
            utm_ref[t * batch:(t + 1) * batch, j * LANES:(j + 1) * LANES] = (
                io_ref[j, pl.ds(t, batch, stride=lt), :])
        ub = utm_ref[:, j * LANES:(j + 1) * LANES].astype(BF16)
        vj = _dot(ub, win_ref[j])
        for i, c in enumerate(slab_tiles_of(j)):
            vx_ref[c] = vj[:, i * LANES:(i + 1) * LANES]
            vx_ref[n_tiles + c] = vj[:, (slab_tiles + i) * LANES:(slab_tiles + i + 1) * LANES]

    def scan(j):
        tiles = slab_tiles_of(j)
        ar = [jnp.broadcast_to(are_ref[:, c * LANES:(c + 1) * LANES], (batch, LANES)) for c in tiles]
        ai = [jnp.broadcast_to(aim_ref[:, c * LANES:(c + 1) * LANES], (batch, LANES)) for c in tiles]
        hr = [hre_ref[c] for c in tiles]
        hi = [him_ref[c] for c in tiles]
        for t in range(lt):
            rows = slice(t * batch, (t + 1) * batch)
            for n, c in enumerate(tiles):
                nr = ar[n] * hr[n] - ai[n] * hi[n] + vx_ref[c, rows, :]
                ni = ar[n] * hi[n] + ai[n] * hr[n] + vx_ref[n_tiles + c, rows, :]
                vx_ref[c, rows, :] = nr
                vx_ref[n_tiles + c, rows, :] = ni
                hr[n], hi[n] = nr, ni
        for n, c in enumerate(tiles):
            hre_ref[c] = hr[n]
            him_ref[c] = hi[n]

    def project(j):
        tiles = slab_tiles_of(j)
        xr = jnp.concatenate([vx_ref[c].astype(BF16) for c in tiles], axis=1)
        xi = jnp.concatenate([vx_ref[n_tiles + c].astype(BF16) for c in tiles], axis=1)
        return _dot(xr, cre_ref[j]) + _dot(xi, cim_ref[j])

    ys = [None] * n_slab
    for j in range(n_slab + 2):
        if j < n_slab:
            expand(j)
        if 1 <= j <= n_slab:
            scan(j - 1)
        if j >= 2:
            ys[j - 2] = project(j - 2)
    y = jnp.concatenate(ys, axis=1) + dskip_ref[...] * utm_ref[...]
    y = jax.nn.gelu(y)
    z = _dot(y.astype(BF16), wglu_ref[...]) + bglu_ref[...]
    out = y * jax.nn.sigmoid(z)
    for c in range(n_slab):
        io_ref[c] = out[:, c * LANES:(c + 1) * LANES]
    for b in range(batch):
        for c in range(n_slab):
            o_ref[b, :, c * LANES:(c + 1) * LANES] = io_ref[c, pl.ds(b, lt, stride=batch), :].astype(o_ref.dtype)


def _router_gates(logits_t, rbias):
    n_exp, tm = logits_t.shape
    per_group = n_exp // N_GROUPS
    scores = jax.nn.sigmoid(logits_t)
    sel = scores + rbias
    neg_inf = -jnp.inf
    sub_iota = lax.broadcasted_iota(jnp.int32, (per_group, tm), 0).astype(F32)
    group_score = []
    for g in range(N_GROUPS):
        sg = sel[g * per_group:(g + 1) * per_group, :]
        m1 = jnp.max(sg, axis=0, keepdims=True)
        first = jnp.min(jnp.where(sg == m1, sub_iota, float(per_group)), axis=0, keepdims=True)
        m2 = jnp.max(jnp.where(sub_iota == first, neg_inf, sg), axis=0, keepdims=True)
        group_score.append(m1 + m2)
    masked = []
    for g in range(N_GROUPS):
        rank = jnp.zeros((1, tm), F32)
        for g2 in range(N_GROUPS):
            if g2 == g:
                continue
            ahead = (group_score[g2] >= group_score[g]) if g2 < g else (group_score[g2] > group_score[g])
            rank = rank + jnp.where(ahead, 1.0, 0.0)
        keep = rank < float(TOP_GROUPS)
        masked.append(jnp.where(keep, sel[g * per_group:(g + 1) * per_group, :], neg_inf))
    cur = jnp.concatenate(masked, axis=0)
    iota = lax.broadcasted_iota(jnp.int32, (n_exp, tm), 0).astype(F32)
    chosen = jnp.zeros((n_exp, tm), F32)
    picks, weights = [], []
    for _ in range(TOP_K):
        m = jnp.max(cur, axis=0, keepdims=True)
        idx = jnp.min(jnp.where(cur == m, iota, float(n_exp)), axis=0, keepdims=True)
        pick = iota == idx
        chosen = jnp.where(pick, 1.0, chosen)
        cur = jnp.where(pick, neg_inf, cur)
        picks.append(idx)
        weights.append(jnp.sum(jnp.where(pick, scores, 0.0), axis=0, keepdims=True))
    idx_k = jnp.concatenate(picks, axis=0)
    w_k = jnp.concatenate(weights, axis=0)
    gate_k = w_k / jnp.sum(w_k, axis=0, keepdims=True) * ROUTED_SCALE
    return idx_k, gate_k, chosen


def _pack_bf16_pairs(x):
    n = x.shape[1] // 2
    hi = lax.bitcast_convert_type(x[:, :n].astype(BF16).astype(F32), jnp.int32)
    lo = lax.bitcast_convert_type(x[:, n:].astype(BF16).astype(F32), jnp.int32)
    return hi | lax.shift_right_logical(lo, 16)


def _unpack_bf16_pairs(p):
    hi = lax.bitcast_convert_type(p & jnp.int32(-65536), F32).astype(BF16)
    lo = lax.bitcast_convert_type(lax.shift_left(p, 16), F32).astype(BF16)
    return hi, lo


def _mix_kernel(att_ref, ssm_ref, xn_ref, ag_ref, sg_ref, woa_ref, wos_ref,
                g_ref, b_ref, wrh_ref, wrl_ref, rb_ref,
                x1_ref, x1p_ref, idx_ref, gate_ref, rank_ref, cnt_ref, carry_ref, before_ref):
    @pl.when(pl.program_id(0) == 0)
    def _():
        carry_ref[...] = jnp.zeros_like(carry_ref)
        tm = before_ref.shape[0]
        before = (lax.broadcasted_iota(jnp.int32, (tm, tm), 0)
                  < lax.broadcasted_iota(jnp.int32, (tm, tm), 1))
        before_ref[...] = jnp.where(before, 1.0, 0.0).astype(BF16)

    an = _rms_norm(att_ref[...].astype(F32), ag_ref[...]).astype(BF16)
    sn = _rms_norm(ssm_ref[...].astype(F32), sg_ref[...]).astype(BF16)
    mix = _dot(an, woa_ref[...]) + _dot(sn, wos_ref[...])
    x1 = _layer_norm(ALPHA * xn_ref[...] + mix, g_ref[...], b_ref[...])
    x1_ref[...] = x1
    x1p_ref[...] = _pack_bf16_pairs(x1)
    x_hi = x1.astype(BF16)
    x_lo = (x1 - x_hi.astype(F32)).astype(BF16)
    dg = functools.partial(lax.dot_general, dimension_numbers=NT_DIMS, preferred_element_type=F32)
    logits_t = dg(wrh_ref[...], x_hi) + dg(wrl_ref[...], x_hi) + dg(wrh_ref[...], x_lo)
    idx_k, gate_k, chosen = _router_gates(logits_t, rb_ref[...])
    idx_ref[...] = idx_k.astype(jnp.int32)
    gate_ref[...] = gate_k.T

    n_exp, tm = chosen.shape
    excl = _dot(chosen.astype(BF16), before_ref[...])
    rank_full = carry_ref[...] + excl
    iota = lax.broadcasted_iota(jnp.int32, (n_exp, tm), 0).astype(F32)
    ranks = [jnp.sum(jnp.where(iota == idx_k[k:k + 1, :], rank_full, 0.0), axis=0, keepdims=True)
             for k in range(TOP_K)]
    rank_ref[...] = jnp.concatenate(ranks, axis=0).astype(jnp.int32)
    total = carry_ref[...] + jnp.sum(chosen, axis=1, keepdims=True)
    carry_ref[...] = total
    cnt_ref[...] = jnp.broadcast_to(total, cnt_ref.shape)


def _positions_kernel(off_ref, idx_ref, rank_ref, pos_ref):
    idx = idx_ref[...]
    base = jnp.zeros(idx.shape, jnp.int32)
    for e in range(N_EXPERTS):
        base = jnp.where(idx == e, off_ref[e], base)
    pos_ref[...] = rank_ref[...] + base


def _swiglu(x_hi, x_lo, wg_ref, wu_ref, wd_ref):
    half = x_hi.shape[1]
    hg = _dot(x_hi, wg_ref[:half, :]) + _dot(x_lo, wg_ref[half:, :])
    hu = _dot(x_hi, wu_ref[:half, :]) + _dot(x_lo, wu_ref[half:, :])
    h = jax.nn.silu(hg) * hu
    return _dot(h.astype(BF16), wd_ref[...])


def _experts_kernel(te_ref, valid_ref, xs_ref, wg_ref, wu_ref, wd_ref, ys_ref,
                    wgb_ref, wub_ref, wdb_ref):
    i = pl.program_id(0)
    valid = valid_ref[i]

    @pl.when((i == 0) | (te_ref[i] != te_ref[jnp.maximum(i - 1, 0)]))
    def _():
        wgb_ref[...] = wg_ref[0].astype(BF16)
        wub_ref[...] = wu_ref[0].astype(BF16)
        wdb_ref[...] = wd_ref[0].astype(BF16)

    @pl.when(valid > 0)
    def _():
        rows = lax.broadcasted_iota(jnp.int32, xs_ref.shape, 0)
        x_hi, x_lo = _unpack_bf16_pairs(jnp.where(rows < valid, xs_ref[...], 0))
        ys_ref[...] = _pack_bf16_pairs(_swiglu(x_hi, x_lo, wgb_ref, wub_ref, wdb_ref))

    @pl.when(valid == 0)
    def _():
        ys_ref[...] = jnp.zeros_like(ys_ref)


def _combine_kernel(yg_ref, gate_ref, x1_ref, wsg_ref, wsu_ref, wsd_ref, g_ref, b_ref, o_ref):
    x1 = x1_ref[...]
    half = x1.shape[1] // 2
    acc = _swiglu(x1[:, :half].astype(BF16), x1[:, half:].astype(BF16), wsg_ref, wsu_ref, wsd_ref)
    gates = gate_ref[...]
    acc_hi, acc_lo = acc[:, :half], acc[:, half:]
    for k in range(TOP_K):
        y_hi, y_lo = _unpack_bf16_pairs(yg_ref[k])
        gk = gates[:, k:k + 1]
        acc_hi = acc_hi + gk * y_hi.astype(F32)
        acc_lo = acc_lo + gk * y_lo.astype(F32)
    ffn = jnp.concatenate([acc_hi, acc_lo], axis=1)
    o_ref[...] = _layer_norm(ALPHA * x1 + ffn, g_ref[...], b_ref[...])


def _combine_into_kernel(prev_ref, *refs):
    del prev_ref
    _combine_kernel(*refs)


SC_CORES = 2
SC_SUBCORES = 16
SC_WORKERS = SC_CORES * SC_SUBCORES
SC_CHUNK = 64


def _sc_mesh():
    return plsc.VectorSubcoreMesh(core_axis_name="c", subcore_axis_name="s")


def _sc_gather_rows(table, idx):
    n = idx.shape[0]
    d = table.shape[1]
    assert n % (SC_WORKERS * 2 * SC_CHUNK) == 0, n
    per_w = n // SC_WORKERS
    n_ch = per_w // SC_CHUNK

    @functools.partial(
        pl.kernel, mesh=_sc_mesh(),
        out_type=jax.ShapeDtypeStruct((n, d), table.dtype),
        scratch_types=[pltpu.VMEM((n_ch, SC_CHUNK), jnp.int32),
                       pltpu.VMEM((2, SC_CHUNK, d), table.dtype),
                       pltpu.SemaphoreType.DMA((2,)),
                       pltpu.SemaphoreType.DMA((2,))],
    )
    def k(table_hbm, idx_hbm, out_hbm, idx_v, buf, gsem, osem):
        wid = lax.axis_index("s") * SC_CORES + lax.axis_index("c")
        base = wid * per_w
        pltpu.sync_copy(idx_hbm.at[wid], idx_v)

        def gather(c, b):
            return pltpu.make_async_copy(table_hbm.at[idx_v.at[c]], buf.at[b], gsem.at[b])

        def put(c, b):
            return pltpu.make_async_copy(buf.at[b], out_hbm.at[pl.ds(base + c * SC_CHUNK, SC_CHUNK)],
                                         osem.at[b])

        gather(0, 0).start()

        @pl.loop(0, n_ch, step=2)
        def _(c):
            for b in range(2):
                cc = c + b
                gather(cc, b).wait()

                @pl.when(cc + 1 < n_ch)
                def _():
                    @pl.when(cc >= 1)
                    def _():
                        put(cc - 1, 1 - b).wait()
                    gather(cc + 1, 1 - b).start()

                put(cc, b).start()

        put(n_ch - 2, 0).wait()
        put(n_ch - 1, 1).wait()

    return k(table, idx.reshape(SC_WORKERS, n_ch, SC_CHUNK))


def _sc_scatter_rows(x, pos, n_out):
    t, d = x.shape
    kk = pos.shape[0]
    assert t % (SC_WORKERS * 2 * SC_CHUNK) == 0, t
    per_w = t // SC_WORKERS
    n_ch = per_w // SC_CHUNK
    pos_w = pos.reshape(kk, SC_WORKERS, n_ch, SC_CHUNK).transpose(1, 2, 0, 3)
    pos_w = pos_w.reshape(SC_WORKERS, n_ch * kk, SC_CHUNK)

    @functools.partial(
        pl.kernel, mesh=_sc_mesh(),
        out_type=jax.ShapeDtypeStruct((n_out, d), x.dtype),
        scratch_types=[pltpu.VMEM((n_ch * kk, SC_CHUNK), jnp.int32),
                       pltpu.VMEM((2, SC_CHUNK, d), x.dtype),
                       pltpu.SemaphoreType.DMA((2,)),
                       pltpu.SemaphoreType.DMA((2,))],
    )
    def k(x_hbm, pos_hbm, out_hbm, idx_v, buf, isem, osem):
        wid = lax.axis_index("s") * SC_CORES + lax.axis_index("c")
        base = wid * per_w
        pltpu.sync_copy(pos_hbm.at[wid], idx_v)

        def get(c, b):
            return pltpu.make_async_copy(x_hbm.at[pl.ds(base + c * SC_CHUNK, SC_CHUNK)], buf.at[b],
                                         isem.at[b])

        def put(c, j, b):
            return pltpu.make_async_copy(buf.at[b], out_hbm.at[idx_v.at[c * kk + j]], osem.at[b])

        get(0, 0).start()

        @pl.loop(0, n_ch, step=2)
        def _(c):
            for b in range(2):
                cc = c + b
                get(cc, b).wait()

                @pl.when(cc + 1 < n_ch)
                def _():
                    @pl.when(cc >= 1)
                    def _():
                        for j in range(kk):
                            put(cc - 1, j, 1 - b).wait()
                    get(cc + 1, 1 - b).start()

                for j in range(kk):
                    put(cc, j, b).start()

        for j in range(kk):
            put(n_ch - 2, j, 0).wait()
        for j in range(kk):
            put(n_ch - 1, j, 1).wait()

    return k(x, pos_w)


def _row(v):
    return v.reshape(1, -1).astype(F32)


def _const_spec(shape):
    nd = len(shape)
    return pl.BlockSpec(shape, lambda *_: (0,) * nd)


def _pad_heads(w, width):
    r, h, _ = w.shape
    return jnp.pad(w, ((0, 0), (0, 0), (0, HEAD_PAD - width))).reshape(r, h * HEAD_PAD)


def _half_rotate(w):
    half = QK_ROPE // 2
    return jnp.concatenate([-w[..., half:], w[..., :half]], axis=-1)


def kernel(x, positions, ln_in_g, ln_in_b, w_in, q_norm_g, kv_norm_g, w_uq, w_ukv, lambda_re, lambda_im, log_step, b_re, b_im, c_re, c_im, d_skip, w_glu, b_glu, attn_out_g, ssm_out_g, w_o, ln1_g, ln1_b, w_router, router_bias, w_gate, w_up, w_down, ws_gate, ws_up, ws_down, ln2_g, ln2_b):
    B, S, D = x.shape
    T = B * S
    assert DEPTH == 1 and w_in.shape[0] == DEPTH
    assert S % QUERY_TILE == 0 and S % TOKEN_TILE == 0 and S % S5_TIME_TILE == 0, S
    assert T % (MOE_SLABS * TOKEN_TILE) == 0 and (T // MOE_SLABS * TOP_K) % EXPERT_ROW_TILE == 0, T
    l = 0
    ssm_width = w_glu.shape[-1]
    n_groups = ssm_width // SSM_GROUP
    n_state = n_groups * SSM_STATE
    mla_width = MLA_HEADS * V_DIM
    qk_pad = MLA_HEADS * HEAD_PAD
    cparams = functools.partial(pltpu.CompilerParams, vmem_limit_bytes=VMEM_LIMIT)

    s1, s2, s3 = Q_RANK, Q_RANK + KV_RANK, Q_RANK + KV_RANK + QK_ROPE
    wi = w_in[l]
    w_kr = wi[:, s2:s3]
    pad_rope = lambda w: jnp.pad(w, ((0, 0), (QK_NOPE, HEAD_PAD - QK_NOPE - QK_ROPE)))
    w1 = jnp.concatenate([wi[:, :s2], wi[:, s3:], pad_rope(w_kr), pad_rope(_half_rotate(w_kr))],
                         axis=1).astype(BF16)
    wq = w_uq[l]
    zeros_nope = jnp.zeros(wq.shape[:2] + (QK_NOPE,), wq.dtype)
    wq_main = _pad_heads(wq, QK_NOPE + QK_ROPE).astype(BF16)
    wq_rot = _pad_heads(jnp.concatenate([zeros_nope, _half_rotate(wq[..., QK_NOPE:])], axis=-1),
                        QK_NOPE + QK_ROPE).astype(BF16)
    wkv = w_ukv[l]
    wk = _pad_heads(wkv[..., :QK_NOPE], QK_NOPE).astype(BF16)
    wv = _pad_heads(wkv[..., QK_NOPE:], V_DIM).astype(BF16)
    half = QK_ROPE // 2
    inv_freq = ROPE_THETA ** (-jnp.arange(half, dtype=F32) / half)
    freq = jnp.concatenate([inv_freq, inv_freq]).reshape(QK_ROPE, 1)
    pos_f = positions.astype(F32).reshape(1, T)

    tm = min(TOKEN_TILE, T)
    w1_cols = w1.shape[1]
    tok = lambda width: pl.BlockSpec((tm, width), lambda i: (i, 0))
    xn, q, k, v, u = pl.pallas_call(
        functools.partial(_inproj_kernel, ssm_width=ssm_width),
        grid=(T // tm,),
        in_specs=[tok(D), pl.BlockSpec((1, tm), lambda i: (0, i)), _const_spec((1, D)), _const_spec((1, D)),
                  _const_spec((D, w1_cols)), _const_spec((1, Q_RANK)), _const_spec((1, KV_RANK)),
                  _const_spec((Q_RANK, qk_pad)), _const_spec((Q_RANK, qk_pad)),
                  _const_spec((KV_RANK, qk_pad)), _const_spec((KV_RANK, qk_pad)),
                  _const_spec((QK_ROPE, 1))],
        out_specs=[tok(D), tok(qk_pad), tok(qk_pad), tok(qk_pad), tok(ssm_width)],
        out_shape=[jax.ShapeDtypeStruct((T, D), F32), jax.ShapeDtypeStruct((T, qk_pad), BF16),
                   jax.ShapeDtypeStruct((T, qk_pad), BF16), jax.ShapeDtypeStruct((T, qk_pad), BF16),
                   jax.ShapeDtypeStruct((T, ssm_width), BF16)],
        compiler_params=cparams(dimension_semantics=("parallel",)),
        name="inproj",
    )(x.reshape(T, D), pos_f, _row(ln_in_g), _row(ln_in_b), w1, _row(q_norm_g[l]), _row(kv_norm_g[l]),
      wq_main, wq_rot, wk, wv, freq)

    tq = min(QUERY_TILE, S)
    nq = S // tq
    att = pl.pallas_call(
        functools.partial(_attn_kernel, tq=tq, heads=MLA_HEADS, n_qtiles=nq),
        grid=(B, nq),
        in_specs=[pl.BlockSpec((tq, qk_pad), lambda b, i: (b * nq + i, 0)),
                  pl.BlockSpec((S, qk_pad), lambda b, i: (b, 0)),
                  pl.BlockSpec((S, qk_pad), lambda b, i: (b, 0))],
        out_specs=pl.BlockSpec((tq, mla_width), lambda b, i: (b * nq + i, 0)),
        out_shape=jax.ShapeDtypeStruct((T, mla_width), BF16),
        compiler_params=cparams(dimension_semantics=("parallel", "arbitrary")),
        name="attention",
    )(q, k, v)

    lam = lax.complex(jnp.minimum(lambda_re[l].astype(F32), -1e-4), lambda_im[l].astype(F32))
    step = jnp.exp(log_step[l].astype(F32))[:, None]
    lam_bar = jnp.exp(lam * step)
    b_bar = ((lam_bar - 1.0) / lam)[..., None] * lax.complex(b_re[l].astype(F32), b_im[l].astype(F32))
    n_slab = ssm_width // LANES
    g_per_slab = n_groups // n_slab
    slab = n_state // n_slab
    eye = jnp.eye(g_per_slab, dtype=F32)

    def expand_in(bpart):
        bt = bpart.transpose(0, 2, 1).reshape(n_slab, g_per_slab, SSM_GROUP, 1, SSM_STATE)
        return (bt * eye[None, :, None, :, None]).reshape(n_slab, LANES, slab)

    def expand_out(cpart):
        ct = cpart.transpose(0, 2, 1).reshape(n_slab, g_per_slab, SSM_STATE, 1, SSM_GROUP)
        return (ct * eye[None, :, None, :, None]).reshape(n_slab, slab, LANES)

    win = jnp.concatenate([expand_in(jnp.real(b_bar)), expand_in(jnp.imag(b_bar))],
                          axis=2).astype(BF16)
    cre = expand_out(c_re[l].astype(F32)).astype(BF16)
    cim = expand_out(-c_im[l].astype(F32)).astype(BF16)
    a_re = jnp.real(lam_bar).reshape(1, n_state)
    a_im = jnp.imag(lam_bar).reshape(1, n_state)

    lt = min(S5_TIME_TILE, S)
    ssm = pl.pallas_call(
        functools.partial(_s5_kernel, batch=B, lt=lt, n_state=n_state),
        grid=(S // lt,),
        in_specs=[pl.BlockSpec((B, lt, ssm_width), lambda t: (0, t, 0)),
                  _const_spec(win.shape), _const_spec((1, n_state)), _const_spec((1, n_state)),
                  _const_spec(cre.shape), _const_spec(cim.shape), _const_spec((1, ssm_width)),
                  _const_spec((ssm_width, ssm_width)), _const_spec((1, ssm_width))],
        out_specs=pl.BlockSpec((B, lt, ssm_width), lambda t: (0, t, 0)),
        out_shape=jax.ShapeDtypeStruct((B, S, ssm_width), BF16),
        scratch_shapes=[pltpu.VMEM((2 * n_state // LANES, B * lt, LANES), F32),
                        pltpu.VMEM((n_state // LANES, B, LANES), F32),
                        pltpu.VMEM((n_state // LANES, B, LANES), F32),
                        pltpu.VMEM((ssm_width // LANES, B * lt, LANES), F32),
                        pltpu.VMEM((B * lt, ssm_width), F32)],
        compiler_params=cparams(dimension_semantics=("arbitrary",)),
        name="s5",
    )(u.reshape(B, S, ssm_width), win, a_re, a_im, cre, cim, _row(d_skip[l]),
      w_glu[l].astype(BF16), _row(b_glu[l]))

    wo = w_o[l].astype(BF16)
    assert mla_width == ssm_width
    wr_t = w_router[l].T.astype(F32)
    wr_hi = wr_t.astype(BF16)
    wr_lo = (wr_t - wr_hi.astype(F32)).astype(BF16)
    half = D // 2
    n_slabs = MOE_SLABS
    ts = T // n_slabs
    nt = ts // tm
    ssm2 = ssm.reshape(T, ssm_width)
    rbias = router_bias[l].astype(F32).reshape(N_EXPERTS, 1)
    kt = lambda dt: jax.ShapeDtypeStruct((TOP_K, ts), dt)
    k_spec = pl.BlockSpec((TOP_K, tm), lambda i: (0, i))

    def route(s):
        tok_s = lambda width: pl.BlockSpec((tm, width), lambda i: (i + s * nt, 0))
        return pl.pallas_call(
            _mix_kernel,
            grid=(nt,),
            in_specs=[tok_s(mla_width), tok_s(ssm_width), tok_s(D), _const_spec((1, mla_width)),
                      _const_spec((1, ssm_width)), pl.BlockSpec((mla_width, D), lambda i: (0, 0)),
                      pl.BlockSpec((ssm_width, D), lambda i: (1, 0)), _const_spec((1, D)), _const_spec((1, D)),
                      _const_spec((N_EXPERTS, D)), _const_spec((N_EXPERTS, D)),
                      _const_spec((N_EXPERTS, 1))],
            out_specs=[tok(D), tok(half), k_spec, tok(TOP_K), k_spec, _const_spec((N_EXPERTS, LANES))],
            out_shape=[jax.ShapeDtypeStruct((ts, D), F32), jax.ShapeDtypeStruct((ts, half), jnp.int32),
                       kt(jnp.int32), jax.ShapeDtypeStruct((ts, TOP_K), F32), kt(jnp.int32),
                       jax.ShapeDtypeStruct((N_EXPERTS, LANES), F32)],
            scratch_shapes=[pltpu.VMEM((N_EXPERTS, 1), F32), pltpu.VMEM((tm, tm), BF16)],
            compiler_params=cparams(dimension_semantics=("arbitrary",)),
            name="mix_router",
        )(att, ssm2, xn, _row(attn_out_g[l]), _row(ssm_out_g[l]), wo, wo,
          _row(ln1_g[l]), _row(ln1_b[l]), wr_hi, wr_lo, rbias)

    tr = EXPERT_ROW_TILE
    n_tiles = (ts * TOP_K) // tr + N_EXPERTS
    n_rows = n_tiles * tr
    tp = min(POSITIONS_TILE, ts)

    def dispatch(x1p, idx_k, rank_k, counts):
        cnt = counts[:, 0].astype(jnp.int32)
        tiles_e = (cnt + tr - 1) // tr
        tile_end = jnp.cumsum(tiles_e)
        tile_start = tile_end - tiles_e
        tile_ids = jnp.arange(n_tiles, dtype=jnp.int32)
        tile_expert = jnp.sum((tile_end[None, :] <= tile_ids[:, None]).astype(jnp.int32), axis=1)
        tile_expert = jnp.minimum(tile_expert, N_EXPERTS - 1)
        owner = (tile_start[None, :] <= tile_ids[:, None]) & (tile_ids[:, None] < tile_end[None, :])
        left = jnp.sum(jnp.where(owner, cnt[None, :] - (tile_ids[:, None] - tile_start[None, :]) * tr, 0),
                       axis=1)
        tile_valid = jnp.clip(left, 0, tr).astype(jnp.int32)
        pos = pl.pallas_call(
            _positions_kernel,
            grid_spec=pltpu.PrefetchScalarGridSpec(
                num_scalar_prefetch=1, grid=(ts // tp,),
                in_specs=[pl.BlockSpec((TOP_K, tp), lambda i, off: (0, i)),
                          pl.BlockSpec((TOP_K, tp), lambda i, off: (0, i))],
                out_specs=pl.BlockSpec((TOP_K, tp), lambda i, off: (0, i))),
            out_shape=kt(jnp.int32),
            name="positions",
        )((tile_start * tr).astype(jnp.int32), idx_k, rank_k)
        return _sc_scatter_rows(x1p, pos, n_rows), pos, tile_expert, tile_valid

    wg, wu, wd = w_gate[l], w_up[l], w_down[l]
    ff = wg.shape[-1]

    def experts(xs, tile_expert, tile_valid):
        return pl.pallas_call(
            _experts_kernel,
            grid_spec=pltpu.PrefetchScalarGridSpec(
                num_scalar_prefetch=2, grid=(n_tiles,),
                in_specs=[pl.BlockSpec((tr, half), lambda i, te, tv: (i, 0)),
                          pl.BlockSpec((1, D, ff), lambda i, te, tv: (te[i], 0, 0)),
                          pl.BlockSpec((1, D, ff), lambda i, te, tv: (te[i], 0, 0)),
                          pl.BlockSpec((1, ff, D), lambda i, te, tv: (te[i], 0, 0))],
                out_specs=pl.BlockSpec((tr, half), lambda i, te, tv: (i, 0)),
                scratch_shapes=[pltpu.VMEM((D, ff), BF16), pltpu.VMEM((D, ff), BF16),
                                pltpu.VMEM((ff, D), BF16)]),
            out_shape=jax.ShapeDtypeStruct((n_rows, half), jnp.int32),
            compiler_params=cparams(dimension_semantics=("arbitrary",)),
            name="experts",
        )(tile_expert, tile_valid, xs, wg, wu, wd)

    shared = (ws_gate[l].astype(BF16), ws_up[l].astype(BF16), ws_down[l].astype(BF16))

    def combine(s, out_so_far, yg, gate_k, x1):
        specs = [pl.BlockSpec((TOP_K, tm, half), lambda i: (0, i, 0)),
                 pl.BlockSpec((tm, TOP_K), lambda i: (i, 0)), tok(D),
                 _const_spec((D, ff)), _const_spec((D, ff)), _const_spec((ff, D)),
                 _const_spec((1, D)), _const_spec((1, D))]
        args = (yg.reshape(TOP_K, ts, half), gate_k, x1, *shared, _row(ln2_g[l]), _row(ln2_b[l]))
        body, aliases = _combine_kernel, {}
        if out_so_far is not None:
            specs = [pl.BlockSpec(memory_space=pl.ANY)] + specs
            args = (out_so_far,) + args
            body, aliases = _combine_into_kernel, {0: 0}
        return pl.pallas_call(
            body,
            grid=(nt,),
            in_specs=specs,
            out_specs=pl.BlockSpec((tm, D), lambda i: (i + s * nt, 0)),
            out_shape=jax.ShapeDtypeStruct((T, D), F32),
            input_output_aliases=aliases,
            compiler_params=cparams(dimension_semantics=("parallel",)),
            name="combine",
        )(*args)

    routed, moved = [], []
    for s in range(n_slabs):
        x1, x1p, idx_k, gate_k, rank_k, counts = route(s)
        routed.append((x1, gate_k))
        moved.append(dispatch(x1p, idx_k, rank_k, counts))
    gathered = []
    for xs, pos, tile_expert, tile_valid in moved:
        gathered.append(_sc_gather_rows(experts(xs, tile_expert, tile_valid), pos.reshape(TOP_K * ts)))
    out = None
    for s in range(n_slabs):
        out = combine(s, out, gathered[s], routed[s][1], routed[s][0])
    return out.reshape(B, S, D)
```

```python
import functools

import jax
import jax.numpy as jnp
from jax import lax
from jax.experimental import pallas as pl
from jax.experimental.pallas import tpu as pltpu
from jax.experimental.pallas import tpu_sc as plsc

CHUNK = 64
MLA_HEADS = 8
QK_NOPE = 64
QK_ROPE = 32
V_DIM = 64
Q_RANK = 256
KV_RANK = 128
ROPE_THETA = 10000.0
SSM_GROUP = 16
SSM_STATE = 64
N_EXPERTS = 64
TOP_K = 8
N_GROUPS = 8
TOP_GROUPS = 4
ROUTED_SCALE = 2.5
DEPTH = 1
ALPHA = (2.0 * DEPTH) ** 0.25
EPS = 1e-5
LOG2_E = 1.4426950408889634

LANES = 128
HEAD_PAD = LANES
VMEM_LIMIT = 56 * 1024 * 1024
TOKEN_TILE = 512
MIX_TILE = 1024
QUERY_TILE = 256
S5_TIME_TILE = 64
POSITIONS_TILE = 2048
EXPERT_ROW_TILE = 1024
MOE_SLABS = 2

BF16 = jnp.bfloat16
F32 = jnp.float32
NT_DIMS = (((1,), (1,)), ((), ()))


def _dot(a, b):
    return jnp.dot(a, b, preferred_element_type=F32)


def _layer_norm(x, g, b):
    mu = jnp.mean(x, axis=-1, keepdims=True)
    xc = x - mu
    var = jnp.mean(xc * xc, axis=-1, keepdims=True)
    return xc * lax.rsqrt(var + EPS) * g + b


def _rms_norm(x, g):
    return x * lax.rsqrt(jnp.mean(x * x, axis=-1, keepdims=True) + EPS) * g


def _inproj_kernel(x_ref, pos_ref, lng_ref, lnb_ref, w1_ref, qg_ref, kvg_ref,
                   wq_ref, wqr_ref, wk_ref, wv_ref, freq_ref,
                   xn_ref, q_ref, k_ref, v_ref, u_ref, *, ssm_width):
    xn = _layer_norm(x_ref[...], lng_ref[...], lnb_ref[...])
    xn_ref[...] = xn
    h = _dot(xn.astype(BF16), w1_ref[...])
    o1 = Q_RANK
    o2 = o1 + KV_RANK
    o3 = o2 + ssm_width
    o4 = o3 + HEAD_PAD
    cq = h[:, :o1]
    ckv = h[:, o1:o2]
    u_ref[...] = h[:, o2:o3]
    kr_raw = h[:, o3:o4]
    kr_rot = h[:, o4:o4 + HEAD_PAD]
    cqn = _rms_norm(cq, qg_ref[...]).astype(BF16)
    ckvn = _rms_norm(ckv, kvg_ref[...]).astype(BF16)

    tm = x_ref.shape[0]
    ang_t = freq_ref[...] * pos_ref[...]

    def to_token_rows(t):
        padded = jnp.concatenate([jnp.zeros((QK_NOPE, tm), F32), t,
                                  jnp.zeros((HEAD_PAD - QK_NOPE - QK_ROPE, tm), F32)], axis=0)
        return padded.T

    c = to_token_rows(jnp.cos(ang_t))
    s = to_token_rows(jnp.sin(ang_t))
    lane = lax.broadcasted_iota(jnp.int32, (1, HEAD_PAD), 1)
    scale = (QK_NOPE + QK_ROPE) ** -0.5 * LOG2_E
    cos1 = (c + jnp.where(lane < QK_NOPE, 1.0, 0.0)) * scale
    sin1 = s * scale
    cos_t = jnp.concatenate([cos1] * MLA_HEADS, axis=1)
    sin_t = jnp.concatenate([sin1] * MLA_HEADS, axis=1)
    q = _dot(cqn, wq_ref[...]) * cos_t + _dot(cqn, wqr_ref[...]) * sin_t
    q_ref[...] = q.astype(BF16)

    kr = kr_raw * c + kr_rot * s
    k = _dot(ckvn, wk_ref[...]) + jnp.concatenate([kr] * MLA_HEADS, axis=1)
    k_ref[...] = k.astype(BF16)
    ones_col = jnp.concatenate([jnp.where(lane == V_DIM, 1.0, 0.0)] * MLA_HEADS, axis=1)
    v_ref[...] = (_dot(ckvn, wv_ref[...]) + ones_col).astype(BF16)


def _attn_kernel(q_ref, k_ref, v_ref, o_ref, *, tq, heads, n_qtiles):
    qi = pl.program_id(1)
    row_chunk = lax.broadcasted_iota(jnp.int32, (tq, tq), 0) // CHUNK
    col_chunk = lax.broadcasted_iota(jnp.int32, (tq, tq), 1) // CHUNK
    diag_mask = row_chunk >= col_chunk

    def tile(n_blocks):
        keys = n_blocks * tq
        cols = [slice(h * HEAD_PAD, (h + 1) * HEAD_PAD) for h in range(heads)]

        def mask_diag(s):
            s_diag = jnp.where(diag_mask, s[:, keys - tq:], -jnp.inf)
            return s_diag if n_blocks == 1 else jnp.concatenate([s[:, :keys - tq], s_diag], axis=1)

        ss = [lax.dot_general(q_ref[:, c], k_ref[:keys, c], NT_DIMS, preferred_element_type=F32)
              for c in cols]
        ss = [mask_diag(s) for s in ss]
        ms = [jnp.max(s, axis=-1, keepdims=True) for s in ss]
        ps = [jnp.exp2(s - m).astype(BF16) for s, m in zip(ss, ms)]
        accs = [_dot(p, v_ref[:keys, c]) for p, c in zip(ps, cols)]
        outs = [acc[:, :V_DIM] / acc[:, V_DIM:V_DIM + 1] for acc in accs]
        o_ref[...] = jnp.concatenate(outs, axis=1).astype(o_ref.dtype)

    for c in range(n_qtiles):
        pl.when(qi == c)(functools.partial(tile, c + 1))


def _s5_kernel(u_ref, win_ref, are_ref, aim_ref, cre_ref, cim_ref, dskip_ref,
               wglu_ref, bglu_ref, o_ref, vx_ref, hre_ref, him_ref, io_ref, utm_ref, *,
               batch, lt, n_state):
    ti = pl.program_id(0)

    @pl.when(ti == 0)
    def _():
        hre_ref[...] = jnp.zeros_like(hre_ref)
        him_ref[...] = jnp.zeros_like(him_ref)

    width = u_ref.shape[-1]
    n_slab = width // LANES
    n_tiles = n_state // LANES
    slab_tiles = n_tiles // n_slab
    for b in range(batch):
        for c in range(n_slab):
            io_ref[c, b * lt:(b + 1) * lt, :] = u_ref[b, :, c * LANES:(c + 1) * LANES]

    def slab_tiles_of(j):
        return range(j * slab_tiles, (j + 1) * slab_tiles)

    def expand(j):
        for t in range(lt):
            utm_ref[t * batch:(t + 1) * batch, j * LANES:(j + 1) * LANES] = (
                io_ref[j, pl.ds(t, batch, stride=lt), :])
        ub = utm_ref[:, j * LANES:(j + 1) * LANES].astype(BF16)
        vj = _dot(ub, win_ref[j])
        for i, c in enumerate(slab_tiles_of(j)):
            vx_ref[c] = vj[:, i * LANES:(i + 1) * LANES]
            vx_ref[n_tiles + c] = vj[:, (slab_tiles + i) * LANES:(slab_tiles + i + 1) * LANES]

    def scan(j):
        tiles = slab_tiles_of(j)
        ar = [jnp.broadcast_to(are_ref[:, c * LANES:(c + 1) * LANES], (batch, LANES)) for c in tiles]
        ai = [jnp.broadcast_to(aim_ref[:, c * LANES:(c + 1) * LANES], (batch, LANES)) for c in tiles]
        hr = [hre_ref[c] for c in tiles]
        hi = [him_ref[c] for c in tiles]
        for t in range(lt):
            rows = slice(t * batch, (t + 1) * batch)
            for n, c in enumerate(tiles):
                nr = ar[n] * hr[n] - ai[n] * hi[n] + vx_ref[c, rows, :]
                ni = ar[n] * hi[n] + ai[n] * hr[n] + vx_ref[n_tiles + c, rows, :]
                vx_ref[c, rows, :] = nr
                vx_ref[n_tiles + c, rows, :] = ni
                hr[n], hi[n] = nr, ni
        for n, c in enumerate(tiles):
            hre_ref[c] = hr[n]
            him_ref[c] = hi[n]

    def project(j):
        tiles = slab_tiles_of(j)
        xr = jnp.concatenate([vx_ref[c].astype(BF16) for c in tiles], axis=1)
        xi = jnp.concatenate([vx_ref[n_tiles + c].astype(BF16) for c in tiles], axis=1)
        return _dot(xr, cre_ref[j]) + _dot(xi, cim_ref[j])

    ys = [None] * n_slab
    for j in range(n_slab + 2):
        if j < n_slab:
            expand(j)
        if 1 <= j <= n_slab:
            scan(j - 1)
        if j >= 2:
            ys[j - 2] = project(j - 2)
    y = jnp.concatenate(ys, axis=1) + dskip_ref[...] * utm_ref[...]
    y = jax.nn.gelu(y)
    z = _dot(y.astype(BF16), wglu_ref[...]) + bglu_ref[...]
    out = y * jax.nn.sigmoid(z)
    for c in range(n_slab):
        io_ref[c] = out[:, c * LANES:(c + 1) * LANES]
    for b in range(batch):
        for c in range(n_slab):
            o_ref[b, :, c * LANES:(c + 1) * LANES] = io_ref[c, pl.ds(b, lt, stride=batch), :].astype(o_ref.dtype)


def _router_gates(logits_t, rbias):
    n_exp, tm = logits_t.shape
    per_group = n_exp // N_GROUPS
    scores = jax.nn.sigmoid(logits_t)
    sel = scores + rbias
    neg_inf = -jnp.inf
    sub_iota = lax.broadcasted_iota(jnp.int32, (per_group, tm), 0).astype(F32)
    group_score = []
    for g in range(N_GROUPS):
        sg = sel[g * per_group:(g + 1) * per_group, :]
        m1 = jnp.max(sg, axis=0, keepdims=True)
        first = jnp.min(jnp.where(sg == m1, sub_iota, float(per_group)), axis=0, keepdims=True)
        m2 = jnp.max(jnp.where(sub_iota == first, neg_inf, sg), axis=0, keepdims=True)
        group_score.append(m1 + m2)
    masked = []
    for g in range(N_GROUPS):
        rank = jnp.zeros((1, tm), F32)
        for g2 in range(N_GROUPS):
            if g2 == g:
                continue
            ahead = (group_score[g2] >= group_score[g]) if g2 < g else (group_score[g2] > group_score[g])
            rank = rank + jnp.where(ahead, 1.0, 0.0)
        keep = rank < float(TOP_GROUPS)
        masked.append(jnp.where(keep, sel[g * per_group:(g + 1) * per_group, :], neg_inf))
    cur = jnp.concatenate(masked, axis=0)
    iota = lax.broadcasted_iota(jnp.int32, (n_exp, tm), 0).astype(F32)
    chosen = jnp.zeros((n_exp, tm), F32)
    picks, weights = [], []
    for _ in range(TOP_K):
        m = jnp.max(cur, axis=0, keepdims=True)
        idx = jnp.min(jnp.where(cur == m, iota, float(n_exp)), axis=0, keepdims=True)
        pick = iota == idx
        chosen = jnp.where(pick, 1.0, chosen)
        cur = jnp.where(pick, neg_inf, cur)
        picks.append(idx)
        weights.append(jnp.sum(jnp.where(pick, scores, 0.0), axis=0, keepdims=True))
    idx_k = jnp.concatenate(picks, axis=0)
    w_k = jnp.concatenate(weights, axis=0)
    gate_k = w_k / jnp.sum(w_k, axis=0, keepdims=True) * ROUTED_SCALE
    return idx_k, gate_k, chosen


def _pack_bf16_pairs(x):
    n = x.shape[1] // 2
    hi = lax.bitcast_convert_type(x[:, :n].astype(BF16).astype(F32), jnp.int32)
    lo = lax.bitcast_convert_type(x[:, n:].astype(BF16).astype(F32), jnp.int32)
    return hi | lax.shift_right_logical(lo, 16)


def _unpack_bf16_pairs(p):
    hi = lax.bitcast_convert_type(p & jnp.int32(-65536), F32).astype(BF16)
    lo = lax.bitcast_convert_type(lax.shift_left(p, 16), F32).astype(BF16)
    return hi, lo


def _mix_kernel(att_ref, ssm_ref, xn_ref, ag_ref, sg_ref, woa_ref, wos_ref,
                g_ref, b_ref, wrh_ref, wrl_ref, rb_ref,
                x1_ref, x1p_ref, idx_ref, gate_ref, rank_ref, cnt_ref, carry_ref, before_ref):
    @pl.when(pl.program_id(0) == 0)
    def _():
        carry_ref[...] = jnp.zeros_like(carry_ref)
        tm = before_ref.shape[0]
        before = (lax.broadcasted_iota(jnp.int32, (tm, tm), 0)
                  < lax.broadcasted_iota(jnp.int32, (tm, tm), 1))
        before_ref[...] = jnp.where(before, 1.0, 0.0).astype(BF16)

    an = _rms_norm(att_ref[...].astype(F32), ag_ref[...]).astype(BF16)
    sn = _rms_norm(ssm_ref[...].astype(F32), sg_ref[...]).astype(BF16)
    mix = _dot(an, woa_ref[...]) + _dot(sn, wos_ref[...])
    x1 = _layer_norm(ALPHA * xn_ref[...] + mix, g_ref[...], b_ref[...])
    x1_ref[...] = x1
    x1p_ref[...] = _pack_bf16_pairs(x1)
    x_hi = x1.astype(BF16)
    x_lo = (x1 - x_hi.astype(F32)).astype(BF16)
    dg = functools.partial(lax.dot_general, dimension_numbers=NT_DIMS, preferred_element_type=F32)
    logits_t = dg(wrh_ref[...], x_hi) + dg(wrl_ref[...], x_hi) + dg(wrh_ref[...], x_lo)
    idx_k, gate_k, chosen = _router_gates(logits_t, rb_ref[...])
    idx_ref[...] = idx_k.astype(jnp.int32)
    gate_ref[...] = gate_k.T

    n_exp, tm = chosen.shape
    excl = _dot(chosen.astype(BF16), before_ref[...])
    rank_full = carry_ref[...] + excl
    iota = lax.broadcasted_iota(jnp.int32, (n_exp, tm), 0).astype(F32)
    ranks = [jnp.sum(jnp.where(iota == idx_k[k:k + 1, :], rank_full, 0.0), axis=0, keepdims=True)
             for k in range(TOP_K)]
    rank_ref[...] = jnp.concatenate(ranks, axis=0).astype(jnp.int32)
    total = carry_ref[...] + jnp.sum(chosen, axis=1, keepdims=True)
    carry_ref[...] = total
    cnt_ref[...] = jnp.broadcast_to(total, cnt_ref.shape)


def _positions_kernel(off_ref, idx_ref, rank_ref, pos_ref):
    idx = idx_ref[...]
    base = jnp.zeros(idx.shape, jnp.int32)
    for e in range(N_EXPERTS):
        base = jnp.where(idx == e, off_ref[e], base)
    pos_ref[...] = rank_ref[...] + base


def _swiglu(x_hi, x_lo, wg_ref, wu_ref, wd_ref):
    half = x_hi.shape[1]
    hg = _dot(x_hi, wg_ref[:half, :]) + _dot(x_lo, wg_ref[half:, :])
    hu = _dot(x_hi, wu_ref[:half, :]) + _dot(x_lo, wu_ref[half:, :])
    h = jax.nn.silu(hg) * hu
    return _dot(h.astype(BF16), wd_ref[...])


def _experts_kernel(te_ref, valid_ref, xs_ref, wg_ref, wu_ref, wd_ref, ys_ref,
                    wgb_ref, wub_ref, wdb_ref):
    i = pl.program_id(0)
    valid = valid_ref[i]

    @pl.when((i == 0) | (te_ref[i] != te_ref[jnp.maximum(i - 1, 0)]))
    def _():
        wgb_ref[...] = wg_ref[0].astype(BF16)
        wub_ref[...] = wu_ref[0].astype(BF16)
        wdb_ref[...] = wd_ref[0].astype(BF16)

    @pl.when(valid > 0)
    def _():
        rows = lax.broadcasted_iota(jnp.int32, xs_ref.shape, 0)
        x_hi, x_lo = _unpack_bf16_pairs(jnp.where(rows < valid, xs_ref[...], 0))
        ys_ref[...] = _pack_bf16_pairs(_swiglu(x_hi, x_lo, wgb_ref, wub_ref, wdb_ref))

    @pl.when(valid == 0)
    def _():
        ys_ref[...] = jnp.zeros_like(ys_ref)


def _combine_kernel(yg_ref, gate_ref, x1_ref, wsg_ref, wsu_ref, wsd_ref, g_ref, b_ref, o_ref):
    x1 = x1_ref[...]
    half = x1.shape[1] // 2
    acc = _swiglu(x1[:, :half].astype(BF16), x1[:, half:].astype(BF16), wsg_ref, wsu_ref, wsd_ref)
    gates = gate_ref[...]
    acc_hi, acc_lo = acc[:, :half], acc[:, half:]
    for k in range(TOP_K):
        y_hi, y_lo = _unpack_bf16_pairs(yg_ref[k])
        gk = gates[:, k:k + 1]
        acc_hi = acc_hi + gk * y_hi.astype(F32)
        acc_lo = acc_lo + gk * y_lo.astype(F32)
    ffn = jnp.concatenate([acc_hi, acc_lo], axis=1)
    o_ref[...] = _layer_norm(ALPHA * x1 + ffn, g_ref[...], b_ref[...])


def _combine_into_kernel(prev_ref, *refs):
    del prev_ref
    _combine_kernel(*refs)


SC_CORES = 2
SC_SUBCORES = 16
SC_WORKERS = SC_CORES * SC_SUBCORES
SC_CHUNK = 64


def _sc_mesh():
    return plsc.VectorSubcoreMesh(core_axis_name="c", subcore_axis_name="s")


def _sc_gather_rows(table, idx):
    n = idx.shape[0]
    d = table.shape[1]
    assert n % (SC_WORKERS * 2 * SC_CHUNK) == 0, n
    per_w = n // SC_WORKERS
    n_ch = per_w // SC_CHUNK

    @functools.partial(
        pl.kernel, mesh=_sc_mesh(),
        out_type=jax.ShapeDtypeStruct((n, d), table.dtype),
        scratch_types=[pltpu.VMEM((n_ch, SC_CHUNK), jnp.int32),
                       pltpu.VMEM((2, SC_CHUNK, d), table.dtype),
                       pltpu.SemaphoreType.DMA((2,)),
                       pltpu.SemaphoreType.DMA((2,))],
    )
    def k(table_hbm, idx_hbm, out_hbm, idx_v, buf, gsem, osem):
        wid = lax.axis_index("s") * SC_CORES + lax.axis_index("c")
        base = wid * per_w
        pltpu.sync_copy(idx_hbm.at[wid], idx_v)

        def gather(c, b):
            return pltpu.make_async_copy(table_hbm.at[idx_v.at[c]], buf.at[b], gsem.at[b])

        def put(c, b):
            return pltpu.make_async_copy(buf.at[b], out_hbm.at[pl.ds(base + c * SC_CHUNK, SC_CHUNK)],
                                         osem.at[b])

        gather(0, 0).start()

        @pl.loop(0, n_ch, step=2)
        def _(c):
            for b in range(2):
                cc = c + b
                gather(cc, b).wait()

                @pl.when(cc + 1 < n_ch)
                def _():
                    @pl.when(cc >= 1)
                    def _():
                        put(cc - 1, 1 - b).wait()
                    gather(cc + 1, 1 - b).start()

                put(cc, b).start()

        put(n_ch - 2, 0).wait()
        put(n_ch - 1, 1).wait()

    return k(table, idx.reshape(SC_WORKERS, n_ch, SC_CHUNK))


def _sc_scatter_rows(x, pos, n_out):
    t, d = x.shape
    kk = pos.shape[0]
    assert t % (SC_WORKERS * 2 * SC_CHUNK) == 0, t
    per_w = t // SC_WORKERS
    n_ch = per_w // SC_CHUNK
    pos_w = pos.reshape(kk, SC_WORKERS, n_ch, SC_CHUNK).transpose(1, 2, 0, 3)
    pos_w = pos_w.reshape(SC_WORKERS, n_ch * kk, SC_CHUNK)

    @functools.partial(
        pl.kernel, mesh=_sc_mesh(),
        out_type=jax.ShapeDtypeStruct((n_out, d), x.dtype),
        scratch_types=[pltpu.VMEM((n_ch * kk, SC_CHUNK), jnp.int32),
                       pltpu.VMEM((2, SC_CHUNK, d), x.dtype),
                       pltpu.SemaphoreType.DMA((2,)),
                       pltpu.SemaphoreType.DMA((2,))],
    )
    def k(x_hbm, pos_hbm, out_hbm, idx_v, buf, isem, osem):
        wid = lax.axis_index("s") * SC_CORES + lax.axis_index("c")
        base = wid * per_w
        pltpu.sync_copy(pos_hbm.at[wid], idx_v)

        def get(c, b):
            return pltpu.make_async_copy(x_hbm.at[pl.ds(base + c * SC_CHUNK, SC_CHUNK)], buf.at[b],
                                         isem.at[b])

        def put(c, j, b):
            return pltpu.make_async_copy(buf.at[b], out_hbm.at[idx_v.at[c * kk + j]], osem.at[b])

        get(0, 0).start()

        @pl.loop(0, n_ch, step=2)
        def _(c):
            for b in range(2):
                cc = c + b
                get(cc, b).wait()

                @pl.when(cc + 1 < n_ch)
                def _():
                    @pl.when(cc >= 1)
                    def _():
                        for j in range(kk):
                            put(cc - 1, j, 1 - b).wait()
                    get(cc + 1, 1 - b).start()

                for j in range(kk):
                    put(cc, j, b).start()

        for j in range(kk):
            put(n_ch - 2, j, 0).wait()
        for j in range(kk):
            put(n_ch - 1, j, 1).wait()

    return k(x, pos_w)


def _row(v):
    return v.reshape(1, -1).astype(F32)


def _const_spec(shape):
    nd = len(shape)
    return pl.BlockSpec(shape, lambda *_: (0,) * nd)


def _pad_heads(w, width):
    r, h, _ = w.shape
    return jnp.pad(w, ((0, 0), (0, 0), (0, HEAD_PAD - width))).reshape(r, h * HEAD_PAD)


def _half_rotate(w):
    half = QK_ROPE // 2
    return jnp.concatenate([-w[..., half:], w[..., :half]], axis=-1)


def kernel(x, positions, ln_in_g, ln_in_b, w_in, q_norm_g, kv_norm_g, w_uq, w_ukv, lambda_re, lambda_im, log_step, b_re, b_im, c_re, c_im, d_skip, w_glu, b_glu, attn_out_g, ssm_out_g, w_o, ln1_g, ln1_b, w_router, router_bias, w_gate, w_up, w_down, ws_gate, ws_up, ws_down, ln2_g, ln2_b):
    B, S, D = x.shape
    T = B * S
    assert DEPTH == 1 and w_in.shape[0] == DEPTH
    assert S % QUERY_TILE == 0 and S % TOKEN_TILE == 0 and S % S5_TIME_TILE == 0, S
    assert T % (MOE_SLABS * TOKEN_TILE) == 0 and (T // MOE_SLABS * TOP_K) % EXPERT_ROW_TILE == 0, T
    l = 0
    ssm_width = w_glu.shape[-1]
    n_groups = ssm_width // SSM_GROUP
    n_state = n_groups * SSM_STATE
    mla_width = MLA_HEADS * V_DIM
    qk_pad = MLA_HEADS * HEAD_PAD
    cparams = functools.partial(pltpu.CompilerParams, vmem_limit_bytes=VMEM_LIMIT)

    s1, s2, s3 = Q_RANK, Q_RANK + KV_RANK, Q_RANK + KV_RANK + QK_ROPE
    wi = w_in[l]
    w_kr = wi[:, s2:s3]
    pad_rope = lambda w: jnp.pad(w, ((0, 0), (QK_NOPE, HEAD_PAD - QK_NOPE - QK_ROPE)))
    w1 = jnp.concatenate([wi[:, :s2], wi[:, s3:], pad_rope(w_kr), pad_rope(_half_rotate(w_kr))],
                         axis=1).astype(BF16)
    wq = w_uq[l]
    zeros_nope = jnp.zeros(wq.shape[:2] + (QK_NOPE,), wq.dtype)
    wq_main = _pad_heads(wq, QK_NOPE + QK_ROPE).astype(BF16)
    wq_rot = _pad_heads(jnp.concatenate([zeros_nope, _half_rotate(wq[..., QK_NOPE:])], axis=-1),
                        QK_NOPE + QK_ROPE).astype(BF16)
    wkv = w_ukv[l]
    wk = _pad_heads(wkv[..., :QK_NOPE], QK_NOPE).astype(BF16)
    wv = _pad_heads(wkv[..., QK_NOPE:], V_DIM).astype(BF16)
    half = QK_ROPE // 2
    inv_freq = ROPE_THETA ** (-jnp.arange(half, dtype=F32) / half)
    freq = jnp.concatenate([inv_freq, inv_freq]).reshape(QK_ROPE, 1)
    pos_f = positions.astype(F32).reshape(1, T)

    tm = min(TOKEN_TILE, T)
    w1_cols = w1.shape[1]
    tok = lambda width: pl.BlockSpec((tm, width), lambda i: (i, 0))
    xn, q, k, v, u = pl.pallas_call(
        functools.partial(_inproj_kernel, ssm_width=ssm_width),
        grid=(T // tm,),
        in_specs=[tok(D), pl.BlockSpec((1, tm), lambda i: (0, i)), _const_spec((1, D)), _const_spec((1, D)),
                  _const_spec((D, w1_cols)), _const_spec((1, Q_RANK)), _const_spec((1, KV_RANK)),
                  _const_spec((Q_RANK, qk_pad)), _const_spec((Q_RANK, qk_pad)),
                  _const_spec((KV_RANK, qk_pad)), _const_spec((KV_RANK, qk_pad)),
                  _const_spec((QK_ROPE, 1))],
        out_specs=[tok(D), tok(qk_pad), tok(qk_pad), tok(qk_pad), tok(ssm_width)],
        out_shape=[jax.ShapeDtypeStruct((T, D), F32), jax.ShapeDtypeStruct((T, qk_pad), BF16),
                   jax.ShapeDtypeStruct((T, qk_pad), BF16), jax.ShapeDtypeStruct((T, qk_pad), BF16),
                   jax.ShapeDtypeStruct((T, ssm_width), F32)],
        compiler_params=cparams(dimension_semantics=("parallel",)),
        name="inproj",
    )(x.reshape(T, D), pos_f, _row(ln_in_g), _row(ln_in_b), w1, _row(q_norm_g[l]), _row(kv_norm_g[l]),
      wq_main, wq_rot, wk, wv, freq)

    tq = min(QUERY_TILE, S)
    nq = S // tq
    att = pl.pallas_call(
        functools.partial(_attn_kernel, tq=tq, heads=MLA_HEADS, n_qtiles=nq),
        grid=(B, nq),
        in_specs=[pl.BlockSpec((tq, qk_pad), lambda b, i: (b * nq + i, 0)),
                  pl.BlockSpec((S, qk_pad), lambda b, i: (b, 0)),
                  pl.BlockSpec((S, qk_pad), lambda b, i: (b, 0))],
        out_specs=pl.BlockSpec((tq, mla_width), lambda b, i: (b * nq + i, 0)),
        out_shape=jax.ShapeDtypeStruct((T, mla_width), BF16),
        compiler_params=cparams(dimension_semantics=("parallel", "arbitrary")),
        name="attention",
    )(q, k, v)

    lam = lax.complex(jnp.minimum(lambda_re[l].astype(F32), -1e-4), lambda_im[l].astype(F32))
    step = jnp.exp(log_step[l].astype(F32))[:, None]
    lam_bar = jnp.exp(lam * step)
    b_bar = ((lam_bar - 1.0) / lam)[..., None] * lax.complex(b_re[l].astype(F32), b_im[l].astype(F32))
    n_slab = ssm_width // LANES
    g_per_slab = n_groups // n_slab
    slab = n_state // n_slab
    eye = jnp.eye(g_per_slab, dtype=F32)

    def expand_in(bpart):
        bt = bpart.transpose(0, 2, 1).reshape(n_slab, g_per_slab, SSM_GROUP, 1, SSM_STATE)
        return (bt * eye[None, :, None, :, None]).reshape(n_slab, LANES, slab)

    def expand_out(cpart):
        ct = cpart.transpose(0, 2, 1).reshape(n_slab, g_per_slab, SSM_STATE, 1, SSM_GROUP)
        return (ct * eye[None, :, None, :, None]).reshape(n_slab, slab, LANES)

    win = jnp.concatenate([expand_in(jnp.real(b_bar)), expand_in(jnp.imag(b_bar))],
                          axis=2).astype(BF16)
    cre = expand_out(c_re[l].astype(F32)).astype(BF16)
    cim = expand_out(-c_im[l].astype(F32)).astype(BF16)
    a_re = jnp.real(lam_bar).reshape(1, n_state)
    a_im = jnp.imag(lam_bar).reshape(1, n_state)

    lt = min(S5_TIME_TILE, S)
    ssm = pl.pallas_call(
        functools.partial(_s5_kernel, batch=B, lt=lt, n_state=n_state),
        grid=(S // lt,),
        in_specs=[pl.BlockSpec((B, lt, ssm_width), lambda t: (0, t, 0)),
                  _const_spec(win.shape), _const_spec((1, n_state)), _const_spec((1, n_state)),
                  _const_spec(cre.shape), _const_spec(cim.shape), _const_spec((1, ssm_width)),
                  _const_spec((ssm_width, ssm_width)), _const_spec((1, ssm_width))],
        out_specs=pl.BlockSpec((B, lt, ssm_width), lambda t: (0, t, 0)),
        out_shape=jax.ShapeDtypeStruct((B, S, ssm_width), BF16),
        scratch_shapes=[pltpu.VMEM((2 * n_state // LANES, B * lt, LANES), F32),
                        pltpu.VMEM((n_state // LANES, B, LANES), F32),
                        pltpu.VMEM((n_state // LANES, B, LANES), F32),
                        pltpu.VMEM((ssm_width // LANES, B * lt, LANES), F32),
                        pltpu.VMEM((B * lt, ssm_width), F32)],
        compiler_params=cparams(dimension_semantics=("arbitrary",)),
        name="s5",
    )(u.reshape(B, S, ssm_width), win, a_re, a_im, cre, cim, _row(d_skip[l]),
      w_glu[l].astype(BF16), _row(b_glu[l]))

    wo = w_o[l].astype(BF16)
    assert mla_width == ssm_width
    wr_t = w_router[l].T.astype(F32)
    wr_hi = wr_t.astype(BF16)
    wr_lo = (wr_t - wr_hi.astype(F32)).astype(BF16)
    half = D // 2
    n_slabs = MOE_SLABS
    ts = T // n_slabs
    nt = ts // tm
    ssm2 = ssm.reshape(T, ssm_width)
    rbias = router_bias[l].astype(F32).reshape(N_EXPERTS, 1)
    kt = lambda dt: jax.ShapeDtypeStruct((TOP_K, ts), dt)
    tmx = min(MIX_TILE, ts)
    ntx = ts // tmx
    k_spec = pl.BlockSpec((TOP_K, tmx), lambda i: (0, i))

    def route(s):
        tok_s = lambda width: pl.BlockSpec((tmx, width), lambda i: (i + s * ntx, 0))
        tok = lambda width: pl.BlockSpec((tmx, width), lambda i: (i, 0))
        return pl.pallas_call(
            _mix_kernel,
            grid=(ntx,),
            in_specs=[tok_s(mla_width), tok_s(ssm_width), tok_s(D), _const_spec((1, mla_width)),
                      _const_spec((1, ssm_width)), pl.BlockSpec((mla_width, D), lambda i: (0, 0)),
                      pl.BlockSpec((ssm_width, D), lambda i: (1, 0)), _const_spec((1, D)), _const_spec((1, D)),
                      _const_spec((N_EXPERTS, D)), _const_spec((N_EXPERTS, D)),
                      _const_spec((N_EXPERTS, 1))],
            out_specs=[tok(D), tok(half), k_spec, tok(TOP_K), k_spec, _const_spec((N_EXPERTS, LANES))],
            out_shape=[jax.ShapeDtypeStruct((ts, D), F32), jax.ShapeDtypeStruct((ts, half), jnp.int32),
                       kt(jnp.int32), jax.ShapeDtypeStruct((ts, TOP_K), F32), kt(jnp.int32),
                       jax.ShapeDtypeStruct((N_EXPERTS, LANES), F32)],
            scratch_shapes=[pltpu.VMEM((N_EXPERTS, 1), F32), pltpu.VMEM((tmx, tmx), BF16)],
            compiler_params=cparams(dimension_semantics=("arbitrary",)),
            name="mix_router",
        )(att, ssm2, xn, _row(attn_out_g[l]), _row(ssm_out_g[l]), wo, wo,
          _row(ln1_g[l]), _row(ln1_b[l]), wr_hi, wr_lo, rbias)

    tr = EXPERT_ROW_TILE
    n_tiles = (ts * TOP_K) // tr + N_EXPERTS
    n_rows = n_tiles * tr
    tp = min(POSITIONS_TILE, ts)

    def dispatch(x1p, idx_k, rank_k, counts):
        cnt = counts[:, 0].astype(jnp.int32)
        tiles_e = (cnt + tr - 1) // tr
        tile_end = jnp.cumsum(tiles_e)
        tile_start = tile_end - tiles_e
        tile_ids = jnp.arange(n_tiles, dtype=jnp.int32)
        tile_expert = jnp.sum((tile_end[None, :] <= tile_ids[:, None]).astype(jnp.int32), axis=1)
        tile_expert = jnp.minimum(tile_expert, N_EXPERTS - 1)
        owner = (tile_start[None, :] <= tile_ids[:, None]) & (tile_ids[:, None] < tile_end[None, :])
        left = jnp.sum(jnp.where(owner, cnt[None, :] - (tile_ids[:, None] - tile_start[None, :]) * tr, 0),
                       axis=1)
        tile_valid = jnp.clip(left, 0, tr).astype(jnp.int32)
        pos = pl.pallas_call(
            _positions_kernel,
            grid_spec=pltpu.PrefetchScalarGridSpec(
                num_scalar_prefetch=1, grid=(ts // tp,),
                in_specs=[pl.BlockSpec((TOP_K, tp), lambda i, off: (0, i)),
                          pl.BlockSpec((TOP_K, tp), lambda i, off: (0, i))],
                out_specs=pl.BlockSpec((TOP_K, tp), lambda i, off: (0, i))),
            out_shape=kt(jnp.int32),
            name="positions",
        )((tile_start * tr).astype(jnp.int32), idx_k, rank_k)
        return _sc_scatter_rows(x1p, pos, n_rows), pos, tile_expert, tile_valid

    wg, wu, wd = w_gate[l], w_up[l], w_down[l]
    ff = wg.shape[-1]

    def experts(xs, tile_expert, tile_valid):
        return pl.pallas_call(
            _experts_kernel,
            grid_spec=pltpu.PrefetchScalarGridSpec(
                num_scalar_prefetch=2, grid=(n_tiles,),
                in_specs=[pl.BlockSpec((tr, half), lambda i, te, tv: (i, 0)),
                          pl.BlockSpec((1, D, ff), lambda i, te, tv: (te[i], 0, 0)),
                          pl.BlockSpec((1, D, ff), lambda i, te, tv: (te[i], 0, 0)),
                          pl.BlockSpec((1, ff, D), lambda i, te, tv: (te[i], 0, 0))],
                out_specs=pl.BlockSpec((tr, half), lambda i, te, tv: (i, 0)),
                scratch_shapes=[pltpu.VMEM((D, ff), BF16), pltpu.VMEM((D, ff), BF16),
                                pltpu.VMEM((ff, D), BF16)]),
            out_shape=jax.ShapeDtypeStruct((n_rows, half), jnp.int32),
            compiler_params=cparams(dimension_semantics=("arbitrary",)),
            name="experts",
        )(tile_expert, tile_valid, xs, wg, wu, wd)

    shared = (ws_gate[l].astype(BF16), ws_up[l].astype(BF16), ws_down[l].astype(BF16))

    def combine(s, out_so_far, yg, gate_k, x1):
        specs = [pl.BlockSpec((TOP_K, tm, half), lambda i: (0, i, 0)),
                 pl.BlockSpec((tm, TOP_K), lambda i: (i, 0)), tok(D),
                 _const_spec((D, ff)), _const_spec((D, ff)), _const_spec((ff, D)),
                 _const_spec((1, D)), _const_spec((1, D))]
        args = (yg.reshape(TOP_K, ts, half), gate_k, x1, *shared, _row(ln2_g[l]), _row(ln2_b[l]))
        body, aliases = _combine_kernel, {}
        if out_so_far is not None:
            specs = [pl.BlockSpec(memory_space=pl.ANY)] + specs
            args = (out_so_far,) + args
            body, aliases = _combine_into_kernel, {0: 0}
        return pl.pallas_call(
            body,
            grid=(nt,),
            in_specs=specs,
            out_specs=pl.BlockSpec((tm, D), lambda i: (i + s * nt, 0)),
            out_shape=jax.ShapeDtypeStruct((T, D), F32),
            input_output_aliases=aliases,
            compiler_params=cparams(dimension_semantics=("parallel",)),
            name="combine",
        )(*args)

    routed, moved = [], []
    for s in range(n_slabs):
        x1, x1p, idx_k, gate_k, rank_k, counts = route(s)
        routed.append((x1, gate_k))
        moved.append(dispatch(x1p, idx_k, rank_k, counts))
    gathered = []
    for xs, pos, tile_expert, tile_valid in moved:
        gathered.append(_sc_gather_rows(experts(xs, tile_expert, tile_valid), pos.reshape(TOP_K * ts)))
    out = None
    for s in range(n_slabs):
        out = combine(s, out, gathered[s], routed[s][1], routed[s][0])
    return out.reshape(B, S, D)
```

```python
import functools

import jax
import jax.numpy as jnp
from jax import lax
from jax.experimental import pallas as pl
from jax.experimental.pallas import tpu as pltpu
from jax.experimental.pallas import tpu_sc as plsc

CHUNK = 64
MLA_HEADS = 8
QK_NOPE = 64
QK_ROPE = 32
V_DIM = 64
Q_RANK = 256
KV_RANK = 128
ROPE_THETA = 10000.0
SSM_GROUP = 16
SSM_STATE = 64
N_EXPERTS = 64
TOP_K = 8
N_GROUPS = 8
TOP_GROUPS = 4
ROUTED_SCALE = 2.5
DEPTH = 1
ALPHA = (2.0 * DEPTH) ** 0.25
EPS = 1e-5
LOG2_E = 1.4426950408889634

LANES = 128
HEAD_PAD = LANES
VMEM_LIMIT = 56 * 1024 * 1024
TOKEN_TILE = 512
MIX_TILE = 1024
QUERY_TILE = 256
S5_TIME_TILE = 64
POSITIONS_TILE = 2048
EXPERT_ROW_TILE = 1024
MOE_SLABS = 1

BF16 = jnp.bfloat16
F32 = jnp.float32
NT_DIMS = (((1,), (1,)), ((), ()))


def _dot(a, b):
    return jnp.dot(a, b, preferred_element_type=F32)


def _layer_norm(x, g, b):
    mu = jnp.mean(x, axis=-1, keepdims=True)
    xc = x - mu
    var = jnp.mean(xc * xc, axis=-1, keepdims=True)
    return xc * lax.rsqrt(var + EPS) * g + b


def _rms_norm(x, g):
    return x * lax.rsqrt(jnp.mean(x * x, axis=-1, keepdims=True) + EPS) * g


def _inproj_kernel(x_ref, pos_ref, lng_ref, lnb_ref, w1_ref, qg_ref, kvg_ref,
                   wq_ref, wqr_ref, wk_ref, wv_ref, freq_ref,
                   xn_ref, q_ref, k_ref, v_ref, u_ref, *, ssm_width):
    xn = _layer_norm(x_ref[...], lng_ref[...], lnb_ref[...])
    xn_ref[...] = xn
    h = _dot(xn.astype(BF16), w1_ref[...])
    o1 = Q_RANK
    o2 = o1 + KV_RANK
    o3 = o2 + ssm_width
    o4 = o3 + HEAD_PAD
    cq = h[:, :o1]
    ckv = h[:, o1:o2]
    u_ref[...] = h[:, o2:o3]
    kr_raw = h[:, o3:o4]
    kr_rot = h[:, o4:o4 + HEAD_PAD]
    cqn = _rms_norm(cq, qg_ref[...]).astype(BF16)
    ckvn = _rms_norm(ckv, kvg_ref[...]).astype(BF16)

    tm = x_ref.shape[0]
    ang_t = freq_ref[...] * pos_ref[...]

    def to_token_rows(t):
        padded = jnp.concatenate([jnp.zeros((QK_NOPE, tm), F32), t,
                                  jnp.zeros((HEAD_PAD - QK_NOPE - QK_ROPE, tm), F32)], axis=0)
        return padded.T

    c = to_token_rows(jnp.cos(ang_t))
    s = to_token_rows(jnp.sin(ang_t))
    lane = lax.broadcasted_iota(jnp.int32, (1, HEAD_PAD), 1)
    scale = (QK_NOPE + QK_ROPE) ** -0.5 * LOG2_E
    cos1 = (c + jnp.where(lane < QK_NOPE, 1.0, 0.0)) * scale
    sin1 = s * scale
    cos_t = jnp.concatenate([cos1] * MLA_HEADS, axis=1)
    sin_t = jnp.concatenate([sin1] * MLA_HEADS, axis=1)
    q = _dot(cqn, wq_ref[...]) * cos_t + _dot(cqn, wqr_ref[...]) * sin_t
    q_ref[...] = q.astype(BF16)

    kr = kr_raw * c + kr_rot * s
    k = _dot(ckvn, wk_ref[...]) + jnp.concatenate([kr] * MLA_HEADS, axis=1)
    k_ref[...] = k.astype(BF16)
    ones_col = jnp.concatenate([jnp.where(lane == V_DIM, 1.0, 0.0)] * MLA_HEADS, axis=1)
    v_ref[...] = (_dot(ckvn, wv_ref[...]) + ones_col).astype(BF16)


def _attn_kernel(q_ref, k_ref, v_ref, o_ref, *, tq, heads, n_qtiles):
    qi = pl.program_id(1)
    row_chunk = lax.broadcasted_iota(jnp.int32, (tq, tq), 0) // CHUNK
    col_chunk = lax.broadcasted_iota(jnp.int32, (tq, tq), 1) // CHUNK
    diag_mask = row_chunk >= col_chunk

    def tile(n_blocks):
        keys = n_blocks * tq
        cols = [slice(h * HEAD_PAD, (h + 1) * HEAD_PAD) for h in range(heads)]

        def mask_diag(s):
            s_diag = jnp.where(diag_mask, s[:, keys - tq:], -jnp.inf)
            return s_diag if n_blocks == 1 else jnp.concatenate([s[:, :keys - tq], s_diag], axis=1)

        ss = [lax.dot_general(q_ref[:, c], k_ref[:keys, c], NT_DIMS, preferred_element_type=F32)
              for c in cols]
        ss = [mask_diag(s) for s in ss]
        ms = [jnp.max(s, axis=-1, keepdims=True) for s in ss]
        ps = [jnp.exp2(s - m).astype(BF16) for s, m in zip(ss, ms)]
        accs = [_dot(p, v_ref[:keys, c]) for p, c in zip(ps, cols)]
        outs = [acc[:, :V_DIM] / acc[:, V_DIM:V_DIM + 1] for acc in accs]
        o_ref[...] = jnp.concatenate(outs, axis=1).astype(o_ref.dtype)

    for c in range(n_qtiles):
        pl.when(qi == c)(functools.partial(tile, c + 1))


def _s5_kernel(u_ref, win_ref, are_ref, aim_ref, cre_ref, cim_ref, dskip_ref,
               wglu_ref, bglu_ref, o_ref, vx_ref, hre_ref, him_ref, io_ref, utm_ref, *,
               batch, lt, n_state):
    ti = pl.program_id(0)

    @pl.when(ti == 0)
    def _():
        hre_ref[...] = jnp.zeros_like(hre_ref)
        him_ref[...] = jnp.zeros_like(him_ref)

    width = u_ref.shape[-1]
    n_slab = width // LANES
    n_tiles = n_state // LANES
    slab_tiles = n_tiles // n_slab
    for b in range(batch):
        for c in range(n_slab):
            io_ref[c, b * lt:(b + 1) * lt, :] = u_ref[b, :, c * LANES:(c + 1) * LANES]

    def slab_tiles_of(j):
        return range(j * slab_tiles, (j + 1) * slab_tiles)

    def expand(j):
        for t in range(lt):
            utm_ref[t * batch:(t + 1) * batch, j * LANES:(j + 1) * LANES] = (
                io_ref[j, pl.ds(t, batch, stride=lt), :])
        ub = utm_ref[:, j * LANES:(j + 1) * LANES].astype(BF16)
        vj = _dot(ub, win_ref[j])
        for i, c in enumerate(slab_tiles_of(j)):
            vx_ref[c] = vj[:, i * LANES:(i + 1) * LANES]
            vx_ref[n_tiles + c] = vj[:, (slab_tiles + i) * LANES:(slab_tiles + i + 1) * LANES]

    def scan(j):
        tiles = slab_tiles_of(j)
        ar = [jnp.broadcast_to(are_ref[:, c * LANES:(c + 1) * LANES], (batch, LANES)) for c in tiles]
        ai = [jnp.broadcast_to(aim_ref[:, c * LANES:(c + 1) * LANES], (batch, LANES)) for c in tiles]
        hr = [hre_ref[c] for c in tiles]
        hi = [him_ref[c] for c in tiles]
        for t in range(lt):
            rows = slice(t * batch, (t + 1) * batch)
            for n, c in enumerate(tiles):
                nr = ar[n] * hr[n] - ai[n] * hi[n] + vx_ref[c, rows, :]
                ni = ar[n] * hi[n] + ai[n] * hr[n] + vx_ref[n_tiles + c, rows, :]
                vx_ref[c, rows, :] = nr
                vx_ref[n_tiles + c, rows, :] = ni
                hr[n], hi[n] = nr, ni
        for n, c in enumerate(tiles):
            hre_ref[c] = hr[n]
            him_ref[c] = hi[n]

    def project(j):
        tiles = slab_tiles_of(j)
        xr = jnp.concatenate([vx_ref[c].astype(BF16) for c in tiles], axis=1)
        xi = jnp.concatenate([vx_ref[n_tiles + c].astype(BF16) for c in tiles], axis=1)
        return _dot(xr, cre_ref[j]) + _dot(xi, cim_ref[j])

    ys = [None] * n_slab
    for j in range(n_slab + 2):
        if j < n_slab:
            expand(j)
        if 1 <= j <= n_slab:
            scan(j - 1)
        if j >= 2:
            ys[j - 2] = project(j - 2)
    y = jnp.concatenate(ys, axis=1) + dskip_ref[...] * utm_ref[...]
    y = jax.nn.gelu(y)
    z = _dot(y.astype(BF16), wglu_ref[...]) + bglu_ref[...]
    out = y * jax.nn.sigmoid(z)
    for c in range(n_slab):
        io_ref[c] = out[:, c * LANES:(c + 1) * LANES]
    for b in range(batch):
        for c in range(n_slab):
            o_ref[b, :, c * LANES:(c + 1) * LANES] = io_ref[c, pl.ds(b, lt, stride=batch), :].astype(o_ref.dtype)


def _router_gates(logits_t, rbias):
    n_exp, tm = logits_t.shape
    per_group = n_exp // N_GROUPS
    scores = jax.nn.sigmoid(logits_t)
    sel = scores + rbias
    neg_inf = -jnp.inf
    sub_iota = lax.broadcasted_iota(jnp.int32, (per_group, tm), 0).astype(F32)
    group_score = []
    for g in range(N_GROUPS):
        sg = sel[g * per_group:(g + 1) * per_group, :]
        m1 = jnp.max(sg, axis=0, keepdims=True)
        first = jnp.min(jnp.where(sg == m1, sub_iota, float(per_group)), axis=0, keepdims=True)
        m2 = jnp.max(jnp.where(sub_iota == first, neg_inf, sg), axis=0, keepdims=True)
        group_score.append(m1 + m2)
    masked = []
    for g in range(N_GROUPS):
        rank = jnp.zeros((1, tm), F32)
        for g2 in range(N_GROUPS):
            if g2 == g:
                continue
            ahead = (group_score[g2] >= group_score[g]) if g2 < g else (group_score[g2] > group_score[g])
            rank = rank + jnp.where(ahead, 1.0, 0.0)
        keep = rank < float(TOP_GROUPS)
        masked.append(jnp.where(keep, sel[g * per_group:(g + 1) * per_group, :], neg_inf))
    cur = jnp.concatenate(masked, axis=0)
    iota = lax.broadcasted_iota(jnp.int32, (n_exp, tm), 0).astype(F32)
    chosen = jnp.zeros((n_exp, tm), F32)
    picks, weights = [], []
    for _ in range(TOP_K):
        m = jnp.max(cur, axis=0, keepdims=True)
        idx = jnp.min(jnp.where(cur == m, iota, float(n_exp)), axis=0, keepdims=True)
        pick = iota == idx
        chosen = jnp.where(pick, 1.0, chosen)
        cur = jnp.where(pick, neg_inf, cur)
        picks.append(idx)
        weights.append(jnp.sum(jnp.where(pick, scores, 0.0), axis=0, keepdims=True))
    idx_k = jnp.concatenate(picks, axis=0)
    w_k = jnp.concatenate(weights, axis=0)
    gate_k = w_k / jnp.sum(w_k, axis=0, keepdims=True) * ROUTED_SCALE
    return idx_k, gate_k, chosen


def _pack_bf16_pairs(x):
    n = x.shape[1] // 2
    hi = lax.bitcast_convert_type(x[:, :n].astype(BF16).astype(F32), jnp.int32)
    lo = lax.bitcast_convert_type(x[:, n:].astype(BF16).astype(F32), jnp.int32)
    return hi | lax.shift_right_logical(lo, 16)


def _unpack_bf16_pairs(p):
    hi = lax.bitcast_convert_type(p & jnp.int32(-65536), F32).astype(BF16)
    lo = lax.bitcast_convert_type(lax.shift_left(p, 16), F32).astype(BF16)
    return hi, lo


def _mix_kernel(att_ref, ssm_ref, xn_ref, ag_ref, sg_ref, woa_ref, wos_ref,
                g_ref, b_ref, wrh_ref, wrl_ref, rb_ref,
                x1_ref, x1p_ref, idx_ref, gate_ref, rank_ref, cnt_ref, carry_ref, before_ref):
    @pl.when(pl.program_id(0) == 0)
    def _():
        carry_ref[...] = jnp.zeros_like(carry_ref)
        tm = before_ref.shape[0]
        before = (lax.broadcasted_iota(jnp.int32, (tm, tm), 0)
                  < lax.broadcasted_iota(jnp.int32, (tm, tm), 1))
        before_ref[...] = jnp.where(before, 1.0, 0.0).astype(BF16)

    an = _rms_norm(att_ref[...].astype(F32), ag_ref[...]).astype(BF16)
    sn = _rms_norm(ssm_ref[...].astype(F32), sg_ref[...]).astype(BF16)
    mix = _dot(an, woa_ref[...]) + _dot(sn, wos_ref[...])
    x1 = _layer_norm(ALPHA * xn_ref[...] + mix, g_ref[...], b_ref[...])
    x1_ref[...] = x1
    x1p_ref[...] = _pack_bf16_pairs(x1)
    x_hi = x1.astype(BF16)
    x_lo = (x1 - x_hi.astype(F32)).astype(BF16)
    dg = functools.partial(lax.dot_general, dimension_numbers=NT_DIMS, preferred_element_type=F32)
    logits_t = dg(wrh_ref[...], x_hi) + dg(wrl_ref[...], x_hi) + dg(wrh_ref[...], x_lo)
    idx_k, gate_k, chosen = _router_gates(logits_t, rb_ref[...])
    idx_ref[...] = idx_k.astype(jnp.int32)
    gate_ref[...] = gate_k.T

    n_exp, tm = chosen.shape
    excl = _dot(chosen.astype(BF16), before_ref[...])
    rank_full = carry_ref[...] + excl
    iota = lax.broadcasted_iota(jnp.int32, (n_exp, tm), 0).astype(F32)
    ranks = [jnp.sum(jnp.where(iota == idx_k[k:k + 1, :], rank_full, 0.0), axis=0, keepdims=True)
             for k in range(TOP_K)]
    rank_ref[...] = jnp.concatenate(ranks, axis=0).astype(jnp.int32)
    total = carry_ref[...] + jnp.sum(chosen, axis=1, keepdims=True)
    carry_ref[...] = total
    cnt_ref[...] = jnp.broadcast_to(total, cnt_ref.shape)


def _positions_kernel(off_ref, idx_ref, rank_ref, pos_ref):
    idx = idx_ref[...]
    base = jnp.zeros(idx.shape, jnp.int32)
    for e in range(N_EXPERTS):
        base = jnp.where(idx == e, off_ref[e], base)
    pos_ref[...] = rank_ref[...] + base


def _swiglu(x_hi, x_lo, wg_ref, wu_ref, wd_ref):
    half = x_hi.shape[1]
    hg = _dot(x_hi, wg_ref[:half, :]) + _dot(x_lo, wg_ref[half:, :])
    hu = _dot(x_hi, wu_ref[:half, :]) + _dot(x_lo, wu_ref[half:, :])
    h = jax.nn.silu(hg) * hu
    return _dot(h.astype(BF16), wd_ref[...])


def _experts_kernel(te_ref, valid_ref, xs_ref, wg_ref, wu_ref, wd_ref, ys_ref,
                    wgb_ref, wub_ref, wdb_ref):
    i = pl.program_id(0)
    valid = valid_ref[i]

    @pl.when((i == 0) | (te_ref[i] != te_ref[jnp.maximum(i - 1, 0)]))
    def _():
        wgb_ref[...] = wg_ref[0].astype(BF16)
        wub_ref[...] = wu_ref[0].astype(BF16)
        wdb_ref[...] = wd_ref[0].astype(BF16)

    @pl.when(valid > 0)
    def _():
        rows = lax.broadcasted_iota(jnp.int32, xs_ref.shape, 0)
        x_hi, x_lo = _unpack_bf16_pairs(jnp.where(rows < valid, xs_ref[...], 0))
        ys_ref[...] = _pack_bf16_pairs(_swiglu(x_hi, x_lo, wgb_ref, wub_ref, wdb_ref))

    @pl.when(valid == 0)
    def _():
        ys_ref[...] = jnp.zeros_like(ys_ref)


def _combine_kernel(yg_ref, gate_ref, x1_ref, wsg_ref, wsu_ref, wsd_ref, g_ref, b_ref, o_ref):
    x1 = x1_ref[...]
    half = x1.shape[1] // 2
    acc = _swiglu(x1[:, :half].astype(BF16), x1[:, half:].astype(BF16), wsg_ref, wsu_ref, wsd_ref)
    gates = gate_ref[...]
    acc_hi, acc_lo = acc[:, :half], acc[:, half:]
    for k in range(TOP_K):
        y_hi, y_lo = _unpack_bf16_pairs(yg_ref[k])
        gk = gates[:, k:k + 1]
        acc_hi = acc_hi + gk * y_hi.astype(F32)
        acc_lo = acc_lo + gk * y_lo.astype(F32)
    ffn = jnp.concatenate([acc_hi, acc_lo], axis=1)
    o_ref[...] = _layer_norm(ALPHA * x1 + ffn, g_ref[...], b_ref[...])


def _combine_into_kernel(prev_ref, *refs):
    del prev_ref
    _combine_kernel(*refs)


SC_CORES = 2
SC_SUBCORES = 16
SC_WORKERS = SC_CORES * SC_SUBCORES
SC_CHUNK = 64


def _sc_mesh():
    return plsc.VectorSubcoreMesh(core_axis_name="c", subcore_axis_name="s")


def _sc_gather_rows(table, idx):
    n = idx.shape[0]
    d = table.shape[1]
    assert n % (SC_WORKERS * 2 * SC_CHUNK) == 0, n
    per_w = n // SC_WORKERS
    n_ch = per_w // SC_CHUNK

    @functools.partial(
        pl.kernel, mesh=_sc_mesh(),
        out_type=jax.ShapeDtypeStruct((n, d), table.dtype),
        scratch_types=[pltpu.VMEM((n_ch, SC_CHUNK), jnp.int32),
                       pltpu.VMEM((2, SC_CHUNK, d), table.dtype),
                       pltpu.SemaphoreType.DMA((2,)),
                       pltpu.SemaphoreType.DMA((2,))],
    )
    def k(table_hbm, idx_hbm, out_hbm, idx_v, buf, gsem, osem):
        wid = lax.axis_index("s") * SC_CORES + lax.axis_index("c")
        base = wid * per_w
        pltpu.sync_copy(idx_hbm.at[wid], idx_v)

        def gather(c, b):
            return pltpu.make_async_copy(table_hbm.at[idx_v.at[c]], buf.at[b], gsem.at[b])

        def put(c, b):
            return pltpu.make_async_copy(buf.at[b], out_hbm.at[pl.ds(base + c * SC_CHUNK, SC_CHUNK)],
                                         osem.at[b])

        gather(0, 0).start()

        @pl.loop(0, n_ch, step=2)
        def _(c):
            for b in range(2):
                cc = c + b
                gather(cc, b).wait()

                @pl.when(cc + 1 < n_ch)
                def _():
                    @pl.when(cc >= 1)
                    def _():
                        put(cc - 1, 1 - b).wait()
                    gather(cc + 1, 1 - b).start()

                put(cc, b).start()

        put(n_ch - 2, 0).wait()
        put(n_ch - 1, 1).wait()

    return k(table, idx.reshape(SC_WORKERS, n_ch, SC_CHUNK))


def _sc_scatter_rows(x, pos, n_out):
    t, d = x.shape
    kk = pos.shape[0]
    assert t % (SC_WORKERS * 2 * SC_CHUNK) == 0, t
    per_w = t // SC_WORKERS
    n_ch = per_w // SC_CHUNK
    pos_w = pos.reshape(kk, SC_WORKERS, n_ch, SC_CHUNK).transpose(1, 2, 0, 3)
    pos_w = pos_w.reshape(SC_WORKERS, n_ch * kk, SC_CHUNK)

    @functools.partial(
        pl.kernel, mesh=_sc_mesh(),
        out_type=jax.ShapeDtypeStruct((n_out, d), x.dtype),
        scratch_types=[pltpu.VMEM((n_ch * kk, SC_CHUNK), jnp.int32),
                       pltpu.VMEM((2, SC_CHUNK, d), x.dtype),
                       pltpu.SemaphoreType.DMA((2,)),
                       pltpu.SemaphoreType.DMA((2,))],
    )
    def k(x_hbm, pos_hbm, out_hbm, idx_v, buf, isem, osem):
        wid = lax.axis_index("s") * SC_CORES + lax.axis_index("c")
        base = wid * per_w
        pltpu.sync_copy(pos_hbm.at[wid], idx_v)

        def get(c, b):
            return pltpu.make_async_copy(x_hbm.at[pl.ds(base + c * SC_CHUNK, SC_CHUNK)], buf.at[b],
                                         isem.at[b])

        def put(c, j, b):
            return pltpu.make_async_copy(buf.at[b], out_hbm.at[idx_v.at[c * kk + j]], osem.at[b])

        get(0, 0).start()

        @pl.loop(0, n_ch, step=2)
        def _(c):
            for b in range(2):
                cc = c + b
                get(cc, b).wait()

                @pl.when(cc + 1 < n_ch)
                def _():
                    @pl.when(cc >= 1)
                    def _():
                        for j in range(kk):
                            put(cc - 1, j, 1 - b).wait()
                    get(cc + 1, 1 - b).start()

                for j in range(kk):
                    put(cc, j, b).start()

        for j in range(kk):
            put(n_ch - 2, j, 0).wait()
        for j in range(kk):
            put(n_ch - 1, j, 1).wait()

    return k(x, pos_w)


def _row(v):
    return v.reshape(1, -1).astype(F32)


def _const_spec(shape):
    nd = len(shape)
    return pl.BlockSpec(shape, lambda *_: (0,) * nd)


def _pad_heads(w, width):
    r, h, _ = w.shape
    return jnp.pad(w, ((0, 0), (0, 0), (0, HEAD_PAD - width))).reshape(r, h * HEAD_PAD)


def _half_rotate(w):
    half = QK_ROPE // 2
    return jnp.concatenate([-w[..., half:], w[..., :half]], axis=-1)


def kernel(x, positions, ln_in_g, ln_in_b, w_in, q_norm_g, kv_norm_g, w_uq, w_ukv, lambda_re, lambda_im, log_step, b_re, b_im, c_re, c_im, d_skip, w_glu, b_glu, attn_out_g, ssm_out_g, w_o, ln1_g, ln1_b, w_router, router_bias, w_gate, w_up, w_down, ws_gate, ws_up, ws_down, ln2_g, ln2_b):
    B, S, D = x.shape
    T = B * S
    assert DEPTH == 1 and w_in.shape[0] == DEPTH
    assert S % QUERY_TILE == 0 and S % TOKEN_TILE == 0 and S % S5_TIME_TILE == 0, S
    assert T % (MOE_SLABS * TOKEN_TILE) == 0 and (T // MOE_SLABS * TOP_K) % EXPERT_ROW_TILE == 0, T
    l = 0
    ssm_width = w_glu.shape[-1]
    n_groups = ssm_width // SSM_GROUP
    n_state = n_groups * SSM_STATE
    mla_width = MLA_HEADS * V_DIM
    qk_pad = MLA_HEADS * HEAD_PAD
    cparams = functools.partial(pltpu.CompilerParams, vmem_limit_bytes=VMEM_LIMIT)

    s1, s2, s3 = Q_RANK, Q_RANK + KV_RANK, Q_RANK + KV_RANK + QK_ROPE
    wi = w_in[l]
    w_kr = wi[:, s2:s3]
    pad_rope = lambda w: jnp.pad(w, ((0, 0), (QK_NOPE, HEAD_PAD - QK_NOPE - QK_ROPE)))
    w1 = jnp.concatenate([wi[:, :s2], wi[:, s3:], pad_rope(w_kr), pad_rope(_half_rotate(w_kr))],
                         axis=1).astype(BF16)
    wq = w_uq[l]
    zeros_nope = jnp.zeros(wq.shape[:2] + (QK_NOPE,), wq.dtype)
    wq_main = _pad_heads(wq, QK_NOPE + QK_ROPE).astype(BF16)
    wq_rot = _pad_heads(jnp.concatenate([zeros_nope, _half_rotate(wq[..., QK_NOPE:])], axis=-1),
                        QK_NOPE + QK_ROPE).astype(BF16)
    wkv = w_ukv[l]
    wk = _pad_heads(wkv[..., :QK_NOPE], QK_NOPE).astype(BF16)
    wv = _pad_heads(wkv[..., QK_NOPE:], V_DIM).astype(BF16)
    half = QK_ROPE // 2
    inv_freq = ROPE_THETA ** (-jnp.arange(half, dtype=F32) / half)
    freq = jnp.concatenate([inv_freq, inv_freq]).reshape(QK_ROPE, 1)
    pos_f = positions.astype(F32).reshape(1, T)

    tm = min(TOKEN_TILE, T)
    w1_cols = w1.shape[1]
    tok = lambda width: pl.BlockSpec((tm, width), lambda i: (i, 0))
    xn, q, k, v, u = pl.pallas_call(
        functools.partial(_inproj_kernel, ssm_width=ssm_width),
        grid=(T // tm,),
        in_specs=[tok(D), pl.BlockSpec((1, tm), lambda i: (0, i)), _const_spec((1, D)), _const_spec((1, D)),
                  _const_spec((D, w1_cols)), _const_spec((1, Q_RANK)), _const_spec((1, KV_RANK)),
                  _const_spec((Q_RANK, qk_pad)), _const_spec((Q_RANK, qk_pad)),
                  _const_spec((KV_RANK, qk_pad)), _const_spec((KV_RANK, qk_pad)),
                  _const_spec((QK_ROPE, 1))],
        out_specs=[tok(D), tok(qk_pad), tok(qk_pad), tok(qk_pad), tok(ssm_width)],
        out_shape=[jax.ShapeDtypeStruct((T, D), F32), jax.ShapeDtypeStruct((T, qk_pad), BF16),
                   jax.ShapeDtypeStruct((T, qk_pad), BF16), jax.ShapeDtypeStruct((T, qk_pad), BF16),
                   jax.ShapeDtypeStruct((T, ssm_width), F32)],
        compiler_params=cparams(dimension_semantics=("parallel",)),
        name="inproj",
    )(x.reshape(T, D), pos_f, _row(ln_in_g), _row(ln_in_b), w1, _row(q_norm_g[l]), _row(kv_norm_g[l]),
      wq_main, wq_rot, wk, wv, freq)

    tq = min(QUERY_TILE, S)
    nq = S // tq
    att = pl.pallas_call(
        functools.partial(_attn_kernel, tq=tq, heads=MLA_HEADS, n_qtiles=nq),
        grid=(B, nq),
        in_specs=[pl.BlockSpec((tq, qk_pad), lambda b, i: (b * nq + i, 0)),
                  pl.BlockSpec((S, qk_pad), lambda b, i: (b, 0)),
                  pl.BlockSpec((S, qk_pad), lambda b, i: (b, 0))],
        out_specs=pl.BlockSpec((tq, mla_width), lambda b, i: (b * nq + i, 0)),
        out_shape=jax.ShapeDtypeStruct((T, mla_width), BF16),
        compiler_params=cparams(dimension_semantics=("parallel", "arbitrary")),
        name="attention",
    )(q, k, v)

    lam = lax.complex(jnp.minimum(lambda_re[l].astype(F32), -1e-4), lambda_im[l].astype(F32))
    step = jnp.exp(log_step[l].astype(F32))[:, None]
    lam_bar = jnp.exp(lam * step)
    b_bar = ((lam_bar - 1.0) / lam)[..., None] * lax.complex(b_re[l].astype(F32), b_im[l].astype(F32))
    n_slab = ssm_width // LANES
    g_per_slab = n_groups // n_slab
    slab = n_state // n_slab
    eye = jnp.eye(g_per_slab, dtype=F32)

    def expand_in(bpart):
        bt = bpart.transpose(0, 2, 1).reshape(n_slab, g_per_slab, SSM_GROUP, 1, SSM_STATE)
        return (bt * eye[None, :, None, :, None]).reshape(n_slab, LANES, slab)

    def expand_out(cpart):
        ct = cpart.transpose(0, 2, 1).reshape(n_slab, g_per_slab, SSM_STATE, 1, SSM_GROUP)
        return (ct * eye[None, :, None, :, None]).reshape(n_slab, slab, LANES)

    win = jnp.concatenate([expand_in(jnp.real(b_bar)), expand_in(jnp.imag(b_bar))],
                          axis=2).astype(BF16)
    cre = expand_out(c_re[l].astype(F32)).astype(BF16)
    cim = expand_out(-c_im[l].astype(F32)).astype(BF16)
    a_re = jnp.real(lam_bar).reshape(1, n_state)
    a_im = jnp.imag(lam_bar).reshape(1, n_state)

    lt = min(S5_TIME_TILE, S)
    ssm = pl.pallas_call(
        functools.partial(_s5_kernel, batch=B, lt=lt, n_state=n_state),
        grid=(S // lt,),
        in_specs=[pl.BlockSpec((B, lt, ssm_width), lambda t: (0, t, 0)),
                  _const_spec(win.shape), _const_spec((1, n_state)), _const_spec((1, n_state)),
                  _const_spec(cre.shape), _const_spec(cim.shape), _const_spec((1, ssm_width)),
                  _const_spec((ssm_width, ssm_width)), _const_spec((1, ssm_width))],
        out_specs=pl.BlockSpec((B, lt, ssm_width), lambda t: (0, t, 0)),
        out_shape=jax.ShapeDtypeStruct((B, S, ssm_width), BF16),
        scratch_shapes=[pltpu.VMEM((2 * n_state // LANES, B * lt, LANES), F32),
                        pltpu.VMEM((n_state // LANES, B, LANES), F32),
                        pltpu.VMEM((n_state // LANES, B, LANES), F32),
                        pltpu.VMEM((ssm_width // LANES, B * lt, LANES), F32),
                        pltpu.VMEM((B * lt, ssm_width), F32)],
        compiler_params=cparams(dimension_semantics=("arbitrary",)),
        name="s5",
    )(u.reshape(B, S, ssm_width), win, a_re, a_im, cre, cim, _row(d_skip[l]),
      w_glu[l].astype(BF16), _row(b_glu[l]))

    wo = w_o[l].astype(BF16)
    assert mla_width == ssm_width
    wr_t = w_router[l].T.astype(F32)
    wr_hi = wr_t.astype(BF16)
    wr_lo = (wr_t - wr_hi.astype(F32)).astype(BF16)
    half = D // 2
    n_slabs = MOE_SLABS
    ts = T // n_slabs
    nt = ts // tm
    ssm2 = ssm.reshape(T, ssm_width)
    rbias = router_bias[l].astype(F32).reshape(N_EXPERTS, 1)
    kt = lambda dt: jax.ShapeDtypeStruct((TOP_K, ts), dt)
    tmx = min(MIX_TILE, ts)
    ntx = ts // tmx
    k_spec = pl.BlockSpec((TOP_K, tmx), lambda i: (0, i))

    def route(s):
        tok_s = lambda width: pl.BlockSpec((tmx, width), lambda i: (i + s * ntx, 0))
        tok = lambda width: pl.BlockSpec((tmx, width), lambda i: (i, 0))
        return pl.pallas_call(
            _mix_kernel,
            grid=(ntx,),
            in_specs=[tok_s(mla_width), tok_s(ssm_width), tok_s(D), _const_spec((1, mla_width)),
                      _const_spec((1, ssm_width)), pl.BlockSpec((mla_width, D), lambda i: (0, 0)),
                      pl.BlockSpec((ssm_width, D), lambda i: (1, 0)), _const_spec((1, D)), _const_spec((1, D)),
                      _const_spec((N_EXPERTS, D)), _const_spec((N_EXPERTS, D)),
                      _const_spec((N_EXPERTS, 1))],
            out_specs=[tok(D), tok(half), k_spec, tok(TOP_K), k_spec, _const_spec((N_EXPERTS, LANES))],
            out_shape=[jax.ShapeDtypeStruct((ts, D), F32), jax.ShapeDtypeStruct((ts, half), jnp.int32),
                       kt(jnp.int32), jax.ShapeDtypeStruct((ts, TOP_K), F32), kt(jnp.int32),
                       jax.ShapeDtypeStruct((N_EXPERTS, LANES), F32)],
            scratch_shapes=[pltpu.VMEM((N_EXPERTS, 1), F32), pltpu.VMEM((tmx, tmx), BF16)],
            compiler_params=cparams(dimension_semantics=("arbitrary",)),
            name="mix_router",
        )(att, ssm2, xn, _row(attn_out_g[l]), _row(ssm_out_g[l]), wo, wo,
          _row(ln1_g[l]), _row(ln1_b[l]), wr_hi, wr_lo, rbias)

    tr = EXPERT_ROW_TILE
    n_tiles = (ts * TOP_K) // tr + N_EXPERTS
    n_rows = n_tiles * tr
    tp = min(POSITIONS_TILE, ts)

    def dispatch(x1p, idx_k, rank_k, counts):
        cnt = counts[:, 0].astype(jnp.int32)
        tiles_e = (cnt + tr - 1) // tr
        tile_end = jnp.cumsum(tiles_e)
        tile_start = tile_end - tiles_e
        tile_ids = jnp.arange(n_tiles, dtype=jnp.int32)
        tile_expert = jnp.sum((tile_end[None, :] <= tile_ids[:, None]).astype(jnp.int32), axis=1)
        tile_expert = jnp.minimum(tile_expert, N_EXPERTS - 1)
        owner = (tile_start[None, :] <= tile_ids[:, None]) & (tile_ids[:, None] < tile_end[None, :])
        left = jnp.sum(jnp.where(owner, cnt[None, :] - (tile_ids[:, None] - tile_start[None, :]) * tr, 0),
                       axis=1)
        tile_valid = jnp.clip(left, 0, tr).astype(jnp.int32)
        pos = pl.pallas_call(
            _positions_kernel,
            grid_spec=pltpu.PrefetchScalarGridSpec(
                num_scalar_prefetch=1, grid=(ts // tp,),
                in_specs=[pl.BlockSpec((TOP_K, tp), lambda i, off: (0, i)),
                          pl.BlockSpec((TOP_K, tp), lambda i, off: (0, i))],
                out_specs=pl.BlockSpec((TOP_K, tp), lambda i, off: (0, i))),
            out_shape=kt(jnp.int32),
            name="positions",
        )((tile_start * tr).astype(jnp.int32), idx_k, rank_k)
        return _sc_scatter_rows(x1p, pos, n_rows), pos, tile_expert, tile_valid

    wg, wu, wd = w_gate[l], w_up[l], w_down[l]
    ff = wg.shape[-1]

    def experts(xs, tile_expert, tile_valid):
        return pl.pallas_call(
            _experts_kernel,
            grid_spec=pltpu.PrefetchScalarGridSpec(
                num_scalar_prefetch=2, grid=(n_tiles,),
                in_specs=[pl.BlockSpec((tr, half), lambda i, te, tv: (i, 0)),
                          pl.BlockSpec((1, D, ff), lambda i, te, tv: (te[i], 0, 0)),
                          pl.BlockSpec((1, D, ff), lambda i, te, tv: (te[i], 0, 0)),
                          pl.BlockSpec((1, ff, D), lambda i, te, tv: (te[i], 0, 0))],
                out_specs=pl.BlockSpec((tr, half), lambda i, te, tv: (i, 0)),
                scratch_shapes=[pltpu.VMEM((D, ff), BF16), pltpu.VMEM((D, ff), BF16),
                                pltpu.VMEM((ff, D), BF16)]),
            out_shape=jax.ShapeDtypeStruct((n_rows, half), jnp.int32),
            compiler_params=cparams(dimension_semantics=("arbitrary",)),
            name="experts",
        )(tile_expert, tile_valid, xs, wg, wu, wd)

    shared = (ws_gate[l].astype(BF16), ws_up[l].astype(BF16), ws_down[l].astype(BF16))

    def combine(s, out_so_far, yg, gate_k, x1):
        specs = [pl.BlockSpec((TOP_K, tm, half), lambda i: (0, i, 0)),
                 pl.BlockSpec((tm, TOP_K), lambda i: (i, 0)), tok(D),
                 _const_spec((D, ff)), _const_spec((D, ff)), _const_spec((ff, D)),
                 _const_spec((1, D)), _const_spec((1, D))]
        args = (yg.reshape(TOP_K, ts, half), gate_k, x1, *shared, _row(ln2_g[l]), _row(ln2_b[l]))
        body, aliases = _combine_kernel, {}
        if out_so_far is not None:
            specs = [pl.BlockSpec(memory_space=pl.ANY)] + specs
            args = (out_so_far,) + args
            body, aliases = _combine_into_kernel, {0: 0}
        return pl.pallas_call(
            body,
            grid=(nt,),
            in_specs=specs,
            out_specs=pl.BlockSpec((tm, D), lambda i: (i + s * nt, 0)),
            out_shape=jax.ShapeDtypeStruct((T, D), F32),
            input_output_aliases=aliases,
            compiler_params=cparams(dimension_semantics=("parallel",)),
            name="combine",
        )(*args)

    routed, moved = [], []
    for s in range(n_slabs):
        x1, x1p, idx_k, gate_k, rank_k, counts = route(s)
        routed.append((x1, gate_k))
        moved.append(dispatch(x1p, idx_k, rank_k, counts))
    gathered = []
    for xs, pos, tile_expert, tile_valid in moved:
        gathered.append(_sc_gather_rows(experts(xs, tile_expert, tile_valid), pos.reshape(TOP_K * ts)))
    out = None
    for s in range(n_slabs):
        out = combine(s, out, gathered[s], routed[s][1], routed[s][0])
    return out.reshape(B, S, D)
```

```python
import functools

import jax
import jax.numpy as jnp
from jax import lax
from jax.experimental import pallas as pl
from jax.experimental.pallas import tpu as pltpu
from jax.experimental.pallas import tpu_sc as plsc

CHUNK = 64
MLA_HEADS = 8
QK_NOPE = 64
QK_ROPE = 32
V_DIM = 64
Q_RANK = 256
KV_RANK = 128
ROPE_THETA = 10000.0
SSM_GROUP = 16
SSM_STATE = 64
N_EXPERTS = 64
TOP_K = 8
N_GROUPS = 8
TOP_GROUPS = 4
ROUTED_SCALE = 2.5
DEPTH = 1
ALPHA = (2.0 * DEPTH) ** 0.25
EPS = 1e-5
LOG2_E = 1.4426950408889634

LANES = 128
HEAD_PAD = LANES
VMEM_LIMIT = 56 * 1024 * 1024
TOKEN_TILE = 512
MIX_TILE = 1024
QUERY_TILE = 256
S5_TIME_TILE = 64
POSITIONS_TILE = 2048
EXPERT_ROW_TILE = 1024
MOE_SLABS = 1
COMBINE_CHUNKS = 4

BF16 = jnp.bfloat16
F32 = jnp.float32
NT_DIMS = (((1,), (1,)), ((), ()))


def _dot(a, b):
    return jnp.dot(a, b, preferred_element_type=F32)


def _layer_norm(x, g, b):
    mu = jnp.mean(x, axis=-1, keepdims=True)
    xc = x - mu
    var = jnp.mean(xc * xc, axis=-1, keepdims=True)
    return xc * lax.rsqrt(var + EPS) * g + b


def _rms_norm(x, g):
    return x * lax.rsqrt(jnp.mean(x * x, axis=-1, keepdims=True) + EPS) * g


def _inproj_kernel(x_ref, pos_ref, lng_ref, lnb_ref, w1_ref, qg_ref, kvg_ref,
                   wq_ref, wqr_ref, wk_ref, wv_ref, freq_ref,
                   xn_ref, q_ref, k_ref, v_ref, u_ref, *, ssm_width):
    xn = _layer_norm(x_ref[...], lng_ref[...], lnb_ref[...])
    xn_ref[...] = xn
    h = _dot(xn.astype(BF16), w1_ref[...])
    o1 = Q_RANK
    o2 = o1 + KV_RANK
    o3 = o2 + ssm_width
    o4 = o3 + HEAD_PAD
    cq = h[:, :o1]
    ckv = h[:, o1:o2]
    u_ref[...] = h[:, o2:o3]
    kr_raw = h[:, o3:o4]
    kr_rot = h[:, o4:o4 + HEAD_PAD]
    cqn = _rms_norm(cq, qg_ref[...]).astype(BF16)
    ckvn = _rms_norm(ckv, kvg_ref[...]).astype(BF16)

    tm = x_ref.shape[0]
    ang_t = freq_ref[...] * pos_ref[...]

    def to_token_rows(t):
        padded = jnp.concatenate([jnp.zeros((QK_NOPE, tm), F32), t,
                                  jnp.zeros((HEAD_PAD - QK_NOPE - QK_ROPE, tm), F32)], axis=0)
        return padded.T

    c = to_token_rows(jnp.cos(ang_t))
    s = to_token_rows(jnp.sin(ang_t))
    lane = lax.broadcasted_iota(jnp.int32, (1, HEAD_PAD), 1)
    scale = (QK_NOPE + QK_ROPE) ** -0.5 * LOG2_E
    cos1 = (c + jnp.where(lane < QK_NOPE, 1.0, 0.0)) * scale
    sin1 = s * scale
    cos_t = jnp.concatenate([cos1] * MLA_HEADS, axis=1)
    sin_t = jnp.concatenate([sin1] * MLA_HEADS, axis=1)
    q = _dot(cqn, wq_ref[...]) * cos_t + _dot(cqn, wqr_ref[...]) * sin_t
    q_ref[...] = q.astype(BF16)

    kr = kr_raw * c + kr_rot * s
    k = _dot(ckvn, wk_ref[...]) + jnp.concatenate([kr] * MLA_HEADS, axis=1)
    k_ref[...] = k.astype(BF16)
    ones_col = jnp.concatenate([jnp.where(lane == V_DIM, 1.0, 0.0)] * MLA_HEADS, axis=1)
    v_ref[...] = (_dot(ckvn, wv_ref[...]) + ones_col).astype(BF16)


def _attn_kernel(q_ref, k_ref, v_ref, o_ref, *, tq, heads, n_qtiles):
    qi = pl.program_id(1)
    row_chunk = lax.broadcasted_iota(jnp.int32, (tq, tq), 0) // CHUNK
    col_chunk = lax.broadcasted_iota(jnp.int32, (tq, tq), 1) // CHUNK
    diag_mask = row_chunk >= col_chunk

    def tile(n_blocks):
        keys = n_blocks * tq
        cols = [slice(h * HEAD_PAD, (h + 1) * HEAD_PAD) for h in range(heads)]

        def mask_diag(s):
            s_diag = jnp.where(diag_mask, s[:, keys - tq:], -jnp.inf)
            return s_diag if n_blocks == 1 else jnp.concatenate([s[:, :keys - tq], s_diag], axis=1)

        ss = [lax.dot_general(q_ref[:, c], k_ref[:keys, c], NT_DIMS, preferred_element_type=F32)
              for c in cols]
        ss = [mask_diag(s) for s in ss]
        ms = [jnp.max(s, axis=-1, keepdims=True) for s in ss]
        ps = [jnp.exp2(s - m).astype(BF16) for s, m in zip(ss, ms)]
        accs = [_dot(p, v_ref[:keys, c]) for p, c in zip(ps, cols)]
        outs = [acc[:, :V_DIM] / acc[:, V_DIM:V_DIM + 1] for acc in accs]
        o_ref[...] = jnp.concatenate(outs, axis=1).astype(o_ref.dtype)

    for c in range(n_qtiles):
        pl.when(qi == c)(functools.partial(tile, c + 1))


def _s5_kernel(u_ref, win_ref, are_ref, aim_ref, cre_ref, cim_ref, dskip_ref,
               wglu_ref, bglu_ref, o_ref, vx_ref, hre_ref, him_ref, io_ref, utm_ref, *,
               batch, lt, n_state):
    ti = pl.program_id(0)

    @pl.when(ti == 0)
    def _():
        hre_ref[...] = jnp.zeros_like(hre_ref)
        him_ref[...] = jnp.zeros_like(him_ref)

    width = u_ref.shape[-1]
    n_slab = width // LANES
    n_tiles = n_state // LANES
    slab_tiles = n_tiles // n_slab
    for b in range(batch):
        for c in range(n_slab):
            io_ref[c, b * lt:(b + 1) * lt, :] = u_ref[b, :, c * LANES:(c + 1) * LANES]

    def slab_tiles_of(j):
        return range(j * slab_tiles, (j + 1) * slab_tiles)

    def expand(j):
        for t in range(lt):
            utm_ref[t * batch:(t + 1) * batch, j * LANES:(j + 1) * LANES] = (
                io_ref[j, pl.ds(t, batch, stride=lt), :])
        ub = utm_ref[:, j * LANES:(j + 1) * LANES].astype(BF16)
        vj = _dot(ub, win_ref[j])
        for i, c in enumerate(slab_tiles_of(j)):
            vx_ref[c] = vj[:, i * LANES:(i + 1) * LANES]
            vx_ref[n_tiles + c] = vj[:, (slab_tiles + i) * LANES:(slab_tiles + i + 1) * LANES]

    def scan(j):
        tiles = slab_tiles_of(j)
        ar = [jnp.broadcast_to(are_ref[:, c * LANES:(c + 1) * LANES], (batch, LANES)) for c in tiles]
        ai = [jnp.broadcast_to(aim_ref[:, c * LANES:(c + 1) * LANES], (batch, LANES)) for c in tiles]
        hr = [hre_ref[c] for c in tiles]
        hi = [him_ref[c] for c in tiles]
        for t in range(lt):
            rows = slice(t * batch, (t + 1) * batch)
            for n, c in enumerate(tiles):
                nr = ar[n] * hr[n] - ai[n] * hi[n] + vx_ref[c, rows, :]
                ni = ar[n] * hi[n] + ai[n] * hr[n] + vx_ref[n_tiles + c, rows, :]
                vx_ref[c, rows, :] = nr
                vx_ref[n_tiles + c, rows, :] = ni
                hr[n], hi[n] = nr, ni
        for n, c in enumerate(tiles):
            hre_ref[c] = hr[n]
            him_ref[c] = hi[n]

    def project(j):
        tiles = slab_tiles_of(j)
        xr = jnp.concatenate([vx_ref[c].astype(BF16) for c in tiles], axis=1)
        xi = jnp.concatenate([vx_ref[n_tiles + c].astype(BF16) for c in tiles], axis=1)
        return _dot(xr, cre_ref[j]) + _dot(xi, cim_ref[j])

    ys = [None] * n_slab
    for j in range(n_slab + 2):
        if j < n_slab:
            expand(j)
        if 1 <= j <= n_slab:
            scan(j - 1)
        if j >= 2:
            ys[j - 2] = project(j - 2)
    y = jnp.concatenate(ys, axis=1) + dskip_ref[...] * utm_ref[...]
    y = jax.nn.gelu(y)
    z = _dot(y.astype(BF16), wglu_ref[...]) + bglu_ref[...]
    out = y * jax.nn.sigmoid(z)
    for c in range(n_slab):
        io_ref[c] = out[:, c * LANES:(c + 1) * LANES]
    for b in range(batch):
        for c in range(n_slab):
            o_ref[b, :, c * LANES:(c + 1) * LANES] = io_ref[c, pl.ds(b, lt, stride=batch), :].astype(o_ref.dtype)


def _router_gates(logits_t, rbias):
    n_exp, tm = logits_t.shape
    per_group = n_exp // N_GROUPS
    scores = jax.nn.sigmoid(logits_t)
    sel = scores + rbias
    neg_inf = -jnp.inf
    sub_iota = lax.broadcasted_iota(jnp.int32, (per_group, tm), 0).astype(F32)
    group_score = []
    for g in range(N_GROUPS):
        sg = sel[g * per_group:(g + 1) * per_group, :]
        m1 = jnp.max(sg, axis=0, keepdims=True)
        first = jnp.min(jnp.where(sg == m1, sub_iota, float(per_group)), axis=0, keepdims=True)
        m2 = jnp.max(jnp.where(sub_iota == first, neg_inf, sg), axis=0, keepdims=True)
        group_score.append(m1 + m2)
    masked = []
    for g in range(N_GROUPS):
        rank = jnp.zeros((1, tm), F32)
        for g2 in range(N_GROUPS):
            if g2 == g:
                continue
            ahead = (group_score[g2] >= group_score[g]) if g2 < g else (group_score[g2] > group_score[g])
            rank = rank + jnp.where(ahead, 1.0, 0.0)
        keep = rank < float(TOP_GROUPS)
        masked.append(jnp.where(keep, sel[g * per_group:(g + 1) * per_group, :], neg_inf))
    cur = jnp.concatenate(masked, axis=0)
    iota = lax.broadcasted_iota(jnp.int32, (n_exp, tm), 0).astype(F32)
    chosen = jnp.zeros((n_exp, tm), F32)
    picks, weights = [], []
    for _ in range(TOP_K):
        m = jnp.max(cur, axis=0, keepdims=True)
        idx = jnp.min(jnp.where(cur == m, iota, float(n_exp)), axis=0, keepdims=True)
        pick = iota == idx
        chosen = jnp.where(pick, 1.0, chosen)
        cur = jnp.where(pick, neg_inf, cur)
        picks.append(idx)
        weights.append(jnp.sum(jnp.where(pick, scores, 0.0), axis=0, keepdims=True))
    idx_k = jnp.concatenate(picks, axis=0)
    w_k = jnp.concatenate(weights, axis=0)
    gate_k = w_k / jnp.sum(w_k, axis=0, keepdims=True) * ROUTED_SCALE
    return idx_k, gate_k, chosen


def _pack_bf16_pairs(x):
    n = x.shape[1] // 2
    hi = lax.bitcast_convert_type(x[:, :n].astype(BF16).astype(F32), jnp.int32)
    lo = lax.bitcast_convert_type(x[:, n:].astype(BF16).astype(F32), jnp.int32)
    return hi | lax.shift_right_logical(lo, 16)


def _unpack_bf16_pairs(p):
    hi = lax.bitcast_convert_type(p & jnp.int32(-65536), F32).astype(BF16)
    lo = lax.bitcast_convert_type(lax.shift_left(p, 16), F32).astype(BF16)
    return hi, lo


def _mix_kernel(att_ref, ssm_ref, xn_ref, ag_ref, sg_ref, woa_ref, wos_ref,
                g_ref, b_ref, wrh_ref, wrl_ref, rb_ref,
                x1_ref, x1p_ref, idx_ref, gate_ref, rank_ref, cnt_ref, carry_ref, before_ref):
    @pl.when(pl.program_id(0) == 0)
    def _():
        carry_ref[...] = jnp.zeros_like(carry_ref)
        tm = before_ref.shape[0]
        before = (lax.broadcasted_iota(jnp.int32, (tm, tm), 0)
                  < lax.broadcasted_iota(jnp.int32, (tm, tm), 1))
        before_ref[...] = jnp.where(before, 1.0, 0.0).astype(BF16)

    an = _rms_norm(att_ref[...].astype(F32), ag_ref[...]).astype(BF16)
    sn = _rms_norm(ssm_ref[...].astype(F32), sg_ref[...]).astype(BF16)
    mix = _dot(an, woa_ref[...]) + _dot(sn, wos_ref[...])
    x1 = _layer_norm(ALPHA * xn_ref[...] + mix, g_ref[...], b_ref[...])
    x1_ref[...] = x1
    x1p_ref[...] = _pack_bf16_pairs(x1)
    x_hi = x1.astype(BF16)
    x_lo = (x1 - x_hi.astype(F32)).astype(BF16)
    dg = functools.partial(lax.dot_general, dimension_numbers=NT_DIMS, preferred_element_type=F32)
    logits_t = dg(wrh_ref[...], x_hi) + dg(wrl_ref[...], x_hi) + dg(wrh_ref[...], x_lo)
    idx_k, gate_k, chosen = _router_gates(logits_t, rb_ref[...])
    idx_ref[...] = idx_k.astype(jnp.int32)
    gate_ref[...] = gate_k.T

    n_exp, tm = chosen.shape
    excl = _dot(chosen.astype(BF16), before_ref[...])
    rank_full = carry_ref[...] + excl
    iota = lax.broadcasted_iota(jnp.int32, (n_exp, tm), 0).astype(F32)
    ranks = [jnp.sum(jnp.where(iota == idx_k[k:k + 1, :], rank_full, 0.0), axis=0, keepdims=True)
             for k in range(TOP_K)]
    rank_ref[...] = jnp.concatenate(ranks, axis=0).astype(jnp.int32)
    total = carry_ref[...] + jnp.sum(chosen, axis=1, keepdims=True)
    carry_ref[...] = total
    cnt_ref[...] = jnp.broadcast_to(total, cnt_ref.shape)


def _positions_kernel(off_ref, idx_ref, rank_ref, pos_ref):
    idx = idx_ref[...]
    base = jnp.zeros(idx.shape, jnp.int32)
    for e in range(N_EXPERTS):
        base = jnp.where(idx == e, off_ref[e], base)
    pos_ref[...] = rank_ref[...] + base


def _swiglu(x_hi, x_lo, wg_ref, wu_ref, wd_ref):
    half = x_hi.shape[1]
    hg = _dot(x_hi, wg_ref[:half, :]) + _dot(x_lo, wg_ref[half:, :])
    hu = _dot(x_hi, wu_ref[:half, :]) + _dot(x_lo, wu_ref[half:, :])
    h = jax.nn.silu(hg) * hu
    return _dot(h.astype(BF16), wd_ref[...])


def _experts_kernel(te_ref, valid_ref, xs_ref, wg_ref, wu_ref, wd_ref, ys_ref,
                    wgb_ref, wub_ref, wdb_ref):
    i = pl.program_id(0)
    valid = valid_ref[i]

    @pl.when((i == 0) | (te_ref[i] != te_ref[jnp.maximum(i - 1, 0)]))
    def _():
        wgb_ref[...] = wg_ref[0].astype(BF16)
        wub_ref[...] = wu_ref[0].astype(BF16)
        wdb_ref[...] = wd_ref[0].astype(BF16)

    @pl.when(valid > 0)
    def _():
        rows = lax.broadcasted_iota(jnp.int32, xs_ref.shape, 0)
        x_hi, x_lo = _unpack_bf16_pairs(jnp.where(rows < valid, xs_ref[...], 0))
        ys_ref[...] = _pack_bf16_pairs(_swiglu(x_hi, x_lo, wgb_ref, wub_ref, wdb_ref))

    @pl.when(valid == 0)
    def _():
        ys_ref[...] = jnp.zeros_like(ys_ref)


def _combine_kernel(yg_ref, gate_ref, x1_ref, wsg_ref, wsu_ref, wsd_ref, g_ref, b_ref, o_ref):
    x1 = x1_ref[...]
    half = x1.shape[1] // 2
    acc = _swiglu(x1[:, :half].astype(BF16), x1[:, half:].astype(BF16), wsg_ref, wsu_ref, wsd_ref)
    gates = gate_ref[...]
    acc_hi, acc_lo = acc[:, :half], acc[:, half:]
    for k in range(TOP_K):
        y_hi, y_lo = _unpack_bf16_pairs(yg_ref[k])
        gk = gates[:, k:k + 1]
        acc_hi = acc_hi + gk * y_hi.astype(F32)
        acc_lo = acc_lo + gk * y_lo.astype(F32)
    ffn = jnp.concatenate([acc_hi, acc_lo], axis=1)
    o_ref[...] = _layer_norm(ALPHA * x1 + ffn, g_ref[...], b_ref[...])


def _combine_into_kernel(prev_ref, *refs):
    del prev_ref
    _combine_kernel(*refs)


SC_CORES = 2
SC_SUBCORES = 16
SC_WORKERS = SC_CORES * SC_SUBCORES
SC_CHUNK = 64


def _sc_mesh():
    return plsc.VectorSubcoreMesh(core_axis_name="c", subcore_axis_name="s")


def _sc_gather_rows(table, idx):
    n = idx.shape[0]
    d = table.shape[1]
    assert n % (SC_WORKERS * 2 * SC_CHUNK) == 0, n
    per_w = n // SC_WORKERS
    n_ch = per_w // SC_CHUNK

    @functools.partial(
        pl.kernel, mesh=_sc_mesh(),
        out_type=jax.ShapeDtypeStruct((n, d), table.dtype),
        scratch_types=[pltpu.VMEM((n_ch, SC_CHUNK), jnp.int32),
                       pltpu.VMEM((2, SC_CHUNK, d), table.dtype),
                       pltpu.SemaphoreType.DMA((2,)),
                       pltpu.SemaphoreType.DMA((2,))],
    )
    def k(table_hbm, idx_hbm, out_hbm, idx_v, buf, gsem, osem):
        wid = lax.axis_index("s") * SC_CORES + lax.axis_index("c")
        base = wid * per_w
        pltpu.sync_copy(idx_hbm.at[wid], idx_v)

        def gather(c, b):
            return pltpu.make_async_copy(table_hbm.at[idx_v.at[c]], buf.at[b], gsem.at[b])

        def put(c, b):
            return pltpu.make_async_copy(buf.at[b], out_hbm.at[pl.ds(base + c * SC_CHUNK, SC_CHUNK)],
                                         osem.at[b])

        gather(0, 0).start()

        @pl.loop(0, n_ch, step=2)
        def _(c):
            for b in range(2):
                cc = c + b
                gather(cc, b).wait()

                @pl.when(cc + 1 < n_ch)
                def _():
                    @pl.when(cc >= 1)
                    def _():
                        put(cc - 1, 1 - b).wait()
                    gather(cc + 1, 1 - b).start()

                put(cc, b).start()

        put(n_ch - 2, 0).wait()
        put(n_ch - 1, 1).wait()

    return k(table, idx.reshape(SC_WORKERS, n_ch, SC_CHUNK))


def _sc_scatter_rows(x, pos, n_out):
    t, d = x.shape
    kk = pos.shape[0]
    assert t % (SC_WORKERS * 2 * SC_CHUNK) == 0, t
    per_w = t // SC_WORKERS
    n_ch = per_w // SC_CHUNK
    pos_w = pos.reshape(kk, SC_WORKERS, n_ch, SC_CHUNK).transpose(1, 2, 0, 3)
    pos_w = pos_w.reshape(SC_WORKERS, n_ch * kk, SC_CHUNK)

    @functools.partial(
        pl.kernel, mesh=_sc_mesh(),
        out_type=jax.ShapeDtypeStruct((n_out, d), x.dtype),
        scratch_types=[pltpu.VMEM((n_ch * kk, SC_CHUNK), jnp.int32),
                       pltpu.VMEM((2, SC_CHUNK, d), x.dtype),
                       pltpu.SemaphoreType.DMA((2,)),
                       pltpu.SemaphoreType.DMA((2,))],
    )
    def k(x_hbm, pos_hbm, out_hbm, idx_v, buf, isem, osem):
        wid = lax.axis_index("s") * SC_CORES + lax.axis_index("c")
        base = wid * per_w
        pltpu.sync_copy(pos_hbm.at[wid], idx_v)

        def get(c, b):
            return pltpu.make_async_copy(x_hbm.at[pl.ds(base + c * SC_CHUNK, SC_CHUNK)], buf.at[b],
                                         isem.at[b])

        def put(c, j, b):
            return pltpu.make_async_copy(buf.at[b], out_hbm.at[idx_v.at[c * kk + j]], osem.at[b])

        get(0, 0).start()

        @pl.loop(0, n_ch, step=2)
        def _(c):
            for b in range(2):
                cc = c + b
                get(cc, b).wait()

                @pl.when(cc + 1 < n_ch)
                def _():
                    @pl.when(cc >= 1)
                    def _():
                        for j in range(kk):
                            put(cc - 1, j, 1 - b).wait()
                    get(cc + 1, 1 - b).start()

                for j in range(kk):
                    put(cc, j, b).start()

        for j in range(kk):
            put(n_ch - 2, j, 0).wait()
        for j in range(kk):
            put(n_ch - 1, j, 1).wait()

    return k(x, pos_w)


def _row(v):
    return v.reshape(1, -1).astype(F32)


def _const_spec(shape):
    nd = len(shape)
    return pl.BlockSpec(shape, lambda *_: (0,) * nd)


def _pad_heads(w, width):
    r, h, _ = w.shape
    return jnp.pad(w, ((0, 0), (0, 0), (0, HEAD_PAD - width))).reshape(r, h * HEAD_PAD)


def _half_rotate(w):
    half = QK_ROPE // 2
    return jnp.concatenate([-w[..., half:], w[..., :half]], axis=-1)


def kernel(x, positions, ln_in_g, ln_in_b, w_in, q_norm_g, kv_norm_g, w_uq, w_ukv, lambda_re, lambda_im, log_step, b_re, b_im, c_re, c_im, d_skip, w_glu, b_glu, attn_out_g, ssm_out_g, w_o, ln1_g, ln1_b, w_router, router_bias, w_gate, w_up, w_down, ws_gate, ws_up, ws_down, ln2_g, ln2_b):
    B, S, D = x.shape
    T = B * S
    assert DEPTH == 1 and w_in.shape[0] == DEPTH
    assert S % QUERY_TILE == 0 and S % TOKEN_TILE == 0 and S % S5_TIME_TILE == 0, S
    assert T % (MOE_SLABS * TOKEN_TILE) == 0 and (T // MOE_SLABS * TOP_K) % EXPERT_ROW_TILE == 0, T
    l = 0
    ssm_width = w_glu.shape[-1]
    n_groups = ssm_width // SSM_GROUP
    n_state = n_groups * SSM_STATE
    mla_width = MLA_HEADS * V_DIM
    qk_pad = MLA_HEADS * HEAD_PAD
    cparams = functools.partial(pltpu.CompilerParams, vmem_limit_bytes=VMEM_LIMIT)

    s1, s2, s3 = Q_RANK, Q_RANK + KV_RANK, Q_RANK + KV_RANK + QK_ROPE
    wi = w_in[l]
    w_kr = wi[:, s2:s3]
    pad_rope = lambda w: jnp.pad(w, ((0, 0), (QK_NOPE, HEAD_PAD - QK_NOPE - QK_ROPE)))
    w1 = jnp.concatenate([wi[:, :s2], wi[:, s3:], pad_rope(w_kr), pad_rope(_half_rotate(w_kr))],
                         axis=1).astype(BF16)
    wq = w_uq[l]
    zeros_nope = jnp.zeros(wq.shape[:2] + (QK_NOPE,), wq.dtype)
    wq_main = _pad_heads(wq, QK_NOPE + QK_ROPE).astype(BF16)
    wq_rot = _pad_heads(jnp.concatenate([zeros_nope, _half_rotate(wq[..., QK_NOPE:])], axis=-1),
                        QK_NOPE + QK_ROPE).astype(BF16)
    wkv = w_ukv[l]
    wk = _pad_heads(wkv[..., :QK_NOPE], QK_NOPE).astype(BF16)
    wv = _pad_heads(wkv[..., QK_NOPE:], V_DIM).astype(BF16)
    half = QK_ROPE // 2
    inv_freq = ROPE_THETA ** (-jnp.arange(half, dtype=F32) / half)
    freq = jnp.concatenate([inv_freq, inv_freq]).reshape(QK_ROPE, 1)
    pos_f = positions.astype(F32).reshape(1, T)

    tm = min(TOKEN_TILE, T)
    w1_cols = w1.shape[1]
    tok = lambda width: pl.BlockSpec((tm, width), lambda i: (i, 0))
    xn, q, k, v, u = pl.pallas_call(
        functools.partial(_inproj_kernel, ssm_width=ssm_width),
        grid=(T // tm,),
        in_specs=[tok(D), pl.BlockSpec((1, tm), lambda i: (0, i)), _const_spec((1, D)), _const_spec((1, D)),
                  _const_spec((D, w1_cols)), _const_spec((1, Q_RANK)), _const_spec((1, KV_RANK)),
                  _const_spec((Q_RANK, qk_pad)), _const_spec((Q_RANK, qk_pad)),
                  _const_spec((KV_RANK, qk_pad)), _const_spec((KV_RANK, qk_pad)),
                  _const_spec((QK_ROPE, 1))],
        out_specs=[tok(D), tok(qk_pad), tok(qk_pad), tok(qk_pad), tok(ssm_width)],
        out_shape=[jax.ShapeDtypeStruct((T, D), F32), jax.ShapeDtypeStruct((T, qk_pad), BF16),
                   jax.ShapeDtypeStruct((T, qk_pad), BF16), jax.ShapeDtypeStruct((T, qk_pad), BF16),
                   jax.ShapeDtypeStruct((T, ssm_width), F32)],
        compiler_params=cparams(dimension_semantics=("parallel",)),
        name="inproj",
    )(x.reshape(T, D), pos_f, _row(ln_in_g), _row(ln_in_b), w1, _row(q_norm_g[l]), _row(kv_norm_g[l]),
      wq_main, wq_rot, wk, wv, freq)

    tq = min(QUERY_TILE, S)
    nq = S // tq
    att = pl.pallas_call(
        functools.partial(_attn_kernel, tq=tq, heads=MLA_HEADS, n_qtiles=nq),
        grid=(B, nq),
        in_specs=[pl.BlockSpec((tq, qk_pad), lambda b, i: (b * nq + i, 0)),
                  pl.BlockSpec((S, qk_pad), lambda b, i: (b, 0)),
                  pl.BlockSpec((S, qk_pad), lambda b, i: (b, 0))],
        out_specs=pl.BlockSpec((tq, mla_width), lambda b, i: (b * nq + i, 0)),
        out_shape=jax.ShapeDtypeStruct((T, mla_width), BF16),
        compiler_params=cparams(dimension_semantics=("parallel", "arbitrary")),
        name="attention",
    )(q, k, v)

    lam = lax.complex(jnp.minimum(lambda_re[l].astype(F32), -1e-4), lambda_im[l].astype(F32))
    step = jnp.exp(log_step[l].astype(F32))[:, None]
    lam_bar = jnp.exp(lam * step)
    b_bar = ((lam_bar - 1.0) / lam)[..., None] * lax.complex(b_re[l].astype(F32), b_im[l].astype(F32))
    n_slab = ssm_width // LANES
    g_per_slab = n_groups // n_slab
    slab = n_state // n_slab
    eye = jnp.eye(g_per_slab, dtype=F32)

    def expand_in(bpart):
        bt = bpart.transpose(0, 2, 1).reshape(n_slab, g_per_slab, SSM_GROUP, 1, SSM_STATE)
        return (bt * eye[None, :, None, :, None]).reshape(n_slab, LANES, slab)

    def expand_out(cpart):
        ct = cpart.transpose(0, 2, 1).reshape(n_slab, g_per_slab, SSM_STATE, 1, SSM_GROUP)
        return (ct * eye[None, :, None, :, None]).reshape(n_slab, slab, LANES)

    win = jnp.concatenate([expand_in(jnp.real(b_bar)), expand_in(jnp.imag(b_bar))],
                          axis=2).astype(BF16)
    cre = expand_out(c_re[l].astype(F32)).astype(BF16)
    cim = expand_out(-c_im[l].astype(F32)).astype(BF16)
    a_re = jnp.real(lam_bar).reshape(1, n_state)
    a_im = jnp.imag(lam_bar).reshape(1, n_state)

    lt = min(S5_TIME_TILE, S)
    ssm = pl.pallas_call(
        functools.partial(_s5_kernel, batch=B, lt=lt, n_state=n_state),
        grid=(S // lt,),
        in_specs=[pl.BlockSpec((B, lt, ssm_width), lambda t: (0, t, 0)),
                  _const_spec(win.shape), _const_spec((1, n_state)), _const_spec((1, n_state)),
                  _const_spec(cre.shape), _const_spec(cim.shape), _const_spec((1, ssm_width)),
                  _const_spec((ssm_width, ssm_width)), _const_spec((1, ssm_width))],
        out_specs=pl.BlockSpec((B, lt, ssm_width), lambda t: (0, t, 0)),
        out_shape=jax.ShapeDtypeStruct((B, S, ssm_width), BF16),
        scratch_shapes=[pltpu.VMEM((2 * n_state // LANES, B * lt, LANES), F32),
                        pltpu.VMEM((n_state // LANES, B, LANES), F32),
                        pltpu.VMEM((n_state // LANES, B, LANES), F32),
                        pltpu.VMEM((ssm_width // LANES, B * lt, LANES), F32),
                        pltpu.VMEM((B * lt, ssm_width), F32)],
        compiler_params=cparams(dimension_semantics=("arbitrary",)),
        name="s5",
    )(u.reshape(B, S, ssm_width), win, a_re, a_im, cre, cim, _row(d_skip[l]),
      w_glu[l].astype(BF16), _row(b_glu[l]))

    wo = w_o[l].astype(BF16)
    assert mla_width == ssm_width
    wr_t = w_router[l].T.astype(F32)
    wr_hi = wr_t.astype(BF16)
    wr_lo = (wr_t - wr_hi.astype(F32)).astype(BF16)
    half = D // 2
    n_slabs = MOE_SLABS
    ts = T // n_slabs
    nt = ts // tm
    ssm2 = ssm.reshape(T, ssm_width)
    rbias = router_bias[l].astype(F32).reshape(N_EXPERTS, 1)
    kt = lambda dt: jax.ShapeDtypeStruct((TOP_K, ts), dt)
    tmx = min(MIX_TILE, ts)
    ntx = ts // tmx
    k_spec = pl.BlockSpec((TOP_K, tmx), lambda i: (0, i))

    def route(s):
        tok_s = lambda width: pl.BlockSpec((tmx, width), lambda i: (i + s * ntx, 0))
        tok = lambda width: pl.BlockSpec((tmx, width), lambda i: (i, 0))
        return pl.pallas_call(
            _mix_kernel,
            grid=(ntx,),
            in_specs=[tok_s(mla_width), tok_s(ssm_width), tok_s(D), _const_spec((1, mla_width)),
                      _const_spec((1, ssm_width)), pl.BlockSpec((mla_width, D), lambda i: (0, 0)),
                      pl.BlockSpec((ssm_width, D), lambda i: (1, 0)), _const_spec((1, D)), _const_spec((1, D)),
                      _const_spec((N_EXPERTS, D)), _const_spec((N_EXPERTS, D)),
                      _const_spec((N_EXPERTS, 1))],
            out_specs=[tok(D), tok(half), k_spec, tok(TOP_K), k_spec, _const_spec((N_EXPERTS, LANES))],
            out_shape=[jax.ShapeDtypeStruct((ts, D), F32), jax.ShapeDtypeStruct((ts, half), jnp.int32),
                       kt(jnp.int32), jax.ShapeDtypeStruct((ts, TOP_K), F32), kt(jnp.int32),
                       jax.ShapeDtypeStruct((N_EXPERTS, LANES), F32)],
            scratch_shapes=[pltpu.VMEM((N_EXPERTS, 1), F32), pltpu.VMEM((tmx, tmx), BF16)],
            compiler_params=cparams(dimension_semantics=("arbitrary",)),
            name="mix_router",
        )(att, ssm2, xn, _row(attn_out_g[l]), _row(ssm_out_g[l]), wo, wo,
          _row(ln1_g[l]), _row(ln1_b[l]), wr_hi, wr_lo, rbias)

    tr = EXPERT_ROW_TILE
    n_tiles = (ts * TOP_K) // tr + N_EXPERTS
    n_rows = n_tiles * tr
    tp = min(POSITIONS_TILE, ts)

    def dispatch(x1p, idx_k, rank_k, counts):
        cnt = counts[:, 0].astype(jnp.int32)
        tiles_e = (cnt + tr - 1) // tr
        tile_end = jnp.cumsum(tiles_e)
        tile_start = tile_end - tiles_e
        tile_ids = jnp.arange(n_tiles, dtype=jnp.int32)
        tile_expert = jnp.sum((tile_end[None, :] <= tile_ids[:, None]).astype(jnp.int32), axis=1)
        tile_expert = jnp.minimum(tile_expert, N_EXPERTS - 1)
        owner = (tile_start[None, :] <= tile_ids[:, None]) & (tile_ids[:, None] < tile_end[None, :])
        left = jnp.sum(jnp.where(owner, cnt[None, :] - (tile_ids[:, None] - tile_start[None, :]) * tr, 0),
                       axis=1)
        tile_valid = jnp.clip(left, 0, tr).astype(jnp.int32)
        pos = pl.pallas_call(
            _positions_kernel,
            grid_spec=pltpu.PrefetchScalarGridSpec(
                num_scalar_prefetch=1, grid=(ts // tp,),
                in_specs=[pl.BlockSpec((TOP_K, tp), lambda i, off: (0, i)),
                          pl.BlockSpec((TOP_K, tp), lambda i, off: (0, i))],
                out_specs=pl.BlockSpec((TOP_K, tp), lambda i, off: (0, i))),
            out_shape=kt(jnp.int32),
            name="positions",
        )((tile_start * tr).astype(jnp.int32), idx_k, rank_k)
        return _sc_scatter_rows(x1p, pos, n_rows), pos, tile_expert, tile_valid

    wg, wu, wd = w_gate[l], w_up[l], w_down[l]
    ff = wg.shape[-1]

    def experts(xs, tile_expert, tile_valid):
        return pl.pallas_call(
            _experts_kernel,
            grid_spec=pltpu.PrefetchScalarGridSpec(
                num_scalar_prefetch=2, grid=(n_tiles,),
                in_specs=[pl.BlockSpec((tr, half), lambda i, te, tv: (i, 0)),
                          pl.BlockSpec((1, D, ff), lambda i, te, tv: (te[i], 0, 0)),
                          pl.BlockSpec((1, D, ff), lambda i, te, tv: (te[i], 0, 0)),
                          pl.BlockSpec((1, ff, D), lambda i, te, tv: (te[i], 0, 0))],
                out_specs=pl.BlockSpec((tr, half), lambda i, te, tv: (i, 0)),
                scratch_shapes=[pltpu.VMEM((D, ff), BF16), pltpu.VMEM((D, ff), BF16),
                                pltpu.VMEM((ff, D), BF16)]),
            out_shape=jax.ShapeDtypeStruct((n_rows, half), jnp.int32),
            compiler_params=cparams(dimension_semantics=("arbitrary",)),
            name="experts",
        )(tile_expert, tile_valid, xs, wg, wu, wd)

    shared = (ws_gate[l].astype(BF16), ws_up[l].astype(BF16), ws_down[l].astype(BF16))

    n_chunks = COMBINE_CHUNKS
    tc = ts // n_chunks
    ntc = nt // n_chunks

    def combine(s, c, out_so_far, yg, gate_k, x1):
        in_slab = lambda width: pl.BlockSpec((tm, width), lambda i: (i + c * ntc, 0))
        specs = [pl.BlockSpec((TOP_K, tm, half), lambda i: (0, i, 0)), in_slab(TOP_K), in_slab(D),
                 _const_spec((D, ff)), _const_spec((D, ff)), _const_spec((ff, D)),
                 _const_spec((1, D)), _const_spec((1, D))]
        args = (yg.reshape(TOP_K, tc, half), gate_k, x1, *shared, _row(ln2_g[l]), _row(ln2_b[l]))
        body, aliases = _combine_kernel, {}
        if out_so_far is not None:
            specs = [pl.BlockSpec(memory_space=pl.ANY)] + specs
            args = (out_so_far,) + args
            body, aliases = _combine_into_kernel, {0: 0}
        return pl.pallas_call(
            body,
            grid=(ntc,),
            in_specs=specs,
            out_specs=pl.BlockSpec((tm, D), lambda i: (i + s * nt + c * ntc, 0)),
            out_shape=jax.ShapeDtypeStruct((T, D), F32),
            input_output_aliases=aliases,
            compiler_params=cparams(dimension_semantics=("parallel",)),
            name="combine",
        )(*args)

    routed, moved = [], []
    for s in range(n_slabs):
        x1, x1p, idx_k, gate_k, rank_k, counts = route(s)
        routed.append((x1, gate_k))
        moved.append(dispatch(x1p, idx_k, rank_k, counts))
    gathered = []
    for xs, pos, tile_expert, tile_valid in moved:
        ys = experts(xs, tile_expert, tile_valid)
        gathered.append([_sc_gather_rows(ys, pos[:, c * tc:(c + 1) * tc].reshape(TOP_K * tc))
                         for c in range(n_chunks)])
    out = None
    for s in range(n_slabs):
        x1, gate_k = routed[s]
        for c in range(n_chunks):
            out = combine(s, c, out, gathered[s][c], gate_k, x1)
    return out.reshape(B, S, D)
```

```python
import functools

import jax
import jax.numpy as jnp
from jax import lax
from jax.experimental import pallas as pl
from jax.experimental.pallas import tpu as pltpu
from jax.experimental.pallas import tpu_sc as plsc

CHUNK = 64
MLA_HEADS = 8
QK_NOPE = 64
QK_ROPE = 32
V_DIM = 64
Q_RANK = 256
KV_RANK = 128
ROPE_THETA = 10000.0
SSM_GROUP = 16
SSM_STATE = 64
N_EXPERTS = 64
TOP_K = 8
N_GROUPS = 8
TOP_GROUPS = 4
ROUTED_SCALE = 2.5
DEPTH = 1
ALPHA = (2.0 * DEPTH) ** 0.25
EPS = 1e-5
LOG2_E = 1.4426950408889634

LANES = 128
HEAD_PAD = LANES
VMEM_LIMIT = 56 * 1024 * 1024
TOKEN_TILE = 512
MIX_TILE = 1024
QUERY_TILE = 256
S5_TIME_TILE = 64
POSITIONS_TILE = 2048
EXPERT_ROW_TILE = 1024
MOE_SLABS = 1

BF16 = jnp.bfloat16
F32 = jnp.float32
NT_DIMS = (((1,), (1,)), ((), ()))


def _dot(a, b):
    return jnp.dot(a, b, preferred_element_type=F32)


def _layer_norm(x, g, b):
    mu = jnp.mean(x, axis=-1, keepdims=True)
    xc = x - mu
    var = jnp.mean(xc * xc, axis=-1, keepdims=True)
    return xc * lax.rsqrt(var + EPS) * g + b


def _rms_norm(x, g):
    return x * lax.rsqrt(jnp.mean(x * x, axis=-1, keepdims=True) + EPS) * g


def _inproj_kernel(x_ref, pos_ref, lng_ref, lnb_ref, w1_ref, qg_ref, kvg_ref,
                   wq_ref, wqr_ref, wk_ref, wv_ref, freq_ref,
                   xn_ref, q_ref, k_ref, v_ref, u_ref, *, ssm_width):
    xn = _layer_norm(x_ref[...], lng_ref[...], lnb_ref[...])
    xn_ref[...] = xn
    h = _dot(xn.astype(BF16), w1_ref[...])
    o1 = Q_RANK
    o2 = o1 + KV_RANK
    o3 = o2 + ssm_width
    o4 = o3 + HEAD_PAD
    cq = h[:, :o1]
    ckv = h[:, o1:o2]
    u_ref[...] = h[:, o2:o3]
    kr_raw = h[:, o3:o4]
    kr_rot = h[:, o4:o4 + HEAD_PAD]
    cqn = _rms_norm(cq, qg_ref[...]).astype(BF16)
    ckvn = _rms_norm(ckv, kvg_ref[...]).astype(BF16)

    tm = x_ref.shape[0]
    ang_t = freq_ref[...] * pos_ref[...]

    def to_token_rows(t):
        padded = jnp.concatenate([jnp.zeros((QK_NOPE, tm), F32), t,
                                  jnp.zeros((HEAD_PAD - QK_NOPE - QK_ROPE, tm), F32)], axis=0)
        return padded.T

    c = to_token_rows(jnp.cos(ang_t))
    s = to_token_rows(jnp.sin(ang_t))
    lane = lax.broadcasted_iota(jnp.int32, (1, HEAD_PAD), 1)
    scale = (QK_NOPE + QK_ROPE) ** -0.5 * LOG2_E
    cos1 = (c + jnp.where(lane < QK_NOPE, 1.0, 0.0)) * scale
    sin1 = s * scale
    cos_t = jnp.concatenate([cos1] * MLA_HEADS, axis=1)
    sin_t = jnp.concatenate([sin1] * MLA_HEADS, axis=1)
    q = _dot(cqn, wq_ref[...]) * cos_t + _dot(cqn, wqr_ref[...]) * sin_t
    q_ref[...] = q.astype(BF16)

    kr = kr_raw * c + kr_rot * s
    k = _dot(ckvn, wk_ref[...]) + jnp.concatenate([kr] * MLA_HEADS, axis=1)
    k_ref[...] = k.astype(BF16)
    ones_col = jnp.concatenate([jnp.where(lane == V_DIM, 1.0, 0.0)] * MLA_HEADS, axis=1)
    v_ref[...] = (_dot(ckvn, wv_ref[...]) + ones_col).astype(BF16)


def _attn_kernel(q_ref, k_ref, v_ref, o_ref, *, tq, heads, n_qtiles):
    qi = pl.program_id(1)
    row_chunk = lax.broadcasted_iota(jnp.int32, (tq, tq), 0) // CHUNK
    col_chunk = lax.broadcasted_iota(jnp.int32, (tq, tq), 1) // CHUNK
    diag_mask = row_chunk >= col_chunk

    def tile(n_blocks):
        keys = n_blocks * tq
        cols = [slice(h * HEAD_PAD, (h + 1) * HEAD_PAD) for h in range(heads)]

        def mask_diag(s):
            s_diag = jnp.where(diag_mask, s[:, keys - tq:], -jnp.inf)
            return s_diag if n_blocks == 1 else jnp.concatenate([s[:, :keys - tq], s_diag], axis=1)

        ss = [lax.dot_general(q_ref[:, c], k_ref[:keys, c], NT_DIMS, preferred_element_type=F32)
              for c in cols]
        ss = [mask_diag(s) for s in ss]
        ms = [jnp.max(s, axis=-1, keepdims=True) for s in ss]
        ps = [jnp.exp2(s - m).astype(BF16) for s, m in zip(ss, ms)]
        accs = [_dot(p, v_ref[:keys, c]) for p, c in zip(ps, cols)]
        outs = [acc[:, :V_DIM] / acc[:, V_DIM:V_DIM + 1] for acc in accs]
        o_ref[...] = jnp.concatenate(outs, axis=1).astype(o_ref.dtype)

    for c in range(n_qtiles):
        pl.when(qi == c)(functools.partial(tile, c + 1))


def _s5_kernel(u_ref, win_ref, are_ref, aim_ref, cre_ref, cim_ref, dskip_ref,
               wglu_ref, bglu_ref, o_ref, vx_ref, hre_ref, him_ref, io_ref, utm_ref, *,
               batch, lt, n_state):
    ti = pl.program_id(0)

    @pl.when(ti == 0)
    def _():
        hre_ref[...] = jnp.zeros_like(hre_ref)
        him_ref[...] = jnp.zeros_like(him_ref)

    width = u_ref.shape[-1]
    n_slab = width // LANES
    n_tiles = n_state // LANES
    slab_tiles = n_tiles // n_slab
    for b in range(batch):
        for c in range(n_slab):
            io_ref[c, b * lt:(b + 1) * lt, :] = u_ref[b, :, c * LANES:(c + 1) * LANES]

    def slab_tiles_of(j):
        return range(j * slab_tiles, (j + 1) * slab_tiles)

    def expand(j):
        for t in range(lt):
            utm_ref[t * batch:(t + 1) * batch, j * LANES:(j + 1) * LANES] = (
                io_ref[j, pl.ds(t, batch, stride=lt), :])
        ub = utm_ref[:, j * LANES:(j + 1) * LANES].astype(BF16)
        vj = _dot(ub, win_ref[j])
        for i, c in enumerate(slab_tiles_of(j)):
            vx_ref[c] = vj[:, i * LANES:(i + 1) * LANES]
            vx_ref[n_tiles + c] = vj[:, (slab_tiles + i) * LANES:(slab_tiles + i + 1) * LANES]

    def scan(j):
        tiles = slab_tiles_of(j)
        ar = [jnp.broadcast_to(are_ref[:, c * LANES:(c + 1) * LANES], (batch, LANES)) for c in tiles]
        ai = [jnp.broadcast_to(aim_ref[:, c * LANES:(c + 1) * LANES], (batch, LANES)) for c in tiles]
        hr = [hre_ref[c] for c in tiles]
        hi = [him_ref[c] for c in tiles]
        for t in range(lt):
            rows = slice(t * batch, (t + 1) * batch)
            for n, c in enumerate(tiles):
                nr = ar[n] * hr[n] - ai[n] * hi[n] + vx_ref[c, rows, :]
                ni = ar[n] * hi[n] + ai[n] * hr[n] + vx_ref[n_tiles + c, rows, :]
                vx_ref[c, rows, :] = nr
                vx_ref[n_tiles + c, rows, :] = ni
                hr[n], hi[n] = nr, ni
        for n, c in enumerate(tiles):
            hre_ref[c] = hr[n]
            him_ref[c] = hi[n]

    def project(j):
        tiles = slab_tiles_of(j)
        xr = jnp.concatenate([vx_ref[c].astype(BF16) for c in tiles], axis=1)
        xi = jnp.concatenate([vx_ref[n_tiles + c].astype(BF16) for c in tiles], axis=1)
        return _dot(xr, cre_ref[j]) + _dot(xi, cim_ref[j])

    ys = [None] * n_slab
    for j in range(n_slab + 2):
        if j < n_slab:
            expand(j)
        if 1 <= j <= n_slab:
            scan(j - 1)
        if j >= 2:
            ys[j - 2] = project(j - 2)
    y = jnp.concatenate(ys, axis=1) + dskip_ref[...] * utm_ref[...]
    y = jax.nn.gelu(y)
    z = _dot(y.astype(BF16), wglu_ref[...]) + bglu_ref[...]
    out = y * jax.nn.sigmoid(z)
    for c in range(n_slab):
        io_ref[c] = out[:, c * LANES:(c + 1) * LANES]
    for b in range(batch):
        for c in range(n_slab):
            o_ref[b, :, c * LANES:(c + 1) * LANES] = io_ref[c, pl.ds(b, lt, stride=batch), :].astype(o_ref.dtype)


def _router_gates(logits_t, rbias):
    n_exp, tm = logits_t.shape
    per_group = n_exp // N_GROUPS
    scores = jax.nn.sigmoid(logits_t)
    sel = scores + rbias
    neg_inf = -jnp.inf
    sub_iota = lax.broadcasted_iota(jnp.int32, (per_group, tm), 0).astype(F32)
    group_score = []
    for g in range(N_GROUPS):
        sg = sel[g * per_group:(g + 1) * per_group, :]
        m1 = jnp.max(sg, axis=0, keepdims=True)
        first = jnp.min(jnp.where(sg == m1, sub_iota, float(per_group)), axis=0, keepdims=True)
        m2 = jnp.max(jnp.where(sub_iota == first, neg_inf, sg), axis=0, keepdims=True)
        group_score.append(m1 + m2)
    masked = []
    for g in range(N_GROUPS):
        rank = jnp.zeros((1, tm), F32)
        for g2 in range(N_GROUPS):
            if g2 == g:
                continue
            ahead = (group_score[g2] >= group_score[g]) if g2 < g else (group_score[g2] > group_score[g])
            rank = rank + jnp.where(ahead, 1.0, 0.0)
        keep = rank < float(TOP_GROUPS)
        masked.append(jnp.where(keep, sel[g * per_group:(g + 1) * per_group, :], neg_inf))
    cur = jnp.concatenate(masked, axis=0)
    iota = lax.broadcasted_iota(jnp.int32, (n_exp, tm), 0).astype(F32)
    chosen = jnp.zeros((n_exp, tm), F32)
    picks, weights = [], []
    for _ in range(TOP_K):
        m = jnp.max(cur, axis=0, keepdims=True)
        idx = jnp.min(jnp.where(cur == m, iota, float(n_exp)), axis=0, keepdims=True)
        pick = iota == idx
        chosen = jnp.where(pick, 1.0, chosen)
        cur = jnp.where(pick, neg_inf, cur)
        picks.append(idx)
        weights.append(jnp.sum(jnp.where(pick, scores, 0.0), axis=0, keepdims=True))
    idx_k = jnp.concatenate(picks, axis=0)
    w_k = jnp.concatenate(weights, axis=0)
    gate_k = w_k / jnp.sum(w_k, axis=0, keepdims=True) * ROUTED_SCALE
    return idx_k, gate_k, chosen


def _pack_bf16_pairs(x):
    n = x.shape[1] // 2
    hi = lax.bitcast_convert_type(x[:, :n].astype(BF16).astype(F32), jnp.int32)
    lo = lax.bitcast_convert_type(x[:, n:].astype(BF16).astype(F32), jnp.int32)
    return hi | lax.shift_right_logical(lo, 16)


def _unpack_bf16_pairs(p):
    hi = lax.bitcast_convert_type(p & jnp.int32(-65536), F32).astype(BF16)
    lo = lax.bitcast_convert_type(lax.shift_left(p, 16), F32).astype(BF16)
    return hi, lo


def _mix_kernel(att_ref, ssm_ref, xn_ref, ag_ref, sg_ref, woa_ref, wos_ref,
                g_ref, b_ref, wrh_ref, wrl_ref, rb_ref,
                x1_ref, x1p_ref, idx_ref, gate_ref, rank_ref, cnt_ref, carry_ref, before_ref):
    @pl.when(pl.program_id(0) == 0)
    def _():
        carry_ref[...] = jnp.zeros_like(carry_ref)
        tm = before_ref.shape[0]
        before = (lax.broadcasted_iota(jnp.int32, (tm, tm), 0)
                  < lax.broadcasted_iota(jnp.int32, (tm, tm), 1))
        before_ref[...] = jnp.where(before, 1.0, 0.0).astype(BF16)

    an = _rms_norm(att_ref[...].astype(F32), ag_ref[...]).astype(BF16)
    sn = _rms_norm(ssm_ref[...].astype(F32), sg_ref[...]).astype(BF16)
    mix = _dot(an, woa_ref[...]) + _dot(sn, wos_ref[...])
    x1 = _layer_norm(ALPHA * xn_ref[...] + mix, g_ref[...], b_ref[...])
    x1_ref[...] = x1
    x1p_ref[...] = _pack_bf16_pairs(x1)
    x_hi = x1.astype(BF16)
    x_lo = (x1 - x_hi.astype(F32)).astype(BF16)
    dg = functools.partial(lax.dot_general, dimension_numbers=NT_DIMS, preferred_element_type=F32)
    logits_t = dg(wrh_ref[...], x_hi) + dg(wrl_ref[...], x_hi) + dg(wrh_ref[...], x_lo)
    idx_k, gate_k, chosen = _router_gates(logits_t, rb_ref[...])
    idx_ref[...] = idx_k.astype(jnp.int32)
    gate_ref[...] = gate_k.T

    n_exp, tm = chosen.shape
    excl = _dot(chosen.astype(BF16), before_ref[...])
    rank_full = carry_ref[...] + excl
    iota = lax.broadcasted_iota(jnp.int32, (n_exp, tm), 0).astype(F32)
    ranks = [jnp.sum(jnp.where(iota == idx_k[k:k + 1, :], rank_full, 0.0), axis=0, keepdims=True)
             for k in range(TOP_K)]
    rank_ref[...] = jnp.concatenate(ranks, axis=0).astype(jnp.int32)
    total = carry_ref[...] + jnp.sum(chosen, axis=1, keepdims=True)
    carry_ref[...] = total
    cnt_ref[...] = jnp.broadcast_to(total, cnt_ref.shape)


def _positions_kernel(off_ref, idx_ref, rank_ref, pos_ref):
    idx = idx_ref[...]
    base = jnp.zeros(idx.shape, jnp.int32)
    for e in range(N_EXPERTS):
        base = jnp.where(idx == e, off_ref[e], base)
    pos_ref[...] = rank_ref[...] + base


def _swiglu(x_hi, x_lo, wg_ref, wu_ref, wd_ref):
    half = x_hi.shape[1]
    hg = _dot(x_hi, wg_ref[:half, :]) + _dot(x_lo, wg_ref[half:, :])
    hu = _dot(x_hi, wu_ref[:half, :]) + _dot(x_lo, wu_ref[half:, :])
    h = jax.nn.silu(hg) * hu
    return _dot(h.astype(BF16), wd_ref[...])


def _experts_kernel(te_ref, valid_ref, xs_ref, wg_ref, wu_ref, wd_ref, ys_ref,
                    wgb_ref, wub_ref, wdb_ref):
    i = pl.program_id(0)
    valid = valid_ref[i]

    @pl.when((i == 0) | (te_ref[i] != te_ref[jnp.maximum(i - 1, 0)]))
    def _():
        wgb_ref[...] = wg_ref[0].astype(BF16)
        wub_ref[...] = wu_ref[0].astype(BF16)
        wdb_ref[...] = wd_ref[0].astype(BF16)

    def ffn(n_rows):
        rows = lax.broadcasted_iota(jnp.int32, (n_rows, xs_ref.shape[1]), 0)
        x_hi, x_lo = _unpack_bf16_pairs(jnp.where(rows < valid, xs_ref[:n_rows, :], 0))
        ys_ref[:n_rows, :] = _pack_bf16_pairs(_swiglu(x_hi, x_lo, wgb_ref, wub_ref, wdb_ref))
        if n_rows < ys_ref.shape[0]:
            ys_ref[n_rows:, :] = jnp.zeros((ys_ref.shape[0] - n_rows, ys_ref.shape[1]), ys_ref.dtype)

    half_rows = xs_ref.shape[0] // 2
    pl.when(valid > half_rows)(functools.partial(ffn, xs_ref.shape[0]))
    pl.when((valid > 0) & (valid <= half_rows))(functools.partial(ffn, half_rows))

    @pl.when(valid == 0)
    def _():
        ys_ref[...] = jnp.zeros_like(ys_ref)


def _combine_kernel(yg_ref, gate_ref, x1_ref, wsg_ref, wsu_ref, wsd_ref, g_ref, b_ref, o_ref):
    x1 = x1_ref[...]
    half = x1.shape[1] // 2
    acc = _swiglu(x1[:, :half].astype(BF16), x1[:, half:].astype(BF16), wsg_ref, wsu_ref, wsd_ref)
    gates = gate_ref[...]
    acc_hi, acc_lo = acc[:, :half], acc[:, half:]
    for k in range(TOP_K):
        y_hi, y_lo = _unpack_bf16_pairs(yg_ref[k])
        gk = gates[:, k:k + 1]
        acc_hi = acc_hi + gk * y_hi.astype(F32)
        acc_lo = acc_lo + gk * y_lo.astype(F32)
    ffn = jnp.concatenate([acc_hi, acc_lo], axis=1)
    o_ref[...] = _layer_norm(ALPHA * x1 + ffn, g_ref[...], b_ref[...])


def _combine_into_kernel(prev_ref, *refs):
    del prev_ref
    _combine_kernel(*refs)


SC_CORES = 2
SC_SUBCORES = 16
SC_WORKERS = SC_CORES * SC_SUBCORES
SC_CHUNK = 64


def _sc_mesh():
    return plsc.VectorSubcoreMesh(core_axis_name="c", subcore_axis_name="s")


def _sc_gather_rows(table, idx):
    n = idx.shape[0]
    d = table.shape[1]
    assert n % (SC_WORKERS * 2 * SC_CHUNK) == 0, n
    per_w = n // SC_WORKERS
    n_ch = per_w // SC_CHUNK

    @functools.partial(
        pl.kernel, mesh=_sc_mesh(),
        out_type=jax.ShapeDtypeStruct((n, d), table.dtype),
        scratch_types=[pltpu.VMEM((n_ch, SC_CHUNK), jnp.int32),
                       pltpu.VMEM((2, SC_CHUNK, d), table.dtype),
                       pltpu.SemaphoreType.DMA((2,)),
                       pltpu.SemaphoreType.DMA((2,))],
    )
    def k(table_hbm, idx_hbm, out_hbm, idx_v, buf, gsem, osem):
        wid = lax.axis_index("s") * SC_CORES + lax.axis_index("c")
        base = wid * per_w
        pltpu.sync_copy(idx_hbm.at[wid], idx_v)

        def gather(c, b):
            return pltpu.make_async_copy(table_hbm.at[idx_v.at[c]], buf.at[b], gsem.at[b])

        def put(c, b):
            return pltpu.make_async_copy(buf.at[b], out_hbm.at[pl.ds(base + c * SC_CHUNK, SC_CHUNK)],
                                         osem.at[b])

        gather(0, 0).start()

        @pl.loop(0, n_ch, step=2)
        def _(c):
            for b in range(2):
                cc = c + b
                gather(cc, b).wait()

                @pl.when(cc + 1 < n_ch)
                def _():
                    @pl.when(cc >= 1)
                    def _():
                        put(cc - 1, 1 - b).wait()
                    gather(cc + 1, 1 - b).start()

                put(cc, b).start()

        put(n_ch - 2, 0).wait()
        put(n_ch - 1, 1).wait()

    return k(table, idx.reshape(SC_WORKERS, n_ch, SC_CHUNK))


def _sc_scatter_rows(x, pos, n_out):
    t, d = x.shape
    kk = pos.shape[0]
    assert t % (SC_WORKERS * 2 * SC_CHUNK) == 0, t
    per_w = t // SC_WORKERS
    n_ch = per_w // SC_CHUNK
    pos_w = pos.reshape(kk, SC_WORKERS, n_ch, SC_CHUNK).transpose(1, 2, 0, 3)
    pos_w = pos_w.reshape(SC_WORKERS, n_ch * kk, SC_CHUNK)

    @functools.partial(
        pl.kernel, mesh=_sc_mesh(),
        out_type=jax.ShapeDtypeStruct((n_out, d), x.dtype),
        scratch_types=[pltpu.VMEM((n_ch * kk, SC_CHUNK), jnp.int32),
                       pltpu.VMEM((2, SC_CHUNK, d), x.dtype),
                       pltpu.SemaphoreType.DMA((2,)),
                       pltpu.SemaphoreType.DMA((2,))],
    )
    def k(x_hbm, pos_hbm, out_hbm, idx_v, buf, isem, osem):
        wid = lax.axis_index("s") * SC_CORES + lax.axis_index("c")
        base = wid * per_w
        pltpu.sync_copy(pos_hbm.at[wid], idx_v)

        def get(c, b):
            return pltpu.make_async_copy(x_hbm.at[pl.ds(base + c * SC_CHUNK, SC_CHUNK)], buf.at[b],
                                         isem.at[b])

        def put(c, j, b):
            return pltpu.make_async_copy(buf.at[b], out_hbm.at[idx_v.at[c * kk + j]], osem.at[b])

        get(0, 0).start()

        @pl.loop(0, n_ch, step=2)
        def _(c):
            for b in range(2):
                cc = c + b
                get(cc, b).wait()

                @pl.when(cc + 1 < n_ch)
                def _():
                    @pl.when(cc >= 1)
                    def _():
                        for j in range(kk):
                            put(cc - 1, j, 1 - b).wait()
                    get(cc + 1, 1 - b).start()

                for j in range(kk):
                    put(cc, j, b).start()

        for j in range(kk):
            put(n_ch - 2, j, 0).wait()
        for j in range(kk):
            put(n_ch - 1, j, 1).wait()

    return k(x, pos_w)


def _row(v):
    return v.reshape(1, -1).astype(F32)


def _const_spec(shape):
    nd = len(shape)
    return pl.BlockSpec(shape, lambda *_: (0,) * nd)


def _pad_heads(w, width):
    r, h, _ = w.shape
    return jnp.pad(w, ((0, 0), (0, 0), (0, HEAD_PAD - width))).reshape(r, h * HEAD_PAD)


def _half_rotate(w):
    half = QK_ROPE // 2
    return jnp.concatenate([-w[..., half:], w[..., :half]], axis=-1)


def kernel(x, positions, ln_in_g, ln_in_b, w_in, q_norm_g, kv_norm_g, w_uq, w_ukv, lambda_re, lambda_im, log_step, b_re, b_im, c_re, c_im, d_skip, w_glu, b_glu, attn_out_g, ssm_out_g, w_o, ln1_g, ln1_b, w_router, router_bias, w_gate, w_up, w_down, ws_gate, ws_up, ws_down, ln2_g, ln2_b):
    B, S, D = x.shape
    T = B * S
    assert DEPTH == 1 and w_in.shape[0] == DEPTH
    assert S % QUERY_TILE == 0 and S % TOKEN_TILE == 0 and S % S5_TIME_TILE == 0, S
    assert T % (MOE_SLABS * TOKEN_TILE) == 0 and (T // MOE_SLABS * TOP_K) % EXPERT_ROW_TILE == 0, T
    l = 0
    ssm_width = w_glu.shape[-1]
    n_groups = ssm_width // SSM_GROUP
    n_state = n_groups * SSM_STATE
    mla_width = MLA_HEADS * V_DIM
    qk_pad = MLA_HEADS * HEAD_PAD
    cparams = functools.partial(pltpu.CompilerParams, vmem_limit_bytes=VMEM_LIMIT)

    s1, s2, s3 = Q_RANK, Q_RANK + KV_RANK, Q_RANK + KV_RANK + QK_ROPE
    wi = w_in[l]
    w_kr = wi[:, s2:s3]
    pad_rope = lambda w: jnp.pad(w, ((0, 0), (QK_NOPE, HEAD_PAD - QK_NOPE - QK_ROPE)))
    w1 = jnp.concatenate([wi[:, :s2], wi[:, s3:], pad_rope(w_kr), pad_rope(_half_rotate(w_kr))],
                         axis=1).astype(BF16)
    wq = w_uq[l]
    zeros_nope = jnp.zeros(wq.shape[:2] + (QK_NOPE,), wq.dtype)
    wq_main = _pad_heads(wq, QK_NOPE + QK_ROPE).astype(BF16)
    wq_rot = _pad_heads(jnp.concatenate([zeros_nope, _half_rotate(wq[..., QK_NOPE:])], axis=-1),
                        QK_NOPE + QK_ROPE).astype(BF16)
    wkv = w_ukv[l]
    wk = _pad_heads(wkv[..., :QK_NOPE], QK_NOPE).astype(BF16)
    wv = _pad_heads(wkv[..., QK_NOPE:], V_DIM).astype(BF16)
    half = QK_ROPE // 2
    inv_freq = ROPE_THETA ** (-jnp.arange(half, dtype=F32) / half)
    freq = jnp.concatenate([inv_freq, inv_freq]).reshape(QK_ROPE, 1)
    pos_f = positions.astype(F32).reshape(1, T)

    tm = min(TOKEN_TILE, T)
    w1_cols = w1.shape[1]
    tok = lambda width: pl.BlockSpec((tm, width), lambda i: (i, 0))
    xn, q, k, v, u = pl.pallas_call(
        functools.partial(_inproj_kernel, ssm_width=ssm_width),
        grid=(T // tm,),
        in_specs=[tok(D), pl.BlockSpec((1, tm), lambda i: (0, i)), _const_spec((1, D)), _const_spec((1, D)),
                  _const_spec((D, w1_cols)), _const_spec((1, Q_RANK)), _const_spec((1, KV_RANK)),
                  _const_spec((Q_RANK, qk_pad)), _const_spec((Q_RANK, qk_pad)),
                  _const_spec((KV_RANK, qk_pad)), _const_spec((KV_RANK, qk_pad)),
                  _const_spec((QK_ROPE, 1))],
        out_specs=[tok(D), tok(qk_pad), tok(qk_pad), tok(qk_pad), tok(ssm_width)],
        out_shape=[jax.ShapeDtypeStruct((T, D), F32), jax.ShapeDtypeStruct((T, qk_pad), BF16),
                   jax.ShapeDtypeStruct((T, qk_pad), BF16), jax.ShapeDtypeStruct((T, qk_pad), BF16),
                   jax.ShapeDtypeStruct((T, ssm_width), F32)],
        compiler_params=cparams(dimension_semantics=("parallel",)),
        name="inproj",
    )(x.reshape(T, D), pos_f, _row(ln_in_g), _row(ln_in_b), w1, _row(q_norm_g[l]), _row(kv_norm_g[l]),
      wq_main, wq_rot, wk, wv, freq)

    tq = min(QUERY_TILE, S)
    nq = S // tq
    att = pl.pallas_call(
        functools.partial(_attn_kernel, tq=tq, heads=MLA_HEADS, n_qtiles=nq),
        grid=(B, nq),
        in_specs=[pl.BlockSpec((tq, qk_pad), lambda b, i: (b * nq + i, 0)),
                  pl.BlockSpec((S, qk_pad), lambda b, i: (b, 0)),
                  pl.BlockSpec((S, qk_pad), lambda b, i: (b, 0))],
        out_specs=pl.BlockSpec((tq, mla_width), lambda b, i: (b * nq + i, 0)),
        out_shape=jax.ShapeDtypeStruct((T, mla_width), BF16),
        compiler_params=cparams(dimension_semantics=("parallel", "arbitrary")),
        name="attention",
    )(q, k, v)

    lam = lax.complex(jnp.minimum(lambda_re[l].astype(F32), -1e-4), lambda_im[l].astype(F32))
    step = jnp.exp(log_step[l].astype(F32))[:, None]
    lam_bar = jnp.exp(lam * step)
    b_bar = ((lam_bar - 1.0) / lam)[..., None] * lax.complex(b_re[l].astype(F32), b_im[l].astype(F32))
    n_slab = ssm_width // LANES
    g_per_slab = n_groups // n_slab
    slab = n_state // n_slab
    eye = jnp.eye(g_per_slab, dtype=F32)

    def expand_in(bpart):
        bt = bpart.transpose(0, 2, 1).reshape(n_slab, g_per_slab, SSM_GROUP, 1, SSM_STATE)
        return (bt * eye[None, :, None, :, None]).reshape(n_slab, LANES, slab)

    def expand_out(cpart):
        ct = cpart.transpose(0, 2, 1).reshape(n_slab, g_per_slab, SSM_STATE, 1, SSM_GROUP)
        return (ct * eye[None, :, None, :, None]).reshape(n_slab, slab, LANES)

    win = jnp.concatenate([expand_in(jnp.real(b_bar)), expand_in(jnp.imag(b_bar))],
                          axis=2).astype(BF16)
    cre = expand_out(c_re[l].astype(F32)).astype(BF16)
    cim = expand_out(-c_im[l].astype(F32)).astype(BF16)
    a_re = jnp.real(lam_bar).reshape(1, n_state)
    a_im = jnp.imag(lam_bar).reshape(1, n_state)

    lt = min(S5_TIME_TILE, S)
    ssm = pl.pallas_call(
        functools.partial(_s5_kernel, batch=B, lt=lt, n_state=n_state),
        grid=(S // lt,),
        in_specs=[pl.BlockSpec((B, lt, ssm_width), lambda t: (0, t, 0)),
                  _const_spec(win.shape), _const_spec((1, n_state)), _const_spec((1, n_state)),
                  _const_spec(cre.shape), _const_spec(cim.shape), _const_spec((1, ssm_width)),
                  _const_spec((ssm_width, ssm_width)), _const_spec((1, ssm_width))],
        out_specs=pl.BlockSpec((B, lt, ssm_width), lambda t: (0, t, 0)),
        out_shape=jax.ShapeDtypeStruct((B, S, ssm_width), BF16),
        scratch_shapes=[pltpu.VMEM((2 * n_state // LANES, B * lt, LANES), F32),
                        pltpu.VMEM((n_state // LANES, B, LANES), F32),
                        pltpu.VMEM((n_state // LANES, B, LANES), F32),
                        pltpu.VMEM((ssm_width // LANES, B * lt, LANES), F32),
                        pltpu.VMEM((B * lt, ssm_width), F32)],
        compiler_params=cparams(dimension_semantics=("arbitrary",)),
        name="s5",
    )(u.reshape(B, S, ssm_width), win, a_re, a_im, cre, cim, _row(d_skip[l]),
      w_glu[l].astype(BF16), _row(b_glu[l]))

    wo = w_o[l].astype(BF16)
    assert mla_width == ssm_width
    wr_t = w_router[l].T.astype(F32)
    wr_hi = wr_t.astype(BF16)
    wr_lo = (wr_t - wr_hi.astype(F32)).astype(BF16)
    half = D // 2
    n_slabs = MOE_SLABS
    ts = T // n_slabs
    nt = ts // tm
    ssm2 = ssm.reshape(T, ssm_width)
    rbias = router_bias[l].astype(F32).reshape(N_EXPERTS, 1)
    kt = lambda dt: jax.ShapeDtypeStruct((TOP_K, ts), dt)
    tmx = min(MIX_TILE, ts)
    ntx = ts // tmx
    k_spec = pl.BlockSpec((TOP_K, tmx), lambda i: (0, i))

    def route(s):
        tok_s = lambda width: pl.BlockSpec((tmx, width), lambda i: (i + s * ntx, 0))
        tok = lambda width: pl.BlockSpec((tmx, width), lambda i: (i, 0))
        return pl.pallas_call(
            _mix_kernel,
            grid=(ntx,),
            in_specs=[tok_s(mla_width), tok_s(ssm_width), tok_s(D), _const_spec((1, mla_width)),
                      _const_spec((1, ssm_width)), pl.BlockSpec((mla_width, D), lambda i: (0, 0)),
                      pl.BlockSpec((ssm_width, D), lambda i: (1, 0)), _const_spec((1, D)), _const_spec((1, D)),
                      _const_spec((N_EXPERTS, D)), _const_spec((N_EXPERTS, D)),
                      _const_spec((N_EXPERTS, 1))],
            out_specs=[tok(D), tok(half), k_spec, tok(TOP_K), k_spec, _const_spec((N_EXPERTS, LANES))],
            out_shape=[jax.ShapeDtypeStruct((ts, D), F32), jax.ShapeDtypeStruct((ts, half), jnp.int32),
                       kt(jnp.int32), jax.ShapeDtypeStruct((ts, TOP_K), F32), kt(jnp.int32),
                       jax.ShapeDtypeStruct((N_EXPERTS, LANES), F32)],
            scratch_shapes=[pltpu.VMEM((N_EXPERTS, 1), F32), pltpu.VMEM((tmx, tmx), BF16)],
            compiler_params=cparams(dimension_semantics=("arbitrary",)),
            name="mix_router",
        )(att, ssm2, xn, _row(attn_out_g[l]), _row(ssm_out_g[l]), wo, wo,
          _row(ln1_g[l]), _row(ln1_b[l]), wr_hi, wr_lo, rbias)

    tr = EXPERT_ROW_TILE
    n_tiles = (ts * TOP_K) // tr + N_EXPERTS
    n_rows = n_tiles * tr
    tp = min(POSITIONS_TILE, ts)

    def dispatch(x1p, idx_k, rank_k, counts):
        cnt = counts[:, 0].astype(jnp.int32)
        tiles_e = (cnt + tr - 1) // tr
        tile_end = jnp.cumsum(tiles_e)
        tile_start = tile_end - tiles_e
        tile_ids = jnp.arange(n_tiles, dtype=jnp.int32)
        tile_expert = jnp.sum((tile_end[None, :] <= tile_ids[:, None]).astype(jnp.int32), axis=1)
        tile_expert = jnp.minimum(tile_expert, N_EXPERTS - 1)
        owner = (tile_start[None, :] <= tile_ids[:, None]) & (tile_ids[:, None] < tile_end[None, :])
        left = jnp.sum(jnp.where(owner, cnt[None, :] - (tile_ids[:, None] - tile_start[None, :]) * tr, 0),
                       axis=1)
        tile_valid = jnp.clip(left, 0, tr).astype(jnp.int32)
        pos = pl.pallas_call(
            _positions_kernel,
            grid_spec=pltpu.PrefetchScalarGridSpec(
                num_scalar_prefetch=1, grid=(ts // tp,),
                in_specs=[pl.BlockSpec((TOP_K, tp), lambda i, off: (0, i)),
                          pl.BlockSpec((TOP_K, tp), lambda i, off: (0, i))],
                out_specs=pl.BlockSpec((TOP_K, tp), lambda i, off: (0, i))),
            out_shape=kt(jnp.int32),
            name="positions",
        )((tile_start * tr).astype(jnp.int32), idx_k, rank_k)
        return _sc_scatter_rows(x1p, pos, n_rows), pos, tile_expert, tile_valid

    wg, wu, wd = w_gate[l], w_up[l], w_down[l]
    ff = wg.shape[-1]

    def experts(xs, tile_expert, tile_valid):
        return pl.pallas_call(
            _experts_kernel,
            grid_spec=pltpu.PrefetchScalarGridSpec(
                num_scalar_prefetch=2, grid=(n_tiles,),
                in_specs=[pl.BlockSpec((tr, half), lambda i, te, tv: (i, 0)),
                          pl.BlockSpec((1, D, ff), lambda i, te, tv: (te[i], 0, 0)),
                          pl.BlockSpec((1, D, ff), lambda i, te, tv: (te[i], 0, 0)),
                          pl.BlockSpec((1, ff, D), lambda i, te, tv: (te[i], 0, 0))],
                out_specs=pl.BlockSpec((tr, half), lambda i, te, tv: (i, 0)),
                scratch_shapes=[pltpu.VMEM((D, ff), BF16), pltpu.VMEM((D, ff), BF16),
                                pltpu.VMEM((ff, D), BF16)]),
            out_shape=jax.ShapeDtypeStruct((n_rows, half), jnp.int32),
            compiler_params=cparams(dimension_semantics=("arbitrary",)),
            name="experts",
        )(tile_expert, tile_valid, xs, wg, wu, wd)

    shared = (ws_gate[l].astype(BF16), ws_up[l].astype(BF16), ws_down[l].astype(BF16))

    def combine(s, out_so_far, yg, gate_k, x1):
        specs = [pl.BlockSpec((TOP_K, tm, half), lambda i: (0, i, 0)),
                 pl.BlockSpec((tm, TOP_K), lambda i: (i, 0)), tok(D),
                 _const_spec((D, ff)), _const_spec((D, ff)), _const_spec((ff, D)),
                 _const_spec((1, D)), _const_spec((1, D))]
        args = (yg.reshape(TOP_K, ts, half), gate_k, x1, *shared, _row(ln2_g[l]), _row(ln2_b[l]))
        body, aliases = _combine_kernel, {}
        if out_so_far is not None:
            specs = [pl.BlockSpec(memory_space=pl.ANY)] + specs
            args = (out_so_far,) + args
            body, aliases = _combine_into_kernel, {0: 0}
        return pl.pallas_call(
            body,
            grid=(nt,),
            in_specs=specs,
            out_specs=pl.BlockSpec((tm, D), lambda i: (i + s * nt, 0)),
            out_shape=jax.ShapeDtypeStruct((T, D), F32),
            input_output_aliases=aliases,
            compiler_params=cparams(dimension_semantics=("parallel",)),
            name="combine",
        )(*args)

    routed, moved = [], []
    for s in range(n_slabs):
        x1, x1p, idx_k, gate_k, rank_k, counts = route(s)
        routed.append((x1, gate_k))
        moved.append(dispatch(x1p, idx_k, rank_k, counts))
    gathered = []
    for xs, pos, tile_expert, tile_valid in moved:
        gathered.append(_sc_gather_rows(experts(xs, tile_expert, tile_valid), pos.reshape(TOP_K * ts)))
    out = None
    for s in range(n_slabs):
        out = combine(s, out, gathered[s], routed[s][1], routed[s][0])
    return out.reshape(B, S, D)
```

```python
import functools

import jax
import jax.numpy as jnp
from jax import lax
from jax.experimental import pallas as pl
from jax.experimental.pallas import tpu as pltpu
from jax.experimental.pallas import tpu_sc as plsc

CHUNK = 64
MLA_HEADS = 8
QK_NOPE = 64
QK_ROPE = 32
V_DIM = 64
Q_RANK = 256
KV_RANK = 128
ROPE_THETA = 10000.0
SSM_GROUP = 16
SSM_STATE = 64
N_EXPERTS = 64
TOP_K = 8
N_GROUPS = 8
TOP_GROUPS = 4
ROUTED_SCALE = 2.5
DEPTH = 1
ALPHA = (2.0 * DEPTH) ** 0.25
EPS = 1e-5
LOG2_E = 1.4426950408889634

LANES = 128
HEAD_PAD = LANES
VMEM_LIMIT = 56 * 1024 * 1024
TOKEN_TILE = 512
MIX_TILE = 1024
QUERY_TILE = 256
S5_TIME_TILE = 64
POSITIONS_TILE = 2048
EXPERT_ROW_TILE = 1024
MOE_SLABS = 1
COMBINE_CHUNKS = 4

BF16 = jnp.bfloat16
F32 = jnp.float32
NT_DIMS = (((1,), (1,)), ((), ()))


def _dot(a, b):
    return jnp.dot(a, b, preferred_element_type=F32)


def _layer_norm(x, g, b):
    mu = jnp.mean(x, axis=-1, keepdims=True)
    xc = x - mu
    var = jnp.mean(xc * xc, axis=-1, keepdims=True)
    return xc * lax.rsqrt(var + EPS) * g + b


def _rms_norm(x, g):
    return x * lax.rsqrt(jnp.mean(x * x, axis=-1, keepdims=True) + EPS) * g


def _inproj_kernel(x_ref, pos_ref, lng_ref, lnb_ref, w1_ref, qg_ref, kvg_ref,
                   wq_ref, wqr_ref, wk_ref, wv_ref, freq_ref,
                   xn_ref, q_ref, k_ref, v_ref, u_ref, *, ssm_width):
    xn = _layer_norm(x_ref[...], lng_ref[...], lnb_ref[...])
    xn_ref[...] = xn
    h = _dot(xn.astype(BF16), w1_ref[...])
    o1 = Q_RANK
    o2 = o1 + KV_RANK
    o3 = o2 + ssm_width
    o4 = o3 + HEAD_PAD
    cq = h[:, :o1]
    ckv = h[:, o1:o2]
    u_ref[...] = h[:, o2:o3]
    kr_raw = h[:, o3:o4]
    kr_rot = h[:, o4:o4 + HEAD_PAD]
    cqn = _rms_norm(cq, qg_ref[...]).astype(BF16)
    ckvn = _rms_norm(ckv, kvg_ref[...]).astype(BF16)

    tm = x_ref.shape[0]
    ang_t = freq_ref[...] * pos_ref[...]

    def to_token_rows(t):
        padded = jnp.concatenate([jnp.zeros((QK_NOPE, tm), F32), t,
                                  jnp.zeros((HEAD_PAD - QK_NOPE - QK_ROPE, tm), F32)], axis=0)
        return padded.T

    c = to_token_rows(jnp.cos(ang_t))
    s = to_token_rows(jnp.sin(ang_t))
    lane = lax.broadcasted_iota(jnp.int32, (1, HEAD_PAD), 1)
    scale = (QK_NOPE + QK_ROPE) ** -0.5 * LOG2_E
    cos1 = (c + jnp.where(lane < QK_NOPE, 1.0, 0.0)) * scale
    sin1 = s * scale
    cos_t = jnp.concatenate([cos1] * MLA_HEADS, axis=1)
    sin_t = jnp.concatenate([sin1] * MLA_HEADS, axis=1)
    q = _dot(cqn, wq_ref[...]) * cos_t + _dot(cqn, wqr_ref[...]) * sin_t
    q_ref[...] = q.astype(BF16)

    kr = kr_raw * c + kr_rot * s
    k = _dot(ckvn, wk_ref[...]) + jnp.concatenate([kr] * MLA_HEADS, axis=1)
    k_ref[...] = k.astype(BF16)
    ones_col = jnp.concatenate([jnp.where(lane == V_DIM, 1.0, 0.0)] * MLA_HEADS, axis=1)
    v_ref[...] = (_dot(ckvn, wv_ref[...]) + ones_col).astype(BF16)


def _attn_kernel(q_ref, k_ref, v_ref, o_ref, *, tq, heads, n_qtiles):
    qi = pl.program_id(1)
    row_chunk = lax.broadcasted_iota(jnp.int32, (tq, tq), 0) // CHUNK
    col_chunk = lax.broadcasted_iota(jnp.int32, (tq, tq), 1) // CHUNK
    diag_mask = row_chunk >= col_chunk

    def tile(n_blocks):
        keys = n_blocks * tq
        cols = [slice(h * HEAD_PAD, (h + 1) * HEAD_PAD) for h in range(heads)]

        def mask_diag(s):
            s_diag = jnp.where(diag_mask, s[:, keys - tq:], -jnp.inf)
            return s_diag if n_blocks == 1 else jnp.concatenate([s[:, :keys - tq], s_diag], axis=1)

        ss = [lax.dot_general(q_ref[:, c], k_ref[:keys, c], NT_DIMS, preferred_element_type=F32)
              for c in cols]
        ss = [mask_diag(s) for s in ss]
        ms = [jnp.max(s, axis=-1, keepdims=True) for s in ss]
        ps = [jnp.exp2(s - m).astype(BF16) for s, m in zip(ss, ms)]
        accs = [_dot(p, v_ref[:keys, c]) for p, c in zip(ps, cols)]
        outs = [acc[:, :V_DIM] / acc[:, V_DIM:V_DIM + 1] for acc in accs]
        o_ref[...] = jnp.concatenate(outs, axis=1).astype(o_ref.dtype)

    for c in range(n_qtiles):
        pl.when(qi == c)(functools.partial(tile, c + 1))


def _s5_kernel(u_ref, win_ref, are_ref, aim_ref, cre_ref, cim_ref, dskip_ref,
               wglu_ref, bglu_ref, o_ref, vx_ref, hre_ref, him_ref, io_ref, utm_ref, *,
               batch, lt, n_state):
    ti = pl.program_id(0)

    @pl.when(ti == 0)
    def _():
        hre_ref[...] = jnp.zeros_like(hre_ref)
        him_ref[...] = jnp.zeros_like(him_ref)

    width = u_ref.shape[-1]
    n_slab = width // LANES
    n_tiles = n_state // LANES
    slab_tiles = n_tiles // n_slab
    for b in range(batch):
        for c in range(n_slab):
            io_ref[c, b * lt:(b + 1) * lt, :] = u_ref[b, :, c * LANES:(c + 1) * LANES]

    def slab_tiles_of(j):
        return range(j * slab_tiles, (j + 1) * slab_tiles)

    def expand(j):
        for t in range(lt):
            utm_ref[t * batch:(t + 1) * batch, j * LANES:(j + 1) * LANES] = (
                io_ref[j, pl.ds(t, batch, stride=lt), :])
        ub = utm_ref[:, j * LANES:(j + 1) * LANES].astype(BF16)
        vj = _dot(ub, win_ref[j])
        for i, c in enumerate(slab_tiles_of(j)):
            vx_ref[c] = vj[:, i * LANES:(i + 1) * LANES]
            vx_ref[n_tiles + c] = vj[:, (slab_tiles + i) * LANES:(slab_tiles + i + 1) * LANES]

    def scan(j):
        tiles = slab_tiles_of(j)
        ar = [jnp.broadcast_to(are_ref[:, c * LANES:(c + 1) * LANES], (batch, LANES)) for c in tiles]
        ai = [jnp.broadcast_to(aim_ref[:, c * LANES:(c + 1) * LANES], (batch, LANES)) for c in tiles]
        hr = [hre_ref[c] for c in tiles]
        hi = [him_ref[c] for c in tiles]
        for t in range(lt):
            rows = slice(t * batch, (t + 1) * batch)
            for n, c in enumerate(tiles):
                nr = ar[n] * hr[n] - ai[n] * hi[n] + vx_ref[c, rows, :]
                ni = ar[n] * hi[n] + ai[n] * hr[n] + vx_ref[n_tiles + c, rows, :]
                vx_ref[c, rows, :] = nr
                vx_ref[n_tiles + c, rows, :] = ni
                hr[n], hi[n] = nr, ni
        for n, c in enumerate(tiles):
            hre_ref[c] = hr[n]
            him_ref[c] = hi[n]

    def project(j):
        tiles = slab_tiles_of(j)
        xr = jnp.concatenate([vx_ref[c].astype(BF16) for c in tiles], axis=1)
        xi = jnp.concatenate([vx_ref[n_tiles + c].astype(BF16) for c in tiles], axis=1)
        return _dot(xr, cre_ref[j]) + _dot(xi, cim_ref[j])

    ys = [None] * n_slab
    for j in range(n_slab + 2):
        if j < n_slab:
            expand(j)
        if 1 <= j <= n_slab:
            scan(j - 1)
        if j >= 2:
            ys[j - 2] = project(j - 2)
    y = jnp.concatenate(ys, axis=1) + dskip_ref[...] * utm_ref[...]
    y = jax.nn.gelu(y)
    z = _dot(y.astype(BF16), wglu_ref[...]) + bglu_ref[...]
    out = y * jax.nn.sigmoid(z)
    for c in range(n_slab):
        io_ref[c] = out[:, c * LANES:(c + 1) * LANES]
    for b in range(batch):
        for c in range(n_slab):
            o_ref[b, :, c * LANES:(c + 1) * LANES] = io_ref[c, pl.ds(b, lt, stride=batch), :].astype(o_ref.dtype)


def _router_gates(logits_t, rbias):
    n_exp, tm = logits_t.shape
    per_group = n_exp // N_GROUPS
    scores = jax.nn.sigmoid(logits_t)
    sel = scores + rbias
    neg_inf = -jnp.inf
    sub_iota = lax.broadcasted_iota(jnp.int32, (per_group, tm), 0).astype(F32)
    group_score = []
    for g in range(N_GROUPS):
        sg = sel[g * per_group:(g + 1) * per_group, :]
        m1 = jnp.max(sg, axis=0, keepdims=True)
        first = jnp.min(jnp.where(sg == m1, sub_iota, float(per_group)), axis=0, keepdims=True)
        m2 = jnp.max(jnp.where(sub_iota == first, neg_inf, sg), axis=0, keepdims=True)
        group_score.append(m1 + m2)
    masked = []
    for g in range(N_GROUPS):
        rank = jnp.zeros((1, tm), F32)
        for g2 in range(N_GROUPS):
            if g2 == g:
                continue
            ahead = (group_score[g2] >= group_score[g]) if g2 < g else (group_score[g2] > group_score[g])
            rank = rank + jnp.where(ahead, 1.0, 0.0)
        keep = rank < float(TOP_GROUPS)
        masked.append(jnp.where(keep, sel[g * per_group:(g + 1) * per_group, :], neg_inf))
    cur = jnp.concatenate(masked, axis=0)
    iota = lax.broadcasted_iota(jnp.int32, (n_exp, tm), 0).astype(F32)
    chosen = jnp.zeros((n_exp, tm), F32)
    picks, weights = [], []
    for _ in range(TOP_K):
        m = jnp.max(cur, axis=0, keepdims=True)
        idx = jnp.min(jnp.where(cur == m, iota, float(n_exp)), axis=0, keepdims=True)
        pick = iota == idx
        chosen = jnp.where(pick, 1.0, chosen)
        cur = jnp.where(pick, neg_inf, cur)
        picks.append(idx)
        weights.append(jnp.sum(jnp.where(pick, scores, 0.0), axis=0, keepdims=True))
    idx_k = jnp.concatenate(picks, axis=0)
    w_k = jnp.concatenate(weights, axis=0)
    gate_k = w_k / jnp.sum(w_k, axis=0, keepdims=True) * ROUTED_SCALE
    return idx_k, gate_k, chosen


def _pack_bf16_pairs(x):
    n = x.shape[1] // 2
    hi = lax.bitcast_convert_type(x[:, :n].astype(BF16).astype(F32), jnp.int32)
    lo = lax.bitcast_convert_type(x[:, n:].astype(BF16).astype(F32), jnp.int32)
    return hi | lax.shift_right_logical(lo, 16)


def _unpack_bf16_pairs(p):
    hi = lax.bitcast_convert_type(p & jnp.int32(-65536), F32).astype(BF16)
    lo = lax.bitcast_convert_type(lax.shift_left(p, 16), F32).astype(BF16)
    return hi, lo


def _mix_kernel(att_ref, ssm_ref, xn_ref, ag_ref, sg_ref, woa_ref, wos_ref,
                g_ref, b_ref, wrh_ref, wrl_ref, rb_ref,
                x1_ref, x1p_ref, idx_ref, gate_ref, rank_ref, cnt_ref, carry_ref, before_ref):
    @pl.when(pl.program_id(0) == 0)
    def _():
        carry_ref[...] = jnp.zeros_like(carry_ref)
        tm = before_ref.shape[0]
        before = (lax.broadcasted_iota(jnp.int32, (tm, tm), 0)
                  < lax.broadcasted_iota(jnp.int32, (tm, tm), 1))
        before_ref[...] = jnp.where(before, 1.0, 0.0).astype(BF16)

    an = _rms_norm(att_ref[...].astype(F32), ag_ref[...]).astype(BF16)
    sn = _rms_norm(ssm_ref[...].astype(F32), sg_ref[...]).astype(BF16)
    mix = _dot(an, woa_ref[...]) + _dot(sn, wos_ref[...])
    x1 = _layer_norm(ALPHA * xn_ref[...] + mix, g_ref[...], b_ref[...])
    x1_ref[...] = x1
    x1p_ref[...] = _pack_bf16_pairs(x1)
    x_hi = x1.astype(BF16)
    x_lo = (x1 - x_hi.astype(F32)).astype(BF16)
    dg = functools.partial(lax.dot_general, dimension_numbers=NT_DIMS, preferred_element_type=F32)
    logits_t = dg(wrh_ref[...], x_hi) + dg(wrl_ref[...], x_hi) + dg(wrh_ref[...], x_lo)
    idx_k, gate_k, chosen = _router_gates(logits_t, rb_ref[...])
    idx_ref[...] = idx_k.astype(jnp.int32)
    gate_ref[...] = gate_k.T

    n_exp, tm = chosen.shape
    excl = _dot(chosen.astype(BF16), before_ref[...])
    rank_full = carry_ref[...] + excl
    iota = lax.broadcasted_iota(jnp.int32, (n_exp, tm), 0).astype(F32)
    ranks = [jnp.sum(jnp.where(iota == idx_k[k:k + 1, :], rank_full, 0.0), axis=0, keepdims=True)
             for k in range(TOP_K)]
    rank_ref[...] = jnp.concatenate(ranks, axis=0).astype(jnp.int32)
    total = carry_ref[...] + jnp.sum(chosen, axis=1, keepdims=True)
    carry_ref[...] = total
    cnt_ref[...] = jnp.broadcast_to(total, cnt_ref.shape)


def _positions_kernel(off_ref, idx_ref, rank_ref, pos_ref):
    idx = idx_ref[...]
    base = jnp.zeros(idx.shape, jnp.int32)
    for e in range(N_EXPERTS):
        base = jnp.where(idx == e, off_ref[e], base)
    pos_ref[...] = rank_ref[...] + base


def _swiglu(x_hi, x_lo, wg_ref, wu_ref, wd_ref):
    half = x_hi.shape[1]
    hg = _dot(x_hi, wg_ref[:half, :]) + _dot(x_lo, wg_ref[half:, :])
    hu = _dot(x_hi, wu_ref[:half, :]) + _dot(x_lo, wu_ref[half:, :])
    h = jax.nn.silu(hg) * hu
    return _dot(h.astype(BF16), wd_ref[...])


def _experts_kernel(te_ref, valid_ref, xs_ref, wg_ref, wu_ref, wd_ref, ys_ref,
                    wgb_ref, wub_ref, wdb_ref):
    i = pl.program_id(0)
    valid = valid_ref[i]

    @pl.when((i == 0) | (te_ref[i] != te_ref[jnp.maximum(i - 1, 0)]))
    def _():
        wgb_ref[...] = wg_ref[0].astype(BF16)
        wub_ref[...] = wu_ref[0].astype(BF16)
        wdb_ref[...] = wd_ref[0].astype(BF16)

    @pl.when(valid > 0)
    def _():
        rows = lax.broadcasted_iota(jnp.int32, xs_ref.shape, 0)
        x_hi, x_lo = _unpack_bf16_pairs(jnp.where(rows < valid, xs_ref[...], 0))
        ys_ref[...] = _pack_bf16_pairs(_swiglu(x_hi, x_lo, wgb_ref, wub_ref, wdb_ref))

    @pl.when(valid == 0)
    def _():
        ys_ref[...] = jnp.zeros_like(ys_ref)


def _combine_kernel(yg_ref, gate_ref, x1_ref, wsg_ref, wsu_ref, wsd_ref, g_ref, b_ref, o_ref):
    x1 = x1_ref[...]
    half = x1.shape[1] // 2
    acc = _swiglu(x1[:, :half].astype(BF16), x1[:, half:].astype(BF16), wsg_ref, wsu_ref, wsd_ref)
    gates = gate_ref[...]
    acc_hi, acc_lo = acc[:, :half], acc[:, half:]
    for k in range(TOP_K):
        y_hi, y_lo = _unpack_bf16_pairs(yg_ref[k])
        gk = gates[:, k:k + 1]
        acc_hi = acc_hi + gk * y_hi.astype(F32)
        acc_lo = acc_lo + gk * y_lo.astype(F32)
    ffn = jnp.concatenate([acc_hi, acc_lo], axis=1)
    o_ref[...] = _layer_norm(ALPHA * x1 + ffn, g_ref[...], b_ref[...])


def _combine_into_kernel(prev_ref, *refs):
    del prev_ref
    _combine_kernel(*refs)


SC_CORES = 2
SC_SUBCORES = 16
SC_WORKERS = SC_CORES * SC_SUBCORES
SC_CHUNK = 64


def _sc_mesh():
    return plsc.VectorSubcoreMesh(core_axis_name="c", subcore_axis_name="s")


def _sc_gather_rows(table, idx):
    n = idx.shape[0]
    d = table.shape[1]
    assert n % (SC_WORKERS * 2 * SC_CHUNK) == 0, n
    per_w = n // SC_WORKERS
    n_ch = per_w // SC_CHUNK

    @functools.partial(
        pl.kernel, mesh=_sc_mesh(),
        out_type=jax.ShapeDtypeStruct((n, d), table.dtype),
        scratch_types=[pltpu.VMEM((n_ch, SC_CHUNK), jnp.int32),
                       pltpu.VMEM((2, SC_CHUNK, d), table.dtype),
                       pltpu.SemaphoreType.DMA((2,)),
                       pltpu.SemaphoreType.DMA((2,))],
    )
    def k(table_hbm, idx_hbm, out_hbm, idx_v, buf, gsem, osem):
        wid = lax.axis_index("s") * SC_CORES + lax.axis_index("c")
        base = wid * per_w
        pltpu.sync_copy(idx_hbm.at[wid], idx_v)

        def gather(c, b):
            return pltpu.make_async_copy(table_hbm.at[idx_v.at[c]], buf.at[b], gsem.at[b])

        def put(c, b):
            return pltpu.make_async_copy(buf.at[b], out_hbm.at[pl.ds(base + c * SC_CHUNK, SC_CHUNK)],
                                         osem.at[b])

        gather(0, 0).start()

        @pl.loop(0, n_ch, step=2)
        def _(c):
            for b in range(2):
                cc = c + b
                gather(cc, b).wait()

                @pl.when(cc + 1 < n_ch)
                def _():
                    @pl.when(cc >= 1)
                    def _():
                        put(cc - 1, 1 - b).wait()
                    gather(cc + 1, 1 - b).start()

                put(cc, b).start()

        put(n_ch - 2, 0).wait()
        put(n_ch - 1, 1).wait()

    return k(table, idx.reshape(SC_WORKERS, n_ch, SC_CHUNK))


def _sc_scatter_rows(x, pos, n_out):
    t, d = x.shape
    kk = pos.shape[0]
    assert t % (SC_WORKERS * 2 * SC_CHUNK) == 0, t
    per_w = t // SC_WORKERS
    n_ch = per_w // SC_CHUNK
    pos_w = pos.reshape(kk, SC_WORKERS, n_ch, SC_CHUNK).transpose(1, 2, 0, 3)
    pos_w = pos_w.reshape(SC_WORKERS, n_ch * kk, SC_CHUNK)

    @functools.partial(
        pl.kernel, mesh=_sc_mesh(),
        out_type=jax.ShapeDtypeStruct((n_out, d), x.dtype),
        scratch_types=[pltpu.VMEM((n_ch * kk, SC_CHUNK), jnp.int32),
                       pltpu.VMEM((2, SC_CHUNK, d), x.dtype),
                       pltpu.SemaphoreType.DMA((2,)),
                       pltpu.SemaphoreType.DMA((2,))],
    )
    def k(x_hbm, pos_hbm, out_hbm, idx_v, buf, isem, osem):
        wid = lax.axis_index("s") * SC_CORES + lax.axis_index("c")
        base = wid * per_w
        pltpu.sync_copy(pos_hbm.at[wid], idx_v)

        def get(c, b):
            return pltpu.make_async_copy(x_hbm.at[pl.ds(base + c * SC_CHUNK, SC_CHUNK)], buf.at[b],
                                         isem.at[b])

        def put(c, j, b):
            return pltpu.make_async_copy(buf.at[b], out_hbm.at[idx_v.at[c * kk + j]], osem.at[b])

        get(0, 0).start()

        @pl.loop(0, n_ch, step=2)
        def _(c):
            for b in range(2):
                cc = c + b
                get(cc, b).wait()

                @pl.when(cc + 1 < n_ch)
                def _():
                    @pl.when(cc >= 1)
                    def _():
                        for j in range(kk):
                            put(cc - 1, j, 1 - b).wait()
                    get(cc + 1, 1 - b).start()

                for j in range(kk):
                    put(cc, j, b).start()

        for j in range(kk):
            put(n_ch - 2, j, 0).wait()
        for j in range(kk):
            put(n_ch - 1, j, 1).wait()

    return k(x, pos_w)


def _row(v):
    return v.reshape(1, -1).astype(F32)


def _const_spec(shape):
    nd = len(shape)
    return pl.BlockSpec(shape, lambda *_: (0,) * nd)


def _pad_heads(w, width):
    r, h, _ = w.shape
    return jnp.pad(w, ((0, 0), (0, 0), (0, HEAD_PAD - width))).reshape(r, h * HEAD_PAD)


def _half_rotate(w):
    half = QK_ROPE // 2
    return jnp.concatenate([-w[..., half:], w[..., :half]], axis=-1)


def kernel(x, positions, ln_in_g, ln_in_b, w_in, q_norm_g, kv_norm_g, w_uq, w_ukv, lambda_re, lambda_im, log_step, b_re, b_im, c_re, c_im, d_skip, w_glu, b_glu, attn_out_g, ssm_out_g, w_o, ln1_g, ln1_b, w_router, router_bias, w_gate, w_up, w_down, ws_gate, ws_up, ws_down, ln2_g, ln2_b):
    B, S, D = x.shape
    T = B * S
    assert DEPTH == 1 and w_in.shape[0] == DEPTH
    assert S % QUERY_TILE == 0 and S % TOKEN_TILE == 0 and S % S5_TIME_TILE == 0, S
    assert T % (MOE_SLABS * TOKEN_TILE) == 0 and (T // MOE_SLABS * TOP_K) % EXPERT_ROW_TILE == 0, T
    l = 0
    ssm_width = w_glu.shape[-1]
    n_groups = ssm_width // SSM_GROUP
    n_state = n_groups * SSM_STATE
    mla_width = MLA_HEADS * V_DIM
    qk_pad = MLA_HEADS * HEAD_PAD
    cparams = functools.partial(pltpu.CompilerParams, vmem_limit_bytes=VMEM_LIMIT)

    s1, s2, s3 = Q_RANK, Q_RANK + KV_RANK, Q_RANK + KV_RANK + QK_ROPE
    wi = w_in[l]
    w_kr = wi[:, s2:s3]
    pad_rope = lambda w: jnp.pad(w, ((0, 0), (QK_NOPE, HEAD_PAD - QK_NOPE - QK_ROPE)))
    w1 = jnp.concatenate([wi[:, :s2], wi[:, s3:], pad_rope(w_kr), pad_rope(_half_rotate(w_kr))],
                         axis=1).astype(BF16)
    wq = w_uq[l]
    zeros_nope = jnp.zeros(wq.shape[:2] + (QK_NOPE,), wq.dtype)
    wq_main = _pad_heads(wq, QK_NOPE + QK_ROPE).astype(BF16)
    wq_rot = _pad_heads(jnp.concatenate([zeros_nope, _half_rotate(wq[..., QK_NOPE:])], axis=-1),
                        QK_NOPE + QK_ROPE).astype(BF16)
    wkv = w_ukv[l]
    wk = _pad_heads(wkv[..., :QK_NOPE], QK_NOPE).astype(BF16)
    wv = _pad_heads(wkv[..., QK_NOPE:], V_DIM).astype(BF16)
    half = QK_ROPE // 2
    inv_freq = ROPE_THETA ** (-jnp.arange(half, dtype=F32) / half)
    freq = jnp.concatenate([inv_freq, inv_freq]).reshape(QK_ROPE, 1)
    pos_f = positions.astype(F32).reshape(1, T)

    tm = min(TOKEN_TILE, T)
    w1_cols = w1.shape[1]
    tok = lambda width: pl.BlockSpec((tm, width), lambda i: (i, 0))
    xn, q, k, v, u = pl.pallas_call(
        functools.partial(_inproj_kernel, ssm_width=ssm_width),
        grid=(T // tm,),
        in_specs=[tok(D), pl.BlockSpec((1, tm), lambda i: (0, i)), _const_spec((1, D)), _const_spec((1, D)),
                  _const_spec((D, w1_cols)), _const_spec((1, Q_RANK)), _const_spec((1, KV_RANK)),
                  _const_spec((Q_RANK, qk_pad)), _const_spec((Q_RANK, qk_pad)),
                  _const_spec((KV_RANK, qk_pad)), _const_spec((KV_RANK, qk_pad)),
                  _const_spec((QK_ROPE, 1))],
        out_specs=[tok(D), tok(qk_pad), tok(qk_pad), tok(qk_pad), tok(ssm_width)],
        out_shape=[jax.ShapeDtypeStruct((T, D), F32), jax.ShapeDtypeStruct((T, qk_pad), BF16),
                   jax.ShapeDtypeStruct((T, qk_pad), BF16), jax.ShapeDtypeStruct((T, qk_pad), BF16),
                   jax.ShapeDtypeStruct((T, ssm_width), F32)],
        compiler_params=cparams(dimension_semantics=("parallel",)),
        name="inproj",
    )(x.reshape(T, D), pos_f, _row(ln_in_g), _row(ln_in_b), w1, _row(q_norm_g[l]), _row(kv_norm_g[l]),
      wq_main, wq_rot, wk, wv, freq)

    tq = min(QUERY_TILE, S)
    nq = S // tq
    att = pl.pallas_call(
        functools.partial(_attn_kernel, tq=tq, heads=MLA_HEADS, n_qtiles=nq),
        grid=(B, nq),
        in_specs=[pl.BlockSpec((tq, qk_pad), lambda b, i: (b * nq + i, 0)),
                  pl.BlockSpec((S, qk_pad), lambda b, i: (b, 0)),
                  pl.BlockSpec((S, qk_pad), lambda b, i: (b, 0))],
        out_specs=pl.BlockSpec((tq, mla_width), lambda b, i: (b * nq + i, 0)),
        out_shape=jax.ShapeDtypeStruct((T, mla_width), BF16),
        compiler_params=cparams(dimension_semantics=("parallel", "arbitrary")),
        name="attention",
    )(q, k, v)

    lam = lax.complex(jnp.minimum(lambda_re[l].astype(F32), -1e-4), lambda_im[l].astype(F32))
    step = jnp.exp(log_step[l].astype(F32))[:, None]
    lam_bar = jnp.exp(lam * step)
    b_bar = ((lam_bar - 1.0) / lam)[..., None] * lax.complex(b_re[l].astype(F32), b_im[l].astype(F32))
    n_slab = ssm_width // LANES
    g_per_slab = n_groups // n_slab
    slab = n_state // n_slab
    eye = jnp.eye(g_per_slab, dtype=F32)

    def expand_in(bpart):
        bt = bpart.transpose(0, 2, 1).reshape(n_slab, g_per_slab, SSM_GROUP, 1, SSM_STATE)
        return (bt * eye[None, :, None, :, None]).reshape(n_slab, LANES, slab)

    def expand_out(cpart):
        ct = cpart.transpose(0, 2, 1).reshape(n_slab, g_per_slab, SSM_STATE, 1, SSM_GROUP)
        return (ct * eye[None, :, None, :, None]).reshape(n_slab, slab, LANES)

    win = jnp.concatenate([expand_in(jnp.real(b_bar)), expand_in(jnp.imag(b_bar))],
                          axis=2).astype(BF16)
    cre = expand_out(c_re[l].astype(F32)).astype(BF16)
    cim = expand_out(-c_im[l].astype(F32)).astype(BF16)
    a_re = jnp.real(lam_bar).reshape(1, n_state)
    a_im = jnp.imag(lam_bar).reshape(1, n_state)

    lt = min(S5_TIME_TILE, S)
    ssm = pl.pallas_call(
        functools.partial(_s5_kernel, batch=B, lt=lt, n_state=n_state),
        grid=(S // lt,),
        in_specs=[pl.BlockSpec((B, lt, ssm_width), lambda t: (0, t, 0)),
                  _const_spec(win.shape), _const_spec((1, n_state)), _const_spec((1, n_state)),
                  _const_spec(cre.shape), _const_spec(cim.shape), _const_spec((1, ssm_width)),
                  _const_spec((ssm_width, ssm_width)), _const_spec((1, ssm_width))],
        out_specs=pl.BlockSpec((B, lt, ssm_width), lambda t: (0, t, 0)),
        out_shape=jax.ShapeDtypeStruct((B, S, ssm_width), BF16),
        scratch_shapes=[pltpu.VMEM((2 * n_state // LANES, B * lt, LANES), F32),
                        pltpu.VMEM((n_state // LANES, B, LANES), F32),
                        pltpu.VMEM((n_state // LANES, B, LANES), F32),
                        pltpu.VMEM((ssm_width // LANES, B * lt, LANES), F32),
                        pltpu.VMEM((B * lt, ssm_width), F32)],
        compiler_params=cparams(dimension_semantics=("arbitrary",)),
        name="s5",
    )(u.reshape(B, S, ssm_width), win, a_re, a_im, cre, cim, _row(d_skip[l]),
      w_glu[l].astype(BF16), _row(b_glu[l]))

    wo = w_o[l].astype(BF16)
    assert mla_width == ssm_width
    wr_t = w_router[l].T.astype(F32)
    wr_hi = wr_t.astype(BF16)
    wr_lo = (wr_t - wr_hi.astype(F32)).astype(BF16)
    half = D // 2
    n_slabs = MOE_SLABS
    ts = T // n_slabs
    nt = ts // tm
    ssm2 = ssm.reshape(T, ssm_width)
    rbias = router_bias[l].astype(F32).reshape(N_EXPERTS, 1)
    kt = lambda dt: jax.ShapeDtypeStruct((TOP_K, ts), dt)
    tmx = min(MIX_TILE, ts)
    ntx = ts // tmx
    k_spec = pl.BlockSpec((TOP_K, tmx), lambda i: (0, i))

    def route(s):
        tok_s = lambda width: pl.BlockSpec((tmx, width), lambda i: (i + s * ntx, 0))
        tok = lambda width: pl.BlockSpec((tmx, width), lambda i: (i, 0))
        return pl.pallas_call(
            _mix_kernel,
            grid=(ntx,),
            in_specs=[tok_s(mla_width), tok_s(ssm_width), tok_s(D), _const_spec((1, mla_width)),
                      _const_spec((1, ssm_width)), pl.BlockSpec((mla_width, D), lambda i: (0, 0)),
                      pl.BlockSpec((ssm_width, D), lambda i: (1, 0)), _const_spec((1, D)), _const_spec((1, D)),
                      _const_spec((N_EXPERTS, D)), _const_spec((N_EXPERTS, D)),
                      _const_spec((N_EXPERTS, 1))],
            out_specs=[tok(D), tok(half), k_spec, tok(TOP_K), k_spec, _const_spec((N_EXPERTS, LANES))],
            out_shape=[jax.ShapeDtypeStruct((ts, D), F32), jax.ShapeDtypeStruct((ts, half), jnp.int32),
                       kt(jnp.int32), jax.ShapeDtypeStruct((ts, TOP_K), F32), kt(jnp.int32),
                       jax.ShapeDtypeStruct((N_EXPERTS, LANES), F32)],
            scratch_shapes=[pltpu.VMEM((N_EXPERTS, 1), F32), pltpu.VMEM((tmx, tmx), BF16)],
            compiler_params=cparams(dimension_semantics=("arbitrary",)),
            name="mix_router",
        )(att, ssm2, xn, _row(attn_out_g[l]), _row(ssm_out_g[l]), wo, wo,
          _row(ln1_g[l]), _row(ln1_b[l]), wr_hi, wr_lo, rbias)

    tr = EXPERT_ROW_TILE
    n_tiles = (ts * TOP_K) // tr + N_EXPERTS
    n_rows = n_tiles * tr
    tp = min(POSITIONS_TILE, ts)

    def dispatch(x1p, idx_k, rank_k, counts):
        cnt = counts[:, 0].astype(jnp.int32)
        tiles_e = (cnt + tr - 1) // tr
        tile_end = jnp.cumsum(tiles_e)
        tile_start = tile_end - tiles_e
        tile_ids = jnp.arange(n_tiles, dtype=jnp.int32)
        tile_expert = jnp.sum((tile_end[None, :] <= tile_ids[:, None]).astype(jnp.int32), axis=1)
        tile_expert = jnp.minimum(tile_expert, N_EXPERTS - 1)
        owner = (tile_start[None, :] <= tile_ids[:, None]) & (tile_ids[:, None] < tile_end[None, :])
        left = jnp.sum(jnp.where(owner, cnt[None, :] - (tile_ids[:, None] - tile_start[None, :]) * tr, 0),
                       axis=1)
        tile_valid = jnp.clip(left, 0, tr).astype(jnp.int32)
        pos = pl.pallas_call(
            _positions_kernel,
            grid_spec=pltpu.PrefetchScalarGridSpec(
                num_scalar_prefetch=1, grid=(ts // tp,),
                in_specs=[pl.BlockSpec((TOP_K, tp), lambda i, off: (0, i)),
                          pl.BlockSpec((TOP_K, tp), lambda i, off: (0, i))],
                out_specs=pl.BlockSpec((TOP_K, tp), lambda i, off: (0, i))),
            out_shape=kt(jnp.int32),
            name="positions",
        )((tile_start * tr).astype(jnp.int32), idx_k, rank_k)
        return _sc_scatter_rows(x1p, pos, n_rows), pos, tile_expert, tile_valid

    wg, wu, wd = w_gate[l], w_up[l], w_down[l]
    ff = wg.shape[-1]

    def experts(xs, tile_expert, tile_valid):
        return pl.pallas_call(
            _experts_kernel,
            grid_spec=pltpu.PrefetchScalarGridSpec(
                num_scalar_prefetch=2, grid=(n_tiles,),
                in_specs=[pl.BlockSpec((tr, half), lambda i, te, tv: (i, 0)),
                          pl.BlockSpec((1, D, ff), lambda i, te, tv: (te[i], 0, 0)),
                          pl.BlockSpec((1, D, ff), lambda i, te, tv: (te[i], 0, 0)),
                          pl.BlockSpec((1, ff, D), lambda i, te, tv: (te[i], 0, 0))],
                out_specs=pl.BlockSpec((tr, half), lambda i, te, tv: (i, 0)),
                scratch_shapes=[pltpu.VMEM((D, ff), BF16), pltpu.VMEM((D, ff), BF16),
                                pltpu.VMEM((ff, D), BF16)]),
            out_shape=jax.ShapeDtypeStruct((n_rows, half), jnp.int32),
            compiler_params=cparams(dimension_semantics=("arbitrary",)),
            name="experts",
        )(tile_expert, tile_valid, xs, wg, wu, wd)

    shared = (ws_gate[l].astype(BF16), ws_up[l].astype(BF16), ws_down[l].astype(BF16))

    n_chunks = COMBINE_CHUNKS
    tc = ts // n_chunks
    ntc = nt // n_chunks

    def combine(s, c, out_so_far, yg, gate_k, x1):
        in_slab = lambda width: pl.BlockSpec((tm, width), lambda i: (i + c * ntc, 0))
        specs = [pl.BlockSpec((TOP_K, tm, half), lambda i: (0, i, 0)), in_slab(TOP_K), in_slab(D),
                 _const_spec((D, ff)), _const_spec((D, ff)), _const_spec((ff, D)),
                 _const_spec((1, D)), _const_spec((1, D))]
        args = (yg.reshape(TOP_K, tc, half), gate_k, x1, *shared, _row(ln2_g[l]), _row(ln2_b[l]))
        body, aliases = _combine_kernel, {}
        if out_so_far is not None:
            specs = [pl.BlockSpec(memory_space=pl.ANY)] + specs
            args = (out_so_far,) + args
            body, aliases = _combine_into_kernel, {0: 0}
        return pl.pallas_call(
            body,
            grid=(ntc,),
            in_specs=specs,
            out_specs=pl.BlockSpec((tm, D), lambda i: (i + s * nt + c * ntc, 0)),
            out_shape=jax.ShapeDtypeStruct((T, D), F32),
            input_output_aliases=aliases,
            compiler_params=cparams(dimension_semantics=("parallel",)),
            name="combine",
        )(*args)

    routed, moved = [], []
    for s in range(n_slabs):
        x1, x1p, idx_k, gate_k, rank_k, counts = route(s)
        routed.append((x1, gate_k))
        moved.append(dispatch(x1p, idx_k, rank_k, counts))
    gathered = []
    for xs, pos, tile_expert, tile_valid in moved:
        ys = experts(xs, tile_expert, tile_valid)
        gathered.append([_sc_gather_rows(ys, pos[:, c * tc:(c + 1) * tc].reshape(TOP_K * tc))
                         for c in range(n_chunks)])
    out = None
    for s in range(n_slabs):
        x1, gate_k = routed[s]
        for c in range(n_chunks):
            out = combine(s, c, out, gathered[s][c], gate_k, x1)
    return out.reshape(B, S, D)
```

```python
import functools

import jax
import jax.numpy as jnp
from jax import lax
from jax.experimental import pallas as pl
from jax.experimental.pallas import tpu as pltpu
from jax.experimental.pallas import tpu_sc as plsc

CHUNK = 64
MLA_HEADS = 8
QK_NOPE = 64
QK_ROPE = 32
V_DIM = 64
Q_RANK = 256
KV_RANK = 128
ROPE_THETA = 10000.0
SSM_GROUP = 16
SSM_STATE = 64
N_EXPERTS = 64
TOP_K = 8
N_GROUPS = 8
TOP_GROUPS = 4
ROUTED_SCALE = 2.5
DEPTH = 1
ALPHA = (2.0 * DEPTH) ** 0.25
EPS = 1e-5
LOG2_E = 1.4426950408889634

LANES = 128
HEAD_PAD = LANES
VMEM_LIMIT = 56 * 1024 * 1024
TOKEN_TILE = 512
MIX_TILE = 1024
QUERY_TILE = 256
S5_TIME_TILE = 64
POSITIONS_TILE = 2048
EXPERT_ROW_TILE = 1024
MOE_SLABS = 1
COMBINE_CHUNKS = 8

BF16 = jnp.bfloat16
F32 = jnp.float32
NT_DIMS = (((1,), (1,)), ((), ()))


def _dot(a, b):
    return jnp.dot(a, b, preferred_element_type=F32)


def _layer_norm(x, g, b):
    mu = jnp.mean(x, axis=-1, keepdims=True)
    xc = x - mu
    var = jnp.mean(xc * xc, axis=-1, keepdims=True)
    return xc * lax.rsqrt(var + EPS) * g + b


def _rms_norm(x, g):
    return x * lax.rsqrt(jnp.mean(x * x, axis=-1, keepdims=True) + EPS) * g


def _inproj_kernel(x_ref, pos_ref, lng_ref, lnb_ref, w1_ref, qg_ref, kvg_ref,
                   wq_ref, wqr_ref, wk_ref, wv_ref, freq_ref,
                   xn_ref, q_ref, k_ref, v_ref, u_ref, *, ssm_width):
    xn = _layer_norm(x_ref[...], lng_ref[...], lnb_ref[...])
    xn_ref[...] = xn
    h = _dot(xn.astype(BF16), w1_ref[...])
    o1 = Q_RANK
    o2 = o1 + KV_RANK
    o3 = o2 + ssm_width
    o4 = o3 + HEAD_PAD
    cq = h[:, :o1]
    ckv = h[:, o1:o2]
    u_ref[...] = h[:, o2:o3]
    kr_raw = h[:, o3:o4]
    kr_rot = h[:, o4:o4 + HEAD_PAD]
    cqn = _rms_norm(cq, qg_ref[...]).astype(BF16)
    ckvn = _rms_norm(ckv, kvg_ref[...]).astype(BF16)

    tm = x_ref.shape[0]
    ang_t = freq_ref[...] * pos_ref[...]

    def to_token_rows(t):
        padded = jnp.concatenate([jnp.zeros((QK_NOPE, tm), F32), t,
                                  jnp.zeros((HEAD_PAD - QK_NOPE - QK_ROPE, tm), F32)], axis=0)
        return padded.T

    c = to_token_rows(jnp.cos(ang_t))
    s = to_token_rows(jnp.sin(ang_t))
    lane = lax.broadcasted_iota(jnp.int32, (1, HEAD_PAD), 1)
    scale = (QK_NOPE + QK_ROPE) ** -0.5 * LOG2_E
    cos1 = (c + jnp.where(lane < QK_NOPE, 1.0, 0.0)) * scale
    sin1 = s * scale
    cos_t = jnp.concatenate([cos1] * MLA_HEADS, axis=1)
    sin_t = jnp.concatenate([sin1] * MLA_HEADS, axis=1)
    q = _dot(cqn, wq_ref[...]) * cos_t + _dot(cqn, wqr_ref[...]) * sin_t
    q_ref[...] = q.astype(BF16)

    kr = kr_raw * c + kr_rot * s
    k = _dot(ckvn, wk_ref[...]) + jnp.concatenate([kr] * MLA_HEADS, axis=1)
    k_ref[...] = k.astype(BF16)
    ones_col = jnp.concatenate([jnp.where(lane == V_DIM, 1.0, 0.0)] * MLA_HEADS, axis=1)
    v_ref[...] = (_dot(ckvn, wv_ref[...]) + ones_col).astype(BF16)


def _attn_kernel(q_ref, k_ref, v_ref, o_ref, *, tq, heads, n_qtiles):
    qi = pl.program_id(1)
    row_chunk = lax.broadcasted_iota(jnp.int32, (tq, tq), 0) // CHUNK
    col_chunk = lax.broadcasted_iota(jnp.int32, (tq, tq), 1) // CHUNK
    diag_mask = row_chunk >= col_chunk

    def tile(n_blocks):
        keys = n_blocks * tq
        cols = [slice(h * HEAD_PAD, (h + 1) * HEAD_PAD) for h in range(heads)]

        def mask_diag(s):
            s_diag = jnp.where(diag_mask, s[:, keys - tq:], -jnp.inf)
            return s_diag if n_blocks == 1 else jnp.concatenate([s[:, :keys - tq], s_diag], axis=1)

        ss = [lax.dot_general(q_ref[:, c], k_ref[:keys, c], NT_DIMS, preferred_element_type=F32)
              for c in cols]
        ss = [mask_diag(s) for s in ss]
        ms = [jnp.max(s, axis=-1, keepdims=True) for s in ss]
        ps = [jnp.exp2(s - m).astype(BF16) for s, m in zip(ss, ms)]
        accs = [_dot(p, v_ref[:keys, c]) for p, c in zip(ps, cols)]
        outs = [acc[:, :V_DIM] / acc[:, V_DIM:V_DIM + 1] for acc in accs]
        o_ref[...] = jnp.concatenate(outs, axis=1).astype(o_ref.dtype)

    for c in range(n_qtiles):
        pl.when(qi == c)(functools.partial(tile, c + 1))


def _s5_kernel(u_ref, win_ref, are_ref, aim_ref, cre_ref, cim_ref, dskip_ref,
               wglu_ref, bglu_ref, o_ref, vx_ref, hre_ref, him_ref, io_ref, utm_ref, *,
               batch, lt, n_state):
    ti = pl.program_id(0)

    @pl.when(ti == 0)
    def _():
        hre_ref[...] = jnp.zeros_like(hre_ref)
        him_ref[...] = jnp.zeros_like(him_ref)

    width = u_ref.shape[-1]
    n_slab = width // LANES
    n_tiles = n_state // LANES
    slab_tiles = n_tiles // n_slab
    for b in range(batch):
        for c in range(n_slab):
            io_ref[c, b * lt:(b + 1) * lt, :] = u_ref[b, :, c * LANES:(c + 1) * LANES]

    def slab_tiles_of(j):
        return range(j * slab_tiles, (j + 1) * slab_tiles)

    def expand(j):
        for t in range(lt):
            utm_ref[t * batch:(t + 1) * batch, j * LANES:(j + 1) * LANES] = (
                io_ref[j, pl.ds(t, batch, stride=lt), :])
        ub = utm_ref[:, j * LANES:(j + 1) * LANES].astype(BF16)
        vj = _dot(ub, win_ref[j])
        for i, c in enumerate(slab_tiles_of(j)):
            vx_ref[c] = vj[:, i * LANES:(i + 1) * LANES]
            vx_ref[n_tiles + c] = vj[:, (slab_tiles + i) * LANES:(slab_tiles + i + 1) * LANES]

    def scan(j):
        tiles = slab_tiles_of(j)
        ar = [jnp.broadcast_to(are_ref[:, c * LANES:(c + 1) * LANES], (batch, LANES)) for c in tiles]
        ai = [jnp.broadcast_to(aim_ref[:, c * LANES:(c + 1) * LANES], (batch, LANES)) for c in tiles]
        hr = [hre_ref[c] for c in tiles]
        hi = [him_ref[c] for c in tiles]
        for t in range(lt):
            rows = slice(t * batch, (t + 1) * batch)
            for n, c in enumerate(tiles):
                nr = ar[n] * hr[n] - ai[n] * hi[n] + vx_ref[c, rows, :]
                ni = ar[n] * hi[n] + ai[n] * hr[n] + vx_ref[n_tiles + c, rows, :]
                vx_ref[c, rows, :] = nr
                vx_ref[n_tiles + c, rows, :] = ni
                hr[n], hi[n] = nr, ni
        for n, c in enumerate(tiles):
            hre_ref[c] = hr[n]
            him_ref[c] = hi[n]

    def project(j):
        tiles = slab_tiles_of(j)
        xr = jnp.concatenate([vx_ref[c].astype(BF16) for c in tiles], axis=1)
        xi = jnp.concatenate([vx_ref[n_tiles + c].astype(BF16) for c in tiles], axis=1)
        return _dot(xr, cre_ref[j]) + _dot(xi, cim_ref[j])

    ys = [None] * n_slab
    for j in range(n_slab + 2):
        if j < n_slab:
            expand(j)
        if 1 <= j <= n_slab:
            scan(j - 1)
        if j >= 2:
            ys[j - 2] = project(j - 2)
    y = jnp.concatenate(ys, axis=1) + dskip_ref[...] * utm_ref[...]
    y = jax.nn.gelu(y)
    z = _dot(y.astype(BF16), wglu_ref[...]) + bglu_ref[...]
    out = y * jax.nn.sigmoid(z)
    for c in range(n_slab):
        io_ref[c] = out[:, c * LANES:(c + 1) * LANES]
    for b in range(batch):
        for c in range(n_slab):
            o_ref[b, :, c * LANES:(c + 1) * LANES] = io_ref[c, pl.ds(b, lt, stride=batch), :].astype(o_ref.dtype)


def _router_gates(logits_t, rbias):
    n_exp, tm = logits_t.shape
    per_group = n_exp // N_GROUPS
    scores = jax.nn.sigmoid(logits_t)
    sel = scores + rbias
    neg_inf = -jnp.inf
    sub_iota = lax.broadcasted_iota(jnp.int32, (per_group, tm), 0).astype(F32)
    group_score = []
    for g in range(N_GROUPS):
        sg = sel[g * per_group:(g + 1) * per_group, :]
        m1 = jnp.max(sg, axis=0, keepdims=True)
        first = jnp.min(jnp.where(sg == m1, sub_iota, float(per_group)), axis=0, keepdims=True)
        m2 = jnp.max(jnp.where(sub_iota == first, neg_inf, sg), axis=0, keepdims=True)
        group_score.append(m1 + m2)
    masked = []
    for g in range(N_GROUPS):
        rank = jnp.zeros((1, tm), F32)
        for g2 in range(N_GROUPS):
            if g2 == g:
                continue
            ahead = (group_score[g2] >= group_score[g]) if g2 < g else (group_score[g2] > group_score[g])
            rank = rank + jnp.where(ahead, 1.0, 0.0)
        keep = rank < float(TOP_GROUPS)
        masked.append(jnp.where(keep, sel[g * per_group:(g + 1) * per_group, :], neg_inf))
    cur = jnp.concatenate(masked, axis=0)
    iota = lax.broadcasted_iota(jnp.int32, (n_exp, tm), 0).astype(F32)
    chosen = jnp.zeros((n_exp, tm), F32)
    picks, weights = [], []
    for _ in range(TOP_K):
        m = jnp.max(cur, axis=0, keepdims=True)
        idx = jnp.min(jnp.where(cur == m, iota, float(n_exp)), axis=0, keepdims=True)
        pick = iota == idx
        chosen = jnp.where(pick, 1.0, chosen)
        cur = jnp.where(pick, neg_inf, cur)
        picks.append(idx)
        weights.append(jnp.sum(jnp.where(pick, scores, 0.0), axis=0, keepdims=True))
    idx_k = jnp.concatenate(picks, axis=0)
    w_k = jnp.concatenate(weights, axis=0)
    gate_k = w_k / jnp.sum(w_k, axis=0, keepdims=True) * ROUTED_SCALE
    return idx_k, gate_k, chosen


def _pack_bf16_pairs(x):
    n = x.shape[1] // 2
    hi = lax.bitcast_convert_type(x[:, :n].astype(BF16).astype(F32), jnp.int32)
    lo = lax.bitcast_convert_type(x[:, n:].astype(BF16).astype(F32), jnp.int32)
    return hi | lax.shift_right_logical(lo, 16)


def _unpack_bf16_pairs(p):
    hi = lax.bitcast_convert_type(p & jnp.int32(-65536), F32).astype(BF16)
    lo = lax.bitcast_convert_type(lax.shift_left(p, 16), F32).astype(BF16)
    return hi, lo


def _mix_kernel(att_ref, ssm_ref, xn_ref, ag_ref, sg_ref, woa_ref, wos_ref,
                g_ref, b_ref, wrh_ref, wrl_ref, rb_ref,
                x1_ref, x1p_ref, idx_ref, gate_ref, rank_ref, cnt_ref, carry_ref, before_ref):
    @pl.when(pl.program_id(0) == 0)
    def _():
        carry_ref[...] = jnp.zeros_like(carry_ref)
        tm = before_ref.shape[0]
        before = (lax.broadcasted_iota(jnp.int32, (tm, tm), 0)
                  < lax.broadcasted_iota(jnp.int32, (tm, tm), 1))
        before_ref[...] = jnp.where(before, 1.0, 0.0).astype(BF16)

    an = _rms_norm(att_ref[...].astype(F32), ag_ref[...]).astype(BF16)
    sn = _rms_norm(ssm_ref[...].astype(F32), sg_ref[...]).astype(BF16)
    mix = _dot(an, woa_ref[...]) + _dot(sn, wos_ref[...])
    x1 = _layer_norm(ALPHA * xn_ref[...] + mix, g_ref[...], b_ref[...])
    x1_ref[...] = x1
    x1p_ref[...] = _pack_bf16_pairs(x1)
    x_hi = x1.astype(BF16)
    x_lo = (x1 - x_hi.astype(F32)).astype(BF16)
    dg = functools.partial(lax.dot_general, dimension_numbers=NT_DIMS, preferred_element_type=F32)
    logits_t = dg(wrh_ref[...], x_hi) + dg(wrl_ref[...], x_hi) + dg(wrh_ref[...], x_lo)
    idx_k, gate_k, chosen = _router_gates(logits_t, rb_ref[...])
    idx_ref[...] = idx_k.astype(jnp.int32)
    gate_ref[...] = gate_k.T

    n_exp, tm = chosen.shape
    excl = _dot(chosen.astype(BF16), before_ref[...])
    rank_full = carry_ref[...] + excl
    iota = lax.broadcasted_iota(jnp.int32, (n_exp, tm), 0).astype(F32)
    ranks = [jnp.sum(jnp.where(iota == idx_k[k:k + 1, :], rank_full, 0.0), axis=0, keepdims=True)
             for k in range(TOP_K)]
    rank_ref[...] = jnp.concatenate(ranks, axis=0).astype(jnp.int32)
    total = carry_ref[...] + jnp.sum(chosen, axis=1, keepdims=True)
    carry_ref[...] = total
    cnt_ref[...] = jnp.broadcast_to(total, cnt_ref.shape)


def _positions_kernel(off_ref, idx_ref, rank_ref, pos_ref):
    idx = idx_ref[...]
    base = jnp.zeros(idx.shape, jnp.int32)
    for e in range(N_EXPERTS):
        base = jnp.where(idx == e, off_ref[e], base)
    pos_ref[...] = rank_ref[...] + base


def _swiglu(x_hi, x_lo, wg_ref, wu_ref, wd_ref):
    half = x_hi.shape[1]
    hg = _dot(x_hi, wg_ref[:half, :]) + _dot(x_lo, wg_ref[half:, :])
    hu = _dot(x_hi, wu_ref[:half, :]) + _dot(x_lo, wu_ref[half:, :])
    h = jax.nn.silu(hg) * hu
    return _dot(h.astype(BF16), wd_ref[...])


def _experts_kernel(te_ref, valid_ref, xs_ref, wg_ref, wu_ref, wd_ref, ys_ref,
                    wgb_ref, wub_ref, wdb_ref):
    i = pl.program_id(0)
    valid = valid_ref[i]

    @pl.when((i == 0) | (te_ref[i] != te_ref[jnp.maximum(i - 1, 0)]))
    def _():
        wgb_ref[...] = wg_ref[0].astype(BF16)
        wub_ref[...] = wu_ref[0].astype(BF16)
        wdb_ref[...] = wd_ref[0].astype(BF16)

    @pl.when(valid > 0)
    def _():
        rows = lax.broadcasted_iota(jnp.int32, xs_ref.shape, 0)
        x_hi, x_lo = _unpack_bf16_pairs(jnp.where(rows < valid, xs_ref[...], 0))
        ys_ref[...] = _pack_bf16_pairs(_swiglu(x_hi, x_lo, wgb_ref, wub_ref, wdb_ref))

    @pl.when(valid == 0)
    def _():
        ys_ref[...] = jnp.zeros_like(ys_ref)


def _combine_kernel(yg_ref, gate_ref, x1_ref, wsg_ref, wsu_ref, wsd_ref, g_ref, b_ref, o_ref):
    x1 = x1_ref[...]
    half = x1.shape[1] // 2
    acc = _swiglu(x1[:, :half].astype(BF16), x1[:, half:].astype(BF16), wsg_ref, wsu_ref, wsd_ref)
    gates = gate_ref[...]
    acc_hi, acc_lo = acc[:, :half], acc[:, half:]
    for k in range(TOP_K):
        y_hi, y_lo = _unpack_bf16_pairs(yg_ref[k])
        gk = gates[:, k:k + 1]
        acc_hi = acc_hi + gk * y_hi.astype(F32)
        acc_lo = acc_lo + gk * y_lo.astype(F32)
    ffn = jnp.concatenate([acc_hi, acc_lo], axis=1)
    o_ref[...] = _layer_norm(ALPHA * x1 + ffn, g_ref[...], b_ref[...])


def _combine_into_kernel(prev_ref, *refs):
    del prev_ref
    _combine_kernel(*refs)


SC_CORES = 2
SC_SUBCORES = 16
SC_WORKERS = SC_CORES * SC_SUBCORES
SC_CHUNK = 64


def _sc_mesh():
    return plsc.VectorSubcoreMesh(core_axis_name="c", subcore_axis_name="s")


def _sc_gather_rows(table, idx):
    n = idx.shape[0]
    d = table.shape[1]
    assert n % (SC_WORKERS * 2 * SC_CHUNK) == 0, n
    per_w = n // SC_WORKERS
    n_ch = per_w // SC_CHUNK

    @functools.partial(
        pl.kernel, mesh=_sc_mesh(),
        out_type=jax.ShapeDtypeStruct((n, d), table.dtype),
        scratch_types=[pltpu.VMEM((n_ch, SC_CHUNK), jnp.int32),
                       pltpu.VMEM((2, SC_CHUNK, d), table.dtype),
                       pltpu.SemaphoreType.DMA((2,)),
                       pltpu.SemaphoreType.DMA((2,))],
    )
    def k(table_hbm, idx_hbm, out_hbm, idx_v, buf, gsem, osem):
        wid = lax.axis_index("s") * SC_CORES + lax.axis_index("c")
        base = wid * per_w
        pltpu.sync_copy(idx_hbm.at[wid], idx_v)

        def gather(c, b):
            return pltpu.make_async_copy(table_hbm.at[idx_v.at[c]], buf.at[b], gsem.at[b])

        def put(c, b):
            return pltpu.make_async_copy(buf.at[b], out_hbm.at[pl.ds(base + c * SC_CHUNK, SC_CHUNK)],
                                         osem.at[b])

        gather(0, 0).start()

        @pl.loop(0, n_ch, step=2)
        def _(c):
            for b in range(2):
                cc = c + b
                gather(cc, b).wait()

                @pl.when(cc + 1 < n_ch)
                def _():
                    @pl.when(cc >= 1)
                    def _():
                        put(cc - 1, 1 - b).wait()
                    gather(cc + 1, 1 - b).start()

                put(cc, b).start()

        put(n_ch - 2, 0).wait()
        put(n_ch - 1, 1).wait()

    return k(table, idx.reshape(SC_WORKERS, n_ch, SC_CHUNK))


def _sc_scatter_rows(x, pos, n_out):
    t, d = x.shape
    kk = pos.shape[0]
    assert t % (SC_WORKERS * 2 * SC_CHUNK) == 0, t
    per_w = t // SC_WORKERS
    n_ch = per_w // SC_CHUNK
    pos_w = pos.reshape(kk, SC_WORKERS, n_ch, SC_CHUNK).transpose(1, 2, 0, 3)
    pos_w = pos_w.reshape(SC_WORKERS, n_ch * kk, SC_CHUNK)

    @functools.partial(
        pl.kernel, mesh=_sc_mesh(),
        out_type=jax.ShapeDtypeStruct((n_out, d), x.dtype),
        scratch_types=[pltpu.VMEM((n_ch * kk, SC_CHUNK), jnp.int32),
                       pltpu.VMEM((2, SC_CHUNK, d), x.dtype),
                       pltpu.SemaphoreType.DMA((2,)),
                       pltpu.SemaphoreType.DMA((2,))],
    )
    def k(x_hbm, pos_hbm, out_hbm, idx_v, buf, isem, osem):
        wid = lax.axis_index("s") * SC_CORES + lax.axis_index("c")
        base = wid * per_w
        pltpu.sync_copy(pos_hbm.at[wid], idx_v)

        def get(c, b):
            return pltpu.make_async_copy(x_hbm.at[pl.ds(base + c * SC_CHUNK, SC_CHUNK)], buf.at[b],
                                         isem.at[b])

        def put(c, j, b):
            return pltpu.make_async_copy(buf.at[b], out_hbm.at[idx_v.at[c * kk + j]], osem.at[b])

        get(0, 0).start()

        @pl.loop(0, n_ch, step=2)
        def _(c):
            for b in range(2):
                cc = c + b
                get(cc, b).wait()

                @pl.when(cc + 1 < n_ch)
                def _():
                    @pl.when(cc >= 1)
                    def _():
                        for j in range(kk):
                            put(cc - 1, j, 1 - b).wait()
                    get(cc + 1, 1 - b).start()

                for j in range(kk):
                    put(cc, j, b).start()

        for j in range(kk):
            put(n_ch - 2, j, 0).wait()
        for j in range(kk):
            put(n_ch - 1, j, 1).wait()

    return k(x, pos_w)


def _row(v):
    return v.reshape(1, -1).astype(F32)


def _const_spec(shape):
    nd = len(shape)
    return pl.BlockSpec(shape, lambda *_: (0,) * nd)


def _pad_heads(w, width):
    r, h, _ = w.shape
    return jnp.pad(w, ((0, 0), (0, 0), (0, HEAD_PAD - width))).reshape(r, h * HEAD_PAD)


def _half_rotate(w):
    half = QK_ROPE // 2
    return jnp.concatenate([-w[..., half:], w[..., :half]], axis=-1)


def kernel(x, positions, ln_in_g, ln_in_b, w_in, q_norm_g, kv_norm_g, w_uq, w_ukv, lambda_re, lambda_im, log_step, b_re, b_im, c_re, c_im, d_skip, w_glu, b_glu, attn_out_g, ssm_out_g, w_o, ln1_g, ln1_b, w_router, router_bias, w_gate, w_up, w_down, ws_gate, ws_up, ws_down, ln2_g, ln2_b):
    B, S, D = x.shape
    T = B * S
    assert DEPTH == 1 and w_in.shape[0] == DEPTH
    assert S % QUERY_TILE == 0 and S % TOKEN_TILE == 0 and S % S5_TIME_TILE == 0, S
    assert T % (MOE_SLABS * TOKEN_TILE) == 0 and (T // MOE_SLABS * TOP_K) % EXPERT_ROW_TILE == 0, T
    l = 0
    ssm_width = w_glu.shape[-1]
    n_groups = ssm_width // SSM_GROUP
    n_state = n_groups * SSM_STATE
    mla_width = MLA_HEADS * V_DIM
    qk_pad = MLA_HEADS * HEAD_PAD
    cparams = functools.partial(pltpu.CompilerParams, vmem_limit_bytes=VMEM_LIMIT)

    s1, s2, s3 = Q_RANK, Q_RANK + KV_RANK, Q_RANK + KV_RANK + QK_ROPE
    wi = w_in[l]
    w_kr = wi[:, s2:s3]
    pad_rope = lambda w: jnp.pad(w, ((0, 0), (QK_NOPE, HEAD_PAD - QK_NOPE - QK_ROPE)))
    w1 = jnp.concatenate([wi[:, :s2], wi[:, s3:], pad_rope(w_kr), pad_rope(_half_rotate(w_kr))],
                         axis=1).astype(BF16)
    wq = w_uq[l]
    zeros_nope = jnp.zeros(wq.shape[:2] + (QK_NOPE,), wq.dtype)
    wq_main = _pad_heads(wq, QK_NOPE + QK_ROPE).astype(BF16)
    wq_rot = _pad_heads(jnp.concatenate([zeros_nope, _half_rotate(wq[..., QK_NOPE:])], axis=-1),
                        QK_NOPE + QK_ROPE).astype(BF16)
    wkv = w_ukv[l]
    wk = _pad_heads(wkv[..., :QK_NOPE], QK_NOPE).astype(BF16)
    wv = _pad_heads(wkv[..., QK_NOPE:], V_DIM).astype(BF16)
    half = QK_ROPE // 2
    inv_freq = ROPE_THETA ** (-jnp.arange(half, dtype=F32) / half)
    freq = jnp.concatenate([inv_freq, inv_freq]).reshape(QK_ROPE, 1)
    pos_f = positions.astype(F32).reshape(1, T)

    tm = min(TOKEN_TILE, T)
    w1_cols = w1.shape[1]
    tok = lambda width: pl.BlockSpec((tm, width), lambda i: (i, 0))
    xn, q, k, v, u = pl.pallas_call(
        functools.partial(_inproj_kernel, ssm_width=ssm_width),
        grid=(T // tm,),
        in_specs=[tok(D), pl.BlockSpec((1, tm), lambda i: (0, i)), _const_spec((1, D)), _const_spec((1, D)),
                  _const_spec((D, w1_cols)), _const_spec((1, Q_RANK)), _const_spec((1, KV_RANK)),
                  _const_spec((Q_RANK, qk_pad)), _const_spec((Q_RANK, qk_pad)),
                  _const_spec((KV_RANK, qk_pad)), _const_spec((KV_RANK, qk_pad)),
                  _const_spec((QK_ROPE, 1))],
        out_specs=[tok(D), tok(qk_pad), tok(qk_pad), tok(qk_pad), tok(ssm_width)],
        out_shape=[jax.ShapeDtypeStruct((T, D), F32), jax.ShapeDtypeStruct((T, qk_pad), BF16),
                   jax.ShapeDtypeStruct((T, qk_pad), BF16), jax.ShapeDtypeStruct((T, qk_pad), BF16),
                   jax.ShapeDtypeStruct((T, ssm_width), F32)],
        compiler_params=cparams(dimension_semantics=("parallel",)),
        name="inproj",
    )(x.reshape(T, D), pos_f, _row(ln_in_g), _row(ln_in_b), w1, _row(q_norm_g[l]), _row(kv_norm_g[l]),
      wq_main, wq_rot, wk, wv, freq)

    tq = min(QUERY_TILE, S)
    nq = S // tq
    att = pl.pallas_call(
        functools.partial(_attn_kernel, tq=tq, heads=MLA_HEADS, n_qtiles=nq),
        grid=(B, nq),
        in_specs=[pl.BlockSpec((tq, qk_pad), lambda b, i: (b * nq + i, 0)),
                  pl.BlockSpec((S, qk_pad), lambda b, i: (b, 0)),
                  pl.BlockSpec((S, qk_pad), lambda b, i: (b, 0))],
        out_specs=pl.BlockSpec((tq, mla_width), lambda b, i: (b * nq + i, 0)),
        out_shape=jax.ShapeDtypeStruct((T, mla_width), BF16),
        compiler_params=cparams(dimension_semantics=("parallel", "arbitrary")),
        name="attention",
    )(q, k, v)

    lam = lax.complex(jnp.minimum(lambda_re[l].astype(F32), -1e-4), lambda_im[l].astype(F32))
    step = jnp.exp(log_step[l].astype(F32))[:, None]
    lam_bar = jnp.exp(lam * step)
    b_bar = ((lam_bar - 1.0) / lam)[..., None] * lax.complex(b_re[l].astype(F32), b_im[l].astype(F32))
    n_slab = ssm_width // LANES
    g_per_slab = n_groups // n_slab
    slab = n_state // n_slab
    eye = jnp.eye(g_per_slab, dtype=F32)

    def expand_in(bpart):
        bt = bpart.transpose(0, 2, 1).reshape(n_slab, g_per_slab, SSM_GROUP, 1, SSM_STATE)
        return (bt * eye[None, :, None, :, None]).reshape(n_slab, LANES, slab)

    def expand_out(cpart):
        ct = cpart.transpose(0, 2, 1).reshape(n_slab, g_per_slab, SSM_STATE, 1, SSM_GROUP)
        return (ct * eye[None, :, None, :, None]).reshape(n_slab, slab, LANES)

    win = jnp.concatenate([expand_in(jnp.real(b_bar)), expand_in(jnp.imag(b_bar))],
                          axis=2).astype(BF16)
    cre = expand_out(c_re[l].astype(F32)).astype(BF16)
    cim = expand_out(-c_im[l].astype(F32)).astype(BF16)
    a_re = jnp.real(lam_bar).reshape(1, n_state)
    a_im = jnp.imag(lam_bar).reshape(1, n_state)

    lt = min(S5_TIME_TILE, S)
    ssm = pl.pallas_call(
        functools.partial(_s5_kernel, batch=B, lt=lt, n_state=n_state),
        grid=(S // lt,),
        in_specs=[pl.BlockSpec((B, lt, ssm_width), lambda t: (0, t, 0)),
                  _const_spec(win.shape), _const_spec((1, n_state)), _const_spec((1, n_state)),
                  _const_spec(cre.shape), _const_spec(cim.shape), _const_spec((1, ssm_width)),
                  _const_spec((ssm_width, ssm_width)), _const_spec((1, ssm_width))],
        out_specs=pl.BlockSpec((B, lt, ssm_width), lambda t: (0, t, 0)),
        out_shape=jax.ShapeDtypeStruct((B, S, ssm_width), BF16),
        scratch_shapes=[pltpu.VMEM((2 * n_state // LANES, B * lt, LANES), F32),
                        pltpu.VMEM((n_state // LANES, B, LANES), F32),
                        pltpu.VMEM((n_state // LANES, B, LANES), F32),
                        pltpu.VMEM((ssm_width // LANES, B * lt, LANES), F32),
                        pltpu.VMEM((B * lt, ssm_width), F32)],
        compiler_params=cparams(dimension_semantics=("arbitrary",)),
        name="s5",
    )(u.reshape(B, S, ssm_width), win, a_re, a_im, cre, cim, _row(d_skip[l]),
      w_glu[l].astype(BF16), _row(b_glu[l]))

    wo = w_o[l].astype(BF16)
    assert mla_width == ssm_width
    wr_t = w_router[l].T.astype(F32)
    wr_hi = wr_t.astype(BF16)
    wr_lo = (wr_t - wr_hi.astype(F32)).astype(BF16)
    half = D // 2
    n_slabs = MOE_SLABS
    ts = T // n_slabs
    nt = ts // tm
    ssm2 = ssm.reshape(T, ssm_width)
    rbias = router_bias[l].astype(F32).reshape(N_EXPERTS, 1)
    kt = lambda dt: jax.ShapeDtypeStruct((TOP_K, ts), dt)
    tmx = min(MIX_TILE, ts)
    ntx = ts // tmx
    k_spec = pl.BlockSpec((TOP_K, tmx), lambda i: (0, i))

    def route(s):
        tok_s = lambda width: pl.BlockSpec((tmx, width), lambda i: (i + s * ntx, 0))
        tok = lambda width: pl.BlockSpec((tmx, width), lambda i: (i, 0))
        return pl.pallas_call(
            _mix_kernel,
            grid=(ntx,),
            in_specs=[tok_s(mla_width), tok_s(ssm_width), tok_s(D), _const_spec((1, mla_width)),
                      _const_spec((1, ssm_width)), pl.BlockSpec((mla_width, D), lambda i: (0, 0)),
                      pl.BlockSpec((ssm_width, D), lambda i: (1, 0)), _const_spec((1, D)), _const_spec((1, D)),
                      _const_spec((N_EXPERTS, D)), _const_spec((N_EXPERTS, D)),
                      _const_spec((N_EXPERTS, 1))],
            out_specs=[tok(D), tok(half), k_spec, tok(TOP_K), k_spec, _const_spec((N_EXPERTS, LANES))],
            out_shape=[jax.ShapeDtypeStruct((ts, D), F32), jax.ShapeDtypeStruct((ts, half), jnp.int32),
                       kt(jnp.int32), jax.ShapeDtypeStruct((ts, TOP_K), F32), kt(jnp.int32),
                       jax.ShapeDtypeStruct((N_EXPERTS, LANES), F32)],
            scratch_shapes=[pltpu.VMEM((N_EXPERTS, 1), F32), pltpu.VMEM((tmx, tmx), BF16)],
            compiler_params=cparams(dimension_semantics=("arbitrary",)),
            name="mix_router",
        )(att, ssm2, xn, _row(attn_out_g[l]), _row(ssm_out_g[l]), wo, wo,
          _row(ln1_g[l]), _row(ln1_b[l]), wr_hi, wr_lo, rbias)

    tr = EXPERT_ROW_TILE
    n_tiles = (ts * TOP_K) // tr + N_EXPERTS
    n_rows = n_tiles * tr
    tp = min(POSITIONS_TILE, ts)

    def dispatch(x1p, idx_k, rank_k, counts):
        cnt = counts[:, 0].astype(jnp.int32)
        tiles_e = (cnt + tr - 1) // tr
        tile_end = jnp.cumsum(tiles_e)
        tile_start = tile_end - tiles_e
        tile_ids = jnp.arange(n_tiles, dtype=jnp.int32)
        tile_expert = jnp.sum((tile_end[None, :] <= tile_ids[:, None]).astype(jnp.int32), axis=1)
        tile_expert = jnp.minimum(tile_expert, N_EXPERTS - 1)
        owner = (tile_start[None, :] <= tile_ids[:, None]) & (tile_ids[:, None] < tile_end[None, :])
        left = jnp.sum(jnp.where(owner, cnt[None, :] - (tile_ids[:, None] - tile_start[None, :]) * tr, 0),
                       axis=1)
        tile_valid = jnp.clip(left, 0, tr).astype(jnp.int32)
        pos = pl.pallas_call(
            _positions_kernel,
            grid_spec=pltpu.PrefetchScalarGridSpec(
                num_scalar_prefetch=1, grid=(ts // tp,),
                in_specs=[pl.BlockSpec((TOP_K, tp), lambda i, off: (0, i)),
                          pl.BlockSpec((TOP_K, tp), lambda i, off: (0, i))],
                out_specs=pl.BlockSpec((TOP_K, tp), lambda i, off: (0, i))),
            out_shape=kt(jnp.int32),
            name="positions",
        )((tile_start * tr).astype(jnp.int32), idx_k, rank_k)
        return _sc_scatter_rows(x1p, pos, n_rows), pos, tile_expert, tile_valid

    wg, wu, wd = w_gate[l], w_up[l], w_down[l]
    ff = wg.shape[-1]

    def experts(xs, tile_expert, tile_valid):
        return pl.pallas_call(
            _experts_kernel,
            grid_spec=pltpu.PrefetchScalarGridSpec(
                num_scalar_prefetch=2, grid=(n_tiles,),
                in_specs=[pl.BlockSpec((tr, half), lambda i, te, tv: (i, 0)),
                          pl.BlockSpec((1, D, ff), lambda i, te, tv: (te[i], 0, 0)),
                          pl.BlockSpec((1, D, ff), lambda i, te, tv: (te[i], 0, 0)),
                          pl.BlockSpec((1, ff, D), lambda i, te, tv: (te[i], 0, 0))],
                out_specs=pl.BlockSpec((tr, half), lambda i, te, tv: (i, 0)),
                scratch_shapes=[pltpu.VMEM((D, ff), BF16), pltpu.VMEM((D, ff), BF16),
                                pltpu.VMEM((ff, D), BF16)]),
            out_shape=jax.ShapeDtypeStruct((n_rows, half), jnp.int32),
            compiler_params=cparams(dimension_semantics=("arbitrary",)),
            name="experts",
        )(tile_expert, tile_valid, xs, wg, wu, wd)

    shared = (ws_gate[l].astype(BF16), ws_up[l].astype(BF16), ws_down[l].astype(BF16))

    n_chunks = COMBINE_CHUNKS
    tc = ts // n_chunks
    ntc = nt // n_chunks

    def combine(s, c, out_so_far, yg, gate_k, x1):
        in_slab = lambda width: pl.BlockSpec((tm, width), lambda i: (i + c * ntc, 0))
        specs = [pl.BlockSpec((TOP_K, tm, half), lambda i: (0, i, 0)), in_slab(TOP_K), in_slab(D),
                 _const_spec((D, ff)), _const_spec((D, ff)), _const_spec((ff, D)),
                 _const_spec((1, D)), _const_spec((1, D))]
        args = (yg.reshape(TOP_K, tc, half), gate_k, x1, *shared, _row(ln2_g[l]), _row(ln2_b[l]))
        body, aliases = _combine_kernel, {}
        if out_so_far is not None:
            specs = [pl.BlockSpec(memory_space=pl.ANY)] + specs
            args = (out_so_far,) + args
            body, aliases = _combine_into_kernel, {0: 0}
        return pl.pallas_call(
            body,
            grid=(ntc,),
            in_specs=specs,
            out_specs=pl.BlockSpec((tm, D), lambda i: (i + s * nt + c * ntc, 0)),
            out_shape=jax.ShapeDtypeStruct((T, D), F32),
            input_output_aliases=aliases,
            compiler_params=cparams(dimension_semantics=("parallel",)),
            name="combine",
        )(*args)

    routed, moved = [], []
    for s in range(n_slabs):
        x1, x1p, idx_k, gate_k, rank_k, counts = route(s)
        routed.append((x1, gate_k))
        moved.append(dispatch(x1p, idx_k, rank_k, counts))
    gathered = []
    for xs, pos, tile_expert, tile_valid in moved:
        ys = experts(xs, tile_expert, tile_valid)
        gathered.append([_sc_gather_rows(ys, pos[:, c * tc:(c + 1) * tc].reshape(TOP_K * tc))
                         for c in range(n_chunks)])
    out = None
    for s in range(n_slabs):
        x1, gate_k = routed[s]
        for c in range(n_chunks):
            out = combine(s, c, out, gathered[s][c], gate_k, x1)
    return out.reshape(B, S, D)
```

```python
import functools

import jax
import jax.numpy as jnp
from jax import lax
from jax.experimental import pallas as pl
from jax.experimental.pallas import tpu as pltpu
from jax.experimental.pallas import tpu_sc as plsc

CHUNK = 64
MLA_HEADS = 8
QK_NOPE = 64
QK_ROPE = 32
V_DIM = 64
Q_RANK = 256
KV_RANK = 128
ROPE_THETA = 10000.0
SSM_GROUP = 16
SSM_STATE = 64
N_EXPERTS = 64
TOP_K = 8
N_GROUPS = 8
TOP_GROUPS = 4
ROUTED_SCALE = 2.5
DEPTH = 1
ALPHA = (2.0 * DEPTH) ** 0.25
EPS = 1e-5
LOG2_E = 1.4426950408889634

LANES = 128
HEAD_PAD = LANES
VMEM_LIMIT = 56 * 1024 * 1024
TOKEN_TILE = 512
MIX_TILE = 1024
QUERY_TILE = 256
S5_TIME_TILE = 64
POSITIONS_TILE = 2048
EXPERT_ROW_TILE = 1024
MOE_SLABS = 1
COMBINE_CHUNKS = 4

BF16 = jnp.bfloat16
F32 = jnp.float32
NT_DIMS = (((1,), (1,)), ((), ()))


def _dot(a, b):
    return jnp.dot(a, b, preferred_element_type=F32)


def _layer_norm(x, g, b):
    mu = jnp.mean(x, axis=-1, keepdims=True)
    xc = x - mu
    var = jnp.mean(xc * xc, axis=-1, keepdims=True)
    return xc * lax.rsqrt(var + EPS) * g + b


def _rms_norm(x, g):
    return x * lax.rsqrt(jnp.mean(x * x, axis=-1, keepdims=True) + EPS) * g


def _inproj_kernel(x_ref, pos_ref, lng_ref, lnb_ref, w1_ref, qg_ref, kvg_ref,
                   wq_ref, wqr_ref, wk_ref, wv_ref, freq_ref,
                   xn_ref, q_ref, k_ref, v_ref, u_ref, *, ssm_width):
    xn = _layer_norm(x_ref[...], lng_ref[...], lnb_ref[...])
    xn_ref[...] = xn
    h = _dot(xn.astype(BF16), w1_ref[...])
    o1 = Q_RANK
    o2 = o1 + KV_RANK
    o3 = o2 + ssm_width
    o4 = o3 + HEAD_PAD
    cq = h[:, :o1]
    ckv = h[:, o1:o2]
    u_ref[...] = h[:, o2:o3]
    kr_raw = h[:, o3:o4]
    kr_rot = h[:, o4:o4 + HEAD_PAD]
    cqn = _rms_norm(cq, qg_ref[...]).astype(BF16)
    ckvn = _rms_norm(ckv, kvg_ref[...]).astype(BF16)

    tm = x_ref.shape[0]
    ang_t = freq_ref[...] * pos_ref[...]

    def to_token_rows(t):
        padded = jnp.concatenate([jnp.zeros((QK_NOPE, tm), F32), t,
                                  jnp.zeros((HEAD_PAD - QK_NOPE - QK_ROPE, tm), F32)], axis=0)
        return padded.T

    c = to_token_rows(jnp.cos(ang_t))
    s = to_token_rows(jnp.sin(ang_t))
    lane = lax.broadcasted_iota(jnp.int32, (1, HEAD_PAD), 1)
    scale = (QK_NOPE + QK_ROPE) ** -0.5 * LOG2_E
    cos1 = (c + jnp.where(lane < QK_NOPE, 1.0, 0.0)) * scale
    sin1 = s * scale
    cos_t = jnp.concatenate([cos1] * MLA_HEADS, axis=1)
    sin_t = jnp.concatenate([sin1] * MLA_HEADS, axis=1)
    q = _dot(cqn, wq_ref[...]) * cos_t + _dot(cqn, wqr_ref[...]) * sin_t
    q_ref[...] = q.astype(BF16)

    kr = kr_raw * c + kr_rot * s
    k = _dot(ckvn, wk_ref[...]) + jnp.concatenate([kr] * MLA_HEADS, axis=1)
    k_ref[...] = k.astype(BF16)
    ones_col = jnp.concatenate([jnp.where(lane == V_DIM, 1.0, 0.0)] * MLA_HEADS, axis=1)
    v_ref[...] = (_dot(ckvn, wv_ref[...]) + ones_col).astype(BF16)


def _attn_kernel(q_ref, k_ref, v_ref, o_ref, *, tq, heads, n_qtiles):
    qi = pl.program_id(1)
    row_chunk = lax.broadcasted_iota(jnp.int32, (tq, tq), 0) // CHUNK
    col_chunk = lax.broadcasted_iota(jnp.int32, (tq, tq), 1) // CHUNK
    diag_mask = row_chunk >= col_chunk

    def tile(n_blocks):
        keys = n_blocks * tq
        cols = [slice(h * HEAD_PAD, (h + 1) * HEAD_PAD) for h in range(heads)]

        def mask_diag(s):
            s_diag = jnp.where(diag_mask, s[:, keys - tq:], -jnp.inf)
            return s_diag if n_blocks == 1 else jnp.concatenate([s[:, :keys - tq], s_diag], axis=1)

        ss = [lax.dot_general(q_ref[:, c], k_ref[:keys, c], NT_DIMS, preferred_element_type=F32)
              for c in cols]
        ss = [mask_diag(s) for s in ss]
        ms = [jnp.max(s, axis=-1, keepdims=True) for s in ss]
        ps = [jnp.exp2(s - m).astype(BF16) for s, m in zip(ss, ms)]
        accs = [_dot(p, v_ref[:keys, c]) for p, c in zip(ps, cols)]
        outs = [acc[:, :V_DIM] / acc[:, V_DIM:V_DIM + 1] for acc in accs]
        o_ref[...] = jnp.concatenate(outs, axis=1).astype(o_ref.dtype)

    for c in range(n_qtiles):
        pl.when(qi == c)(functools.partial(tile, c + 1))


def _s5_kernel(u_ref, win_ref, are_ref, aim_ref, cre_ref, cim_ref, dskip_ref,
               wglu_ref, bglu_ref, o_ref, vx_ref, hre_ref, him_ref, io_ref, utm_ref, *,
               batch, lt, n_state):
    ti = pl.program_id(0)

    @pl.when(ti == 0)
    def _():
        hre_ref[...] = jnp.zeros_like(hre_ref)
        him_ref[...] = jnp.zeros_like(him_ref)

    width = u_ref.shape[-1]
    n_slab = width // LANES
    n_tiles = n_state // LANES
    slab_tiles = n_tiles // n_slab
    for b in range(batch):
        for c in range(n_slab):
            io_ref[c, b * lt:(b + 1) * lt, :] = u_ref[b, :, c * LANES:(c + 1) * LANES]

    def slab_tiles_of(j):
        return range(j * slab_tiles, (j + 1) * slab_tiles)

    def expand(j):
        for t in range(lt):
            utm_ref[t * batch:(t + 1) * batch, j * LANES:(j + 1) * LANES] = (
                io_ref[j, pl.ds(t, batch, stride=lt), :])
        ub = utm_ref[:, j * LANES:(j + 1) * LANES].astype(BF16)
        vj = _dot(ub, win_ref[j])
        for i, c in enumerate(slab_tiles_of(j)):
            vx_ref[c] = vj[:, i * LANES:(i + 1) * LANES]
            vx_ref[n_tiles + c] = vj[:, (slab_tiles + i) * LANES:(slab_tiles + i + 1) * LANES]

    def scan(j):
        tiles = slab_tiles_of(j)
        ar = [jnp.broadcast_to(are_ref[:, c * LANES:(c + 1) * LANES], (batch, LANES)) for c in tiles]
        ai = [jnp.broadcast_to(aim_ref[:, c * LANES:(c + 1) * LANES], (batch, LANES)) for c in tiles]
        hr = [hre_ref[c] for c in tiles]
        hi = [him_ref[c] for c in tiles]
        for t in range(lt):
            rows = slice(t * batch, (t + 1) * batch)
            for n, c in enumerate(tiles):
                nr = ar[n] * hr[n] - ai[n] * hi[n] + vx_ref[c, rows, :]
                ni = ar[n] * hi[n] + ai[n] * hr[n] + vx_ref[n_tiles + c, rows, :]
                vx_ref[c, rows, :] = nr
                vx_ref[n_tiles + c, rows, :] = ni
                hr[n], hi[n] = nr, ni
        for n, c in enumerate(tiles):
            hre_ref[c] = hr[n]
            him_ref[c] = hi[n]

    def project(j):
        tiles = slab_tiles_of(j)
        xr = jnp.concatenate([vx_ref[c].astype(BF16) for c in tiles], axis=1)
        xi = jnp.concatenate([vx_ref[n_tiles + c].astype(BF16) for c in tiles], axis=1)
        return _dot(xr, cre_ref[j]) + _dot(xi, cim_ref[j])

    ys = [None] * n_slab
    for j in range(n_slab + 2):
        if j < n_slab:
            expand(j)
        if 1 <= j <= n_slab:
            scan(j - 1)
        if j >= 2:
            ys[j - 2] = project(j - 2)
    y = jnp.concatenate(ys, axis=1) + dskip_ref[...] * utm_ref[...]
    y = jax.nn.gelu(y)
    z = _dot(y.astype(BF16), wglu_ref[...]) + bglu_ref[...]
    out = y * jax.nn.sigmoid(z)
    for c in range(n_slab):
        io_ref[c] = out[:, c * LANES:(c + 1) * LANES]
    for b in range(batch):
        for c in range(n_slab):
            o_ref[b, :, c * LANES:(c + 1) * LANES] = io_ref[c, pl.ds(b, lt, stride=batch), :].astype(o_ref.dtype)


def _router_gates(logits_t, rbias):
    n_exp, tm = logits_t.shape
    per_group = n_exp // N_GROUPS
    scores = jax.nn.sigmoid(logits_t)
    sel = scores + rbias
    neg_inf = -jnp.inf
    sub_iota = lax.broadcasted_iota(jnp.int32, (per_group, tm), 0).astype(F32)
    group_score = []
    for g in range(N_GROUPS):
        sg = sel[g * per_group:(g + 1) * per_group, :]
        m1 = jnp.max(sg, axis=0, keepdims=True)
        first = jnp.min(jnp.where(sg == m1, sub_iota, float(per_group)), axis=0, keepdims=True)
        m2 = jnp.max(jnp.where(sub_iota == first, neg_inf, sg), axis=0, keepdims=True)
        group_score.append(m1 + m2)
    masked = []
    for g in range(N_GROUPS):
        rank = jnp.zeros((1, tm), F32)
        for g2 in range(N_GROUPS):
            if g2 == g:
                continue
            ahead = (group_score[g2] >= group_score[g]) if g2 < g else (group_score[g2] > group_score[g])
            rank = rank + jnp.where(ahead, 1.0, 0.0)
        keep = rank < float(TOP_GROUPS)
        masked.append(jnp.where(keep, sel[g * per_group:(g + 1) * per_group, :], neg_inf))
    cur = jnp.concatenate(masked, axis=0)
    iota = lax.broadcasted_iota(jnp.int32, (n_exp, tm), 0).astype(F32)
    chosen = jnp.zeros((n_exp, tm), F32)
    picks, weights = [], []
    for _ in range(TOP_K):
        m = jnp.max(cur, axis=0, keepdims=True)
        idx = jnp.min(jnp.where(cur == m, iota, float(n_exp)), axis=0, keepdims=True)
        pick = iota == idx
        chosen = jnp.where(pick, 1.0, chosen)
        cur = jnp.where(pick, neg_inf, cur)
        picks.append(idx)
        weights.append(jnp.sum(jnp.where(pick, scores, 0.0), axis=0, keepdims=True))
    idx_k = jnp.concatenate(picks, axis=0)
    w_k = jnp.concatenate(weights, axis=0)
    gate_k = w_k / jnp.sum(w_k, axis=0, keepdims=True) * ROUTED_SCALE
    return idx_k, gate_k, chosen


def _pack_bf16_pairs(x):
    n = x.shape[1] // 2
    hi = lax.bitcast_convert_type(x[:, :n].astype(BF16).astype(F32), jnp.int32)
    lo = lax.bitcast_convert_type(x[:, n:].astype(BF16).astype(F32), jnp.int32)
    return hi | lax.shift_right_logical(lo, 16)


def _unpack_bf16_pairs(p):
    hi = lax.bitcast_convert_type(p & jnp.int32(-65536), F32).astype(BF16)
    lo = lax.bitcast_convert_type(lax.shift_left(p, 16), F32).astype(BF16)
    return hi, lo


def _mix_kernel(att_ref, ssm_ref, xn_ref, ag_ref, sg_ref, woa_ref, wos_ref,
                g_ref, b_ref, wrh_ref, wrl_ref, rb_ref,
                x1_ref, x1p_ref, idx_ref, gate_ref, rank_ref, cnt_ref, carry_ref, before_ref):
    @pl.when(pl.program_id(0) == 0)
    def _():
        carry_ref[...] = jnp.zeros_like(carry_ref)
        tm = before_ref.shape[0]
        before = (lax.broadcasted_iota(jnp.int32, (tm, tm), 0)
                  < lax.broadcasted_iota(jnp.int32, (tm, tm), 1))
        before_ref[...] = jnp.where(before, 1.0, 0.0).astype(BF16)

    an = _rms_norm(att_ref[...].astype(F32), ag_ref[...]).astype(BF16)
    sn = _rms_norm(ssm_ref[...].astype(F32), sg_ref[...]).astype(BF16)
    mix = _dot(an, woa_ref[...]) + _dot(sn, wos_ref[...])
    x1 = _layer_norm(ALPHA * xn_ref[...] + mix, g_ref[...], b_ref[...])
    x1_ref[...] = x1
    x1p_ref[...] = _pack_bf16_pairs(x1)
    x_hi = x1.astype(BF16)
    x_lo = (x1 - x_hi.astype(F32)).astype(BF16)
    dg = functools.partial(lax.dot_general, dimension_numbers=NT_DIMS, preferred_element_type=F32)
    logits_t = dg(wrh_ref[...], x_hi) + dg(wrl_ref[...], x_hi) + dg(wrh_ref[...], x_lo)
    idx_k, gate_k, chosen = _router_gates(logits_t, rb_ref[...])
    idx_ref[...] = idx_k.astype(jnp.int32)
    gate_ref[...] = gate_k.T

    n_exp, tm = chosen.shape
    excl = _dot(chosen.astype(BF16), before_ref[...])
    rank_full = carry_ref[...] + excl
    iota = lax.broadcasted_iota(jnp.int32, (n_exp, tm), 0).astype(F32)
    ranks = [jnp.sum(jnp.where(iota == idx_k[k:k + 1, :], rank_full, 0.0), axis=0, keepdims=True)
             for k in range(TOP_K)]
    rank_ref[...] = jnp.concatenate(ranks, axis=0).astype(jnp.int32)
    total = carry_ref[...] + jnp.sum(chosen, axis=1, keepdims=True)
    carry_ref[...] = total
    cnt_ref[...] = jnp.broadcast_to(total, cnt_ref.shape)


def _positions_kernel(off_ref, idx_ref, rank_ref, pos_ref):
    idx = idx_ref[...]
    base = jnp.zeros(idx.shape, jnp.int32)
    for e in range(N_EXPERTS):
        base = jnp.where(idx == e, off_ref[e], base)
    pos_ref[...] = rank_ref[...] + base


def _swiglu(x_hi, x_lo, wg_ref, wu_ref, wd_ref):
    half = x_hi.shape[1]
    hg = _dot(x_hi, wg_ref[:half, :]) + _dot(x_lo, wg_ref[half:, :])
    hu = _dot(x_hi, wu_ref[:half, :]) + _dot(x_lo, wu_ref[half:, :])
    h = jax.nn.silu(hg) * hu
    return _dot(h.astype(BF16), wd_ref[...])


XS_RING = 3


def _experts_kernel(te_ref, valid_ref, xs_hbm, wg_ref, wu_ref, wd_ref, ys_ref,
                    wgb_ref, wub_ref, wdb_ref, xbuf_ref, xsem):
    i = pl.program_id(0)
    n_steps = pl.num_programs(0)
    valid = valid_ref[i]
    tr = xbuf_ref.shape[1]

    def fetch(j):
        slot = j % XS_RING
        return pltpu.make_async_copy(xs_hbm.at[pl.ds(pl.multiple_of(j * tr, tr), tr)],
                                     xbuf_ref.at[slot], xsem.at[slot])

    @pl.when(i == 0)
    def _():
        for j in range(XS_RING - 1):
            fetch(j).start()

    @pl.when(i + XS_RING - 1 < n_steps)
    def _():
        fetch(i + XS_RING - 1).start()

    fetch(i).wait()
    xs_ref = xbuf_ref.at[i % XS_RING]

    @pl.when((i == 0) | (te_ref[i] != te_ref[jnp.maximum(i - 1, 0)]))
    def _():
        wgb_ref[...] = wg_ref[0].astype(BF16)
        wub_ref[...] = wu_ref[0].astype(BF16)
        wdb_ref[...] = wd_ref[0].astype(BF16)

    @pl.when(valid > 0)
    def _():
        rows = lax.broadcasted_iota(jnp.int32, xs_ref.shape, 0)
        x_hi, x_lo = _unpack_bf16_pairs(jnp.where(rows < valid, xs_ref[...], 0))
        ys_ref[...] = _pack_bf16_pairs(_swiglu(x_hi, x_lo, wgb_ref, wub_ref, wdb_ref))

    @pl.when(valid == 0)
    def _():
        ys_ref[...] = jnp.zeros_like(ys_ref)


def _combine_kernel(yg_ref, gate_ref, x1_ref, wsg_ref, wsu_ref, wsd_ref, g_ref, b_ref, o_ref):
    x1 = x1_ref[...]
    half = x1.shape[1] // 2
    acc = _swiglu(x1[:, :half].astype(BF16), x1[:, half:].astype(BF16), wsg_ref, wsu_ref, wsd_ref)
    gates = gate_ref[...]
    acc_hi, acc_lo = acc[:, :half], acc[:, half:]
    for k in range(TOP_K):
        y_hi, y_lo = _unpack_bf16_pairs(yg_ref[k])
        gk = gates[:, k:k + 1]
        acc_hi = acc_hi + gk * y_hi.astype(F32)
        acc_lo = acc_lo + gk * y_lo.astype(F32)
    ffn = jnp.concatenate([acc_hi, acc_lo], axis=1)
    o_ref[...] = _layer_norm(ALPHA * x1 + ffn, g_ref[...], b_ref[...])


def _combine_into_kernel(prev_ref, *refs):
    del prev_ref
    _combine_kernel(*refs)


SC_CORES = 2
SC_SUBCORES = 16
SC_WORKERS = SC_CORES * SC_SUBCORES
SC_CHUNK = 64


def _sc_mesh():
    return plsc.VectorSubcoreMesh(core_axis_name="c", subcore_axis_name="s")


def _sc_gather_rows(table, idx):
    n = idx.shape[0]
    d = table.shape[1]
    assert n % (SC_WORKERS * 2 * SC_CHUNK) == 0, n
    per_w = n // SC_WORKERS
    n_ch = per_w // SC_CHUNK

    @functools.partial(
        pl.kernel, mesh=_sc_mesh(),
        out_type=jax.ShapeDtypeStruct((n, d), table.dtype),
        scratch_types=[pltpu.VMEM((n_ch, SC_CHUNK), jnp.int32),
                       pltpu.VMEM((2, SC_CHUNK, d), table.dtype),
                       pltpu.SemaphoreType.DMA((2,)),
                       pltpu.SemaphoreType.DMA((2,))],
    )
    def k(table_hbm, idx_hbm, out_hbm, idx_v, buf, gsem, osem):
        wid = lax.axis_index("s") * SC_CORES + lax.axis_index("c")
        base = wid * per_w
        pltpu.sync_copy(idx_hbm.at[wid], idx_v)

        def gather(c, b):
            return pltpu.make_async_copy(table_hbm.at[idx_v.at[c]], buf.at[b], gsem.at[b])

        def put(c, b):
            return pltpu.make_async_copy(buf.at[b], out_hbm.at[pl.ds(base + c * SC_CHUNK, SC_CHUNK)],
                                         osem.at[b])

        gather(0, 0).start()

        @pl.loop(0, n_ch, step=2)
        def _(c):
            for b in range(2):
                cc = c + b
                gather(cc, b).wait()

                @pl.when(cc + 1 < n_ch)
                def _():
                    @pl.when(cc >= 1)
                    def _():
                        put(cc - 1, 1 - b).wait()
                    gather(cc + 1, 1 - b).start()

                put(cc, b).start()

        put(n_ch - 2, 0).wait()
        put(n_ch - 1, 1).wait()

    return k(table, idx.reshape(SC_WORKERS, n_ch, SC_CHUNK))


def _sc_scatter_rows(x, pos, n_out):
    t, d = x.shape
    kk = pos.shape[0]
    assert t % (SC_WORKERS * 2 * SC_CHUNK) == 0, t
    per_w = t // SC_WORKERS
    n_ch = per_w // SC_CHUNK
    pos_w = pos.reshape(kk, SC_WORKERS, n_ch, SC_CHUNK).transpose(1, 2, 0, 3)
    pos_w = pos_w.reshape(SC_WORKERS, n_ch * kk, SC_CHUNK)

    @functools.partial(
        pl.kernel, mesh=_sc_mesh(),
        out_type=jax.ShapeDtypeStruct((n_out, d), x.dtype),
        scratch_types=[pltpu.VMEM((n_ch * kk, SC_CHUNK), jnp.int32),
                       pltpu.VMEM((2, SC_CHUNK, d), x.dtype),
                       pltpu.SemaphoreType.DMA((2,)),
                       pltpu.SemaphoreType.DMA((2,))],
    )
    def k(x_hbm, pos_hbm, out_hbm, idx_v, buf, isem, osem):
        wid = lax.axis_index("s") * SC_CORES + lax.axis_index("c")
        base = wid * per_w
        pltpu.sync_copy(pos_hbm.at[wid], idx_v)

        def get(c, b):
            return pltpu.make_async_copy(x_hbm.at[pl.ds(base + c * SC_CHUNK, SC_CHUNK)], buf.at[b],
                                         isem.at[b])

        def put(c, j, b):
            return pltpu.make_async_copy(buf.at[b], out_hbm.at[idx_v.at[c * kk + j]], osem.at[b])

        get(0, 0).start()

        @pl.loop(0, n_ch, step=2)
        def _(c):
            for b in range(2):
                cc = c + b
                get(cc, b).wait()

                @pl.when(cc + 1 < n_ch)
                def _():
                    @pl.when(cc >= 1)
                    def _():
                        for j in range(kk):
                            put(cc - 1, j, 1 - b).wait()
                    get(cc + 1, 1 - b).start()

                for j in range(kk):
                    put(cc, j, b).start()

        for j in range(kk):
            put(n_ch - 2, j, 0).wait()
        for j in range(kk):
            put(n_ch - 1, j, 1).wait()

    return k(x, pos_w)


def _row(v):
    return v.reshape(1, -1).astype(F32)


def _const_spec(shape):
    nd = len(shape)
    return pl.BlockSpec(shape, lambda *_: (0,) * nd)


def _pad_heads(w, width):
    r, h, _ = w.shape
    return jnp.pad(w, ((0, 0), (0, 0), (0, HEAD_PAD - width))).reshape(r, h * HEAD_PAD)


def _half_rotate(w):
    half = QK_ROPE // 2
    return jnp.concatenate([-w[..., half:], w[..., :half]], axis=-1)


def kernel(x, positions, ln_in_g, ln_in_b, w_in, q_norm_g, kv_norm_g, w_uq, w_ukv, lambda_re, lambda_im, log_step, b_re, b_im, c_re, c_im, d_skip, w_glu, b_glu, attn_out_g, ssm_out_g, w_o, ln1_g, ln1_b, w_router, router_bias, w_gate, w_up, w_down, ws_gate, ws_up, ws_down, ln2_g, ln2_b):
    B, S, D = x.shape
    T = B * S
    assert DEPTH == 1 and w_in.shape[0] == DEPTH
    assert S % QUERY_TILE == 0 and S % TOKEN_TILE == 0 and S % S5_TIME_TILE == 0, S
    assert T % (MOE_SLABS * TOKEN_TILE) == 0 and (T // MOE_SLABS * TOP_K) % EXPERT_ROW_TILE == 0, T
    l = 0
    ssm_width = w_glu.shape[-1]
    n_groups = ssm_width // SSM_GROUP
    n_state = n_groups * SSM_STATE
    mla_width = MLA_HEADS * V_DIM
    qk_pad = MLA_HEADS * HEAD_PAD
    cparams = functools.partial(pltpu.CompilerParams, vmem_limit_bytes=VMEM_LIMIT)

    s1, s2, s3 = Q_RANK, Q_RANK + KV_RANK, Q_RANK + KV_RANK + QK_ROPE
    wi = w_in[l]
    w_kr = wi[:, s2:s3]
    pad_rope = lambda w: jnp.pad(w, ((0, 0), (QK_NOPE, HEAD_PAD - QK_NOPE - QK_ROPE)))
    w1 = jnp.concatenate([wi[:, :s2], wi[:, s3:], pad_rope(w_kr), pad_rope(_half_rotate(w_kr))],
                         axis=1).astype(BF16)
    wq = w_uq[l]
    zeros_nope = jnp.zeros(wq.shape[:2] + (QK_NOPE,), wq.dtype)
    wq_main = _pad_heads(wq, QK_NOPE + QK_ROPE).astype(BF16)
    wq_rot = _pad_heads(jnp.concatenate([zeros_nope, _half_rotate(wq[..., QK_NOPE:])], axis=-1),
                        QK_NOPE + QK_ROPE).astype(BF16)
    wkv = w_ukv[l]
    wk = _pad_heads(wkv[..., :QK_NOPE], QK_NOPE).astype(BF16)
    wv = _pad_heads(wkv[..., QK_NOPE:], V_DIM).astype(BF16)
    half = QK_ROPE // 2
    inv_freq = ROPE_THETA ** (-jnp.arange(half, dtype=F32) / half)
    freq = jnp.concatenate([inv_freq, inv_freq]).reshape(QK_ROPE, 1)
    pos_f = positions.astype(F32).reshape(1, T)

    tm = min(TOKEN_TILE, T)
    w1_cols = w1.shape[1]
    tok = lambda width: pl.BlockSpec((tm, width), lambda i: (i, 0))
    xn, q, k, v, u = pl.pallas_call(
        functools.partial(_inproj_kernel, ssm_width=ssm_width),
        grid=(T // tm,),
        in_specs=[tok(D), pl.BlockSpec((1, tm), lambda i: (0, i)), _const_spec((1, D)), _const_spec((1, D)),
                  _const_spec((D, w1_cols)), _const_spec((1, Q_RANK)), _const_spec((1, KV_RANK)),
                  _const_spec((Q_RANK, qk_pad)), _const_spec((Q_RANK, qk_pad)),
                  _const_spec((KV_RANK, qk_pad)), _const_spec((KV_RANK, qk_pad)),
                  _const_spec((QK_ROPE, 1))],
        out_specs=[tok(D), tok(qk_pad), tok(qk_pad), tok(qk_pad), tok(ssm_width)],
        out_shape=[jax.ShapeDtypeStruct((T, D), F32), jax.ShapeDtypeStruct((T, qk_pad), BF16),
                   jax.ShapeDtypeStruct((T, qk_pad), BF16), jax.ShapeDtypeStruct((T, qk_pad), BF16),
                   jax.ShapeDtypeStruct((T, ssm_width), F32)],
        compiler_params=cparams(dimension_semantics=("parallel",)),
        name="inproj",
    )(x.reshape(T, D), pos_f, _row(ln_in_g), _row(ln_in_b), w1, _row(q_norm_g[l]), _row(kv_norm_g[l]),
      wq_main, wq_rot, wk, wv, freq)

    tq = min(QUERY_TILE, S)
    nq = S // tq
    att = pl.pallas_call(
        functools.partial(_attn_kernel, tq=tq, heads=MLA_HEADS, n_qtiles=nq),
        grid=(B, nq),
        in_specs=[pl.BlockSpec((tq, qk_pad), lambda b, i: (b * nq + i, 0)),
                  pl.BlockSpec((S, qk_pad), lambda b, i: (b, 0)),
                  pl.BlockSpec((S, qk_pad), lambda b, i: (b, 0))],
        out_specs=pl.BlockSpec((tq, mla_width), lambda b, i: (b * nq + i, 0)),
        out_shape=jax.ShapeDtypeStruct((T, mla_width), BF16),
        compiler_params=cparams(dimension_semantics=("parallel", "arbitrary")),
        name="attention",
    )(q, k, v)

    lam = lax.complex(jnp.minimum(lambda_re[l].astype(F32), -1e-4), lambda_im[l].astype(F32))
    step = jnp.exp(log_step[l].astype(F32))[:, None]
    lam_bar = jnp.exp(lam * step)
    b_bar = ((lam_bar - 1.0) / lam)[..., None] * lax.complex(b_re[l].astype(F32), b_im[l].astype(F32))
    n_slab = ssm_width // LANES
    g_per_slab = n_groups // n_slab
    slab = n_state // n_slab
    eye = jnp.eye(g_per_slab, dtype=F32)

    def expand_in(bpart):
        bt = bpart.transpose(0, 2, 1).reshape(n_slab, g_per_slab, SSM_GROUP, 1, SSM_STATE)
        return (bt * eye[None, :, None, :, None]).reshape(n_slab, LANES, slab)

    def expand_out(cpart):
        ct = cpart.transpose(0, 2, 1).reshape(n_slab, g_per_slab, SSM_STATE, 1, SSM_GROUP)
        return (ct * eye[None, :, None, :, None]).reshape(n_slab, slab, LANES)

    win = jnp.concatenate([expand_in(jnp.real(b_bar)), expand_in(jnp.imag(b_bar))],
                          axis=2).astype(BF16)
    cre = expand_out(c_re[l].astype(F32)).astype(BF16)
    cim = expand_out(-c_im[l].astype(F32)).astype(BF16)
    a_re = jnp.real(lam_bar).reshape(1, n_state)
    a_im = jnp.imag(lam_bar).reshape(1, n_state)

    lt = min(S5_TIME_TILE, S)
    ssm = pl.pallas_call(
        functools.partial(_s5_kernel, batch=B, lt=lt, n_state=n_state),
        grid=(S // lt,),
        in_specs=[pl.BlockSpec((B, lt, ssm_width), lambda t: (0, t, 0)),
                  _const_spec(win.shape), _const_spec((1, n_state)), _const_spec((1, n_state)),
                  _const_spec(cre.shape), _const_spec(cim.shape), _const_spec((1, ssm_width)),
                  _const_spec((ssm_width, ssm_width)), _const_spec((1, ssm_width))],
        out_specs=pl.BlockSpec((B, lt, ssm_width), lambda t: (0, t, 0)),
        out_shape=jax.ShapeDtypeStruct((B, S, ssm_width), BF16),
        scratch_shapes=[pltpu.VMEM((2 * n_state // LANES, B * lt, LANES), F32),
                        pltpu.VMEM((n_state // LANES, B, LANES), F32),
                        pltpu.VMEM((n_state // LANES, B, LANES), F32),
                        pltpu.VMEM((ssm_width // LANES, B * lt, LANES), F32),
                        pltpu.VMEM((B * lt, ssm_width), F32)],
        compiler_params=cparams(dimension_semantics=("arbitrary",)),
        name="s5",
    )(u.reshape(B, S, ssm_width), win, a_re, a_im, cre, cim, _row(d_skip[l]),
      w_glu[l].astype(BF16), _row(b_glu[l]))

    wo = w_o[l].astype(BF16)
    assert mla_width == ssm_width
    wr_t = w_router[l].T.astype(F32)
    wr_hi = wr_t.astype(BF16)
    wr_lo = (wr_t - wr_hi.astype(F32)).astype(BF16)
    half = D // 2
    n_slabs = MOE_SLABS
    ts = T // n_slabs
    nt = ts // tm
    ssm2 = ssm.reshape(T, ssm_width)
    rbias = router_bias[l].astype(F32).reshape(N_EXPERTS, 1)
    kt = lambda dt: jax.ShapeDtypeStruct((TOP_K, ts), dt)
    tmx = min(MIX_TILE, ts)
    ntx = ts // tmx
    k_spec = pl.BlockSpec((TOP_K, tmx), lambda i: (0, i))

    def route(s):
        tok_s = lambda width: pl.BlockSpec((tmx, width), lambda i: (i + s * ntx, 0))
        tok = lambda width: pl.BlockSpec((tmx, width), lambda i: (i, 0))
        return pl.pallas_call(
            _mix_kernel,
            grid=(ntx,),
            in_specs=[tok_s(mla_width), tok_s(ssm_width), tok_s(D), _const_spec((1, mla_width)),
                      _const_spec((1, ssm_width)), pl.BlockSpec((mla_width, D), lambda i: (0, 0)),
                      pl.BlockSpec((ssm_width, D), lambda i: (1, 0)), _const_spec((1, D)), _const_spec((1, D)),
                      _const_spec((N_EXPERTS, D)), _const_spec((N_EXPERTS, D)),
                      _const_spec((N_EXPERTS, 1))],
            out_specs=[tok(D), tok(half), k_spec, tok(TOP_K), k_spec, _const_spec((N_EXPERTS, LANES))],
            out_shape=[jax.ShapeDtypeStruct((ts, D), F32), jax.ShapeDtypeStruct((ts, half), jnp.int32),
                       kt(jnp.int32), jax.ShapeDtypeStruct((ts, TOP_K), F32), kt(jnp.int32),
                       jax.ShapeDtypeStruct((N_EXPERTS, LANES), F32)],
            scratch_shapes=[pltpu.VMEM((N_EXPERTS, 1), F32), pltpu.VMEM((tmx, tmx), BF16)],
            compiler_params=cparams(dimension_semantics=("arbitrary",)),
            name="mix_router",
        )(att, ssm2, xn, _row(attn_out_g[l]), _row(ssm_out_g[l]), wo, wo,
          _row(ln1_g[l]), _row(ln1_b[l]), wr_hi, wr_lo, rbias)

    tr = EXPERT_ROW_TILE
    n_tiles = (ts * TOP_K) // tr + N_EXPERTS
    n_rows = n_tiles * tr
    tp = min(POSITIONS_TILE, ts)

    def dispatch(x1p, idx_k, rank_k, counts):
        cnt = counts[:, 0].astype(jnp.int32)
        tiles_e = (cnt + tr - 1) // tr
        tile_end = jnp.cumsum(tiles_e)
        tile_start = tile_end - tiles_e
        tile_ids = jnp.arange(n_tiles, dtype=jnp.int32)
        tile_expert = jnp.sum((tile_end[None, :] <= tile_ids[:, None]).astype(jnp.int32), axis=1)
        tile_expert = jnp.minimum(tile_expert, N_EXPERTS - 1)
        owner = (tile_start[None, :] <= tile_ids[:, None]) & (tile_ids[:, None] < tile_end[None, :])
        left = jnp.sum(jnp.where(owner, cnt[None, :] - (tile_ids[:, None] - tile_start[None, :]) * tr, 0),
                       axis=1)
        tile_valid = jnp.clip(left, 0, tr).astype(jnp.int32)
        pos = pl.pallas_call(
            _positions_kernel,
            grid_spec=pltpu.PrefetchScalarGridSpec(
                num_scalar_prefetch=1, grid=(ts // tp,),
                in_specs=[pl.BlockSpec((TOP_K, tp), lambda i, off: (0, i)),
                          pl.BlockSpec((TOP_K, tp), lambda i, off: (0, i))],
                out_specs=pl.BlockSpec((TOP_K, tp), lambda i, off: (0, i))),
            out_shape=kt(jnp.int32),
            name="positions",
        )((tile_start * tr).astype(jnp.int32), idx_k, rank_k)
        return _sc_scatter_rows(x1p, pos, n_rows), pos, tile_expert, tile_valid

    wg, wu, wd = w_gate[l], w_up[l], w_down[l]
    ff = wg.shape[-1]

    def experts(xs, tile_expert, tile_valid):
        return pl.pallas_call(
            _experts_kernel,
            grid_spec=pltpu.PrefetchScalarGridSpec(
                num_scalar_prefetch=2, grid=(n_tiles,),
                in_specs=[pl.BlockSpec(memory_space=pl.ANY),
                          pl.BlockSpec((1, D, ff), lambda i, te, tv: (te[i], 0, 0)),
                          pl.BlockSpec((1, D, ff), lambda i, te, tv: (te[i], 0, 0)),
                          pl.BlockSpec((1, ff, D), lambda i, te, tv: (te[i], 0, 0))],
                out_specs=pl.BlockSpec((tr, half), lambda i, te, tv: (i, 0)),
                scratch_shapes=[pltpu.VMEM((D, ff), BF16), pltpu.VMEM((D, ff), BF16),
                                pltpu.VMEM((ff, D), BF16),
                                pltpu.VMEM((XS_RING, tr, half), jnp.int32),
                                pltpu.SemaphoreType.DMA((XS_RING,))]),
            out_shape=jax.ShapeDtypeStruct((n_rows, half), jnp.int32),
            compiler_params=cparams(dimension_semantics=("arbitrary",)),
            name="experts",
        )(tile_expert, tile_valid, xs, wg, wu, wd)

    shared = (ws_gate[l].astype(BF16), ws_up[l].astype(BF16), ws_down[l].astype(BF16))

    n_chunks = COMBINE_CHUNKS
    tc = ts // n_chunks
    ntc = nt // n_chunks

    def combine(s, c, out_so_far, yg, gate_k, x1):
        in_slab = lambda width: pl.BlockSpec((tm, width), lambda i: (i + c * ntc, 0))
        specs = [pl.BlockSpec((TOP_K, tm, half), lambda i: (0, i, 0)), in_slab(TOP_K), in_slab(D),
                 _const_spec((D, ff)), _const_spec((D, ff)), _const_spec((ff, D)),
                 _const_spec((1, D)), _const_spec((1, D))]
        args = (yg.reshape(TOP_K, tc, half), gate_k, x1, *shared, _row(ln2_g[l]), _row(ln2_b[l]))
        body, aliases = _combine_kernel, {}
        if out_so_far is not None:
            specs = [pl.BlockSpec(memory_space=pl.ANY)] + specs
            args = (out_so_far,) + args
            body, aliases = _combine_into_kernel, {0: 0}
        return pl.pallas_call(
            body,
            grid=(ntc,),
            in_specs=specs,
            out_specs=pl.BlockSpec((tm, D), lambda i: (i + s * nt + c * ntc, 0)),
            out_shape=jax.ShapeDtypeStruct((T, D), F32),
            input_output_aliases=aliases,
            compiler_params=cparams(dimension_semantics=("parallel",)),
            name="combine",
        )(*args)

    routed, moved = [], []
    for s in range(n_slabs):
        x1, x1p, idx_k, gate_k, rank_k, counts = route(s)
        routed.append((x1, gate_k))
        moved.append(dispatch(x1p, idx_k, rank_k, counts))
    gathered = []
    for xs, pos, tile_expert, tile_valid in moved:
        ys = experts(xs, tile_expert, tile_valid)
        gathered.append([_sc_gather_rows(ys, pos[:, c * tc:(c + 1) * tc].reshape(TOP_K * tc))
                         for c in range(n_chunks)])
    out = None
    for s in range(n_slabs):
        x1, gate_k = routed[s]
        for c in range(n_chunks):
            out = combine(s, c, out, gathered[s][c], gate_k, x1)
    return out.reshape(B, S, D)
```

```python
import functools

import jax
import jax.numpy as jnp
from jax import lax
from jax.experimental import pallas as pl
from jax.experimental.pallas import tpu as pltpu
from jax.experimental.pallas import tpu_sc as plsc

CHUNK = 64
MLA_HEADS = 8
QK_NOPE = 64
QK_ROPE = 32
V_DIM = 64
Q_RANK = 256
KV_RANK = 128
ROPE_THETA = 10000.0
SSM_GROUP = 16
SSM_STATE = 64
N_EXPERTS = 64
TOP_K = 8
N_GROUPS = 8
TOP_GROUPS = 4
ROUTED_SCALE = 2.5
DEPTH = 1
ALPHA = (2.0 * DEPTH) ** 0.25
EPS = 1e-5
LOG2_E = 1.4426950408889634

LANES = 128
HEAD_PAD = LANES
VMEM_LIMIT = 56 * 1024 * 1024
TOKEN_TILE = 512
MIX_TILE = 1024
QUERY_TILE = 256
S5_TIME_TILE = 64
POSITIONS_TILE = 2048
EXPERT_ROW_TILE = 1024
MOE_SLABS = 1
COMBINE_CHUNKS = 4

BF16 = jnp.bfloat16
F32 = jnp.float32
NT_DIMS = (((1,), (1,)), ((), ()))


def _dot(a, b):
    return jnp.dot(a, b, preferred_element_type=F32)


def _layer_norm(x, g, b):
    mu = jnp.mean(x, axis=-1, keepdims=True)
    xc = x - mu
    var = jnp.mean(xc * xc, axis=-1, keepdims=True)
    return xc * lax.rsqrt(var + EPS) * g + b


def _rms_norm(x, g):
    return x * lax.rsqrt(jnp.mean(x * x, axis=-1, keepdims=True) + EPS) * g


def _inproj_kernel(x_ref, pos_ref, lng_ref, lnb_ref, w1_ref, qg_ref, kvg_ref,
                   wq_ref, wqr_ref, wk_ref, wv_ref, freq_ref,
                   xn_ref, q_ref, k_ref, v_ref, u_ref, *, ssm_width):
    xn = _layer_norm(x_ref[...], lng_ref[...], lnb_ref[...])
    xn_ref[...] = xn
    h = _dot(xn.astype(BF16), w1_ref[...])
    o1 = Q_RANK
    o2 = o1 + KV_RANK
    o3 = o2 + ssm_width
    o4 = o3 + HEAD_PAD
    cq = h[:, :o1]
    ckv = h[:, o1:o2]
    u_ref[...] = h[:, o2:o3]
    kr_raw = h[:, o3:o4]
    kr_rot = h[:, o4:o4 + HEAD_PAD]
    cqn = _rms_norm(cq, qg_ref[...]).astype(BF16)
    ckvn = _rms_norm(ckv, kvg_ref[...]).astype(BF16)

    tm = x_ref.shape[0]
    ang_t = freq_ref[...] * pos_ref[...]

    def to_token_rows(t):
        padded = jnp.concatenate([jnp.zeros((QK_NOPE, tm), F32), t,
                                  jnp.zeros((HEAD_PAD - QK_NOPE - QK_ROPE, tm), F32)], axis=0)
        return padded.T

    c = to_token_rows(jnp.cos(ang_t))
    s = to_token_rows(jnp.sin(ang_t))
    lane = lax.broadcasted_iota(jnp.int32, (1, HEAD_PAD), 1)
    scale = (QK_NOPE + QK_ROPE) ** -0.5 * LOG2_E
    cos1 = (c + jnp.where(lane < QK_NOPE, 1.0, 0.0)) * scale
    sin1 = s * scale
    cos_t = jnp.concatenate([cos1] * MLA_HEADS, axis=1)
    sin_t = jnp.concatenate([sin1] * MLA_HEADS, axis=1)
    q = _dot(cqn, wq_ref[...]) * cos_t + _dot(cqn, wqr_ref[...]) * sin_t
    q_ref[...] = q.astype(BF16)

    kr = kr_raw * c + kr_rot * s
    k = _dot(ckvn, wk_ref[...]) + jnp.concatenate([kr] * MLA_HEADS, axis=1)
    k_ref[...] = k.astype(BF16)
    ones_col = jnp.concatenate([jnp.where(lane == V_DIM, 1.0, 0.0)] * MLA_HEADS, axis=1)
    v_ref[...] = (_dot(ckvn, wv_ref[...]) + ones_col).astype(BF16)


def _attn_kernel(q_ref, k_ref, v_ref, o_ref, *, tq, heads, n_qtiles):
    qi = pl.program_id(1)
    row_chunk = lax.broadcasted_iota(jnp.int32, (tq, tq), 0) // CHUNK
    col_chunk = lax.broadcasted_iota(jnp.int32, (tq, tq), 1) // CHUNK
    diag_mask = row_chunk >= col_chunk

    def tile(n_blocks):
        keys = n_blocks * tq
        cols = [slice(h * HEAD_PAD, (h + 1) * HEAD_PAD) for h in range(heads)]

        def mask_diag(s):
            s_diag = jnp.where(diag_mask, s[:, keys - tq:], -jnp.inf)
            return s_diag if n_blocks == 1 else jnp.concatenate([s[:, :keys - tq], s_diag], axis=1)

        ss = [lax.dot_general(q_ref[:, c], k_ref[:keys, c], NT_DIMS, preferred_element_type=F32)
              for c in cols]
        ss = [mask_diag(s) for s in ss]
        ms = [jnp.max(s, axis=-1, keepdims=True) for s in ss]
        ps = [jnp.exp2(s - m).astype(BF16) for s, m in zip(ss, ms)]
        accs = [_dot(p, v_ref[:keys, c]) for p, c in zip(ps, cols)]
        outs = [acc[:, :V_DIM] / acc[:, V_DIM:V_DIM + 1] for acc in accs]
        o_ref[...] = jnp.concatenate(outs, axis=1).astype(o_ref.dtype)

    for c in range(n_qtiles):
        pl.when(qi == c)(functools.partial(tile, c + 1))


def _s5_kernel(u_ref, win_ref, are_ref, aim_ref, cre_ref, cim_ref, dskip_ref,
               wglu_ref, bglu_ref, o_ref, vx_ref, hre_ref, him_ref, io_ref, utm_ref, *,
               batch, lt, n_state):
    ti = pl.program_id(0)

    @pl.when(ti == 0)
    def _():
        hre_ref[...] = jnp.zeros_like(hre_ref)
        him_ref[...] = jnp.zeros_like(him_ref)

    width = u_ref.shape[-1]
    n_slab = width // LANES
    n_tiles = n_state // LANES
    slab_tiles = n_tiles // n_slab
    for b in range(batch):
        for c in range(n_slab):
            io_ref[c, b * lt:(b + 1) * lt, :] = u_ref[b, :, c * LANES:(c + 1) * LANES]

    def slab_tiles_of(j):
        return range(j * slab_tiles, (j + 1) * slab_tiles)

    def expand(j):
        for t in range(lt):
            utm_ref[t * batch:(t + 1) * batch, j * LANES:(j + 1) * LANES] = (
                io_ref[j, pl.ds(t, batch, stride=lt), :])
        ub = utm_ref[:, j * LANES:(j + 1) * LANES].astype(BF16)
        vj = _dot(ub, win_ref[j])
        for i, c in enumerate(slab_tiles_of(j)):
            vx_ref[c] = vj[:, i * LANES:(i + 1) * LANES]
            vx_ref[n_tiles + c] = vj[:, (slab_tiles + i) * LANES:(slab_tiles + i + 1) * LANES]

    def scan(j):
        tiles = slab_tiles_of(j)
        ar = [jnp.broadcast_to(are_ref[:, c * LANES:(c + 1) * LANES], (batch, LANES)) for c in tiles]
        ai = [jnp.broadcast_to(aim_ref[:, c * LANES:(c + 1) * LANES], (batch, LANES)) for c in tiles]
        hr = [hre_ref[c] for c in tiles]
        hi = [him_ref[c] for c in tiles]
        for t in range(lt):
            rows = slice(t * batch, (t + 1) * batch)
            for n, c in enumerate(tiles):
                nr = ar[n] * hr[n] - ai[n] * hi[n] + vx_ref[c, rows, :]
                ni = ar[n] * hi[n] + ai[n] * hr[n] + vx_ref[n_tiles + c, rows, :]
                vx_ref[c, rows, :] = nr
                vx_ref[n_tiles + c, rows, :] = ni
                hr[n], hi[n] = nr, ni
        for n, c in enumerate(tiles):
            hre_ref[c] = hr[n]
            him_ref[c] = hi[n]

    def project(j):
        tiles = slab_tiles_of(j)
        xr = jnp.concatenate([vx_ref[c].astype(BF16) for c in tiles], axis=1)
        xi = jnp.concatenate([vx_ref[n_tiles + c].astype(BF16) for c in tiles], axis=1)
        return _dot(xr, cre_ref[j]) + _dot(xi, cim_ref[j])

    ys = [None] * n_slab
    for j in range(n_slab + 2):
        if j < n_slab:
            expand(j)
        if 1 <= j <= n_slab:
            scan(j - 1)
        if j >= 2:
            ys[j - 2] = project(j - 2)
    y = jnp.concatenate(ys, axis=1) + dskip_ref[...] * utm_ref[...]
    y = jax.nn.gelu(y)
    z = _dot(y.astype(BF16), wglu_ref[...]) + bglu_ref[...]
    out = y * jax.nn.sigmoid(z)
    for c in range(n_slab):
        io_ref[c] = out[:, c * LANES:(c + 1) * LANES]
    for b in range(batch):
        for c in range(n_slab):
            o_ref[b, :, c * LANES:(c + 1) * LANES] = io_ref[c, pl.ds(b, lt, stride=batch), :].astype(o_ref.dtype)


def _router_gates(logits_t, rbias):
    n_exp, tm = logits_t.shape
    per_group = n_exp // N_GROUPS
    scores = jax.nn.sigmoid(logits_t)
    sel = scores + rbias
    neg_inf = -jnp.inf
    sub_iota = lax.broadcasted_iota(jnp.int32, (per_group, tm), 0).astype(F32)
    group_score = []
    for g in range(N_GROUPS):
        sg = sel[g * per_group:(g + 1) * per_group, :]
        m1 = jnp.max(sg, axis=0, keepdims=True)
        first = jnp.min(jnp.where(sg == m1, sub_iota, float(per_group)), axis=0, keepdims=True)
        m2 = jnp.max(jnp.where(sub_iota == first, neg_inf, sg), axis=0, keepdims=True)
        group_score.append(m1 + m2)
    masked = []
    for g in range(N_GROUPS):
        rank = jnp.zeros((1, tm), F32)
        for g2 in range(N_GROUPS):
            if g2 == g:
                continue
            ahead = (group_score[g2] >= group_score[g]) if g2 < g else (group_score[g2] > group_score[g])
            rank = rank + jnp.where(ahead, 1.0, 0.0)
        keep = rank < float(TOP_GROUPS)
        masked.append(jnp.where(keep, sel[g * per_group:(g + 1) * per_group, :], neg_inf))
    cur = jnp.concatenate(masked, axis=0)
    iota = lax.broadcasted_iota(jnp.int32, (n_exp, tm), 0).astype(F32)
    chosen = jnp.zeros((n_exp, tm), F32)
    picks, weights = [], []
    for _ in range(TOP_K):
        m = jnp.max(cur, axis=0, keepdims=True)
        idx = jnp.min(jnp.where(cur == m, iota, float(n_exp)), axis=0, keepdims=True)
        pick = iota == idx
        chosen = jnp.where(pick, 1.0, chosen)
        cur = jnp.where(pick, neg_inf, cur)
        picks.append(idx)
        weights.append(jnp.sum(jnp.where(pick, scores, 0.0), axis=0, keepdims=True))
    idx_k = jnp.concatenate(picks, axis=0)
    w_k = jnp.concatenate(weights, axis=0)
    gate_k = w_k / jnp.sum(w_k, axis=0, keepdims=True) * ROUTED_SCALE
    return idx_k, gate_k, chosen


def _pack_bf16_pairs(x):
    n = x.shape[1] // 2
    hi = lax.bitcast_convert_type(x[:, :n].astype(BF16).astype(F32), jnp.int32)
    lo = lax.bitcast_convert_type(x[:, n:].astype(BF16).astype(F32), jnp.int32)
    return hi | lax.shift_right_logical(lo, 16)


def _unpack_bf16_pairs(p):
    hi = lax.bitcast_convert_type(p & jnp.int32(-65536), F32).astype(BF16)
    lo = lax.bitcast_convert_type(lax.shift_left(p, 16), F32).astype(BF16)
    return hi, lo


def _mix_kernel(att_ref, ssm_ref, xn_ref, ag_ref, sg_ref, woa_ref, wos_ref,
                g_ref, b_ref, wrh_ref, wrl_ref, rb_ref,
                x1_ref, x1p_ref, idx_ref, gate_ref, rank_ref, cnt_ref, carry_ref, before_ref):
    @pl.when(pl.program_id(0) == 0)
    def _():
        carry_ref[...] = jnp.zeros_like(carry_ref)
        tm = before_ref.shape[0]
        before = (lax.broadcasted_iota(jnp.int32, (tm, tm), 0)
                  < lax.broadcasted_iota(jnp.int32, (tm, tm), 1))
        before_ref[...] = jnp.where(before, 1.0, 0.0).astype(BF16)

    an = _rms_norm(att_ref[...].astype(F32), ag_ref[...]).astype(BF16)
    sn = _rms_norm(ssm_ref[...].astype(F32), sg_ref[...]).astype(BF16)
    mix = _dot(an, woa_ref[...]) + _dot(sn, wos_ref[...])
    x1 = _layer_norm(ALPHA * xn_ref[...] + mix, g_ref[...], b_ref[...])
    x1_ref[...] = x1
    x1p_ref[...] = _pack_bf16_pairs(x1)
    x_hi = x1.astype(BF16)
    x_lo = (x1 - x_hi.astype(F32)).astype(BF16)
    dg = functools.partial(lax.dot_general, dimension_numbers=NT_DIMS, preferred_element_type=F32)
    logits_t = dg(wrh_ref[...], x_hi) + dg(wrl_ref[...], x_hi) + dg(wrh_ref[...], x_lo)
    idx_k, gate_k, chosen = _router_gates(logits_t, rb_ref[...])
    idx_ref[...] = idx_k.astype(jnp.int32)
    gate_ref[...] = gate_k.T

    n_exp, tm = chosen.shape
    excl = _dot(chosen.astype(BF16), before_ref[...])
    rank_full = carry_ref[...] + excl
    iota = lax.broadcasted_iota(jnp.int32, (n_exp, tm), 0).astype(F32)
    ranks = [jnp.sum(jnp.where(iota == idx_k[k:k + 1, :], rank_full, 0.0), axis=0, keepdims=True)
             for k in range(TOP_K)]
    rank_ref[...] = jnp.concatenate(ranks, axis=0).astype(jnp.int32)
    total = carry_ref[...] + jnp.sum(chosen, axis=1, keepdims=True)
    carry_ref[...] = total
    cnt_ref[...] = jnp.broadcast_to(total, cnt_ref.shape)


def _positions_kernel(off_ref, idx_ref, rank_ref, pos_ref):
    idx = idx_ref[...]
    base = jnp.zeros(idx.shape, jnp.int32)
    for e in range(N_EXPERTS):
        base = jnp.where(idx == e, off_ref[e], base)
    pos_ref[...] = rank_ref[...] + base


def _swiglu(x_hi, x_lo, wg_ref, wu_ref, wd_ref):
    half = x_hi.shape[1]
    hg = _dot(x_hi, wg_ref[:half, :]) + _dot(x_lo, wg_ref[half:, :])
    hu = _dot(x_hi, wu_ref[:half, :]) + _dot(x_lo, wu_ref[half:, :])
    h = jax.nn.silu(hg) * hu
    return _dot(h.astype(BF16), wd_ref[...])


XS_RING = 3


def _experts_kernel(te_ref, valid_ref, run_ref, next_ref, xs_hbm, wg_hbm, wu_hbm, wd_hbm, ys_ref,
                    wgb_ref, wub_ref, wdb_ref, xbuf_ref, xsem, wgf_ref, wuf_ref, wdf_ref, wsem):
    i = pl.program_id(0)
    n_steps = pl.num_programs(0)
    valid = valid_ref[i]
    tr = xbuf_ref.shape[1]

    def fetch(j):
        slot = j % XS_RING
        return pltpu.make_async_copy(xs_hbm.at[pl.ds(pl.multiple_of(j * tr, tr), tr)],
                                     xbuf_ref.at[slot], xsem.at[slot])

    @pl.when(i == 0)
    def _():
        for j in range(XS_RING - 1):
            fetch(j).start()

    @pl.when(i + XS_RING - 1 < n_steps)
    def _():
        fetch(i + XS_RING - 1).start()

    fetch(i).wait()
    xs_ref = xbuf_ref.at[i % XS_RING]

    def weight_copies(e, slot):
        return [pltpu.make_async_copy(src.at[e], dst.at[slot], wsem.at[slot, n])
                for n, (src, dst) in enumerate(((wg_hbm, wgf_ref), (wu_hbm, wuf_ref), (wd_hbm, wdf_ref)))]

    @pl.when(i == 0)
    def _():
        for cp in weight_copies(te_ref[0], 0):
            cp.start()

    @pl.when((i == 0) | (te_ref[i] != te_ref[jnp.maximum(i - 1, 0)]))
    def _():
        slot = run_ref[i] % 2
        for cp in weight_copies(te_ref[i], slot):
            cp.wait()
        wgb_ref[...] = wgf_ref[slot].astype(BF16)
        wub_ref[...] = wuf_ref[slot].astype(BF16)
        wdb_ref[...] = wdf_ref[slot].astype(BF16)

        @pl.when(next_ref[i] >= 0)
        def _():
            for cp in weight_copies(next_ref[i], 1 - slot):
                cp.start()

    @pl.when(valid > 0)
    def _():
        rows = lax.broadcasted_iota(jnp.int32, xs_ref.shape, 0)
        x_hi, x_lo = _unpack_bf16_pairs(jnp.where(rows < valid, xs_ref[...], 0))
        ys_ref[...] = _pack_bf16_pairs(_swiglu(x_hi, x_lo, wgb_ref, wub_ref, wdb_ref))

    @pl.when(valid == 0)
    def _():
        ys_ref[...] = jnp.zeros_like(ys_ref)


def _combine_kernel(yg_ref, gate_ref, x1_ref, wsg_ref, wsu_ref, wsd_ref, g_ref, b_ref, o_ref):
    x1 = x1_ref[...]
    half = x1.shape[1] // 2
    acc = _swiglu(x1[:, :half].astype(BF16), x1[:, half:].astype(BF16), wsg_ref, wsu_ref, wsd_ref)
    gates = gate_ref[...]
    acc_hi, acc_lo = acc[:, :half], acc[:, half:]
    for k in range(TOP_K):
        y_hi, y_lo = _unpack_bf16_pairs(yg_ref[k])
        gk = gates[:, k:k + 1]
        acc_hi = acc_hi + gk * y_hi.astype(F32)
        acc_lo = acc_lo + gk * y_lo.astype(F32)
    ffn = jnp.concatenate([acc_hi, acc_lo], axis=1)
    o_ref[...] = _layer_norm(ALPHA * x1 + ffn, g_ref[...], b_ref[...])


def _combine_into_kernel(prev_ref, *refs):
    del prev_ref
    _combine_kernel(*refs)


SC_CORES = 2
SC_SUBCORES = 16
SC_WORKERS = SC_CORES * SC_SUBCORES
SC_CHUNK = 64


def _sc_mesh():
    return plsc.VectorSubcoreMesh(core_axis_name="c", subcore_axis_name="s")


def _sc_gather_rows(table, idx):
    n = idx.shape[0]
    d = table.shape[1]
    assert n % (SC_WORKERS * 2 * SC_CHUNK) == 0, n
    per_w = n // SC_WORKERS
    n_ch = per_w // SC_CHUNK

    @functools.partial(
        pl.kernel, mesh=_sc_mesh(),
        out_type=jax.ShapeDtypeStruct((n, d), table.dtype),
        scratch_types=[pltpu.VMEM((n_ch, SC_CHUNK), jnp.int32),
                       pltpu.VMEM((2, SC_CHUNK, d), table.dtype),
                       pltpu.SemaphoreType.DMA((2,)),
                       pltpu.SemaphoreType.DMA((2,))],
    )
    def k(table_hbm, idx_hbm, out_hbm, idx_v, buf, gsem, osem):
        wid = lax.axis_index("s") * SC_CORES + lax.axis_index("c")
        base = wid * per_w
        pltpu.sync_copy(idx_hbm.at[wid], idx_v)

        def gather(c, b):
            return pltpu.make_async_copy(table_hbm.at[idx_v.at[c]], buf.at[b], gsem.at[b])

        def put(c, b):
            return pltpu.make_async_copy(buf.at[b], out_hbm.at[pl.ds(base + c * SC_CHUNK, SC_CHUNK)],
                                         osem.at[b])

        gather(0, 0).start()

        @pl.loop(0, n_ch, step=2)
        def _(c):
            for b in range(2):
                cc = c + b
                gather(cc, b).wait()

                @pl.when(cc + 1 < n_ch)
                def _():
                    @pl.when(cc >= 1)
                    def _():
                        put(cc - 1, 1 - b).wait()
                    gather(cc + 1, 1 - b).start()

                put(cc, b).start()

        put(n_ch - 2, 0).wait()
        put(n_ch - 1, 1).wait()

    return k(table, idx.reshape(SC_WORKERS, n_ch, SC_CHUNK))


def _sc_scatter_rows(x, pos, n_out):
    t, d = x.shape
    kk = pos.shape[0]
    assert t % (SC_WORKERS * 2 * SC_CHUNK) == 0, t
    per_w = t // SC_WORKERS
    n_ch = per_w // SC_CHUNK
    pos_w = pos.reshape(kk, SC_WORKERS, n_ch, SC_CHUNK).transpose(1, 2, 0, 3)
    pos_w = pos_w.reshape(SC_WORKERS, n_ch * kk, SC_CHUNK)

    @functools.partial(
        pl.kernel, mesh=_sc_mesh(),
        out_type=jax.ShapeDtypeStruct((n_out, d), x.dtype),
        scratch_types=[pltpu.VMEM((n_ch * kk, SC_CHUNK), jnp.int32),
                       pltpu.VMEM((2, SC_CHUNK, d), x.dtype),
                       pltpu.SemaphoreType.DMA((2,)),
                       pltpu.SemaphoreType.DMA((2,))],
    )
    def k(x_hbm, pos_hbm, out_hbm, idx_v, buf, isem, osem):
        wid = lax.axis_index("s") * SC_CORES + lax.axis_index("c")
        base = wid * per_w
        pltpu.sync_copy(pos_hbm.at[wid], idx_v)

        def get(c, b):
            return pltpu.make_async_copy(x_hbm.at[pl.ds(base + c * SC_CHUNK, SC_CHUNK)], buf.at[b],
                                         isem.at[b])

        def put(c, j, b):
            return pltpu.make_async_copy(buf.at[b], out_hbm.at[idx_v.at[c * kk + j]], osem.at[b])

        get(0, 0).start()

        @pl.loop(0, n_ch, step=2)
        def _(c):
            for b in range(2):
                cc = c + b
                get(cc, b).wait()

                @pl.when(cc + 1 < n_ch)
                def _():
                    @pl.when(cc >= 1)
                    def _():
                        for j in range(kk):
                            put(cc - 1, j, 1 - b).wait()
                    get(cc + 1, 1 - b).start()

                for j in range(kk):
                    put(cc, j, b).start()

        for j in range(kk):
            put(n_ch - 2, j, 0).wait()
        for j in range(kk):
            put(n_ch - 1, j, 1).wait()

    return k(x, pos_w)


def _row(v):
    return v.reshape(1, -1).astype(F32)


def _const_spec(shape):
    nd = len(shape)
    return pl.BlockSpec(shape, lambda *_: (0,) * nd)


def _pad_heads(w, width):
    r, h, _ = w.shape
    return jnp.pad(w, ((0, 0), (0, 0), (0, HEAD_PAD - width))).reshape(r, h * HEAD_PAD)


def _half_rotate(w):
    half = QK_ROPE // 2
    return jnp.concatenate([-w[..., half:], w[..., :half]], axis=-1)


def kernel(x, positions, ln_in_g, ln_in_b, w_in, q_norm_g, kv_norm_g, w_uq, w_ukv, lambda_re, lambda_im, log_step, b_re, b_im, c_re, c_im, d_skip, w_glu, b_glu, attn_out_g, ssm_out_g, w_o, ln1_g, ln1_b, w_router, router_bias, w_gate, w_up, w_down, ws_gate, ws_up, ws_down, ln2_g, ln2_b):
    B, S, D = x.shape
    T = B * S
    assert DEPTH == 1 and w_in.shape[0] == DEPTH
    assert S % QUERY_TILE == 0 and S % TOKEN_TILE == 0 and S % S5_TIME_TILE == 0, S
    assert T % (MOE_SLABS * TOKEN_TILE) == 0 and (T // MOE_SLABS * TOP_K) % EXPERT_ROW_TILE == 0, T
    l = 0
    ssm_width = w_glu.shape[-1]
    n_groups = ssm_width // SSM_GROUP
    n_state = n_groups * SSM_STATE
    mla_width = MLA_HEADS * V_DIM
    qk_pad = MLA_HEADS * HEAD_PAD
    cparams = functools.partial(pltpu.CompilerParams, vmem_limit_bytes=VMEM_LIMIT)

    s1, s2, s3 = Q_RANK, Q_RANK + KV_RANK, Q_RANK + KV_RANK + QK_ROPE
    wi = w_in[l]
    w_kr = wi[:, s2:s3]
    pad_rope = lambda w: jnp.pad(w, ((0, 0), (QK_NOPE, HEAD_PAD - QK_NOPE - QK_ROPE)))
    w1 = jnp.concatenate([wi[:, :s2], wi[:, s3:], pad_rope(w_kr), pad_rope(_half_rotate(w_kr))],
                         axis=1).astype(BF16)
    wq = w_uq[l]
    zeros_nope = jnp.zeros(wq.shape[:2] + (QK_NOPE,), wq.dtype)
    wq_main = _pad_heads(wq, QK_NOPE + QK_ROPE).astype(BF16)
    wq_rot = _pad_heads(jnp.concatenate([zeros_nope, _half_rotate(wq[..., QK_NOPE:])], axis=-1),
                        QK_NOPE + QK_ROPE).astype(BF16)
    wkv = w_ukv[l]
    wk = _pad_heads(wkv[..., :QK_NOPE], QK_NOPE).astype(BF16)
    wv = _pad_heads(wkv[..., QK_NOPE:], V_DIM).astype(BF16)
    half = QK_ROPE // 2
    inv_freq = ROPE_THETA ** (-jnp.arange(half, dtype=F32) / half)
    freq = jnp.concatenate([inv_freq, inv_freq]).reshape(QK_ROPE, 1)
    pos_f = positions.astype(F32).reshape(1, T)

    tm = min(TOKEN_TILE, T)
    w1_cols = w1.shape[1]
    tok = lambda width: pl.BlockSpec((tm, width), lambda i: (i, 0))
    xn, q, k, v, u = pl.pallas_call(
        functools.partial(_inproj_kernel, ssm_width=ssm_width),
        grid=(T // tm,),
        in_specs=[tok(D), pl.BlockSpec((1, tm), lambda i: (0, i)), _const_spec((1, D)), _const_spec((1, D)),
                  _const_spec((D, w1_cols)), _const_spec((1, Q_RANK)), _const_spec((1, KV_RANK)),
                  _const_spec((Q_RANK, qk_pad)), _const_spec((Q_RANK, qk_pad)),
                  _const_spec((KV_RANK, qk_pad)), _const_spec((KV_RANK, qk_pad)),
                  _const_spec((QK_ROPE, 1))],
        out_specs=[tok(D), tok(qk_pad), tok(qk_pad), tok(qk_pad), tok(ssm_width)],
        out_shape=[jax.ShapeDtypeStruct((T, D), F32), jax.ShapeDtypeStruct((T, qk_pad), BF16),
                   jax.ShapeDtypeStruct((T, qk_pad), BF16), jax.ShapeDtypeStruct((T, qk_pad), BF16),
                   jax.ShapeDtypeStruct((T, ssm_width), F32)],
        compiler_params=cparams(dimension_semantics=("parallel",)),
        name="inproj",
    )(x.reshape(T, D), pos_f, _row(ln_in_g), _row(ln_in_b), w1, _row(q_norm_g[l]), _row(kv_norm_g[l]),
      wq_main, wq_rot, wk, wv, freq)

    tq = min(QUERY_TILE, S)
    nq = S // tq
    att = pl.pallas_call(
        functools.partial(_attn_kernel, tq=tq, heads=MLA_HEADS, n_qtiles=nq),
        grid=(B, nq),
        in_specs=[pl.BlockSpec((tq, qk_pad), lambda b, i: (b * nq + i, 0)),
                  pl.BlockSpec((S, qk_pad), lambda b, i: (b, 0)),
                  pl.BlockSpec((S, qk_pad), lambda b, i: (b, 0))],
        out_specs=pl.BlockSpec((tq, mla_width), lambda b, i: (b * nq + i, 0)),
        out_shape=jax.ShapeDtypeStruct((T, mla_width), BF16),
        compiler_params=cparams(dimension_semantics=("parallel", "arbitrary")),
        name="attention",
    )(q, k, v)

    lam = lax.complex(jnp.minimum(lambda_re[l].astype(F32), -1e-4), lambda_im[l].astype(F32))
    step = jnp.exp(log_step[l].astype(F32))[:, None]
    lam_bar = jnp.exp(lam * step)
    b_bar = ((lam_bar - 1.0) / lam)[..., None] * lax.complex(b_re[l].astype(F32), b_im[l].astype(F32))
    n_slab = ssm_width // LANES
    g_per_slab = n_groups // n_slab
    slab = n_state // n_slab
    eye = jnp.eye(g_per_slab, dtype=F32)

    def expand_in(bpart):
        bt = bpart.transpose(0, 2, 1).reshape(n_slab, g_per_slab, SSM_GROUP, 1, SSM_STATE)
        return (bt * eye[None, :, None, :, None]).reshape(n_slab, LANES, slab)

    def expand_out(cpart):
        ct = cpart.transpose(0, 2, 1).reshape(n_slab, g_per_slab, SSM_STATE, 1, SSM_GROUP)
        return (ct * eye[None, :, None, :, None]).reshape(n_slab, slab, LANES)

    win = jnp.concatenate([expand_in(jnp.real(b_bar)), expand_in(jnp.imag(b_bar))],
                          axis=2).astype(BF16)
    cre = expand_out(c_re[l].astype(F32)).astype(BF16)
    cim = expand_out(-c_im[l].astype(F32)).astype(BF16)
    a_re = jnp.real(lam_bar).reshape(1, n_state)
    a_im = jnp.imag(lam_bar).reshape(1, n_state)

    lt = min(S5_TIME_TILE, S)
    ssm = pl.pallas_call(
        functools.partial(_s5_kernel, batch=B, lt=lt, n_state=n_state),
        grid=(S // lt,),
        in_specs=[pl.BlockSpec((B, lt, ssm_width), lambda t: (0, t, 0)),
                  _const_spec(win.shape), _const_spec((1, n_state)), _const_spec((1, n_state)),
                  _const_spec(cre.shape), _const_spec(cim.shape), _const_spec((1, ssm_width)),
                  _const_spec((ssm_width, ssm_width)), _const_spec((1, ssm_width))],
        out_specs=pl.BlockSpec((B, lt, ssm_width), lambda t: (0, t, 0)),
        out_shape=jax.ShapeDtypeStruct((B, S, ssm_width), BF16),
        scratch_shapes=[pltpu.VMEM((2 * n_state // LANES, B * lt, LANES), F32),
                        pltpu.VMEM((n_state // LANES, B, LANES), F32),
                        pltpu.VMEM((n_state // LANES, B, LANES), F32),
                        pltpu.VMEM((ssm_width // LANES, B * lt, LANES), F32),
                        pltpu.VMEM((B * lt, ssm_width), F32)],
        compiler_params=cparams(dimension_semantics=("arbitrary",)),
        name="s5",
    )(u.reshape(B, S, ssm_width), win, a_re, a_im, cre, cim, _row(d_skip[l]),
      w_glu[l].astype(BF16), _row(b_glu[l]))

    wo = w_o[l].astype(BF16)
    assert mla_width == ssm_width
    wr_t = w_router[l].T.astype(F32)
    wr_hi = wr_t.astype(BF16)
    wr_lo = (wr_t - wr_hi.astype(F32)).astype(BF16)
    half = D // 2
    n_slabs = MOE_SLABS
    ts = T // n_slabs
    nt = ts // tm
    ssm2 = ssm.reshape(T, ssm_width)
    rbias = router_bias[l].astype(F32).reshape(N_EXPERTS, 1)
    kt = lambda dt: jax.ShapeDtypeStruct((TOP_K, ts), dt)
    tmx = min(MIX_TILE, ts)
    ntx = ts // tmx
    k_spec = pl.BlockSpec((TOP_K, tmx), lambda i: (0, i))

    def route(s):
        tok_s = lambda width: pl.BlockSpec((tmx, width), lambda i: (i + s * ntx, 0))
        tok = lambda width: pl.BlockSpec((tmx, width), lambda i: (i, 0))
        return pl.pallas_call(
            _mix_kernel,
            grid=(ntx,),
            in_specs=[tok_s(mla_width), tok_s(ssm_width), tok_s(D), _const_spec((1, mla_width)),
                      _const_spec((1, ssm_width)), pl.BlockSpec((mla_width, D), lambda i: (0, 0)),
                      pl.BlockSpec((ssm_width, D), lambda i: (1, 0)), _const_spec((1, D)), _const_spec((1, D)),
                      _const_spec((N_EXPERTS, D)), _const_spec((N_EXPERTS, D)),
                      _const_spec((N_EXPERTS, 1))],
            out_specs=[tok(D), tok(half), k_spec, tok(TOP_K), k_spec, _const_spec((N_EXPERTS, LANES))],
            out_shape=[jax.ShapeDtypeStruct((ts, D), F32), jax.ShapeDtypeStruct((ts, half), jnp.int32),
                       kt(jnp.int32), jax.ShapeDtypeStruct((ts, TOP_K), F32), kt(jnp.int32),
                       jax.ShapeDtypeStruct((N_EXPERTS, LANES), F32)],
            scratch_shapes=[pltpu.VMEM((N_EXPERTS, 1), F32), pltpu.VMEM((tmx, tmx), BF16)],
            compiler_params=cparams(dimension_semantics=("arbitrary",)),
            name="mix_router",
        )(att, ssm2, xn, _row(attn_out_g[l]), _row(ssm_out_g[l]), wo, wo,
          _row(ln1_g[l]), _row(ln1_b[l]), wr_hi, wr_lo, rbias)

    tr = EXPERT_ROW_TILE
    n_tiles = (ts * TOP_K) // tr + N_EXPERTS
    n_rows = n_tiles * tr
    tp = min(POSITIONS_TILE, ts)

    def dispatch(x1p, idx_k, rank_k, counts):
        cnt = counts[:, 0].astype(jnp.int32)
        tiles_e = (cnt + tr - 1) // tr
        tile_end = jnp.cumsum(tiles_e)
        tile_start = tile_end - tiles_e
        tile_ids = jnp.arange(n_tiles, dtype=jnp.int32)
        tile_expert = jnp.sum((tile_end[None, :] <= tile_ids[:, None]).astype(jnp.int32), axis=1)
        tile_expert = jnp.minimum(tile_expert, N_EXPERTS - 1)
        owner = (tile_start[None, :] <= tile_ids[:, None]) & (tile_ids[:, None] < tile_end[None, :])
        left = jnp.sum(jnp.where(owner, cnt[None, :] - (tile_ids[:, None] - tile_start[None, :]) * tr, 0),
                       axis=1)
        tile_valid = jnp.clip(left, 0, tr).astype(jnp.int32)
        pos = pl.pallas_call(
            _positions_kernel,
            grid_spec=pltpu.PrefetchScalarGridSpec(
                num_scalar_prefetch=1, grid=(ts // tp,),
                in_specs=[pl.BlockSpec((TOP_K, tp), lambda i, off: (0, i)),
                          pl.BlockSpec((TOP_K, tp), lambda i, off: (0, i))],
                out_specs=pl.BlockSpec((TOP_K, tp), lambda i, off: (0, i))),
            out_shape=kt(jnp.int32),
            name="positions",
        )((tile_start * tr).astype(jnp.int32), idx_k, rank_k)
        return _sc_scatter_rows(x1p, pos, n_rows), pos, tile_expert, tile_valid

    wg, wu, wd = w_gate[l], w_up[l], w_down[l]
    ff = wg.shape[-1]

    def experts(xs, tile_expert, tile_valid):
        prev = jnp.concatenate([tile_expert[:1] - 1, tile_expert[:-1]])
        run_id = (jnp.cumsum((tile_expert != prev).astype(jnp.int32)) - 1).astype(jnp.int32)
        e_ids = jnp.arange(N_EXPERTS, dtype=jnp.int32)
        present = jnp.any(tile_expert[:, None] == e_ids[None, :], axis=0)
        later = jnp.where((e_ids[None, :] > tile_expert[:, None]) & present[None, :], e_ids[None, :], N_EXPERTS)
        nxt = jnp.min(later, axis=1)
        next_expert = jnp.where(nxt < N_EXPERTS, nxt, -1).astype(jnp.int32)
        any_spec = pl.BlockSpec(memory_space=pl.ANY)
        return pl.pallas_call(
            _experts_kernel,
            grid_spec=pltpu.PrefetchScalarGridSpec(
                num_scalar_prefetch=4, grid=(n_tiles,),
                in_specs=[any_spec, any_spec, any_spec, any_spec],
                out_specs=pl.BlockSpec((tr, half), lambda i, te, tv, rn, nx: (i, 0)),
                scratch_shapes=[pltpu.VMEM((D, ff), BF16), pltpu.VMEM((D, ff), BF16),
                                pltpu.VMEM((ff, D), BF16),
                                pltpu.VMEM((XS_RING, tr, half), jnp.int32),
                                pltpu.SemaphoreType.DMA((XS_RING,)),
                                pltpu.VMEM((2, D, ff), F32), pltpu.VMEM((2, D, ff), F32),
                                pltpu.VMEM((2, ff, D), F32),
                                pltpu.SemaphoreType.DMA((2, 3))]),
            out_shape=jax.ShapeDtypeStruct((n_rows, half), jnp.int32),
            compiler_params=cparams(dimension_semantics=("arbitrary",)),
            name="experts",
        )(tile_expert, tile_valid, run_id, next_expert, xs, wg, wu, wd)

    shared = (ws_gate[l].astype(BF16), ws_up[l].astype(BF16), ws_down[l].astype(BF16))

    n_chunks = COMBINE_CHUNKS
    tc = ts // n_chunks
    ntc = nt // n_chunks

    def combine(s, c, out_so_far, yg, gate_k, x1):
        in_slab = lambda width: pl.BlockSpec((tm, width), lambda i: (i + c * ntc, 0))
        specs = [pl.BlockSpec((TOP_K, tm, half), lambda i: (0, i, 0)), in_slab(TOP_K), in_slab(D),
                 _const_spec((D, ff)), _const_spec((D, ff)), _const_spec((ff, D)),
                 _const_spec((1, D)), _const_spec((1, D))]
        args = (yg.reshape(TOP_K, tc, half), gate_k, x1, *shared, _row(ln2_g[l]), _row(ln2_b[l]))
        body, aliases = _combine_kernel, {}
        if out_so_far is not None:
            specs = [pl.BlockSpec(memory_space=pl.ANY)] + specs
            args = (out_so_far,) + args
            body, aliases = _combine_into_kernel, {0: 0}
        return pl.pallas_call(
            body,
            grid=(ntc,),
            in_specs=specs,
            out_specs=pl.BlockSpec((tm, D), lambda i: (i + s * nt + c * ntc, 0)),
            out_shape=jax.ShapeDtypeStruct((T, D), F32),
            input_output_aliases=aliases,
            compiler_params=cparams(dimension_semantics=("parallel",)),
            name="combine",
        )(*args)

    routed, moved = [], []
    for s in range(n_slabs):
        x1, x1p, idx_k, gate_k, rank_k, counts = route(s)
        routed.append((x1, gate_k))
        moved.append(dispatch(x1p, idx_k, rank_k, counts))
    gathered = []
    for xs, pos, tile_expert, tile_valid in moved:
        ys = experts(xs, tile_expert, tile_valid)
        gathered.append([_sc_gather_rows(ys, pos[:, c * tc:(c + 1) * tc].reshape(TOP_K * tc))
                         for c in range(n_chunks)])
    out = None
    for s in range(n_slabs):
        x1, gate_k = routed[s]
        for c in range(n_chunks):
            out = combine(s, c, out, gathered[s][c], gate_k, x1)
    return out.reshape(B, S, D)
```

```python
import functools

import jax
import jax.numpy as jnp
from jax import lax
from jax.experimental import pallas as pl
from jax.experimental.pallas import tpu as pltpu
from jax.experimental.pallas import tpu_sc as plsc

CHUNK = 64
MLA_HEADS = 8
QK_NOPE = 64
QK_ROPE = 32
V_DIM = 64
Q_RANK = 256
KV_RANK = 128
ROPE_THETA = 10000.0
SSM_GROUP = 16
SSM_STATE = 64
N_EXPERTS = 64
TOP_K = 8
N_GROUPS = 8
TOP_GROUPS = 4
ROUTED_SCALE = 2.5
DEPTH = 1
ALPHA = (2.0 * DEPTH) ** 0.25
EPS = 1e-5
LOG2_E = 1.4426950408889634

LANES = 128
HEAD_PAD = LANES
VMEM_LIMIT = 56 * 1024 * 1024
TOKEN_TILE = 512
MIX_TILE = 1024
QUERY_TILE = 256
S5_TIME_TILE = 64
POSITIONS_TILE = 2048
EXPERT_ROW_TILE = 1024
MOE_SLABS = 1
COMBINE_CHUNKS = 4

BF16 = jnp.bfloat16
F32 = jnp.float32
NT_DIMS = (((1,), (1,)), ((), ()))


def _dot(a, b):
    return jnp.dot(a, b, preferred_element_type=F32)


def _layer_norm(x, g, b):
    mu = jnp.mean(x, axis=-1, keepdims=True)
    xc = x - mu
    var = jnp.mean(xc * xc, axis=-1, keepdims=True)
    return xc * lax.rsqrt(var + EPS) * g + b


def _rms_norm(x, g):
    return x * lax.rsqrt(jnp.mean(x * x, axis=-1, keepdims=True) + EPS) * g


def _inproj_kernel(x_ref, pos_ref, lng_ref, lnb_ref, w1_ref, qg_ref, kvg_ref,
                   wq_ref, wqr_ref, wk_ref, wv_ref, freq_ref,
                   xn_ref, q_ref, k_ref, v_ref, u_ref, *, ssm_width):
    xn = _layer_norm(x_ref[...], lng_ref[...], lnb_ref[...])
    xn_ref[...] = xn
    h = _dot(xn.astype(BF16), w1_ref[...])
    o1 = Q_RANK
    o2 = o1 + KV_RANK
    o3 = o2 + ssm_width
    o4 = o3 + HEAD_PAD
    cq = h[:, :o1]
    ckv = h[:, o1:o2]
    u_ref[...] = h[:, o2:o3]
    kr_raw = h[:, o3:o4]
    kr_rot = h[:, o4:o4 + HEAD_PAD]
    cqn = _rms_norm(cq, qg_ref[...]).astype(BF16)
    ckvn = _rms_norm(ckv, kvg_ref[...]).astype(BF16)

    tm = x_ref.shape[0]
    ang_t = freq_ref[...] * pos_ref[...]

    def to_token_rows(t):
        padded = jnp.concatenate([jnp.zeros((QK_NOPE, tm), F32), t,
                                  jnp.zeros((HEAD_PAD - QK_NOPE - QK_ROPE, tm), F32)], axis=0)
        return padded.T

    c = to_token_rows(jnp.cos(ang_t))
    s = to_token_rows(jnp.sin(ang_t))
    lane = lax.broadcasted_iota(jnp.int32, (1, HEAD_PAD), 1)
    scale = (QK_NOPE + QK_ROPE) ** -0.5 * LOG2_E
    cos1 = (c + jnp.where(lane < QK_NOPE, 1.0, 0.0)) * scale
    sin1 = s * scale
    cos_t = jnp.concatenate([cos1] * MLA_HEADS, axis=1)
    sin_t = jnp.concatenate([sin1] * MLA_HEADS, axis=1)
    q = _dot(cqn, wq_ref[...]) * cos_t + _dot(cqn, wqr_ref[...]) * sin_t
    q_ref[...] = q.astype(BF16)

    kr = kr_raw * c + kr_rot * s
    k = _dot(ckvn, wk_ref[...]) + jnp.concatenate([kr] * MLA_HEADS, axis=1)
    k_ref[...] = k.astype(BF16)
    ones_col = jnp.concatenate([jnp.where(lane == V_DIM, 1.0, 0.0)] * MLA_HEADS, axis=1)
    v_ref[...] = (_dot(ckvn, wv_ref[...]) + ones_col).astype(BF16)


def _attn_kernel(q_ref, k_ref, v_ref, o_ref, *, tq, heads, n_qtiles):
    qi = pl.program_id(1)
    row_chunk = lax.broadcasted_iota(jnp.int32, (tq, tq), 0) // CHUNK
    col_chunk = lax.broadcasted_iota(jnp.int32, (tq, tq), 1) // CHUNK
    diag_mask = row_chunk >= col_chunk

    def tile(n_blocks):
        keys = n_blocks * tq
        cols = [slice(h * HEAD_PAD, (h + 1) * HEAD_PAD) for h in range(heads)]

        def mask_diag(s):
            s_diag = jnp.where(diag_mask, s[:, keys - tq:], -jnp.inf)
            return s_diag if n_blocks == 1 else jnp.concatenate([s[:, :keys - tq], s_diag], axis=1)

        ss = [lax.dot_general(q_ref[:, c], k_ref[:keys, c], NT_DIMS, preferred_element_type=F32)
              for c in cols]
        ss = [mask_diag(s) for s in ss]
        ms = [jnp.max(s, axis=-1, keepdims=True) for s in ss]
        ps = [jnp.exp2(s - m).astype(BF16) for s, m in zip(ss, ms)]
        accs = [_dot(p, v_ref[:keys, c]) for p, c in zip(ps, cols)]
        outs = [acc[:, :V_DIM] / acc[:, V_DIM:V_DIM + 1] for acc in accs]
        o_ref[...] = jnp.concatenate(outs, axis=1).astype(o_ref.dtype)

    for c in range(n_qtiles):
        pl.when(qi == c)(functools.partial(tile, c + 1))


def _s5_kernel(u_ref, win_ref, are_ref, aim_ref, cre_ref, cim_ref, dskip_ref,
               wglu_ref, bglu_ref, o_ref, vx_ref, hre_ref, him_ref, io_ref, utm_ref, *,
               batch, lt, n_state):
    ti = pl.program_id(0)

    @pl.when(ti == 0)
    def _():
        hre_ref[...] = jnp.zeros_like(hre_ref)
        him_ref[...] = jnp.zeros_like(him_ref)

    width = u_ref.shape[-1]
    n_slab = width // LANES
    n_tiles = n_state // LANES
    slab_tiles = n_tiles // n_slab
    for b in range(batch):
        for c in range(n_slab):
            io_ref[c, b * lt:(b + 1) * lt, :] = u_ref[b, :, c * LANES:(c + 1) * LANES]

    def slab_tiles_of(j):
        return range(j * slab_tiles, (j + 1) * slab_tiles)

    def expand(j):
        for t in range(lt):
            utm_ref[t * batch:(t + 1) * batch, j * LANES:(j + 1) * LANES] = (
                io_ref[j, pl.ds(t, batch, stride=lt), :])
        ub = utm_ref[:, j * LANES:(j + 1) * LANES].astype(BF16)
        vj = _dot(ub, win_ref[j])
        for i, c in enumerate(slab_tiles_of(j)):
            vx_ref[c] = vj[:, i * LANES:(i + 1) * LANES]
            vx_ref[n_tiles + c] = vj[:, (slab_tiles + i) * LANES:(slab_tiles + i + 1) * LANES]

    def scan(j):
        tiles = slab_tiles_of(j)
        ar = [jnp.broadcast_to(are_ref[:, c * LANES:(c + 1) * LANES], (batch, LANES)) for c in tiles]
        ai = [jnp.broadcast_to(aim_ref[:, c * LANES:(c + 1) * LANES], (batch, LANES)) for c in tiles]
        hr = [hre_ref[c] for c in tiles]
        hi = [him_ref[c] for c in tiles]
        for t in range(lt):
            rows = slice(t * batch, (t + 1) * batch)
            for n, c in enumerate(tiles):
                nr = ar[n] * hr[n] - ai[n] * hi[n] + vx_ref[c, rows, :]
                ni = ar[n] * hi[n] + ai[n] * hr[n] + vx_ref[n_tiles + c, rows, :]
                vx_ref[c, rows, :] = nr
                vx_ref[n_tiles + c, rows, :] = ni
                hr[n], hi[n] = nr, ni
        for n, c in enumerate(tiles):
            hre_ref[c] = hr[n]
            him_ref[c] = hi[n]

    def project(j):
        tiles = slab_tiles_of(j)
        xr = jnp.concatenate([vx_ref[c].astype(BF16) for c in tiles], axis=1)
        xi = jnp.concatenate([vx_ref[n_tiles + c].astype(BF16) for c in tiles], axis=1)
        return _dot(xr, cre_ref[j]) + _dot(xi, cim_ref[j])

    ys = [None] * n_slab
    for j in range(n_slab + 2):
        if j < n_slab:
            expand(j)
        if 1 <= j <= n_slab:
            scan(j - 1)
        if j >= 2:
            ys[j - 2] = project(j - 2)
    y = jnp.concatenate(ys, axis=1) + dskip_ref[...] * utm_ref[...]
    y = jax.nn.gelu(y)
    z = _dot(y.astype(BF16), wglu_ref[...]) + bglu_ref[...]
    out = y * jax.nn.sigmoid(z)
    for c in range(n_slab):
        io_ref[c] = out[:, c * LANES:(c + 1) * LANES]
    for b in range(batch):
        for c in range(n_slab):
            o_ref[b, :, c * LANES:(c + 1) * LANES] = io_ref[c, pl.ds(b, lt, stride=batch), :].astype(o_ref.dtype)


def _router_gates(logits_t, rbias):
    n_exp, tm = logits_t.shape
    per_group = n_exp // N_GROUPS
    scores = jax.nn.sigmoid(logits_t)
    sel = scores + rbias
    neg_inf = -jnp.inf
    sub_iota = lax.broadcasted_iota(jnp.int32, (per_group, tm), 0).astype(F32)
    group_score = []
    for g in range(N_GROUPS):
        sg = sel[g * per_group:(g + 1) * per_group, :]
        m1 = jnp.max(sg, axis=0, keepdims=True)
        first = jnp.min(jnp.where(sg == m1, sub_iota, float(per_group)), axis=0, keepdims=True)
        m2 = jnp.max(jnp.where(sub_iota == first, neg_inf, sg), axis=0, keepdims=True)
        group_score.append(m1 + m2)
    masked = []
    for g in range(N_GROUPS):
        rank = jnp.zeros((1, tm), F32)
        for g2 in range(N_GROUPS):
            if g2 == g:
                continue
            ahead = (group_score[g2] >= group_score[g]) if g2 < g else (group_score[g2] > group_score[g])
            rank = rank + jnp.where(ahead, 1.0, 0.0)
        keep = rank < float(TOP_GROUPS)
        masked.append(jnp.where(keep, sel[g * per_group:(g + 1) * per_group, :], neg_inf))
    cur = jnp.concatenate(masked, axis=0)
    iota = lax.broadcasted_iota(jnp.int32, (n_exp, tm), 0).astype(F32)
    chosen = jnp.zeros((n_exp, tm), F32)
    picks, weights = [], []
    for _ in range(TOP_K):
        m = jnp.max(cur, axis=0, keepdims=True)
        idx = jnp.min(jnp.where(cur == m, iota, float(n_exp)), axis=0, keepdims=True)
        pick = iota == idx
        chosen = jnp.where(pick, 1.0, chosen)
        cur = jnp.where(pick, neg_inf, cur)
        picks.append(idx)
        weights.append(jnp.sum(jnp.where(pick, scores, 0.0), axis=0, keepdims=True))
    idx_k = jnp.concatenate(picks, axis=0)
    w_k = jnp.concatenate(weights, axis=0)
    gate_k = w_k / jnp.sum(w_k, axis=0, keepdims=True) * ROUTED_SCALE
    return idx_k, gate_k, chosen


def _pack_bf16_pairs(x):
    n = x.shape[1] // 2
    hi = lax.bitcast_convert_type(x[:, :n].astype(BF16).astype(F32), jnp.int32)
    lo = lax.bitcast_convert_type(x[:, n:].astype(BF16).astype(F32), jnp.int32)
    return hi | lax.shift_right_logical(lo, 16)


def _unpack_bf16_pairs(p):
    hi = lax.bitcast_convert_type(p & jnp.int32(-65536), F32).astype(BF16)
    lo = lax.bitcast_convert_type(lax.shift_left(p, 16), F32).astype(BF16)
    return hi, lo


def _mix_kernel(att_ref, ssm_ref, xn_ref, ag_ref, sg_ref, woa_ref, wos_ref,
                g_ref, b_ref, wrh_ref, wrl_ref, rb_ref,
                x1_ref, x1p_ref, idx_ref, gate_ref, rank_ref, cnt_ref, carry_ref, before_ref):
    @pl.when(pl.program_id(0) == 0)
    def _():
        carry_ref[...] = jnp.zeros_like(carry_ref)
        tm = before_ref.shape[0]
        before = (lax.broadcasted_iota(jnp.int32, (tm, tm), 0)
                  < lax.broadcasted_iota(jnp.int32, (tm, tm), 1))
        before_ref[...] = jnp.where(before, 1.0, 0.0).astype(BF16)

    an = _rms_norm(att_ref[...].astype(F32), ag_ref[...]).astype(BF16)
    sn = _rms_norm(ssm_ref[...].astype(F32), sg_ref[...]).astype(BF16)
    mix = _dot(an, woa_ref[...]) + _dot(sn, wos_ref[...])
    x1 = _layer_norm(ALPHA * xn_ref[...] + mix, g_ref[...], b_ref[...])
    x1_ref[...] = x1
    x1p_ref[...] = _pack_bf16_pairs(x1)
    x_hi = x1.astype(BF16)
    x_lo = (x1 - x_hi.astype(F32)).astype(BF16)
    dg = functools.partial(lax.dot_general, dimension_numbers=NT_DIMS, preferred_element_type=F32)
    logits_t = dg(wrh_ref[...], x_hi) + dg(wrl_ref[...], x_hi) + dg(wrh_ref[...], x_lo)
    idx_k, gate_k, chosen = _router_gates(logits_t, rb_ref[...])
    idx_ref[...] = idx_k.astype(jnp.int32)
    gate_ref[...] = gate_k.T

    n_exp, tm = chosen.shape
    excl = _dot(chosen.astype(BF16), before_ref[...])
    rank_full = carry_ref[...] + excl
    iota = lax.broadcasted_iota(jnp.int32, (n_exp, tm), 0).astype(F32)
    ranks = [jnp.sum(jnp.where(iota == idx_k[k:k + 1, :], rank_full, 0.0), axis=0, keepdims=True)
             for k in range(TOP_K)]
    rank_ref[...] = jnp.concatenate(ranks, axis=0).astype(jnp.int32)
    total = carry_ref[...] + jnp.sum(chosen, axis=1, keepdims=True)
    carry_ref[...] = total
    cnt_ref[...] = jnp.broadcast_to(total, cnt_ref.shape)


def _positions_kernel(off_ref, idx_ref, rank_ref, pos_ref):
    idx = idx_ref[...]
    base = jnp.zeros(idx.shape, jnp.int32)
    for e in range(N_EXPERTS):
        base = jnp.where(idx == e, off_ref[e], base)
    pos_ref[...] = rank_ref[...] + base


def _swiglu(x_hi, x_lo, wg_ref, wu_ref, wd_ref):
    half = x_hi.shape[1]
    hg = _dot(x_hi, wg_ref[:half, :]) + _dot(x_lo, wg_ref[half:, :])
    hu = _dot(x_hi, wu_ref[:half, :]) + _dot(x_lo, wu_ref[half:, :])
    h = jax.nn.silu(hg) * hu
    return _dot(h.astype(BF16), wd_ref[...])


XS_RING = 3


def _experts_kernel(te_ref, valid_ref, run_ref, next_ref, xs_hbm, wg_hbm, wu_hbm, wd_hbm, ys_ref,
                    wgb_ref, wub_ref, wdb_ref, xbuf_ref, xsem, wgf_ref, wuf_ref, wdf_ref, wsem):
    i = pl.program_id(0)
    n_steps = pl.num_programs(0)
    valid = valid_ref[i]
    tr = xbuf_ref.shape[1]

    def fetch(j):
        slot = j % XS_RING
        return pltpu.make_async_copy(xs_hbm.at[pl.ds(pl.multiple_of(j * tr, tr), tr)],
                                     xbuf_ref.at[slot], xsem.at[slot])

    @pl.when(i == 0)
    def _():
        for j in range(XS_RING - 1):
            fetch(j).start()

    @pl.when(i + XS_RING - 1 < n_steps)
    def _():
        fetch(i + XS_RING - 1).start()

    fetch(i).wait()
    xs_ref = xbuf_ref.at[i % XS_RING]

    def weight_copies(e, slot):
        return [pltpu.make_async_copy(src.at[e], dst.at[slot], wsem.at[slot, n])
                for n, (src, dst) in enumerate(((wg_hbm, wgf_ref), (wu_hbm, wuf_ref), (wd_hbm, wdf_ref)))]

    @pl.when(i == 0)
    def _():
        for cp in weight_copies(te_ref[0], 0):
            cp.start()

    @pl.when((i == 0) | (te_ref[i] != te_ref[jnp.maximum(i - 1, 0)]))
    def _():
        slot = run_ref[i] % 2
        for cp in weight_copies(te_ref[i], slot):
            cp.wait()
        wgb_ref[...] = wgf_ref[slot].astype(BF16)
        wub_ref[...] = wuf_ref[slot].astype(BF16)
        wdb_ref[...] = wdf_ref[slot].astype(BF16)

        @pl.when(next_ref[i] >= 0)
        def _():
            for cp in weight_copies(next_ref[i], 1 - slot):
                cp.start()

    @pl.when(valid > 0)
    def _():
        rows = lax.broadcasted_iota(jnp.int32, xs_ref.shape, 0)
        x_hi, x_lo = _unpack_bf16_pairs(jnp.where(rows < valid, xs_ref[...], 0))
        ys_ref[...] = _pack_bf16_pairs(_swiglu(x_hi, x_lo, wgb_ref, wub_ref, wdb_ref))

    @pl.when(valid == 0)
    def _():
        ys_ref[...] = jnp.zeros_like(ys_ref)


def _combine_kernel(yg_hbm, gate_ref, x1_ref, wsg_ref, wsu_ref, wsd_ref, g_ref, b_ref, o_ref,
                    ybuf_ref, ysem):
    i = pl.program_id(0)
    n_steps = pl.num_programs(0)
    tm = ybuf_ref.shape[2]

    def fetch(j):
        slot = j % XS_RING
        return pltpu.make_async_copy(yg_hbm.at[:, pl.ds(pl.multiple_of(j * tm, tm), tm), :],
                                     ybuf_ref.at[slot], ysem.at[slot])

    @pl.when(i == 0)
    def _():
        for j in range(XS_RING - 1):
            @pl.when(j < n_steps)
            def _():
                fetch(j).start()

    @pl.when(i + XS_RING - 1 < n_steps)
    def _():
        fetch(i + XS_RING - 1).start()

    fetch(i).wait()
    yg_ref = ybuf_ref.at[i % XS_RING]
    x1 = x1_ref[...]
    half = x1.shape[1] // 2
    acc = _swiglu(x1[:, :half].astype(BF16), x1[:, half:].astype(BF16), wsg_ref, wsu_ref, wsd_ref)
    gates = gate_ref[...]
    acc_hi, acc_lo = acc[:, :half], acc[:, half:]
    for k in range(TOP_K):
        y_hi, y_lo = _unpack_bf16_pairs(yg_ref[k])
        gk = gates[:, k:k + 1]
        acc_hi = acc_hi + gk * y_hi.astype(F32)
        acc_lo = acc_lo + gk * y_lo.astype(F32)
    ffn = jnp.concatenate([acc_hi, acc_lo], axis=1)
    o_ref[...] = _layer_norm(ALPHA * x1 + ffn, g_ref[...], b_ref[...])


def _combine_into_kernel(prev_ref, *refs):
    del prev_ref
    _combine_kernel(*refs)


SC_CORES = 2
SC_SUBCORES = 16
SC_WORKERS = SC_CORES * SC_SUBCORES
SC_CHUNK = 64


def _sc_mesh():
    return plsc.VectorSubcoreMesh(core_axis_name="c", subcore_axis_name="s")


def _sc_gather_rows(table, idx):
    n = idx.shape[0]
    d = table.shape[1]
    assert n % (SC_WORKERS * 2 * SC_CHUNK) == 0, n
    per_w = n // SC_WORKERS
    n_ch = per_w // SC_CHUNK

    @functools.partial(
        pl.kernel, mesh=_sc_mesh(),
        out_type=jax.ShapeDtypeStruct((n, d), table.dtype),
        scratch_types=[pltpu.VMEM((n_ch, SC_CHUNK), jnp.int32),
                       pltpu.VMEM((2, SC_CHUNK, d), table.dtype),
                       pltpu.SemaphoreType.DMA((2,)),
                       pltpu.SemaphoreType.DMA((2,))],
    )
    def k(table_hbm, idx_hbm, out_hbm, idx_v, buf, gsem, osem):
        wid = lax.axis_index("s") * SC_CORES + lax.axis_index("c")
        base = wid * per_w
        pltpu.sync_copy(idx_hbm.at[wid], idx_v)

        def gather(c, b):
            return pltpu.make_async_copy(table_hbm.at[idx_v.at[c]], buf.at[b], gsem.at[b])

        def put(c, b):
            return pltpu.make_async_copy(buf.at[b], out_hbm.at[pl.ds(base + c * SC_CHUNK, SC_CHUNK)],
                                         osem.at[b])

        gather(0, 0).start()

        @pl.loop(0, n_ch, step=2)
        def _(c):
            for b in range(2):
                cc = c + b
                gather(cc, b).wait()

                @pl.when(cc + 1 < n_ch)
                def _():
                    @pl.when(cc >= 1)
                    def _():
                        put(cc - 1, 1 - b).wait()
                    gather(cc + 1, 1 - b).start()

                put(cc, b).start()

        put(n_ch - 2, 0).wait()
        put(n_ch - 1, 1).wait()

    return k(table, idx.reshape(SC_WORKERS, n_ch, SC_CHUNK))


def _sc_scatter_rows(x, pos, n_out):
    t, d = x.shape
    kk = pos.shape[0]
    assert t % (SC_WORKERS * 2 * SC_CHUNK) == 0, t
    per_w = t // SC_WORKERS
    n_ch = per_w // SC_CHUNK
    pos_w = pos.reshape(kk, SC_WORKERS, n_ch, SC_CHUNK).transpose(1, 2, 0, 3)
    pos_w = pos_w.reshape(SC_WORKERS, n_ch * kk, SC_CHUNK)

    @functools.partial(
        pl.kernel, mesh=_sc_mesh(),
        out_type=jax.ShapeDtypeStruct((n_out, d), x.dtype),
        scratch_types=[pltpu.VMEM((n_ch * kk, SC_CHUNK), jnp.int32),
                       pltpu.VMEM((2, SC_CHUNK, d), x.dtype),
                       pltpu.SemaphoreType.DMA((2,)),
                       pltpu.SemaphoreType.DMA((2,))],
    )
    def k(x_hbm, pos_hbm, out_hbm, idx_v, buf, isem, osem):
        wid = lax.axis_index("s") * SC_CORES + lax.axis_index("c")
        base = wid * per_w
        pltpu.sync_copy(pos_hbm.at[wid], idx_v)

        def get(c, b):
            return pltpu.make_async_copy(x_hbm.at[pl.ds(base + c * SC_CHUNK, SC_CHUNK)], buf.at[b],
                                         isem.at[b])

        def put(c, j, b):
            return pltpu.make_async_copy(buf.at[b], out_hbm.at[idx_v.at[c * kk + j]], osem.at[b])

        get(0, 0).start()

        @pl.loop(0, n_ch, step=2)
        def _(c):
            for b in range(2):
                cc = c + b
                get(cc, b).wait()

                @pl.when(cc + 1 < n_ch)
                def _():
                    @pl.when(cc >= 1)
                    def _():
                        for j in range(kk):
                            put(cc - 1, j, 1 - b).wait()
                    get(cc + 1, 1 - b).start()

                for j in range(kk):
                    put(cc, j, b).start()

        for j in range(kk):
            put(n_ch - 2, j, 0).wait()
        for j in range(kk):
            put(n_ch - 1, j, 1).wait()

    return k(x, pos_w)


def _row(v):
    return v.reshape(1, -1).astype(F32)


def _const_spec(shape):
    nd = len(shape)
    return pl.BlockSpec(shape, lambda *_: (0,) * nd)


def _pad_heads(w, width):
    r, h, _ = w.shape
    return jnp.pad(w, ((0, 0), (0, 0), (0, HEAD_PAD - width))).reshape(r, h * HEAD_PAD)


def _half_rotate(w):
    half = QK_ROPE // 2
    return jnp.concatenate([-w[..., half:], w[..., :half]], axis=-1)


def kernel(x, positions, ln_in_g, ln_in_b, w_in, q_norm_g, kv_norm_g, w_uq, w_ukv, lambda_re, lambda_im, log_step, b_re, b_im, c_re, c_im, d_skip, w_glu, b_glu, attn_out_g, ssm_out_g, w_o, ln1_g, ln1_b, w_router, router_bias, w_gate, w_up, w_down, ws_gate, ws_up, ws_down, ln2_g, ln2_b):
    B, S, D = x.shape
    T = B * S
    assert DEPTH == 1 and w_in.shape[0] == DEPTH
    assert S % QUERY_TILE == 0 and S % TOKEN_TILE == 0 and S % S5_TIME_TILE == 0, S
    assert T % (MOE_SLABS * TOKEN_TILE) == 0 and (T // MOE_SLABS * TOP_K) % EXPERT_ROW_TILE == 0, T
    l = 0
    ssm_width = w_glu.shape[-1]
    n_groups = ssm_width // SSM_GROUP
    n_state = n_groups * SSM_STATE
    mla_width = MLA_HEADS * V_DIM
    qk_pad = MLA_HEADS * HEAD_PAD
    cparams = functools.partial(pltpu.CompilerParams, vmem_limit_bytes=VMEM_LIMIT)

    s1, s2, s3 = Q_RANK, Q_RANK + KV_RANK, Q_RANK + KV_RANK + QK_ROPE
    wi = w_in[l]
    w_kr = wi[:, s2:s3]
    pad_rope = lambda w: jnp.pad(w, ((0, 0), (QK_NOPE, HEAD_PAD - QK_NOPE - QK_ROPE)))
    w1 = jnp.concatenate([wi[:, :s2], wi[:, s3:], pad_rope(w_kr), pad_rope(_half_rotate(w_kr))],
                         axis=1).astype(BF16)
    wq = w_uq[l]
    zeros_nope = jnp.zeros(wq.shape[:2] + (QK_NOPE,), wq.dtype)
    wq_main = _pad_heads(wq, QK_NOPE + QK_ROPE).astype(BF16)
    wq_rot = _pad_heads(jnp.concatenate([zeros_nope, _half_rotate(wq[..., QK_NOPE:])], axis=-1),
                        QK_NOPE + QK_ROPE).astype(BF16)
    wkv = w_ukv[l]
    wk = _pad_heads(wkv[..., :QK_NOPE], QK_NOPE).astype(BF16)
    wv = _pad_heads(wkv[..., QK_NOPE:], V_DIM).astype(BF16)
    half = QK_ROPE // 2
    inv_freq = ROPE_THETA ** (-jnp.arange(half, dtype=F32) / half)
    freq = jnp.concatenate([inv_freq, inv_freq]).reshape(QK_ROPE, 1)
    pos_f = positions.astype(F32).reshape(1, T)

    tm = min(TOKEN_TILE, T)
    w1_cols = w1.shape[1]
    tok = lambda width: pl.BlockSpec((tm, width), lambda i: (i, 0))
    xn, q, k, v, u = pl.pallas_call(
        functools.partial(_inproj_kernel, ssm_width=ssm_width),
        grid=(T // tm,),
        in_specs=[tok(D), pl.BlockSpec((1, tm), lambda i: (0, i)), _const_spec((1, D)), _const_spec((1, D)),
                  _const_spec((D, w1_cols)), _const_spec((1, Q_RANK)), _const_spec((1, KV_RANK)),
                  _const_spec((Q_RANK, qk_pad)), _const_spec((Q_RANK, qk_pad)),
                  _const_spec((KV_RANK, qk_pad)), _const_spec((KV_RANK, qk_pad)),
                  _const_spec((QK_ROPE, 1))],
        out_specs=[tok(D), tok(qk_pad), tok(qk_pad), tok(qk_pad), tok(ssm_width)],
        out_shape=[jax.ShapeDtypeStruct((T, D), F32), jax.ShapeDtypeStruct((T, qk_pad), BF16),
                   jax.ShapeDtypeStruct((T, qk_pad), BF16), jax.ShapeDtypeStruct((T, qk_pad), BF16),
                   jax.ShapeDtypeStruct((T, ssm_width), F32)],
        compiler_params=cparams(dimension_semantics=("parallel",)),
        name="inproj",
    )(x.reshape(T, D), pos_f, _row(ln_in_g), _row(ln_in_b), w1, _row(q_norm_g[l]), _row(kv_norm_g[l]),
      wq_main, wq_rot, wk, wv, freq)

    tq = min(QUERY_TILE, S)
    nq = S // tq
    att = pl.pallas_call(
        functools.partial(_attn_kernel, tq=tq, heads=MLA_HEADS, n_qtiles=nq),
        grid=(B, nq),
        in_specs=[pl.BlockSpec((tq, qk_pad), lambda b, i: (b * nq + i, 0)),
                  pl.BlockSpec((S, qk_pad), lambda b, i: (b, 0)),
                  pl.BlockSpec((S, qk_pad), lambda b, i: (b, 0))],
        out_specs=pl.BlockSpec((tq, mla_width), lambda b, i: (b * nq + i, 0)),
        out_shape=jax.ShapeDtypeStruct((T, mla_width), BF16),
        compiler_params=cparams(dimension_semantics=("parallel", "arbitrary")),
        name="attention",
    )(q, k, v)

    lam = lax.complex(jnp.minimum(lambda_re[l].astype(F32), -1e-4), lambda_im[l].astype(F32))
    step = jnp.exp(log_step[l].astype(F32))[:, None]
    lam_bar = jnp.exp(lam * step)
    b_bar = ((lam_bar - 1.0) / lam)[..., None] * lax.complex(b_re[l].astype(F32), b_im[l].astype(F32))
    n_slab = ssm_width // LANES
    g_per_slab = n_groups // n_slab
    slab = n_state // n_slab
    eye = jnp.eye(g_per_slab, dtype=F32)

    def expand_in(bpart):
        bt = bpart.transpose(0, 2, 1).reshape(n_slab, g_per_slab, SSM_GROUP, 1, SSM_STATE)
        return (bt * eye[None, :, None, :, None]).reshape(n_slab, LANES, slab)

    def expand_out(cpart):
        ct = cpart.transpose(0, 2, 1).reshape(n_slab, g_per_slab, SSM_STATE, 1, SSM_GROUP)
        return (ct * eye[None, :, None, :, None]).reshape(n_slab, slab, LANES)

    win = jnp.concatenate([expand_in(jnp.real(b_bar)), expand_in(jnp.imag(b_bar))],
                          axis=2).astype(BF16)
    cre = expand_out(c_re[l].astype(F32)).astype(BF16)
    cim = expand_out(-c_im[l].astype(F32)).astype(BF16)
    a_re = jnp.real(lam_bar).reshape(1, n_state)
    a_im = jnp.imag(lam_bar).reshape(1, n_state)

    lt = min(S5_TIME_TILE, S)
    ssm = pl.pallas_call(
        functools.partial(_s5_kernel, batch=B, lt=lt, n_state=n_state),
        grid=(S // lt,),
        in_specs=[pl.BlockSpec((B, lt, ssm_width), lambda t: (0, t, 0)),
                  _const_spec(win.shape), _const_spec((1, n_state)), _const_spec((1, n_state)),
                  _const_spec(cre.shape), _const_spec(cim.shape), _const_spec((1, ssm_width)),
                  _const_spec((ssm_width, ssm_width)), _const_spec((1, ssm_width))],
        out_specs=pl.BlockSpec((B, lt, ssm_width), lambda t: (0, t, 0)),
        out_shape=jax.ShapeDtypeStruct((B, S, ssm_width), BF16),
        scratch_shapes=[pltpu.VMEM((2 * n_state // LANES, B * lt, LANES), F32),
                        pltpu.VMEM((n_state // LANES, B, LANES), F32),
                        pltpu.VMEM((n_state // LANES, B, LANES), F32),
                        pltpu.VMEM((ssm_width // LANES, B * lt, LANES), F32),
                        pltpu.VMEM((B * lt, ssm_width), F32)],
        compiler_params=cparams(dimension_semantics=("arbitrary",)),
        name="s5",
    )(u.reshape(B, S, ssm_width), win, a_re, a_im, cre, cim, _row(d_skip[l]),
      w_glu[l].astype(BF16), _row(b_glu[l]))

    wo = w_o[l].astype(BF16)
    assert mla_width == ssm_width
    wr_t = w_router[l].T.astype(F32)
    wr_hi = wr_t.astype(BF16)
    wr_lo = (wr_t - wr_hi.astype(F32)).astype(BF16)
    half = D // 2
    n_slabs = MOE_SLABS
    ts = T // n_slabs
    nt = ts // tm
    ssm2 = ssm.reshape(T, ssm_width)
    rbias = router_bias[l].astype(F32).reshape(N_EXPERTS, 1)
    kt = lambda dt: jax.ShapeDtypeStruct((TOP_K, ts), dt)
    tmx = min(MIX_TILE, ts)
    ntx = ts // tmx
    k_spec = pl.BlockSpec((TOP_K, tmx), lambda i: (0, i))

    def route(s):
        tok_s = lambda width: pl.BlockSpec((tmx, width), lambda i: (i + s * ntx, 0))
        tok = lambda width: pl.BlockSpec((tmx, width), lambda i: (i, 0))
        return pl.pallas_call(
            _mix_kernel,
            grid=(ntx,),
            in_specs=[tok_s(mla_width), tok_s(ssm_width), tok_s(D), _const_spec((1, mla_width)),
                      _const_spec((1, ssm_width)), pl.BlockSpec((mla_width, D), lambda i: (0, 0)),
                      pl.BlockSpec((ssm_width, D), lambda i: (1, 0)), _const_spec((1, D)), _const_spec((1, D)),
                      _const_spec((N_EXPERTS, D)), _const_spec((N_EXPERTS, D)),
                      _const_spec((N_EXPERTS, 1))],
            out_specs=[tok(D), tok(half), k_spec, tok(TOP_K), k_spec, _const_spec((N_EXPERTS, LANES))],
            out_shape=[jax.ShapeDtypeStruct((ts, D), F32), jax.ShapeDtypeStruct((ts, half), jnp.int32),
                       kt(jnp.int32), jax.ShapeDtypeStruct((ts, TOP_K), F32), kt(jnp.int32),
                       jax.ShapeDtypeStruct((N_EXPERTS, LANES), F32)],
            scratch_shapes=[pltpu.VMEM((N_EXPERTS, 1), F32), pltpu.VMEM((tmx, tmx), BF16)],
            compiler_params=cparams(dimension_semantics=("arbitrary",)),
            name="mix_router",
        )(att, ssm2, xn, _row(attn_out_g[l]), _row(ssm_out_g[l]), wo, wo,
          _row(ln1_g[l]), _row(ln1_b[l]), wr_hi, wr_lo, rbias)

    tr = EXPERT_ROW_TILE
    n_tiles = (ts * TOP_K) // tr + N_EXPERTS
    n_rows = n_tiles * tr
    tp = min(POSITIONS_TILE, ts)

    def dispatch(x1p, idx_k, rank_k, counts):
        cnt = counts[:, 0].astype(jnp.int32)
        tiles_e = (cnt + tr - 1) // tr
        tile_end = jnp.cumsum(tiles_e)
        tile_start = tile_end - tiles_e
        tile_ids = jnp.arange(n_tiles, dtype=jnp.int32)
        tile_expert = jnp.sum((tile_end[None, :] <= tile_ids[:, None]).astype(jnp.int32), axis=1)
        tile_expert = jnp.minimum(tile_expert, N_EXPERTS - 1)
        owner = (tile_start[None, :] <= tile_ids[:, None]) & (tile_ids[:, None] < tile_end[None, :])
        left = jnp.sum(jnp.where(owner, cnt[None, :] - (tile_ids[:, None] - tile_start[None, :]) * tr, 0),
                       axis=1)
        tile_valid = jnp.clip(left, 0, tr).astype(jnp.int32)
        pos = pl.pallas_call(
            _positions_kernel,
            grid_spec=pltpu.PrefetchScalarGridSpec(
                num_scalar_prefetch=1, grid=(ts // tp,),
                in_specs=[pl.BlockSpec((TOP_K, tp), lambda i, off: (0, i)),
                          pl.BlockSpec((TOP_K, tp), lambda i, off: (0, i))],
                out_specs=pl.BlockSpec((TOP_K, tp), lambda i, off: (0, i))),
            out_shape=kt(jnp.int32),
            name="positions",
        )((tile_start * tr).astype(jnp.int32), idx_k, rank_k)
        return _sc_scatter_rows(x1p, pos, n_rows), pos, tile_expert, tile_valid

    wg, wu, wd = w_gate[l], w_up[l], w_down[l]
    ff = wg.shape[-1]

    def experts(xs, tile_expert, tile_valid):
        prev = jnp.concatenate([tile_expert[:1] - 1, tile_expert[:-1]])
        run_id = (jnp.cumsum((tile_expert != prev).astype(jnp.int32)) - 1).astype(jnp.int32)
        e_ids = jnp.arange(N_EXPERTS, dtype=jnp.int32)
        present = jnp.any(tile_expert[:, None] == e_ids[None, :], axis=0)
        later = jnp.where((e_ids[None, :] > tile_expert[:, None]) & present[None, :], e_ids[None, :], N_EXPERTS)
        nxt = jnp.min(later, axis=1)
        next_expert = jnp.where(nxt < N_EXPERTS, nxt, -1).astype(jnp.int32)
        any_spec = pl.BlockSpec(memory_space=pl.ANY)
        return pl.pallas_call(
            _experts_kernel,
            grid_spec=pltpu.PrefetchScalarGridSpec(
                num_scalar_prefetch=4, grid=(n_tiles,),
                in_specs=[any_spec, any_spec, any_spec, any_spec],
                out_specs=pl.BlockSpec((tr, half), lambda i, te, tv, rn, nx: (i, 0)),
                scratch_shapes=[pltpu.VMEM((D, ff), BF16), pltpu.VMEM((D, ff), BF16),
                                pltpu.VMEM((ff, D), BF16),
                                pltpu.VMEM((XS_RING, tr, half), jnp.int32),
                                pltpu.SemaphoreType.DMA((XS_RING,)),
                                pltpu.VMEM((2, D, ff), F32), pltpu.VMEM((2, D, ff), F32),
                                pltpu.VMEM((2, ff, D), F32),
                                pltpu.SemaphoreType.DMA((2, 3))]),
            out_shape=jax.ShapeDtypeStruct((n_rows, half), jnp.int32),
            compiler_params=cparams(dimension_semantics=("arbitrary",)),
            name="experts",
        )(tile_expert, tile_valid, run_id, next_expert, xs, wg, wu, wd)

    shared = (ws_gate[l].astype(BF16), ws_up[l].astype(BF16), ws_down[l].astype(BF16))

    n_chunks = COMBINE_CHUNKS
    tc = ts // n_chunks
    ntc = nt // n_chunks

    def combine(s, c, out_so_far, yg, gate_k, x1):
        in_slab = lambda width: pl.BlockSpec((tm, width), lambda i: (i + c * ntc, 0))
        specs = [pl.BlockSpec(memory_space=pl.ANY), in_slab(TOP_K), in_slab(D),
                 _const_spec((D, ff)), _const_spec((D, ff)), _const_spec((ff, D)),
                 _const_spec((1, D)), _const_spec((1, D))]
        args = (yg.reshape(TOP_K, tc, half), gate_k, x1, *shared, _row(ln2_g[l]), _row(ln2_b[l]))
        body, aliases = _combine_kernel, {}
        if out_so_far is not None:
            specs = [pl.BlockSpec(memory_space=pl.ANY)] + specs
            args = (out_so_far,) + args
            body, aliases = _combine_into_kernel, {0: 0}
        return pl.pallas_call(
            body,
            grid=(ntc,),
            in_specs=specs,
            out_specs=pl.BlockSpec((tm, D), lambda i: (i + s * nt + c * ntc, 0)),
            out_shape=jax.ShapeDtypeStruct((T, D), F32),
            input_output_aliases=aliases,
            scratch_shapes=[pltpu.VMEM((XS_RING, TOP_K, tm, half), jnp.int32),
                            pltpu.SemaphoreType.DMA((XS_RING,))],
            compiler_params=cparams(dimension_semantics=("arbitrary",)),
            name="combine",
        )(*args)

    routed, moved = [], []
    for s in range(n_slabs):
        x1, x1p, idx_k, gate_k, rank_k, counts = route(s)
        routed.append((x1, gate_k))
        moved.append(dispatch(x1p, idx_k, rank_k, counts))
    gathered = []
    for xs, pos, tile_expert, tile_valid in moved:
        ys = experts(xs, tile_expert, tile_valid)
        gathered.append([_sc_gather_rows(ys, pos[:, c * tc:(c + 1) * tc].reshape(TOP_K * tc))
                         for c in range(n_chunks)])
    out = None
    for s in range(n_slabs):
        x1, gate_k = routed[s]
        for c in range(n_chunks):
            out = combine(s, c, out, gathered[s][c], gate_k, x1)
    return out.reshape(B, S, D)
```
